```python
import math
import jax, jax.numpy as jnp
from jax import lax
import numpy as np

D_MODEL = 2048
BATCH = 4
SEQ = 2048
DEPTH = 2

N_MEM = 256
GRID_W = 64
MIX_W = D_MODEL
NA_HEAD_DIM = 64
NA_W = 3 * MIX_W // 8
NA_HEADS = NA_W // NA_HEAD_DIM
NA_KR = 8
NA_KC = 16
HY_W = MIX_W // 4
HY_ORDER = 2
HY_BANDS = 8
HY_POS_DIM = 1 + 2 * HY_BANDS
HY_FILT_FF = 64
HY_FAST_DECAY_PCT = 0.3
HY_SLOW_DECAY_PCT = 1.5
HY_DECAY_TARGET = 1e-2
RET_HEAD_DIM = 128
RET_W = MIX_W - NA_W - HY_W
RET_HEADS = RET_W // RET_HEAD_DIM
RET_CHUNK = 128
P_IN = 3 * NA_W + 3 * HY_W + 4 * RET_W
CROSS_HEADS = 4
CROSS_HEAD_DIM = D_MODEL // CROSS_HEADS
N_EXPERTS = 16
EXPERT_FF = 2048
EC_CAPACITY = 2
RMS_EPS = 1e-6
GN_EPS = 1e-5

kernel_name = "hymba_style_hybrid_encoder"

F32 = jnp.float32


def rms_norm(x, g):
    xf = x.astype(F32)
    y = xf * lax.rsqrt(jnp.mean(xf * xf, axis=-1, keepdims=True) + RMS_EPS)
    return (y * g.astype(F32)).astype(x.dtype)


def group_rms_norm(parts, gain):
    normed = [p.astype(F32) * lax.rsqrt(jnp.mean(jnp.square(p.astype(F32)), axis=-1, keepdims=True) + RMS_EPS) for p in parts]
    y = jnp.concatenate(normed, axis=-1) * gain.astype(F32)
    return y.astype(parts[0].dtype)


def neighbourhood_attention(q, k, v, rpb):
    B, L, H, dh = q.shape
    rows = L // GRID_W
    kr = min(NA_KR, rows)
    q = q.reshape(B, rows, GRID_W, H, dh)
    k = k.reshape(B, rows, GRID_W, H, dh)
    v = v.reshape(B, rows, GRID_W, H, dh)
    r = jnp.arange(rows)
    row_start = jnp.clip(r - kr // 2, 0, rows - kr)
    row_idx = row_start[:, None] + jnp.arange(kr)[None, :]
    k_band = k[:, row_idx]
    v_band = v[:, row_idx]
    s = jnp.einsum('brchd,brkwhd->bhrckw', q, k_band).astype(F32) * (dh ** -0.5)
    c = jnp.arange(GRID_W)
    col_start = jnp.clip(c - NA_KC // 2, 0, GRID_W - NA_KC)
    col_in = (c[None, :] >= col_start[:, None]) & (c[None, :] < col_start[:, None] + NA_KC)
    dr = row_idx - r[:, None] + (NA_KR - 1)
    dc = jnp.clip(c[None, :] - c[:, None] + (NA_KC - 1), 0, 2 * NA_KC - 2)
    bias = rpb[:, dr[:, None, :, None], dc[None, :, None, :]].astype(F32)
    s = jnp.where(col_in[None, None, None, :, None, :], s + bias[None], -jnp.inf)
    p = jax.nn.softmax(s, axis=(-2, -1)).astype(v.dtype)
    o = jnp.einsum('bhrckw,brkwhd->brchd', p, v_band)
    return o.reshape(B, L, H * dh)


def hyena_filters(L, w1, b1, w2, b2, w3, freq):
    t = jnp.arange(L, dtype=F32)
    t01 = t / (L - 1)
    bands = jnp.linspace(1e-4, HY_BANDS - 1, HY_BANDS, dtype=F32)
    ang = (2.0 * math.pi) * (t[:, None] / L) * bands[None, :]
    feats = jnp.concatenate([t01[:, None], jnp.cos(ang), -jnp.sin(ang)], axis=-1)
    f = freq.astype(F32)
    h = jnp.sin(f * (feats @ w1.astype(F32) + b1.astype(F32)))
    h = jnp.sin(f * (h @ w2.astype(F32) + b2.astype(F32)))
    h = (h @ w3.astype(F32)).reshape(L, HY_ORDER, 2, HY_W)
    min_decay = math.log(HY_DECAY_TARGET) / HY_SLOW_DECAY_PCT
    max_decay = math.log(HY_DECAY_TARGET) / HY_FAST_DECAY_PCT
    deltas = jnp.abs(jnp.linspace(min_decay, max_decay, HY_W, dtype=F32))
    window = jnp.exp(-t01[:, None] * deltas[None, :])
    return h * window[:, None, None, :]


def hyena_mixer(u, conv_w, conv_b, w1, b1, w2, b2, w3, freq, skip_d):
    B, L, _ = u.shape
    p = jnp.pad(u, ((0, 0), (1, 1), (0, 0)))
    s = p[:, :-2] * conv_w[0] + p[:, 1:-1] * conv_w[1] + p[:, 2:] * conv_w[2] + conv_b
    x1, x2, v = jnp.split(s, 3, axis=-1)
    filt = hyena_filters(L, w1, b1, w2, b2, w3, freq)
    kbuf = jnp.concatenate([filt[:, :, 0], jnp.zeros((1, HY_ORDER, HY_W), F32), filt[1:, :, 1][::-1]], axis=0)
    kfreq = jnp.fft.rfft(kbuf, axis=0)
    n = 2 * L
    z = v.astype(F32)
    for o, gate in enumerate((x1, x2)):
        conv = jnp.fft.irfft(jnp.fft.rfft(z, n=n, axis=1) * kfreq[None, :, o], n=n, axis=1)[:, :L]
        z = gate.astype(F32) * (conv + skip_d[o].astype(F32) * z)
    return z.astype(u.dtype)


def rotary(x, pos):
    half = x.shape[-1] // 2
    inv = 1.0 / (10000.0 ** jnp.linspace(0.0, 1.0, half, dtype=F32))
    ang = pos[:, None] * inv[None, :]
    cos = jnp.cos(ang)[None, :, None, :]
    sin = jnp.sin(ang)[None, :, None, :]
    x1, x2 = x[..., :half], x[..., half:]
    return jnp.concatenate([x1 * cos - x2 * sin, x1 * sin + x2 * cos], axis=-1)


def retention_chunkwise(q, k, v, log_gamma, include_diag):
    B, H, L, d = q.shape
    c = RET_CHUNK
    n = L // c
    qc = q.reshape(B, H, n, c, d)
    kc = k.reshape(B, H, n, c, d)
    vc = v.reshape(B, H, n, c, d)
    i = jnp.arange(c, dtype=F32)
    diff = i[:, None] - i[None, :]
    mask = (diff >= 0) if include_diag else (diff > 0)
    dec = jnp.where(mask[None], jnp.exp(log_gamma[:, None, None] * jnp.where(mask, diff, 0.0)[None]), 0.0)
    a = jnp.einsum('bhnid,bhnjd->bhnij', qc, kc) * dec[None, :, None]
    y = jnp.einsum('bhnij,bhnje->bhnie', a, vc)
    zeta = jnp.exp(log_gamma[:, None] * (c - 1 - i)[None, :])
    kv = jnp.einsum('bhnjd,hj,bhnje->nbhde', kc, zeta, vc)
    chunk_decay = jnp.exp(log_gamma * c)[None, :, None, None]

    def step(state, kv_n):
        return chunk_decay * state + kv_n, state

    _, prev = lax.scan(step, jnp.zeros((B, H, d, d), F32), kv)
    xi = jnp.exp(log_gamma[:, None] * (i + 1.0)[None, :])
    y = y + jnp.einsum('bhnid,hi,nbhde->bhnie', qc, xi, prev)
    return y.reshape(B, H, L, d)


def retention_mixer(r_in):
    B, L, _ = r_in.shape
    q, k, v, g = jnp.split(r_in, 4, axis=-1)
    shp = (B, L, RET_HEADS, RET_HEAD_DIM)
    pos = jnp.arange(L, dtype=F32)
    q = rotary(q.reshape(shp).astype(F32), pos) * (RET_HEAD_DIM ** -0.5)
    k = rotary(k.reshape(shp).astype(F32), pos)
    v = v.reshape(shp).astype(F32)
    q, k, v = (jnp.transpose(t, (0, 2, 1, 3)) for t in (q, k, v))
    hidx = jnp.arange(RET_HEADS, dtype=F32)
    lg_fwd = jnp.log1p(-jnp.exp2(-5.0 - hidx))
    lg_bwd = jnp.log1p(-jnp.exp2(-5.5 - hidx))
    y_f = retention_chunkwise(q, k, v, lg_fwd, True)
    y_b = jnp.flip(retention_chunkwise(jnp.flip(q, 2), jnp.flip(k, 2), jnp.flip(v, 2), lg_bwd, False), 2)
    y = jnp.transpose(y_f + y_b, (0, 2, 1, 3))
    mu = jnp.mean(y, axis=-1, keepdims=True)
    var = jnp.mean(jnp.square(y - mu), axis=-1, keepdims=True)
    y = ((y - mu) * lax.rsqrt(var + GN_EPS)).reshape(B, L, RET_W)
    return (y * jax.nn.silu(g.astype(F32))).astype(r_in.dtype)


def cross_attend(h, mem_n, w_cq, w_ckv, w_co):
    B, L, _ = h.shape
    M = mem_n.shape[1]
    q = (h @ w_cq).reshape(B, L, CROSS_HEADS, CROSS_HEAD_DIM)
    kv = (mem_n @ w_ckv).reshape(B, M, 2, CROSS_HEADS, CROSS_HEAD_DIM)
    k, v = kv[:, :, 0], kv[:, :, 1]
    s = jnp.einsum('blhd,bmhd->bhlm', q, k).astype(F32) * (CROSS_HEAD_DIM ** -0.5)
    p = jax.nn.softmax(s, axis=-1).astype(v.dtype)
    o = jnp.einsum('bhlm,bmhd->blhd', p, v).reshape(B, L, D_MODEL)
    return o @ w_co


def expert_choice_ffn(h, w_router, w_gate, w_up, w_down):
    B, T, D = h.shape
    cap = EC_CAPACITY * T // N_EXPERTS
    aff = jax.nn.softmax(jnp.einsum('btd,de->bte', h, w_router).astype(F32), axis=-1)
    g, idx = lax.top_k(jnp.transpose(aff, (0, 2, 1)), cap)
    xe = jax.vmap(lambda hb, ib: hb[ib])(h, idx)
    a = jnp.einsum('becd,edf->becf', xe, w_gate)
    u = jnp.einsum('becd,edf->becf', xe, w_up)
    y = jnp.einsum('becf,efd->becd', jax.nn.silu(a) * u, w_down)
    y = y * g[..., None].astype(y.dtype)
    return jax.vmap(lambda yb, ib: jnp.zeros((T, D), yb.dtype).at[ib.reshape(-1)].add(yb.reshape(-1, D)))(y, idx)


def setup_inputs(seed: int = 0) -> dict:
    key = jax.random.key(seed)
    ks = jax.random.split(key, 32)

    def nrm(k, shape, scale):
        return jax.random.normal(k, shape, F32) * scale

    def gain(k, shape):
        return 1.0 + 0.05 * jax.random.normal(k, shape, F32)

    L_ = DEPTH
    return {
        "x": nrm(ks[0], (BATCH, SEQ, D_MODEL), 1.0),
        "mem": nrm(ks[1], (BATCH, N_MEM, D_MODEL), 1.0),
        "norm_mix": gain(ks[2], (L_, D_MODEL)),
        "w_in": nrm(ks[3], (L_, D_MODEL, P_IN), D_MODEL ** -0.5),
        "na_rpb": nrm(ks[4], (L_, NA_HEADS, 2 * NA_KR - 1, 2 * NA_KC - 1), 0.02),
        "hy_conv_w": nrm(ks[5], (L_, 3, 3 * HY_W), 3 ** -0.5),
        "hy_conv_b": nrm(ks[6], (L_, 3 * HY_W), 0.02),
        "hy_filt_w1": nrm(ks[7], (L_, HY_POS_DIM, HY_FILT_FF), HY_POS_DIM ** -0.5),
        "hy_filt_b1": nrm(ks[8], (L_, HY_FILT_FF), 0.02),
        "hy_filt_w2": nrm(ks[9], (L_, HY_FILT_FF, HY_FILT_FF), HY_FILT_FF ** -0.5),
        "hy_filt_b2": nrm(ks[10], (L_, HY_FILT_FF), 0.02),
        "hy_filt_w3": nrm(ks[11], (L_, HY_FILT_FF, HY_ORDER * 2 * HY_W), HY_FILT_FF ** -0.5),
        "hy_sin_freq": gain(ks[12], (L_, HY_FILT_FF)),
        "hy_skip_d": nrm(ks[13], (L_, HY_ORDER, HY_W), 0.1),
        "branch_norm": gain(ks[14], (L_, MIX_W)),
        "w_out": nrm(ks[15], (L_, MIX_W, D_MODEL), MIX_W ** -0.5),
        "norm_cross": gain(ks[16], (L_, D_MODEL)),
        "mem_norm": gain(ks[17], (D_MODEL,)),
        "w_cq": nrm(ks[18], (L_, D_MODEL, D_MODEL), D_MODEL ** -0.5),
        "w_ckv": nrm(ks[19], (L_, D_MODEL, 2 * D_MODEL), D_MODEL ** -0.5),
        "w_co": nrm(ks[20], (L_, D_MODEL, D_MODEL), D_MODEL ** -0.5),
        "norm_moe": gain(ks[21], (L_, D_MODEL)),
        "w_router": nrm(ks[22], (L_, D_MODEL, N_EXPERTS), D_MODEL ** -0.5),
        "w_gate": nrm(ks[23], (L_, N_EXPERTS, D_MODEL, EXPERT_FF), D_MODEL ** -0.5),
        "w_up": nrm(ks[24], (L_, N_EXPERTS, D_MODEL, EXPERT_FF), D_MODEL ** -0.5),
        "w_down": nrm(ks[25], (L_, N_EXPERTS, EXPERT_FF, D_MODEL), EXPERT_FF ** -0.5),
        "final_norm": gain(ks[26], (D_MODEL,)),
    }


def reference(x, mem, norm_mix, w_in, na_rpb, hy_conv_w, hy_conv_b, hy_filt_w1, hy_filt_b1,
              hy_filt_w2, hy_filt_b2, hy_filt_w3, hy_sin_freq, hy_skip_d, branch_norm, w_out,
              norm_cross, mem_norm, w_cq, w_ckv, w_co, norm_moe, w_router, w_gate, w_up, w_down,
              final_norm):
    B, L, _ = x.shape
    mem_n = rms_norm(mem, mem_norm)
    for l in range(DEPTH):
        h = rms_norm(x, norm_mix[l])
        proj = h @ w_in[l]
        na_in, hy_in, ret_in = jnp.split(proj, [3 * NA_W, 3 * NA_W + 3 * HY_W], axis=-1)
        na_q, na_k, na_v = (t.reshape(B, L, NA_HEADS, NA_HEAD_DIM) for t in jnp.split(na_in, 3, axis=-1))
        y_na = neighbourhood_attention(na_q, na_k, na_v, na_rpb[l])
        y_hy = hyena_mixer(hy_in, hy_conv_w[l], hy_conv_b[l], hy_filt_w1[l], hy_filt_b1[l],
                           hy_filt_w2[l], hy_filt_b2[l], hy_filt_w3[l], hy_sin_freq[l], hy_skip_d[l])
        y_ret = retention_mixer(ret_in)
        y = group_rms_norm([y_na, y_hy, y_ret], branch_norm[l])
        x = x + y @ w_out[l]
        x = x + cross_attend(rms_norm(x, norm_cross[l]), mem_n, w_cq[l], w_ckv[l], w_co[l])
        x = x + expert_choice_ffn(rms_norm(x, norm_moe[l]), w_router[l], w_gate[l], w_up[l], w_down[l])
    return rms_norm(x, final_norm)
```

```python
import functools
import math

import numpy as np
import jax
import jax.numpy as jnp
from jax import lax
from jax.experimental import pallas as pl
from jax.experimental.pallas import tpu as pltpu

F32 = jnp.float32
BF16 = jnp.bfloat16
I32 = jnp.int32

D_MODEL = 2048
GRID_W = 64
NA_HEAD_DIM = 64
NA_W = 768
NA_HEADS = 12
NA_KR = 8
NA_KC = 16
HY_W = 512
HY_ORDER = 2
HY_BANDS = 8
HY_POS_DIM = 17
HY_FILT_FF = 64
RET_HEAD_DIM = 128
RET_W = 768
RET_HEADS = 6
RET_CHUNK = 128
CROSS_HEADS = 4
CROSS_HEAD_DIM = 512
N_EXPERTS = 16
EXPERT_FF = 2048
EC_CAPACITY = 2
RMS_EPS = 1e-6
GN_EPS = 1e-5

MASK_VALUE = -1e30
VMEM_LIMIT_BYTES = 56 * 1024 * 1024

NT_DIMS = (((1,), (1,)), ((), ()))


def _params(*sem):
    return pltpu.CompilerParams(dimension_semantics=sem, vmem_limit_bytes=VMEM_LIMIT_BYTES)


def _rms(xf, g):
    return xf * lax.rsqrt(jnp.mean(xf * xf, axis=-1, keepdims=True) + RMS_EPS) * g


def _mm_kernel(prologue, n_pro, has_res, *refs):
    pro_refs = refs[:n_pro]
    w_ref = refs[n_pro]
    res_ref = refs[n_pro + 1] if has_res else None
    o_ref = refs[n_pro + 1 + has_res]
    h_ref = refs[n_pro + 2 + has_res]

    @pl.when(pl.program_id(1) == 0)
    def _():
        h_ref[...] = prologue(*pro_refs).astype(BF16)

    acc = jnp.dot(h_ref[...], w_ref[...], preferred_element_type=F32)
    if has_res:
        acc = acc + res_ref[...]
    o_ref[...] = acc.astype(o_ref.dtype)


def _fused_matmul(prologue, pro_args, pro_specs, wsel, res, m, tm, tn, out_dtype, name):
    w, layer, col0, n = wsel
    k = w.shape[1]
    cb0 = col0 // tn
    assert col0 % tn == 0 and n % tn == 0 and m % tm == 0
    has_res = res is not None
    in_specs = list(pro_specs) + [pl.BlockSpec((None, k, tn), lambda i, j: (layer, 0, cb0 + j))]
    args = list(pro_args) + [w]
    if has_res:
        in_specs.append(pl.BlockSpec((tm, tn), lambda i, j: (i, j)))
        args.append(res)
    return pl.pallas_call(
        functools.partial(_mm_kernel, prologue, len(pro_args), has_res),
        grid=(m // tm, n // tn),
        in_specs=in_specs,
        out_specs=pl.BlockSpec((tm, tn), lambda i, j: (i, j)),
        out_shape=jax.ShapeDtypeStruct((m, n), out_dtype),
        scratch_shapes=[pltpu.VMEM((tm, k), BF16)],
        compiler_params=_params("parallel", "arbitrary"),
        name=name,
    )(*args)


def _rms_prologue(x_ref, g_ref):
    return _rms(x_ref[...], g_ref[...])


def _whole(w, layer):
    return (w, layer, 0, w.shape[-1])


def _norm_matmul(x2d, gain, wsel, tm, tn, out_dtype, name, res=None):
    m, k = x2d.shape
    specs = [pl.BlockSpec((tm, k), lambda i, j: (i, 0)), pl.BlockSpec((1, k), lambda i, j: (0, 0))]
    return _fused_matmul(_rms_prologue, [x2d, gain.reshape(1, k)], specs, wsel, res, m, tm, tn, out_dtype, name)


def _identity_prologue(a_ref):
    return a_ref[...]


def _plain_matmul(a, wsel, tm, tn, out_dtype, name, res=None):
    m, k = a.shape
    specs = [pl.BlockSpec((tm, k), lambda i, j: (i, 0))]
    return _fused_matmul(_identity_prologue, [a], specs, wsel, res, m, tm, tn, out_dtype, name)


def _branch_prologue(na_ref, hy_ref, ret_ref, g_ref):
    def nrm(p):
        return p * lax.rsqrt(jnp.mean(p * p, axis=-1, keepdims=True) + RMS_EPS)

    y = jnp.concatenate([nrm(na_ref[...]), nrm(hy_ref[...]), nrm(ret_ref[...])], axis=-1)
    return y * g_ref[...]


def _branch_out_proj(y_na, y_hy, y_ret, gain, wsel, res, tm, tn):
    m = y_na.shape[0]
    specs = [
        pl.BlockSpec((tm, NA_W), lambda i, j: (i, 0)),
        pl.BlockSpec((tm, HY_W), lambda i, j: (i, 0)),
        pl.BlockSpec((tm, RET_W), lambda i, j: (i, 0)),
        pl.BlockSpec((1, D_MODEL), lambda i, j: (0, 0)),
    ]
    return _fused_matmul(_branch_prologue, [y_na, y_hy, y_ret, gain.reshape(1, D_MODEL)], specs, wsel, res,
                         m, tm, tn, F32, "branch_out_proj")


NA_PAIR = 2
NA_BAND = NA_KR + NA_PAIR


def _na_band_base(r, rows):
    return np.clip(r - NA_KR // 2, 0, rows - NA_BAND)


@functools.lru_cache(maxsize=None)
def _na_variants(rows):
    assert rows % NA_PAIR == 0 and rows >= NA_BAND + 2
    n_var = NA_KR // 2 + 1
    dr = np.full((n_var, NA_BAND, NA_PAIR), -2, np.int64)
    for r in range(0, rows, NA_PAIR):
        base = int(_na_band_base(r, rows))
        v = (r - base) // 2
        for j in range(NA_PAIR):
            rs = int(np.clip(r + j - NA_KR // 2, 0, rows - NA_KR))
            for i in range(NA_BAND):
                val = base + i - (r + j) + (NA_KR - 1) if rs <= base + i < rs + NA_KR else -1
                assert dr[v, i, j] in (-2, val)
                dr[v, i, j] = val
    assert (dr > -2).all()
    return dr


def _na_bias_table(rpb, rows):
    c = np.arange(GRID_W)
    col_start = np.clip(c - NA_KC // 2, 0, GRID_W - NA_KC)
    col_in = (c[None, :] >= col_start[:, None]) & (c[None, :] < col_start[:, None] + NA_KC)
    dc = np.clip(c[None, :] - c[:, None] + (NA_KC - 1), 0, 2 * NA_KC - 2)
    onehot = (dc.reshape(-1)[None, :] == np.arange(2 * NA_KC - 1)[:, None]).astype(np.float32)
    cols = jnp.einsum("hab,bn->han", rpb.astype(F32), jnp.asarray(onehot), precision=lax.Precision.HIGHEST)
    cols = cols.reshape(NA_HEADS, 2 * NA_KR - 1, GRID_W, GRID_W)
    cols = jnp.where(col_in[None, None], cols, MASK_VALUE).transpose(0, 1, 3, 2)
    masked = jnp.full((NA_HEADS, GRID_W, GRID_W), MASK_VALUE, F32)
    dr = _na_variants(rows)
    variants = []
    for v in range(dr.shape[0]):
        band = [jnp.concatenate([cols[:, dr[v, i, j]] if dr[v, i, j] >= 0 else masked for j in range(NA_PAIR)],
                                axis=-1) for i in range(NA_BAND)]
        variants.append(jnp.concatenate(band, axis=1))
    return jnp.stack(variants, axis=1)


def _na_kernel(rows, q_ref, k_ref, v_ref, bias_ref, o_ref, vt_ref):
    dh = NA_HEAD_DIM
    nq = NA_PAIR * GRID_W
    nk = NA_BAND * GRID_W
    n_chunk = nk // 128
    lane = lax.broadcasted_iota(I32, (nq, 2 * dh), 1)

    for ch in range(vt_ref.shape[0]):
        vt_ref[ch] = v_ref[0, ch * 128:(ch + 1) * 128, :].astype(F32).T.astype(BF16)

    def body(p, carry):
        r = p * NA_PAIR
        base = jnp.clip(r - NA_KR // 2, 0, rows - NA_BAND)
        variant = (r - base) // 2
        q = q_ref[0, pl.ds(pl.multiple_of(r * GRID_W, nq), nq), :] * (dh ** -0.5)
        kb = k_ref[0, pl.ds(pl.multiple_of(base * GRID_W, 128), nk), :]
        c0 = base // 2
        vt = jnp.concatenate([vt_ref[c0 + i] for i in range(n_chunk)], axis=1)
        outs = []
        for hh in range(2):
            qh = jnp.where((lane >= dh) == bool(hh), q, jnp.zeros_like(q))
            st = lax.dot_general(kb, qh, NT_DIMS, preferred_element_type=F32)
            st = st + bias_ref[hh, variant]
            m = jnp.max(st, axis=0, keepdims=True)
            pt = jnp.exp(st - m)
            l = jnp.sum(pt, axis=0, keepdims=True)
            ot = jnp.dot(vt[hh * dh:(hh + 1) * dh, :], pt.astype(BF16), preferred_element_type=F32)
            outs.append(ot / l)
        o_ref[0, pl.ds(pl.multiple_of(r * GRID_W, nq), nq), :] = jnp.concatenate(outs, axis=0).T
        return carry

    lax.fori_loop(0, rows // NA_PAIR, body, 0, unroll=2)


def _neighbourhood_attention(proj_na, bias_tbl, batch, seq):
    rows = seq // GRID_W
    n_pairs = NA_HEADS // 2
    blk = (1, seq, 2 * NA_HEAD_DIM)
    n_var, nk, nq = bias_tbl.shape[1:]
    return pl.pallas_call(
        functools.partial(_na_kernel, rows),
        grid=(batch, n_pairs),
        in_specs=[
            pl.BlockSpec(blk, lambda b, h: (b, 0, h)),
            pl.BlockSpec(blk, lambda b, h: (b, 0, n_pairs + h)),
            pl.BlockSpec(blk, lambda b, h: (b, 0, 2 * n_pairs + h)),
            pl.BlockSpec((2, n_var, nk, nq), lambda b, h: (h, 0, 0, 0)),
        ],
        out_specs=pl.BlockSpec(blk, lambda b, h: (b, 0, h)),
        out_shape=jax.ShapeDtypeStruct((batch, seq, NA_W), F32),
        scratch_shapes=[pltpu.VMEM((seq // 128, 2 * NA_HEAD_DIM, 128), BF16)],
        compiler_params=_params("parallel", "arbitrary"),
        name="neighbourhood_attention",
    )(proj_na, proj_na, proj_na, bias_tbl)


@functools.lru_cache(maxsize=None)
def _dft_factors(seq):
    n = 2 * seq
    t = np.arange(seq, dtype=np.int64)
    f1 = np.arange(seq // 64, dtype=np.int64)
    f0 = np.arange(64, dtype=np.int64)
    a = 2.0 * np.pi * ((64 * f1[:, None] * t[None, :]) % n).astype(np.float64) / n
    b = 2.0 * np.pi * ((f0[:, None] * t[None, :]) % n).astype(np.float64) / n
    ny = np.where(t % 2 == 0, 1.0, -1.0)
    return tuple(np.asarray(v, np.float32) for v in (np.cos(a), np.sin(a), np.cos(b), np.sin(b), ny))


def _dft_tables(seq):
    ca, sa, cb, sb, ny = (jnp.asarray(v) for v in _dft_factors(seq))
    cos = (ca[:, None, :] * cb[None] - sa[:, None, :] * sb[None]).reshape(seq, seq)
    sin = (sa[:, None, :] * cb[None] + ca[:, None, :] * sb[None]).reshape(seq, seq)
    sin = jnp.concatenate([ny[None, :], sin[1:]], axis=0)
    fwd = jnp.stack([cos, sin]).astype(BF16)
    inv = jnp.concatenate([cos.T, sin.T], axis=1).astype(BF16)
    return fwd, inv


@functools.lru_cache(maxsize=None)
def _hyena_consts(seq):
    t = np.arange(seq, dtype=np.float64)
    t01 = t / (seq - 1)
    bands = np.linspace(1e-4, HY_BANDS - 1, HY_BANDS)
    ang = (2.0 * math.pi) * (t[:, None] / seq) * bands[None, :]
    feats = np.concatenate([t01[:, None], np.cos(ang), -np.sin(ang)], axis=-1)
    feats_p = np.zeros((seq, 128), np.float32)
    feats_p[:, :HY_POS_DIM] = feats
    min_decay = math.log(1e-2) / 1.5
    max_decay = math.log(1e-2) / 0.3
    deltas = np.abs(np.linspace(min_decay, max_decay, HY_W))
    window = np.exp(-t01[:, None] * deltas[None, :]).astype(np.float32)
    return feats_p, window


def _filter_kernel(feats_ref, w1_ref, b1_ref, w2_ref, b2_ref, freq_ref, w3_ref, win_ref, o_ref):
    hp = lax.Precision.HIGHEST
    f = freq_ref[...]
    h = jnp.sin(f * (jnp.dot(feats_ref[...], w1_ref[...], precision=hp, preferred_element_type=F32) + b1_ref[...]))
    h = jnp.sin(f * (jnp.dot(h, w2_ref[...], precision=hp, preferred_element_type=F32) + b2_ref[...]))
    h = jnp.dot(h, w3_ref[...], precision=hp, preferred_element_type=F32) * win_ref[...]
    row = lax.broadcasted_iota(I32, h.shape, 0)
    drop = jnp.logical_and(row == 0, pl.program_id(0) >= HY_ORDER)
    o_ref[...] = jnp.where(drop, 0.0, h).astype(BF16)


def _hyena_filters_time(w1, b1, w2, b2, w3, freq, seq):
    feats, window = _hyena_consts(seq)
    w1p = jnp.zeros((128, HY_FILT_FF), F32).at[:HY_POS_DIM].set(w1)
    w3r = w3.reshape(HY_FILT_FF, HY_ORDER, 2, HY_W).transpose(0, 2, 1, 3).reshape(HY_FILT_FF, 4 * HY_W)
    full = lambda shape: pl.BlockSpec(shape, lambda c: (0,) * len(shape))
    return pl.pallas_call(
        _filter_kernel,
        grid=(4,),
        in_specs=[
            full((seq, 128)), full((128, HY_FILT_FF)), full((1, HY_FILT_FF)),
            full((HY_FILT_FF, HY_FILT_FF)), full((1, HY_FILT_FF)), full((1, HY_FILT_FF)),
            pl.BlockSpec((HY_FILT_FF, HY_W), lambda c: (0, c)),
            full((seq, HY_W)),
        ],
        out_specs=pl.BlockSpec((seq, HY_W), lambda c: (0, c)),
        out_shape=jax.ShapeDtypeStruct((seq, 4 * HY_W), BF16),
        compiler_params=_params("arbitrary"),
        name="hyena_filter_mlp",
    )(jnp.asarray(feats), w1p, b1.reshape(1, -1), w2, b2.reshape(1, -1), freq.reshape(1, -1), w3r,
      jnp.asarray(window))


def _filter_dft_kernel(f_ref, fwd_ref, bwd_ref, o_ref):
    fwd = fwd_ref[...]
    bwd = bwd_ref[...]
    uc = jnp.dot(f_ref[0], fwd, preferred_element_type=F32)
    vc = jnp.dot(f_ref[0], bwd, preferred_element_type=F32)
    us = jnp.dot(f_ref[1], fwd, preferred_element_type=F32)
    vs = jnp.dot(f_ref[1], bwd, preferred_element_type=F32)
    o_ref[0] = uc + vc
    row = lax.broadcasted_iota(I32, us.shape, 0)
    nyq = jnp.logical_and(row == 0, pl.program_id(0) == 0)
    o_ref[1] = jnp.where(nyq, us + vs, us - vs)


def _filter_dft(dft_fwd, filt_t, seq, fb):
    return pl.pallas_call(
        _filter_dft_kernel,
        grid=(seq // fb, HY_ORDER),
        in_specs=[
            pl.BlockSpec((2, fb, seq), lambda f, o: (0, f, 0)),
            pl.BlockSpec((seq, HY_W), lambda f, o: (0, o)),
            pl.BlockSpec((seq, HY_W), lambda f, o: (0, HY_ORDER + o)),
        ],
        out_specs=pl.BlockSpec((2, fb, HY_W), lambda f, o: (0, f, o)),
        out_shape=jax.ShapeDtypeStruct((2, seq, HY_ORDER * HY_W), F32),
        compiler_params=_params("parallel", "arbitrary"),
        name="hyena_filter_dft",
    )(dft_fwd, filt_t, filt_t)


def _short_conv_kernel(p_ref, w_ref, b_ref, o_ref):
    p = p_ref[0].astype(F32)
    seq = p.shape[0]
    row = lax.broadcasted_iota(I32, p.shape, 0)
    prev = jnp.where(row == 0, 0.0, pltpu.roll(p, 1, 0))
    nxt = jnp.where(row == seq - 1, 0.0, pltpu.roll(p, seq - 1, 0))
    w = w_ref[...]
    o_ref[0] = prev * w[0:1] + p * w[1:2] + nxt * w[2:3] + b_ref[...]


def _short_conv(proj, col0, conv_w, conv_b, batch, seq, tc):
    n_cols = conv_w.shape[-1]
    assert col0 % tc == 0 and n_cols % tc == 0
    cb0 = col0 // tc
    return pl.pallas_call(
        _short_conv_kernel,
        grid=(batch, n_cols // tc),
        in_specs=[
            pl.BlockSpec((1, seq, tc), lambda b, c: (b, 0, cb0 + c)),
            pl.BlockSpec((3, tc), lambda b, c: (0, c)),
            pl.BlockSpec((1, tc), lambda b, c: (0, c)),
        ],
        out_specs=pl.BlockSpec((1, seq, tc), lambda b, c: (b, 0, c)),
        out_shape=jax.ShapeDtypeStruct((batch, seq, n_cols), F32),
        compiler_params=_params("parallel", "arbitrary"),
        name="hyena_short_conv",
    )(proj, conv_w, conv_b.reshape(1, -1))


def _spectrum_kernel(n_fft, f_ref, z_ref, k_ref, o_ref):
    z = z_ref[0].astype(BF16)
    xr = jnp.dot(f_ref[0], z, preferred_element_type=F32)
    xs = jnp.dot(f_ref[1], z, preferred_element_type=F32)
    kr = k_ref[0]
    ks = k_ref[1]
    row = lax.broadcasted_iota(I32, xr.shape, 0)
    edge = jnp.logical_and(row == 0, pl.program_id(0) == 0)
    yr = jnp.where(edge, xr * kr * (1.0 / n_fft), (xr * kr - xs * ks) * (2.0 / n_fft))
    ys = jnp.where(edge, xs * ks * (1.0 / n_fft), (xr * ks + xs * kr) * (2.0 / n_fft))
    o_ref[0, 0] = yr.astype(BF16)
    o_ref[0, 1] = ys.astype(BF16)


def _spectrum_product(dft_fwd, z_arr, z_col, kfreq, order, batch, seq, fb):
    return pl.pallas_call(
        functools.partial(_spectrum_kernel, 2 * seq),
        grid=(seq // fb, batch),
        in_specs=[
            pl.BlockSpec((2, fb, seq), lambda f, b: (0, f, 0)),
            pl.BlockSpec((1, seq, HY_W), lambda f, b: (b, 0, z_col)),
            pl.BlockSpec((2, fb, HY_W), lambda f, b: (0, f, order)),
        ],
        out_specs=pl.BlockSpec((1, 2, fb, HY_W), lambda f, b: (b, 0, f, 0)),
        out_shape=jax.ShapeDtypeStruct((batch, 2, seq, HY_W), BF16),
        compiler_params=_params("parallel", "arbitrary"),
        name="hyena_spectrum",
    )(dft_fwd, z_arr, kfreq)


def _inverse_kernel(ft_ref, y_ref, gate_ref, z_ref, d_ref, o_ref):
    conv = jnp.dot(ft_ref[...], y_ref[0], preferred_element_type=F32)
    o_ref[0] = gate_ref[0] * (conv + d_ref[...] * z_ref[0])


def _inverse_gate(dft_inv, y, gate_arr, gate_col, z_arr, z_col, skip_row, batch, seq, tb):
    y2 = y.reshape(batch, 2 * seq, HY_W)
    return pl.pallas_call(
        _inverse_kernel,
        grid=(seq // tb, batch),
        in_specs=[
            pl.BlockSpec((tb, 2 * seq), lambda t, b: (t, 0)),
            pl.BlockSpec((1, 2 * seq, HY_W), lambda t, b: (b, 0, 0)),
            pl.BlockSpec((1, tb, HY_W), lambda t, b: (b, t, gate_col)),
            pl.BlockSpec((1, tb, HY_W), lambda t, b: (b, t, z_col)),
            pl.BlockSpec((1, HY_W), lambda t, b: (0, 0)),
        ],
        out_specs=pl.BlockSpec((1, tb, HY_W), lambda t, b: (b, t, 0)),
        out_shape=jax.ShapeDtypeStruct((batch, seq, HY_W), F32),
        compiler_params=_params("parallel", "arbitrary"),
        name="hyena_inverse_gate",
    )(dft_inv, y2, gate_arr, z_arr, skip_row.reshape(1, HY_W))


def _hyena_mixer(proj, col0, conv_w, conv_b, w1, b1, w2, b2, w3, freq, skip_d, dft_fwd, dft_inv, batch, seq):
    s = _short_conv(proj, col0, conv_w, conv_b, batch, seq, 256)
    filt_t = _hyena_filters_time(w1, b1, w2, b2, w3, freq, seq)
    kfreq = _filter_dft(dft_fwd, filt_t, seq, 512)
    z_arr, z_col = s, 2
    for o in range(HY_ORDER):
        y = _spectrum_product(dft_fwd, z_arr, z_col, kfreq, o, batch, seq, 512)
        z_arr = _inverse_gate(dft_inv, y, s, o, z_arr, z_col, skip_d[o], batch, seq, 512)
        z_col = 0
    return z_arr


@functools.lru_cache(maxsize=None)
def _retention_consts(seq):
    c = RET_CHUNK
    half = RET_HEAD_DIM // 2
    inv = 1.0 / (10000.0 ** np.linspace(0.0, 1.0, half))
    ang = np.arange(seq, dtype=np.float64)[:, None] * inv[None, :]
    cos2 = np.concatenate([np.cos(ang), np.cos(ang)], axis=-1).astype(np.float32)
    sin2 = np.concatenate([-np.sin(ang), np.sin(ang)], axis=-1).astype(np.float32)
    hidx = np.arange(RET_HEADS, dtype=np.float64)
    lg_f = np.log1p(-np.exp2(-5.0 - hidx))[:, None, None]
    lg_b = np.log1p(-np.exp2(-5.5 - hidx))[:, None, None]
    i = np.arange(c, dtype=np.float64)
    diff = i[:, None] - i[None, :]
    ones = np.ones((1, c, c))
    dec = np.where(diff >= 0, np.exp(lg_f * np.maximum(diff, 0.0)), np.exp(lg_b * np.maximum(-diff, 0.0)))
    rowv = lambda v: v[:, :, None] * ones
    tab = np.stack([
        dec,
        rowv(np.exp(lg_f[:, :, 0] * (i + 1.0)[None, :])),
        rowv(np.exp(lg_f[:, :, 0] * (c - 1.0 - i)[None, :])),
        rowv(np.exp(lg_b[:, :, 0] * (c - i)[None, :])),
        rowv(np.exp(lg_b[:, :, 0] * i[None, :])),
        np.exp(lg_f * c) * ones,
        np.exp(lg_b * c) * ones,
    ], axis=1).astype(np.float32)
    return cos2, sin2, tab


def _retention_kernel(q_ref, k_ref, v_ref, g_ref, cos_ref, sin_ref, tab_ref, o_ref, qs_ref, ks_ref):
    c = RET_CHUNK
    d = RET_HEAD_DIM
    seq = q_ref.shape[1]
    n_chunks = seq // c
    cos = cos_ref[...]
    sin = sin_ref[...]
    q = q_ref[0].astype(F32)
    k = k_ref[0].astype(F32)
    qs_ref[...] = (q * cos + pltpu.roll(q, d // 2, 1) * sin) * (d ** -0.5)
    ks_ref[...] = k * cos + pltpu.roll(k, d // 2, 1) * sin

    def mm(a, b):
        return jnp.dot(a.astype(BF16), b.astype(BF16), preferred_element_type=F32)

    def fwd_body(n, state):
        sl = pl.ds(pl.multiple_of(n * c, c), c)
        qc = qs_ref[sl, :]
        kc = ks_ref[sl, :]
        vc = v_ref[0, sl, :]
        a = lax.dot_general(qc.astype(BF16), kc.astype(BF16), NT_DIMS, preferred_element_type=F32) * tab_ref[0, 0]
        y = mm(a, vc) + mm(qc * tab_ref[0, 1], state)
        o_ref[0, sl, :] = y
        return tab_ref[0, 5] * state + mm((kc * tab_ref[0, 2]).T, vc)

    lax.fori_loop(0, n_chunks, fwd_body, jnp.zeros((d, d), F32), unroll=2)

    def bwd_body(m, state):
        n = n_chunks - 1 - m
        sl = pl.ds(pl.multiple_of(n * c, c), c)
        qc = qs_ref[sl, :]
        kc = ks_ref[sl, :]
        vc = v_ref[0, sl, :]
        y = o_ref[0, sl, :] + mm(qc * tab_ref[0, 3], state)
        mu = jnp.mean(y, axis=-1, keepdims=True)
        yc = y - mu
        var = jnp.mean(yc * yc, axis=-1, keepdims=True)
        g = g_ref[0, sl, :].astype(F32)
        o_ref[0, sl, :] = yc * lax.rsqrt(var + GN_EPS) * (g * jax.nn.sigmoid(g))
        return tab_ref[0, 6] * state + mm((kc * tab_ref[0, 4]).T, vc)

    lax.fori_loop(0, n_chunks, bwd_body, jnp.zeros((d, d), F32), unroll=2)


def _retention_mixer(proj, col0, batch, seq):
    cos2, sin2, tab = _retention_consts(seq)
    blk = (1, seq, RET_HEAD_DIM)
    h_ = RET_HEADS
    assert col0 % RET_HEAD_DIM == 0
    c0 = col0 // RET_HEAD_DIM
    return pl.pallas_call(
        _retention_kernel,
        grid=(batch, RET_HEADS),
        in_specs=[
            pl.BlockSpec(blk, lambda b, h: (b, 0, c0 + h)),
            pl.BlockSpec(blk, lambda b, h: (b, 0, c0 + h_ + h)),
            pl.BlockSpec(blk, lambda b, h: (b, 0, c0 + 2 * h_ + h)),
            pl.BlockSpec(blk, lambda b, h: (b, 0, c0 + 3 * h_ + h)),
            pl.BlockSpec((seq, RET_HEAD_DIM), lambda b, h: (0, 0)),
            pl.BlockSpec((seq, RET_HEAD_DIM), lambda b, h: (0, 0)),
            pl.BlockSpec((1, 7, RET_CHUNK, RET_CHUNK), lambda b, h: (h, 0, 0, 0)),
        ],
        out_specs=pl.BlockSpec(blk, lambda b, h: (b, 0, h)),
        out_shape=jax.ShapeDtypeStruct((batch, seq, RET_W), F32),
        scratch_shapes=[pltpu.VMEM((seq, RET_HEAD_DIM), F32), pltpu.VMEM((seq, RET_HEAD_DIM), F32)],
        compiler_params=_params("parallel", "arbitrary"),
        name="retention",
    )(proj, proj, proj, proj, jnp.asarray(cos2), jnp.asarray(sin2), jnp.asarray(tab))


def _cross_kernel(q_ref, k_ref, v_ref, o_ref):
    dh = CROSS_HEAD_DIM
    for h in range(CROSS_HEADS):
        sl = slice(h * dh, (h + 1) * dh)
        s = lax.dot_general(q_ref[:, sl], k_ref[:, sl], NT_DIMS, preferred_element_type=F32) * (dh ** -0.5)
        m = jnp.max(s, axis=-1, keepdims=True)
        p = jnp.exp(s - m)
        l = jnp.sum(p, axis=-1, keepdims=True)
        o = jnp.dot(p.astype(BF16), v_ref[:, sl], preferred_element_type=F32) / l
        o_ref[:, sl] = o.astype(o_ref.dtype)


def _cross_attention(q, kv, batch, seq, n_mem, tq):
    per_b = seq // tq
    return pl.pallas_call(
        _cross_kernel,
        grid=(batch, per_b),
        in_specs=[
            pl.BlockSpec((tq, D_MODEL), lambda b, i: (b * per_b + i, 0)),
            pl.BlockSpec((n_mem, D_MODEL), lambda b, i: (b, 0)),
            pl.BlockSpec((n_mem, D_MODEL), lambda b, i: (b, 1)),
        ],
        out_specs=pl.BlockSpec((tq, D_MODEL), lambda b, i: (b * per_b + i, 0)),
        out_shape=jax.ShapeDtypeStruct((batch * seq, D_MODEL), BF16),
        compiler_params=_params("parallel", "arbitrary"),
        name="cross_attention",
    )(q, kv, kv)


def _router_kernel(x_ref, g_ref, wr_ref, hm_ref, aff_ref):
    h = _rms(x_ref[...], g_ref[...])
    hm_ref[...] = h.astype(BF16)
    logits = lax.dot_general(wr_ref[...], h, NT_DIMS, precision=lax.Precision.HIGHEST,
                             preferred_element_type=F32)
    m = jnp.max(logits, axis=0, keepdims=True)
    e = jnp.exp(logits - m)
    aff_ref[0] = e / jnp.sum(e, axis=0, keepdims=True)


def _router(x2d, gain, w_router_t, batch, seq, tm):
    m = x2d.shape[0]
    per_b = seq // tm
    return pl.pallas_call(
        _router_kernel,
        grid=(m // tm,),
        in_specs=[
            pl.BlockSpec((tm, D_MODEL), lambda i: (i, 0)),
            pl.BlockSpec((1, D_MODEL), lambda i: (0, 0)),
            pl.BlockSpec((N_EXPERTS, D_MODEL), lambda i: (0, 0)),
        ],
        out_specs=[
            pl.BlockSpec((tm, D_MODEL), lambda i: (i, 0)),
            pl.BlockSpec((1, N_EXPERTS, tm), lambda i: (i // per_b, 0, i % per_b)),
        ],
        out_shape=[
            jax.ShapeDtypeStruct((m, D_MODEL), BF16),
            jax.ShapeDtypeStruct((batch, N_EXPERTS, seq), F32),
        ],
        compiler_params=_params("parallel"),
        name="moe_router",
    )(x2d, gain.reshape(1, D_MODEL), w_router_t)


def _sort_descending(x):
    rows, n = x.shape
    lanes = 128
    n_chunks = n // lanes
    chunks = [x[:, c * lanes:(c + 1) * lanes] for c in range(n_chunks)]
    lane = lax.broadcasted_iota(I32, (rows, lanes), 1)
    k = 2
    while k <= n:
        j = k // 2
        while j >= 1:
            nxt = []
            for c, xc in enumerate(chunks):
                if j >= lanes:
                    lo = (c & (j // lanes)) == 0
                    partner = chunks[c ^ (j // lanes)]
                else:
                    lo = (lane & j) == 0
                    partner = jnp.where(lo, pltpu.roll(xc, lanes - j, 1), pltpu.roll(xc, j, 1))
                desc = ((c * lanes) & k) == 0 if k >= lanes else (lane & k) == 0
                hi, lw = jnp.maximum(xc, partner), jnp.minimum(xc, partner)
                if j >= lanes and k >= lanes:
                    nxt.append(hi if lo == desc else lw)
                else:
                    nxt.append(jnp.where(lo == desc, hi, lw))
            chunks = nxt
            j //= 2
        k *= 2
    return jnp.concatenate(chunks, axis=1)


def _topk_kernel(cap, n_e, aff_ref, slot_ref, wsel_ref, slot_t_ref, thr_ref):
    n_rows, n_tok = aff_ref.shape

    def thr_body(i, carry):
        rows8 = pl.ds(pl.multiple_of(i * 8, 8), 8)
        thr_ref[rows8, :] = _sort_descending(aff_ref[rows8, :])[:, cap - 1:cap]
        return carry

    lax.fori_loop(0, n_rows // 8, thr_body, 0)
    a = aff_ref[...]
    thr = thr_ref[...]
    gt = a > thr
    eq = a == thr
    need = cap - jnp.sum(gt.astype(I32), axis=1, keepdims=True)
    upper = (lax.broadcasted_iota(I32, (n_tok, n_tok), 0) < lax.broadcasted_iota(I32, (n_tok, n_tok), 1))
    upper = upper.astype(BF16)
    eq_rank = jnp.dot(eq.astype(BF16), upper, preferred_element_type=F32)
    sel = jnp.logical_or(gt, jnp.logical_and(eq, eq_rank < need.astype(F32)))
    rank = jnp.dot(sel.astype(BF16), upper, preferred_element_type=F32)
    slot = jnp.where(sel, rank, -1.0)
    slot_ref[...] = slot.astype(I32)
    wsel_ref[...] = jnp.where(sel, a, 0.0)
    pad = jnp.full((128 - n_e, n_tok), -1.0, F32)
    for b in range(n_rows // n_e):
        slot_t_ref[b] = jnp.concatenate([slot[b * n_e:(b + 1) * n_e], pad], axis=0).T.astype(I32)


def _topk_select(aff2d, cap, n_e):
    rows, n_tok = aff2d.shape
    spec = pl.BlockSpec((rows, n_tok), lambda i: (0, 0))
    spec_t = pl.BlockSpec((rows // n_e, n_tok, 128), lambda i: (0, 0, 0))
    return pl.pallas_call(
        functools.partial(_topk_kernel, cap, n_e),
        grid=(1,),
        in_specs=[spec],
        out_specs=[spec, spec, spec_t],
        out_shape=[jax.ShapeDtypeStruct((rows, n_tok), I32), jax.ShapeDtypeStruct((rows, n_tok), F32),
                   jax.ShapeDtypeStruct((rows // n_e, n_tok, 128), I32)],
        scratch_shapes=[pltpu.VMEM((rows, 1), F32)],
        compiler_params=_params("arbitrary"),
        name="moe_topk_select",
    )(aff2d)


def _gather_kernel(cap, slot_ref, wsel_ref, hm_ref, xe_ref, gs_ref):
    n_tok = hm_ref.shape[0]
    onehot = slot_ref[0] == lax.broadcasted_iota(I32, (cap, n_tok), 0)
    xe_ref[0] = jnp.dot(onehot.astype(BF16), hm_ref[...], preferred_element_type=F32).astype(BF16)
    gs_ref[0] = jnp.sum(jnp.where(onehot, wsel_ref[0], 0.0), axis=1, keepdims=True)


def _gather_tokens(slot, wsel, hm, batch, seq, cap):
    rows = batch * N_EXPERTS
    row_spec = pl.BlockSpec((1, 1, seq), lambda b, e: (b * N_EXPERTS + e, 0, 0))
    return pl.pallas_call(
        functools.partial(_gather_kernel, cap),
        grid=(batch, N_EXPERTS),
        in_specs=[row_spec, row_spec, pl.BlockSpec((seq, D_MODEL), lambda b, e: (b, 0))],
        out_specs=[
            pl.BlockSpec((1, cap, D_MODEL), lambda b, e: (e, b, 0)),
            pl.BlockSpec((1, cap, 1), lambda b, e: (e, b, 0)),
        ],
        out_shape=[
            jax.ShapeDtypeStruct((N_EXPERTS, batch * cap, D_MODEL), BF16),
            jax.ShapeDtypeStruct((N_EXPERTS, batch * cap, 1), F32),
        ],
        compiler_params=_params("parallel", "arbitrary"),
        name="moe_gather",
    )(slot.reshape(rows, 1, seq), wsel.reshape(rows, 1, seq), hm)


def _expert_kernel(xe_ref, wg_ref, wu_ref, wd_ref, gs_ref, ye_ref, acc_ref):
    f = pl.program_id(1)
    x = xe_ref[0]
    a = jnp.dot(x, wg_ref[...].astype(BF16), preferred_element_type=F32)
    u = jnp.dot(x, wu_ref[...].astype(BF16), preferred_element_type=F32)
    mid = (a * jax.nn.sigmoid(a) * u).astype(BF16)
    contrib = jnp.dot(mid, wd_ref[...].astype(BF16), preferred_element_type=F32)

    @pl.when(f == 0)
    def _():
        acc_ref[...] = contrib

    @pl.when(f > 0)
    def _():
        acc_ref[...] += contrib

    @pl.when(f == pl.num_programs(1) - 1)
    def _():
        ye_ref[0] = (acc_ref[...] * gs_ref[0]).astype(BF16)


def _expert_ffn(xe, gs, w_gate, w_up, w_down, layer, tf):
    n_e, rows, d = xe.shape
    ff = w_gate.shape[-1]
    return pl.pallas_call(
        _expert_kernel,
        grid=(n_e, ff // tf),
        in_specs=[
            pl.BlockSpec((1, rows, d), lambda e, f: (e, 0, 0)),
            pl.BlockSpec((None, None, d, tf), lambda e, f: (layer, e, 0, f)),
            pl.BlockSpec((None, None, d, tf), lambda e, f: (layer, e, 0, f)),
            pl.BlockSpec((None, None, tf, d), lambda e, f: (layer, e, f, 0)),
            pl.BlockSpec((1, rows, 1), lambda e, f: (e, 0, 0)),
        ],
        out_specs=pl.BlockSpec((1, rows, d), lambda e, f: (e, 0, 0)),
        out_shape=jax.ShapeDtypeStruct((n_e, rows, d), BF16),
        scratch_shapes=[pltpu.VMEM((rows, d), F32)],
        compiler_params=_params("parallel", "arbitrary"),
        name="moe_expert_ffn",
    )(xe, w_gate, w_up, w_down, gs)


def _scatter_kernel(cap, slot_ref, ye_ref, x_ref, o_ref, onehot_ref):
    n_e = ye_ref.shape[0]
    n_tok = x_ref.shape[0]

    @pl.when(pl.program_id(1) == 0)
    def _():
        col = lax.broadcasted_iota(I32, (n_tok, cap), 1)
        slots = slot_ref[0]
        for e in range(n_e):
            onehot_ref[:, e * cap:(e + 1) * cap] = (slots[:, e:e + 1] == col).astype(BF16)

    ye = ye_ref[...].reshape(n_e * cap, ye_ref.shape[-1])
    o_ref[...] = x_ref[...] + jnp.dot(onehot_ref[...], ye, preferred_element_type=F32)


def _scatter_add(slot_t, ye, x2d, batch, seq, cap, tn):
    return pl.pallas_call(
        functools.partial(_scatter_kernel, cap),
        grid=(batch, D_MODEL // tn),
        in_specs=[
            pl.BlockSpec((1, seq, 128), lambda b, n: (b, 0, 0)),
            pl.BlockSpec((N_EXPERTS, cap, tn), lambda b, n: (0, b, n)),
            pl.BlockSpec((seq, tn), lambda b, n: (b, n)),
        ],
        out_specs=pl.BlockSpec((seq, tn), lambda b, n: (b, n)),
        out_shape=jax.ShapeDtypeStruct(x2d.shape, F32),
        scratch_shapes=[pltpu.VMEM((seq, N_EXPERTS * cap), BF16)],
        compiler_params=_params("parallel", "arbitrary"),
        name="moe_scatter_add",
    )(slot_t, ye, x2d)


def _expert_choice_ffn(x2d, gain, w_router, w_gate, w_up, w_down, layer, batch, seq):
    cap = EC_CAPACITY * seq // N_EXPERTS
    hm, aff = _router(x2d, gain, w_router.T, batch, seq, 512)
    slot, wsel, slot_t = _topk_select(aff.reshape(batch * N_EXPERTS, seq), cap, N_EXPERTS)
    xe, gs = _gather_tokens(slot, wsel, hm, batch, seq, cap)
    ye = _expert_ffn(xe, gs, w_gate, w_up, w_down, layer, 256)
    return _scatter_add(slot_t, ye, x2d, batch, seq, cap, 512)


def _final_norm_kernel(x_ref, g_ref, o_ref):
    o_ref[...] = _rms(x_ref[...], g_ref[...])


def _final_norm(x2d, gain, tm):
    m, d = x2d.shape
    return pl.pallas_call(
        _final_norm_kernel,
        grid=(m // tm,),
        in_specs=[pl.BlockSpec((tm, d), lambda i: (i, 0)), pl.BlockSpec((1, d), lambda i: (0, 0))],
        out_specs=pl.BlockSpec((tm, d), lambda i: (i, 0)),
        out_shape=jax.ShapeDtypeStruct((m, d), F32),
        compiler_params=_params("parallel"),
        name="final_norm",
    )(x2d, gain.reshape(1, d))


def kernel(x, mem, norm_mix, w_in, na_rpb, hy_conv_w, hy_conv_b, hy_filt_w1, hy_filt_b1, hy_filt_w2, hy_filt_b2, hy_filt_w3, hy_sin_freq, hy_skip_d, branch_norm, w_out, norm_cross, mem_norm, w_cq, w_ckv, w_co, norm_moe, w_router, w_gate, w_up, w_down, final_norm):
    batch, seq, d = x.shape
    n_mem = mem.shape[1]
    depth = w_in.shape[0]
    m = batch * seq
    na_cols = 3 * NA_W
    hy_cols = 3 * HY_W
    rows = seq // GRID_W

    dft_fwd, dft_inv = _dft_tables(seq)
    x2d = x.reshape(m, d)
    mem2d = mem.reshape(batch * n_mem, d)

    w_in_b, w_out_b, w_cq_b, w_ckv_b, w_co_b = (w.astype(BF16) for w in (w_in, w_out, w_cq, w_ckv, w_co))
    p_in = w_in.shape[-1]

    for l in range(depth):
        proj = _norm_matmul(x2d, norm_mix[l], _whole(w_in_b, l), 1024, 768, BF16, "in_proj")
        proj = proj.reshape(batch, seq, p_in)

        y_na = _neighbourhood_attention(proj, _na_bias_table(na_rpb[l], rows), batch, seq)
        y_hy = _hyena_mixer(proj, na_cols, hy_conv_w[l], hy_conv_b[l], hy_filt_w1[l],
                            hy_filt_b1[l], hy_filt_w2[l], hy_filt_b2[l], hy_filt_w3[l], hy_sin_freq[l],
                            hy_skip_d[l], dft_fwd, dft_inv, batch, seq)
        y_ret = _retention_mixer(proj, na_cols + hy_cols, batch, seq)
        x2d = _branch_out_proj(y_na.reshape(m, NA_W), y_hy.reshape(m, HY_W), y_ret.reshape(m, RET_W),
                               branch_norm[l], _whole(w_out_b, l), x2d, 1024, 512)

        q = _norm_matmul(x2d, norm_cross[l], _whole(w_cq_b, l), 1024, 512, BF16, "cross_q_proj")
        kv = _norm_matmul(mem2d, mem_norm, _whole(w_ckv_b, l), 512, 1024, BF16, "cross_kv_proj")
        o = _cross_attention(q, kv, batch, seq, n_mem, 512)
        x2d = _plain_matmul(o, _whole(w_co_b, l), 1024, 512, F32, "cross_out_proj", res=x2d)

        x2d = _expert_choice_ffn(x2d, norm_moe[l], w_router[l], w_gate, w_up, w_down, l, batch, seq)

    return _final_norm(x2d, final_norm, 512).reshape(batch, seq, d)
```

```python
import functools
import math

import numpy as np
import jax
import jax.numpy as jnp
from jax import lax
from jax.experimental import pallas as pl
from jax.experimental.pallas import tpu as pltpu

F32 = jnp.float32
BF16 = jnp.bfloat16
I32 = jnp.int32

D_MODEL = 2048
GRID_W = 64
NA_HEAD_DIM = 64
NA_W = 768
NA_HEADS = 12
NA_KR = 8
NA_KC = 16
HY_W = 512
HY_ORDER = 2
HY_BANDS = 8
HY_POS_DIM = 17
HY_FILT_FF = 64
RET_HEAD_DIM = 128
RET_W = 768
RET_HEADS = 6
RET_CHUNK = 128
CROSS_HEADS = 4
CROSS_HEAD_DIM = 512
N_EXPERTS = 16
EXPERT_FF = 2048
EC_CAPACITY = 2
RMS_EPS = 1e-6
GN_EPS = 1e-5

MASK_VALUE = -1e30
VMEM_LIMIT_BYTES = 56 * 1024 * 1024

NT_DIMS = (((1,), (1,)), ((), ()))


def _params(*sem):
    return pltpu.CompilerParams(dimension_semantics=sem, vmem_limit_bytes=VMEM_LIMIT_BYTES)


def _rms(xf, g):
    return xf * lax.rsqrt(jnp.mean(xf * xf, axis=-1, keepdims=True) + RMS_EPS) * g


def _mm_kernel(prologue, n_pro, has_res, *refs):
    pro_refs = refs[:n_pro]
    w_ref = refs[n_pro]
    res_ref = refs[n_pro + 1] if has_res else None
    o_ref = refs[n_pro + 1 + has_res]
    h_ref = refs[n_pro + 2 + has_res]

    @pl.when(pl.program_id(1) == 0)
    def _():
        h_ref[...] = prologue(*pro_refs).astype(BF16)

    acc = jnp.dot(h_ref[...], w_ref[...], preferred_element_type=F32)
    if has_res:
        acc = acc + res_ref[...]
    o_ref[...] = acc.astype(o_ref.dtype)


def _fused_matmul(prologue, pro_args, pro_specs, wsel, res, m, tm, tn, out_dtype, name):
    w, layer, col0, n = wsel
    k = w.shape[1]
    cb0 = col0 // tn
    assert col0 % tn == 0 and n % tn == 0 and m % tm == 0
    has_res = res is not None
    in_specs = list(pro_specs) + [pl.BlockSpec((None, k, tn), lambda i, j: (layer, 0, cb0 + j))]
    args = list(pro_args) + [w]
    if has_res:
        in_specs.append(pl.BlockSpec((tm, tn), lambda i, j: (i, j)))
        args.append(res)
    return pl.pallas_call(
        functools.partial(_mm_kernel, prologue, len(pro_args), has_res),
        grid=(m // tm, n // tn),
        in_specs=in_specs,
        out_specs=pl.BlockSpec((tm, tn), lambda i, j: (i, j)),
        out_shape=jax.ShapeDtypeStruct((m, n), out_dtype),
        scratch_shapes=[pltpu.VMEM((tm, k), BF16)],
        compiler_params=_params("parallel", "arbitrary"),
        name=name,
    )(*args)


def _rms_prologue(x_ref, g_ref):
    return _rms(x_ref[...], g_ref[...])


def _whole(w, layer):
    return (w, layer, 0, w.shape[-1])


def _norm_matmul(x2d, gain, wsel, tm, tn, out_dtype, name, res=None):
    m, k = x2d.shape
    specs = [pl.BlockSpec((tm, k), lambda i, j: (i, 0)), pl.BlockSpec((1, k), lambda i, j: (0, 0))]
    return _fused_matmul(_rms_prologue, [x2d, gain.reshape(1, k)], specs, wsel, res, m, tm, tn, out_dtype, name)


def _identity_prologue(a_ref):
    return a_ref[...]


def _plain_matmul(a, wsel, tm, tn, out_dtype, name, res=None):
    m, k = a.shape
    specs = [pl.BlockSpec((tm, k), lambda i, j: (i, 0))]
    return _fused_matmul(_identity_prologue, [a], specs, wsel, res, m, tm, tn, out_dtype, name)


def _branch_prologue(na_ref, hy_ref, ret_ref, g_ref):
    def nrm(p):
        return p * lax.rsqrt(jnp.mean(p * p, axis=-1, keepdims=True) + RMS_EPS)

    y = jnp.concatenate([nrm(na_ref[...]), nrm(hy_ref[...]), nrm(ret_ref[...])], axis=-1)
    return y * g_ref[...]


def _branch_out_proj(y_na, y_hy, y_ret, gain, wsel, res, tm, tn):
    m = y_na.shape[0]
    specs = [
        pl.BlockSpec((tm, NA_W), lambda i, j: (i, 0)),
        pl.BlockSpec((tm, HY_W), lambda i, j: (i, 0)),
        pl.BlockSpec((tm, RET_W), lambda i, j: (i, 0)),
        pl.BlockSpec((1, D_MODEL), lambda i, j: (0, 0)),
    ]
    return _fused_matmul(_branch_prologue, [y_na, y_hy, y_ret, gain.reshape(1, D_MODEL)], specs, wsel, res,
                         m, tm, tn, F32, "branch_out_proj")


NA_PAIR = 2
NA_BAND = NA_KR + NA_PAIR


def _na_band_base(r, rows):
    return np.clip(r - NA_KR // 2, 0, rows - NA_BAND)


@functools.lru_cache(maxsize=None)
def _na_variants(rows):
    assert rows % NA_PAIR == 0 and rows >= NA_BAND + 2
    n_var = NA_KR // 2 + 1
    dr = np.full((n_var, NA_BAND, NA_PAIR), -2, np.int64)
    for r in range(0, rows, NA_PAIR):
        base = int(_na_band_base(r, rows))
        v = (r - base) // 2
        for j in range(NA_PAIR):
            rs = int(np.clip(r + j - NA_KR // 2, 0, rows - NA_KR))
            for i in range(NA_BAND):
                val = base + i - (r + j) + (NA_KR - 1) if rs <= base + i < rs + NA_KR else -1
                assert dr[v, i, j] in (-2, val)
                dr[v, i, j] = val
    assert (dr > -2).all()
    return dr


def _na_bias_table(rpb, rows):
    c = np.arange(GRID_W)
    col_start = np.clip(c - NA_KC // 2, 0, GRID_W - NA_KC)
    col_in = (c[None, :] >= col_start[:, None]) & (c[None, :] < col_start[:, None] + NA_KC)
    dc = np.clip(c[None, :] - c[:, None] + (NA_KC - 1), 0, 2 * NA_KC - 2)
    onehot = (dc.reshape(-1)[None, :] == np.arange(2 * NA_KC - 1)[:, None]).astype(np.float32)
    cols = jnp.einsum("hab,bn->han", rpb.astype(F32), jnp.asarray(onehot), precision=lax.Precision.HIGHEST)
    cols = cols.reshape(NA_HEADS, 2 * NA_KR - 1, GRID_W, GRID_W)
    cols = jnp.where(col_in[None, None], cols, MASK_VALUE).transpose(0, 1, 3, 2)
    masked = jnp.full((NA_HEADS, GRID_W, GRID_W), MASK_VALUE, F32)
    dr = _na_variants(rows)
    variants = []
    for v in range(dr.shape[0]):
        band = [jnp.concatenate([cols[:, dr[v, i, j]] if dr[v, i, j] >= 0 else masked for j in range(NA_PAIR)],
                                axis=-1) for i in range(NA_BAND)]
        variants.append(jnp.concatenate(band, axis=1))
    return jnp.stack(variants, axis=1)


def _na_kernel(rows, q_ref, k_ref, v_ref, bias_ref, o_ref, vt_ref):
    dh = NA_HEAD_DIM
    nq = NA_PAIR * GRID_W
    nk = NA_BAND * GRID_W
    n_chunk = nk // 128
    lane = lax.broadcasted_iota(I32, (nq, 2 * dh), 1)

    for ch in range(vt_ref.shape[0]):
        vt_ref[ch] = v_ref[0, ch * 128:(ch + 1) * 128, :].astype(F32).T.astype(BF16)

    def body(p, carry):
        r = p * NA_PAIR
        base = jnp.clip(r - NA_KR // 2, 0, rows - NA_BAND)
        variant = (r - base) // 2
        q = q_ref[0, pl.ds(pl.multiple_of(r * GRID_W, nq), nq), :] * (dh ** -0.5)
        kb = k_ref[0, pl.ds(pl.multiple_of(base * GRID_W, 128), nk), :]
        c0 = base // 2
        vt = jnp.concatenate([vt_ref[c0 + i] for i in range(n_chunk)], axis=1)
        outs = []
        for hh in range(2):
            qh = jnp.where((lane >= dh) == bool(hh), q, jnp.zeros_like(q))
            st = lax.dot_general(kb, qh, NT_DIMS, preferred_element_type=F32)
            st = st + bias_ref[hh, variant]
            m = jnp.max(st, axis=0, keepdims=True)
            pt = jnp.exp(st - m)
            l = jnp.sum(pt, axis=0, keepdims=True)
            ot = jnp.dot(vt[hh * dh:(hh + 1) * dh, :], pt.astype(BF16), preferred_element_type=F32)
            outs.append(ot / l)
        o_ref[0, pl.ds(pl.multiple_of(r * GRID_W, nq), nq), :] = jnp.concatenate(outs, axis=0).T
        return carry

    lax.fori_loop(0, rows // NA_PAIR, body, 0, unroll=2)


def _neighbourhood_attention(proj_na, bias_tbl, batch, seq):
    rows = seq // GRID_W
    n_pairs = NA_HEADS // 2
    blk = (1, seq, 2 * NA_HEAD_DIM)
    n_var, nk, nq = bias_tbl.shape[1:]
    return pl.pallas_call(
        functools.partial(_na_kernel, rows),
        grid=(batch, n_pairs),
        in_specs=[
            pl.BlockSpec(blk, lambda b, h: (b, 0, h)),
            pl.BlockSpec(blk, lambda b, h: (b, 0, n_pairs + h)),
            pl.BlockSpec(blk, lambda b, h: (b, 0, 2 * n_pairs + h)),
            pl.BlockSpec((2, n_var, nk, nq), lambda b, h: (h, 0, 0, 0)),
        ],
        out_specs=pl.BlockSpec(blk, lambda b, h: (b, 0, h)),
        out_shape=jax.ShapeDtypeStruct((batch, seq, NA_W), F32),
        scratch_shapes=[pltpu.VMEM((seq // 128, 2 * NA_HEAD_DIM, 128), BF16)],
        compiler_params=_params("parallel", "arbitrary"),
        name="neighbourhood_attention",
    )(proj_na, proj_na, proj_na, bias_tbl)


@functools.lru_cache(maxsize=None)
def _dft_factors(seq):
    n = 2 * seq
    t = np.arange(seq, dtype=np.int64)
    f1 = np.arange(seq // 64, dtype=np.int64)
    f0 = np.arange(64, dtype=np.int64)
    a = 2.0 * np.pi * ((64 * f1[:, None] * t[None, :]) % n).astype(np.float64) / n
    b = 2.0 * np.pi * ((f0[:, None] * t[None, :]) % n).astype(np.float64) / n
    ny = np.where(t % 2 == 0, 1.0, -1.0)
    return tuple(np.asarray(v, np.float32) for v in (np.cos(a), np.sin(a), np.cos(b), np.sin(b), ny))


def _dft_tables(seq):
    ca, sa, cb, sb, ny = (jnp.asarray(v) for v in _dft_factors(seq))
    cos = (ca[:, None, :] * cb[None] - sa[:, None, :] * sb[None]).reshape(seq, seq)
    sin = (sa[:, None, :] * cb[None] + ca[:, None, :] * sb[None]).reshape(seq, seq)
    sin = jnp.concatenate([ny[None, :], sin[1:]], axis=0)
    fwd = jnp.stack([cos, sin]).astype(BF16)
    inv = jnp.concatenate([cos.T, sin.T], axis=1).astype(BF16)
    return fwd, inv


@functools.lru_cache(maxsize=None)
def _hyena_consts(seq):
    t = np.arange(seq, dtype=np.float64)
    t01 = t / (seq - 1)
    bands = np.linspace(1e-4, HY_BANDS - 1, HY_BANDS)
    ang = (2.0 * math.pi) * (t[:, None] / seq) * bands[None, :]
    feats = np.concatenate([t01[:, None], np.cos(ang), -np.sin(ang)], axis=-1)
    feats_p = np.zeros((seq, 128), np.float32)
    feats_p[:, :HY_POS_DIM] = feats
    min_decay = math.log(1e-2) / 1.5
    max_decay = math.log(1e-2) / 0.3
    deltas = np.abs(np.linspace(min_decay, max_decay, HY_W))
    window = np.exp(-t01[:, None] * deltas[None, :]).astype(np.float32)
    return feats_p, window


def _filter_kernel(feats_ref, w1_ref, b1_ref, w2_ref, b2_ref, freq_ref, w3_ref, win_ref, o_ref, hid_ref):
    hp = lax.Precision.HIGHEST

    @pl.when(pl.program_id(0) == 0)
    def _():
        f = freq_ref[...]
        h1 = jnp.sin(f * (jnp.dot(feats_ref[...], w1_ref[...], precision=hp, preferred_element_type=F32)
                          + b1_ref[...]))
        hid_ref[...] = jnp.sin(f * (jnp.dot(h1, w2_ref[...], precision=hp, preferred_element_type=F32)
                                    + b2_ref[...]))

    h = jnp.dot(hid_ref[...], w3_ref[...], precision=hp, preferred_element_type=F32) * win_ref[...]
    row = lax.broadcasted_iota(I32, h.shape, 0)
    drop = jnp.logical_and(row == 0, pl.program_id(0) >= HY_ORDER)
    o_ref[...] = jnp.where(drop, 0.0, h).astype(BF16)


def _hyena_filters_time(w1, b1, w2, b2, w3, freq, seq):
    feats, window = _hyena_consts(seq)
    w1p = jnp.zeros((128, HY_FILT_FF), F32).at[:HY_POS_DIM].set(w1)
    w3r = w3.reshape(HY_FILT_FF, HY_ORDER, 2, HY_W).transpose(0, 2, 1, 3).reshape(HY_FILT_FF, 4 * HY_W)
    full = lambda shape: pl.BlockSpec(shape, lambda c: (0,) * len(shape))
    return pl.pallas_call(
        _filter_kernel,
        grid=(4,),
        in_specs=[
            full((seq, 128)), full((128, HY_FILT_FF)), full((1, HY_FILT_FF)),
            full((HY_FILT_FF, HY_FILT_FF)), full((1, HY_FILT_FF)), full((1, HY_FILT_FF)),
            pl.BlockSpec((HY_FILT_FF, HY_W), lambda c: (0, c)),
            full((seq, HY_W)),
        ],
        out_specs=pl.BlockSpec((seq, HY_W), lambda c: (0, c)),
        out_shape=jax.ShapeDtypeStruct((seq, 4 * HY_W), BF16),
        scratch_shapes=[pltpu.VMEM((seq, HY_FILT_FF), F32)],
        compiler_params=_params("arbitrary"),
        name="hyena_filter_mlp",
    )(jnp.asarray(feats), w1p, b1.reshape(1, -1), w2, b2.reshape(1, -1), freq.reshape(1, -1), w3r,
      jnp.asarray(window))


def _filter_dft_kernel(f_ref, fwd_ref, bwd_ref, o_ref):
    fwd = fwd_ref[...]
    bwd = bwd_ref[...]
    uc = jnp.dot(f_ref[0], fwd, preferred_element_type=F32)
    vc = jnp.dot(f_ref[0], bwd, preferred_element_type=F32)
    us = jnp.dot(f_ref[1], fwd, preferred_element_type=F32)
    vs = jnp.dot(f_ref[1], bwd, preferred_element_type=F32)
    o_ref[0] = uc + vc
    row = lax.broadcasted_iota(I32, us.shape, 0)
    nyq = jnp.logical_and(row == 0, pl.program_id(0) == 0)
    o_ref[1] = jnp.where(nyq, us + vs, us - vs)


def _filter_dft(dft_fwd, filt_t, seq, fb):
    return pl.pallas_call(
        _filter_dft_kernel,
        grid=(seq // fb, HY_ORDER),
        in_specs=[
            pl.BlockSpec((2, fb, seq), lambda f, o: (0, f, 0)),
            pl.BlockSpec((seq, HY_W), lambda f, o: (0, o)),
            pl.BlockSpec((seq, HY_W), lambda f, o: (0, HY_ORDER + o)),
        ],
        out_specs=pl.BlockSpec((2, fb, HY_W), lambda f, o: (0, f, o)),
        out_shape=jax.ShapeDtypeStruct((2, seq, HY_ORDER * HY_W), F32),
        compiler_params=_params("parallel", "arbitrary"),
        name="hyena_filter_dft",
    )(dft_fwd, filt_t, filt_t)


def _short_conv_kernel(p_ref, w_ref, b_ref, o_ref):
    p = p_ref[0].astype(F32)
    seq = p.shape[0]
    row = lax.broadcasted_iota(I32, p.shape, 0)
    prev = jnp.where(row == 0, 0.0, pltpu.roll(p, 1, 0))
    nxt = jnp.where(row == seq - 1, 0.0, pltpu.roll(p, seq - 1, 0))
    w = w_ref[...]
    o_ref[0] = prev * w[0:1] + p * w[1:2] + nxt * w[2:3] + b_ref[...]


def _short_conv(proj, col0, conv_w, conv_b, batch, seq, tc):
    n_cols = conv_w.shape[-1]
    assert col0 % tc == 0 and n_cols % tc == 0
    cb0 = col0 // tc
    return pl.pallas_call(
        _short_conv_kernel,
        grid=(batch, n_cols // tc),
        in_specs=[
            pl.BlockSpec((1, seq, tc), lambda b, c: (b, 0, cb0 + c)),
            pl.BlockSpec((3, tc), lambda b, c: (0, c)),
            pl.BlockSpec((1, tc), lambda b, c: (0, c)),
        ],
        out_specs=pl.BlockSpec((1, seq, tc), lambda b, c: (b, 0, c)),
        out_shape=jax.ShapeDtypeStruct((batch, seq, n_cols), F32),
        compiler_params=_params("parallel", "arbitrary"),
        name="hyena_short_conv",
    )(proj, conv_w, conv_b.reshape(1, -1))


def _spectrum_kernel(n_fft, f_ref, z_ref, k_ref, o_ref):
    z = z_ref[0].astype(BF16)
    xr = jnp.dot(f_ref[0], z, preferred_element_type=F32)
    xs = jnp.dot(f_ref[1], z, preferred_element_type=F32)
    kr = k_ref[0]
    ks = k_ref[1]
    row = lax.broadcasted_iota(I32, xr.shape, 0)
    edge = jnp.logical_and(row == 0, pl.program_id(0) == 0)
    yr = jnp.where(edge, xr * kr * (1.0 / n_fft), (xr * kr - xs * ks) * (2.0 / n_fft))
    ys = jnp.where(edge, xs * ks * (1.0 / n_fft), (xr * ks + xs * kr) * (2.0 / n_fft))
    o_ref[0, 0] = yr.astype(BF16)
    o_ref[0, 1] = ys.astype(BF16)


def _spectrum_product(dft_fwd, z_arr, z_col, kfreq, order, batch, seq, fb):
    return pl.pallas_call(
        functools.partial(_spectrum_kernel, 2 * seq),
        grid=(seq // fb, batch),
        in_specs=[
            pl.BlockSpec((2, fb, seq), lambda f, b: (0, f, 0)),
            pl.BlockSpec((1, seq, HY_W), lambda f, b: (b, 0, z_col)),
            pl.BlockSpec((2, fb, HY_W), lambda f, b: (0, f, order)),
        ],
        out_specs=pl.BlockSpec((1, 2, fb, HY_W), lambda f, b: (b, 0, f, 0)),
        out_shape=jax.ShapeDtypeStruct((batch, 2, seq, HY_W), BF16),
        compiler_params=_params("parallel", "arbitrary"),
        name="hyena_spectrum",
    )(dft_fwd, z_arr, kfreq)


def _inverse_kernel(ft_ref, y_ref, gate_ref, z_ref, d_ref, o_ref):
    conv = jnp.dot(ft_ref[...], y_ref[0], preferred_element_type=F32)
    o_ref[0] = gate_ref[0] * (conv + d_ref[...] * z_ref[0])


def _inverse_gate(dft_inv, y, gate_arr, gate_col, z_arr, z_col, skip_row, batch, seq, tb):
    y2 = y.reshape(batch, 2 * seq, HY_W)
    return pl.pallas_call(
        _inverse_kernel,
        grid=(seq // tb, batch),
        in_specs=[
            pl.BlockSpec((tb, 2 * seq), lambda t, b: (t, 0)),
            pl.BlockSpec((1, 2 * seq, HY_W), lambda t, b: (b, 0, 0)),
            pl.BlockSpec((1, tb, HY_W), lambda t, b: (b, t, gate_col)),
            pl.BlockSpec((1, tb, HY_W), lambda t, b: (b, t, z_col)),
            pl.BlockSpec((1, HY_W), lambda t, b: (0, 0)),
        ],
        out_specs=pl.BlockSpec((1, tb, HY_W), lambda t, b: (b, t, 0)),
        out_shape=jax.ShapeDtypeStruct((batch, seq, HY_W), F32),
        compiler_params=_params("parallel", "arbitrary"),
        name="hyena_inverse_gate",
    )(dft_inv, y2, gate_arr, z_arr, skip_row.reshape(1, HY_W))


def _hyena_mixer(proj, col0, conv_w, conv_b, w1, b1, w2, b2, w3, freq, skip_d, dft_fwd, dft_inv, batch, seq):
    s = _short_conv(proj, col0, conv_w, conv_b, batch, seq, 256)
    filt_t = _hyena_filters_time(w1, b1, w2, b2, w3, freq, seq)
    kfreq = _filter_dft(dft_fwd, filt_t, seq, 512)
    z_arr, z_col = s, 2
    for o in range(HY_ORDER):
        y = _spectrum_product(dft_fwd, z_arr, z_col, kfreq, o, batch, seq, 512)
        z_arr = _inverse_gate(dft_inv, y, s, o, z_arr, z_col, skip_d[o], batch, seq, 512)
        z_col = 0
    return z_arr


@functools.lru_cache(maxsize=None)
def _retention_consts(seq):
    c = RET_CHUNK
    half = RET_HEAD_DIM // 2
    inv = 1.0 / (10000.0 ** np.linspace(0.0, 1.0, half))
    ang = np.arange(seq, dtype=np.float64)[:, None] * inv[None, :]
    cos2 = np.concatenate([np.cos(ang), np.cos(ang)], axis=-1).astype(np.float32)
    sin2 = np.concatenate([-np.sin(ang), np.sin(ang)], axis=-1).astype(np.float32)
    hidx = np.arange(RET_HEADS, dtype=np.float64)
    lg_f = np.log1p(-np.exp2(-5.0 - hidx))[:, None, None]
    lg_b = np.log1p(-np.exp2(-5.5 - hidx))[:, None, None]
    i = np.arange(c, dtype=np.float64)
    diff = i[:, None] - i[None, :]
    ones = np.ones((1, c, c))
    dec = np.where(diff >= 0, np.exp(lg_f * np.maximum(diff, 0.0)), np.exp(lg_b * np.maximum(-diff, 0.0)))
    rowv = lambda v: v[:, :, None] * ones
    tab = np.stack([
        dec,
        rowv(np.exp(lg_f[:, :, 0] * (i + 1.0)[None, :])),
        rowv(np.exp(lg_f[:, :, 0] * (c - 1.0 - i)[None, :])),
        rowv(np.exp(lg_b[:, :, 0] * (c - i)[None, :])),
        rowv(np.exp(lg_b[:, :, 0] * i[None, :])),
        np.exp(lg_f * c) * ones,
        np.exp(lg_b * c) * ones,
    ], axis=1).astype(np.float32)
    return cos2, sin2, tab


def _retention_kernel(q_ref, k_ref, v_ref, g_ref, cos_ref, sin_ref, tab_ref, o_ref, qs_ref, ks_ref, kvf_ref,
                      kvb_ref):
    c = RET_CHUNK
    d = RET_HEAD_DIM
    seq = q_ref.shape[1]
    n_chunks = seq // c
    cos = cos_ref[...]
    sin = sin_ref[...]
    q = q_ref[0].astype(F32)
    k = k_ref[0].astype(F32)
    qs_ref[...] = (q * cos + pltpu.roll(q, d // 2, 1) * sin) * (d ** -0.5)
    ks_ref[...] = k * cos + pltpu.roll(k, d // 2, 1) * sin

    def mm(a, b):
        return jnp.dot(a.astype(BF16), b.astype(BF16), preferred_element_type=F32)

    def chunk(n):
        return pl.ds(pl.multiple_of(n * c, c), c)

    def kv_body(n, carry):
        kc = ks_ref[chunk(n), :]
        vc = v_ref[0, chunk(n), :]
        kvf_ref[n] = mm((kc * tab_ref[0, 2]).T, vc)
        kvb_ref[n] = mm((kc * tab_ref[0, 4]).T, vc)
        return carry

    lax.fori_loop(0, n_chunks, kv_body, 0, unroll=4)

    def scan_fwd(n, state):
        kv = kvf_ref[n]
        kvf_ref[n] = state
        return tab_ref[0, 5] * state + kv

    lax.fori_loop(0, n_chunks, scan_fwd, jnp.zeros((d, d), F32))

    def scan_bwd(m, state):
        n = n_chunks - 1 - m
        kv = kvb_ref[n]
        kvb_ref[n] = state
        return tab_ref[0, 6] * state + kv

    lax.fori_loop(0, n_chunks, scan_bwd, jnp.zeros((d, d), F32))

    def out_body(n, carry):
        qc = qs_ref[chunk(n), :]
        kc = ks_ref[chunk(n), :]
        vc = v_ref[0, chunk(n), :]
        a = lax.dot_general(qc.astype(BF16), kc.astype(BF16), NT_DIMS, preferred_element_type=F32) * tab_ref[0, 0]
        y = mm(a, vc) + mm(qc * tab_ref[0, 1], kvf_ref[n]) + mm(qc * tab_ref[0, 3], kvb_ref[n])
        mu = jnp.mean(y, axis=-1, keepdims=True)
        yc = y - mu
        var = jnp.mean(yc * yc, axis=-1, keepdims=True)
        g = g_ref[0, chunk(n), :].astype(F32)
        o_ref[0, chunk(n), :] = yc * lax.rsqrt(var + GN_EPS) * (g * jax.nn.sigmoid(g))
        return carry

    lax.fori_loop(0, n_chunks, out_body, 0, unroll=4)


def _retention_mixer(proj, col0, batch, seq):
    cos2, sin2, tab = _retention_consts(seq)
    blk = (1, seq, RET_HEAD_DIM)
    h_ = RET_HEADS
    assert col0 % RET_HEAD_DIM == 0
    c0 = col0 // RET_HEAD_DIM
    return pl.pallas_call(
        _retention_kernel,
        grid=(batch, RET_HEADS),
        in_specs=[
            pl.BlockSpec(blk, lambda b, h: (b, 0, c0 + h)),
            pl.BlockSpec(blk, lambda b, h: (b, 0, c0 + h_ + h)),
            pl.BlockSpec(blk, lambda b, h: (b, 0, c0 + 2 * h_ + h)),
            pl.BlockSpec(blk, lambda b, h: (b, 0, c0 + 3 * h_ + h)),
            pl.BlockSpec((seq, RET_HEAD_DIM), lambda b, h: (0, 0)),
            pl.BlockSpec((seq, RET_HEAD_DIM), lambda b, h: (0, 0)),
            pl.BlockSpec((1, 7, RET_CHUNK, RET_CHUNK), lambda b, h: (h, 0, 0, 0)),
        ],
        out_specs=pl.BlockSpec(blk, lambda b, h: (b, 0, h)),
        out_shape=jax.ShapeDtypeStruct((batch, seq, RET_W), F32),
        scratch_shapes=[pltpu.VMEM((seq, RET_HEAD_DIM), F32), pltpu.VMEM((seq, RET_HEAD_DIM), F32),
                        pltpu.VMEM((seq // RET_CHUNK, RET_HEAD_DIM, RET_HEAD_DIM), F32),
                        pltpu.VMEM((seq // RET_CHUNK, RET_HEAD_DIM, RET_HEAD_DIM), F32)],
        compiler_params=_params("parallel", "arbitrary"),
        name="retention",
    )(proj, proj, proj, proj, jnp.asarray(cos2), jnp.asarray(sin2), jnp.asarray(tab))


def _cross_kernel(q_ref, k_ref, v_ref, o_ref):
    dh = CROSS_HEAD_DIM
    for h in range(CROSS_HEADS):
        sl = slice(h * dh, (h + 1) * dh)
        s = lax.dot_general(q_ref[:, sl], k_ref[:, sl], NT_DIMS, preferred_element_type=F32) * (dh ** -0.5)
        m = jnp.max(s, axis=-1, keepdims=True)
        p = jnp.exp(s - m)
        l = jnp.sum(p, axis=-1, keepdims=True)
        o = jnp.dot(p.astype(BF16), v_ref[:, sl], preferred_element_type=F32) / l
        o_ref[:, sl] = o.astype(o_ref.dtype)


def _cross_attention(q, kv, batch, seq, n_mem, tq):
    per_b = seq // tq
    return pl.pallas_call(
        _cross_kernel,
        grid=(batch, per_b),
        in_specs=[
            pl.BlockSpec((tq, D_MODEL), lambda b, i: (b * per_b + i, 0)),
            pl.BlockSpec((n_mem, D_MODEL), lambda b, i: (b, 0)),
            pl.BlockSpec((n_mem, D_MODEL), lambda b, i: (b, 1)),
        ],
        out_specs=pl.BlockSpec((tq, D_MODEL), lambda b, i: (b * per_b + i, 0)),
        out_shape=jax.ShapeDtypeStruct((batch * seq, D_MODEL), BF16),
        compiler_params=_params("parallel", "arbitrary"),
        name="cross_attention",
    )(q, kv, kv)


def _router_kernel(x_ref, g_ref, wr_ref, hm_ref, aff_ref):
    n_e = aff_ref.shape[1]
    h = _rms(x_ref[...], g_ref[...])
    h_hi = h.astype(BF16)
    hm_ref[...] = h_hi
    h_lo = (h - h_hi.astype(F32)).astype(BF16)
    w = wr_ref[...]
    w_hi = w.astype(BF16)
    w_lo = (w - w_hi.astype(F32)).astype(BF16)
    logits = (jnp.dot(h_hi, w_hi, preferred_element_type=F32) + jnp.dot(h_lo, w_hi, preferred_element_type=F32)
              + jnp.dot(h_hi, w_lo, preferred_element_type=F32)).T[:n_e]
    m = jnp.max(logits, axis=0, keepdims=True)
    e = jnp.exp(logits - m)
    aff_ref[0] = e / jnp.sum(e, axis=0, keepdims=True)


def _router(x2d, gain, w_router, batch, seq, tm):
    m = x2d.shape[0]
    per_b = seq // tm
    w_pad = jnp.zeros((D_MODEL, 128), F32).at[:, :N_EXPERTS].set(w_router)
    return pl.pallas_call(
        _router_kernel,
        grid=(m // tm,),
        in_specs=[
            pl.BlockSpec((tm, D_MODEL), lambda i: (i, 0)),
            pl.BlockSpec((1, D_MODEL), lambda i: (0, 0)),
            pl.BlockSpec((D_MODEL, 128), lambda i: (0, 0)),
        ],
        out_specs=[
            pl.BlockSpec((tm, D_MODEL), lambda i: (i, 0)),
            pl.BlockSpec((1, N_EXPERTS, tm), lambda i: (i // per_b, 0, i % per_b)),
        ],
        out_shape=[
            jax.ShapeDtypeStruct((m, D_MODEL), BF16),
            jax.ShapeDtypeStruct((batch, N_EXPERTS, seq), F32),
        ],
        compiler_params=_params("parallel"),
        name="moe_router",
    )(x2d, gain.reshape(1, D_MODEL), w_pad)


def _sort_descending(x):
    rows, n = x.shape
    lanes = 128
    n_chunks = n // lanes
    chunks = [x[:, c * lanes:(c + 1) * lanes] for c in range(n_chunks)]
    lane = lax.broadcasted_iota(I32, (rows, lanes), 1)
    k = 2
    while k <= n:
        j = k // 2
        while j >= 1:
            nxt = []
            for c, xc in enumerate(chunks):
                if j >= lanes:
                    lo = (c & (j // lanes)) == 0
                    partner = chunks[c ^ (j // lanes)]
                else:
                    lo = (lane & j) == 0
                    partner = jnp.where(lo, pltpu.roll(xc, lanes - j, 1), pltpu.roll(xc, j, 1))
                desc = ((c * lanes) & k) == 0 if k >= lanes else (lane & k) == 0
                hi, lw = jnp.maximum(xc, partner), jnp.minimum(xc, partner)
                if j >= lanes and k >= lanes:
                    nxt.append(hi if lo == desc else lw)
                else:
                    nxt.append(jnp.where(lo == desc, hi, lw))
            chunks = nxt
            j //= 2
        k *= 2
    return jnp.concatenate(chunks, axis=1)


def _topk_kernel(cap, n_e, aff_ref, slot_ref, wsel_ref, slot_t_ref, thr_ref):
    n_rows, n_tok = aff_ref.shape

    def thr_body(i, carry):
        rows8 = pl.ds(pl.multiple_of(i * 8, 8), 8)
        thr_ref[rows8, :] = _sort_descending(aff_ref[rows8, :])[:, cap - 1:cap]
        return carry

    lax.fori_loop(0, n_rows // 8, thr_body, 0)
    a = aff_ref[...]
    thr = thr_ref[...]
    gt = a > thr
    eq = a == thr
    need = cap - jnp.sum(gt.astype(I32), axis=1, keepdims=True)
    upper = (lax.broadcasted_iota(I32, (n_tok, n_tok), 0) < lax.broadcasted_iota(I32, (n_tok, n_tok), 1))
    upper = upper.astype(BF16)
    eq_rank = jnp.dot(eq.astype(BF16), upper, preferred_element_type=F32)
    sel = jnp.logical_or(gt, jnp.logical_and(eq, eq_rank < need.astype(F32)))
    rank = jnp.dot(sel.astype(BF16), upper, preferred_element_type=F32)
    slot = jnp.where(sel, rank, -1.0)
    slot_ref[...] = slot.astype(I32)
    wsel_ref[...] = jnp.where(sel, a, 0.0)
    pad = jnp.full((128 - n_e, n_tok), -1.0, F32)
    for b in range(n_rows // n_e):
        slot_t_ref[b] = jnp.concatenate([slot[b * n_e:(b + 1) * n_e], pad], axis=0).T.astype(I32)


def _topk_select(aff2d, cap, n_e):
    rows, n_tok = aff2d.shape
    spec = pl.BlockSpec((rows, n_tok), lambda i: (0, 0))
    spec_t = pl.BlockSpec((rows // n_e, n_tok, 128), lambda i: (0, 0, 0))
    return pl.pallas_call(
        functools.partial(_topk_kernel, cap, n_e),
        grid=(1,),
        in_specs=[spec],
        out_specs=[spec, spec, spec_t],
        out_shape=[jax.ShapeDtypeStruct((rows, n_tok), I32), jax.ShapeDtypeStruct((rows, n_tok), F32),
                   jax.ShapeDtypeStruct((rows // n_e, n_tok, 128), I32)],
        scratch_shapes=[pltpu.VMEM((rows, 1), F32)],
        compiler_params=_params("arbitrary"),
        name="moe_topk_select",
    )(aff2d)


def _gather_kernel(cap, slot_ref, wsel_ref, hm_ref, xe_ref, gs_ref):
    n_tok = hm_ref.shape[0]
    onehot = slot_ref[0] == lax.broadcasted_iota(I32, (cap, n_tok), 0)
    xe_ref[0] = jnp.dot(onehot.astype(BF16), hm_ref[...], preferred_element_type=F32).astype(BF16)
    gs_ref[0] = jnp.sum(jnp.where(onehot, wsel_ref[0], 0.0), axis=1, keepdims=True)


def _gather_tokens(slot, wsel, hm, batch, seq, cap):
    rows = batch * N_EXPERTS
    row_spec = pl.BlockSpec((1, 1, seq), lambda b, e: (b * N_EXPERTS + e, 0, 0))
    return pl.pallas_call(
        functools.partial(_gather_kernel, cap),
        grid=(batch, N_EXPERTS),
        in_specs=[row_spec, row_spec, pl.BlockSpec((seq, D_MODEL), lambda b, e: (b, 0))],
        out_specs=[
            pl.BlockSpec((1, cap, D_MODEL), lambda b, e: (e, b, 0)),
            pl.BlockSpec((1, cap, 1), lambda b, e: (e, b, 0)),
        ],
        out_shape=[
            jax.ShapeDtypeStruct((N_EXPERTS, batch * cap, D_MODEL), BF16),
            jax.ShapeDtypeStruct((N_EXPERTS, batch * cap, 1), F32),
        ],
        compiler_params=_params("parallel", "arbitrary"),
        name="moe_gather",
    )(slot.reshape(rows, 1, seq), wsel.reshape(rows, 1, seq), hm)


def _expert_kernel(n_f, xe_ref, wg_ref, wu_ref, wd_ref, gs_ref, ye_ref, mid_ref):
    s = pl.program_id(1)
    tf = wg_ref.shape[-1]

    @pl.when(s < n_f)
    def _():
        x = xe_ref[0]
        a = jnp.dot(x, wg_ref[...].astype(BF16), preferred_element_type=F32)
        u = jnp.dot(x, wu_ref[...].astype(BF16), preferred_element_type=F32)
        mid_ref[s] = (a * jax.nn.sigmoid(a) * u).astype(BF16)

    @pl.when(s >= n_f)
    def _():
        acc = jnp.dot(mid_ref[0], wd_ref[0:tf, :].astype(BF16), preferred_element_type=F32)
        for f in range(1, n_f):
            acc += jnp.dot(mid_ref[f], wd_ref[f * tf:(f + 1) * tf, :].astype(BF16), preferred_element_type=F32)
        ye_ref[0] = (acc * gs_ref[0]).astype(BF16)


def _expert_ffn(xe, gs, w_gate, w_up, w_down, layer, tf, tn):
    n_e, rows, d = xe.shape
    ff = w_gate.shape[-1]
    n_f = ff // tf
    n_n = d // tn
    return pl.pallas_call(
        functools.partial(_expert_kernel, n_f),
        grid=(n_e, n_f + n_n),
        in_specs=[
            pl.BlockSpec((1, rows, d), lambda e, s: (e, 0, 0)),
            pl.BlockSpec((None, None, d, tf), lambda e, s: (layer, e, 0, jnp.minimum(s, n_f - 1))),
            pl.BlockSpec((None, None, d, tf), lambda e, s: (layer, e, 0, jnp.minimum(s, n_f - 1))),
            pl.BlockSpec((None, None, ff, tn), lambda e, s: (layer, e, 0, jnp.maximum(s - n_f, 0))),
            pl.BlockSpec((1, rows, 1), lambda e, s: (e, 0, 0)),
        ],
        out_specs=pl.BlockSpec((1, rows, tn), lambda e, s: (e, 0, jnp.maximum(s - n_f, 0))),
        out_shape=jax.ShapeDtypeStruct((n_e, rows, d), BF16),
        scratch_shapes=[pltpu.VMEM((n_f, rows, tf), BF16)],
        compiler_params=_params("parallel", "arbitrary"),
        name="moe_expert_ffn",
    )(xe, w_gate, w_up, w_down, gs)


def _scatter_kernel(cap, slot_ref, ye_ref, x_ref, o_ref, onehot_ref):
    n_e = ye_ref.shape[0]
    n_tok = x_ref.shape[0]

    @pl.when(pl.program_id(1) == 0)
    def _():
        col = lax.broadcasted_iota(I32, (n_tok, cap), 1)
        slots = slot_ref[0]
        for e in range(n_e):
            onehot_ref[:, e * cap:(e + 1) * cap] = (slots[:, e:e + 1] == col).astype(BF16)

    ye = ye_ref[...].reshape(n_e * cap, ye_ref.shape[-1])
    o_ref[...] = x_ref[...] + jnp.dot(onehot_ref[...], ye, preferred_element_type=F32)


def _scatter_add(slot_t, ye, x2d, batch, seq, cap, tn):
    return pl.pallas_call(
        functools.partial(_scatter_kernel, cap),
        grid=(batch, D_MODEL // tn),
        in_specs=[
            pl.BlockSpec((1, seq, 128), lambda b, n: (b, 0, 0)),
            pl.BlockSpec((N_EXPERTS, cap, tn), lambda b, n: (0, b, n)),
            pl.BlockSpec((seq, tn), lambda b, n: (b, n)),
        ],
        out_specs=pl.BlockSpec((seq, tn), lambda b, n: (b, n)),
        out_shape=jax.ShapeDtypeStruct(x2d.shape, F32),
        scratch_shapes=[pltpu.VMEM((seq, N_EXPERTS * cap), BF16)],
        compiler_params=_params("parallel", "arbitrary"),
        name="moe_scatter_add",
    )(slot_t, ye, x2d)


def _expert_choice_ffn(x2d, gain, w_router, w_gate, w_up, w_down, layer, batch, seq):
    cap = EC_CAPACITY * seq // N_EXPERTS
    hm, aff = _router(x2d, gain, w_router, batch, seq, 512)
    slot, wsel, slot_t = _topk_select(aff.reshape(batch * N_EXPERTS, seq), cap, N_EXPERTS)
    xe, gs = _gather_tokens(slot, wsel, hm, batch, seq, cap)
    ye = _expert_ffn(xe, gs, w_gate, w_up, w_down, layer, 512, 512)
    return _scatter_add(slot_t, ye, x2d, batch, seq, cap, 512)


def _final_norm_kernel(x_ref, g_ref, o_ref):
    o_ref[...] = _rms(x_ref[...], g_ref[...])


def _final_norm(x2d, gain, tm):
    m, d = x2d.shape
    return pl.pallas_call(
        _final_norm_kernel,
        grid=(m // tm,),
        in_specs=[pl.BlockSpec((tm, d), lambda i: (i, 0)), pl.BlockSpec((1, d), lambda i: (0, 0))],
        out_specs=pl.BlockSpec((tm, d), lambda i: (i, 0)),
        out_shape=jax.ShapeDtypeStruct((m, d), F32),
        compiler_params=_params("parallel"),
        name="final_norm",
    )(x2d, gain.reshape(1, d))


def kernel(x, mem, norm_mix, w_in, na_rpb, hy_conv_w, hy_conv_b, hy_filt_w1, hy_filt_b1, hy_filt_w2, hy_filt_b2, hy_filt_w3, hy_sin_freq, hy_skip_d, branch_norm, w_out, norm_cross, mem_norm, w_cq, w_ckv, w_co, norm_moe, w_router, w_gate, w_up, w_down, final_norm):
    batch, seq, d = x.shape
    n_mem = mem.shape[1]
    depth = w_in.shape[0]
    m = batch * seq
    na_cols = 3 * NA_W
    hy_cols = 3 * HY_W
    rows = seq // GRID_W

    dft_fwd, dft_inv = _dft_tables(seq)
    x2d = x.reshape(m, d)
    mem2d = mem.reshape(batch * n_mem, d)

    w_in_b, w_out_b, w_cq_b, w_ckv_b, w_co_b = (w.astype(BF16) for w in (w_in, w_out, w_cq, w_ckv, w_co))
    p_in = w_in.shape[-1]

    for l in range(depth):
        proj = _norm_matmul(x2d, norm_mix[l], _whole(w_in_b, l), 1024, 768, BF16, "in_proj")
        proj = proj.reshape(batch, seq, p_in)

        y_na = _neighbourhood_attention(proj, _na_bias_table(na_rpb[l], rows), batch, seq)
        y_hy = _hyena_mixer(proj, na_cols, hy_conv_w[l], hy_conv_b[l], hy_filt_w1[l],
                            hy_filt_b1[l], hy_filt_w2[l], hy_filt_b2[l], hy_filt_w3[l], hy_sin_freq[l],
                            hy_skip_d[l], dft_fwd, dft_inv, batch, seq)
        y_ret = _retention_mixer(proj, na_cols + hy_cols, batch, seq)
        x2d = _branch_out_proj(y_na.reshape(m, NA_W), y_hy.reshape(m, HY_W), y_ret.reshape(m, RET_W),
                               branch_norm[l], _whole(w_out_b, l), x2d, 1024, 512)

        q = _norm_matmul(x2d, norm_cross[l], _whole(w_cq_b, l), 1024, 1024, BF16, "cross_q_proj")
        kv = _norm_matmul(mem2d, mem_norm, _whole(w_ckv_b, l), 512, 1024, BF16, "cross_kv_proj")
        o = _cross_attention(q, kv, batch, seq, n_mem, 512)
        x2d = _plain_matmul(o, _whole(w_co_b, l), 1024, 1024, F32, "cross_out_proj", res=x2d)

        x2d = _expert_choice_ffn(x2d, norm_moe[l], w_router[l], w_gate, w_up, w_down, l, batch, seq)

    return _final_norm(x2d, final_norm, 512).reshape(batch, seq, d)
```

```python
import functools
import math

import numpy as np
import jax
import jax.numpy as jnp
from jax import lax
from jax.experimental import pallas as pl
from jax.experimental.pallas import tpu as pltpu

F32 = jnp.float32
BF16 = jnp.bfloat16
I32 = jnp.int32

D_MODEL = 2048
GRID_W = 64
NA_HEAD_DIM = 64
NA_W = 768
NA_HEADS = 12
NA_KR = 8
NA_KC = 16
HY_W = 512
HY_ORDER = 2
HY_BANDS = 8
HY_POS_DIM = 17
HY_FILT_FF = 64
RET_HEAD_DIM = 128
RET_W = 768
RET_HEADS = 6
RET_CHUNK = 128
CROSS_HEADS = 4
CROSS_HEAD_DIM = 512
N_EXPERTS = 16
EXPERT_FF = 2048
EC_CAPACITY = 2
RMS_EPS = 1e-6
GN_EPS = 1e-5

MASK_VALUE = -1e30
VMEM_LIMIT_BYTES = 56 * 1024 * 1024

NT_DIMS = (((1,), (1,)), ((), ()))


def _params(*sem):
    return pltpu.CompilerParams(dimension_semantics=sem, vmem_limit_bytes=VMEM_LIMIT_BYTES)


def _rms(xf, g):
    return xf * lax.rsqrt(jnp.mean(xf * xf, axis=-1, keepdims=True) + RMS_EPS) * g


def _mm_kernel(prologue, n_pro, has_res, *refs):
    pro_refs = refs[:n_pro]
    w_ref = refs[n_pro]
    res_ref = refs[n_pro + 1] if has_res else None
    o_ref = refs[n_pro + 1 + has_res]
    h_ref = refs[n_pro + 2 + has_res]

    @pl.when(pl.program_id(1) == 0)
    def _():
        h_ref[...] = prologue(*pro_refs).astype(BF16)

    acc = jnp.dot(h_ref[...], w_ref[...], preferred_element_type=F32)
    if has_res:
        acc = acc + res_ref[...]
    o_ref[...] = acc.astype(o_ref.dtype)


def _fused_matmul(prologue, pro_args, pro_specs, wsel, res, m, tm, tn, out_dtype, name):
    w, layer, col0, n = wsel
    k = w.shape[1]
    cb0 = col0 // tn
    assert col0 % tn == 0 and n % tn == 0 and m % tm == 0
    has_res = res is not None
    in_specs = list(pro_specs) + [pl.BlockSpec((None, k, tn), lambda i, j: (layer, 0, cb0 + j))]
    args = list(pro_args) + [w]
    if has_res:
        in_specs.append(pl.BlockSpec((tm, tn), lambda i, j: (i, j)))
        args.append(res)
    return pl.pallas_call(
        functools.partial(_mm_kernel, prologue, len(pro_args), has_res),
        grid=(m // tm, n // tn),
        in_specs=in_specs,
        out_specs=pl.BlockSpec((tm, tn), lambda i, j: (i, j)),
        out_shape=jax.ShapeDtypeStruct((m, n), out_dtype),
        scratch_shapes=[pltpu.VMEM((tm, k), BF16)],
        compiler_params=_params("parallel", "arbitrary"),
        name=name,
    )(*args)


def _rms_prologue(x_ref, g_ref):
    return _rms(x_ref[...], g_ref[...])


def _whole(w, layer):
    return (w, layer, 0, w.shape[-1])


def _norm_matmul(x2d, gain, wsel, tm, tn, out_dtype, name, res=None):
    m, k = x2d.shape
    specs = [pl.BlockSpec((tm, k), lambda i, j: (i, 0)), pl.BlockSpec((1, k), lambda i, j: (0, 0))]
    return _fused_matmul(_rms_prologue, [x2d, gain.reshape(1, k)], specs, wsel, res, m, tm, tn, out_dtype, name)


def _identity_prologue(a_ref):
    return a_ref[...]


def _plain_matmul(a, wsel, tm, tn, out_dtype, name, res=None):
    m, k = a.shape
    specs = [pl.BlockSpec((tm, k), lambda i, j: (i, 0))]
    return _fused_matmul(_identity_prologue, [a], specs, wsel, res, m, tm, tn, out_dtype, name)


def _branch_prologue(na_ref, hy_ref, ret_ref, g_ref):
    def nrm(p):
        return p * lax.rsqrt(jnp.mean(p * p, axis=-1, keepdims=True) + RMS_EPS)

    y = jnp.concatenate([nrm(na_ref[...]), nrm(hy_ref[...]), nrm(ret_ref[...])], axis=-1)
    return y * g_ref[...]


def _branch_out_proj(y_na, y_hy, y_ret, gain, wsel, res, tm, tn):
    m = y_na.shape[0]
    specs = [
        pl.BlockSpec((tm, NA_W), lambda i, j: (i, 0)),
        pl.BlockSpec((tm, HY_W), lambda i, j: (i, 0)),
        pl.BlockSpec((tm, RET_W), lambda i, j: (i, 0)),
        pl.BlockSpec((1, D_MODEL), lambda i, j: (0, 0)),
    ]
    return _fused_matmul(_branch_prologue, [y_na, y_hy, y_ret, gain.reshape(1, D_MODEL)], specs, wsel, res,
                         m, tm, tn, F32, "branch_out_proj")


NA_PAIR = 2
NA_BAND = NA_KR + NA_PAIR


def _na_band_base(r, rows):
    return np.clip(r - NA_KR // 2, 0, rows - NA_BAND)


@functools.lru_cache(maxsize=None)
def _na_variants(rows):
    assert rows % NA_PAIR == 0 and rows >= NA_BAND + 2
    n_var = NA_KR // 2 + 1
    dr = np.full((n_var, NA_BAND, NA_PAIR), -2, np.int64)
    for r in range(0, rows, NA_PAIR):
        base = int(_na_band_base(r, rows))
        v = (r - base) // 2
        for j in range(NA_PAIR):
            rs = int(np.clip(r + j - NA_KR // 2, 0, rows - NA_KR))
            for i in range(NA_BAND):
                val = base + i - (r + j) + (NA_KR - 1) if rs <= base + i < rs + NA_KR else -1
                assert dr[v, i, j] in (-2, val)
                dr[v, i, j] = val
    assert (dr > -2).all()
    return dr


def _na_bias_table(rpb, rows):
    c = np.arange(GRID_W)
    col_start = np.clip(c - NA_KC // 2, 0, GRID_W - NA_KC)
    col_in = (c[None, :] >= col_start[:, None]) & (c[None, :] < col_start[:, None] + NA_KC)
    dc = np.clip(c[None, :] - c[:, None] + (NA_KC - 1), 0, 2 * NA_KC - 2)
    onehot = (dc.reshape(-1)[None, :] == np.arange(2 * NA_KC - 1)[:, None]).astype(np.float32)
    cols = jnp.einsum("hab,bn->han", rpb.astype(F32), jnp.asarray(onehot), precision=lax.Precision.HIGHEST)
    cols = cols.reshape(NA_HEADS, 2 * NA_KR - 1, GRID_W, GRID_W)
    cols = jnp.where(col_in[None, None], cols, MASK_VALUE).transpose(0, 1, 3, 2)
    masked = jnp.full((NA_HEADS, GRID_W, GRID_W), MASK_VALUE, F32)
    dr = _na_variants(rows)
    variants = []
    for v in range(dr.shape[0]):
        band = [jnp.concatenate([cols[:, dr[v, i, j]] if dr[v, i, j] >= 0 else masked for j in range(NA_PAIR)],
                                axis=-1) for i in range(NA_BAND)]
        variants.append(jnp.concatenate(band, axis=1))
    return jnp.stack(variants, axis=1)


def _na_kernel(rows, q_ref, k_ref, v_ref, bias_ref, o_ref, vt_ref):
    dh = NA_HEAD_DIM
    nq = NA_PAIR * GRID_W
    nk = NA_BAND * GRID_W
    n_chunk = nk // 128
    lane = lax.broadcasted_iota(I32, (nq, 2 * dh), 1)

    for ch in range(vt_ref.shape[0]):
        vt_ref[ch] = v_ref[0, ch * 128:(ch + 1) * 128, :].astype(F32).T.astype(BF16)

    def body(p, carry):
        r = p * NA_PAIR
        base = jnp.clip(r - NA_KR // 2, 0, rows - NA_BAND)
        variant = (r - base) // 2
        q = q_ref[0, pl.ds(pl.multiple_of(r * GRID_W, nq), nq), :] * (dh ** -0.5)
        kb = k_ref[0, pl.ds(pl.multiple_of(base * GRID_W, 128), nk), :]
        c0 = base // 2
        vt = jnp.concatenate([vt_ref[c0 + i] for i in range(n_chunk)], axis=1)
        zero = jnp.zeros_like(q)
        q2 = jnp.concatenate([jnp.where(lane < dh, q, zero), jnp.where(lane >= dh, q, zero)], axis=0)
        st2 = lax.dot_general(kb, q2, NT_DIMS, preferred_element_type=F32)
        outs = []
        for hh in range(2):
            st = st2[:, hh * nq:(hh + 1) * nq] + bias_ref[hh, variant]
            m = jnp.max(st, axis=0, keepdims=True)
            pt = jnp.exp(st - m)
            l = jnp.sum(pt, axis=0, keepdims=True)
            ot = jnp.dot(vt[hh * dh:(hh + 1) * dh, :], pt.astype(BF16), preferred_element_type=F32)
            outs.append(ot * (1.0 / l))
        o_ref[0, pl.ds(pl.multiple_of(r * GRID_W, nq), nq), :] = jnp.concatenate(outs, axis=0).T
        return carry

    lax.fori_loop(0, rows // NA_PAIR, body, 0, unroll=2)


def _neighbourhood_attention(proj_na, bias_tbl, batch, seq):
    rows = seq // GRID_W
    n_pairs = NA_HEADS // 2
    blk = (1, seq, 2 * NA_HEAD_DIM)
    n_var, nk, nq = bias_tbl.shape[1:]
    return pl.pallas_call(
        functools.partial(_na_kernel, rows),
        grid=(batch, n_pairs),
        in_specs=[
            pl.BlockSpec(blk, lambda b, h: (b, 0, h)),
            pl.BlockSpec(blk, lambda b, h: (b, 0, n_pairs + h)),
            pl.BlockSpec(blk, lambda b, h: (b, 0, 2 * n_pairs + h)),
            pl.BlockSpec((2, n_var, nk, nq), lambda b, h: (h, 0, 0, 0)),
        ],
        out_specs=pl.BlockSpec(blk, lambda b, h: (b, 0, h)),
        out_shape=jax.ShapeDtypeStruct((batch, seq, NA_W), F32),
        scratch_shapes=[pltpu.VMEM((seq // 128, 2 * NA_HEAD_DIM, 128), BF16)],
        compiler_params=_params("parallel", "arbitrary"),
        name="neighbourhood_attention",
    )(proj_na, proj_na, proj_na, bias_tbl)


@functools.lru_cache(maxsize=None)
def _dft_factors(seq):
    n = 2 * seq
    t = np.arange(seq, dtype=np.int64)
    f1 = np.arange(seq // 64, dtype=np.int64)
    f0 = np.arange(64, dtype=np.int64)
    a = 2.0 * np.pi * ((64 * f1[:, None] * t[None, :]) % n).astype(np.float64) / n
    b = 2.0 * np.pi * ((f0[:, None] * t[None, :]) % n).astype(np.float64) / n
    ny = np.where(t % 2 == 0, 1.0, -1.0)
    return tuple(np.asarray(v, np.float32) for v in (np.cos(a), np.sin(a), np.cos(b), np.sin(b), ny))


def _dft_tables(seq):
    ca, sa, cb, sb, ny = (jnp.asarray(v) for v in _dft_factors(seq))
    cos = (ca[:, None, :] * cb[None] - sa[:, None, :] * sb[None]).reshape(seq, seq)
    sin = (sa[:, None, :] * cb[None] + ca[:, None, :] * sb[None]).reshape(seq, seq)
    sin = jnp.concatenate([ny[None, :], sin[1:]], axis=0)
    fwd = jnp.stack([cos, sin]).astype(BF16)
    inv = jnp.concatenate([cos.T, sin.T], axis=1).astype(BF16)
    return fwd, inv


@functools.lru_cache(maxsize=None)
def _hyena_consts(seq):
    t = np.arange(seq, dtype=np.float64)
    t01 = t / (seq - 1)
    bands = np.linspace(1e-4, HY_BANDS - 1, HY_BANDS)
    ang = (2.0 * math.pi) * (t[:, None] / seq) * bands[None, :]
    feats = np.concatenate([t01[:, None], np.cos(ang), -np.sin(ang)], axis=-1)
    feats_p = np.zeros((seq, 128), np.float32)
    feats_p[:, :HY_POS_DIM] = feats
    min_decay = math.log(1e-2) / 1.5
    max_decay = math.log(1e-2) / 0.3
    deltas = np.abs(np.linspace(min_decay, max_decay, HY_W))
    window = np.exp(-t01[:, None] * deltas[None, :]).astype(np.float32)
    return feats_p, window


def _filter_kernel(feats_ref, w1_ref, b1_ref, w2_ref, b2_ref, freq_ref, w3_ref, win_ref, o_ref, hid_ref):
    hp = lax.Precision.HIGHEST

    @pl.when(pl.program_id(0) == 0)
    def _():
        f = freq_ref[...]
        h1 = jnp.sin(f * (jnp.dot(feats_ref[...], w1_ref[...], precision=hp, preferred_element_type=F32)
                          + b1_ref[...]))
        hid_ref[...] = jnp.sin(f * (jnp.dot(h1, w2_ref[...], precision=hp, preferred_element_type=F32)
                                    + b2_ref[...]))

    h = jnp.dot(hid_ref[...], w3_ref[...], precision=hp, preferred_element_type=F32) * win_ref[...]
    row = lax.broadcasted_iota(I32, h.shape, 0)
    drop = jnp.logical_and(row == 0, pl.program_id(0) >= HY_ORDER)
    o_ref[...] = jnp.where(drop, 0.0, h).astype(BF16)


def _hyena_filters_time(w1, b1, w2, b2, w3, freq, seq):
    feats, window = _hyena_consts(seq)
    w1p = jnp.zeros((128, HY_FILT_FF), F32).at[:HY_POS_DIM].set(w1)
    w3r = w3.reshape(HY_FILT_FF, HY_ORDER, 2, HY_W).transpose(0, 2, 1, 3).reshape(HY_FILT_FF, 4 * HY_W)
    full = lambda shape: pl.BlockSpec(shape, lambda c: (0,) * len(shape))
    return pl.pallas_call(
        _filter_kernel,
        grid=(4,),
        in_specs=[
            full((seq, 128)), full((128, HY_FILT_FF)), full((1, HY_FILT_FF)),
            full((HY_FILT_FF, HY_FILT_FF)), full((1, HY_FILT_FF)), full((1, HY_FILT_FF)),
            pl.BlockSpec((HY_FILT_FF, HY_W), lambda c: (0, c)),
            full((seq, HY_W)),
        ],
        out_specs=pl.BlockSpec((seq, HY_W), lambda c: (0, c)),
        out_shape=jax.ShapeDtypeStruct((seq, 4 * HY_W), BF16),
        scratch_shapes=[pltpu.VMEM((seq, HY_FILT_FF), F32)],
        compiler_params=_params("arbitrary"),
        name="hyena_filter_mlp",
    )(jnp.asarray(feats), w1p, b1.reshape(1, -1), w2, b2.reshape(1, -1), freq.reshape(1, -1), w3r,
      jnp.asarray(window))


def _filter_dft_kernel(f_ref, fwd_ref, bwd_ref, o_ref):
    fwd = fwd_ref[...]
    bwd = bwd_ref[...]
    uc = jnp.dot(f_ref[0], fwd, preferred_element_type=F32)
    vc = jnp.dot(f_ref[0], bwd, preferred_element_type=F32)
    us = jnp.dot(f_ref[1], fwd, preferred_element_type=F32)
    vs = jnp.dot(f_ref[1], bwd, preferred_element_type=F32)
    o_ref[0] = uc + vc
    row = lax.broadcasted_iota(I32, us.shape, 0)
    nyq = jnp.logical_and(row == 0, pl.program_id(0) == 0)
    o_ref[1] = jnp.where(nyq, us + vs, us - vs)


def _filter_dft(dft_fwd, filt_t, seq, fb):
    return pl.pallas_call(
        _filter_dft_kernel,
        grid=(seq // fb, HY_ORDER),
        in_specs=[
            pl.BlockSpec((2, fb, seq), lambda f, o: (0, f, 0)),
            pl.BlockSpec((seq, HY_W), lambda f, o: (0, o)),
            pl.BlockSpec((seq, HY_W), lambda f, o: (0, HY_ORDER + o)),
        ],
        out_specs=pl.BlockSpec((2, fb, HY_W), lambda f, o: (0, f, o)),
        out_shape=jax.ShapeDtypeStruct((2, seq, HY_ORDER * HY_W), F32),
        compiler_params=_params("parallel", "arbitrary"),
        name="hyena_filter_dft",
    )(dft_fwd, filt_t, filt_t)


def _short_conv_kernel(p_ref, w_ref, b_ref, o_ref):
    p = p_ref[0].astype(F32)
    seq = p.shape[0]
    row = lax.broadcasted_iota(I32, p.shape, 0)
    prev = jnp.where(row == 0, 0.0, pltpu.roll(p, 1, 0))
    nxt = jnp.where(row == seq - 1, 0.0, pltpu.roll(p, seq - 1, 0))
    w = w_ref[...]
    o_ref[0] = prev * w[0:1] + p * w[1:2] + nxt * w[2:3] + b_ref[...]


def _short_conv(proj, col0, conv_w, conv_b, batch, seq, tc):
    n_cols = conv_w.shape[-1]
    assert col0 % tc == 0 and n_cols % tc == 0
    cb0 = col0 // tc
    return pl.pallas_call(
        _short_conv_kernel,
        grid=(batch, n_cols // tc),
        in_specs=[
            pl.BlockSpec((1, seq, tc), lambda b, c: (b, 0, cb0 + c)),
            pl.BlockSpec((3, tc), lambda b, c: (0, c)),
            pl.BlockSpec((1, tc), lambda b, c: (0, c)),
        ],
        out_specs=pl.BlockSpec((1, seq, tc), lambda b, c: (b, 0, c)),
        out_shape=jax.ShapeDtypeStruct((batch, seq, n_cols), F32),
        compiler_params=_params("parallel", "arbitrary"),
        name="hyena_short_conv",
    )(proj, conv_w, conv_b.reshape(1, -1))


def _spectrum_kernel(n_fft, f_ref, z_ref, k_ref, o_ref):
    z = z_ref[0].astype(BF16)
    xr = jnp.dot(f_ref[0], z, preferred_element_type=F32)
    xs = jnp.dot(f_ref[1], z, preferred_element_type=F32)
    kr = k_ref[0]
    ks = k_ref[1]
    row = lax.broadcasted_iota(I32, xr.shape, 0)
    edge = jnp.logical_and(row == 0, pl.program_id(0) == 0)
    yr = jnp.where(edge, xr * kr * (1.0 / n_fft), (xr * kr - xs * ks) * (2.0 / n_fft))
    ys = jnp.where(edge, xs * ks * (1.0 / n_fft), (xr * ks + xs * kr) * (2.0 / n_fft))
    o_ref[0, 0] = yr.astype(BF16)
    o_ref[0, 1] = ys.astype(BF16)


def _spectrum_product(dft_fwd, z_arr, z_col, kfreq, order, batch, seq, fb):
    return pl.pallas_call(
        functools.partial(_spectrum_kernel, 2 * seq),
        grid=(seq // fb, batch),
        in_specs=[
            pl.BlockSpec((2, fb, seq), lambda f, b: (0, f, 0)),
            pl.BlockSpec((1, seq, HY_W), lambda f, b: (b, 0, z_col)),
            pl.BlockSpec((2, fb, HY_W), lambda f, b: (0, f, order)),
        ],
        out_specs=pl.BlockSpec((1, 2, fb, HY_W), lambda f, b: (b, 0, f, 0)),
        out_shape=jax.ShapeDtypeStruct((batch, 2, seq, HY_W), BF16),
        compiler_params=_params("parallel", "arbitrary"),
        name="hyena_spectrum",
    )(dft_fwd, z_arr, kfreq)


def _inverse_kernel(ft_ref, y_ref, gate_ref, z_ref, d_ref, o_ref):
    conv = jnp.dot(ft_ref[...], y_ref[0], preferred_element_type=F32)
    o_ref[0] = gate_ref[0] * (conv + d_ref[...] * z_ref[0])


def _inverse_gate(dft_inv, y, gate_arr, gate_col, z_arr, z_col, skip_row, batch, seq, tb):
    y2 = y.reshape(batch, 2 * seq, HY_W)
    return pl.pallas_call(
        _inverse_kernel,
        grid=(seq // tb, batch),
        in_specs=[
            pl.BlockSpec((tb, 2 * seq), lambda t, b: (t, 0)),
            pl.BlockSpec((1, 2 * seq, HY_W), lambda t, b: (b, 0, 0)),
            pl.BlockSpec((1, tb, HY_W), lambda t, b: (b, t, gate_col)),
            pl.BlockSpec((1, tb, HY_W), lambda t, b: (b, t, z_col)),
            pl.BlockSpec((1, HY_W), lambda t, b: (0, 0)),
        ],
        out_specs=pl.BlockSpec((1, tb, HY_W), lambda t, b: (b, t, 0)),
        out_shape=jax.ShapeDtypeStruct((batch, seq, HY_W), F32),
        compiler_params=_params("parallel", "arbitrary"),
        name="hyena_inverse_gate",
    )(dft_inv, y2, gate_arr, z_arr, skip_row.reshape(1, HY_W))


def _hyena_mixer(proj, col0, conv_w, conv_b, w1, b1, w2, b2, w3, freq, skip_d, dft_fwd, dft_inv, batch, seq):
    s = _short_conv(proj, col0, conv_w, conv_b, batch, seq, 256)
    filt_t = _hyena_filters_time(w1, b1, w2, b2, w3, freq, seq)
    kfreq = _filter_dft(dft_fwd, filt_t, seq, 512)
    z_arr, z_col = s, 2
    for o in range(HY_ORDER):
        y = _spectrum_product(dft_fwd, z_arr, z_col, kfreq, o, batch, seq, 512)
        z_arr = _inverse_gate(dft_inv, y, s, o, z_arr, z_col, skip_d[o], batch, seq, 512)
        z_col = 0
    return z_arr


@functools.lru_cache(maxsize=None)
def _retention_consts(seq):
    c = RET_CHUNK
    half = RET_HEAD_DIM // 2
    inv = 1.0 / (10000.0 ** np.linspace(0.0, 1.0, half))
    ang = np.arange(seq, dtype=np.float64)[:, None] * inv[None, :]
    cos2 = np.concatenate([np.cos(ang), np.cos(ang)], axis=-1).astype(np.float32)
    sin2 = np.concatenate([-np.sin(ang), np.sin(ang)], axis=-1).astype(np.float32)
    hidx = np.arange(RET_HEADS, dtype=np.float64)
    lg_f = np.log1p(-np.exp2(-5.0 - hidx))[:, None, None]
    lg_b = np.log1p(-np.exp2(-5.5 - hidx))[:, None, None]
    i = np.arange(c, dtype=np.float64)
    diff = i[:, None] - i[None, :]
    ones = np.ones((1, c, c))
    dec = np.where(diff >= 0, np.exp(lg_f * np.maximum(diff, 0.0)), np.exp(lg_b * np.maximum(-diff, 0.0)))
    rowv = lambda v: v[:, :, None] * ones
    tab = np.stack([
        dec,
        rowv(np.exp(lg_f[:, :, 0] * (i + 1.0)[None, :])),
        rowv(np.exp(lg_f[:, :, 0] * (c - 1.0 - i)[None, :])),
        rowv(np.exp(lg_b[:, :, 0] * (c - i)[None, :])),
        rowv(np.exp(lg_b[:, :, 0] * i[None, :])),
        np.exp(lg_f * c) * ones,
        np.exp(lg_b * c) * ones,
    ], axis=1).astype(np.float32)
    return cos2, sin2, tab


def _retention_kernel(q_ref, k_ref, v_ref, g_ref, cos_ref, sin_ref, tab_ref, o_ref, qs_ref, ks_ref, kvf_ref,
                      kvb_ref):
    c = RET_CHUNK
    d = RET_HEAD_DIM
    seq = q_ref.shape[1]
    n_chunks = seq // c
    cos = cos_ref[...]
    sin = sin_ref[...]
    q = q_ref[0].astype(F32)
    k = k_ref[0].astype(F32)
    qs_ref[...] = (q * cos + pltpu.roll(q, d // 2, 1) * sin) * (d ** -0.5)
    ks_ref[...] = k * cos + pltpu.roll(k, d // 2, 1) * sin

    def mm(a, b):
        return jnp.dot(a.astype(BF16), b.astype(BF16), preferred_element_type=F32)

    def chunk(n):
        return pl.ds(pl.multiple_of(n * c, c), c)

    def kv_body(n, carry):
        kc = ks_ref[chunk(n), :]
        vc = v_ref[0, chunk(n), :]
        kvf_ref[n] = mm((kc * tab_ref[0, 2]).T, vc)
        kvb_ref[n] = mm((kc * tab_ref[0, 4]).T, vc)
        return carry

    lax.fori_loop(0, n_chunks, kv_body, 0, unroll=4)

    def scan_fwd(n, state):
        kv = kvf_ref[n]
        kvf_ref[n] = state
        return tab_ref[0, 5] * state + kv

    lax.fori_loop(0, n_chunks, scan_fwd, jnp.zeros((d, d), F32))

    def scan_bwd(m, state):
        n = n_chunks - 1 - m
        kv = kvb_ref[n]
        kvb_ref[n] = state
        return tab_ref[0, 6] * state + kv

    lax.fori_loop(0, n_chunks, scan_bwd, jnp.zeros((d, d), F32))

    def out_body(n, carry):
        qc = qs_ref[chunk(n), :]
        kc = ks_ref[chunk(n), :]
        vc = v_ref[0, chunk(n), :]
        a = lax.dot_general(qc.astype(BF16), kc.astype(BF16), NT_DIMS, preferred_element_type=F32) * tab_ref[0, 0]
        y = mm(a, vc) + mm(qc * tab_ref[0, 1], kvf_ref[n]) + mm(qc * tab_ref[0, 3], kvb_ref[n])
        mu = jnp.mean(y, axis=-1, keepdims=True)
        yc = y - mu
        var = jnp.mean(yc * yc, axis=-1, keepdims=True)
        g = g_ref[0, chunk(n), :].astype(F32)
        o_ref[0, chunk(n), :] = yc * lax.rsqrt(var + GN_EPS) * (g * jax.nn.sigmoid(g))
        return carry

    lax.fori_loop(0, n_chunks, out_body, 0, unroll=4)


def _retention_mixer(proj, col0, batch, seq):
    cos2, sin2, tab = _retention_consts(seq)
    blk = (1, seq, RET_HEAD_DIM)
    h_ = RET_HEADS
    assert col0 % RET_HEAD_DIM == 0
    c0 = col0 // RET_HEAD_DIM
    return pl.pallas_call(
        _retention_kernel,
        grid=(batch, RET_HEADS),
        in_specs=[
            pl.BlockSpec(blk, lambda b, h: (b, 0, c0 + h)),
            pl.BlockSpec(blk, lambda b, h: (b, 0, c0 + h_ + h)),
            pl.BlockSpec(blk, lambda b, h: (b, 0, c0 + 2 * h_ + h)),
            pl.BlockSpec(blk, lambda b, h: (b, 0, c0 + 3 * h_ + h)),
            pl.BlockSpec((seq, RET_HEAD_DIM), lambda b, h: (0, 0)),
            pl.BlockSpec((seq, RET_HEAD_DIM), lambda b, h: (0, 0)),
            pl.BlockSpec((1, 7, RET_CHUNK, RET_CHUNK), lambda b, h: (h, 0, 0, 0)),
        ],
        out_specs=pl.BlockSpec(blk, lambda b, h: (b, 0, h)),
        out_shape=jax.ShapeDtypeStruct((batch, seq, RET_W), F32),
        scratch_shapes=[pltpu.VMEM((seq, RET_HEAD_DIM), F32), pltpu.VMEM((seq, RET_HEAD_DIM), F32),
                        pltpu.VMEM((seq // RET_CHUNK, RET_HEAD_DIM, RET_HEAD_DIM), F32),
                        pltpu.VMEM((seq // RET_CHUNK, RET_HEAD_DIM, RET_HEAD_DIM), F32)],
        compiler_params=_params("parallel", "arbitrary"),
        name="retention",
    )(proj, proj, proj, proj, jnp.asarray(cos2), jnp.asarray(sin2), jnp.asarray(tab))


def _cross_kernel(q_ref, k_ref, v_ref, o_ref):
    dh = CROSS_HEAD_DIM
    for h in range(CROSS_HEADS):
        sl = slice(h * dh, (h + 1) * dh)
        s = lax.dot_general(q_ref[:, sl], k_ref[:, sl], NT_DIMS, preferred_element_type=F32) * (dh ** -0.5)
        m = jnp.max(s, axis=-1, keepdims=True)
        p = jnp.exp(s - m)
        l = jnp.sum(p, axis=-1, keepdims=True)
        o = jnp.dot(p.astype(BF16), v_ref[:, sl], preferred_element_type=F32) * (1.0 / l)
        o_ref[:, sl] = o.astype(o_ref.dtype)


def _cross_attention(q, kv, batch, seq, n_mem, tq):
    per_b = seq // tq
    return pl.pallas_call(
        _cross_kernel,
        grid=(batch, per_b),
        in_specs=[
            pl.BlockSpec((tq, D_MODEL), lambda b, i: (b * per_b + i, 0)),
            pl.BlockSpec((n_mem, D_MODEL), lambda b, i: (b, 0)),
            pl.BlockSpec((n_mem, D_MODEL), lambda b, i: (b, 1)),
        ],
        out_specs=pl.BlockSpec((tq, D_MODEL), lambda b, i: (b * per_b + i, 0)),
        out_shape=jax.ShapeDtypeStruct((batch * seq, D_MODEL), BF16),
        compiler_params=_params("parallel", "arbitrary"),
        name="cross_attention",
    )(q, kv, kv)


def _router_kernel(x_ref, g_ref, wr_ref, hm_ref, aff_ref):
    n_e = aff_ref.shape[1]
    h = _rms(x_ref[...], g_ref[...])
    h_hi = h.astype(BF16)
    hm_ref[...] = h_hi
    h_lo = (h - h_hi.astype(F32)).astype(BF16)
    w = wr_ref[...]
    w_hi = w.astype(BF16)
    w_lo = (w - w_hi.astype(F32)).astype(BF16)
    logits = (jnp.dot(h_hi, w_hi, preferred_element_type=F32) + jnp.dot(h_lo, w_hi, preferred_element_type=F32)
              + jnp.dot(h_hi, w_lo, preferred_element_type=F32)).T[:n_e]
    m = jnp.max(logits, axis=0, keepdims=True)
    e = jnp.exp(logits - m)
    aff_ref[0] = e / jnp.sum(e, axis=0, keepdims=True)


def _router(x2d, gain, w_router, batch, seq, tm):
    m = x2d.shape[0]
    per_b = seq // tm
    w_pad = jnp.zeros((D_MODEL, 128), F32).at[:, :N_EXPERTS].set(w_router)
    return pl.pallas_call(
        _router_kernel,
        grid=(m // tm,),
        in_specs=[
            pl.BlockSpec((tm, D_MODEL), lambda i: (i, 0)),
            pl.BlockSpec((1, D_MODEL), lambda i: (0, 0)),
            pl.BlockSpec((D_MODEL, 128), lambda i: (0, 0)),
        ],
        out_specs=[
            pl.BlockSpec((tm, D_MODEL), lambda i: (i, 0)),
            pl.BlockSpec((1, N_EXPERTS, tm), lambda i: (i // per_b, 0, i % per_b)),
        ],
        out_shape=[
            jax.ShapeDtypeStruct((m, D_MODEL), BF16),
            jax.ShapeDtypeStruct((batch, N_EXPERTS, seq), F32),
        ],
        compiler_params=_params("parallel"),
        name="moe_router",
    )(x2d, gain.reshape(1, D_MODEL), w_pad)


def _sort_descending(x):
    rows, n = x.shape
    lanes = 128
    n_chunks = n // lanes
    chunks = [x[:, c * lanes:(c + 1) * lanes] for c in range(n_chunks)]
    lane = lax.broadcasted_iota(I32, (rows, lanes), 1)
    k = 2
    while k <= n:
        j = k // 2
        while j >= 1:
            nxt = []
            for c, xc in enumerate(chunks):
                if j >= lanes:
                    lo = (c & (j // lanes)) == 0
                    partner = chunks[c ^ (j // lanes)]
                else:
                    lo = (lane & j) == 0
                    partner = jnp.where(lo, pltpu.roll(xc, lanes - j, 1), pltpu.roll(xc, j, 1))
                desc = ((c * lanes) & k) == 0 if k >= lanes else (lane & k) == 0
                hi, lw = jnp.maximum(xc, partner), jnp.minimum(xc, partner)
                if j >= lanes and k >= lanes:
                    nxt.append(hi if lo == desc else lw)
                else:
                    nxt.append(jnp.where(lo == desc, hi, lw))
            chunks = nxt
            j //= 2
        k *= 2
    return jnp.concatenate(chunks, axis=1)


def _topk_kernel(cap, n_e, aff_ref, slot_ref, wsel_ref, slot_t_ref, thr_ref):
    n_rows, n_tok = aff_ref.shape

    def thr_body(i, carry):
        rows8 = pl.ds(pl.multiple_of(i * 8, 8), 8)
        thr_ref[rows8, :] = _sort_descending(aff_ref[rows8, :])[:, cap - 1:cap]
        return carry

    lax.fori_loop(0, n_rows // 8, thr_body, 0)
    a = aff_ref[...]
    thr = thr_ref[...]
    gt = a > thr
    eq = a == thr
    need = cap - jnp.sum(gt.astype(I32), axis=1, keepdims=True)
    upper = (lax.broadcasted_iota(I32, (n_tok, n_tok), 0) < lax.broadcasted_iota(I32, (n_tok, n_tok), 1))
    upper = upper.astype(BF16)
    eq_rank = jnp.dot(eq.astype(BF16), upper, preferred_element_type=F32)
    sel = jnp.logical_or(gt, jnp.logical_and(eq, eq_rank < need.astype(F32)))
    rank = jnp.dot(sel.astype(BF16), upper, preferred_element_type=F32)
    slot = jnp.where(sel, rank, -1.0)
    slot_ref[...] = slot.astype(I32)
    wsel_ref[...] = jnp.where(sel, a, 0.0)
    pad = jnp.full((128 - n_e, n_tok), -1.0, F32)
    for b in range(n_rows // n_e):
        slot_t_ref[b] = jnp.concatenate([slot[b * n_e:(b + 1) * n_e], pad], axis=0).T.astype(I32)


def _topk_select(aff2d, cap, n_e):
    rows, n_tok = aff2d.shape
    spec = pl.BlockSpec((rows, n_tok), lambda i: (0, 0))
    spec_t = pl.BlockSpec((rows // n_e, n_tok, 128), lambda i: (0, 0, 0))
    return pl.pallas_call(
        functools.partial(_topk_kernel, cap, n_e),
        grid=(1,),
        in_specs=[spec],
        out_specs=[spec, spec, spec_t],
        out_shape=[jax.ShapeDtypeStruct((rows, n_tok), I32), jax.ShapeDtypeStruct((rows, n_tok), F32),
                   jax.ShapeDtypeStruct((rows // n_e, n_tok, 128), I32)],
        scratch_shapes=[pltpu.VMEM((rows, 1), F32)],
        compiler_params=_params("arbitrary"),
        name="moe_topk_select",
    )(aff2d)


def _gather_kernel(cap, slot_ref, wsel_ref, hm_ref, xe_ref, gs_ref):
    n_tok = hm_ref.shape[0]
    onehot = slot_ref[0] == lax.broadcasted_iota(I32, (cap, n_tok), 0)
    xe_ref[0] = jnp.dot(onehot.astype(BF16), hm_ref[...], preferred_element_type=F32).astype(BF16)
    gs_ref[0] = jnp.sum(jnp.where(onehot, wsel_ref[0], 0.0), axis=1, keepdims=True)


def _gather_tokens(slot, wsel, hm, batch, seq, cap):
    rows = batch * N_EXPERTS
    row_spec = pl.BlockSpec((1, 1, seq), lambda b, e: (b * N_EXPERTS + e, 0, 0))
    return pl.pallas_call(
        functools.partial(_gather_kernel, cap),
        grid=(batch, N_EXPERTS),
        in_specs=[row_spec, row_spec, pl.BlockSpec((seq, D_MODEL), lambda b, e: (b, 0))],
        out_specs=[
            pl.BlockSpec((1, cap, D_MODEL), lambda b, e: (e, b, 0)),
            pl.BlockSpec((1, cap, 1), lambda b, e: (e, b, 0)),
        ],
        out_shape=[
            jax.ShapeDtypeStruct((N_EXPERTS, batch * cap, D_MODEL), BF16),
            jax.ShapeDtypeStruct((N_EXPERTS, batch * cap, 1), F32),
        ],
        compiler_params=_params("parallel", "arbitrary"),
        name="moe_gather",
    )(slot.reshape(rows, 1, seq), wsel.reshape(rows, 1, seq), hm)


def _expert_kernel(n_e, n_t, xe_ref, wg_ref, wu_ref, wd_ref, gs_ref, ye_ref, mid_ref):
    g = pl.program_id(0)
    s = pl.program_id(1)
    t = s // 2
    tf = wg_ref.shape[-1]

    @pl.when(jnp.logical_and(s % 2 == 0, g < n_e))
    def _():
        x = xe_ref[0]
        a = jnp.dot(x, wg_ref[...].astype(BF16), preferred_element_type=F32)
        u = jnp.dot(x, wu_ref[...].astype(BF16), preferred_element_type=F32)
        mid_ref[g % 2, t] = (a * jax.nn.sigmoid(a) * u).astype(BF16)

    @pl.when(jnp.logical_and(s % 2 == 1, g >= 1))
    def _():
        prev = (g - 1) % 2
        acc = jnp.dot(mid_ref[prev, 0], wd_ref[0:tf, :].astype(BF16), preferred_element_type=F32)
        for f in range(1, n_t):
            acc += jnp.dot(mid_ref[prev, f], wd_ref[f * tf:(f + 1) * tf, :].astype(BF16),
                           preferred_element_type=F32)
        ye_ref[0] = (acc * gs_ref[0]).astype(BF16)


def _expert_ffn(xe, gs, w_gate, w_up, w_down, layer, tf):
    n_e, rows, d = xe.shape
    ff = w_gate.shape[-1]
    n_t = ff // tf
    assert d // tf == n_t
    last = n_e - 1

    def fill_tile(g, s):
        return (layer, jnp.minimum(g, last), 0, jnp.where(g < n_e, s // 2, n_t - 1))

    def emit_tile(g, s):
        return (layer, jnp.maximum(g - 1, 0), 0, jnp.where(g >= 1, s // 2, 0))

    return pl.pallas_call(
        functools.partial(_expert_kernel, n_e, n_t),
        grid=(n_e + 1, 2 * n_t),
        in_specs=[
            pl.BlockSpec((1, rows, d), lambda g, s: (jnp.minimum(g, last), 0, 0)),
            pl.BlockSpec((None, None, d, tf), fill_tile),
            pl.BlockSpec((None, None, d, tf), fill_tile),
            pl.BlockSpec((None, None, ff, tf), emit_tile),
            pl.BlockSpec((1, rows, 1), lambda g, s: (jnp.maximum(g - 1, 0), 0, 0)),
        ],
        out_specs=pl.BlockSpec((1, rows, tf), lambda g, s: emit_tile(g, s)[1:]),
        out_shape=jax.ShapeDtypeStruct((n_e, rows, d), BF16),
        scratch_shapes=[pltpu.VMEM((2, n_t, rows, tf), BF16)],
        compiler_params=_params("arbitrary", "arbitrary"),
        name="moe_expert_ffn",
    )(xe, w_gate, w_up, w_down, gs)


def _scatter_kernel(cap, slot_ref, ye_ref, x_ref, o_ref, onehot_ref):
    n_e = ye_ref.shape[0]
    n_tok = x_ref.shape[0]

    @pl.when(pl.program_id(1) == 0)
    def _():
        col = lax.broadcasted_iota(I32, (n_tok, cap), 1)
        slots = slot_ref[0]
        for e in range(n_e):
            onehot_ref[:, e * cap:(e + 1) * cap] = (slots[:, e:e + 1] == col).astype(BF16)

    ye = ye_ref[...].reshape(n_e * cap, ye_ref.shape[-1])
    o_ref[...] = x_ref[...] + jnp.dot(onehot_ref[...], ye, preferred_element_type=F32)


def _scatter_add(slot_t, ye, x2d, batch, seq, cap, tn):
    return pl.pallas_call(
        functools.partial(_scatter_kernel, cap),
        grid=(batch, D_MODEL // tn),
        in_specs=[
            pl.BlockSpec((1, seq, 128), lambda b, n: (b, 0, 0)),
            pl.BlockSpec((N_EXPERTS, cap, tn), lambda b, n: (0, b, n)),
            pl.BlockSpec((seq, tn), lambda b, n: (b, n)),
        ],
        out_specs=pl.BlockSpec((seq, tn), lambda b, n: (b, n)),
        out_shape=jax.ShapeDtypeStruct(x2d.shape, F32),
        scratch_shapes=[pltpu.VMEM((seq, N_EXPERTS * cap), BF16)],
        compiler_params=_params("parallel", "arbitrary"),
        name="moe_scatter_add",
    )(slot_t, ye, x2d)


def _expert_choice_ffn(x2d, gain, w_router, w_gate, w_up, w_down, layer, batch, seq):
    cap = EC_CAPACITY * seq // N_EXPERTS
    hm, aff = _router(x2d, gain, w_router, batch, seq, 512)
    slot, wsel, slot_t = _topk_select(aff.reshape(batch * N_EXPERTS, seq), cap, N_EXPERTS)
    xe, gs = _gather_tokens(slot, wsel, hm, batch, seq, cap)
    ye = _expert_ffn(xe, gs, w_gate, w_up, w_down, layer, 512)
    return _scatter_add(slot_t, ye, x2d, batch, seq, cap, 512)


def _final_norm_kernel(x_ref, g_ref, o_ref):
    o_ref[...] = _rms(x_ref[...], g_ref[...])


def _final_norm(x2d, gain, tm):
    m, d = x2d.shape
    return pl.pallas_call(
        _final_norm_kernel,
        grid=(m // tm,),
        in_specs=[pl.BlockSpec((tm, d), lambda i: (i, 0)), pl.BlockSpec((1, d), lambda i: (0, 0))],
        out_specs=pl.BlockSpec((tm, d), lambda i: (i, 0)),
        out_shape=jax.ShapeDtypeStruct((m, d), F32),
        compiler_params=_params("parallel"),
        name="final_norm",
    )(x2d, gain.reshape(1, d))


def kernel(x, mem, norm_mix, w_in, na_rpb, hy_conv_w, hy_conv_b, hy_filt_w1, hy_filt_b1, hy_filt_w2, hy_filt_b2, hy_filt_w3, hy_sin_freq, hy_skip_d, branch_norm, w_out, norm_cross, mem_norm, w_cq, w_ckv, w_co, norm_moe, w_router, w_gate, w_up, w_down, final_norm):
    batch, seq, d = x.shape
    n_mem = mem.shape[1]
    depth = w_in.shape[0]
    m = batch * seq
    na_cols = 3 * NA_W
    hy_cols = 3 * HY_W
    rows = seq // GRID_W

    dft_fwd, dft_inv = _dft_tables(seq)
    x2d = x.reshape(m, d)
    mem2d = mem.reshape(batch * n_mem, d)

    w_in_b, w_out_b, w_cq_b, w_ckv_b, w_co_b = (w.astype(BF16) for w in (w_in, w_out, w_cq, w_ckv, w_co))
    p_in = w_in.shape[-1]

    for l in range(depth):
        proj = _norm_matmul(x2d, norm_mix[l], _whole(w_in_b, l), 1024, 768, BF16, "in_proj")
        proj = proj.reshape(batch, seq, p_in)

        y_na = _neighbourhood_attention(proj, _na_bias_table(na_rpb[l], rows), batch, seq)
        y_hy = _hyena_mixer(proj, na_cols, hy_conv_w[l], hy_conv_b[l], hy_filt_w1[l],
                            hy_filt_b1[l], hy_filt_w2[l], hy_filt_b2[l], hy_filt_w3[l], hy_sin_freq[l],
                            hy_skip_d[l], dft_fwd, dft_inv, batch, seq)
        y_ret = _retention_mixer(proj, na_cols + hy_cols, batch, seq)
        x2d = _branch_out_proj(y_na.reshape(m, NA_W), y_hy.reshape(m, HY_W), y_ret.reshape(m, RET_W),
                               branch_norm[l], _whole(w_out_b, l), x2d, 1024, 1024)

        q = _norm_matmul(x2d, norm_cross[l], _whole(w_cq_b, l), 1024, 1024, BF16, "cross_q_proj")
        kv = _norm_matmul(mem2d, mem_norm, _whole(w_ckv_b, l), 512, 1024, BF16, "cross_kv_proj")
        o = _cross_attention(q, kv, batch, seq, n_mem, 512)
        x2d = _plain_matmul(o, _whole(w_co_b, l), 1024, 1024, F32, "cross_out_proj", res=x2d)

        x2d = _expert_choice_ffn(x2d, norm_moe[l], w_router[l], w_gate, w_up, w_down, l, batch, seq)

    return _final_norm(x2d, final_norm, 512).reshape(batch, seq, d)
```

```python
import functools
import math

import numpy as np
import jax
import jax.numpy as jnp
from jax import lax
from jax.experimental import pallas as pl
from jax.experimental.pallas import tpu as pltpu

F32 = jnp.float32
BF16 = jnp.bfloat16
I32 = jnp.int32

D_MODEL = 2048
GRID_W = 64
NA_HEAD_DIM = 64
NA_W = 768
NA_HEADS = 12
NA_KR = 8
NA_KC = 16
HY_W = 512
HY_ORDER = 2
HY_BANDS = 8
HY_POS_DIM = 17
HY_FILT_FF = 64
RET_HEAD_DIM = 128
RET_W = 768
RET_HEADS = 6
RET_CHUNK = 128
CROSS_HEADS = 4
CROSS_HEAD_DIM = 512
N_EXPERTS = 16
EXPERT_FF = 2048
EC_CAPACITY = 2
RMS_EPS = 1e-6
GN_EPS = 1e-5

MASK_VALUE = -1e30
VMEM_LIMIT_BYTES = 56 * 1024 * 1024

NT_DIMS = (((1,), (1,)), ((), ()))


def _params(*sem):
    return pltpu.CompilerParams(dimension_semantics=sem, vmem_limit_bytes=VMEM_LIMIT_BYTES)


def _rms(xf, g):
    return xf * lax.rsqrt(jnp.mean(xf * xf, axis=-1, keepdims=True) + RMS_EPS) * g


def _mm_kernel(prologue, n_pro, has_res, *refs):
    pro_refs = refs[:n_pro]
    w_ref = refs[n_pro]
    res_ref = refs[n_pro + 1] if has_res else None
    o_ref = refs[n_pro + 1 + has_res]
    h_ref = refs[n_pro + 2 + has_res]

    @pl.when(pl.program_id(1) == 0)
    def _():
        h_ref[...] = prologue(*pro_refs).astype(BF16)

    acc = jnp.dot(h_ref[...], w_ref[...], preferred_element_type=F32)
    if has_res:
        acc = acc + res_ref[...]
    o_ref[...] = acc.astype(o_ref.dtype)


def _fused_matmul(prologue, pro_args, pro_specs, wsel, res, m, tm, tn, out_dtype, name):
    w, layer, col0, n = wsel
    k = w.shape[1]
    cb0 = col0 // tn
    assert col0 % tn == 0 and n % tn == 0 and m % tm == 0
    has_res = res is not None
    in_specs = list(pro_specs) + [pl.BlockSpec((None, k, tn), lambda i, j: (layer, 0, cb0 + j))]
    args = list(pro_args) + [w]
    if has_res:
        in_specs.append(pl.BlockSpec((tm, tn), lambda i, j: (i, j)))
        args.append(res)
    return pl.pallas_call(
        functools.partial(_mm_kernel, prologue, len(pro_args), has_res),
        grid=(m // tm, n // tn),
        in_specs=in_specs,
        out_specs=pl.BlockSpec((tm, tn), lambda i, j: (i, j)),
        out_shape=jax.ShapeDtypeStruct((m, n), out_dtype),
        scratch_shapes=[pltpu.VMEM((tm, k), BF16)],
        compiler_params=_params("parallel", "arbitrary"),
        name=name,
    )(*args)


def _rms_prologue(x_ref, g_ref):
    return _rms(x_ref[...], g_ref[...])


def _whole(w, layer):
    return (w, layer, 0, w.shape[-1])


def _norm_matmul(x2d, gain, wsel, tm, tn, out_dtype, name, res=None):
    m, k = x2d.shape
    specs = [pl.BlockSpec((tm, k), lambda i, j: (i, 0)), pl.BlockSpec((1, k), lambda i, j: (0, 0))]
    return _fused_matmul(_rms_prologue, [x2d, gain.reshape(1, k)], specs, wsel, res, m, tm, tn, out_dtype, name)


def _identity_prologue(a_ref):
    return a_ref[...]


def _plain_matmul(a, wsel, tm, tn, out_dtype, name, res=None):
    m, k = a.shape
    specs = [pl.BlockSpec((tm, k), lambda i, j: (i, 0))]
    return _fused_matmul(_identity_prologue, [a], specs, wsel, res, m, tm, tn, out_dtype, name)


def _branch_prologue(na_ref, hy_ref, ret_ref, g_ref):
    def nrm(p):
        return p * lax.rsqrt(jnp.mean(p * p, axis=-1, keepdims=True) + RMS_EPS)

    y = jnp.concatenate([nrm(na_ref[...]), nrm(hy_ref[...]), nrm(ret_ref[...])], axis=-1)
    return y * g_ref[...]


def _branch_out_proj(y_na, y_hy, y_ret, gain, wsel, res, tm, tn):
    m = y_na.shape[0]
    specs = [
        pl.BlockSpec((tm, NA_W), lambda i, j: (i, 0)),
        pl.BlockSpec((tm, HY_W), lambda i, j: (i, 0)),
        pl.BlockSpec((tm, RET_W), lambda i, j: (i, 0)),
        pl.BlockSpec((1, D_MODEL), lambda i, j: (0, 0)),
    ]
    return _fused_matmul(_branch_prologue, [y_na, y_hy, y_ret, gain.reshape(1, D_MODEL)], specs, wsel, res,
                         m, tm, tn, F32, "branch_out_proj")


NA_PAIR = 2
NA_BAND = NA_KR + NA_PAIR


def _na_band_base(r, rows):
    return np.clip(r - NA_KR // 2, 0, rows - NA_BAND)


@functools.lru_cache(maxsize=None)
def _na_variants(rows):
    assert rows % NA_PAIR == 0 and rows >= NA_BAND + 2
    n_var = NA_KR // 2 + 1
    dr = np.full((n_var, NA_BAND, NA_PAIR), -2, np.int64)
    for r in range(0, rows, NA_PAIR):
        base = int(_na_band_base(r, rows))
        v = (r - base) // 2
        for j in range(NA_PAIR):
            rs = int(np.clip(r + j - NA_KR // 2, 0, rows - NA_KR))
            for i in range(NA_BAND):
                val = base + i - (r + j) + (NA_KR - 1) if rs <= base + i < rs + NA_KR else -1
                assert dr[v, i, j] in (-2, val)
                dr[v, i, j] = val
    assert (dr > -2).all()
    return dr


def _na_bias_table(rpb, rows):
    c = np.arange(GRID_W)
    col_start = np.clip(c - NA_KC // 2, 0, GRID_W - NA_KC)
    col_in = (c[None, :] >= col_start[:, None]) & (c[None, :] < col_start[:, None] + NA_KC)
    dc = np.clip(c[None, :] - c[:, None] + (NA_KC - 1), 0, 2 * NA_KC - 2)
    onehot = (dc.reshape(-1)[None, :] == np.arange(2 * NA_KC - 1)[:, None]).astype(np.float32)
    cols = jnp.einsum("hab,bn->han", rpb.astype(F32), jnp.asarray(onehot), precision=lax.Precision.HIGHEST)
    cols = cols.reshape(NA_HEADS, 2 * NA_KR - 1, GRID_W, GRID_W)
    cols = jnp.where(col_in[None, None], cols, MASK_VALUE).transpose(0, 1, 3, 2)
    masked = jnp.full((NA_HEADS, GRID_W, GRID_W), MASK_VALUE, F32)
    dr = _na_variants(rows)
    variants = []
    for v in range(dr.shape[0]):
        band = [jnp.concatenate([cols[:, dr[v, i, j]] if dr[v, i, j] >= 0 else masked for j in range(NA_PAIR)],
                                axis=-1) for i in range(NA_BAND)]
        variants.append(jnp.concatenate(band, axis=1))
    return jnp.stack(variants, axis=1)


def _na_kernel(rows, q_ref, k_ref, v_ref, bias_ref, o_ref, vt_ref):
    dh = NA_HEAD_DIM
    nq = NA_PAIR * GRID_W
    nk = NA_BAND * GRID_W
    n_chunk = nk // 128
    lane = lax.broadcasted_iota(I32, (nq, 2 * dh), 1)

    for ch in range(vt_ref.shape[0]):
        vt_ref[ch] = v_ref[0, ch * 128:(ch + 1) * 128, :].astype(F32).T.astype(BF16)

    def body(p, carry):
        r = p * NA_PAIR
        base = jnp.clip(r - NA_KR // 2, 0, rows - NA_BAND)
        variant = (r - base) // 2
        q = q_ref[0, pl.ds(pl.multiple_of(r * GRID_W, nq), nq), :] * (dh ** -0.5)
        kb = k_ref[0, pl.ds(pl.multiple_of(base * GRID_W, 128), nk), :]
        c0 = base // 2
        vt = jnp.concatenate([vt_ref[c0 + i] for i in range(n_chunk)], axis=1)
        zero = jnp.zeros_like(q)
        q2 = jnp.concatenate([jnp.where(lane < dh, q, zero), jnp.where(lane >= dh, q, zero)], axis=0)
        st2 = lax.dot_general(kb, q2, NT_DIMS, preferred_element_type=F32)
        outs = []
        for hh in range(2):
            st = st2[:, hh * nq:(hh + 1) * nq] + bias_ref[hh, variant]
            m = jnp.max(st, axis=0, keepdims=True)
            pt = jnp.exp(st - m)
            l = jnp.sum(pt, axis=0, keepdims=True)
            ot = jnp.dot(vt[hh * dh:(hh + 1) * dh, :], pt.astype(BF16), preferred_element_type=F32)
            outs.append(ot * (1.0 / l))
        o_ref[0, pl.ds(pl.multiple_of(r * GRID_W, nq), nq), :] = jnp.concatenate(outs, axis=0).T
        return carry

    lax.fori_loop(0, rows // NA_PAIR, body, 0, unroll=2)


def _neighbourhood_attention(proj_na, bias_tbl, batch, seq):
    rows = seq // GRID_W
    n_pairs = NA_HEADS // 2
    blk = (1, seq, 2 * NA_HEAD_DIM)
    n_var, nk, nq = bias_tbl.shape[1:]
    return pl.pallas_call(
        functools.partial(_na_kernel, rows),
        grid=(batch, n_pairs),
        in_specs=[
            pl.BlockSpec(blk, lambda b, h: (b, 0, h)),
            pl.BlockSpec(blk, lambda b, h: (b, 0, n_pairs + h)),
            pl.BlockSpec(blk, lambda b, h: (b, 0, 2 * n_pairs + h)),
            pl.BlockSpec((2, n_var, nk, nq), lambda b, h: (h, 0, 0, 0)),
        ],
        out_specs=pl.BlockSpec(blk, lambda b, h: (b, 0, h)),
        out_shape=jax.ShapeDtypeStruct((batch, seq, NA_W), F32),
        scratch_shapes=[pltpu.VMEM((seq // 128, 2 * NA_HEAD_DIM, 128), BF16)],
        compiler_params=_params("parallel", "arbitrary"),
        name="neighbourhood_attention",
    )(proj_na, proj_na, proj_na, bias_tbl)


@functools.lru_cache(maxsize=None)
def _dft_factors(seq):
    n = 2 * seq
    t = np.arange(seq, dtype=np.int64)
    f1 = np.arange(seq // 64, dtype=np.int64)
    f0 = np.arange(64, dtype=np.int64)
    a = 2.0 * np.pi * ((64 * f1[:, None] * t[None, :]) % n).astype(np.float64) / n
    b = 2.0 * np.pi * ((f0[:, None] * t[None, :]) % n).astype(np.float64) / n
    ny = np.where(t % 2 == 0, 1.0, -1.0)
    return tuple(np.asarray(v, np.float32) for v in (np.cos(a), np.sin(a), np.cos(b), np.sin(b), ny))


def _dft_tables(seq):
    ca, sa, cb, sb, ny = (jnp.asarray(v) for v in _dft_factors(seq))
    cos = (ca[:, None, :] * cb[None] - sa[:, None, :] * sb[None]).reshape(seq, seq)
    sin = (sa[:, None, :] * cb[None] + ca[:, None, :] * sb[None]).reshape(seq, seq)
    sin = jnp.concatenate([ny[None, :], sin[1:]], axis=0)
    fwd = jnp.stack([cos, sin]).astype(BF16)
    inv = jnp.concatenate([cos.T, sin.T], axis=1).astype(BF16)
    return fwd, inv


@functools.lru_cache(maxsize=None)
def _hyena_consts(seq):
    t = np.arange(seq, dtype=np.float64)
    t01 = t / (seq - 1)
    bands = np.linspace(1e-4, HY_BANDS - 1, HY_BANDS)
    ang = (2.0 * math.pi) * (t[:, None] / seq) * bands[None, :]
    feats = np.concatenate([t01[:, None], np.cos(ang), -np.sin(ang)], axis=-1)
    feats_p = np.zeros((seq, 128), np.float32)
    feats_p[:, :HY_POS_DIM] = feats
    min_decay = math.log(1e-2) / 1.5
    max_decay = math.log(1e-2) / 0.3
    deltas = np.abs(np.linspace(min_decay, max_decay, HY_W))
    window = np.exp(-t01[:, None] * deltas[None, :]).astype(np.float32)
    return feats_p, window


def _filter_kernel(feats_ref, w1_ref, b1_ref, w2_ref, b2_ref, freq_ref, w3_ref, win_ref, o_ref, hid_ref):
    hp = lax.Precision.HIGHEST

    @pl.when(pl.program_id(0) == 0)
    def _():
        f = freq_ref[...]
        h1 = jnp.sin(f * (jnp.dot(feats_ref[...], w1_ref[...], precision=hp, preferred_element_type=F32)
                          + b1_ref[...]))
        hid_ref[...] = jnp.sin(f * (jnp.dot(h1, w2_ref[...], precision=hp, preferred_element_type=F32)
                                    + b2_ref[...]))

    h = jnp.dot(hid_ref[...], w3_ref[...], precision=hp, preferred_element_type=F32) * win_ref[...]
    row = lax.broadcasted_iota(I32, h.shape, 0)
    drop = jnp.logical_and(row == 0, pl.program_id(0) >= HY_ORDER)
    o_ref[...] = jnp.where(drop, 0.0, h).astype(BF16)


def _hyena_filters_time(w1, b1, w2, b2, w3, freq, seq):
    feats, window = _hyena_consts(seq)
    w1p = jnp.zeros((128, HY_FILT_FF), F32).at[:HY_POS_DIM].set(w1)
    w3r = w3.reshape(HY_FILT_FF, HY_ORDER, 2, HY_W).transpose(0, 2, 1, 3).reshape(HY_FILT_FF, 4 * HY_W)
    full = lambda shape: pl.BlockSpec(shape, lambda c: (0,) * len(shape))
    return pl.pallas_call(
        _filter_kernel,
        grid=(4,),
        in_specs=[
            full((seq, 128)), full((128, HY_FILT_FF)), full((1, HY_FILT_FF)),
            full((HY_FILT_FF, HY_FILT_FF)), full((1, HY_FILT_FF)), full((1, HY_FILT_FF)),
            pl.BlockSpec((HY_FILT_FF, HY_W), lambda c: (0, c)),
            full((seq, HY_W)),
        ],
        out_specs=pl.BlockSpec((seq, HY_W), lambda c: (0, c)),
        out_shape=jax.ShapeDtypeStruct((seq, 4 * HY_W), BF16),
        scratch_shapes=[pltpu.VMEM((seq, HY_FILT_FF), F32)],
        compiler_params=_params("arbitrary"),
        name="hyena_filter_mlp",
    )(jnp.asarray(feats), w1p, b1.reshape(1, -1), w2, b2.reshape(1, -1), freq.reshape(1, -1), w3r,
      jnp.asarray(window))


def _filter_dft_kernel(f_ref, fwd_ref, bwd_ref, o_ref):
    fwd = fwd_ref[...]
    bwd = bwd_ref[...]
    uc = jnp.dot(f_ref[0], fwd, preferred_element_type=F32)
    vc = jnp.dot(f_ref[0], bwd, preferred_element_type=F32)
    us = jnp.dot(f_ref[1], fwd, preferred_element_type=F32)
    vs = jnp.dot(f_ref[1], bwd, preferred_element_type=F32)
    o_ref[0] = uc + vc
    row = lax.broadcasted_iota(I32, us.shape, 0)
    nyq = jnp.logical_and(row == 0, pl.program_id(0) == 0)
    o_ref[1] = jnp.where(nyq, us + vs, us - vs)


def _filter_dft(dft_fwd, filt_t, seq, fb):
    return pl.pallas_call(
        _filter_dft_kernel,
        grid=(seq // fb, HY_ORDER),
        in_specs=[
            pl.BlockSpec((2, fb, seq), lambda f, o: (0, f, 0)),
            pl.BlockSpec((seq, HY_W), lambda f, o: (0, o)),
            pl.BlockSpec((seq, HY_W), lambda f, o: (0, HY_ORDER + o)),
        ],
        out_specs=pl.BlockSpec((2, fb, HY_W), lambda f, o: (0, f, o)),
        out_shape=jax.ShapeDtypeStruct((2, seq, HY_ORDER * HY_W), F32),
        compiler_params=_params("parallel", "arbitrary"),
        name="hyena_filter_dft",
    )(dft_fwd, filt_t, filt_t)


def _short_conv_kernel(p_ref, w_ref, b_ref, o_ref):
    p = p_ref[0].astype(F32)
    seq = p.shape[0]
    row = lax.broadcasted_iota(I32, p.shape, 0)
    prev = jnp.where(row == 0, 0.0, pltpu.roll(p, 1, 0))
    nxt = jnp.where(row == seq - 1, 0.0, pltpu.roll(p, seq - 1, 0))
    w = w_ref[...]
    o_ref[0] = prev * w[0:1] + p * w[1:2] + nxt * w[2:3] + b_ref[...]


def _short_conv(proj, col0, conv_w, conv_b, batch, seq, tc):
    n_cols = conv_w.shape[-1]
    assert col0 % tc == 0 and n_cols % tc == 0
    cb0 = col0 // tc
    return pl.pallas_call(
        _short_conv_kernel,
        grid=(batch, n_cols // tc),
        in_specs=[
            pl.BlockSpec((1, seq, tc), lambda b, c: (b, 0, cb0 + c)),
            pl.BlockSpec((3, tc), lambda b, c: (0, c)),
            pl.BlockSpec((1, tc), lambda b, c: (0, c)),
        ],
        out_specs=pl.BlockSpec((1, seq, tc), lambda b, c: (b, 0, c)),
        out_shape=jax.ShapeDtypeStruct((batch, seq, n_cols), F32),
        compiler_params=_params("parallel", "arbitrary"),
        name="hyena_short_conv",
    )(proj, conv_w, conv_b.reshape(1, -1))


def _spectrum_kernel(n_fft, f_ref, z_ref, k_ref, o_ref):
    z = z_ref[0].astype(BF16)
    xr = jnp.dot(f_ref[0], z, preferred_element_type=F32)
    xs = jnp.dot(f_ref[1], z, preferred_element_type=F32)
    kr = k_ref[0]
    ks = k_ref[1]
    row = lax.broadcasted_iota(I32, xr.shape, 0)
    edge = jnp.logical_and(row == 0, pl.program_id(0) == 0)
    yr = jnp.where(edge, xr * kr * (1.0 / n_fft), (xr * kr - xs * ks) * (2.0 / n_fft))
    ys = jnp.where(edge, xs * ks * (1.0 / n_fft), (xr * ks + xs * kr) * (2.0 / n_fft))
    o_ref[0, 0] = yr.astype(BF16)
    o_ref[0, 1] = ys.astype(BF16)


def _spectrum_product(dft_fwd, z_arr, z_col, kfreq, order, batch, seq, fb):
    return pl.pallas_call(
        functools.partial(_spectrum_kernel, 2 * seq),
        grid=(seq // fb, batch),
        in_specs=[
            pl.BlockSpec((2, fb, seq), lambda f, b: (0, f, 0)),
            pl.BlockSpec((1, seq, HY_W), lambda f, b: (b, 0, z_col)),
            pl.BlockSpec((2, fb, HY_W), lambda f, b: (0, f, order)),
        ],
        out_specs=pl.BlockSpec((1, 2, fb, HY_W), lambda f, b: (b, 0, f, 0)),
        out_shape=jax.ShapeDtypeStruct((batch, 2, seq, HY_W), BF16),
        compiler_params=_params("parallel", "arbitrary"),
        name="hyena_spectrum",
    )(dft_fwd, z_arr, kfreq)


def _inverse_kernel(ft_ref, y_ref, gate_ref, z_ref, d_ref, o_ref):
    conv = jnp.dot(ft_ref[...], y_ref[0], preferred_element_type=F32)
    o_ref[0] = gate_ref[0] * (conv + d_ref[...] * z_ref[0])


def _inverse_gate(dft_inv, y, gate_arr, gate_col, z_arr, z_col, skip_row, batch, seq, tb):
    y2 = y.reshape(batch, 2 * seq, HY_W)
    return pl.pallas_call(
        _inverse_kernel,
        grid=(seq // tb, batch),
        in_specs=[
            pl.BlockSpec((tb, 2 * seq), lambda t, b: (t, 0)),
            pl.BlockSpec((1, 2 * seq, HY_W), lambda t, b: (b, 0, 0)),
            pl.BlockSpec((1, tb, HY_W), lambda t, b: (b, t, gate_col)),
            pl.BlockSpec((1, tb, HY_W), lambda t, b: (b, t, z_col)),
            pl.BlockSpec((1, HY_W), lambda t, b: (0, 0)),
        ],
        out_specs=pl.BlockSpec((1, tb, HY_W), lambda t, b: (b, t, 0)),
        out_shape=jax.ShapeDtypeStruct((batch, seq, HY_W), F32),
        compiler_params=_params("parallel", "arbitrary"),
        name="hyena_inverse_gate",
    )(dft_inv, y2, gate_arr, z_arr, skip_row.reshape(1, HY_W))


def _hyena_mixer(proj, col0, conv_w, conv_b, w1, b1, w2, b2, w3, freq, skip_d, dft_fwd, dft_inv, batch, seq):
    s = _short_conv(proj, col0, conv_w, conv_b, batch, seq, 256)
    filt_t = _hyena_filters_time(w1, b1, w2, b2, w3, freq, seq)
    kfreq = _filter_dft(dft_fwd, filt_t, seq, 512)
    z_arr, z_col = s, 2
    for o in range(HY_ORDER):
        y = _spectrum_product(dft_fwd, z_arr, z_col, kfreq, o, batch, seq, 512)
        z_arr = _inverse_gate(dft_inv, y, s, o, z_arr, z_col, skip_d[o], batch, seq, 512)
        z_col = 0
    return z_arr


@functools.lru_cache(maxsize=None)
def _retention_consts(seq):
    c = RET_CHUNK
    half = RET_HEAD_DIM // 2
    inv = 1.0 / (10000.0 ** np.linspace(0.0, 1.0, half))
    ang = np.arange(seq, dtype=np.float64)[:, None] * inv[None, :]
    cos2 = np.concatenate([np.cos(ang), np.cos(ang)], axis=-1).astype(np.float32)
    sin2 = np.concatenate([-np.sin(ang), np.sin(ang)], axis=-1).astype(np.float32)
    hidx = np.arange(RET_HEADS, dtype=np.float64)
    lg_f = np.log1p(-np.exp2(-5.0 - hidx))[:, None, None]
    lg_b = np.log1p(-np.exp2(-5.5 - hidx))[:, None, None]
    i = np.arange(c, dtype=np.float64)
    diff = i[:, None] - i[None, :]
    ones = np.ones((1, c, c))
    dec = np.where(diff >= 0, np.exp(lg_f * np.maximum(diff, 0.0)), np.exp(lg_b * np.maximum(-diff, 0.0)))
    rowv = lambda v: v[:, :, None] * ones
    tab = np.stack([
        dec,
        rowv(np.exp(lg_f[:, :, 0] * (i + 1.0)[None, :])),
        rowv(np.exp(lg_f[:, :, 0] * (c - 1.0 - i)[None, :])),
        rowv(np.exp(lg_b[:, :, 0] * (c - i)[None, :])),
        rowv(np.exp(lg_b[:, :, 0] * i[None, :])),
        np.exp(lg_f * c) * ones,
        np.exp(lg_b * c) * ones,
    ], axis=1).astype(np.float32)
    return cos2, sin2, tab


def _retention_kernel(q_ref, k_ref, v_ref, g_ref, cos_ref, sin_ref, tab_ref, o_ref, qs_ref, ks_ref, kvf_ref,
                      kvb_ref):
    c = RET_CHUNK
    d = RET_HEAD_DIM
    seq = q_ref.shape[1]
    n_chunks = seq // c
    cos = cos_ref[...]
    sin = sin_ref[...]
    q = q_ref[0].astype(F32)
    k = k_ref[0].astype(F32)
    qs_ref[...] = (q * cos + pltpu.roll(q, d // 2, 1) * sin) * (d ** -0.5)
    ks_ref[...] = k * cos + pltpu.roll(k, d // 2, 1) * sin

    def mm(a, b):
        return jnp.dot(a.astype(BF16), b.astype(BF16), preferred_element_type=F32)

    def chunk(n):
        return pl.ds(pl.multiple_of(n * c, c), c)

    def kv_body(n, carry):
        kc = ks_ref[chunk(n), :]
        vc = v_ref[0, chunk(n), :]
        kvf_ref[n] = mm((kc * tab_ref[0, 2]).T, vc)
        kvb_ref[n] = mm((kc * tab_ref[0, 4]).T, vc)
        return carry

    lax.fori_loop(0, n_chunks, kv_body, 0, unroll=4)

    def scan_fwd(n, state):
        kv = kvf_ref[n]
        kvf_ref[n] = state
        return tab_ref[0, 5] * state + kv

    lax.fori_loop(0, n_chunks, scan_fwd, jnp.zeros((d, d), F32))

    def scan_bwd(m, state):
        n = n_chunks - 1 - m
        kv = kvb_ref[n]
        kvb_ref[n] = state
        return tab_ref[0, 6] * state + kv

    lax.fori_loop(0, n_chunks, scan_bwd, jnp.zeros((d, d), F32))

    def out_body(n, carry):
        qc = qs_ref[chunk(n), :]
        kc = ks_ref[chunk(n), :]
        vc = v_ref[0, chunk(n), :]
        a = lax.dot_general(qc.astype(BF16), kc.astype(BF16), NT_DIMS, preferred_element_type=F32) * tab_ref[0, 0]
        y = mm(a, vc) + mm(qc * tab_ref[0, 1], kvf_ref[n]) + mm(qc * tab_ref[0, 3], kvb_ref[n])
        mu = jnp.mean(y, axis=-1, keepdims=True)
        yc = y - mu
        var = jnp.mean(yc * yc, axis=-1, keepdims=True)
        g = g_ref[0, chunk(n), :].astype(F32)
        o_ref[0, chunk(n), :] = yc * lax.rsqrt(var + GN_EPS) * (g * jax.nn.sigmoid(g))
        return carry

    lax.fori_loop(0, n_chunks, out_body, 0, unroll=4)


def _retention_mixer(proj, col0, batch, seq):
    cos2, sin2, tab = _retention_consts(seq)
    blk = (1, seq, RET_HEAD_DIM)
    h_ = RET_HEADS
    assert col0 % RET_HEAD_DIM == 0
    c0 = col0 // RET_HEAD_DIM
    return pl.pallas_call(
        _retention_kernel,
        grid=(batch, RET_HEADS),
        in_specs=[
            pl.BlockSpec(blk, lambda b, h: (b, 0, c0 + h)),
            pl.BlockSpec(blk, lambda b, h: (b, 0, c0 + h_ + h)),
            pl.BlockSpec(blk, lambda b, h: (b, 0, c0 + 2 * h_ + h)),
            pl.BlockSpec(blk, lambda b, h: (b, 0, c0 + 3 * h_ + h)),
            pl.BlockSpec((seq, RET_HEAD_DIM), lambda b, h: (0, 0)),
            pl.BlockSpec((seq, RET_HEAD_DIM), lambda b, h: (0, 0)),
            pl.BlockSpec((1, 7, RET_CHUNK, RET_CHUNK), lambda b, h: (h, 0, 0, 0)),
        ],
        out_specs=pl.BlockSpec(blk, lambda b, h: (b, 0, h)),
        out_shape=jax.ShapeDtypeStruct((batch, seq, RET_W), F32),
        scratch_shapes=[pltpu.VMEM((seq, RET_HEAD_DIM), F32), pltpu.VMEM((seq, RET_HEAD_DIM), F32),
                        pltpu.VMEM((seq // RET_CHUNK, RET_HEAD_DIM, RET_HEAD_DIM), F32),
                        pltpu.VMEM((seq // RET_CHUNK, RET_HEAD_DIM, RET_HEAD_DIM), F32)],
        compiler_params=_params("parallel", "arbitrary"),
        name="retention",
    )(proj, proj, proj, proj, jnp.asarray(cos2), jnp.asarray(sin2), jnp.asarray(tab))


def _cross_kernel(q_ref, k_ref, v_ref, o_ref):
    dh = CROSS_HEAD_DIM
    for h in range(CROSS_HEADS):
        sl = slice(h * dh, (h + 1) * dh)
        s = lax.dot_general(q_ref[:, sl], k_ref[:, sl], NT_DIMS, preferred_element_type=F32) * (dh ** -0.5)
        m = jnp.max(s, axis=-1, keepdims=True)
        p = jnp.exp(s - m)
        l = jnp.sum(p, axis=-1, keepdims=True)
        o = jnp.dot(p.astype(BF16), v_ref[:, sl], preferred_element_type=F32) * (1.0 / l)
        o_ref[:, sl] = o.astype(o_ref.dtype)


def _cross_attention(q, kv, batch, seq, n_mem, tq):
    per_b = seq // tq
    return pl.pallas_call(
        _cross_kernel,
        grid=(batch, per_b),
        in_specs=[
            pl.BlockSpec((tq, D_MODEL), lambda b, i: (b * per_b + i, 0)),
            pl.BlockSpec((n_mem, D_MODEL), lambda b, i: (b, 0)),
            pl.BlockSpec((n_mem, D_MODEL), lambda b, i: (b, 1)),
        ],
        out_specs=pl.BlockSpec((tq, D_MODEL), lambda b, i: (b * per_b + i, 0)),
        out_shape=jax.ShapeDtypeStruct((batch * seq, D_MODEL), BF16),
        compiler_params=_params("parallel", "arbitrary"),
        name="cross_attention",
    )(q, kv, kv)


def _router_kernel(x_ref, g_ref, wr_ref, hm_ref, aff_ref):
    n_e = aff_ref.shape[1]
    h = _rms(x_ref[...], g_ref[...])
    h_hi = h.astype(BF16)
    hm_ref[...] = h_hi
    h_lo = (h - h_hi.astype(F32)).astype(BF16)
    w = wr_ref[...]
    w_hi = w.astype(BF16)
    w_lo = (w - w_hi.astype(F32)).astype(BF16)
    logits = (jnp.dot(h_hi, w_hi, preferred_element_type=F32) + jnp.dot(h_lo, w_hi, preferred_element_type=F32)
              + jnp.dot(h_hi, w_lo, preferred_element_type=F32)).T[:n_e]
    m = jnp.max(logits, axis=0, keepdims=True)
    e = jnp.exp(logits - m)
    aff_ref[0] = e / jnp.sum(e, axis=0, keepdims=True)


def _router(x2d, gain, w_router, batch, seq, tm):
    m = x2d.shape[0]
    per_b = seq // tm
    w_pad = jnp.zeros((D_MODEL, 128), F32).at[:, :N_EXPERTS].set(w_router)
    return pl.pallas_call(
        _router_kernel,
        grid=(m // tm,),
        in_specs=[
            pl.BlockSpec((tm, D_MODEL), lambda i: (i, 0)),
            pl.BlockSpec((1, D_MODEL), lambda i: (0, 0)),
            pl.BlockSpec((D_MODEL, 128), lambda i: (0, 0)),
        ],
        out_specs=[
            pl.BlockSpec((tm, D_MODEL), lambda i: (i, 0)),
            pl.BlockSpec((1, N_EXPERTS, tm), lambda i: (i // per_b, 0, i % per_b)),
        ],
        out_shape=[
            jax.ShapeDtypeStruct((m, D_MODEL), BF16),
            jax.ShapeDtypeStruct((batch, N_EXPERTS, seq), F32),
        ],
        compiler_params=_params("parallel"),
        name="moe_router",
    )(x2d, gain.reshape(1, D_MODEL), w_pad)


def _sort_descending(x):
    rows, n = x.shape
    lanes = 128
    n_chunks = n // lanes
    chunks = [x[:, c * lanes:(c + 1) * lanes] for c in range(n_chunks)]
    lane = lax.broadcasted_iota(I32, (rows, lanes), 1)
    k = 2
    while k <= n:
        j = k // 2
        while j >= 1:
            nxt = []
            for c, xc in enumerate(chunks):
                if j >= lanes:
                    lo = (c & (j // lanes)) == 0
                    partner = chunks[c ^ (j // lanes)]
                else:
                    lo = (lane & j) == 0
                    partner = jnp.where(lo, pltpu.roll(xc, lanes - j, 1), pltpu.roll(xc, j, 1))
                desc = ((c * lanes) & k) == 0 if k >= lanes else (lane & k) == 0
                hi, lw = jnp.maximum(xc, partner), jnp.minimum(xc, partner)
                if j >= lanes and k >= lanes:
                    nxt.append(hi if lo == desc else lw)
                else:
                    nxt.append(jnp.where(lo == desc, hi, lw))
            chunks = nxt
            j //= 2
        k *= 2
    return jnp.concatenate(chunks, axis=1)


def _topk_kernel(cap, n_e, aff_ref, slot_ref, wsel_ref, slot_t_ref, thr_ref):
    n_rows, n_tok = aff_ref.shape

    def thr_body(i, carry):
        rows8 = pl.ds(pl.multiple_of(i * 8, 8), 8)
        thr_ref[rows8, :] = _sort_descending(aff_ref[rows8, :])[:, cap - 1:cap]
        return carry

    lax.fori_loop(0, n_rows // 8, thr_body, 0)
    a = aff_ref[...]
    thr = thr_ref[...]
    gt = a > thr
    eq = a == thr
    need = cap - jnp.sum(gt.astype(I32), axis=1, keepdims=True)
    upper = (lax.broadcasted_iota(I32, (n_tok, n_tok), 0) < lax.broadcasted_iota(I32, (n_tok, n_tok), 1))
    upper = upper.astype(BF16)
    eq_rank = jnp.dot(eq.astype(BF16), upper, preferred_element_type=F32)
    sel = jnp.logical_or(gt, jnp.logical_and(eq, eq_rank < need.astype(F32)))
    rank = jnp.dot(sel.astype(BF16), upper, preferred_element_type=F32)
    slot = jnp.where(sel, rank, -1.0)
    slot_ref[...] = slot.astype(I32)
    wsel_ref[...] = jnp.where(sel, a, 0.0)
    pad = jnp.full((128 - n_e, n_tok), -1.0, F32)
    for b in range(n_rows // n_e):
        slot_t_ref[b] = jnp.concatenate([slot[b * n_e:(b + 1) * n_e], pad], axis=0).T.astype(I32)


def _topk_select(aff2d, cap, n_e):
    rows, n_tok = aff2d.shape
    spec = pl.BlockSpec((rows, n_tok), lambda i: (0, 0))
    spec_t = pl.BlockSpec((rows // n_e, n_tok, 128), lambda i: (0, 0, 0))
    return pl.pallas_call(
        functools.partial(_topk_kernel, cap, n_e),
        grid=(1,),
        in_specs=[spec],
        out_specs=[spec, spec, spec_t],
        out_shape=[jax.ShapeDtypeStruct((rows, n_tok), I32), jax.ShapeDtypeStruct((rows, n_tok), F32),
                   jax.ShapeDtypeStruct((rows // n_e, n_tok, 128), I32)],
        scratch_shapes=[pltpu.VMEM((rows, 1), F32)],
        compiler_params=_params("arbitrary"),
        name="moe_topk_select",
    )(aff2d)


def _gather_kernel(cap, slot_ref, wsel_ref, hm_ref, xe_ref, gs_ref):
    n_tok = hm_ref.shape[0]
    onehot = slot_ref[0] == lax.broadcasted_iota(I32, (cap, n_tok), 0)
    xe_ref[0] = jnp.dot(onehot.astype(BF16), hm_ref[...], preferred_element_type=F32).astype(BF16)
    gs_ref[0] = jnp.sum(jnp.where(onehot, wsel_ref[0], 0.0), axis=1, keepdims=True)


def _gather_tokens(slot, wsel, hm, batch, seq, cap):
    rows = batch * N_EXPERTS
    row_spec = pl.BlockSpec((1, 1, seq), lambda b, e: (b * N_EXPERTS + e, 0, 0))
    return pl.pallas_call(
        functools.partial(_gather_kernel, cap),
        grid=(batch, N_EXPERTS),
        in_specs=[row_spec, row_spec, pl.BlockSpec((seq, D_MODEL), lambda b, e: (b, 0))],
        out_specs=[
            pl.BlockSpec((1, cap, D_MODEL), lambda b, e: (e, b, 0)),
            pl.BlockSpec((1, cap, 1), lambda b, e: (e, b, 0)),
        ],
        out_shape=[
            jax.ShapeDtypeStruct((N_EXPERTS, batch * cap, D_MODEL), BF16),
            jax.ShapeDtypeStruct((N_EXPERTS, batch * cap, 1), F32),
        ],
        compiler_params=_params("parallel", "arbitrary"),
        name="moe_gather",
    )(slot.reshape(rows, 1, seq), wsel.reshape(rows, 1, seq), hm)


def _expert_kernel(n_e, n_t, xe_ref, wg_ref, wu_ref, wd_ref, gs_ref, ye_ref, mid_ref):
    g = pl.program_id(0)
    s = pl.program_id(1)
    t = s // 2
    tf = wg_ref.shape[-1]

    @pl.when(jnp.logical_and(s % 2 == 0, g < n_e))
    def _():
        x = xe_ref[0]
        a = jnp.dot(x, wg_ref[...].astype(BF16), preferred_element_type=F32)
        u = jnp.dot(x, wu_ref[...].astype(BF16), preferred_element_type=F32)
        mid_ref[g % 2, t] = (a * jax.nn.sigmoid(a) * u).astype(BF16)

    @pl.when(jnp.logical_and(s % 2 == 1, g >= 1))
    def _():
        prev = (g - 1) % 2
        acc = jnp.dot(mid_ref[prev, 0], wd_ref[0:tf, :].astype(BF16), preferred_element_type=F32)
        for f in range(1, n_t):
            acc += jnp.dot(mid_ref[prev, f], wd_ref[f * tf:(f + 1) * tf, :].astype(BF16),
                           preferred_element_type=F32)
        ye_ref[0] = (acc * gs_ref[0]).astype(BF16)


def _expert_ffn(xe, gs, w_gate, w_up, w_down, layer, tf):
    n_e, rows, d = xe.shape
    ff = w_gate.shape[-1]
    n_t = ff // tf
    assert d // tf == n_t
    last = n_e - 1

    n_s = 2 * n_t

    def fill_tile(g, s, lead):
        v = jnp.minimum(g * n_s + s + lead, n_e * n_s - 1)
        return (layer, v // n_s, 0, (v % n_s) // 2)

    def emit_tile(g, s):
        v = jnp.maximum(g * n_s + s - 1 - n_s, 0)
        return (layer, v // n_s, 0, (v % n_s) // 2)

    return pl.pallas_call(
        functools.partial(_expert_kernel, n_e, n_t),
        grid=(n_e + 1, n_s),
        in_specs=[
            pl.BlockSpec((1, rows, d), lambda g, s: (jnp.minimum(g, last), 0, 0)),
            pl.BlockSpec((None, None, d, tf), functools.partial(fill_tile, lead=1)),
            pl.BlockSpec((None, None, d, tf), functools.partial(fill_tile, lead=0)),
            pl.BlockSpec((None, None, ff, tf), emit_tile),
            pl.BlockSpec((1, rows, 1), lambda g, s: (jnp.maximum(g - 1, 0), 0, 0)),
        ],
        out_specs=pl.BlockSpec((1, rows, tf), lambda g, s: emit_tile(g, s)[1:]),
        out_shape=jax.ShapeDtypeStruct((n_e, rows, d), BF16),
        scratch_shapes=[pltpu.VMEM((2, n_t, rows, tf), BF16)],
        compiler_params=_params("arbitrary", "arbitrary"),
        name="moe_expert_ffn",
    )(xe, w_gate, w_up, w_down, gs)


def _scatter_kernel(cap, slot_ref, ye_ref, x_ref, o_ref, onehot_ref):
    n_e = ye_ref.shape[0]
    n_tok = x_ref.shape[0]

    @pl.when(pl.program_id(1) == 0)
    def _():
        col = lax.broadcasted_iota(I32, (n_tok, cap), 1)
        slots = slot_ref[0]
        for e in range(n_e):
            onehot_ref[:, e * cap:(e + 1) * cap] = (slots[:, e:e + 1] == col).astype(BF16)

    ye = ye_ref[...].reshape(n_e * cap, ye_ref.shape[-1])
    o_ref[...] = x_ref[...] + jnp.dot(onehot_ref[...], ye, preferred_element_type=F32)


def _scatter_add(slot_t, ye, x2d, batch, seq, cap, tn):
    return pl.pallas_call(
        functools.partial(_scatter_kernel, cap),
        grid=(batch, D_MODEL // tn),
        in_specs=[
            pl.BlockSpec((1, seq, 128), lambda b, n: (b, 0, 0)),
            pl.BlockSpec((N_EXPERTS, cap, tn), lambda b, n: (0, b, n)),
            pl.BlockSpec((seq, tn), lambda b, n: (b, n)),
        ],
        out_specs=pl.BlockSpec((seq, tn), lambda b, n: (b, n)),
        out_shape=jax.ShapeDtypeStruct(x2d.shape, F32),
        scratch_shapes=[pltpu.VMEM((seq, N_EXPERTS * cap), BF16)],
        compiler_params=_params("parallel", "arbitrary"),
        name="moe_scatter_add",
    )(slot_t, ye, x2d)


def _expert_choice_ffn(x2d, gain, w_router, w_gate, w_up, w_down, layer, batch, seq):
    cap = EC_CAPACITY * seq // N_EXPERTS
    hm, aff = _router(x2d, gain, w_router, batch, seq, 512)
    slot, wsel, slot_t = _topk_select(aff.reshape(batch * N_EXPERTS, seq), cap, N_EXPERTS)
    xe, gs = _gather_tokens(slot, wsel, hm, batch, seq, cap)
    ye = _expert_ffn(xe, gs, w_gate, w_up, w_down, layer, 512)
    return _scatter_add(slot_t, ye, x2d, batch, seq, cap, 512)


def _final_norm_kernel(x_ref, g_ref, o_ref):
    o_ref[...] = _rms(x_ref[...], g_ref[...])


def _final_norm(x2d, gain, tm):
    m, d = x2d.shape
    return pl.pallas_call(
        _final_norm_kernel,
        grid=(m // tm,),
        in_specs=[pl.BlockSpec((tm, d), lambda i: (i, 0)), pl.BlockSpec((1, d), lambda i: (0, 0))],
        out_specs=pl.BlockSpec((tm, d), lambda i: (i, 0)),
        out_shape=jax.ShapeDtypeStruct((m, d), F32),
        compiler_params=_params("parallel"),
        name="final_norm",
    )(x2d, gain.reshape(1, d))


def kernel(x, mem, norm_mix, w_in, na_rpb, hy_conv_w, hy_conv_b, hy_filt_w1, hy_filt_b1, hy_filt_w2, hy_filt_b2, hy_filt_w3, hy_sin_freq, hy_skip_d, branch_norm, w_out, norm_cross, mem_norm, w_cq, w_ckv, w_co, norm_moe, w_router, w_gate, w_up, w_down, final_norm):
    batch, seq, d = x.shape
    n_mem = mem.shape[1]
    depth = w_in.shape[0]
    m = batch * seq
    na_cols = 3 * NA_W
    hy_cols = 3 * HY_W
    rows = seq // GRID_W

    dft_fwd, dft_inv = _dft_tables(seq)
    x2d = x.reshape(m, d)
    mem2d = mem.reshape(batch * n_mem, d)

    w_in_b, w_out_b, w_cq_b, w_ckv_b, w_co_b = (w.astype(BF16) for w in (w_in, w_out, w_cq, w_ckv, w_co))
    p_in = w_in.shape[-1]

    for l in range(depth):
        proj = _norm_matmul(x2d, norm_mix[l], _whole(w_in_b, l), 1024, 768, BF16, "in_proj")
        proj = proj.reshape(batch, seq, p_in)

        y_na = _neighbourhood_attention(proj, _na_bias_table(na_rpb[l], rows), batch, seq)
        y_hy = _hyena_mixer(proj, na_cols, hy_conv_w[l], hy_conv_b[l], hy_filt_w1[l],
                            hy_filt_b1[l], hy_filt_w2[l], hy_filt_b2[l], hy_filt_w3[l], hy_sin_freq[l],
                            hy_skip_d[l], dft_fwd, dft_inv, batch, seq)
        y_ret = _retention_mixer(proj, na_cols + hy_cols, batch, seq)
        x2d = _branch_out_proj(y_na.reshape(m, NA_W), y_hy.reshape(m, HY_W), y_ret.reshape(m, RET_W),
                               branch_norm[l], _whole(w_out_b, l), x2d, 1024, 1024)

        q = _norm_matmul(x2d, norm_cross[l], _whole(w_cq_b, l), 1024, 1024, BF16, "cross_q_proj")
        kv = _norm_matmul(mem2d, mem_norm, _whole(w_ckv_b, l), 512, 1024, BF16, "cross_kv_proj")
        o = _cross_attention(q, kv, batch, seq, n_mem, 512)
        x2d = _plain_matmul(o, _whole(w_co_b, l), 1024, 1024, F32, "cross_out_proj", res=x2d)

        x2d = _expert_choice_ffn(x2d, norm_moe[l], w_router[l], w_gate, w_up, w_down, l, batch, seq)

    return _final_norm(x2d, final_norm, 512).reshape(batch, seq, d)
```

```python
import functools
import math

import numpy as np
import jax
import jax.numpy as jnp
from jax import lax
from jax.experimental import pallas as pl
from jax.experimental.pallas import tpu as pltpu

F32 = jnp.float32
BF16 = jnp.bfloat16
I32 = jnp.int32

D_MODEL = 2048
GRID_W = 64
NA_HEAD_DIM = 64
NA_W = 768
NA_HEADS = 12
NA_KR = 8
NA_KC = 16
HY_W = 512
HY_ORDER = 2
HY_BANDS = 8
HY_POS_DIM = 17
HY_FILT_FF = 64
RET_HEAD_DIM = 128
RET_W = 768
RET_HEADS = 6
RET_CHUNK = 128
CROSS_HEADS = 4
CROSS_HEAD_DIM = 512
N_EXPERTS = 16
EXPERT_FF = 2048
EC_CAPACITY = 2
RMS_EPS = 1e-6
GN_EPS = 1e-5

MASK_VALUE = -1e30
VMEM_LIMIT_BYTES = 56 * 1024 * 1024

NT_DIMS = (((1,), (1,)), ((), ()))


def _params(*sem):
    return pltpu.CompilerParams(dimension_semantics=sem, vmem_limit_bytes=VMEM_LIMIT_BYTES)


def _rms(xf, g):
    return xf * lax.rsqrt(jnp.mean(xf * xf, axis=-1, keepdims=True) + RMS_EPS) * g


def _mm_kernel(prologue, n_pro, has_res, *refs):
    pro_refs = refs[:n_pro]
    w_ref = refs[n_pro]
    res_ref = refs[n_pro + 1] if has_res else None
    o_ref = refs[n_pro + 1 + has_res]
    h_ref = refs[n_pro + 2 + has_res]

    @pl.when(pl.program_id(1) == 0)
    def _():
        h_ref[...] = prologue(*pro_refs).astype(BF16)

    w = w_ref[...]
    if w.dtype != BF16:
        w = w.astype(BF16)
    acc = jnp.dot(h_ref[...], w, preferred_element_type=F32)
    if has_res:
        acc = acc + res_ref[...]
    o_ref[...] = acc.astype(o_ref.dtype)


def _fused_matmul(prologue, pro_args, pro_specs, wsel, res, m, tm, tn, out_dtype, name):
    w, layer, col0, n = wsel
    k = w.shape[1]
    cb0 = col0 // tn
    assert col0 % tn == 0 and n % tn == 0 and m % tm == 0
    has_res = res is not None
    in_specs = list(pro_specs) + [pl.BlockSpec((None, k, tn), lambda i, j: (layer, 0, cb0 + j))]
    args = list(pro_args) + [w]
    if has_res:
        in_specs.append(pl.BlockSpec((tm, tn), lambda i, j: (i, j)))
        args.append(res)
    return pl.pallas_call(
        functools.partial(_mm_kernel, prologue, len(pro_args), has_res),
        grid=(m // tm, n // tn),
        in_specs=in_specs,
        out_specs=pl.BlockSpec((tm, tn), lambda i, j: (i, j)),
        out_shape=jax.ShapeDtypeStruct((m, n), out_dtype),
        scratch_shapes=[pltpu.VMEM((tm, k), BF16)],
        compiler_params=_params("parallel", "arbitrary"),
        name=name,
    )(*args)


def _rms_prologue(x_ref, g_ref):
    return _rms(x_ref[...], g_ref[...])


def _whole(w, layer):
    return (w, layer, 0, w.shape[-1])


def _norm_matmul(x2d, gain, wsel, tm, tn, out_dtype, name, res=None):
    m, k = x2d.shape
    specs = [pl.BlockSpec((tm, k), lambda i, j: (i, 0)), pl.BlockSpec((1, k), lambda i, j: (0, 0))]
    return _fused_matmul(_rms_prologue, [x2d, gain.reshape(1, k)], specs, wsel, res, m, tm, tn, out_dtype, name)


def _identity_prologue(a_ref):
    return a_ref[...]


def _plain_matmul(a, wsel, tm, tn, out_dtype, name, res=None):
    m, k = a.shape
    specs = [pl.BlockSpec((tm, k), lambda i, j: (i, 0))]
    return _fused_matmul(_identity_prologue, [a], specs, wsel, res, m, tm, tn, out_dtype, name)


def _branch_prologue(na_ref, hy_ref, ret_ref, g_ref):
    def nrm(p):
        return p * lax.rsqrt(jnp.mean(p * p, axis=-1, keepdims=True) + RMS_EPS)

    y = jnp.concatenate([nrm(r[...].astype(F32)) for r in (na_ref, hy_ref, ret_ref)], axis=-1)
    return y * g_ref[...]


def _branch_out_proj(y_na, y_hy, y_ret, gain, wsel, res, tm, tn):
    m = y_na.shape[0]
    specs = [
        pl.BlockSpec((tm, NA_W), lambda i, j: (i, 0)),
        pl.BlockSpec((tm, HY_W), lambda i, j: (i, 0)),
        pl.BlockSpec((tm, RET_W), lambda i, j: (i, 0)),
        pl.BlockSpec((1, D_MODEL), lambda i, j: (0, 0)),
    ]
    return _fused_matmul(_branch_prologue, [y_na, y_hy, y_ret, gain.reshape(1, D_MODEL)], specs, wsel, res,
                         m, tm, tn, F32, "branch_out_proj")


NA_PAIR = 2
NA_BAND = NA_KR + NA_PAIR


def _na_band_base(r, rows):
    return np.clip(r - NA_KR // 2, 0, rows - NA_BAND)


@functools.lru_cache(maxsize=None)
def _na_variants(rows):
    assert rows % NA_PAIR == 0 and rows >= NA_BAND + 2
    n_var = NA_KR // 2 + 1
    dr = np.full((n_var, NA_BAND, NA_PAIR), -2, np.int64)
    for r in range(0, rows, NA_PAIR):
        base = int(_na_band_base(r, rows))
        v = (r - base) // 2
        for j in range(NA_PAIR):
            rs = int(np.clip(r + j - NA_KR // 2, 0, rows - NA_KR))
            for i in range(NA_BAND):
                val = base + i - (r + j) + (NA_KR - 1) if rs <= base + i < rs + NA_KR else -1
                assert dr[v, i, j] in (-2, val)
                dr[v, i, j] = val
    assert (dr > -2).all()
    return dr


def _na_bias_table(rpb, rows):
    c = np.arange(GRID_W)
    col_start = np.clip(c - NA_KC // 2, 0, GRID_W - NA_KC)
    col_in = (c[None, :] >= col_start[:, None]) & (c[None, :] < col_start[:, None] + NA_KC)
    dc = np.clip(c[None, :] - c[:, None] + (NA_KC - 1), 0, 2 * NA_KC - 2)
    onehot = (dc.reshape(-1)[None, :] == np.arange(2 * NA_KC - 1)[:, None]).astype(np.float32)
    cols = jnp.einsum("hab,bn->han", rpb.astype(F32), jnp.asarray(onehot), precision=lax.Precision.HIGHEST)
    cols = cols.reshape(NA_HEADS, 2 * NA_KR - 1, GRID_W, GRID_W)
    cols = jnp.where(col_in[None, None], cols, MASK_VALUE).transpose(0, 1, 3, 2)
    masked = jnp.full((NA_HEADS, GRID_W, GRID_W), MASK_VALUE, F32)
    dr = _na_variants(rows)
    variants = []
    for v in range(dr.shape[0]):
        band = [jnp.concatenate([cols[:, dr[v, i, j]] if dr[v, i, j] >= 0 else masked for j in range(NA_PAIR)],
                                axis=-1) for i in range(NA_BAND)]
        variants.append(jnp.concatenate(band, axis=1))
    return jnp.stack(variants, axis=1)


def _na_kernel(rows, q_ref, k_ref, v_ref, bias_ref, o_ref, vt_ref, s0_ref, s1_ref, p0_ref, p1_ref, l0_ref,
               l1_ref):
    dh = NA_HEAD_DIM
    nq = NA_PAIR * GRID_W
    nk = NA_BAND * GRID_W
    n_chunk = nk // 128
    lane = lax.broadcasted_iota(I32, (nq, 2 * dh), 1)

    for ch in range(vt_ref.shape[0]):
        vt_ref[ch] = v_ref[0, ch * 128:(ch + 1) * 128, :].astype(F32).T.astype(BF16)

    n_steps = rows // NA_PAIR

    def band_base(p):
        return jnp.clip(p * NA_PAIR - NA_KR // 2, 0, rows - NA_BAND)

    def scores(p, s_out):
        r = p * NA_PAIR
        base = band_base(p)
        variant = (r - base) // 2
        q = q_ref[0, pl.ds(pl.multiple_of(r * GRID_W, nq), nq), :] * (dh ** -0.5)
        kb = k_ref[0, pl.ds(pl.multiple_of(base * GRID_W, 128), nk), :]
        zero = jnp.zeros_like(q)
        q2 = jnp.concatenate([jnp.where(lane < dh, q, zero), jnp.where(lane >= dh, q, zero)], axis=0)
        st2 = lax.dot_general(kb, q2, NT_DIMS, preferred_element_type=F32)
        for hh in range(2):
            s_out[:, hh * nq:(hh + 1) * nq] = st2[:, hh * nq:(hh + 1) * nq] + bias_ref[hh, variant]

    def softmax(s_in, p_out, l_out):
        st = s_in[...]
        pt = jnp.exp(st - jnp.max(st, axis=0, keepdims=True))
        p_out[...] = pt.astype(BF16)
        l_out[...] = 1.0 / jnp.sum(pt, axis=0, keepdims=True)

    def values(p, p_in, l_in):
        r = p * NA_PAIR
        c0 = band_base(p) // 2
        vt = jnp.concatenate([vt_ref[c0 + i] for i in range(n_chunk)], axis=1)
        pt = p_in[...]
        linv = l_in[...]
        outs = []
        for hh in range(2):
            cols = slice(hh * nq, (hh + 1) * nq)
            ot = jnp.dot(vt[hh * dh:(hh + 1) * dh, :], pt[:, cols], preferred_element_type=F32)
            outs.append(ot * linv[:, cols])
        o_ref[0, pl.ds(pl.multiple_of(r * GRID_W, nq), nq), :] = jnp.concatenate(outs, axis=0).T.astype(BF16)

    s_slots = (s0_ref, s1_ref)
    p_slots = (p0_ref, p1_ref)
    l_slots = (l0_ref, l1_ref)

    def step(t, parity, do_scores=True, do_softmax=True, do_values=True):
        a, b = parity, 1 - parity
        t = jnp.asarray(t, I32)
        if do_scores:
            scores(t, s_slots[a])
        if do_softmax:
            softmax(s_slots[b], p_slots[b], l_slots[b])
        if do_values:
            values(t - 2, p_slots[a], l_slots[a])

    assert n_steps % 2 == 0 and n_steps >= 4
    step(0, 0, do_softmax=False, do_values=False)
    step(1, 1, do_values=False)

    def steady(i, carry):
        t = 2 + 2 * i
        step(t, 0)
        step(t + 1, 1)
        return carry

    lax.fori_loop(0, (n_steps - 2) // 2, steady, 0)
    step(n_steps, 0, do_scores=False)
    step(n_steps + 1, 1, do_scores=False, do_softmax=False)


def _neighbourhood_attention(proj_na, bias_tbl, batch, seq):
    rows = seq // GRID_W
    n_pairs = NA_HEADS // 2
    blk = (1, seq, 2 * NA_HEAD_DIM)
    n_var, nk, nq = bias_tbl.shape[1:]
    return pl.pallas_call(
        functools.partial(_na_kernel, rows),
        grid=(batch, n_pairs),
        in_specs=[
            pl.BlockSpec(blk, lambda b, h: (b, 0, h)),
            pl.BlockSpec(blk, lambda b, h: (b, 0, n_pairs + h)),
            pl.BlockSpec(blk, lambda b, h: (b, 0, 2 * n_pairs + h)),
            pl.BlockSpec((2, n_var, nk, nq), lambda b, h: (h, 0, 0, 0)),
        ],
        out_specs=pl.BlockSpec(blk, lambda b, h: (b, 0, h)),
        out_shape=jax.ShapeDtypeStruct((batch, seq, NA_W), BF16),
        scratch_shapes=[pltpu.VMEM((seq // 128, 2 * NA_HEAD_DIM, 128), BF16),
                        pltpu.VMEM((nk, 2 * nq), F32), pltpu.VMEM((nk, 2 * nq), F32),
                        pltpu.VMEM((nk, 2 * nq), BF16), pltpu.VMEM((nk, 2 * nq), BF16),
                        pltpu.VMEM((1, 2 * nq), F32), pltpu.VMEM((1, 2 * nq), F32)],
        compiler_params=_params("parallel", "arbitrary"),
        name="neighbourhood_attention",
    )(proj_na, proj_na, proj_na, bias_tbl)


@functools.lru_cache(maxsize=None)
def _dft_factors(seq):
    n = 2 * seq
    t = np.arange(seq, dtype=np.int64)
    f1 = np.arange(seq // 64, dtype=np.int64)
    f0 = np.arange(64, dtype=np.int64)
    a = 2.0 * np.pi * ((64 * f1[:, None] * t[None, :]) % n).astype(np.float64) / n
    b = 2.0 * np.pi * ((f0[:, None] * t[None, :]) % n).astype(np.float64) / n
    ny = np.where(t % 2 == 0, 1.0, -1.0)
    return tuple(np.asarray(v, np.float32) for v in (np.cos(a), np.sin(a), np.cos(b), np.sin(b), ny))


def _dft_tables(seq):
    ca, sa, cb, sb, ny = (jnp.asarray(v) for v in _dft_factors(seq))
    cos = (ca[:, None, :] * cb[None] - sa[:, None, :] * sb[None]).reshape(seq, seq)
    sin = (sa[:, None, :] * cb[None] + ca[:, None, :] * sb[None]).reshape(seq, seq)
    sin = jnp.concatenate([ny[None, :], sin[1:]], axis=0)
    fwd = jnp.stack([cos, sin]).astype(BF16)
    inv = jnp.concatenate([cos.T, sin.T], axis=1).astype(BF16)
    return fwd, inv


@functools.lru_cache(maxsize=None)
def _hyena_consts(seq):
    t = np.arange(seq, dtype=np.float64)
    t01 = t / (seq - 1)
    bands = np.linspace(1e-4, HY_BANDS - 1, HY_BANDS)
    ang = (2.0 * math.pi) * (t[:, None] / seq) * bands[None, :]
    feats = np.concatenate([t01[:, None], np.cos(ang), -np.sin(ang)], axis=-1)
    feats_p = np.zeros((seq, 128), np.float32)
    feats_p[:, :HY_POS_DIM] = feats
    min_decay = math.log(1e-2) / 1.5
    max_decay = math.log(1e-2) / 0.3
    deltas = np.abs(np.linspace(min_decay, max_decay, HY_W))
    window = np.exp(-t01[:, None] * deltas[None, :]).astype(np.float32)
    return feats_p, window


def _filter_kernel(feats_ref, w1_ref, b1_ref, w2_ref, b2_ref, freq_ref, w3_ref, win_ref, o_ref, hid_ref):
    hp = lax.Precision.HIGHEST

    @pl.when(pl.program_id(0) == 0)
    def _():
        f = freq_ref[...]
        h1 = jnp.sin(f * (jnp.dot(feats_ref[...], w1_ref[...], precision=hp, preferred_element_type=F32)
                          + b1_ref[...]))
        hid_ref[...] = jnp.sin(f * (jnp.dot(h1, w2_ref[...], precision=hp, preferred_element_type=F32)
                                    + b2_ref[...]))

    h = jnp.dot(hid_ref[...], w3_ref[...], precision=hp, preferred_element_type=F32) * win_ref[...]
    row = lax.broadcasted_iota(I32, h.shape, 0)
    drop = jnp.logical_and(row == 0, pl.program_id(0) >= HY_ORDER)
    o_ref[...] = jnp.where(drop, 0.0, h).astype(BF16)


def _hyena_filters_time(w1, b1, w2, b2, w3, freq, seq):
    feats, window = _hyena_consts(seq)
    w1p = jnp.zeros((128, HY_FILT_FF), F32).at[:HY_POS_DIM].set(w1)
    w3r = w3.reshape(HY_FILT_FF, HY_ORDER, 2, HY_W).transpose(0, 2, 1, 3).reshape(HY_FILT_FF, 4 * HY_W)
    full = lambda shape: pl.BlockSpec(shape, lambda c: (0,) * len(shape))
    return pl.pallas_call(
        _filter_kernel,
        grid=(4,),
        in_specs=[
            full((seq, 128)), full((128, HY_FILT_FF)), full((1, HY_FILT_FF)),
            full((HY_FILT_FF, HY_FILT_FF)), full((1, HY_FILT_FF)), full((1, HY_FILT_FF)),
            pl.BlockSpec((HY_FILT_FF, HY_W), lambda c: (0, c)),
            full((seq, HY_W)),
        ],
        out_specs=pl.BlockSpec((seq, HY_W), lambda c: (0, c)),
        out_shape=jax.ShapeDtypeStruct((seq, 4 * HY_W), BF16),
        scratch_shapes=[pltpu.VMEM((seq, HY_FILT_FF), F32)],
        compiler_params=_params("arbitrary"),
        name="hyena_filter_mlp",
    )(jnp.asarray(feats), w1p, b1.reshape(1, -1), w2, b2.reshape(1, -1), freq.reshape(1, -1), w3r,
      jnp.asarray(window))


def _filter_dft_kernel(f_ref, fwd_ref, bwd_ref, o_ref):
    fwd = fwd_ref[...]
    bwd = bwd_ref[...]
    uc = jnp.dot(f_ref[0], fwd, preferred_element_type=F32)
    vc = jnp.dot(f_ref[0], bwd, preferred_element_type=F32)
    us = jnp.dot(f_ref[1], fwd, preferred_element_type=F32)
    vs = jnp.dot(f_ref[1], bwd, preferred_element_type=F32)
    o_ref[0] = uc + vc
    row = lax.broadcasted_iota(I32, us.shape, 0)
    nyq = jnp.logical_and(row == 0, pl.program_id(0) == 0)
    o_ref[1] = jnp.where(nyq, us + vs, us - vs)


def _filter_dft(dft_fwd, filt_t, seq, fb):
    return pl.pallas_call(
        _filter_dft_kernel,
        grid=(seq // fb, HY_ORDER),
        in_specs=[
            pl.BlockSpec((2, fb, seq), lambda f, o: (0, f, 0)),
            pl.BlockSpec((seq, HY_W), lambda f, o: (0, o)),
            pl.BlockSpec((seq, HY_W), lambda f, o: (0, HY_ORDER + o)),
        ],
        out_specs=pl.BlockSpec((2, fb, HY_W), lambda f, o: (0, f, o)),
        out_shape=jax.ShapeDtypeStruct((2, seq, HY_ORDER * HY_W), F32),
        compiler_params=_params("parallel", "arbitrary"),
        name="hyena_filter_dft",
    )(dft_fwd, filt_t, filt_t)


def _short_conv_kernel(p_ref, w_ref, b_ref, o_ref):
    p = p_ref[0].astype(F32)
    seq = p.shape[0]
    row = lax.broadcasted_iota(I32, p.shape, 0)
    prev = jnp.where(row == 0, 0.0, pltpu.roll(p, 1, 0))
    nxt = jnp.where(row == seq - 1, 0.0, pltpu.roll(p, seq - 1, 0))
    w = w_ref[...]
    o_ref[0] = prev * w[0:1] + p * w[1:2] + nxt * w[2:3] + b_ref[...]


def _short_conv(proj, col0, conv_w, conv_b, batch, seq, tc):
    n_cols = conv_w.shape[-1]
    assert col0 % tc == 0 and n_cols % tc == 0
    cb0 = col0 // tc
    return pl.pallas_call(
        _short_conv_kernel,
        grid=(batch, n_cols // tc),
        in_specs=[
            pl.BlockSpec((1, seq, tc), lambda b, c: (b, 0, cb0 + c)),
            pl.BlockSpec((3, tc), lambda b, c: (0, c)),
            pl.BlockSpec((1, tc), lambda b, c: (0, c)),
        ],
        out_specs=pl.BlockSpec((1, seq, tc), lambda b, c: (b, 0, c)),
        out_shape=jax.ShapeDtypeStruct((batch, seq, n_cols), F32),
        compiler_params=_params("parallel", "arbitrary"),
        name="hyena_short_conv",
    )(proj, conv_w, conv_b.reshape(1, -1))


def _spectrum_kernel(n_fft, f_ref, z_ref, k_ref, o_ref):
    z = z_ref[0].astype(BF16)
    xr = jnp.dot(f_ref[0], z, preferred_element_type=F32)
    xs = jnp.dot(f_ref[1], z, preferred_element_type=F32)
    kr = k_ref[0]
    ks = k_ref[1]
    row = lax.broadcasted_iota(I32, xr.shape, 0)
    edge = jnp.logical_and(row == 0, pl.program_id(0) == 0)
    yr = jnp.where(edge, xr * kr * (1.0 / n_fft), (xr * kr - xs * ks) * (2.0 / n_fft))
    ys = jnp.where(edge, xs * ks * (1.0 / n_fft), (xr * ks + xs * kr) * (2.0 / n_fft))
    o_ref[0, 0] = yr.astype(BF16)
    o_ref[0, 1] = ys.astype(BF16)


def _spectrum_product(dft_fwd, z_arr, z_col, kfreq, order, batch, seq, fb):
    return pl.pallas_call(
        functools.partial(_spectrum_kernel, 2 * seq),
        grid=(seq // fb, batch),
        in_specs=[
            pl.BlockSpec((2, fb, seq), lambda f, b: (0, f, 0)),
            pl.BlockSpec((1, seq, HY_W), lambda f, b: (b, 0, z_col)),
            pl.BlockSpec((2, fb, HY_W), lambda f, b: (0, f, order)),
        ],
        out_specs=pl.BlockSpec((1, 2, fb, HY_W), lambda f, b: (b, 0, f, 0)),
        out_shape=jax.ShapeDtypeStruct((batch, 2, seq, HY_W), BF16),
        compiler_params=_params("parallel", "arbitrary"),
        name="hyena_spectrum",
    )(dft_fwd, z_arr, kfreq)


def _inverse_kernel(ft_ref, y_ref, gate_ref, z_ref, d_ref, o_ref):
    conv = jnp.dot(ft_ref[...], y_ref[0], preferred_element_type=F32)
    o_ref[0] = gate_ref[0] * (conv + d_ref[...] * z_ref[0])


def _inverse_gate(dft_inv, y, gate_arr, gate_col, z_arr, z_col, skip_row, batch, seq, tb):
    y2 = y.reshape(batch, 2 * seq, HY_W)
    return pl.pallas_call(
        _inverse_kernel,
        grid=(seq // tb, batch),
        in_specs=[
            pl.BlockSpec((tb, 2 * seq), lambda t, b: (t, 0)),
            pl.BlockSpec((1, 2 * seq, HY_W), lambda t, b: (b, 0, 0)),
            pl.BlockSpec((1, tb, HY_W), lambda t, b: (b, t, gate_col)),
            pl.BlockSpec((1, tb, HY_W), lambda t, b: (b, t, z_col)),
            pl.BlockSpec((1, HY_W), lambda t, b: (0, 0)),
        ],
        out_specs=pl.BlockSpec((1, tb, HY_W), lambda t, b: (b, t, 0)),
        out_shape=jax.ShapeDtypeStruct((batch, seq, HY_W), F32),
        compiler_params=_params("parallel", "arbitrary"),
        name="hyena_inverse_gate",
    )(dft_inv, y2, gate_arr, z_arr, skip_row.reshape(1, HY_W))


def _hyena_mixer(proj, col0, conv_w, conv_b, w1, b1, w2, b2, w3, freq, skip_d, dft_fwd, dft_inv, batch, seq):
    s = _short_conv(proj, col0, conv_w, conv_b, batch, seq, 256)
    filt_t = _hyena_filters_time(w1, b1, w2, b2, w3, freq, seq)
    kfreq = _filter_dft(dft_fwd, filt_t, seq, 512)
    z_arr, z_col = s, 2
    for o in range(HY_ORDER):
        y = _spectrum_product(dft_fwd, z_arr, z_col, kfreq, o, batch, seq, 512)
        z_arr = _inverse_gate(dft_inv, y, s, o, z_arr, z_col, skip_d[o], batch, seq, 512)
        z_col = 0
    return z_arr


@functools.lru_cache(maxsize=None)
def _retention_consts(seq):
    c = RET_CHUNK
    half = RET_HEAD_DIM // 2
    inv = 1.0 / (10000.0 ** np.linspace(0.0, 1.0, half))
    ang = np.arange(seq, dtype=np.float64)[:, None] * inv[None, :]
    cos2 = np.concatenate([np.cos(ang), np.cos(ang)], axis=-1).astype(np.float32)
    sin2 = np.concatenate([-np.sin(ang), np.sin(ang)], axis=-1).astype(np.float32)
    hidx = np.arange(RET_HEADS, dtype=np.float64)
    lg_f = np.log1p(-np.exp2(-5.0 - hidx))[:, None, None]
    lg_b = np.log1p(-np.exp2(-5.5 - hidx))[:, None, None]
    i = np.arange(c, dtype=np.float64)
    diff = i[:, None] - i[None, :]
    ones = np.ones((1, c, c))
    dec = np.where(diff >= 0, np.exp(lg_f * np.maximum(diff, 0.0)), np.exp(lg_b * np.maximum(-diff, 0.0)))
    rowv = lambda v: v[:, :, None] * ones
    tab = np.stack([
        dec,
        rowv(np.exp(lg_f[:, :, 0] * (i + 1.0)[None, :])),
        rowv(np.exp(lg_f[:, :, 0] * (c - 1.0 - i)[None, :])),
        rowv(np.exp(lg_b[:, :, 0] * (c - i)[None, :])),
        rowv(np.exp(lg_b[:, :, 0] * i[None, :])),
        np.exp(lg_f * c) * ones,
        np.exp(lg_b * c) * ones,
    ], axis=1).astype(np.float32)
    return cos2, sin2, tab


def _retention_kernel(q_ref, k_ref, v_ref, g_ref, cos_ref, sin_ref, tab_ref, o_ref, qs_ref, ks_ref, kvf_ref,
                      kvb_ref):
    c = RET_CHUNK
    d = RET_HEAD_DIM
    seq = q_ref.shape[1]
    n_chunks = seq // c
    cos = cos_ref[...]
    sin = sin_ref[...]
    q = q_ref[0].astype(F32)
    k = k_ref[0].astype(F32)
    qs_ref[...] = (q * cos + pltpu.roll(q, d // 2, 1) * sin) * (d ** -0.5)
    ks_ref[...] = k * cos + pltpu.roll(k, d // 2, 1) * sin

    def mm(a, b):
        return jnp.dot(a.astype(BF16), b.astype(BF16), preferred_element_type=F32)

    def chunk(n):
        return pl.ds(pl.multiple_of(n * c, c), c)

    def kv_body(n, carry):
        kc = ks_ref[chunk(n), :]
        vc = v_ref[0, chunk(n), :]
        kvf_ref[n] = mm((kc * tab_ref[0, 2]).T, vc)
        kvb_ref[n] = mm((kc * tab_ref[0, 4]).T, vc)
        return carry

    lax.fori_loop(0, n_chunks, kv_body, 0, unroll=4)

    def scan_fwd(n, state):
        kv = kvf_ref[n]
        kvf_ref[n] = state
        return tab_ref[0, 5] * state + kv

    lax.fori_loop(0, n_chunks, scan_fwd, jnp.zeros((d, d), F32))

    def scan_bwd(m, state):
        n = n_chunks - 1 - m
        kv = kvb_ref[n]
        kvb_ref[n] = state
        return tab_ref[0, 6] * state + kv

    lax.fori_loop(0, n_chunks, scan_bwd, jnp.zeros((d, d), F32))

    def out_body(n, carry):
        qc = qs_ref[chunk(n), :]
        kc = ks_ref[chunk(n), :]
        vc = v_ref[0, chunk(n), :]
        a = lax.dot_general(qc.astype(BF16), kc.astype(BF16), NT_DIMS, preferred_element_type=F32) * tab_ref[0, 0]
        y = mm(a, vc) + mm(qc * tab_ref[0, 1], kvf_ref[n]) + mm(qc * tab_ref[0, 3], kvb_ref[n])
        mu = jnp.mean(y, axis=-1, keepdims=True)
        yc = y - mu
        var = jnp.mean(yc * yc, axis=-1, keepdims=True)
        g = g_ref[0, chunk(n), :].astype(F32)
        o_ref[0, chunk(n), :] = (yc * lax.rsqrt(var + GN_EPS) * (g * jax.nn.sigmoid(g))).astype(BF16)
        return carry

    lax.fori_loop(0, n_chunks, out_body, 0, unroll=4)


def _retention_mixer(proj, col0, batch, seq):
    cos2, sin2, tab = _retention_consts(seq)
    blk = (1, seq, RET_HEAD_DIM)
    h_ = RET_HEADS
    assert col0 % RET_HEAD_DIM == 0
    c0 = col0 // RET_HEAD_DIM
    return pl.pallas_call(
        _retention_kernel,
        grid=(batch, RET_HEADS),
        in_specs=[
            pl.BlockSpec(blk, lambda b, h: (b, 0, c0 + h)),
            pl.BlockSpec(blk, lambda b, h: (b, 0, c0 + h_ + h)),
            pl.BlockSpec(blk, lambda b, h: (b, 0, c0 + 2 * h_ + h)),
            pl.BlockSpec(blk, lambda b, h: (b, 0, c0 + 3 * h_ + h)),
            pl.BlockSpec((seq, RET_HEAD_DIM), lambda b, h: (0, 0)),
            pl.BlockSpec((seq, RET_HEAD_DIM), lambda b, h: (0, 0)),
            pl.BlockSpec((1, 7, RET_CHUNK, RET_CHUNK), lambda b, h: (h, 0, 0, 0)),
        ],
        out_specs=pl.BlockSpec(blk, lambda b, h: (b, 0, h)),
        out_shape=jax.ShapeDtypeStruct((batch, seq, RET_W), BF16),
        scratch_shapes=[pltpu.VMEM((seq, RET_HEAD_DIM), F32), pltpu.VMEM((seq, RET_HEAD_DIM), F32),
                        pltpu.VMEM((seq // RET_CHUNK, RET_HEAD_DIM, RET_HEAD_DIM), F32),
                        pltpu.VMEM((seq // RET_CHUNK, RET_HEAD_DIM, RET_HEAD_DIM), F32)],
        compiler_params=_params("parallel", "arbitrary"),
        name="retention",
    )(proj, proj, proj, proj, jnp.asarray(cos2), jnp.asarray(sin2), jnp.asarray(tab))


def _cross_kernel(q_ref, k_ref, v_ref, o_ref):
    dh = CROSS_HEAD_DIM
    for h in range(CROSS_HEADS):
        sl = slice(h * dh, (h + 1) * dh)
        s = lax.dot_general(q_ref[:, sl], k_ref[:, sl], NT_DIMS, preferred_element_type=F32) * (dh ** -0.5)
        m = jnp.max(s, axis=-1, keepdims=True)
        p = jnp.exp(s - m)
        l = jnp.sum(p, axis=-1, keepdims=True)
        o = jnp.dot(p.astype(BF16), v_ref[:, sl], preferred_element_type=F32) * (1.0 / l)
        o_ref[:, sl] = o.astype(o_ref.dtype)


def _cross_attention(q, kv, batch, seq, n_mem, tq):
    per_b = seq // tq
    return pl.pallas_call(
        _cross_kernel,
        grid=(batch, per_b),
        in_specs=[
            pl.BlockSpec((tq, D_MODEL), lambda b, i: (b * per_b + i, 0)),
            pl.BlockSpec((n_mem, D_MODEL), lambda b, i: (b, 0)),
            pl.BlockSpec((n_mem, D_MODEL), lambda b, i: (b, 1)),
        ],
        out_specs=pl.BlockSpec((tq, D_MODEL), lambda b, i: (b * per_b + i, 0)),
        out_shape=jax.ShapeDtypeStruct((batch * seq, D_MODEL), BF16),
        compiler_params=_params("parallel", "arbitrary"),
        name="cross_attention",
    )(q, kv, kv)


def _router_kernel(x_ref, g_ref, wr_ref, hm_ref, aff_ref):
    n_e = aff_ref.shape[1]
    h = _rms(x_ref[...], g_ref[...])
    h_hi = h.astype(BF16)
    hm_ref[...] = h_hi
    h_lo = (h - h_hi.astype(F32)).astype(BF16)
    w = wr_ref[...]
    w_hi = w.astype(BF16)
    w_lo = (w - w_hi.astype(F32)).astype(BF16)
    logits = (jnp.dot(h_hi, w_hi, preferred_element_type=F32) + jnp.dot(h_lo, w_hi, preferred_element_type=F32)
              + jnp.dot(h_hi, w_lo, preferred_element_type=F32)).T[:n_e]
    m = jnp.max(logits, axis=0, keepdims=True)
    e = jnp.exp(logits - m)
    aff_ref[0] = e / jnp.sum(e, axis=0, keepdims=True)


def _router(x2d, gain, w_router, batch, seq, tm):
    m = x2d.shape[0]
    per_b = seq // tm
    w_pad = jnp.zeros((D_MODEL, 128), F32).at[:, :N_EXPERTS].set(w_router)
    return pl.pallas_call(
        _router_kernel,
        grid=(m // tm,),
        in_specs=[
            pl.BlockSpec((tm, D_MODEL), lambda i: (i, 0)),
            pl.BlockSpec((1, D_MODEL), lambda i: (0, 0)),
            pl.BlockSpec((D_MODEL, 128), lambda i: (0, 0)),
        ],
        out_specs=[
            pl.BlockSpec((tm, D_MODEL), lambda i: (i, 0)),
            pl.BlockSpec((1, N_EXPERTS, tm), lambda i: (i // per_b, 0, i % per_b)),
        ],
        out_shape=[
            jax.ShapeDtypeStruct((m, D_MODEL), BF16),
            jax.ShapeDtypeStruct((batch, N_EXPERTS, seq), F32),
        ],
        compiler_params=_params("parallel"),
        name="moe_router",
    )(x2d, gain.reshape(1, D_MODEL), w_pad)


def _sort_descending(x):
    rows, n = x.shape
    lanes = 128
    n_chunks = n // lanes
    chunks = [x[:, c * lanes:(c + 1) * lanes] for c in range(n_chunks)]
    lane = lax.broadcasted_iota(I32, (rows, lanes), 1)
    k = 2
    while k <= n:
        j = k // 2
        while j >= 1:
            nxt = []
            for c, xc in enumerate(chunks):
                if j >= lanes:
                    lo = (c & (j // lanes)) == 0
                    partner = chunks[c ^ (j // lanes)]
                else:
                    lo = (lane & j) == 0
                    partner = jnp.where(lo, pltpu.roll(xc, lanes - j, 1), pltpu.roll(xc, j, 1))
                desc = ((c * lanes) & k) == 0 if k >= lanes else (lane & k) == 0
                hi, lw = jnp.maximum(xc, partner), jnp.minimum(xc, partner)
                if j >= lanes and k >= lanes:
                    nxt.append(hi if lo == desc else lw)
                else:
                    nxt.append(jnp.where(lo == desc, hi, lw))
            chunks = nxt
            j //= 2
        k *= 2
    return jnp.concatenate(chunks, axis=1)


def _topk_kernel(cap, n_e, aff_ref, slot_ref, wsel_ref, slot_t_ref, thr_ref):
    n_rows, n_tok = aff_ref.shape

    def thr_body(i, carry):
        rows8 = pl.ds(pl.multiple_of(i * 8, 8), 8)
        thr_ref[rows8, :] = _sort_descending(aff_ref[rows8, :])[:, cap - 1:cap]
        return carry

    lax.fori_loop(0, n_rows // 8, thr_body, 0)
    a = aff_ref[...]
    thr = thr_ref[...]
    gt = a > thr
    eq = a == thr
    need = cap - jnp.sum(gt.astype(I32), axis=1, keepdims=True)
    upper = (lax.broadcasted_iota(I32, (n_tok, n_tok), 0) < lax.broadcasted_iota(I32, (n_tok, n_tok), 1))
    upper = upper.astype(BF16)
    eq_rank = jnp.dot(eq.astype(BF16), upper, preferred_element_type=F32)
    sel = jnp.logical_or(gt, jnp.logical_and(eq, eq_rank < need.astype(F32)))
    rank = jnp.dot(sel.astype(BF16), upper, preferred_element_type=F32)
    slot = jnp.where(sel, rank, -1.0)
    slot_ref[...] = slot.astype(I32)
    wsel_ref[...] = jnp.where(sel, a, 0.0)
    pad = jnp.full((128 - n_e, n_tok), -1.0, F32)
    for b in range(n_rows // n_e):
        slot_t_ref[b] = jnp.concatenate([slot[b * n_e:(b + 1) * n_e], pad], axis=0).T.astype(I32)


def _topk_select(aff2d, cap, n_e):
    rows, n_tok = aff2d.shape
    spec = pl.BlockSpec((rows, n_tok), lambda i: (0, 0))
    spec_t = pl.BlockSpec((rows // n_e, n_tok, 128), lambda i: (0, 0, 0))
    return pl.pallas_call(
        functools.partial(_topk_kernel, cap, n_e),
        grid=(1,),
        in_specs=[spec],
        out_specs=[spec, spec, spec_t],
        out_shape=[jax.ShapeDtypeStruct((rows, n_tok), I32), jax.ShapeDtypeStruct((rows, n_tok), F32),
                   jax.ShapeDtypeStruct((rows // n_e, n_tok, 128), I32)],
        scratch_shapes=[pltpu.VMEM((rows, 1), F32)],
        compiler_params=_params("arbitrary"),
        name="moe_topk_select",
    )(aff2d)


def _gather_kernel(cap, slot_ref, wsel_ref, hm_ref, xe_ref, gs_ref):
    n_tok = hm_ref.shape[0]
    onehot = slot_ref[0] == lax.broadcasted_iota(I32, (cap, n_tok), 0)
    xe_ref[0] = jnp.dot(onehot.astype(BF16), hm_ref[...], preferred_element_type=F32).astype(BF16)
    gs_ref[0] = jnp.sum(jnp.where(onehot, wsel_ref[0], 0.0), axis=1, keepdims=True)


def _gather_tokens(slot, wsel, hm, batch, seq, cap):
    rows = batch * N_EXPERTS
    row_spec = pl.BlockSpec((1, 1, seq), lambda b, e: (b * N_EXPERTS + e, 0, 0))
    return pl.pallas_call(
        functools.partial(_gather_kernel, cap),
        grid=(batch, N_EXPERTS),
        in_specs=[row_spec, row_spec, pl.BlockSpec((seq, D_MODEL), lambda b, e: (b, 0))],
        out_specs=[
            pl.BlockSpec((1, cap, D_MODEL), lambda b, e: (e, b, 0)),
            pl.BlockSpec((1, cap, 1), lambda b, e: (e, b, 0)),
        ],
        out_shape=[
            jax.ShapeDtypeStruct((N_EXPERTS, batch * cap, D_MODEL), BF16),
            jax.ShapeDtypeStruct((N_EXPERTS, batch * cap, 1), F32),
        ],
        compiler_params=_params("parallel", "arbitrary"),
        name="moe_gather",
    )(slot.reshape(rows, 1, seq), wsel.reshape(rows, 1, seq), hm)


def _expert_kernel(n_e, n_t, xe_ref, wg_ref, wu_ref, wd_ref, gs_ref, ye_ref, mid_ref):
    g = pl.program_id(0)
    s = pl.program_id(1)
    t = s // 2
    tf = wg_ref.shape[-1]

    @pl.when(jnp.logical_and(s % 2 == 0, g < n_e))
    def _():
        x = xe_ref[0]
        a = jnp.dot(x, wg_ref[...].astype(BF16), preferred_element_type=F32)
        u = jnp.dot(x, wu_ref[...].astype(BF16), preferred_element_type=F32)
        mid_ref[g % 2, t] = (a * jax.nn.sigmoid(a) * u).astype(BF16)

    @pl.when(jnp.logical_and(s % 2 == 1, g >= 1))
    def _():
        prev = (g - 1) % 2
        acc = jnp.dot(mid_ref[prev, 0], wd_ref[0:tf, :].astype(BF16), preferred_element_type=F32)
        for f in range(1, n_t):
            acc += jnp.dot(mid_ref[prev, f], wd_ref[f * tf:(f + 1) * tf, :].astype(BF16),
                           preferred_element_type=F32)
        ye_ref[0] = (acc * gs_ref[0]).astype(BF16)


def _expert_ffn(xe, gs, w_gate, w_up, w_down, layer, tf):
    n_e, rows, d = xe.shape
    ff = w_gate.shape[-1]
    n_t = ff // tf
    assert d // tf == n_t
    last = n_e - 1

    n_s = 2 * n_t

    def fill_tile(g, s, lead):
        v = jnp.minimum(g * n_s + s + lead, n_e * n_s - 1)
        return (layer, v // n_s, 0, (v % n_s) // 2)

    def emit_tile(g, s):
        v = jnp.maximum(g * n_s + s - 1 - n_s, 0)
        return (layer, v // n_s, 0, (v % n_s) // 2)

    return pl.pallas_call(
        functools.partial(_expert_kernel, n_e, n_t),
        grid=(n_e + 1, n_s),
        in_specs=[
            pl.BlockSpec((1, rows, d), lambda g, s: (jnp.minimum(g, last), 0, 0)),
            pl.BlockSpec((None, None, d, tf), functools.partial(fill_tile, lead=1)),
            pl.BlockSpec((None, None, d, tf), functools.partial(fill_tile, lead=0)),
            pl.BlockSpec((None, None, ff, tf), emit_tile),
            pl.BlockSpec((1, rows, 1), lambda g, s: (jnp.maximum(g - 1, 0), 0, 0)),
        ],
        out_specs=pl.BlockSpec((1, rows, tf), lambda g, s: emit_tile(g, s)[1:]),
        out_shape=jax.ShapeDtypeStruct((n_e, rows, d), BF16),
        scratch_shapes=[pltpu.VMEM((2, n_t, rows, tf), BF16)],
        compiler_params=_params("arbitrary", "arbitrary"),
        name="moe_expert_ffn",
    )(xe, w_gate, w_up, w_down, gs)


def _scatter_kernel(cap, slot_ref, ye_ref, x_ref, o_ref, onehot_ref):
    n_e = ye_ref.shape[0]
    n_tok = x_ref.shape[0]

    @pl.when(pl.program_id(1) == 0)
    def _():
        col = lax.broadcasted_iota(I32, (n_tok, cap), 1)
        slots = slot_ref[0]
        for e in range(n_e):
            onehot_ref[:, e * cap:(e + 1) * cap] = (slots[:, e:e + 1] == col).astype(BF16)

    ye = ye_ref[...].reshape(n_e * cap, ye_ref.shape[-1])
    o_ref[...] = x_ref[...] + jnp.dot(onehot_ref[...], ye, preferred_element_type=F32)


def _scatter_add(slot_t, ye, x2d, batch, seq, cap, tn):
    return pl.pallas_call(
        functools.partial(_scatter_kernel, cap),
        grid=(batch, D_MODEL // tn),
        in_specs=[
            pl.BlockSpec((1, seq, 128), lambda b, n: (b, 0, 0)),
            pl.BlockSpec((N_EXPERTS, cap, tn), lambda b, n: (0, b, n)),
            pl.BlockSpec((seq, tn), lambda b, n: (b, n)),
        ],
        out_specs=pl.BlockSpec((seq, tn), lambda b, n: (b, n)),
        out_shape=jax.ShapeDtypeStruct(x2d.shape, F32),
        scratch_shapes=[pltpu.VMEM((seq, N_EXPERTS * cap), BF16)],
        compiler_params=_params("parallel", "arbitrary"),
        name="moe_scatter_add",
    )(slot_t, ye, x2d)


def _expert_choice_ffn(x2d, gain, w_router, w_gate, w_up, w_down, layer, batch, seq):
    cap = EC_CAPACITY * seq // N_EXPERTS
    hm, aff = _router(x2d, gain, w_router, batch, seq, 512)
    slot, wsel, slot_t = _topk_select(aff.reshape(batch * N_EXPERTS, seq), cap, N_EXPERTS)
    xe, gs = _gather_tokens(slot, wsel, hm, batch, seq, cap)
    ye = _expert_ffn(xe, gs, w_gate, w_up, w_down, layer, 512)
    return _scatter_add(slot_t, ye, x2d, batch, seq, cap, 512)


def _final_norm_kernel(x_ref, g_ref, o_ref):
    o_ref[...] = _rms(x_ref[...], g_ref[...])


def _final_norm(x2d, gain, tm):
    m, d = x2d.shape
    return pl.pallas_call(
        _final_norm_kernel,
        grid=(m // tm,),
        in_specs=[pl.BlockSpec((tm, d), lambda i: (i, 0)), pl.BlockSpec((1, d), lambda i: (0, 0))],
        out_specs=pl.BlockSpec((tm, d), lambda i: (i, 0)),
        out_shape=jax.ShapeDtypeStruct((m, d), F32),
        compiler_params=_params("parallel"),
        name="final_norm",
    )(x2d, gain.reshape(1, d))


def kernel(x, mem, norm_mix, w_in, na_rpb, hy_conv_w, hy_conv_b, hy_filt_w1, hy_filt_b1, hy_filt_w2, hy_filt_b2, hy_filt_w3, hy_sin_freq, hy_skip_d, branch_norm, w_out, norm_cross, mem_norm, w_cq, w_ckv, w_co, norm_moe, w_router, w_gate, w_up, w_down, final_norm):
    batch, seq, d = x.shape
    n_mem = mem.shape[1]
    depth = w_in.shape[0]
    m = batch * seq
    na_cols = 3 * NA_W
    hy_cols = 3 * HY_W
    rows = seq // GRID_W

    dft_fwd, dft_inv = _dft_tables(seq)
    x2d = x.reshape(m, d)
    mem2d = mem.reshape(batch * n_mem, d)

    w_in_b, w_out_b, w_cq_b, w_co_b = (w.astype(BF16) for w in (w_in, w_out, w_cq, w_co))
    p_in = w_in.shape[-1]

    for l in range(depth):
        proj = _norm_matmul(x2d, norm_mix[l], _whole(w_in_b, l), 1024, 768, BF16, "in_proj")
        proj = proj.reshape(batch, seq, p_in)

        y_na = _neighbourhood_attention(proj, _na_bias_table(na_rpb[l], rows), batch, seq)
        y_hy = _hyena_mixer(proj, na_cols, hy_conv_w[l], hy_conv_b[l], hy_filt_w1[l],
                            hy_filt_b1[l], hy_filt_w2[l], hy_filt_b2[l], hy_filt_w3[l], hy_sin_freq[l],
                            hy_skip_d[l], dft_fwd, dft_inv, batch, seq)
        y_ret = _retention_mixer(proj, na_cols + hy_cols, batch, seq)
        x2d = _branch_out_proj(y_na.reshape(m, NA_W), y_hy.reshape(m, HY_W), y_ret.reshape(m, RET_W),
                               branch_norm[l], _whole(w_out_b, l), x2d, 1024, 1024)

        q = _norm_matmul(x2d, norm_cross[l], _whole(w_cq_b, l), 1024, 1024, BF16, "cross_q_proj")
        kv = _norm_matmul(mem2d, mem_norm, _whole(w_ckv, l), batch * n_mem, 1024, BF16, "cross_kv_proj")
        o = _cross_attention(q, kv, batch, seq, n_mem, 512)
        x2d = _plain_matmul(o, _whole(w_co_b, l), 1024, 1024, F32, "cross_out_proj", res=x2d)

        x2d = _expert_choice_ffn(x2d, norm_moe[l], w_router[l], w_gate, w_up, w_down, l, batch, seq)

    return _final_norm(x2d, final_norm, 512).reshape(batch, seq, d)
```

```python
import functools
import math

import numpy as np
import jax
import jax.numpy as jnp
from jax import lax
from jax.experimental import pallas as pl
from jax.experimental.pallas import tpu as pltpu

F32 = jnp.float32
BF16 = jnp.bfloat16
I32 = jnp.int32

D_MODEL = 2048
GRID_W = 64
NA_HEAD_DIM = 64
NA_W = 768
NA_HEADS = 12
NA_KR = 8
NA_KC = 16
HY_W = 512
HY_ORDER = 2
HY_BANDS = 8
HY_POS_DIM = 17
HY_FILT_FF = 64
RET_HEAD_DIM = 128
RET_W = 768
RET_HEADS = 6
RET_CHUNK = 128
CROSS_HEADS = 4
CROSS_HEAD_DIM = 512
N_EXPERTS = 16
EXPERT_FF = 2048
EC_CAPACITY = 2
RMS_EPS = 1e-6
GN_EPS = 1e-5

MASK_VALUE = -1e30
VMEM_LIMIT_BYTES = 56 * 1024 * 1024

NT_DIMS = (((1,), (1,)), ((), ()))


def _params(*sem):
    return pltpu.CompilerParams(dimension_semantics=sem, vmem_limit_bytes=VMEM_LIMIT_BYTES)


def _rms(xf, g):
    return xf * lax.rsqrt(jnp.mean(xf * xf, axis=-1, keepdims=True) + RMS_EPS) * g


def _mm_kernel(prologue, n_pro, has_res, *refs):
    pro_refs = refs[:n_pro]
    w_ref = refs[n_pro]
    res_ref = refs[n_pro + 1] if has_res else None
    o_ref = refs[n_pro + 1 + has_res]
    h_ref = refs[n_pro + 2 + has_res]

    @pl.when(pl.program_id(1) == 0)
    def _():
        h_ref[...] = prologue(*pro_refs).astype(BF16)

    w = w_ref[...]
    if w.dtype != BF16:
        w = w.astype(BF16)
    acc = jnp.dot(h_ref[...], w, preferred_element_type=F32)
    if has_res:
        acc = acc + res_ref[...]
    o_ref[...] = acc.astype(o_ref.dtype)


def _fused_matmul(prologue, pro_args, pro_specs, wsel, res, m, tm, tn, out_dtype, name):
    w, layer, col0, n = wsel
    k = w.shape[1]
    cb0 = col0 // tn
    assert col0 % tn == 0 and n % tn == 0 and m % tm == 0
    has_res = res is not None
    in_specs = list(pro_specs) + [pl.BlockSpec((None, k, tn), lambda i, j: (layer, 0, cb0 + j))]
    args = list(pro_args) + [w]
    if has_res:
        in_specs.append(pl.BlockSpec((tm, tn), lambda i, j: (i, j)))
        args.append(res)
    return pl.pallas_call(
        functools.partial(_mm_kernel, prologue, len(pro_args), has_res),
        grid=(m // tm, n // tn),
        in_specs=in_specs,
        out_specs=pl.BlockSpec((tm, tn), lambda i, j: (i, j)),
        out_shape=jax.ShapeDtypeStruct((m, n), out_dtype),
        scratch_shapes=[pltpu.VMEM((tm, k), BF16)],
        compiler_params=_params("parallel", "arbitrary"),
        name=name,
    )(*args)


def _rms_prologue(x_ref, g_ref):
    return _rms(x_ref[...], g_ref[...])


def _whole(w, layer):
    return (w, layer, 0, w.shape[-1])


def _norm_matmul(x2d, gain, wsel, tm, tn, out_dtype, name, res=None):
    m, k = x2d.shape
    specs = [pl.BlockSpec((tm, k), lambda i, j: (i, 0)), pl.BlockSpec((1, k), lambda i, j: (0, 0))]
    return _fused_matmul(_rms_prologue, [x2d, gain.reshape(1, k)], specs, wsel, res, m, tm, tn, out_dtype, name)


def _identity_prologue(a_ref):
    return a_ref[...]


def _plain_matmul(a, wsel, tm, tn, out_dtype, name, res=None):
    m, k = a.shape
    specs = [pl.BlockSpec((tm, k), lambda i, j: (i, 0))]
    return _fused_matmul(_identity_prologue, [a], specs, wsel, res, m, tm, tn, out_dtype, name)


def _branch_prologue(na_ref, hy_ref, ret_ref, g_ref):
    def nrm(p):
        return p * lax.rsqrt(jnp.mean(p * p, axis=-1, keepdims=True) + RMS_EPS)

    y = jnp.concatenate([nrm(r[...].astype(F32)) for r in (na_ref, hy_ref, ret_ref)], axis=-1)
    return y * g_ref[...]


def _branch_out_proj(y_na, y_hy, y_ret, gain, wsel, res, tm, tn):
    m = y_na.shape[0]
    specs = [
        pl.BlockSpec((tm, NA_W), lambda i, j: (i, 0)),
        pl.BlockSpec((tm, HY_W), lambda i, j: (i, 0)),
        pl.BlockSpec((tm, RET_W), lambda i, j: (i, 0)),
        pl.BlockSpec((1, D_MODEL), lambda i, j: (0, 0)),
    ]
    return _fused_matmul(_branch_prologue, [y_na, y_hy, y_ret, gain.reshape(1, D_MODEL)], specs, wsel, res,
                         m, tm, tn, F32, "branch_out_proj")


NA_PAIR = 2
NA_BAND = NA_KR + NA_PAIR


def _na_band_base(r, rows):
    return np.clip(r - NA_KR // 2, 0, rows - NA_BAND)


@functools.lru_cache(maxsize=None)
def _na_variants(rows):
    assert rows % NA_PAIR == 0 and rows >= NA_BAND + 2
    n_var = NA_KR // 2 + 1
    dr = np.full((n_var, NA_BAND, NA_PAIR), -2, np.int64)
    for r in range(0, rows, NA_PAIR):
        base = int(_na_band_base(r, rows))
        v = (r - base) // 2
        for j in range(NA_PAIR):
            rs = int(np.clip(r + j - NA_KR // 2, 0, rows - NA_KR))
            for i in range(NA_BAND):
                val = base + i - (r + j) + (NA_KR - 1) if rs <= base + i < rs + NA_KR else -1
                assert dr[v, i, j] in (-2, val)
                dr[v, i, j] = val
    assert (dr > -2).all()
    return dr


def _na_bias_table(rpb, rows):
    c = np.arange(GRID_W)
    col_start = np.clip(c - NA_KC // 2, 0, GRID_W - NA_KC)
    col_in = (c[None, :] >= col_start[:, None]) & (c[None, :] < col_start[:, None] + NA_KC)
    dc = np.clip(c[None, :] - c[:, None] + (NA_KC - 1), 0, 2 * NA_KC - 2)
    onehot = (dc.reshape(-1)[None, :] == np.arange(2 * NA_KC - 1)[:, None]).astype(np.float32)
    cols = jnp.einsum("hab,bn->han", rpb.astype(F32), jnp.asarray(onehot), precision=lax.Precision.HIGHEST)
    cols = cols.reshape(NA_HEADS, 2 * NA_KR - 1, GRID_W, GRID_W)
    cols = jnp.where(col_in[None, None], cols, MASK_VALUE).transpose(0, 1, 3, 2)
    masked = jnp.full((NA_HEADS, GRID_W, GRID_W), MASK_VALUE, F32)
    dr = _na_variants(rows)
    variants = []
    for v in range(dr.shape[0]):
        band = [jnp.concatenate([cols[:, dr[v, i, j]] if dr[v, i, j] >= 0 else masked for j in range(NA_PAIR)],
                                axis=-1) for i in range(NA_BAND)]
        variants.append(jnp.concatenate(band, axis=1))
    return jnp.stack(variants, axis=1)


def _na_kernel(rows, q_ref, k_ref, v_ref, bias_ref, o_ref, vt_ref, s0_ref, s1_ref, p0_ref, p1_ref, l0_ref,
               l1_ref):
    dh = NA_HEAD_DIM
    nq = NA_PAIR * GRID_W
    nk = NA_BAND * GRID_W
    n_chunk = nk // 128
    lane = lax.broadcasted_iota(I32, (nq, 2 * dh), 1)

    for ch in range(vt_ref.shape[0]):
        vt_ref[ch] = v_ref[0, ch * 128:(ch + 1) * 128, :].astype(F32).T.astype(BF16)

    n_steps = rows // NA_PAIR

    def band_base(p):
        return jnp.clip(p * NA_PAIR - NA_KR // 2, 0, rows - NA_BAND)

    def scores(p, s_out):
        r = p * NA_PAIR
        base = band_base(p)
        variant = (r - base) // 2
        q = q_ref[0, pl.ds(pl.multiple_of(r * GRID_W, nq), nq), :] * (dh ** -0.5)
        kb = k_ref[0, pl.ds(pl.multiple_of(base * GRID_W, 128), nk), :]
        zero = jnp.zeros_like(q)
        q2 = jnp.concatenate([jnp.where(lane < dh, q, zero), jnp.where(lane >= dh, q, zero)], axis=0)
        st2 = lax.dot_general(kb, q2, NT_DIMS, preferred_element_type=F32)
        for hh in range(2):
            s_out[:, hh * nq:(hh + 1) * nq] = st2[:, hh * nq:(hh + 1) * nq] + bias_ref[hh, variant]

    def softmax(s_in, p_out, l_out):
        st = s_in[...]
        pt = jnp.exp(st - jnp.max(st, axis=0, keepdims=True))
        p_out[...] = pt.astype(BF16)
        l_out[...] = 1.0 / jnp.sum(pt, axis=0, keepdims=True)

    def values(p, p_in, l_in):
        r = p * NA_PAIR
        c0 = band_base(p) // 2
        vt = jnp.concatenate([vt_ref[c0 + i] for i in range(n_chunk)], axis=1)
        pt = p_in[...]
        linv = l_in[...]
        outs = []
        for hh in range(2):
            cols = slice(hh * nq, (hh + 1) * nq)
            ot = jnp.dot(vt[hh * dh:(hh + 1) * dh, :], pt[:, cols], preferred_element_type=F32)
            outs.append(ot * linv[:, cols])
        o_ref[0, pl.ds(pl.multiple_of(r * GRID_W, nq), nq), :] = jnp.concatenate(outs, axis=0).T.astype(BF16)

    s_slots = (s0_ref, s1_ref)
    p_slots = (p0_ref, p1_ref)
    l_slots = (l0_ref, l1_ref)

    def step(t, parity, do_scores=True, do_softmax=True, do_values=True):
        a, b = parity, 1 - parity
        t = jnp.asarray(t, I32)
        if do_scores:
            scores(t, s_slots[a])
        if do_softmax:
            softmax(s_slots[b], p_slots[b], l_slots[b])
        if do_values:
            values(t - 2, p_slots[a], l_slots[a])

    assert n_steps % 2 == 0 and n_steps >= 4
    step(0, 0, do_softmax=False, do_values=False)
    step(1, 1, do_values=False)

    def steady(i, carry):
        t = 2 + 2 * i
        step(t, 0)
        step(t + 1, 1)
        return carry

    lax.fori_loop(0, (n_steps - 2) // 2, steady, 0, unroll=True)
    step(n_steps, 0, do_scores=False)
    step(n_steps + 1, 1, do_scores=False, do_softmax=False)


def _neighbourhood_attention(proj_na, bias_tbl, batch, seq):
    rows = seq // GRID_W
    n_pairs = NA_HEADS // 2
    blk = (1, seq, 2 * NA_HEAD_DIM)
    n_var, nk, nq = bias_tbl.shape[1:]
    return pl.pallas_call(
        functools.partial(_na_kernel, rows),
        grid=(batch, n_pairs),
        in_specs=[
            pl.BlockSpec(blk, lambda b, h: (b, 0, h)),
            pl.BlockSpec(blk, lambda b, h: (b, 0, n_pairs + h)),
            pl.BlockSpec(blk, lambda b, h: (b, 0, 2 * n_pairs + h)),
            pl.BlockSpec((2, n_var, nk, nq), lambda b, h: (h, 0, 0, 0)),
        ],
        out_specs=pl.BlockSpec(blk, lambda b, h: (b, 0, h)),
        out_shape=jax.ShapeDtypeStruct((batch, seq, NA_W), BF16),
        scratch_shapes=[pltpu.VMEM((seq // 128, 2 * NA_HEAD_DIM, 128), BF16),
                        pltpu.VMEM((nk, 2 * nq), F32), pltpu.VMEM((nk, 2 * nq), F32),
                        pltpu.VMEM((nk, 2 * nq), BF16), pltpu.VMEM((nk, 2 * nq), BF16),
                        pltpu.VMEM((1, 2 * nq), F32), pltpu.VMEM((1, 2 * nq), F32)],
        compiler_params=_params("parallel", "arbitrary"),
        name="neighbourhood_attention",
    )(proj_na, proj_na, proj_na, bias_tbl)


@functools.lru_cache(maxsize=None)
def _dft_factors(seq):
    n = 2 * seq
    t = np.arange(seq, dtype=np.int64)
    f1 = np.arange(seq // 64, dtype=np.int64)
    f0 = np.arange(64, dtype=np.int64)
    a = 2.0 * np.pi * ((64 * f1[:, None] * t[None, :]) % n).astype(np.float64) / n
    b = 2.0 * np.pi * ((f0[:, None] * t[None, :]) % n).astype(np.float64) / n
    ny = np.where(t % 2 == 0, 1.0, -1.0)
    return tuple(np.asarray(v, np.float32) for v in (np.cos(a), np.sin(a), np.cos(b), np.sin(b), ny))


def _dft_tables(seq):
    ca, sa, cb, sb, ny = (jnp.asarray(v) for v in _dft_factors(seq))
    cos = (ca[:, None, :] * cb[None] - sa[:, None, :] * sb[None]).reshape(seq, seq)
    sin = (sa[:, None, :] * cb[None] + ca[:, None, :] * sb[None]).reshape(seq, seq)
    sin = jnp.concatenate([ny[None, :], sin[1:]], axis=0)
    fwd = jnp.stack([cos, sin]).astype(BF16)
    inv = jnp.concatenate([cos.T, sin.T], axis=1).astype(BF16)
    return fwd, inv


@functools.lru_cache(maxsize=None)
def _hyena_consts(seq):
    t = np.arange(seq, dtype=np.float64)
    t01 = t / (seq - 1)
    bands = np.linspace(1e-4, HY_BANDS - 1, HY_BANDS)
    ang = (2.0 * math.pi) * (t[:, None] / seq) * bands[None, :]
    feats = np.concatenate([t01[:, None], np.cos(ang), -np.sin(ang)], axis=-1)
    feats_p = np.zeros((seq, 128), np.float32)
    feats_p[:, :HY_POS_DIM] = feats
    min_decay = math.log(1e-2) / 1.5
    max_decay = math.log(1e-2) / 0.3
    deltas = np.abs(np.linspace(min_decay, max_decay, HY_W))
    window = np.exp(-t01[:, None] * deltas[None, :]).astype(np.float32)
    return feats_p, window


def _filter_kernel(feats_ref, w1_ref, b1_ref, w2_ref, b2_ref, freq_ref, w3_ref, win_ref, o_ref, hid_ref):
    hp = lax.Precision.HIGHEST

    @pl.when(pl.program_id(0) == 0)
    def _():
        f = freq_ref[...]
        h1 = jnp.sin(f * (jnp.dot(feats_ref[...], w1_ref[...], precision=hp, preferred_element_type=F32)
                          + b1_ref[...]))
        hid_ref[...] = jnp.sin(f * (jnp.dot(h1, w2_ref[...], precision=hp, preferred_element_type=F32)
                                    + b2_ref[...]))

    h = jnp.dot(hid_ref[...], w3_ref[...], precision=hp, preferred_element_type=F32) * win_ref[...]
    row = lax.broadcasted_iota(I32, h.shape, 0)
    drop = jnp.logical_and(row == 0, pl.program_id(0) >= HY_ORDER)
    o_ref[...] = jnp.where(drop, 0.0, h).astype(BF16)


def _hyena_filters_time(w1, b1, w2, b2, w3, freq, seq):
    feats, window = _hyena_consts(seq)
    w1p = jnp.zeros((128, HY_FILT_FF), F32).at[:HY_POS_DIM].set(w1)
    w3r = w3.reshape(HY_FILT_FF, HY_ORDER, 2, HY_W).transpose(0, 2, 1, 3).reshape(HY_FILT_FF, 4 * HY_W)
    full = lambda shape: pl.BlockSpec(shape, lambda c: (0,) * len(shape))
    return pl.pallas_call(
        _filter_kernel,
        grid=(4,),
        in_specs=[
            full((seq, 128)), full((128, HY_FILT_FF)), full((1, HY_FILT_FF)),
            full((HY_FILT_FF, HY_FILT_FF)), full((1, HY_FILT_FF)), full((1, HY_FILT_FF)),
            pl.BlockSpec((HY_FILT_FF, HY_W), lambda c: (0, c)),
            full((seq, HY_W)),
        ],
        out_specs=pl.BlockSpec((seq, HY_W), lambda c: (0, c)),
        out_shape=jax.ShapeDtypeStruct((seq, 4 * HY_W), BF16),
        scratch_shapes=[pltpu.VMEM((seq, HY_FILT_FF), F32)],
        compiler_params=_params("arbitrary"),
        name="hyena_filter_mlp",
    )(jnp.asarray(feats), w1p, b1.reshape(1, -1), w2, b2.reshape(1, -1), freq.reshape(1, -1), w3r,
      jnp.asarray(window))


def _filter_dft_kernel(f_ref, fwd_ref, bwd_ref, o_ref):
    fwd = fwd_ref[...]
    bwd = bwd_ref[...]
    uc = jnp.dot(f_ref[0], fwd, preferred_element_type=F32)
    vc = jnp.dot(f_ref[0], bwd, preferred_element_type=F32)
    us = jnp.dot(f_ref[1], fwd, preferred_element_type=F32)
    vs = jnp.dot(f_ref[1], bwd, preferred_element_type=F32)
    o_ref[0] = uc + vc
    row = lax.broadcasted_iota(I32, us.shape, 0)
    nyq = jnp.logical_and(row == 0, pl.program_id(0) == 0)
    o_ref[1] = jnp.where(nyq, us + vs, us - vs)


def _filter_dft(dft_fwd, filt_t, seq, fb):
    return pl.pallas_call(
        _filter_dft_kernel,
        grid=(seq // fb, HY_ORDER),
        in_specs=[
            pl.BlockSpec((2, fb, seq), lambda f, o: (0, f, 0)),
            pl.BlockSpec((seq, HY_W), lambda f, o: (0, o)),
            pl.BlockSpec((seq, HY_W), lambda f, o: (0, HY_ORDER + o)),
        ],
        out_specs=pl.BlockSpec((2, fb, HY_W), lambda f, o: (0, f, o)),
        out_shape=jax.ShapeDtypeStruct((2, seq, HY_ORDER * HY_W), F32),
        compiler_params=_params("parallel", "arbitrary"),
        name="hyena_filter_dft",
    )(dft_fwd, filt_t, filt_t)


def _short_conv_kernel(p_ref, w_ref, b_ref, o_ref):
    p = p_ref[0].astype(F32)
    seq = p.shape[0]
    row = lax.broadcasted_iota(I32, p.shape, 0)
    prev = jnp.where(row == 0, 0.0, pltpu.roll(p, 1, 0))
    nxt = jnp.where(row == seq - 1, 0.0, pltpu.roll(p, seq - 1, 0))
    w = w_ref[...]
    o_ref[0] = prev * w[0:1] + p * w[1:2] + nxt * w[2:3] + b_ref[...]


def _short_conv(proj, col0, conv_w, conv_b, batch, seq, tc):
    n_cols = conv_w.shape[-1]
    assert col0 % tc == 0 and n_cols % tc == 0
    cb0 = col0 // tc
    return pl.pallas_call(
        _short_conv_kernel,
        grid=(batch, n_cols // tc),
        in_specs=[
            pl.BlockSpec((1, seq, tc), lambda b, c: (b, 0, cb0 + c)),
            pl.BlockSpec((3, tc), lambda b, c: (0, c)),
            pl.BlockSpec((1, tc), lambda b, c: (0, c)),
        ],
        out_specs=pl.BlockSpec((1, seq, tc), lambda b, c: (b, 0, c)),
        out_shape=jax.ShapeDtypeStruct((batch, seq, n_cols), F32),
        compiler_params=_params("parallel", "arbitrary"),
        name="hyena_short_conv",
    )(proj, conv_w, conv_b.reshape(1, -1))


def _spectrum_kernel(n_fft, f_ref, z_ref, k_ref, o_ref):
    z = z_ref[0].astype(BF16)
    xr = jnp.dot(f_ref[0], z, preferred_element_type=F32)
    xs = jnp.dot(f_ref[1], z, preferred_element_type=F32)
    kr = k_ref[0]
    ks = k_ref[1]
    row = lax.broadcasted_iota(I32, xr.shape, 0)
    edge = jnp.logical_and(row == 0, pl.program_id(0) == 0)
    yr = jnp.where(edge, xr * kr * (1.0 / n_fft), (xr * kr - xs * ks) * (2.0 / n_fft))
    ys = jnp.where(edge, xs * ks * (1.0 / n_fft), (xr * ks + xs * kr) * (2.0 / n_fft))
    o_ref[0, 0] = yr.astype(BF16)
    o_ref[0, 1] = ys.astype(BF16)


def _spectrum_product(dft_fwd, z_arr, z_col, kfreq, order, batch, seq, fb):
    return pl.pallas_call(
        functools.partial(_spectrum_kernel, 2 * seq),
        grid=(seq // fb, batch),
        in_specs=[
            pl.BlockSpec((2, fb, seq), lambda f, b: (0, f, 0)),
            pl.BlockSpec((1, seq, HY_W), lambda f, b: (b, 0, z_col)),
            pl.BlockSpec((2, fb, HY_W), lambda f, b: (0, f, order)),
        ],
        out_specs=pl.BlockSpec((1, 2, fb, HY_W), lambda f, b: (b, 0, f, 0)),
        out_shape=jax.ShapeDtypeStruct((batch, 2, seq, HY_W), BF16),
        compiler_params=_params("parallel", "arbitrary"),
        name="hyena_spectrum",
    )(dft_fwd, z_arr, kfreq)


def _inverse_kernel(ft_ref, y_ref, gate_ref, z_ref, d_ref, o_ref):
    conv = jnp.dot(ft_ref[...], y_ref[0], preferred_element_type=F32)
    o_ref[0] = gate_ref[0] * (conv + d_ref[...] * z_ref[0])


def _inverse_gate(dft_inv, y, gate_arr, gate_col, z_arr, z_col, skip_row, batch, seq, tb):
    y2 = y.reshape(batch, 2 * seq, HY_W)
    return pl.pallas_call(
        _inverse_kernel,
        grid=(seq // tb, batch),
        in_specs=[
            pl.BlockSpec((tb, 2 * seq), lambda t, b: (t, 0)),
            pl.BlockSpec((1, 2 * seq, HY_W), lambda t, b: (b, 0, 0)),
            pl.BlockSpec((1, tb, HY_W), lambda t, b: (b, t, gate_col)),
            pl.BlockSpec((1, tb, HY_W), lambda t, b: (b, t, z_col)),
            pl.BlockSpec((1, HY_W), lambda t, b: (0, 0)),
        ],
        out_specs=pl.BlockSpec((1, tb, HY_W), lambda t, b: (b, t, 0)),
        out_shape=jax.ShapeDtypeStruct((batch, seq, HY_W), F32),
        compiler_params=_params("parallel", "arbitrary"),
        name="hyena_inverse_gate",
    )(dft_inv, y2, gate_arr, z_arr, skip_row.reshape(1, HY_W))


def _hyena_mixer(proj, col0, conv_w, conv_b, w1, b1, w2, b2, w3, freq, skip_d, dft_fwd, dft_inv, batch, seq):
    s = _short_conv(proj, col0, conv_w, conv_b, batch, seq, 256)
    filt_t = _hyena_filters_time(w1, b1, w2, b2, w3, freq, seq)
    kfreq = _filter_dft(dft_fwd, filt_t, seq, 512)
    z_arr, z_col = s, 2
    for o in range(HY_ORDER):
        y = _spectrum_product(dft_fwd, z_arr, z_col, kfreq, o, batch, seq, 512)
        z_arr = _inverse_gate(dft_inv, y, s, o, z_arr, z_col, skip_d[o], batch, seq, 512)
        z_col = 0
    return z_arr


@functools.lru_cache(maxsize=None)
def _retention_consts(seq):
    c = RET_CHUNK
    half = RET_HEAD_DIM // 2
    inv = 1.0 / (10000.0 ** np.linspace(0.0, 1.0, half))
    ang = np.arange(seq, dtype=np.float64)[:, None] * inv[None, :]
    cos2 = np.concatenate([np.cos(ang), np.cos(ang)], axis=-1).astype(np.float32)
    sin2 = np.concatenate([-np.sin(ang), np.sin(ang)], axis=-1).astype(np.float32)
    hidx = np.arange(RET_HEADS, dtype=np.float64)
    lg_f = np.log1p(-np.exp2(-5.0 - hidx))[:, None, None]
    lg_b = np.log1p(-np.exp2(-5.5 - hidx))[:, None, None]
    i = np.arange(c, dtype=np.float64)
    diff = i[:, None] - i[None, :]
    ones = np.ones((1, c, c))
    dec = np.where(diff >= 0, np.exp(lg_f * np.maximum(diff, 0.0)), np.exp(lg_b * np.maximum(-diff, 0.0)))
    rowv = lambda v: v[:, :, None] * ones
    tab = np.stack([
        dec,
        rowv(np.exp(lg_f[:, :, 0] * (i + 1.0)[None, :])),
        rowv(np.exp(lg_f[:, :, 0] * (c - 1.0 - i)[None, :])),
        rowv(np.exp(lg_b[:, :, 0] * (c - i)[None, :])),
        rowv(np.exp(lg_b[:, :, 0] * i[None, :])),
        np.exp(lg_f * c) * ones,
        np.exp(lg_b * c) * ones,
    ], axis=1).astype(np.float32)
    return cos2, sin2, tab


def _retention_kernel(q_ref, k_ref, v_ref, g_ref, cos_ref, sin_ref, tab_ref, o_ref, qs_ref, ks_ref, kvf_ref,
                      kvb_ref, a0_ref, a1_ref, y0_ref, y1_ref):
    c = RET_CHUNK
    d = RET_HEAD_DIM
    seq = q_ref.shape[1]
    n_chunks = seq // c
    cos = cos_ref[...]
    sin = sin_ref[...]
    q = q_ref[0].astype(F32)
    k = k_ref[0].astype(F32)
    qs_ref[...] = (q * cos + pltpu.roll(q, d // 2, 1) * sin) * (d ** -0.5)
    ks_ref[...] = k * cos + pltpu.roll(k, d // 2, 1) * sin

    def mm(a, b):
        return jnp.dot(a.astype(BF16), b.astype(BF16), preferred_element_type=F32)

    def chunk(n):
        return pl.ds(pl.multiple_of(n * c, c), c)

    def kv_body(n, carry):
        kc = ks_ref[chunk(n), :]
        vc = v_ref[0, chunk(n), :]
        kvf_ref[n] = mm((kc * tab_ref[0, 2]).T, vc)
        kvb_ref[n] = mm((kc * tab_ref[0, 4]).T, vc)
        return carry

    lax.fori_loop(0, n_chunks, kv_body, 0, unroll=True)

    def scan_fwd(n, state):
        kv = kvf_ref[n]
        kvf_ref[n] = state
        return tab_ref[0, 5] * state + kv

    lax.fori_loop(0, n_chunks, scan_fwd, jnp.zeros((d, d), F32))

    def scan_bwd(m, state):
        n = n_chunks - 1 - m
        kv = kvb_ref[n]
        kvb_ref[n] = state
        return tab_ref[0, 6] * state + kv

    lax.fori_loop(0, n_chunks, scan_bwd, jnp.zeros((d, d), F32))

    def in_chunk(n, a_out):
        qc = qs_ref[chunk(n), :].astype(BF16)
        kc = ks_ref[chunk(n), :].astype(BF16)
        a = lax.dot_general(qc, kc, NT_DIMS, preferred_element_type=F32) * tab_ref[0, 0]
        a_out[...] = a.astype(BF16)

    def mix(n, a_in, y_out):
        qc = qs_ref[chunk(n), :]
        y_out[...] = (jnp.dot(a_in[...], v_ref[0, chunk(n), :], preferred_element_type=F32)
                      + mm(qc * tab_ref[0, 1], kvf_ref[n]) + mm(qc * tab_ref[0, 3], kvb_ref[n]))

    def finish(n, y_in):
        y = y_in[...]
        mu = jnp.mean(y, axis=-1, keepdims=True)
        yc = y - mu
        var = jnp.mean(yc * yc, axis=-1, keepdims=True)
        g = g_ref[0, chunk(n), :].astype(F32)
        o_ref[0, chunk(n), :] = (yc * lax.rsqrt(var + GN_EPS) * (g * jax.nn.sigmoid(g))).astype(BF16)

    a_slots = (a0_ref, a1_ref)
    y_slots = (y0_ref, y1_ref)

    def step(t, parity, do_a=True, do_mix=True, do_finish=True):
        t = jnp.asarray(t, I32)
        if do_a:
            in_chunk(t, a_slots[parity])
        if do_mix:
            mix(t - 1, a_slots[1 - parity], y_slots[1 - parity])
        if do_finish:
            finish(t - 2, y_slots[parity])

    assert n_chunks % 2 == 0 and n_chunks >= 4
    step(0, 0, do_mix=False, do_finish=False)
    step(1, 1, do_finish=False)

    def steady(i, carry):
        step(2 + 2 * i, 0)
        step(3 + 2 * i, 1)
        return carry

    lax.fori_loop(0, (n_chunks - 2) // 2, steady, 0, unroll=True)
    step(n_chunks, 0, do_a=False)
    step(n_chunks + 1, 1, do_a=False, do_mix=False)


def _retention_mixer(proj, col0, batch, seq):
    cos2, sin2, tab = _retention_consts(seq)
    blk = (1, seq, RET_HEAD_DIM)
    h_ = RET_HEADS
    assert col0 % RET_HEAD_DIM == 0
    c0 = col0 // RET_HEAD_DIM
    return pl.pallas_call(
        _retention_kernel,
        grid=(batch, RET_HEADS),
        in_specs=[
            pl.BlockSpec(blk, lambda b, h: (b, 0, c0 + h)),
            pl.BlockSpec(blk, lambda b, h: (b, 0, c0 + h_ + h)),
            pl.BlockSpec(blk, lambda b, h: (b, 0, c0 + 2 * h_ + h)),
            pl.BlockSpec(blk, lambda b, h: (b, 0, c0 + 3 * h_ + h)),
            pl.BlockSpec((seq, RET_HEAD_DIM), lambda b, h: (0, 0)),
            pl.BlockSpec((seq, RET_HEAD_DIM), lambda b, h: (0, 0)),
            pl.BlockSpec((1, 7, RET_CHUNK, RET_CHUNK), lambda b, h: (h, 0, 0, 0)),
        ],
        out_specs=pl.BlockSpec(blk, lambda b, h: (b, 0, h)),
        out_shape=jax.ShapeDtypeStruct((batch, seq, RET_W), BF16),
        scratch_shapes=[pltpu.VMEM((seq, RET_HEAD_DIM), F32), pltpu.VMEM((seq, RET_HEAD_DIM), F32),
                        pltpu.VMEM((seq // RET_CHUNK, RET_HEAD_DIM, RET_HEAD_DIM), F32),
                        pltpu.VMEM((seq // RET_CHUNK, RET_HEAD_DIM, RET_HEAD_DIM), F32),
                        pltpu.VMEM((RET_CHUNK, RET_CHUNK), BF16), pltpu.VMEM((RET_CHUNK, RET_CHUNK), BF16),
                        pltpu.VMEM((RET_CHUNK, RET_HEAD_DIM), F32), pltpu.VMEM((RET_CHUNK, RET_HEAD_DIM), F32)],
        compiler_params=_params("parallel", "arbitrary"),
        name="retention",
    )(proj, proj, proj, proj, jnp.asarray(cos2), jnp.asarray(sin2), jnp.asarray(tab))


def _cross_kernel(q_ref, k_ref, v_ref, o_ref):
    dh = CROSS_HEAD_DIM
    for h in range(CROSS_HEADS):
        sl = slice(h * dh, (h + 1) * dh)
        s = lax.dot_general(q_ref[:, sl], k_ref[:, sl], NT_DIMS, preferred_element_type=F32) * (dh ** -0.5)
        m = jnp.max(s, axis=-1, keepdims=True)
        p = jnp.exp(s - m)
        l = jnp.sum(p, axis=-1, keepdims=True)
        o = jnp.dot(p.astype(BF16), v_ref[:, sl], preferred_element_type=F32) * (1.0 / l)
        o_ref[:, sl] = o.astype(o_ref.dtype)


def _cross_attention(q, kv, batch, seq, n_mem, tq):
    per_b = seq // tq
    return pl.pallas_call(
        _cross_kernel,
        grid=(batch, per_b),
        in_specs=[
            pl.BlockSpec((tq, D_MODEL), lambda b, i: (b * per_b + i, 0)),
            pl.BlockSpec((n_mem, D_MODEL), lambda b, i: (b, 0)),
            pl.BlockSpec((n_mem, D_MODEL), lambda b, i: (b, 1)),
        ],
        out_specs=pl.BlockSpec((tq, D_MODEL), lambda b, i: (b * per_b + i, 0)),
        out_shape=jax.ShapeDtypeStruct((batch * seq, D_MODEL), BF16),
        compiler_params=_params("parallel", "arbitrary"),
        name="cross_attention",
    )(q, kv, kv)


def _router_kernel(x_ref, g_ref, wr_ref, hm_ref, aff_ref):
    n_e = aff_ref.shape[1]
    h = _rms(x_ref[...], g_ref[...])
    h_hi = h.astype(BF16)
    hm_ref[...] = h_hi
    h_lo = (h - h_hi.astype(F32)).astype(BF16)
    w = wr_ref[...]
    w_hi = w.astype(BF16)
    w_lo = (w - w_hi.astype(F32)).astype(BF16)
    logits = (jnp.dot(h_hi, w_hi, preferred_element_type=F32) + jnp.dot(h_lo, w_hi, preferred_element_type=F32)
              + jnp.dot(h_hi, w_lo, preferred_element_type=F32)).T[:n_e]
    m = jnp.max(logits, axis=0, keepdims=True)
    e = jnp.exp(logits - m)
    aff_ref[0] = e / jnp.sum(e, axis=0, keepdims=True)


def _router(x2d, gain, w_router, batch, seq, tm):
    m = x2d.shape[0]
    per_b = seq // tm
    w_pad = jnp.zeros((D_MODEL, 128), F32).at[:, :N_EXPERTS].set(w_router)
    return pl.pallas_call(
        _router_kernel,
        grid=(m // tm,),
        in_specs=[
            pl.BlockSpec((tm, D_MODEL), lambda i: (i, 0)),
            pl.BlockSpec((1, D_MODEL), lambda i: (0, 0)),
            pl.BlockSpec((D_MODEL, 128), lambda i: (0, 0)),
        ],
        out_specs=[
            pl.BlockSpec((tm, D_MODEL), lambda i: (i, 0)),
            pl.BlockSpec((1, N_EXPERTS, tm), lambda i: (i // per_b, 0, i % per_b)),
        ],
        out_shape=[
            jax.ShapeDtypeStruct((m, D_MODEL), BF16),
            jax.ShapeDtypeStruct((batch, N_EXPERTS, seq), F32),
        ],
        compiler_params=_params("parallel"),
        name="moe_router",
    )(x2d, gain.reshape(1, D_MODEL), w_pad)


def _sort_descending(x):
    rows, n = x.shape
    lanes = 128
    n_chunks = n // lanes
    chunks = [x[:, c * lanes:(c + 1) * lanes] for c in range(n_chunks)]
    lane = lax.broadcasted_iota(I32, (rows, lanes), 1)
    k = 2
    while k <= n:
        j = k // 2
        while j >= 1:
            nxt = []
            for c, xc in enumerate(chunks):
                if j >= lanes:
                    lo = (c & (j // lanes)) == 0
                    partner = chunks[c ^ (j // lanes)]
                else:
                    lo = (lane & j) == 0
                    partner = jnp.where(lo, pltpu.roll(xc, lanes - j, 1), pltpu.roll(xc, j, 1))
                desc = ((c * lanes) & k) == 0 if k >= lanes else (lane & k) == 0
                hi, lw = jnp.maximum(xc, partner), jnp.minimum(xc, partner)
                if j >= lanes and k >= lanes:
                    nxt.append(hi if lo == desc else lw)
                else:
                    nxt.append(jnp.where(lo == desc, hi, lw))
            chunks = nxt
            j //= 2
        k *= 2
    return jnp.concatenate(chunks, axis=1)


def _topk_kernel(cap, n_e, aff_ref, slot_ref, wsel_ref, slot_t_ref, thr_ref):
    n_rows, n_tok = aff_ref.shape

    def thr_body(i, carry):
        rows8 = pl.ds(pl.multiple_of(i * 8, 8), 8)
        thr_ref[rows8, :] = _sort_descending(aff_ref[rows8, :])[:, cap - 1:cap]
        return carry

    lax.fori_loop(0, n_rows // 8, thr_body, 0)
    a = aff_ref[...]
    thr = thr_ref[...]
    gt = a > thr
    eq = a == thr
    need = cap - jnp.sum(gt.astype(I32), axis=1, keepdims=True)
    upper = (lax.broadcasted_iota(I32, (n_tok, n_tok), 0) < lax.broadcasted_iota(I32, (n_tok, n_tok), 1))
    upper = upper.astype(BF16)
    eq_rank = jnp.dot(eq.astype(BF16), upper, preferred_element_type=F32)
    sel = jnp.logical_or(gt, jnp.logical_and(eq, eq_rank < need.astype(F32)))
    rank = jnp.dot(sel.astype(BF16), upper, preferred_element_type=F32)
    slot = jnp.where(sel, rank, -1.0)
    slot_ref[...] = slot.astype(I32)
    wsel_ref[...] = jnp.where(sel, a, 0.0)
    pad = jnp.full((128 - n_e, n_tok), -1.0, F32)
    for b in range(n_rows // n_e):
        slot_t_ref[b] = jnp.concatenate([slot[b * n_e:(b + 1) * n_e], pad], axis=0).T.astype(I32)


def _topk_select(aff2d, cap, n_e):
    rows, n_tok = aff2d.shape
    spec = pl.BlockSpec((rows, n_tok), lambda i: (0, 0))
    spec_t = pl.BlockSpec((rows // n_e, n_tok, 128), lambda i: (0, 0, 0))
    return pl.pallas_call(
        functools.partial(_topk_kernel, cap, n_e),
        grid=(1,),
        in_specs=[spec],
        out_specs=[spec, spec, spec_t],
        out_shape=[jax.ShapeDtypeStruct((rows, n_tok), I32), jax.ShapeDtypeStruct((rows, n_tok), F32),
                   jax.ShapeDtypeStruct((rows // n_e, n_tok, 128), I32)],
        scratch_shapes=[pltpu.VMEM((rows, 1), F32)],
        compiler_params=_params("arbitrary"),
        name="moe_topk_select",
    )(aff2d)


def _gather_kernel(cap, slot_ref, wsel_ref, hm_ref, xe_ref, gs_ref):
    n_tok = hm_ref.shape[0]
    onehot = slot_ref[0] == lax.broadcasted_iota(I32, (cap, n_tok), 0)
    xe_ref[0] = jnp.dot(onehot.astype(BF16), hm_ref[...], preferred_element_type=F32).astype(BF16)
    gs_ref[0] = jnp.sum(jnp.where(onehot, wsel_ref[0], 0.0), axis=1, keepdims=True)


def _gather_tokens(slot, wsel, hm, batch, seq, cap):
    rows = batch * N_EXPERTS
    row_spec = pl.BlockSpec((1, 1, seq), lambda b, e: (b * N_EXPERTS + e, 0, 0))
    return pl.pallas_call(
        functools.partial(_gather_kernel, cap),
        grid=(batch, N_EXPERTS),
        in_specs=[row_spec, row_spec, pl.BlockSpec((seq, D_MODEL), lambda b, e: (b, 0))],
        out_specs=[
            pl.BlockSpec((1, cap, D_MODEL), lambda b, e: (e, b, 0)),
            pl.BlockSpec((1, cap, 1), lambda b, e: (e, b, 0)),
        ],
        out_shape=[
            jax.ShapeDtypeStruct((N_EXPERTS, batch * cap, D_MODEL), BF16),
            jax.ShapeDtypeStruct((N_EXPERTS, batch * cap, 1), F32),
        ],
        compiler_params=_params("parallel", "arbitrary"),
        name="moe_gather",
    )(slot.reshape(rows, 1, seq), wsel.reshape(rows, 1, seq), hm)


def _expert_kernel(n_e, n_t, xe_ref, wg_ref, wu_ref, wd_ref, gs_ref, ye_ref, mid_ref):
    g = pl.program_id(0)
    s = pl.program_id(1)
    t = s // 2
    tf = wg_ref.shape[-1]

    @pl.when(jnp.logical_and(s % 2 == 0, g < n_e))
    def _():
        x = xe_ref[0]
        a = jnp.dot(x, wg_ref[...].astype(BF16), preferred_element_type=F32)
        u = jnp.dot(x, wu_ref[...].astype(BF16), preferred_element_type=F32)
        mid_ref[g % 2, t] = (a * jax.nn.sigmoid(a) * u).astype(BF16)

    @pl.when(jnp.logical_and(s % 2 == 1, g >= 1))
    def _():
        prev = (g - 1) % 2
        acc = jnp.dot(mid_ref[prev, 0], wd_ref[0:tf, :].astype(BF16), preferred_element_type=F32)
        for f in range(1, n_t):
            acc += jnp.dot(mid_ref[prev, f], wd_ref[f * tf:(f + 1) * tf, :].astype(BF16),
                           preferred_element_type=F32)
        ye_ref[0] = (acc * gs_ref[0]).astype(BF16)


def _expert_ffn(xe, gs, w_gate, w_up, w_down, layer, tf):
    n_e, rows, d = xe.shape
    ff = w_gate.shape[-1]
    n_t = ff // tf
    assert d // tf == n_t
    last = n_e - 1

    n_s = 2 * n_t

    def fill_tile(g, s, lead):
        v = jnp.minimum(g * n_s + s + lead, n_e * n_s - 1)
        return (layer, v // n_s, 0, (v % n_s) // 2)

    def emit_tile(g, s):
        v = jnp.maximum(g * n_s + s - 1 - n_s, 0)
        return (layer, v // n_s, 0, (v % n_s) // 2)

    return pl.pallas_call(
        functools.partial(_expert_kernel, n_e, n_t),
        grid=(n_e + 1, n_s),
        in_specs=[
            pl.BlockSpec((1, rows, d), lambda g, s: (jnp.minimum(g, last), 0, 0)),
            pl.BlockSpec((None, None, d, tf), functools.partial(fill_tile, lead=1)),
            pl.BlockSpec((None, None, d, tf), functools.partial(fill_tile, lead=0)),
            pl.BlockSpec((None, None, ff, tf), emit_tile),
            pl.BlockSpec((1, rows, 1), lambda g, s: (jnp.maximum(g - 1, 0), 0, 0)),
        ],
        out_specs=pl.BlockSpec((1, rows, tf), lambda g, s: emit_tile(g, s)[1:]),
        out_shape=jax.ShapeDtypeStruct((n_e, rows, d), BF16),
        scratch_shapes=[pltpu.VMEM((2, n_t, rows, tf), BF16)],
        compiler_params=_params("arbitrary", "arbitrary"),
        name="moe_expert_ffn",
    )(xe, w_gate, w_up, w_down, gs)


def _scatter_kernel(cap, slot_ref, ye_ref, x_ref, o_ref, onehot_ref):
    n_e = ye_ref.shape[0]
    n_tok = x_ref.shape[0]

    @pl.when(pl.program_id(1) == 0)
    def _():
        col = lax.broadcasted_iota(I32, (n_tok, cap), 1)
        slots = slot_ref[0]
        for e in range(n_e):
            onehot_ref[:, e * cap:(e + 1) * cap] = (slots[:, e:e + 1] == col).astype(BF16)

    ye = ye_ref[...].reshape(n_e * cap, ye_ref.shape[-1])
    o_ref[...] = x_ref[...] + jnp.dot(onehot_ref[...], ye, preferred_element_type=F32)


def _scatter_add(slot_t, ye, x2d, batch, seq, cap, tn):
    return pl.pallas_call(
        functools.partial(_scatter_kernel, cap),
        grid=(batch, D_MODEL // tn),
        in_specs=[
            pl.BlockSpec((1, seq, 128), lambda b, n: (b, 0, 0)),
            pl.BlockSpec((N_EXPERTS, cap, tn), lambda b, n: (0, b, n)),
            pl.BlockSpec((seq, tn), lambda b, n: (b, n)),
        ],
        out_specs=pl.BlockSpec((seq, tn), lambda b, n: (b, n)),
        out_shape=jax.ShapeDtypeStruct(x2d.shape, F32),
        scratch_shapes=[pltpu.VMEM((seq, N_EXPERTS * cap), BF16)],
        compiler_params=_params("parallel", "arbitrary"),
        name="moe_scatter_add",
    )(slot_t, ye, x2d)


def _expert_choice_ffn(x2d, gain, w_router, w_gate, w_up, w_down, layer, batch, seq):
    cap = EC_CAPACITY * seq // N_EXPERTS
    hm, aff = _router(x2d, gain, w_router, batch, seq, 512)
    slot, wsel, slot_t = _topk_select(aff.reshape(batch * N_EXPERTS, seq), cap, N_EXPERTS)
    xe, gs = _gather_tokens(slot, wsel, hm, batch, seq, cap)
    ye = _expert_ffn(xe, gs, w_gate, w_up, w_down, layer, 512)
    return _scatter_add(slot_t, ye, x2d, batch, seq, cap, 512)


def _final_norm_kernel(x_ref, g_ref, o_ref):
    o_ref[...] = _rms(x_ref[...], g_ref[...])


def _final_norm(x2d, gain, tm):
    m, d = x2d.shape
    return pl.pallas_call(
        _final_norm_kernel,
        grid=(m // tm,),
        in_specs=[pl.BlockSpec((tm, d), lambda i: (i, 0)), pl.BlockSpec((1, d), lambda i: (0, 0))],
        out_specs=pl.BlockSpec((tm, d), lambda i: (i, 0)),
        out_shape=jax.ShapeDtypeStruct((m, d), F32),
        compiler_params=_params("parallel"),
        name="final_norm",
    )(x2d, gain.reshape(1, d))


def kernel(x, mem, norm_mix, w_in, na_rpb, hy_conv_w, hy_conv_b, hy_filt_w1, hy_filt_b1, hy_filt_w2, hy_filt_b2, hy_filt_w3, hy_sin_freq, hy_skip_d, branch_norm, w_out, norm_cross, mem_norm, w_cq, w_ckv, w_co, norm_moe, w_router, w_gate, w_up, w_down, final_norm):
    batch, seq, d = x.shape
    n_mem = mem.shape[1]
    depth = w_in.shape[0]
    m = batch * seq
    na_cols = 3 * NA_W
    hy_cols = 3 * HY_W
    rows = seq // GRID_W

    dft_fwd, dft_inv = _dft_tables(seq)
    x2d = x.reshape(m, d)
    mem2d = mem.reshape(batch * n_mem, d)

    w_in_b, w_out_b, w_cq_b, w_co_b = (w.astype(BF16) for w in (w_in, w_out, w_cq, w_co))
    p_in = w_in.shape[-1]

    for l in range(depth):
        proj = _norm_matmul(x2d, norm_mix[l], _whole(w_in_b, l), 1024, 768, BF16, "in_proj")
        proj = proj.reshape(batch, seq, p_in)

        y_na = _neighbourhood_attention(proj, _na_bias_table(na_rpb[l], rows), batch, seq)
        y_hy = _hyena_mixer(proj, na_cols, hy_conv_w[l], hy_conv_b[l], hy_filt_w1[l],
                            hy_filt_b1[l], hy_filt_w2[l], hy_filt_b2[l], hy_filt_w3[l], hy_sin_freq[l],
                            hy_skip_d[l], dft_fwd, dft_inv, batch, seq)
        y_ret = _retention_mixer(proj, na_cols + hy_cols, batch, seq)
        x2d = _branch_out_proj(y_na.reshape(m, NA_W), y_hy.reshape(m, HY_W), y_ret.reshape(m, RET_W),
                               branch_norm[l], _whole(w_out_b, l), x2d, 1024, 1024)

        q = _norm_matmul(x2d, norm_cross[l], _whole(w_cq_b, l), 1024, 1024, BF16, "cross_q_proj")
        kv = _norm_matmul(mem2d, mem_norm, _whole(w_ckv, l), batch * n_mem, 1024, BF16, "cross_kv_proj")
        o = _cross_attention(q, kv, batch, seq, n_mem, 512)
        x2d = _plain_matmul(o, _whole(w_co_b, l), 1024, 1024, F32, "cross_out_proj", res=x2d)

        x2d = _expert_choice_ffn(x2d, norm_moe[l], w_router[l], w_gate, w_up, w_down, l, batch, seq)

    return _final_norm(x2d, final_norm, 512).reshape(batch, seq, d)
```

```python
import functools
import math

import numpy as np
import jax
import jax.numpy as jnp
from jax import lax
from jax.experimental import pallas as pl
from jax.experimental.pallas import tpu as pltpu

F32 = jnp.float32
BF16 = jnp.bfloat16
I32 = jnp.int32

D_MODEL = 2048
GRID_W = 64
NA_HEAD_DIM = 64
NA_W = 768
NA_HEADS = 12
NA_KR = 8
NA_KC = 16
HY_W = 512
HY_ORDER = 2
HY_BANDS = 8
HY_POS_DIM = 17
HY_FILT_FF = 64
RET_HEAD_DIM = 128
RET_W = 768
RET_HEADS = 6
RET_CHUNK = 128
CROSS_HEADS = 4
CROSS_HEAD_DIM = 512
N_EXPERTS = 16
EXPERT_FF = 2048
EC_CAPACITY = 2
RMS_EPS = 1e-6
GN_EPS = 1e-5

MASK_VALUE = -1e30
VMEM_LIMIT_BYTES = 56 * 1024 * 1024

NT_DIMS = (((1,), (1,)), ((), ()))


def _params(*sem):
    return pltpu.CompilerParams(dimension_semantics=sem, vmem_limit_bytes=VMEM_LIMIT_BYTES)


def _rms(xf, g):
    return xf * lax.rsqrt(jnp.mean(xf * xf, axis=-1, keepdims=True) + RMS_EPS) * g


def _mm_kernel(prologue, n_pro, has_res, *refs):
    pro_refs = refs[:n_pro]
    w_ref = refs[n_pro]
    res_ref = refs[n_pro + 1] if has_res else None
    o_ref = refs[n_pro + 1 + has_res]
    h_ref = refs[n_pro + 2 + has_res]

    @pl.when(pl.program_id(1) == 0)
    def _():
        prologue(h_ref, *pro_refs)

    w = w_ref[...]
    if w.dtype != BF16:
        w = w.astype(BF16)
    acc = jnp.dot(h_ref[...], w, preferred_element_type=F32)
    if has_res:
        acc = acc + res_ref[...]
    o_ref[...] = acc.astype(o_ref.dtype)


def _fused_matmul(prologue, pro_args, pro_specs, wsel, res, m, tm, tn, out_dtype, name):
    w, layer, col0, n = wsel
    k = w.shape[1]
    cb0 = col0 // tn
    assert col0 % tn == 0 and n % tn == 0 and m % tm == 0
    has_res = res is not None
    in_specs = list(pro_specs) + [pl.BlockSpec((None, k, tn), lambda i, j: (layer, 0, cb0 + j))]
    args = list(pro_args) + [w]
    if has_res:
        in_specs.append(pl.BlockSpec((tm, tn), lambda i, j: (i, j)))
        args.append(res)
    return pl.pallas_call(
        functools.partial(_mm_kernel, prologue, len(pro_args), has_res),
        grid=(m // tm, n // tn),
        in_specs=in_specs,
        out_specs=pl.BlockSpec((tm, tn), lambda i, j: (i, j)),
        out_shape=jax.ShapeDtypeStruct((m, n), out_dtype),
        scratch_shapes=[pltpu.VMEM((tm, k), BF16)],
        compiler_params=_params("parallel", "arbitrary"),
        name=name,
    )(*args)


def _rms_prologue(h_ref, x_ref, g_ref):
    h_ref[...] = _rms(x_ref[...], g_ref[...]).astype(BF16)


def _whole(w, layer):
    return (w, layer, 0, w.shape[-1])


def _norm_matmul(x2d, gain, wsel, tm, tn, out_dtype, name, res=None):
    m, k = x2d.shape
    specs = [pl.BlockSpec((tm, k), lambda i, j: (i, 0)), pl.BlockSpec((1, k), lambda i, j: (0, 0))]
    return _fused_matmul(_rms_prologue, [x2d, gain.reshape(1, k)], specs, wsel, res, m, tm, tn, out_dtype, name)


def _cross_prologue(h_ref, q_ref, k_ref, v_ref):
    dh = CROSS_HEAD_DIM
    for h in range(CROSS_HEADS):
        sl = slice(h * dh, (h + 1) * dh)
        s = lax.dot_general(q_ref[:, sl], k_ref[:, sl], NT_DIMS, preferred_element_type=F32) * (dh ** -0.5)
        m = jnp.max(s, axis=-1, keepdims=True)
        p = jnp.exp(s - m)
        l = jnp.sum(p, axis=-1, keepdims=True)
        o = jnp.dot(p.astype(BF16), v_ref[:, sl], preferred_element_type=F32) * (1.0 / l)
        h_ref[:, sl] = o.astype(BF16)


def _cross_attend_out_proj(q, kv, wsel, res, seq, n_mem, tm, tn):
    m, d = q.shape
    per_b = seq // tm
    assert seq % tm == 0
    specs = [
        pl.BlockSpec((tm, d), lambda i, j: (i, 0)),
        pl.BlockSpec((n_mem, d), lambda i, j: (i // per_b, 0)),
        pl.BlockSpec((n_mem, d), lambda i, j: (i // per_b, 1)),
    ]
    return _fused_matmul(_cross_prologue, [q, kv, kv], specs, wsel, res, m, tm, tn, F32, "cross_attend_out_proj")


def _branch_prologue(h_ref, na_ref, hy_ref, ret_ref, g_ref):
    def nrm(p):
        return p * lax.rsqrt(jnp.mean(p * p, axis=-1, keepdims=True) + RMS_EPS)

    y = jnp.concatenate([nrm(r[...].astype(F32)) for r in (na_ref, hy_ref, ret_ref)], axis=-1)
    h_ref[...] = (y * g_ref[...]).astype(BF16)


def _branch_out_proj(y_na, y_hy, y_ret, gain, wsel, res, tm, tn):
    m = y_na.shape[0]
    specs = [
        pl.BlockSpec((tm, NA_W), lambda i, j: (i, 0)),
        pl.BlockSpec((tm, HY_W), lambda i, j: (i, 0)),
        pl.BlockSpec((tm, RET_W), lambda i, j: (i, 0)),
        pl.BlockSpec((1, D_MODEL), lambda i, j: (0, 0)),
    ]
    return _fused_matmul(_branch_prologue, [y_na, y_hy, y_ret, gain.reshape(1, D_MODEL)], specs, wsel, res,
                         m, tm, tn, F32, "branch_out_proj")


NA_PAIR = 2
NA_BAND = NA_KR + NA_PAIR


def _na_band_base(r, rows):
    return np.clip(r - NA_KR // 2, 0, rows - NA_BAND)


@functools.lru_cache(maxsize=None)
def _na_variants(rows):
    assert rows % NA_PAIR == 0 and rows >= NA_BAND + 2
    n_var = NA_KR // 2 + 1
    dr = np.full((n_var, NA_BAND, NA_PAIR), -2, np.int64)
    for r in range(0, rows, NA_PAIR):
        base = int(_na_band_base(r, rows))
        v = (r - base) // 2
        for j in range(NA_PAIR):
            rs = int(np.clip(r + j - NA_KR // 2, 0, rows - NA_KR))
            for i in range(NA_BAND):
                val = base + i - (r + j) + (NA_KR - 1) if rs <= base + i < rs + NA_KR else -1
                assert dr[v, i, j] in (-2, val)
                dr[v, i, j] = val
    assert (dr > -2).all()
    return dr


def _na_bias_table(rpb, rows):
    c = np.arange(GRID_W)
    col_start = np.clip(c - NA_KC // 2, 0, GRID_W - NA_KC)
    col_in = (c[None, :] >= col_start[:, None]) & (c[None, :] < col_start[:, None] + NA_KC)
    dc = np.clip(c[None, :] - c[:, None] + (NA_KC - 1), 0, 2 * NA_KC - 2)
    onehot = (dc.reshape(-1)[None, :] == np.arange(2 * NA_KC - 1)[:, None]).astype(np.float32)
    cols = jnp.einsum("hab,bn->han", rpb.astype(F32), jnp.asarray(onehot), precision=lax.Precision.HIGHEST)
    cols = cols.reshape(NA_HEADS, 2 * NA_KR - 1, GRID_W, GRID_W)
    cols = jnp.where(col_in[None, None], cols, MASK_VALUE).transpose(0, 1, 3, 2)
    masked = jnp.full((NA_HEADS, GRID_W, GRID_W), MASK_VALUE, F32)
    dr = _na_variants(rows)
    variants = []
    for v in range(dr.shape[0]):
        band = [jnp.concatenate([cols[:, dr[v, i, j]] if dr[v, i, j] >= 0 else masked for j in range(NA_PAIR)],
                                axis=-1) for i in range(NA_BAND)]
        variants.append(jnp.concatenate(band, axis=1))
    return jnp.stack(variants, axis=1)


def _na_kernel(rows, q_ref, k_ref, v_ref, bias_ref, o_ref, vt_ref, s0_ref, s1_ref, p0_ref, p1_ref, l0_ref,
               l1_ref):
    dh = NA_HEAD_DIM
    nq = NA_PAIR * GRID_W
    nk = NA_BAND * GRID_W
    n_chunk = nk // 128
    lane = lax.broadcasted_iota(I32, (nq, 2 * dh), 1)

    for ch in range(vt_ref.shape[0]):
        vt_ref[ch] = v_ref[0, ch * 128:(ch + 1) * 128, :].astype(F32).T.astype(BF16)

    n_steps = rows // NA_PAIR

    def band_base(p):
        return jnp.clip(p * NA_PAIR - NA_KR // 2, 0, rows - NA_BAND)

    def scores(p, s_out):
        r = p * NA_PAIR
        base = band_base(p)
        variant = (r - base) // 2
        q = q_ref[0, pl.ds(pl.multiple_of(r * GRID_W, nq), nq), :] * (dh ** -0.5)
        kb = k_ref[0, pl.ds(pl.multiple_of(base * GRID_W, 128), nk), :]
        zero = jnp.zeros_like(q)
        q2 = jnp.concatenate([jnp.where(lane < dh, q, zero), jnp.where(lane >= dh, q, zero)], axis=0)
        st2 = lax.dot_general(kb, q2, NT_DIMS, preferred_element_type=F32)
        for hh in range(2):
            s_out[:, hh * nq:(hh + 1) * nq] = st2[:, hh * nq:(hh + 1) * nq] + bias_ref[hh, variant]

    def softmax(s_in, p_out, l_out):
        st = s_in[...]
        pt = jnp.exp(st - jnp.max(st, axis=0, keepdims=True))
        p_out[...] = pt.astype(BF16)
        l_out[...] = 1.0 / jnp.sum(pt, axis=0, keepdims=True)

    def values(p, p_in, l_in):
        r = p * NA_PAIR
        c0 = band_base(p) // 2
        vt = jnp.concatenate([vt_ref[c0 + i] for i in range(n_chunk)], axis=1)
        pt = p_in[...]
        linv = l_in[...]
        outs = []
        for hh in range(2):
            cols = slice(hh * nq, (hh + 1) * nq)
            ot = jnp.dot(vt[hh * dh:(hh + 1) * dh, :], pt[:, cols], preferred_element_type=F32)
            outs.append(ot * linv[:, cols])
        o_ref[0, pl.ds(pl.multiple_of(r * GRID_W, nq), nq), :] = jnp.concatenate(outs, axis=0).T.astype(BF16)

    s_slots = (s0_ref, s1_ref)
    p_slots = (p0_ref, p1_ref)
    l_slots = (l0_ref, l1_ref)

    def step(t, parity, do_scores=True, do_softmax=True, do_values=True):
        a, b = parity, 1 - parity
        t = jnp.asarray(t, I32)
        if do_scores:
            scores(t, s_slots[a])
        if do_softmax:
            softmax(s_slots[b], p_slots[b], l_slots[b])
        if do_values:
            values(t - 2, p_slots[a], l_slots[a])

    assert n_steps % 2 == 0 and n_steps >= 4
    step(0, 0, do_softmax=False, do_values=False)
    step(1, 1, do_values=False)

    def steady(i, carry):
        t = 2 + 2 * i
        step(t, 0)
        step(t + 1, 1)
        return carry

    lax.fori_loop(0, (n_steps - 2) // 2, steady, 0, unroll=True)
    step(n_steps, 0, do_scores=False)
    step(n_steps + 1, 1, do_scores=False, do_softmax=False)


def _neighbourhood_attention(proj_na, bias_tbl, batch, seq):
    rows = seq // GRID_W
    n_pairs = NA_HEADS // 2
    blk = (1, seq, 2 * NA_HEAD_DIM)
    n_var, nk, nq = bias_tbl.shape[1:]
    return pl.pallas_call(
        functools.partial(_na_kernel, rows),
        grid=(batch, n_pairs),
        in_specs=[
            pl.BlockSpec(blk, lambda b, h: (b, 0, h)),
            pl.BlockSpec(blk, lambda b, h: (b, 0, n_pairs + h)),
            pl.BlockSpec(blk, lambda b, h: (b, 0, 2 * n_pairs + h)),
            pl.BlockSpec((2, n_var, nk, nq), lambda b, h: (h, 0, 0, 0)),
        ],
        out_specs=pl.BlockSpec(blk, lambda b, h: (b, 0, h)),
        out_shape=jax.ShapeDtypeStruct((batch, seq, NA_W), BF16),
        scratch_shapes=[pltpu.VMEM((seq // 128, 2 * NA_HEAD_DIM, 128), BF16),
                        pltpu.VMEM((nk, 2 * nq), F32), pltpu.VMEM((nk, 2 * nq), F32),
                        pltpu.VMEM((nk, 2 * nq), BF16), pltpu.VMEM((nk, 2 * nq), BF16),
                        pltpu.VMEM((1, 2 * nq), F32), pltpu.VMEM((1, 2 * nq), F32)],
        compiler_params=_params("parallel", "arbitrary"),
        name="neighbourhood_attention",
    )(proj_na, proj_na, proj_na, bias_tbl)


@functools.lru_cache(maxsize=None)
def _dft_factors(seq):
    n = 2 * seq
    t = np.arange(seq, dtype=np.int64)
    f1 = np.arange(seq // 64, dtype=np.int64)
    f0 = np.arange(64, dtype=np.int64)
    a = 2.0 * np.pi * ((64 * f1[:, None] * t[None, :]) % n).astype(np.float64) / n
    b = 2.0 * np.pi * ((f0[:, None] * t[None, :]) % n).astype(np.float64) / n
    ny = np.where(t % 2 == 0, 1.0, -1.0)
    return tuple(np.asarray(v, np.float32) for v in (np.cos(a), np.sin(a), np.cos(b), np.sin(b), ny))


def _dft_tables(seq):
    ca, sa, cb, sb, ny = _dft_factors(seq)
    u = np.stack([ca, sa])
    v = np.stack([-sa, ca])
    cb, sb, ny = jnp.asarray(cb), jnp.asarray(sb), jnp.asarray(ny)
    fwd = (jnp.asarray(u)[:, :, None, :] * cb[None, None] + jnp.asarray(v)[:, :, None, :] * sb[None, None])
    fwd = fwd.reshape(2, seq, seq)
    is_nyq = (lax.broadcasted_iota(I32, fwd.shape, 0) == 1) & (lax.broadcasted_iota(I32, fwd.shape, 1) == 0)
    fwd = jnp.where(is_nyq, ny[None, None, :], fwd).astype(BF16)
    ut, vt = jnp.asarray(u.transpose(1, 0, 2)), jnp.asarray(v.transpose(1, 0, 2))
    inv = (ut[:, None] * cb[None, :, None, :] + vt[:, None] * sb[None, :, None, :]).reshape(seq, 2, seq)
    is_nyq = (lax.broadcasted_iota(I32, inv.shape, 1) == 1) & (lax.broadcasted_iota(I32, inv.shape, 2) == 0)
    inv = jnp.where(is_nyq, ny[:, None, None], inv).astype(BF16).reshape(seq, 2 * seq)
    return fwd, inv


@functools.lru_cache(maxsize=None)
def _hyena_consts(seq):
    t = np.arange(seq, dtype=np.float64)
    t01 = t / (seq - 1)
    bands = np.linspace(1e-4, HY_BANDS - 1, HY_BANDS)
    ang = (2.0 * math.pi) * (t[:, None] / seq) * bands[None, :]
    feats = np.concatenate([t01[:, None], np.cos(ang), -np.sin(ang)], axis=-1)
    feats_p = np.zeros((seq, 128), np.float32)
    feats_p[:, :HY_POS_DIM] = feats
    min_decay = math.log(1e-2) / 1.5
    max_decay = math.log(1e-2) / 0.3
    deltas = np.abs(np.linspace(min_decay, max_decay, HY_W))
    window = np.exp(-t01[:, None] * deltas[None, :]).astype(np.float32)
    return feats_p, window


def _filter_kernel(feats_ref, w1_ref, b1_ref, w2_ref, b2_ref, freq_ref, w3_ref, win_ref, o_ref, hid_ref):
    hp = lax.Precision.HIGHEST

    @pl.when(pl.program_id(0) == 0)
    def _():
        f = freq_ref[...]
        h1 = jnp.sin(f * (jnp.dot(feats_ref[...], w1_ref[...], precision=hp, preferred_element_type=F32)
                          + b1_ref[...]))
        hid_ref[...] = jnp.sin(f * (jnp.dot(h1, w2_ref[...], precision=hp, preferred_element_type=F32)
                                    + b2_ref[...]))

    h = jnp.dot(hid_ref[...], w3_ref[...], precision=hp, preferred_element_type=F32) * win_ref[...]
    row = lax.broadcasted_iota(I32, h.shape, 0)
    drop = jnp.logical_and(row == 0, pl.program_id(0) >= HY_ORDER)
    o_ref[...] = jnp.where(drop, 0.0, h).astype(BF16)


def _hyena_filters_time(w1, b1, w2, b2, w3, freq, seq):
    feats, window = _hyena_consts(seq)
    w1p = jnp.zeros((128, HY_FILT_FF), F32).at[:HY_POS_DIM].set(w1)
    w3r = w3.reshape(HY_FILT_FF, HY_ORDER, 2, HY_W).transpose(0, 2, 1, 3).reshape(HY_FILT_FF, 4 * HY_W)
    full = lambda shape: pl.BlockSpec(shape, lambda c: (0,) * len(shape))
    return pl.pallas_call(
        _filter_kernel,
        grid=(4,),
        in_specs=[
            full((seq, 128)), full((128, HY_FILT_FF)), full((1, HY_FILT_FF)),
            full((HY_FILT_FF, HY_FILT_FF)), full((1, HY_FILT_FF)), full((1, HY_FILT_FF)),
            pl.BlockSpec((HY_FILT_FF, HY_W), lambda c: (0, c)),
            full((seq, HY_W)),
        ],
        out_specs=pl.BlockSpec((seq, HY_W), lambda c: (0, c)),
        out_shape=jax.ShapeDtypeStruct((seq, 4 * HY_W), BF16),
        scratch_shapes=[pltpu.VMEM((seq, HY_FILT_FF), F32)],
        compiler_params=_params("arbitrary"),
        name="hyena_filter_mlp",
    )(jnp.asarray(feats), w1p, b1.reshape(1, -1), w2, b2.reshape(1, -1), freq.reshape(1, -1), w3r,
      jnp.asarray(window))


def _filter_dft_kernel(f_ref, fwd_ref, bwd_ref, o_ref):
    fwd = fwd_ref[...]
    bwd = bwd_ref[...]
    uc = jnp.dot(f_ref[0], fwd, preferred_element_type=F32)
    vc = jnp.dot(f_ref[0], bwd, preferred_element_type=F32)
    us = jnp.dot(f_ref[1], fwd, preferred_element_type=F32)
    vs = jnp.dot(f_ref[1], bwd, preferred_element_type=F32)
    o_ref[0] = uc + vc
    row = lax.broadcasted_iota(I32, us.shape, 0)
    nyq = jnp.logical_and(row == 0, pl.program_id(0) == 0)
    o_ref[1] = jnp.where(nyq, us + vs, us - vs)


def _filter_dft(dft_fwd, filt_t, seq, fb):
    return pl.pallas_call(
        _filter_dft_kernel,
        grid=(seq // fb, HY_ORDER),
        in_specs=[
            pl.BlockSpec((2, fb, seq), lambda f, o: (0, f, 0)),
            pl.BlockSpec((seq, HY_W), lambda f, o: (0, o)),
            pl.BlockSpec((seq, HY_W), lambda f, o: (0, HY_ORDER + o)),
        ],
        out_specs=pl.BlockSpec((2, fb, HY_W), lambda f, o: (0, f, o)),
        out_shape=jax.ShapeDtypeStruct((2, seq, HY_ORDER * HY_W), F32),
        compiler_params=_params("parallel", "arbitrary"),
        name="hyena_filter_dft",
    )(dft_fwd, filt_t, filt_t)


def _short_conv_kernel(p_ref, w_ref, b_ref, o_ref):
    p = p_ref[0].astype(F32)
    seq = p.shape[0]
    row = lax.broadcasted_iota(I32, p.shape, 0)
    prev = jnp.where(row == 0, 0.0, pltpu.roll(p, 1, 0))
    nxt = jnp.where(row == seq - 1, 0.0, pltpu.roll(p, seq - 1, 0))
    w = w_ref[...]
    o_ref[0] = prev * w[0:1] + p * w[1:2] + nxt * w[2:3] + b_ref[...]


def _short_conv(proj, col0, conv_w, conv_b, batch, seq, tc):
    n_cols = conv_w.shape[-1]
    assert col0 % tc == 0 and n_cols % tc == 0
    cb0 = col0 // tc
    return pl.pallas_call(
        _short_conv_kernel,
        grid=(batch, n_cols // tc),
        in_specs=[
            pl.BlockSpec((1, seq, tc), lambda b, c: (b, 0, cb0 + c)),
            pl.BlockSpec((3, tc), lambda b, c: (0, c)),
            pl.BlockSpec((1, tc), lambda b, c: (0, c)),
        ],
        out_specs=pl.BlockSpec((1, seq, tc), lambda b, c: (b, 0, c)),
        out_shape=jax.ShapeDtypeStruct((batch, seq, n_cols), F32),
        compiler_params=_params("parallel", "arbitrary"),
        name="hyena_short_conv",
    )(proj, conv_w, conv_b.reshape(1, -1))


def _spectrum_kernel(n_fft, f_ref, z_ref, k_ref, o_ref):
    z = z_ref[0].astype(BF16)
    xr = jnp.dot(f_ref[0], z, preferred_element_type=F32)
    xs = jnp.dot(f_ref[1], z, preferred_element_type=F32)
    kr = k_ref[0]
    ks = k_ref[1]
    row = lax.broadcasted_iota(I32, xr.shape, 0)
    edge = jnp.logical_and(row == 0, pl.program_id(0) == 0)
    yr = jnp.where(edge, xr * kr * (1.0 / n_fft), (xr * kr - xs * ks) * (2.0 / n_fft))
    ys = jnp.where(edge, xs * ks * (1.0 / n_fft), (xr * ks + xs * kr) * (2.0 / n_fft))
    o_ref[0, 0] = yr.astype(BF16)
    o_ref[0, 1] = ys.astype(BF16)


def _spectrum_product(dft_fwd, z_arr, z_col, kfreq, order, batch, seq, fb):
    return pl.pallas_call(
        functools.partial(_spectrum_kernel, 2 * seq),
        grid=(seq // fb, batch),
        in_specs=[
            pl.BlockSpec((2, fb, seq), lambda f, b: (0, f, 0)),
            pl.BlockSpec((1, seq, HY_W), lambda f, b: (b, 0, z_col)),
            pl.BlockSpec((2, fb, HY_W), lambda f, b: (0, f, order)),
        ],
        out_specs=pl.BlockSpec((1, 2, fb, HY_W), lambda f, b: (b, 0, f, 0)),
        out_shape=jax.ShapeDtypeStruct((batch, 2, seq, HY_W), BF16),
        compiler_params=_params("parallel", "arbitrary"),
        name="hyena_spectrum",
    )(dft_fwd, z_arr, kfreq)


def _inverse_kernel(ft_ref, y_ref, gate_ref, z_ref, d_ref, o_ref):
    conv = jnp.dot(ft_ref[...], y_ref[0], preferred_element_type=F32)
    o_ref[0] = gate_ref[0] * (conv + d_ref[...] * z_ref[0])


def _inverse_gate(dft_inv, y, gate_arr, gate_col, z_arr, z_col, skip_row, batch, seq, tb):
    y2 = y.reshape(batch, 2 * seq, HY_W)
    return pl.pallas_call(
        _inverse_kernel,
        grid=(seq // tb, batch),
        in_specs=[
            pl.BlockSpec((tb, 2 * seq), lambda t, b: (t, 0)),
            pl.BlockSpec((1, 2 * seq, HY_W), lambda t, b: (b, 0, 0)),
            pl.BlockSpec((1, tb, HY_W), lambda t, b: (b, t, gate_col)),
            pl.BlockSpec((1, tb, HY_W), lambda t, b: (b, t, z_col)),
            pl.BlockSpec((1, HY_W), lambda t, b: (0, 0)),
        ],
        out_specs=pl.BlockSpec((1, tb, HY_W), lambda t, b: (b, t, 0)),
        out_shape=jax.ShapeDtypeStruct((batch, seq, HY_W), F32),
        compiler_params=_params("parallel", "arbitrary"),
        name="hyena_inverse_gate",
    )(dft_inv, y2, gate_arr, z_arr, skip_row.reshape(1, HY_W))


def _hyena_mixer(proj, col0, conv_w, conv_b, w1, b1, w2, b2, w3, freq, skip_d, dft_fwd, dft_inv, batch, seq):
    s = _short_conv(proj, col0, conv_w, conv_b, batch, seq, 768)
    filt_t = _hyena_filters_time(w1, b1, w2, b2, w3, freq, seq)
    kfreq = _filter_dft(dft_fwd, filt_t, seq, 512)
    z_arr, z_col = s, 2
    for o in range(HY_ORDER):
        y = _spectrum_product(dft_fwd, z_arr, z_col, kfreq, o, batch, seq, 512)
        z_arr = _inverse_gate(dft_inv, y, s, o, z_arr, z_col, skip_d[o], batch, seq, 512)
        z_col = 0
    return z_arr


@functools.lru_cache(maxsize=None)
def _retention_consts(seq):
    c = RET_CHUNK
    half = RET_HEAD_DIM // 2
    inv = 1.0 / (10000.0 ** np.linspace(0.0, 1.0, half))
    ang = np.arange(seq, dtype=np.float64)[:, None] * inv[None, :]
    cos2 = np.concatenate([np.cos(ang), np.cos(ang)], axis=-1).astype(np.float32)
    sin2 = np.concatenate([-np.sin(ang), np.sin(ang)], axis=-1).astype(np.float32)
    hidx = np.arange(RET_HEADS, dtype=np.float64)
    lg_f = np.log1p(-np.exp2(-5.0 - hidx))[:, None, None]
    lg_b = np.log1p(-np.exp2(-5.5 - hidx))[:, None, None]
    i = np.arange(c, dtype=np.float64)
    diff = i[:, None] - i[None, :]
    ones = np.ones((1, c, c))
    dec = np.where(diff >= 0, np.exp(lg_f * np.maximum(diff, 0.0)), np.exp(lg_b * np.maximum(-diff, 0.0)))
    rowv = lambda v: v[:, :, None] * ones
    tab = np.stack([
        dec,
        rowv(np.exp(lg_f[:, :, 0] * (i + 1.0)[None, :])),
        rowv(np.exp(lg_f[:, :, 0] * (c - 1.0 - i)[None, :])),
        rowv(np.exp(lg_b[:, :, 0] * (c - i)[None, :])),
        rowv(np.exp(lg_b[:, :, 0] * i[None, :])),
        np.exp(lg_f * c) * ones,
        np.exp(lg_b * c) * ones,
    ], axis=1).astype(np.float32)
    return cos2, sin2, tab


def _retention_kernel(q_ref, k_ref, v_ref, g_ref, cos_ref, sin_ref, tab_ref, o_ref, qs_ref, ks_ref, kvf_ref,
                      kvb_ref, a0_ref, a1_ref, y0_ref, y1_ref):
    c = RET_CHUNK
    d = RET_HEAD_DIM
    seq = q_ref.shape[1]
    n_chunks = seq // c
    cos = cos_ref[...]
    sin = sin_ref[...]
    q = q_ref[0].astype(F32)
    k = k_ref[0].astype(F32)
    qs_ref[...] = (q * cos + pltpu.roll(q, d // 2, 1) * sin) * (d ** -0.5)
    ks_ref[...] = k * cos + pltpu.roll(k, d // 2, 1) * sin

    def mm(a, b):
        return jnp.dot(a.astype(BF16), b.astype(BF16), preferred_element_type=F32)

    def chunk(n):
        return pl.ds(pl.multiple_of(n * c, c), c)

    def kv_body(n, carry):
        kc = ks_ref[chunk(n), :]
        vc = v_ref[0, chunk(n), :]
        kvf_ref[n] = mm((kc * tab_ref[0, 2]).T, vc)
        kvb_ref[n] = mm((kc * tab_ref[0, 4]).T, vc)
        return carry

    lax.fori_loop(0, n_chunks, kv_body, 0, unroll=True)

    def scan_fwd(n, state):
        kv = kvf_ref[n]
        kvf_ref[n] = state
        return tab_ref[0, 5] * state + kv

    lax.fori_loop(0, n_chunks, scan_fwd, jnp.zeros((d, d), F32))

    def scan_bwd(m, state):
        n = n_chunks - 1 - m
        kv = kvb_ref[n]
        kvb_ref[n] = state
        return tab_ref[0, 6] * state + kv

    lax.fori_loop(0, n_chunks, scan_bwd, jnp.zeros((d, d), F32))

    def in_chunk(n, a_out):
        qc = qs_ref[chunk(n), :].astype(BF16)
        kc = ks_ref[chunk(n), :].astype(BF16)
        a = lax.dot_general(qc, kc, NT_DIMS, preferred_element_type=F32) * tab_ref[0, 0]
        a_out[...] = a.astype(BF16)

    def mix(n, a_in, y_out):
        qc = qs_ref[chunk(n), :]
        y_out[...] = (jnp.dot(a_in[...], v_ref[0, chunk(n), :], preferred_element_type=F32)
                      + mm(qc * tab_ref[0, 1], kvf_ref[n]) + mm(qc * tab_ref[0, 3], kvb_ref[n]))

    def finish(n, y_in):
        y = y_in[...]
        mu = jnp.mean(y, axis=-1, keepdims=True)
        yc = y - mu
        var = jnp.mean(yc * yc, axis=-1, keepdims=True)
        g = g_ref[0, chunk(n), :].astype(F32)
        o_ref[0, chunk(n), :] = (yc * lax.rsqrt(var + GN_EPS) * (g * jax.nn.sigmoid(g))).astype(BF16)

    a_slots = (a0_ref, a1_ref)
    y_slots = (y0_ref, y1_ref)

    def step(t, parity, do_a=True, do_mix=True, do_finish=True):
        t = jnp.asarray(t, I32)
        if do_a:
            in_chunk(t, a_slots[parity])
        if do_mix:
            mix(t - 1, a_slots[1 - parity], y_slots[1 - parity])
        if do_finish:
            finish(t - 2, y_slots[parity])

    assert n_chunks % 2 == 0 and n_chunks >= 4
    step(0, 0, do_mix=False, do_finish=False)
    step(1, 1, do_finish=False)

    def steady(i, carry):
        step(2 + 2 * i, 0)
        step(3 + 2 * i, 1)
        return carry

    lax.fori_loop(0, (n_chunks - 2) // 2, steady, 0, unroll=True)
    step(n_chunks, 0, do_a=False)
    step(n_chunks + 1, 1, do_a=False, do_mix=False)


def _retention_mixer(proj, col0, batch, seq):
    cos2, sin2, tab = _retention_consts(seq)
    blk = (1, seq, RET_HEAD_DIM)
    h_ = RET_HEADS
    assert col0 % RET_HEAD_DIM == 0
    c0 = col0 // RET_HEAD_DIM
    return pl.pallas_call(
        _retention_kernel,
        grid=(batch, RET_HEADS),
        in_specs=[
            pl.BlockSpec(blk, lambda b, h: (b, 0, c0 + h)),
            pl.BlockSpec(blk, lambda b, h: (b, 0, c0 + h_ + h)),
            pl.BlockSpec(blk, lambda b, h: (b, 0, c0 + 2 * h_ + h)),
            pl.BlockSpec(blk, lambda b, h: (b, 0, c0 + 3 * h_ + h)),
            pl.BlockSpec((seq, RET_HEAD_DIM), lambda b, h: (0, 0)),
            pl.BlockSpec((seq, RET_HEAD_DIM), lambda b, h: (0, 0)),
            pl.BlockSpec((1, 7, RET_CHUNK, RET_CHUNK), lambda b, h: (h, 0, 0, 0)),
        ],
        out_specs=pl.BlockSpec(blk, lambda b, h: (b, 0, h)),
        out_shape=jax.ShapeDtypeStruct((batch, seq, RET_W), BF16),
        scratch_shapes=[pltpu.VMEM((seq, RET_HEAD_DIM), F32), pltpu.VMEM((seq, RET_HEAD_DIM), F32),
                        pltpu.VMEM((seq // RET_CHUNK, RET_HEAD_DIM, RET_HEAD_DIM), F32),
                        pltpu.VMEM((seq // RET_CHUNK, RET_HEAD_DIM, RET_HEAD_DIM), F32),
                        pltpu.VMEM((RET_CHUNK, RET_CHUNK), BF16), pltpu.VMEM((RET_CHUNK, RET_CHUNK), BF16),
                        pltpu.VMEM((RET_CHUNK, RET_HEAD_DIM), F32), pltpu.VMEM((RET_CHUNK, RET_HEAD_DIM), F32)],
        compiler_params=_params("parallel", "arbitrary"),
        name="retention",
    )(proj, proj, proj, proj, jnp.asarray(cos2), jnp.asarray(sin2), jnp.asarray(tab))


def _router_kernel(x_ref, g_ref, wr_ref, hm_ref, aff_ref):
    n_e = aff_ref.shape[1]
    h = _rms(x_ref[...], g_ref[...])
    h_hi = h.astype(BF16)
    hm_ref[...] = h_hi
    h_lo = (h - h_hi.astype(F32)).astype(BF16)
    w = wr_ref[...]
    w_hi = w.astype(BF16)
    w_lo = (w - w_hi.astype(F32)).astype(BF16)
    logits = (jnp.dot(h_hi, w_hi, preferred_element_type=F32) + jnp.dot(h_lo, w_hi, preferred_element_type=F32)
              + jnp.dot(h_hi, w_lo, preferred_element_type=F32)).T[:n_e]
    m = jnp.max(logits, axis=0, keepdims=True)
    e = jnp.exp(logits - m)
    aff_ref[0] = e / jnp.sum(e, axis=0, keepdims=True)


def _router(x2d, gain, w_router, batch, seq, tm):
    m = x2d.shape[0]
    per_b = seq // tm
    w_pad = jnp.zeros((D_MODEL, 128), F32).at[:, :N_EXPERTS].set(w_router)
    return pl.pallas_call(
        _router_kernel,
        grid=(m // tm,),
        in_specs=[
            pl.BlockSpec((tm, D_MODEL), lambda i: (i, 0)),
            pl.BlockSpec((1, D_MODEL), lambda i: (0, 0)),
            pl.BlockSpec((D_MODEL, 128), lambda i: (0, 0)),
        ],
        out_specs=[
            pl.BlockSpec((tm, D_MODEL), lambda i: (i, 0)),
            pl.BlockSpec((1, N_EXPERTS, tm), lambda i: (i // per_b, 0, i % per_b)),
        ],
        out_shape=[
            jax.ShapeDtypeStruct((m, D_MODEL), BF16),
            jax.ShapeDtypeStruct((batch, N_EXPERTS, seq), F32),
        ],
        compiler_params=_params("parallel"),
        name="moe_router",
    )(x2d, gain.reshape(1, D_MODEL), w_pad)


def _sort_descending(x):
    rows, n = x.shape
    lanes = 128
    n_chunks = n // lanes
    chunks = [x[:, c * lanes:(c + 1) * lanes] for c in range(n_chunks)]
    lane = lax.broadcasted_iota(I32, (rows, lanes), 1)
    k = 2
    while k <= n:
        j = k // 2
        while j >= 1:
            nxt = []
            for c, xc in enumerate(chunks):
                if j >= lanes:
                    lo = (c & (j // lanes)) == 0
                    partner = chunks[c ^ (j // lanes)]
                else:
                    lo = (lane & j) == 0
                    partner = jnp.where(lo, pltpu.roll(xc, lanes - j, 1), pltpu.roll(xc, j, 1))
                desc = ((c * lanes) & k) == 0 if k >= lanes else (lane & k) == 0
                hi, lw = jnp.maximum(xc, partner), jnp.minimum(xc, partner)
                if j >= lanes and k >= lanes:
                    nxt.append(hi if lo == desc else lw)
                else:
                    nxt.append(jnp.where(lo == desc, hi, lw))
            chunks = nxt
            j //= 2
        k *= 2
    return jnp.concatenate(chunks, axis=1)


def _topk_kernel(cap, n_e, aff_ref, slot_ref, wsel_ref, slot_t_ref, thr_ref):
    n_rows, n_tok = aff_ref.shape

    def thr_body(i, carry):
        rows8 = pl.ds(pl.multiple_of(i * 8, 8), 8)
        thr_ref[rows8, :] = _sort_descending(aff_ref[rows8, :])[:, cap - 1:cap]
        return carry

    lax.fori_loop(0, n_rows // 8, thr_body, 0)
    a = aff_ref[...]
    thr = thr_ref[...]
    gt = a > thr
    eq = a == thr
    need = cap - jnp.sum(gt.astype(I32), axis=1, keepdims=True)
    upper = (lax.broadcasted_iota(I32, (n_tok, n_tok), 0) < lax.broadcasted_iota(I32, (n_tok, n_tok), 1))
    upper = upper.astype(BF16)
    eq_rank = jnp.dot(eq.astype(BF16), upper, preferred_element_type=F32)
    sel = jnp.logical_or(gt, jnp.logical_and(eq, eq_rank < need.astype(F32)))
    rank = jnp.dot(sel.astype(BF16), upper, preferred_element_type=F32)
    slot = jnp.where(sel, rank, -1.0)
    slot_ref[...] = slot.astype(I32)
    wsel_ref[...] = jnp.where(sel, a, 0.0)
    pad = jnp.full((128 - n_e, n_tok), -1.0, F32)
    for b in range(n_rows // n_e):
        slot_t_ref[b] = jnp.concatenate([slot[b * n_e:(b + 1) * n_e], pad], axis=0).T.astype(I32)


def _topk_select(aff2d, cap, n_e):
    rows, n_tok = aff2d.shape
    spec = pl.BlockSpec((rows, n_tok), lambda i: (0, 0))
    spec_t = pl.BlockSpec((rows // n_e, n_tok, 128), lambda i: (0, 0, 0))
    return pl.pallas_call(
        functools.partial(_topk_kernel, cap, n_e),
        grid=(1,),
        in_specs=[spec],
        out_specs=[spec, spec, spec_t],
        out_shape=[jax.ShapeDtypeStruct((rows, n_tok), I32), jax.ShapeDtypeStruct((rows, n_tok), F32),
                   jax.ShapeDtypeStruct((rows // n_e, n_tok, 128), I32)],
        scratch_shapes=[pltpu.VMEM((rows, 1), F32)],
        compiler_params=_params("arbitrary"),
        name="moe_topk_select",
    )(aff2d)


def _gather_kernel(cap, slot_ref, wsel_ref, hm_ref, xe_ref, gs_ref):
    n_tok = hm_ref.shape[0]
    onehot = slot_ref[0] == lax.broadcasted_iota(I32, (cap, n_tok), 0)
    xe_ref[0] = jnp.dot(onehot.astype(BF16), hm_ref[...], preferred_element_type=F32).astype(BF16)
    gs_ref[0] = jnp.sum(jnp.where(onehot, wsel_ref[0], 0.0), axis=1, keepdims=True)


def _gather_tokens(slot, wsel, hm, batch, seq, cap):
    rows = batch * N_EXPERTS
    row_spec = pl.BlockSpec((1, 1, seq), lambda b, e: (b * N_EXPERTS + e, 0, 0))
    return pl.pallas_call(
        functools.partial(_gather_kernel, cap),
        grid=(batch, N_EXPERTS),
        in_specs=[row_spec, row_spec, pl.BlockSpec((seq, D_MODEL), lambda b, e: (b, 0))],
        out_specs=[
            pl.BlockSpec((1, cap, D_MODEL), lambda b, e: (e, b, 0)),
            pl.BlockSpec((1, cap, 1), lambda b, e: (e, b, 0)),
        ],
        out_shape=[
            jax.ShapeDtypeStruct((N_EXPERTS, batch * cap, D_MODEL), BF16),
            jax.ShapeDtypeStruct((N_EXPERTS, batch * cap, 1), F32),
        ],
        compiler_params=_params("parallel", "arbitrary"),
        name="moe_gather",
    )(slot.reshape(rows, 1, seq), wsel.reshape(rows, 1, seq), hm)


def _expert_kernel(n_e, n_t, xe_ref, wg_ref, wu_ref, wd_ref, gs_ref, ye_ref, mid_ref):
    g = pl.program_id(0)
    s = pl.program_id(1)
    t = s // 2
    tf = wg_ref.shape[-1]

    @pl.when(jnp.logical_and(s % 2 == 0, g < n_e))
    def _():
        x = xe_ref[0]
        a = jnp.dot(x, wg_ref[...].astype(BF16), preferred_element_type=F32)
        u = jnp.dot(x, wu_ref[...].astype(BF16), preferred_element_type=F32)
        mid_ref[g % 2, t] = (a * jax.nn.sigmoid(a) * u).astype(BF16)

    @pl.when(jnp.logical_and(s % 2 == 1, g >= 1))
    def _():
        prev = (g - 1) % 2
        acc = jnp.dot(mid_ref[prev, 0], wd_ref[0:tf, :].astype(BF16), preferred_element_type=F32)
        for f in range(1, n_t):
            acc += jnp.dot(mid_ref[prev, f], wd_ref[f * tf:(f + 1) * tf, :].astype(BF16),
                           preferred_element_type=F32)
        ye_ref[0] = (acc * gs_ref[0]).astype(BF16)


def _expert_ffn(xe, gs, w_gate, w_up, w_down, layer, tf):
    n_e, rows, d = xe.shape
    ff = w_gate.shape[-1]
    n_t = ff // tf
    assert d // tf == n_t
    last = n_e - 1

    n_s = 2 * n_t

    def fill_tile(g, s, lead):
        v = jnp.minimum(g * n_s + s + lead, n_e * n_s - 1)
        return (layer, v // n_s, 0, (v % n_s) // 2)

    def emit_tile(g, s):
        v = jnp.maximum(g * n_s + s - 1 - n_s, 0)
        return (layer, v // n_s, 0, (v % n_s) // 2)

    return pl.pallas_call(
        functools.partial(_expert_kernel, n_e, n_t),
        grid=(n_e + 1, n_s),
        in_specs=[
            pl.BlockSpec((1, rows, d), lambda g, s: (jnp.minimum(g, last), 0, 0)),
            pl.BlockSpec((None, None, d, tf), functools.partial(fill_tile, lead=1)),
            pl.BlockSpec((None, None, d, tf), functools.partial(fill_tile, lead=0)),
            pl.BlockSpec((None, None, ff, tf), emit_tile),
            pl.BlockSpec((1, rows, 1), lambda g, s: (jnp.maximum(g - 1, 0), 0, 0)),
        ],
        out_specs=pl.BlockSpec((1, rows, tf), lambda g, s: emit_tile(g, s)[1:]),
        out_shape=jax.ShapeDtypeStruct((n_e, rows, d), BF16),
        scratch_shapes=[pltpu.VMEM((2, n_t, rows, tf), BF16)],
        compiler_params=_params("arbitrary", "arbitrary"),
        name="moe_expert_ffn",
    )(xe, w_gate, w_up, w_down, gs)


def _scatter_kernel(cap, slot_ref, ye_ref, x_ref, o_ref, onehot_ref):
    n_e = ye_ref.shape[0]
    n_tok = x_ref.shape[0]

    @pl.when(pl.program_id(1) == 0)
    def _():
        col = lax.broadcasted_iota(I32, (n_tok, cap), 1)
        slots = slot_ref[0]
        for e in range(n_e):
            onehot_ref[:, e * cap:(e + 1) * cap] = (slots[:, e:e + 1] == col).astype(BF16)

    ye = ye_ref[...].reshape(n_e * cap, ye_ref.shape[-1])
    o_ref[...] = x_ref[...] + jnp.dot(onehot_ref[...], ye, preferred_element_type=F32)


def _scatter_add(slot_t, ye, x2d, batch, seq, cap, tn):
    return pl.pallas_call(
        functools.partial(_scatter_kernel, cap),
        grid=(batch, D_MODEL // tn),
        in_specs=[
            pl.BlockSpec((1, seq, 128), lambda b, n: (b, 0, 0)),
            pl.BlockSpec((N_EXPERTS, cap, tn), lambda b, n: (0, b, n)),
            pl.BlockSpec((seq, tn), lambda b, n: (b, n)),
        ],
        out_specs=pl.BlockSpec((seq, tn), lambda b, n: (b, n)),
        out_shape=jax.ShapeDtypeStruct(x2d.shape, F32),
        scratch_shapes=[pltpu.VMEM((seq, N_EXPERTS * cap), BF16)],
        compiler_params=_params("parallel", "arbitrary"),
        name="moe_scatter_add",
    )(slot_t, ye, x2d)


def _expert_choice_ffn(x2d, gain, w_router, w_gate, w_up, w_down, layer, batch, seq):
    cap = EC_CAPACITY * seq // N_EXPERTS
    hm, aff = _router(x2d, gain, w_router, batch, seq, 512)
    slot, wsel, slot_t = _topk_select(aff.reshape(batch * N_EXPERTS, seq), cap, N_EXPERTS)
    xe, gs = _gather_tokens(slot, wsel, hm, batch, seq, cap)
    ye = _expert_ffn(xe, gs, w_gate, w_up, w_down, layer, 512)
    return _scatter_add(slot_t, ye, x2d, batch, seq, cap, 512)


def _final_norm_kernel(x_ref, g_ref, o_ref):
    o_ref[...] = _rms(x_ref[...], g_ref[...])


def _final_norm(x2d, gain, tm):
    m, d = x2d.shape
    return pl.pallas_call(
        _final_norm_kernel,
        grid=(m // tm,),
        in_specs=[pl.BlockSpec((tm, d), lambda i: (i, 0)), pl.BlockSpec((1, d), lambda i: (0, 0))],
        out_specs=pl.BlockSpec((tm, d), lambda i: (i, 0)),
        out_shape=jax.ShapeDtypeStruct((m, d), F32),
        compiler_params=_params("parallel"),
        name="final_norm",
    )(x2d, gain.reshape(1, d))


def kernel(x, mem, norm_mix, w_in, na_rpb, hy_conv_w, hy_conv_b, hy_filt_w1, hy_filt_b1, hy_filt_w2, hy_filt_b2, hy_filt_w3, hy_sin_freq, hy_skip_d, branch_norm, w_out, norm_cross, mem_norm, w_cq, w_ckv, w_co, norm_moe, w_router, w_gate, w_up, w_down, final_norm):
    batch, seq, d = x.shape
    n_mem = mem.shape[1]
    depth = w_in.shape[0]
    m = batch * seq
    na_cols = 3 * NA_W
    hy_cols = 3 * HY_W
    rows = seq // GRID_W

    dft_fwd, dft_inv = _dft_tables(seq)
    x2d = x.reshape(m, d)
    mem2d = mem.reshape(batch * n_mem, d)

    w_in_b, w_out_b, w_cq_b, w_co_b = (w.astype(BF16) for w in (w_in, w_out, w_cq, w_co))
    p_in = w_in.shape[-1]

    for l in range(depth):
        proj = _norm_matmul(x2d, norm_mix[l], _whole(w_in_b, l), 1024, 768, BF16, "in_proj")
        proj = proj.reshape(batch, seq, p_in)

        y_na = _neighbourhood_attention(proj, _na_bias_table(na_rpb[l], rows), batch, seq)
        y_hy = _hyena_mixer(proj, na_cols, hy_conv_w[l], hy_conv_b[l], hy_filt_w1[l],
                            hy_filt_b1[l], hy_filt_w2[l], hy_filt_b2[l], hy_filt_w3[l], hy_sin_freq[l],
                            hy_skip_d[l], dft_fwd, dft_inv, batch, seq)
        y_ret = _retention_mixer(proj, na_cols + hy_cols, batch, seq)
        x2d = _branch_out_proj(y_na.reshape(m, NA_W), y_hy.reshape(m, HY_W), y_ret.reshape(m, RET_W),
                               branch_norm[l], _whole(w_out_b, l), x2d, 1024, 1024)

        q = _norm_matmul(x2d, norm_cross[l], _whole(w_cq_b, l), 1024, 1024, BF16, "cross_q_proj")
        kv = _norm_matmul(mem2d, mem_norm, _whole(w_ckv, l), batch * n_mem, 1024, BF16, "cross_kv_proj")
        x2d = _cross_attend_out_proj(q, kv, _whole(w_co_b, l), x2d, seq, n_mem, 1024, 1024)

        x2d = _expert_choice_ffn(x2d, norm_moe[l], w_router[l], w_gate, w_up, w_down, l, batch, seq)

    return _final_norm(x2d, final_norm, 512).reshape(batch, seq, d)
```

```python
import functools
import math

import numpy as np
import jax
import jax.numpy as jnp
from jax import lax
from jax.experimental import pallas as pl
from jax.experimental.pallas import tpu as pltpu

F32 = jnp.float32
BF16 = jnp.bfloat16
I32 = jnp.int32

D_MODEL = 2048
GRID_W = 64
NA_HEAD_DIM = 64
NA_W = 768
NA_HEADS = 12
NA_KR = 8
NA_KC = 16
HY_W = 512
HY_ORDER = 2
HY_BANDS = 8
HY_POS_DIM = 17
HY_FILT_FF = 64
RET_HEAD_DIM = 128
RET_W = 768
RET_HEADS = 6
RET_CHUNK = 128
CROSS_HEADS = 4
CROSS_HEAD_DIM = 512
N_EXPERTS = 16
EXPERT_FF = 2048
EC_CAPACITY = 2
RMS_EPS = 1e-6
GN_EPS = 1e-5

MASK_VALUE = -1e30
VMEM_LIMIT_BYTES = 56 * 1024 * 1024

NT_DIMS = (((1,), (1,)), ((), ()))


def _params(*sem):
    return pltpu.CompilerParams(dimension_semantics=sem, vmem_limit_bytes=VMEM_LIMIT_BYTES)


def _rms(xf, g):
    return xf * lax.rsqrt(jnp.mean(xf * xf, axis=-1, keepdims=True) + RMS_EPS) * g


def _mm_kernel(prologue, n_pro, has_res, *refs):
    pro_refs = refs[:n_pro]
    w_ref = refs[n_pro]
    res_ref = refs[n_pro + 1] if has_res else None
    o_ref = refs[n_pro + 1 + has_res]
    h_ref = refs[n_pro + 2 + has_res]

    @pl.when(pl.program_id(1) == 0)
    def _():
        prologue(h_ref, *pro_refs)

    w = w_ref[...]
    if w.dtype != BF16:
        w = w.astype(BF16)
    acc = jnp.dot(h_ref[...], w, preferred_element_type=F32)
    if has_res:
        acc = acc + res_ref[...]
    o_ref[...] = acc.astype(o_ref.dtype)


def _fused_matmul(prologue, pro_args, pro_specs, wsel, res, m, tm, tn, out_dtype, name):
    w, layer, col0, n = wsel
    k = w.shape[1]
    cb0 = col0 // tn
    assert col0 % tn == 0 and n % tn == 0 and m % tm == 0
    has_res = res is not None
    in_specs = list(pro_specs) + [pl.BlockSpec((None, k, tn), lambda i, j: (layer, 0, cb0 + j))]
    args = list(pro_args) + [w]
    if has_res:
        in_specs.append(pl.BlockSpec((tm, tn), lambda i, j: (i, j)))
        args.append(res)
    return pl.pallas_call(
        functools.partial(_mm_kernel, prologue, len(pro_args), has_res),
        grid=(m // tm, n // tn),
        in_specs=in_specs,
        out_specs=pl.BlockSpec((tm, tn), lambda i, j: (i, j)),
        out_shape=jax.ShapeDtypeStruct((m, n), out_dtype),
        scratch_shapes=[pltpu.VMEM((tm, k), BF16)],
        compiler_params=_params("parallel", "arbitrary"),
        name=name,
    )(*args)


def _rms_prologue(h_ref, x_ref, g_ref):
    h_ref[...] = _rms(x_ref[...], g_ref[...]).astype(BF16)


def _whole(w, layer):
    return (w, layer, 0, w.shape[-1])


def _norm_matmul(x2d, gain, wsel, tm, tn, out_dtype, name, res=None):
    m, k = x2d.shape
    specs = [pl.BlockSpec((tm, k), lambda i, j: (i, 0)), pl.BlockSpec((1, k), lambda i, j: (0, 0))]
    return _fused_matmul(_rms_prologue, [x2d, gain.reshape(1, k)], specs, wsel, res, m, tm, tn, out_dtype, name)


def _cross_prologue(h_ref, q_ref, k_ref, v_ref):
    dh = CROSS_HEAD_DIM
    for h in range(CROSS_HEADS):
        sl = slice(h * dh, (h + 1) * dh)
        s = lax.dot_general(q_ref[:, sl], k_ref[:, sl], NT_DIMS, preferred_element_type=F32) * (dh ** -0.5)
        m = jnp.max(s, axis=-1, keepdims=True)
        p = jnp.exp(s - m)
        l = jnp.sum(p, axis=-1, keepdims=True)
        o = jnp.dot(p.astype(BF16), v_ref[:, sl], preferred_element_type=F32) * (1.0 / l)
        h_ref[:, sl] = o.astype(BF16)


def _cross_attend_out_proj(q, kv, wsel, res, seq, n_mem, tm, tn):
    m, d = q.shape
    per_b = seq // tm
    assert seq % tm == 0
    specs = [
        pl.BlockSpec((tm, d), lambda i, j: (i, 0)),
        pl.BlockSpec((n_mem, d), lambda i, j: (i // per_b, 0)),
        pl.BlockSpec((n_mem, d), lambda i, j: (i // per_b, 1)),
    ]
    return _fused_matmul(_cross_prologue, [q, kv, kv], specs, wsel, res, m, tm, tn, F32, "cross_attend_out_proj")


def _branch_prologue(h_ref, na_ref, hy_ref, ret_ref, g_ref):
    def nrm(p):
        return p * lax.rsqrt(jnp.mean(p * p, axis=-1, keepdims=True) + RMS_EPS)

    y = jnp.concatenate([nrm(r[...].astype(F32)) for r in (na_ref, hy_ref, ret_ref)], axis=-1)
    h_ref[...] = (y * g_ref[...]).astype(BF16)


def _branch_out_proj(y_na, y_hy, y_ret, gain, wsel, res, tm, tn):
    m = y_na.shape[0]
    specs = [
        pl.BlockSpec((tm, NA_W), lambda i, j: (i, 0)),
        pl.BlockSpec((tm, HY_W), lambda i, j: (i, 0)),
        pl.BlockSpec((tm, RET_W), lambda i, j: (i, 0)),
        pl.BlockSpec((1, D_MODEL), lambda i, j: (0, 0)),
    ]
    return _fused_matmul(_branch_prologue, [y_na, y_hy, y_ret, gain.reshape(1, D_MODEL)], specs, wsel, res,
                         m, tm, tn, F32, "branch_out_proj")


NA_PAIR = 2
NA_BAND = NA_KR + NA_PAIR


def _na_band_base(r, rows):
    return np.clip(r - NA_KR // 2, 0, rows - NA_BAND)


@functools.lru_cache(maxsize=None)
def _na_variants(rows):
    assert rows % NA_PAIR == 0 and rows >= NA_BAND + 2
    n_var = NA_KR // 2 + 1
    dr = np.full((n_var, NA_BAND, NA_PAIR), -2, np.int64)
    for r in range(0, rows, NA_PAIR):
        base = int(_na_band_base(r, rows))
        v = (r - base) // 2
        for j in range(NA_PAIR):
            rs = int(np.clip(r + j - NA_KR // 2, 0, rows - NA_KR))
            for i in range(NA_BAND):
                val = base + i - (r + j) + (NA_KR - 1) if rs <= base + i < rs + NA_KR else -1
                assert dr[v, i, j] in (-2, val)
                dr[v, i, j] = val
    assert (dr > -2).all()
    return dr


def _na_bias_table(rpb, rows):
    c = np.arange(GRID_W)
    col_start = np.clip(c - NA_KC // 2, 0, GRID_W - NA_KC)
    col_in = (c[None, :] >= col_start[:, None]) & (c[None, :] < col_start[:, None] + NA_KC)
    dc = np.clip(c[None, :] - c[:, None] + (NA_KC - 1), 0, 2 * NA_KC - 2)
    onehot = (dc.reshape(-1)[None, :] == np.arange(2 * NA_KC - 1)[:, None]).astype(np.float32)
    cols = jnp.einsum("hab,bn->han", rpb.astype(F32), jnp.asarray(onehot), precision=lax.Precision.HIGHEST)
    cols = cols.reshape(NA_HEADS, 2 * NA_KR - 1, GRID_W, GRID_W)
    cols = jnp.where(col_in[None, None], cols, MASK_VALUE).transpose(0, 1, 3, 2)
    masked = jnp.full((NA_HEADS, GRID_W, GRID_W), MASK_VALUE, F32)
    dr = _na_variants(rows)
    variants = []
    for v in range(dr.shape[0]):
        band = [jnp.concatenate([cols[:, dr[v, i, j]] if dr[v, i, j] >= 0 else masked for j in range(NA_PAIR)],
                                axis=-1) for i in range(NA_BAND)]
        variants.append(jnp.concatenate(band, axis=1))
    return jnp.stack(variants, axis=1)


def _na_kernel(rows, q_ref, k_ref, v_ref, bias_ref, o_ref, vt_ref, s0_ref, s1_ref, p0_ref, p1_ref, l0_ref,
               l1_ref):
    dh = NA_HEAD_DIM
    nq = NA_PAIR * GRID_W
    nk = NA_BAND * GRID_W
    n_chunk = nk // 128
    lane = lax.broadcasted_iota(I32, (nq, 2 * dh), 1)

    for ch in range(vt_ref.shape[0]):
        vt_ref[ch] = v_ref[0, ch * 128:(ch + 1) * 128, :].astype(F32).T.astype(BF16)

    n_steps = rows // NA_PAIR

    def band_base(p):
        return jnp.clip(p * NA_PAIR - NA_KR // 2, 0, rows - NA_BAND)

    def scores(p, s_out):
        r = p * NA_PAIR
        base = band_base(p)
        variant = (r - base) // 2
        q = q_ref[0, pl.ds(pl.multiple_of(r * GRID_W, nq), nq), :] * (dh ** -0.5)
        kb = k_ref[0, pl.ds(pl.multiple_of(base * GRID_W, 128), nk), :]
        zero = jnp.zeros_like(q)
        q2 = jnp.concatenate([jnp.where(lane < dh, q, zero), jnp.where(lane >= dh, q, zero)], axis=0)
        st2 = lax.dot_general(kb, q2, NT_DIMS, preferred_element_type=F32)
        for hh in range(2):
            s_out[:, hh * nq:(hh + 1) * nq] = st2[:, hh * nq:(hh + 1) * nq] + bias_ref[hh, variant]

    def softmax(s_in, p_out, l_out):
        st = s_in[...]
        pt = jnp.exp(st - jnp.max(st, axis=0, keepdims=True))
        p_out[...] = pt.astype(BF16)
        l_out[...] = 1.0 / jnp.sum(pt, axis=0, keepdims=True)

    def values(p, p_in, l_in):
        r = p * NA_PAIR
        c0 = band_base(p) // 2
        vt = jnp.concatenate([vt_ref[c0 + i] for i in range(n_chunk)], axis=1)
        pt = p_in[...]
        linv = l_in[...]
        outs = []
        for hh in range(2):
            cols = slice(hh * nq, (hh + 1) * nq)
            ot = jnp.dot(vt[hh * dh:(hh + 1) * dh, :], pt[:, cols], preferred_element_type=F32)
            outs.append(ot * linv[:, cols])
        o_ref[0, pl.ds(pl.multiple_of(r * GRID_W, nq), nq), :] = jnp.concatenate(outs, axis=0).T.astype(BF16)

    s_slots = (s0_ref, s1_ref)
    p_slots = (p0_ref, p1_ref)
    l_slots = (l0_ref, l1_ref)

    def step(t, parity, do_scores=True, do_softmax=True, do_values=True):
        a, b = parity, 1 - parity
        t = jnp.asarray(t, I32)
        if do_scores:
            scores(t, s_slots[a])
        if do_softmax:
            softmax(s_slots[b], p_slots[b], l_slots[b])
        if do_values:
            values(t - 2, p_slots[a], l_slots[a])

    assert n_steps % 2 == 0 and n_steps >= 4
    step(0, 0, do_softmax=False, do_values=False)
    step(1, 1, do_values=False)

    def steady(i, carry):
        t = 2 + 2 * i
        step(t, 0)
        step(t + 1, 1)
        return carry

    lax.fori_loop(0, (n_steps - 2) // 2, steady, 0, unroll=True)
    step(n_steps, 0, do_scores=False)
    step(n_steps + 1, 1, do_scores=False, do_softmax=False)


def _neighbourhood_attention(proj_na, bias_tbl, batch, seq):
    rows = seq // GRID_W
    n_pairs = NA_HEADS // 2
    blk = (1, seq, 2 * NA_HEAD_DIM)
    n_var, nk, nq = bias_tbl.shape[1:]
    return pl.pallas_call(
        functools.partial(_na_kernel, rows),
        grid=(batch, n_pairs),
        in_specs=[
            pl.BlockSpec(blk, lambda b, h: (b, 0, h)),
            pl.BlockSpec(blk, lambda b, h: (b, 0, n_pairs + h)),
            pl.BlockSpec(blk, lambda b, h: (b, 0, 2 * n_pairs + h)),
            pl.BlockSpec((2, n_var, nk, nq), lambda b, h: (h, 0, 0, 0)),
        ],
        out_specs=pl.BlockSpec(blk, lambda b, h: (b, 0, h)),
        out_shape=jax.ShapeDtypeStruct((batch, seq, NA_W), BF16),
        scratch_shapes=[pltpu.VMEM((seq // 128, 2 * NA_HEAD_DIM, 128), BF16),
                        pltpu.VMEM((nk, 2 * nq), F32), pltpu.VMEM((nk, 2 * nq), F32),
                        pltpu.VMEM((nk, 2 * nq), BF16), pltpu.VMEM((nk, 2 * nq), BF16),
                        pltpu.VMEM((1, 2 * nq), F32), pltpu.VMEM((1, 2 * nq), F32)],
        compiler_params=_params("parallel", "arbitrary"),
        name="neighbourhood_attention",
    )(proj_na, proj_na, proj_na, bias_tbl)


@functools.lru_cache(maxsize=None)
def _dft_factors(seq):
    n = 2 * seq
    t = np.arange(seq, dtype=np.int64)
    f1 = np.arange(seq // 64, dtype=np.int64)
    f0 = np.arange(64, dtype=np.int64)
    a = 2.0 * np.pi * ((64 * f1[:, None] * t[None, :]) % n).astype(np.float64) / n
    b = 2.0 * np.pi * ((f0[:, None] * t[None, :]) % n).astype(np.float64) / n
    ny = np.where(t % 2 == 0, 1.0, -1.0)
    return tuple(np.asarray(v, np.float32) for v in (np.cos(a), np.sin(a), np.cos(b), np.sin(b), ny))


def _dft_table_kernel(u_ref, v_ref, cb_ref, sb_ref, fwd_ref, inv_ref):
    sin_half = pl.program_id(0) == 1
    first = pl.program_id(1) == 0
    n_fine, seq = cb_ref.shape
    row = lax.broadcasted_iota(I32, (n_fine, seq), 0)
    col = lax.broadcasted_iota(I32, (n_fine, seq), 1)
    for c in range(u_ref.shape[1]):
        tile = u_ref[0, c:c + 1, :] * cb_ref[...] + v_ref[0, c:c + 1, :] * sb_ref[...]
        rows = slice(c * n_fine, (c + 1) * n_fine)
        fwd = tile
        if c == 0:
            fwd_nyq = jnp.logical_and(jnp.logical_and(sin_half, first), row == 0)
            fwd = jnp.where(fwd_nyq, (1 - 2 * (col & 1)).astype(F32), tile)
        fwd_ref[0, rows, :] = fwd.astype(BF16)
        inv_nyq = jnp.logical_and(sin_half, col == 0)
        inv_ref[rows, :] = jnp.where(inv_nyq, (1 - 2 * (row & 1)).astype(F32), tile).astype(BF16)


def _dft_tables(seq):
    ca, sa, cb, sb, _ = _dft_factors(seq)
    u = np.stack([ca, sa])
    v = np.stack([-sa, ca])
    n_coarse = 8
    rows = 64 * n_coarse
    coarse_spec = pl.BlockSpec((1, n_coarse, seq), lambda h, i: (h, i, 0))
    fine_spec = pl.BlockSpec((64, seq), lambda h, i: (0, 0))
    return pl.pallas_call(
        _dft_table_kernel,
        grid=(2, seq // rows),
        in_specs=[coarse_spec, coarse_spec, fine_spec, fine_spec],
        out_specs=[pl.BlockSpec((1, rows, seq), lambda h, i: (h, i, 0)),
                   pl.BlockSpec((rows, seq), lambda h, i: (i, h))],
        out_shape=[jax.ShapeDtypeStruct((2, seq, seq), BF16), jax.ShapeDtypeStruct((seq, 2 * seq), BF16)],
        compiler_params=_params("parallel", "arbitrary"),
        name="dft_tables",
    )(jnp.asarray(u), jnp.asarray(v), jnp.asarray(cb), jnp.asarray(sb))


@functools.lru_cache(maxsize=None)
def _hyena_consts(seq):
    t = np.arange(seq, dtype=np.float64)
    t01 = t / (seq - 1)
    bands = np.linspace(1e-4, HY_BANDS - 1, HY_BANDS)
    ang = (2.0 * math.pi) * (t[:, None] / seq) * bands[None, :]
    feats = np.concatenate([t01[:, None], np.cos(ang), -np.sin(ang)], axis=-1)
    feats_p = np.zeros((seq, 128), np.float32)
    feats_p[:, :HY_POS_DIM] = feats
    min_decay = math.log(1e-2) / 1.5
    max_decay = math.log(1e-2) / 0.3
    deltas = np.abs(np.linspace(min_decay, max_decay, HY_W))
    window = np.exp(-t01[:, None] * deltas[None, :]).astype(np.float32)
    return feats_p, window


def _filter_kernel(feats_ref, w1_ref, b1_ref, w2_ref, b2_ref, freq_ref, w3_ref, win_ref, o_ref, hid_ref):
    hp = lax.Precision.HIGHEST

    @pl.when(pl.program_id(0) == 0)
    def _():
        f = freq_ref[...]
        h1 = jnp.sin(f * (jnp.dot(feats_ref[...], w1_ref[...], precision=hp, preferred_element_type=F32)
                          + b1_ref[...]))
        hid_ref[...] = jnp.sin(f * (jnp.dot(h1, w2_ref[...], precision=hp, preferred_element_type=F32)
                                    + b2_ref[...]))

    h = jnp.dot(hid_ref[...], w3_ref[...], precision=hp, preferred_element_type=F32) * win_ref[...]
    row = lax.broadcasted_iota(I32, h.shape, 0)
    drop = jnp.logical_and(row == 0, pl.program_id(0) >= HY_ORDER)
    o_ref[...] = jnp.where(drop, 0.0, h).astype(BF16)


def _hyena_filters_time(w1, b1, w2, b2, w3, freq, seq):
    feats, window = _hyena_consts(seq)
    w1p = jnp.zeros((128, HY_FILT_FF), F32).at[:HY_POS_DIM].set(w1)
    w3r = w3.reshape(HY_FILT_FF, HY_ORDER, 2, HY_W).transpose(0, 2, 1, 3).reshape(HY_FILT_FF, 4 * HY_W)
    full = lambda shape: pl.BlockSpec(shape, lambda c: (0,) * len(shape))
    return pl.pallas_call(
        _filter_kernel,
        grid=(4,),
        in_specs=[
            full((seq, 128)), full((128, HY_FILT_FF)), full((1, HY_FILT_FF)),
            full((HY_FILT_FF, HY_FILT_FF)), full((1, HY_FILT_FF)), full((1, HY_FILT_FF)),
            pl.BlockSpec((HY_FILT_FF, HY_W), lambda c: (0, c)),
            full((seq, HY_W)),
        ],
        out_specs=pl.BlockSpec((seq, HY_W), lambda c: (0, c)),
        out_shape=jax.ShapeDtypeStruct((seq, 4 * HY_W), BF16),
        scratch_shapes=[pltpu.VMEM((seq, HY_FILT_FF), F32)],
        compiler_params=_params("arbitrary"),
        name="hyena_filter_mlp",
    )(jnp.asarray(feats), w1p, b1.reshape(1, -1), w2, b2.reshape(1, -1), freq.reshape(1, -1), w3r,
      jnp.asarray(window))


def _filter_dft_kernel(f_ref, fwd_ref, bwd_ref, o_ref):
    fwd = fwd_ref[...]
    bwd = bwd_ref[...]
    uc = jnp.dot(f_ref[0], fwd, preferred_element_type=F32)
    vc = jnp.dot(f_ref[0], bwd, preferred_element_type=F32)
    us = jnp.dot(f_ref[1], fwd, preferred_element_type=F32)
    vs = jnp.dot(f_ref[1], bwd, preferred_element_type=F32)
    o_ref[0] = uc + vc
    row = lax.broadcasted_iota(I32, us.shape, 0)
    nyq = jnp.logical_and(row == 0, pl.program_id(0) == 0)
    o_ref[1] = jnp.where(nyq, us + vs, us - vs)


def _filter_dft(dft_fwd, filt_t, seq, fb):
    return pl.pallas_call(
        _filter_dft_kernel,
        grid=(seq // fb, HY_ORDER),
        in_specs=[
            pl.BlockSpec((2, fb, seq), lambda f, o: (0, f, 0)),
            pl.BlockSpec((seq, HY_W), lambda f, o: (0, o)),
            pl.BlockSpec((seq, HY_W), lambda f, o: (0, HY_ORDER + o)),
        ],
        out_specs=pl.BlockSpec((2, fb, HY_W), lambda f, o: (0, f, o)),
        out_shape=jax.ShapeDtypeStruct((2, seq, HY_ORDER * HY_W), F32),
        compiler_params=_params("parallel", "arbitrary"),
        name="hyena_filter_dft",
    )(dft_fwd, filt_t, filt_t)


def _short_conv_kernel(p_ref, w_ref, b_ref, o_ref):
    p = p_ref[0].astype(F32)
    seq = p.shape[0]
    row = lax.broadcasted_iota(I32, p.shape, 0)
    prev = jnp.where(row == 0, 0.0, pltpu.roll(p, 1, 0))
    nxt = jnp.where(row == seq - 1, 0.0, pltpu.roll(p, seq - 1, 0))
    w = w_ref[...]
    o_ref[0] = prev * w[0:1] + p * w[1:2] + nxt * w[2:3] + b_ref[...]


def _short_conv(proj, col0, conv_w, conv_b, batch, seq, tc):
    n_cols = conv_w.shape[-1]
    assert col0 % tc == 0 and n_cols % tc == 0
    cb0 = col0 // tc
    return pl.pallas_call(
        _short_conv_kernel,
        grid=(batch, n_cols // tc),
        in_specs=[
            pl.BlockSpec((1, seq, tc), lambda b, c: (b, 0, cb0 + c)),
            pl.BlockSpec((3, tc), lambda b, c: (0, c)),
            pl.BlockSpec((1, tc), lambda b, c: (0, c)),
        ],
        out_specs=pl.BlockSpec((1, seq, tc), lambda b, c: (b, 0, c)),
        out_shape=jax.ShapeDtypeStruct((batch, seq, n_cols), F32),
        compiler_params=_params("parallel", "arbitrary"),
        name="hyena_short_conv",
    )(proj, conv_w, conv_b.reshape(1, -1))


def _spectrum_kernel(n_fft, f_ref, z_ref, k_ref, o_ref):
    z = z_ref[0].astype(BF16)
    xr = jnp.dot(f_ref[0], z, preferred_element_type=F32)
    xs = jnp.dot(f_ref[1], z, preferred_element_type=F32)
    kr = k_ref[0]
    ks = k_ref[1]
    row = lax.broadcasted_iota(I32, xr.shape, 0)
    edge = jnp.logical_and(row == 0, pl.program_id(0) == 0)
    yr = jnp.where(edge, xr * kr * (1.0 / n_fft), (xr * kr - xs * ks) * (2.0 / n_fft))
    ys = jnp.where(edge, xs * ks * (1.0 / n_fft), (xr * ks + xs * kr) * (2.0 / n_fft))
    o_ref[0, 0] = yr.astype(BF16)
    o_ref[0, 1] = ys.astype(BF16)


def _spectrum_product(dft_fwd, z_arr, z_col, kfreq, order, batch, seq, fb):
    return pl.pallas_call(
        functools.partial(_spectrum_kernel, 2 * seq),
        grid=(seq // fb, batch),
        in_specs=[
            pl.BlockSpec((2, fb, seq), lambda f, b: (0, f, 0)),
            pl.BlockSpec((1, seq, HY_W), lambda f, b: (b, 0, z_col)),
            pl.BlockSpec((2, fb, HY_W), lambda f, b: (0, f, order)),
        ],
        out_specs=pl.BlockSpec((1, 2, fb, HY_W), lambda f, b: (b, 0, f, 0)),
        out_shape=jax.ShapeDtypeStruct((batch, 2, seq, HY_W), BF16),
        compiler_params=_params("parallel", "arbitrary"),
        name="hyena_spectrum",
    )(dft_fwd, z_arr, kfreq)


def _inverse_kernel(ft_ref, y_ref, gate_ref, z_ref, d_ref, o_ref):
    conv = jnp.dot(ft_ref[...], y_ref[0], preferred_element_type=F32)
    o_ref[0] = gate_ref[0] * (conv + d_ref[...] * z_ref[0])


def _inverse_gate(dft_inv, y, gate_arr, gate_col, z_arr, z_col, skip_row, batch, seq, tb):
    y2 = y.reshape(batch, 2 * seq, HY_W)
    return pl.pallas_call(
        _inverse_kernel,
        grid=(seq // tb, batch),
        in_specs=[
            pl.BlockSpec((tb, 2 * seq), lambda t, b: (t, 0)),
            pl.BlockSpec((1, 2 * seq, HY_W), lambda t, b: (b, 0, 0)),
            pl.BlockSpec((1, tb, HY_W), lambda t, b: (b, t, gate_col)),
            pl.BlockSpec((1, tb, HY_W), lambda t, b: (b, t, z_col)),
            pl.BlockSpec((1, HY_W), lambda t, b: (0, 0)),
        ],
        out_specs=pl.BlockSpec((1, tb, HY_W), lambda t, b: (b, t, 0)),
        out_shape=jax.ShapeDtypeStruct((batch, seq, HY_W), F32),
        compiler_params=_params("parallel", "arbitrary"),
        name="hyena_inverse_gate",
    )(dft_inv, y2, gate_arr, z_arr, skip_row.reshape(1, HY_W))


def _hyena_mixer(proj, col0, conv_w, conv_b, w1, b1, w2, b2, w3, freq, skip_d, dft_fwd, dft_inv, batch, seq):
    s = _short_conv(proj, col0, conv_w, conv_b, batch, seq, 768)
    filt_t = _hyena_filters_time(w1, b1, w2, b2, w3, freq, seq)
    kfreq = _filter_dft(dft_fwd, filt_t, seq, 512)
    z_arr, z_col = s, 2
    for o in range(HY_ORDER):
        y = _spectrum_product(dft_fwd, z_arr, z_col, kfreq, o, batch, seq, 512)
        z_arr = _inverse_gate(dft_inv, y, s, o, z_arr, z_col, skip_d[o], batch, seq, 512)
        z_col = 0
    return z_arr


@functools.lru_cache(maxsize=None)
def _retention_consts(seq):
    c = RET_CHUNK
    half = RET_HEAD_DIM // 2
    inv = 1.0 / (10000.0 ** np.linspace(0.0, 1.0, half))
    ang = np.arange(seq, dtype=np.float64)[:, None] * inv[None, :]
    cos2 = np.concatenate([np.cos(ang), np.cos(ang)], axis=-1).astype(np.float32)
    sin2 = np.concatenate([-np.sin(ang), np.sin(ang)], axis=-1).astype(np.float32)
    hidx = np.arange(RET_HEADS, dtype=np.float64)
    lg_f = np.log1p(-np.exp2(-5.0 - hidx))[:, None, None]
    lg_b = np.log1p(-np.exp2(-5.5 - hidx))[:, None, None]
    i = np.arange(c, dtype=np.float64)
    diff = i[:, None] - i[None, :]
    ones = np.ones((1, c, c))
    dec = np.where(diff >= 0, np.exp(lg_f * np.maximum(diff, 0.0)), np.exp(lg_b * np.maximum(-diff, 0.0)))
    rowv = lambda v: v[:, :, None] * ones
    tab = np.stack([
        dec,
        rowv(np.exp(lg_f[:, :, 0] * (i + 1.0)[None, :])),
        rowv(np.exp(lg_f[:, :, 0] * (c - 1.0 - i)[None, :])),
        rowv(np.exp(lg_b[:, :, 0] * (c - i)[None, :])),
        rowv(np.exp(lg_b[:, :, 0] * i[None, :])),
        np.exp(lg_f * c) * ones,
        np.exp(lg_b * c) * ones,
    ], axis=1).astype(np.float32)
    return cos2, sin2, tab


def _retention_kernel(q_ref, k_ref, v_ref, g_ref, cos_ref, sin_ref, tab_ref, o_ref, qs_ref, ks_ref, kvf_ref,
                      kvb_ref, a0_ref, a1_ref, y0_ref, y1_ref):
    c = RET_CHUNK
    d = RET_HEAD_DIM
    seq = q_ref.shape[1]
    n_chunks = seq // c
    cos = cos_ref[...]
    sin = sin_ref[...]
    q = q_ref[0].astype(F32)
    k = k_ref[0].astype(F32)
    qs_ref[...] = (q * cos + pltpu.roll(q, d // 2, 1) * sin) * (d ** -0.5)
    ks_ref[...] = k * cos + pltpu.roll(k, d // 2, 1) * sin

    def mm(a, b):
        return jnp.dot(a.astype(BF16), b.astype(BF16), preferred_element_type=F32)

    def chunk(n):
        return pl.ds(pl.multiple_of(n * c, c), c)

    def kv_body(n, carry):
        kc = ks_ref[chunk(n), :]
        vc = v_ref[0, chunk(n), :]
        kvf_ref[n] = mm((kc * tab_ref[0, 2]).T, vc)
        kvb_ref[n] = mm((kc * tab_ref[0, 4]).T, vc)
        return carry

    lax.fori_loop(0, n_chunks, kv_body, 0, unroll=True)

    def scan_fwd(n, state):
        kv = kvf_ref[n]
        kvf_ref[n] = state
        return tab_ref[0, 5] * state + kv

    lax.fori_loop(0, n_chunks, scan_fwd, jnp.zeros((d, d), F32))

    def scan_bwd(m, state):
        n = n_chunks - 1 - m
        kv = kvb_ref[n]
        kvb_ref[n] = state
        return tab_ref[0, 6] * state + kv

    lax.fori_loop(0, n_chunks, scan_bwd, jnp.zeros((d, d), F32))

    def in_chunk(n, a_out):
        qc = qs_ref[chunk(n), :].astype(BF16)
        kc = ks_ref[chunk(n), :].astype(BF16)
        a = lax.dot_general(qc, kc, NT_DIMS, preferred_element_type=F32) * tab_ref[0, 0]
        a_out[...] = a.astype(BF16)

    def mix(n, a_in, y_out):
        qc = qs_ref[chunk(n), :]
        y_out[...] = (jnp.dot(a_in[...], v_ref[0, chunk(n), :], preferred_element_type=F32)
                      + mm(qc * tab_ref[0, 1], kvf_ref[n]) + mm(qc * tab_ref[0, 3], kvb_ref[n]))

    def finish(n, y_in):
        y = y_in[...]
        mu = jnp.mean(y, axis=-1, keepdims=True)
        yc = y - mu
        var = jnp.mean(yc * yc, axis=-1, keepdims=True)
        g = g_ref[0, chunk(n), :].astype(F32)
        o_ref[0, chunk(n), :] = (yc * lax.rsqrt(var + GN_EPS) * (g * jax.nn.sigmoid(g))).astype(BF16)

    a_slots = (a0_ref, a1_ref)
    y_slots = (y0_ref, y1_ref)

    def step(t, parity, do_a=True, do_mix=True, do_finish=True):
        t = jnp.asarray(t, I32)
        if do_a:
            in_chunk(t, a_slots[parity])
        if do_mix:
            mix(t - 1, a_slots[1 - parity], y_slots[1 - parity])
        if do_finish:
            finish(t - 2, y_slots[parity])

    assert n_chunks % 2 == 0 and n_chunks >= 4
    step(0, 0, do_mix=False, do_finish=False)
    step(1, 1, do_finish=False)

    def steady(i, carry):
        step(2 + 2 * i, 0)
        step(3 + 2 * i, 1)
        return carry

    lax.fori_loop(0, (n_chunks - 2) // 2, steady, 0, unroll=True)
    step(n_chunks, 0, do_a=False)
    step(n_chunks + 1, 1, do_a=False, do_mix=False)


def _retention_mixer(proj, col0, batch, seq):
    cos2, sin2, tab = _retention_consts(seq)
    blk = (1, seq, RET_HEAD_DIM)
    h_ = RET_HEADS
    assert col0 % RET_HEAD_DIM == 0
    c0 = col0 // RET_HEAD_DIM
    return pl.pallas_call(
        _retention_kernel,
        grid=(batch, RET_HEADS),
        in_specs=[
            pl.BlockSpec(blk, lambda b, h: (b, 0, c0 + h)),
            pl.BlockSpec(blk, lambda b, h: (b, 0, c0 + h_ + h)),
            pl.BlockSpec(blk, lambda b, h: (b, 0, c0 + 2 * h_ + h)),
            pl.BlockSpec(blk, lambda b, h: (b, 0, c0 + 3 * h_ + h)),
            pl.BlockSpec((seq, RET_HEAD_DIM), lambda b, h: (0, 0)),
            pl.BlockSpec((seq, RET_HEAD_DIM), lambda b, h: (0, 0)),
            pl.BlockSpec((1, 7, RET_CHUNK, RET_CHUNK), lambda b, h: (h, 0, 0, 0)),
        ],
        out_specs=pl.BlockSpec(blk, lambda b, h: (b, 0, h)),
        out_shape=jax.ShapeDtypeStruct((batch, seq, RET_W), BF16),
        scratch_shapes=[pltpu.VMEM((seq, RET_HEAD_DIM), F32), pltpu.VMEM((seq, RET_HEAD_DIM), F32),
                        pltpu.VMEM((seq // RET_CHUNK, RET_HEAD_DIM, RET_HEAD_DIM), F32),
                        pltpu.VMEM((seq // RET_CHUNK, RET_HEAD_DIM, RET_HEAD_DIM), F32),
                        pltpu.VMEM((RET_CHUNK, RET_CHUNK), BF16), pltpu.VMEM((RET_CHUNK, RET_CHUNK), BF16),
                        pltpu.VMEM((RET_CHUNK, RET_HEAD_DIM), F32), pltpu.VMEM((RET_CHUNK, RET_HEAD_DIM), F32)],
        compiler_params=_params("parallel", "arbitrary"),
        name="retention",
    )(proj, proj, proj, proj, jnp.asarray(cos2), jnp.asarray(sin2), jnp.asarray(tab))


def _router_kernel(x_ref, g_ref, wr_ref, hm_ref, aff_ref):
    n_e = aff_ref.shape[1]
    h = _rms(x_ref[...], g_ref[...])
    h_hi = h.astype(BF16)
    hm_ref[...] = h_hi
    h_lo = (h - h_hi.astype(F32)).astype(BF16)
    w = wr_ref[...]
    w_hi = w.astype(BF16)
    w_lo = (w - w_hi.astype(F32)).astype(BF16)
    logits = (jnp.dot(h_hi, w_hi, preferred_element_type=F32) + jnp.dot(h_lo, w_hi, preferred_element_type=F32)
              + jnp.dot(h_hi, w_lo, preferred_element_type=F32)).T[:n_e]
    m = jnp.max(logits, axis=0, keepdims=True)
    e = jnp.exp(logits - m)
    aff_ref[0] = e / jnp.sum(e, axis=0, keepdims=True)


def _router(x2d, gain, w_router, batch, seq, tm):
    m = x2d.shape[0]
    per_b = seq // tm
    w_pad = jnp.zeros((D_MODEL, 128), F32).at[:, :N_EXPERTS].set(w_router)
    return pl.pallas_call(
        _router_kernel,
        grid=(m // tm,),
        in_specs=[
            pl.BlockSpec((tm, D_MODEL), lambda i: (i, 0)),
            pl.BlockSpec((1, D_MODEL), lambda i: (0, 0)),
            pl.BlockSpec((D_MODEL, 128), lambda i: (0, 0)),
        ],
        out_specs=[
            pl.BlockSpec((tm, D_MODEL), lambda i: (i, 0)),
            pl.BlockSpec((1, N_EXPERTS, tm), lambda i: (i // per_b, 0, i % per_b)),
        ],
        out_shape=[
            jax.ShapeDtypeStruct((m, D_MODEL), BF16),
            jax.ShapeDtypeStruct((batch, N_EXPERTS, seq), F32),
        ],
        compiler_params=_params("parallel"),
        name="moe_router",
    )(x2d, gain.reshape(1, D_MODEL), w_pad)


def _sort_descending(x):
    rows, n = x.shape
    lanes = 128
    n_chunks = n // lanes
    chunks = [x[:, c * lanes:(c + 1) * lanes] for c in range(n_chunks)]
    lane = lax.broadcasted_iota(I32, (rows, lanes), 1)
    k = 2
    while k <= n:
        j = k // 2
        while j >= 1:
            nxt = []
            for c, xc in enumerate(chunks):
                if j >= lanes:
                    lo = (c & (j // lanes)) == 0
                    partner = chunks[c ^ (j // lanes)]
                else:
                    lo = (lane & j) == 0
                    partner = jnp.where(lo, pltpu.roll(xc, lanes - j, 1), pltpu.roll(xc, j, 1))
                desc = ((c * lanes) & k) == 0 if k >= lanes else (lane & k) == 0
                hi, lw = jnp.maximum(xc, partner), jnp.minimum(xc, partner)
                if j >= lanes and k >= lanes:
                    nxt.append(hi if lo == desc else lw)
                else:
                    nxt.append(jnp.where(lo == desc, hi, lw))
            chunks = nxt
            j //= 2
        k *= 2
    return jnp.concatenate(chunks, axis=1)


def _topk_kernel(cap, n_e, aff_ref, slot_ref, wsel_ref, slot_t_ref, thr_ref):
    n_rows, n_tok = aff_ref.shape

    def thr_body(i, carry):
        rows8 = pl.ds(pl.multiple_of(i * 8, 8), 8)
        thr_ref[rows8, :] = _sort_descending(aff_ref[rows8, :])[:, cap - 1:cap]
        return carry

    lax.fori_loop(0, n_rows // 8, thr_body, 0)
    a = aff_ref[...]
    thr = thr_ref[...]
    gt = a > thr
    eq = a == thr
    need = cap - jnp.sum(gt.astype(I32), axis=1, keepdims=True)
    upper = (lax.broadcasted_iota(I32, (n_tok, n_tok), 0) < lax.broadcasted_iota(I32, (n_tok, n_tok), 1))
    upper = upper.astype(BF16)
    eq_rank = jnp.dot(eq.astype(BF16), upper, preferred_element_type=F32)
    sel = jnp.logical_or(gt, jnp.logical_and(eq, eq_rank < need.astype(F32)))
    rank = jnp.dot(sel.astype(BF16), upper, preferred_element_type=F32)
    slot = jnp.where(sel, rank, -1.0)
    slot_ref[...] = slot.astype(I32)
    wsel_ref[...] = jnp.where(sel, a, 0.0)
    pad = jnp.full((128 - n_e, n_tok), -1.0, F32)
    for b in range(n_rows // n_e):
        slot_t_ref[b] = jnp.concatenate([slot[b * n_e:(b + 1) * n_e], pad], axis=0).T.astype(I32)


def _topk_select(aff2d, cap, n_e):
    rows, n_tok = aff2d.shape
    spec = pl.BlockSpec((rows, n_tok), lambda i: (0, 0))
    spec_t = pl.BlockSpec((rows // n_e, n_tok, 128), lambda i: (0, 0, 0))
    return pl.pallas_call(
        functools.partial(_topk_kernel, cap, n_e),
        grid=(1,),
        in_specs=[spec],
        out_specs=[spec, spec, spec_t],
        out_shape=[jax.ShapeDtypeStruct((rows, n_tok), I32), jax.ShapeDtypeStruct((rows, n_tok), F32),
                   jax.ShapeDtypeStruct((rows // n_e, n_tok, 128), I32)],
        scratch_shapes=[pltpu.VMEM((rows, 1), F32)],
        compiler_params=_params("arbitrary"),
        name="moe_topk_select",
    )(aff2d)


def _gather_kernel(cap, slot_ref, wsel_ref, hm_ref, xe_ref, gs_ref):
    n_tok = hm_ref.shape[0]
    onehot = slot_ref[0] == lax.broadcasted_iota(I32, (cap, n_tok), 0)
    xe_ref[0] = jnp.dot(onehot.astype(BF16), hm_ref[...], preferred_element_type=F32).astype(BF16)
    gs_ref[0] = jnp.sum(jnp.where(onehot, wsel_ref[0], 0.0), axis=1, keepdims=True)


def _gather_tokens(slot, wsel, hm, batch, seq, cap):
    rows = batch * N_EXPERTS
    row_spec = pl.BlockSpec((1, 1, seq), lambda b, e: (b * N_EXPERTS + e, 0, 0))
    return pl.pallas_call(
        functools.partial(_gather_kernel, cap),
        grid=(batch, N_EXPERTS),
        in_specs=[row_spec, row_spec, pl.BlockSpec((seq, D_MODEL), lambda b, e: (b, 0))],
        out_specs=[
            pl.BlockSpec((1, cap, D_MODEL), lambda b, e: (e, b, 0)),
            pl.BlockSpec((1, cap, 1), lambda b, e: (e, b, 0)),
        ],
        out_shape=[
            jax.ShapeDtypeStruct((N_EXPERTS, batch * cap, D_MODEL), BF16),
            jax.ShapeDtypeStruct((N_EXPERTS, batch * cap, 1), F32),
        ],
        compiler_params=_params("parallel", "arbitrary"),
        name="moe_gather",
    )(slot.reshape(rows, 1, seq), wsel.reshape(rows, 1, seq), hm)


def _expert_kernel(n_e, n_t, xe_ref, wg_ref, wu_ref, wd_ref, gs_ref, ye_ref, mid_ref):
    g = pl.program_id(0)
    s = pl.program_id(1)
    t = s // 2
    tf = wg_ref.shape[-1]

    @pl.when(jnp.logical_and(s % 2 == 0, g < n_e))
    def _():
        x = xe_ref[0]
        a = jnp.dot(x, wg_ref[...].astype(BF16), preferred_element_type=F32)
        u = jnp.dot(x, wu_ref[...].astype(BF16), preferred_element_type=F32)
        mid_ref[g % 2, t] = (a * jax.nn.sigmoid(a) * u).astype(BF16)

    @pl.when(jnp.logical_and(s % 2 == 1, g >= 1))
    def _():
        prev = (g - 1) % 2
        acc = jnp.dot(mid_ref[prev, 0], wd_ref[0:tf, :].astype(BF16), preferred_element_type=F32)
        for f in range(1, n_t):
            acc += jnp.dot(mid_ref[prev, f], wd_ref[f * tf:(f + 1) * tf, :].astype(BF16),
                           preferred_element_type=F32)
        ye_ref[0] = (acc * gs_ref[0]).astype(BF16)


def _expert_ffn(xe, gs, w_gate, w_up, w_down, layer, tf):
    n_e, rows, d = xe.shape
    ff = w_gate.shape[-1]
    n_t = ff // tf
    assert d // tf == n_t
    last = n_e - 1

    n_s = 2 * n_t

    def fill_tile(g, s, lead):
        v = jnp.minimum(g * n_s + s + lead, n_e * n_s - 1)
        return (layer, v // n_s, 0, (v % n_s) // 2)

    def emit_tile(g, s):
        v = jnp.maximum(g * n_s + s - 1 - n_s, 0)
        return (layer, v // n_s, 0, (v % n_s) // 2)

    return pl.pallas_call(
        functools.partial(_expert_kernel, n_e, n_t),
        grid=(n_e + 1, n_s),
        in_specs=[
            pl.BlockSpec((1, rows, d), lambda g, s: (jnp.minimum(g, last), 0, 0)),
            pl.BlockSpec((None, None, d, tf), functools.partial(fill_tile, lead=1)),
            pl.BlockSpec((None, None, d, tf), functools.partial(fill_tile, lead=0)),
            pl.BlockSpec((None, None, ff, tf), emit_tile),
            pl.BlockSpec((1, rows, 1), lambda g, s: (jnp.maximum(g - 1, 0), 0, 0)),
        ],
        out_specs=pl.BlockSpec((1, rows, tf), lambda g, s: emit_tile(g, s)[1:]),
        out_shape=jax.ShapeDtypeStruct((n_e, rows, d), BF16),
        scratch_shapes=[pltpu.VMEM((2, n_t, rows, tf), BF16)],
        compiler_params=_params("arbitrary", "arbitrary"),
        name="moe_expert_ffn",
    )(xe, w_gate, w_up, w_down, gs)


def _scatter_kernel(cap, slot_ref, ye_ref, x_ref, o_ref, onehot_ref):
    n_e = ye_ref.shape[0]
    n_tok = x_ref.shape[0]

    @pl.when(pl.program_id(1) == 0)
    def _():
        col = lax.broadcasted_iota(I32, (n_tok, cap), 1)
        slots = slot_ref[0]
        for e in range(n_e):
            onehot_ref[:, e * cap:(e + 1) * cap] = (slots[:, e:e + 1] == col).astype(BF16)

    ye = ye_ref[...].reshape(n_e * cap, ye_ref.shape[-1])
    o_ref[...] = x_ref[...] + jnp.dot(onehot_ref[...], ye, preferred_element_type=F32)


def _scatter_add(slot_t, ye, x2d, batch, seq, cap, tn):
    return pl.pallas_call(
        functools.partial(_scatter_kernel, cap),
        grid=(batch, D_MODEL // tn),
        in_specs=[
            pl.BlockSpec((1, seq, 128), lambda b, n: (b, 0, 0)),
            pl.BlockSpec((N_EXPERTS, cap, tn), lambda b, n: (0, b, n)),
            pl.BlockSpec((seq, tn), lambda b, n: (b, n)),
        ],
        out_specs=pl.BlockSpec((seq, tn), lambda b, n: (b, n)),
        out_shape=jax.ShapeDtypeStruct(x2d.shape, F32),
        scratch_shapes=[pltpu.VMEM((seq, N_EXPERTS * cap), BF16)],
        compiler_params=_params("parallel", "arbitrary"),
        name="moe_scatter_add",
    )(slot_t, ye, x2d)


def _expert_choice_ffn(x2d, gain, w_router, w_gate, w_up, w_down, layer, batch, seq):
    cap = EC_CAPACITY * seq // N_EXPERTS
    hm, aff = _router(x2d, gain, w_router, batch, seq, 512)
    slot, wsel, slot_t = _topk_select(aff.reshape(batch * N_EXPERTS, seq), cap, N_EXPERTS)
    xe, gs = _gather_tokens(slot, wsel, hm, batch, seq, cap)
    ye = _expert_ffn(xe, gs, w_gate, w_up, w_down, layer, 512)
    return _scatter_add(slot_t, ye, x2d, batch, seq, cap, 512)


def _final_norm_kernel(x_ref, g_ref, o_ref):
    o_ref[...] = _rms(x_ref[...], g_ref[...])


def _final_norm(x2d, gain, tm):
    m, d = x2d.shape
    return pl.pallas_call(
        _final_norm_kernel,
        grid=(m // tm,),
        in_specs=[pl.BlockSpec((tm, d), lambda i: (i, 0)), pl.BlockSpec((1, d), lambda i: (0, 0))],
        out_specs=pl.BlockSpec((tm, d), lambda i: (i, 0)),
        out_shape=jax.ShapeDtypeStruct((m, d), F32),
        compiler_params=_params("parallel"),
        name="final_norm",
    )(x2d, gain.reshape(1, d))


def kernel(x, mem, norm_mix, w_in, na_rpb, hy_conv_w, hy_conv_b, hy_filt_w1, hy_filt_b1, hy_filt_w2, hy_filt_b2, hy_filt_w3, hy_sin_freq, hy_skip_d, branch_norm, w_out, norm_cross, mem_norm, w_cq, w_ckv, w_co, norm_moe, w_router, w_gate, w_up, w_down, final_norm):
    batch, seq, d = x.shape
    n_mem = mem.shape[1]
    depth = w_in.shape[0]
    m = batch * seq
    na_cols = 3 * NA_W
    hy_cols = 3 * HY_W
    rows = seq // GRID_W

    dft_fwd, dft_inv = _dft_tables(seq)
    x2d = x.reshape(m, d)
    mem2d = mem.reshape(batch * n_mem, d)

    w_in_b, w_out_b, w_cq_b, w_co_b = (w.astype(BF16) for w in (w_in, w_out, w_cq, w_co))
    p_in = w_in.shape[-1]

    for l in range(depth):
        proj = _norm_matmul(x2d, norm_mix[l], _whole(w_in_b, l), 1024, 768, BF16, "in_proj")
        proj = proj.reshape(batch, seq, p_in)

        y_na = _neighbourhood_attention(proj, _na_bias_table(na_rpb[l], rows), batch, seq)
        y_hy = _hyena_mixer(proj, na_cols, hy_conv_w[l], hy_conv_b[l], hy_filt_w1[l],
                            hy_filt_b1[l], hy_filt_w2[l], hy_filt_b2[l], hy_filt_w3[l], hy_sin_freq[l],
                            hy_skip_d[l], dft_fwd, dft_inv, batch, seq)
        y_ret = _retention_mixer(proj, na_cols + hy_cols, batch, seq)
        x2d = _branch_out_proj(y_na.reshape(m, NA_W), y_hy.reshape(m, HY_W), y_ret.reshape(m, RET_W),
                               branch_norm[l], _whole(w_out_b, l), x2d, 1024, 1024)

        q = _norm_matmul(x2d, norm_cross[l], _whole(w_cq_b, l), 1024, 1024, BF16, "cross_q_proj")
        kv = _norm_matmul(mem2d, mem_norm, _whole(w_ckv, l), batch * n_mem, 1024, BF16, "cross_kv_proj")
        x2d = _cross_attend_out_proj(q, kv, _whole(w_co_b, l), x2d, seq, n_mem, 1024, 1024)

        x2d = _expert_choice_ffn(x2d, norm_moe[l], w_router[l], w_gate, w_up, w_down, l, batch, seq)

    return _final_norm(x2d, final_norm, 512).reshape(batch, seq, d)
```

```python
import functools
import math

import numpy as np
import jax
import jax.numpy as jnp
from jax import lax
from jax.experimental import pallas as pl
from jax.experimental.pallas import tpu as pltpu

F32 = jnp.float32
BF16 = jnp.bfloat16
I32 = jnp.int32

D_MODEL = 2048
GRID_W = 64
NA_HEAD_DIM = 64
NA_W = 768
NA_HEADS = 12
NA_KR = 8
NA_KC = 16
HY_W = 512
HY_ORDER = 2
HY_BANDS = 8
HY_POS_DIM = 17
HY_FILT_FF = 64
RET_HEAD_DIM = 128
RET_W = 768
RET_HEADS = 6
RET_CHUNK = 128
CROSS_HEADS = 4
CROSS_HEAD_DIM = 512
N_EXPERTS = 16
EXPERT_FF = 2048
EC_CAPACITY = 2
RMS_EPS = 1e-6
GN_EPS = 1e-5

MASK_VALUE = -1e30
VMEM_LIMIT_BYTES = 56 * 1024 * 1024

NT_DIMS = (((1,), (1,)), ((), ()))


def _params(*sem):
    return pltpu.CompilerParams(dimension_semantics=sem, vmem_limit_bytes=VMEM_LIMIT_BYTES)


def _rms(xf, g):
    return xf * lax.rsqrt(jnp.mean(xf * xf, axis=-1, keepdims=True) + RMS_EPS) * g


def _mm_kernel(prologue, n_pro, has_res, *refs):
    pro_refs = refs[:n_pro]
    w_ref = refs[n_pro]
    res_ref = refs[n_pro + 1] if has_res else None
    o_ref = refs[n_pro + 1 + has_res]
    h_ref = refs[n_pro + 2 + has_res]

    @pl.when(pl.program_id(1) == 0)
    def _():
        prologue(h_ref, *pro_refs)

    w = w_ref[...]
    if w.dtype != BF16:
        w = w.astype(BF16)
    acc = jnp.dot(h_ref[...], w, preferred_element_type=F32)
    if has_res:
        acc = acc + res_ref[...]
    o_ref[...] = acc.astype(o_ref.dtype)


def _fused_matmul(prologue, pro_args, pro_specs, wsel, res, m, tm, tn, out_dtype, name):
    w, layer, col0, n = wsel
    k = w.shape[1]
    cb0 = col0 // tn
    assert col0 % tn == 0 and n % tn == 0 and m % tm == 0
    has_res = res is not None
    in_specs = list(pro_specs) + [pl.BlockSpec((None, k, tn), lambda i, j: (layer, 0, cb0 + j))]
    args = list(pro_args) + [w]
    if has_res:
        in_specs.append(pl.BlockSpec((tm, tn), lambda i, j: (i, j)))
        args.append(res)
    return pl.pallas_call(
        functools.partial(_mm_kernel, prologue, len(pro_args), has_res),
        grid=(m // tm, n // tn),
        in_specs=in_specs,
        out_specs=pl.BlockSpec((tm, tn), lambda i, j: (i, j)),
        out_shape=jax.ShapeDtypeStruct((m, n), out_dtype),
        scratch_shapes=[pltpu.VMEM((tm, k), BF16)],
        compiler_params=_params("parallel", "arbitrary"),
        name=name,
    )(*args)


def _rms_prologue(h_ref, x_ref, g_ref):
    h_ref[...] = _rms(x_ref[...], g_ref[...]).astype(BF16)


def _whole(w, layer):
    return (w, layer, 0, w.shape[-1])


def _norm_matmul(x2d, gain, wsel, tm, tn, out_dtype, name, res=None):
    m, k = x2d.shape
    specs = [pl.BlockSpec((tm, k), lambda i, j: (i, 0)), pl.BlockSpec((1, k), lambda i, j: (0, 0))]
    return _fused_matmul(_rms_prologue, [x2d, gain.reshape(1, k)], specs, wsel, res, m, tm, tn, out_dtype, name)


def _cross_prologue(h_ref, q_ref, k_ref, v_ref):
    dh = CROSS_HEAD_DIM
    for h in range(CROSS_HEADS):
        sl = slice(h * dh, (h + 1) * dh)
        s = lax.dot_general(q_ref[:, sl], k_ref[:, sl], NT_DIMS, preferred_element_type=F32) * (dh ** -0.5)
        m = jnp.max(s, axis=-1, keepdims=True)
        p = jnp.exp(s - m)
        l = jnp.sum(p, axis=-1, keepdims=True)
        o = jnp.dot(p.astype(BF16), v_ref[:, sl], preferred_element_type=F32) * (1.0 / l)
        h_ref[:, sl] = o.astype(BF16)


def _cross_attend_out_proj(q, kv, wsel, res, seq, n_mem, tm, tn):
    m, d = q.shape
    per_b = seq // tm
    assert seq % tm == 0
    specs = [
        pl.BlockSpec((tm, d), lambda i, j: (i, 0)),
        pl.BlockSpec((n_mem, d), lambda i, j: (i // per_b, 0)),
        pl.BlockSpec((n_mem, d), lambda i, j: (i // per_b, 1)),
    ]
    return _fused_matmul(_cross_prologue, [q, kv, kv], specs, wsel, res, m, tm, tn, F32, "cross_attend_out_proj")


def _branch_prologue(h_ref, na_ref, hy_ref, ret_ref, g_ref):
    def nrm(p):
        return p * lax.rsqrt(jnp.mean(p * p, axis=-1, keepdims=True) + RMS_EPS)

    y = jnp.concatenate([nrm(r[...].astype(F32)) for r in (na_ref, hy_ref, ret_ref)], axis=-1)
    h_ref[...] = (y * g_ref[...]).astype(BF16)


def _branch_out_proj(y_na, y_hy, y_ret, gain, wsel, res, tm, tn):
    m = y_na.shape[0]
    specs = [
        pl.BlockSpec((tm, NA_W), lambda i, j: (i, 0)),
        pl.BlockSpec((tm, HY_W), lambda i, j: (i, 0)),
        pl.BlockSpec((tm, RET_W), lambda i, j: (i, 0)),
        pl.BlockSpec((1, D_MODEL), lambda i, j: (0, 0)),
    ]
    return _fused_matmul(_branch_prologue, [y_na, y_hy, y_ret, gain.reshape(1, D_MODEL)], specs, wsel, res,
                         m, tm, tn, F32, "branch_out_proj")


NA_PAIR = 2
NA_BAND = NA_KR + NA_PAIR


def _na_band_base(r, rows):
    return np.clip(r - NA_KR // 2, 0, rows - NA_BAND)


@functools.lru_cache(maxsize=None)
def _na_variants(rows):
    assert rows % NA_PAIR == 0 and rows >= NA_BAND + 2
    n_var = NA_KR // 2 + 1
    dr = np.full((n_var, NA_BAND, NA_PAIR), -2, np.int64)
    for r in range(0, rows, NA_PAIR):
        base = int(_na_band_base(r, rows))
        v = (r - base) // 2
        for j in range(NA_PAIR):
            rs = int(np.clip(r + j - NA_KR // 2, 0, rows - NA_KR))
            for i in range(NA_BAND):
                val = base + i - (r + j) + (NA_KR - 1) if rs <= base + i < rs + NA_KR else -1
                assert dr[v, i, j] in (-2, val)
                dr[v, i, j] = val
    assert (dr > -2).all()
    return dr


def _na_bias_table(rpb, rows):
    c = np.arange(GRID_W)
    col_start = np.clip(c - NA_KC // 2, 0, GRID_W - NA_KC)
    col_in = (c[None, :] >= col_start[:, None]) & (c[None, :] < col_start[:, None] + NA_KC)
    dc = np.clip(c[None, :] - c[:, None] + (NA_KC - 1), 0, 2 * NA_KC - 2)
    onehot = (dc.T.reshape(-1)[None, :] == np.arange(2 * NA_KC - 1)[:, None]).astype(np.float32)
    cols = jnp.einsum("hab,bn->han", rpb.astype(F32), jnp.asarray(onehot), precision=lax.Precision.HIGHEST)
    cols = cols.reshape(NA_HEADS, 2 * NA_KR - 1, GRID_W, GRID_W)
    cols = jnp.where(col_in.T[None, None], cols, MASK_VALUE)
    dr = _na_variants(rows)
    n_var = dr.shape[0]
    return pl.pallas_call(
        functools.partial(_na_table_kernel, dr),
        grid=(NA_HEADS,),
        in_specs=[pl.BlockSpec((1, 2 * NA_KR - 1, GRID_W, GRID_W), lambda h: (h, 0, 0, 0))],
        out_specs=pl.BlockSpec((1, n_var, NA_BAND * GRID_W, NA_PAIR * GRID_W), lambda h: (h, 0, 0, 0)),
        out_shape=jax.ShapeDtypeStruct((NA_HEADS, n_var, NA_BAND * GRID_W, NA_PAIR * GRID_W), F32),
        compiler_params=_params("parallel"),
        name="na_bias_table",
    )(cols)


def _na_table_kernel(dr, cols_ref, o_ref):
    masked = jnp.full((GRID_W, GRID_W), MASK_VALUE, F32)
    for v in range(dr.shape[0]):
        for i in range(NA_BAND):
            blocks = [cols_ref[0, int(dr[v, i, j])] if dr[v, i, j] >= 0 else masked for j in range(NA_PAIR)]
            o_ref[0, v, i * GRID_W:(i + 1) * GRID_W, :] = jnp.concatenate(blocks, axis=-1)


def _na_kernel(rows, q_ref, k_ref, v_ref, bias_ref, o_ref, vt_ref, s0_ref, s1_ref, p0_ref, p1_ref, l0_ref,
               l1_ref):
    dh = NA_HEAD_DIM
    nq = NA_PAIR * GRID_W
    nk = NA_BAND * GRID_W
    n_chunk = nk // 128
    lane = lax.broadcasted_iota(I32, (nq, 2 * dh), 1)

    for ch in range(vt_ref.shape[0]):
        vt_ref[ch] = v_ref[0, ch * 128:(ch + 1) * 128, :].astype(F32).T.astype(BF16)

    n_steps = rows // NA_PAIR

    def band_base(p):
        return jnp.clip(p * NA_PAIR - NA_KR // 2, 0, rows - NA_BAND)

    def scores(p, s_out):
        r = p * NA_PAIR
        base = band_base(p)
        variant = (r - base) // 2
        q = q_ref[0, pl.ds(pl.multiple_of(r * GRID_W, nq), nq), :] * (dh ** -0.5)
        kb = k_ref[0, pl.ds(pl.multiple_of(base * GRID_W, 128), nk), :]
        zero = jnp.zeros_like(q)
        q2 = jnp.concatenate([jnp.where(lane < dh, q, zero), jnp.where(lane >= dh, q, zero)], axis=0)
        st2 = lax.dot_general(kb, q2, NT_DIMS, preferred_element_type=F32)
        for hh in range(2):
            s_out[:, hh * nq:(hh + 1) * nq] = st2[:, hh * nq:(hh + 1) * nq] + bias_ref[hh, variant]

    def softmax(s_in, p_out, l_out):
        st = s_in[...]
        pt = jnp.exp(st - jnp.max(st, axis=0, keepdims=True))
        p_out[...] = pt.astype(BF16)
        l_out[...] = 1.0 / jnp.sum(pt, axis=0, keepdims=True)

    def values(p, p_in, l_in):
        r = p * NA_PAIR
        c0 = band_base(p) // 2
        vt = jnp.concatenate([vt_ref[c0 + i] for i in range(n_chunk)], axis=1)
        pt = p_in[...]
        linv = l_in[...]
        outs = []
        for hh in range(2):
            cols = slice(hh * nq, (hh + 1) * nq)
            ot = jnp.dot(vt[hh * dh:(hh + 1) * dh, :], pt[:, cols], preferred_element_type=F32)
            outs.append(ot * linv[:, cols])
        o_ref[0, pl.ds(pl.multiple_of(r * GRID_W, nq), nq), :] = jnp.concatenate(outs, axis=0).T.astype(BF16)

    s_slots = (s0_ref, s1_ref)
    p_slots = (p0_ref, p1_ref)
    l_slots = (l0_ref, l1_ref)

    def step(t, parity, do_scores=True, do_softmax=True, do_values=True):
        a, b = parity, 1 - parity
        t = jnp.asarray(t, I32)
        if do_scores:
            scores(t, s_slots[a])
        if do_softmax:
            softmax(s_slots[b], p_slots[b], l_slots[b])
        if do_values:
            values(t - 2, p_slots[a], l_slots[a])

    assert n_steps % 2 == 0 and n_steps >= 4
    step(0, 0, do_softmax=False, do_values=False)
    step(1, 1, do_values=False)

    def steady(i, carry):
        t = 2 + 2 * i
        step(t, 0)
        step(t + 1, 1)
        return carry

    lax.fori_loop(0, (n_steps - 2) // 2, steady, 0, unroll=True)
    step(n_steps, 0, do_scores=False)
    step(n_steps + 1, 1, do_scores=False, do_softmax=False)


def _neighbourhood_attention(proj_na, bias_tbl, batch, seq):
    rows = seq // GRID_W
    n_pairs = NA_HEADS // 2
    blk = (1, seq, 2 * NA_HEAD_DIM)
    n_var, nk, nq = bias_tbl.shape[1:]
    return pl.pallas_call(
        functools.partial(_na_kernel, rows),
        grid=(batch, n_pairs),
        in_specs=[
            pl.BlockSpec(blk, lambda b, h: (b, 0, h)),
            pl.BlockSpec(blk, lambda b, h: (b, 0, n_pairs + h)),
            pl.BlockSpec(blk, lambda b, h: (b, 0, 2 * n_pairs + h)),
            pl.BlockSpec((2, n_var, nk, nq), lambda b, h: (h, 0, 0, 0)),
        ],
        out_specs=pl.BlockSpec(blk, lambda b, h: (b, 0, h)),
        out_shape=jax.ShapeDtypeStruct((batch, seq, NA_W), BF16),
        scratch_shapes=[pltpu.VMEM((seq // 128, 2 * NA_HEAD_DIM, 128), BF16),
                        pltpu.VMEM((nk, 2 * nq), F32), pltpu.VMEM((nk, 2 * nq), F32),
                        pltpu.VMEM((nk, 2 * nq), BF16), pltpu.VMEM((nk, 2 * nq), BF16),
                        pltpu.VMEM((1, 2 * nq), F32), pltpu.VMEM((1, 2 * nq), F32)],
        compiler_params=_params("parallel", "arbitrary"),
        name="neighbourhood_attention",
    )(proj_na, proj_na, proj_na, bias_tbl)


@functools.lru_cache(maxsize=None)
def _dft_factors(seq):
    n = 2 * seq
    t = np.arange(seq, dtype=np.int64)
    f1 = np.arange(seq // 64, dtype=np.int64)
    f0 = np.arange(64, dtype=np.int64)
    a = 2.0 * np.pi * ((64 * f1[:, None] * t[None, :]) % n).astype(np.float64) / n
    b = 2.0 * np.pi * ((f0[:, None] * t[None, :]) % n).astype(np.float64) / n
    ny = np.where(t % 2 == 0, 1.0, -1.0)
    return tuple(np.asarray(v, np.float32) for v in (np.cos(a), np.sin(a), np.cos(b), np.sin(b), ny))


def _dft_table_kernel(u_ref, v_ref, cb_ref, sb_ref, fwd_ref, inv_ref):
    sin_half = pl.program_id(0) == 1
    first = pl.program_id(1) == 0
    n_fine, seq = cb_ref.shape
    row = lax.broadcasted_iota(I32, (n_fine, seq), 0)
    col = lax.broadcasted_iota(I32, (n_fine, seq), 1)
    for c in range(u_ref.shape[1]):
        tile = u_ref[0, c:c + 1, :] * cb_ref[...] + v_ref[0, c:c + 1, :] * sb_ref[...]
        rows = slice(c * n_fine, (c + 1) * n_fine)
        fwd = tile
        if c == 0:
            fwd_nyq = jnp.logical_and(jnp.logical_and(sin_half, first), row == 0)
            fwd = jnp.where(fwd_nyq, (1 - 2 * (col & 1)).astype(F32), tile)
        fwd_ref[0, rows, :] = fwd.astype(BF16)
        inv_nyq = jnp.logical_and(sin_half, col == 0)
        inv_ref[rows, :] = jnp.where(inv_nyq, (1 - 2 * (row & 1)).astype(F32), tile).astype(BF16)


def _dft_tables(seq):
    ca, sa, cb, sb, _ = _dft_factors(seq)
    u = np.stack([ca, sa])
    v = np.stack([-sa, ca])
    n_coarse = 8
    rows = 64 * n_coarse
    coarse_spec = pl.BlockSpec((1, n_coarse, seq), lambda h, i: (h, i, 0))
    fine_spec = pl.BlockSpec((64, seq), lambda h, i: (0, 0))
    return pl.pallas_call(
        _dft_table_kernel,
        grid=(2, seq // rows),
        in_specs=[coarse_spec, coarse_spec, fine_spec, fine_spec],
        out_specs=[pl.BlockSpec((1, rows, seq), lambda h, i: (h, i, 0)),
                   pl.BlockSpec((rows, seq), lambda h, i: (i, h))],
        out_shape=[jax.ShapeDtypeStruct((2, seq, seq), BF16), jax.ShapeDtypeStruct((seq, 2 * seq), BF16)],
        compiler_params=_params("parallel", "arbitrary"),
        name="dft_tables",
    )(jnp.asarray(u), jnp.asarray(v), jnp.asarray(cb), jnp.asarray(sb))


@functools.lru_cache(maxsize=None)
def _hyena_consts(seq):
    t = np.arange(seq, dtype=np.float64)
    t01 = t / (seq - 1)
    bands = np.linspace(1e-4, HY_BANDS - 1, HY_BANDS)
    ang = (2.0 * math.pi) * (t[:, None] / seq) * bands[None, :]
    feats = np.concatenate([t01[:, None], np.cos(ang), -np.sin(ang)], axis=-1)
    feats_p = np.zeros((seq, 128), np.float32)
    feats_p[:, :HY_POS_DIM] = feats
    min_decay = math.log(1e-2) / 1.5
    max_decay = math.log(1e-2) / 0.3
    deltas = np.abs(np.linspace(min_decay, max_decay, HY_W))
    window = np.exp(-t01[:, None] * deltas[None, :]).astype(np.float32)
    return feats_p, window


def _filter_kernel(feats_ref, w1_ref, b1_ref, w2_ref, b2_ref, freq_ref, w3f_ref, w3b_ref, win_ref, sum_ref,
                   diff_ref, hid_ref):
    hp = lax.Precision.HIGHEST

    @pl.when(pl.program_id(0) == 0)
    def _():
        f = freq_ref[...]
        h1 = jnp.sin(f * (jnp.dot(feats_ref[...], w1_ref[...], precision=hp, preferred_element_type=F32)
                          + b1_ref[...]))
        hid_ref[...] = jnp.sin(f * (jnp.dot(h1, w2_ref[...], precision=hp, preferred_element_type=F32)
                                    + b2_ref[...]))

    hid = hid_ref[...]
    win = win_ref[...]
    fwd = jnp.dot(hid, w3f_ref[...], precision=hp, preferred_element_type=F32) * win
    bwd = jnp.dot(hid, w3b_ref[...], precision=hp, preferred_element_type=F32) * win
    bwd = jnp.where(lax.broadcasted_iota(I32, bwd.shape, 0) == 0, 0.0, bwd)
    sum_ref[...] = (fwd + bwd).astype(BF16)
    diff_ref[...] = (fwd - bwd).astype(BF16)


def _hyena_filters_time(w1, b1, w2, b2, w3, freq, seq):
    feats, window = _hyena_consts(seq)
    w1p = jnp.zeros((128, HY_FILT_FF), F32).at[:HY_POS_DIM].set(w1)
    full = lambda shape: pl.BlockSpec(shape, lambda o: (0,) * len(shape))
    out_spec = pl.BlockSpec((seq, HY_W), lambda o: (0, o))
    out_shape = jax.ShapeDtypeStruct((seq, HY_ORDER * HY_W), BF16)
    return pl.pallas_call(
        _filter_kernel,
        grid=(HY_ORDER,),
        in_specs=[
            full((seq, 128)), full((128, HY_FILT_FF)), full((1, HY_FILT_FF)),
            full((HY_FILT_FF, HY_FILT_FF)), full((1, HY_FILT_FF)), full((1, HY_FILT_FF)),
            pl.BlockSpec((HY_FILT_FF, HY_W), lambda o: (0, 2 * o)),
            pl.BlockSpec((HY_FILT_FF, HY_W), lambda o: (0, 2 * o + 1)),
            full((seq, HY_W)),
        ],
        out_specs=[out_spec, out_spec],
        out_shape=[out_shape, out_shape],
        scratch_shapes=[pltpu.VMEM((seq, HY_FILT_FF), F32)],
        compiler_params=_params("arbitrary"),
        name="hyena_filter_mlp",
    )(jnp.asarray(feats), w1p, b1.reshape(1, -1), w2, b2.reshape(1, -1), freq.reshape(1, -1), w3, w3,
      jnp.asarray(window))


def _filter_dft_kernel(f_ref, sum_ref, diff_ref, o_ref):
    o_ref[0] = jnp.dot(f_ref[0], sum_ref[...], preferred_element_type=F32)
    o_ref[1] = jnp.dot(f_ref[1], diff_ref[...], preferred_element_type=F32)

    @pl.when(pl.program_id(0) == 0)
    def _():
        top = jnp.dot(f_ref[1, 0:8, :], sum_ref[...], preferred_element_type=F32)
        row = lax.broadcasted_iota(I32, top.shape, 0)
        o_ref[1, 0:8, :] = jnp.where(row == 0, top, o_ref[1, 0:8, :])


def _filter_dft(dft_fwd, filt_sum, filt_diff, seq, fb):
    return pl.pallas_call(
        _filter_dft_kernel,
        grid=(seq // fb, HY_ORDER),
        in_specs=[
            pl.BlockSpec((2, fb, seq), lambda f, o: (0, f, 0)),
            pl.BlockSpec((seq, HY_W), lambda f, o: (0, o)),
            pl.BlockSpec((seq, HY_W), lambda f, o: (0, o)),
        ],
        out_specs=pl.BlockSpec((2, fb, HY_W), lambda f, o: (0, f, o)),
        out_shape=jax.ShapeDtypeStruct((2, seq, HY_ORDER * HY_W), F32),
        compiler_params=_params("parallel", "arbitrary"),
        name="hyena_filter_dft",
    )(dft_fwd, filt_sum, filt_diff)


def _short_conv_kernel(p_ref, w_ref, b_ref, o_ref):
    p = p_ref[0].astype(F32)
    seq = p.shape[0]
    row = lax.broadcasted_iota(I32, p.shape, 0)
    prev = jnp.where(row == 0, 0.0, pltpu.roll(p, 1, 0))
    nxt = jnp.where(row == seq - 1, 0.0, pltpu.roll(p, seq - 1, 0))
    w = w_ref[...]
    o_ref[0] = prev * w[0:1] + p * w[1:2] + nxt * w[2:3] + b_ref[...]


def _short_conv(proj, col0, conv_w, conv_b, batch, seq, tc):
    n_cols = conv_w.shape[-1]
    assert col0 % tc == 0 and n_cols % tc == 0
    cb0 = col0 // tc
    return pl.pallas_call(
        _short_conv_kernel,
        grid=(batch, n_cols // tc),
        in_specs=[
            pl.BlockSpec((1, seq, tc), lambda b, c: (b, 0, cb0 + c)),
            pl.BlockSpec((3, tc), lambda b, c: (0, c)),
            pl.BlockSpec((1, tc), lambda b, c: (0, c)),
        ],
        out_specs=pl.BlockSpec((1, seq, tc), lambda b, c: (b, 0, c)),
        out_shape=jax.ShapeDtypeStruct((batch, seq, n_cols), F32),
        compiler_params=_params("parallel", "arbitrary"),
        name="hyena_short_conv",
    )(proj, conv_w, conv_b.reshape(1, -1))


def _spectrum_kernel(n_fft, f_ref, z_ref, k_ref, o_ref):
    z = z_ref[0].astype(BF16)
    xr = jnp.dot(f_ref[0], z, preferred_element_type=F32)
    xs = jnp.dot(f_ref[1], z, preferred_element_type=F32)
    kr = k_ref[0]
    ks = k_ref[1]
    row = lax.broadcasted_iota(I32, xr.shape, 0)
    edge = jnp.logical_and(row == 0, pl.program_id(0) == 0)
    yr = jnp.where(edge, xr * kr * (1.0 / n_fft), (xr * kr - xs * ks) * (2.0 / n_fft))
    ys = jnp.where(edge, xs * ks * (1.0 / n_fft), (xr * ks + xs * kr) * (2.0 / n_fft))
    o_ref[0, 0] = yr.astype(BF16)
    o_ref[0, 1] = ys.astype(BF16)


def _spectrum_product(dft_fwd, z_arr, z_col, kfreq, order, batch, seq, fb):
    return pl.pallas_call(
        functools.partial(_spectrum_kernel, 2 * seq),
        grid=(seq // fb, batch),
        in_specs=[
            pl.BlockSpec((2, fb, seq), lambda f, b: (0, f, 0)),
            pl.BlockSpec((1, seq, HY_W), lambda f, b: (b, 0, z_col)),
            pl.BlockSpec((2, fb, HY_W), lambda f, b: (0, f, order)),
        ],
        out_specs=pl.BlockSpec((1, 2, fb, HY_W), lambda f, b: (b, 0, f, 0)),
        out_shape=jax.ShapeDtypeStruct((batch, 2, seq, HY_W), BF16),
        compiler_params=_params("parallel", "arbitrary"),
        name="hyena_spectrum",
    )(dft_fwd, z_arr, kfreq)


def _inverse_kernel(ft_ref, y_ref, gate_ref, z_ref, d_ref, o_ref):
    conv = jnp.dot(ft_ref[...], y_ref[0], preferred_element_type=F32)
    o_ref[0] = gate_ref[0] * (conv + d_ref[...] * z_ref[0])


def _inverse_gate(dft_inv, y, gate_arr, gate_col, z_arr, z_col, skip_row, batch, seq, tb):
    y2 = y.reshape(batch, 2 * seq, HY_W)
    return pl.pallas_call(
        _inverse_kernel,
        grid=(seq // tb, batch),
        in_specs=[
            pl.BlockSpec((tb, 2 * seq), lambda t, b: (t, 0)),
            pl.BlockSpec((1, 2 * seq, HY_W), lambda t, b: (b, 0, 0)),
            pl.BlockSpec((1, tb, HY_W), lambda t, b: (b, t, gate_col)),
            pl.BlockSpec((1, tb, HY_W), lambda t, b: (b, t, z_col)),
            pl.BlockSpec((1, HY_W), lambda t, b: (0, 0)),
        ],
        out_specs=pl.BlockSpec((1, tb, HY_W), lambda t, b: (b, t, 0)),
        out_shape=jax.ShapeDtypeStruct((batch, seq, HY_W), F32),
        compiler_params=_params("parallel", "arbitrary"),
        name="hyena_inverse_gate",
    )(dft_inv, y2, gate_arr, z_arr, skip_row.reshape(1, HY_W))


def _hyena_mixer(proj, col0, conv_w, conv_b, w1, b1, w2, b2, w3, freq, skip_d, dft_fwd, dft_inv, batch, seq):
    s = _short_conv(proj, col0, conv_w, conv_b, batch, seq, 768)
    filt_sum, filt_diff = _hyena_filters_time(w1, b1, w2, b2, w3, freq, seq)
    kfreq = _filter_dft(dft_fwd, filt_sum, filt_diff, seq, 512)
    z_arr, z_col = s, 2
    for o in range(HY_ORDER):
        y = _spectrum_product(dft_fwd, z_arr, z_col, kfreq, o, batch, seq, 512)
        z_arr = _inverse_gate(dft_inv, y, s, o, z_arr, z_col, skip_d[o], batch, seq, 512)
        z_col = 0
    return z_arr


@functools.lru_cache(maxsize=None)
def _retention_consts(seq):
    c = RET_CHUNK
    half = RET_HEAD_DIM // 2
    inv = 1.0 / (10000.0 ** np.linspace(0.0, 1.0, half))
    ang = np.arange(seq, dtype=np.float64)[:, None] * inv[None, :]
    cos2 = np.concatenate([np.cos(ang), np.cos(ang)], axis=-1).astype(np.float32)
    sin2 = np.concatenate([-np.sin(ang), np.sin(ang)], axis=-1).astype(np.float32)
    hidx = np.arange(RET_HEADS, dtype=np.float64)
    lg_f = np.log1p(-np.exp2(-5.0 - hidx))[:, None, None]
    lg_b = np.log1p(-np.exp2(-5.5 - hidx))[:, None, None]
    i = np.arange(c, dtype=np.float64)
    diff = i[:, None] - i[None, :]
    ones = np.ones((1, c, c))
    dec = np.where(diff >= 0, np.exp(lg_f * np.maximum(diff, 0.0)), np.exp(lg_b * np.maximum(-diff, 0.0)))
    rowv = lambda v: v[:, :, None] * ones
    tab = np.stack([
        dec,
        rowv(np.exp(lg_f[:, :, 0] * (i + 1.0)[None, :])),
        rowv(np.exp(lg_f[:, :, 0] * (c - 1.0 - i)[None, :])),
        rowv(np.exp(lg_b[:, :, 0] * (c - i)[None, :])),
        rowv(np.exp(lg_b[:, :, 0] * i[None, :])),
        np.exp(lg_f * c) * ones,
        np.exp(lg_b * c) * ones,
    ], axis=1).astype(np.float32)
    return cos2, sin2, tab


def _retention_kernel(q_ref, k_ref, v_ref, g_ref, cos_ref, sin_ref, tab_ref, o_ref, qs_ref, ks_ref, kvf_ref,
                      kvb_ref, a0_ref, a1_ref, y0_ref, y1_ref):
    c = RET_CHUNK
    d = RET_HEAD_DIM
    seq = q_ref.shape[1]
    n_chunks = seq // c
    cos = cos_ref[...]
    sin = sin_ref[...]
    q = q_ref[0].astype(F32)
    k = k_ref[0].astype(F32)
    qs_ref[...] = (q * cos + pltpu.roll(q, d // 2, 1) * sin) * (d ** -0.5)
    ks_ref[...] = k * cos + pltpu.roll(k, d // 2, 1) * sin

    def mm(a, b):
        return jnp.dot(a.astype(BF16), b.astype(BF16), preferred_element_type=F32)

    def chunk(n):
        return pl.ds(pl.multiple_of(n * c, c), c)

    def kv_body(n, carry):
        kc = ks_ref[chunk(n), :]
        vc = v_ref[0, chunk(n), :]
        kvf_ref[n] = mm((kc * tab_ref[0, 2]).T, vc)
        kvb_ref[n] = mm((kc * tab_ref[0, 4]).T, vc)
        return carry

    lax.fori_loop(0, n_chunks, kv_body, 0, unroll=True)

    def scan_fwd(n, state):
        kv = kvf_ref[n]
        kvf_ref[n] = state
        return tab_ref[0, 5] * state + kv

    lax.fori_loop(0, n_chunks, scan_fwd, jnp.zeros((d, d), F32))

    def scan_bwd(m, state):
        n = n_chunks - 1 - m
        kv = kvb_ref[n]
        kvb_ref[n] = state
        return tab_ref[0, 6] * state + kv

    lax.fori_loop(0, n_chunks, scan_bwd, jnp.zeros((d, d), F32))

    def in_chunk(n, a_out):
        qc = qs_ref[chunk(n), :].astype(BF16)
        kc = ks_ref[chunk(n), :].astype(BF16)
        a = lax.dot_general(qc, kc, NT_DIMS, preferred_element_type=F32) * tab_ref[0, 0]
        a_out[...] = a.astype(BF16)

    def mix(n, a_in, y_out):
        qc = qs_ref[chunk(n), :]
        y_out[...] = (jnp.dot(a_in[...], v_ref[0, chunk(n), :], preferred_element_type=F32)
                      + mm(qc * tab_ref[0, 1], kvf_ref[n]) + mm(qc * tab_ref[0, 3], kvb_ref[n]))

    def finish(n, y_in):
        y = y_in[...]
        mu = jnp.mean(y, axis=-1, keepdims=True)
        yc = y - mu
        var = jnp.mean(yc * yc, axis=-1, keepdims=True)
        g = g_ref[0, chunk(n), :].astype(F32)
        o_ref[0, chunk(n), :] = (yc * lax.rsqrt(var + GN_EPS) * (g * jax.nn.sigmoid(g))).astype(BF16)

    a_slots = (a0_ref, a1_ref)
    y_slots = (y0_ref, y1_ref)

    def step(t, parity, do_a=True, do_mix=True, do_finish=True):
        t = jnp.asarray(t, I32)
        if do_a:
            in_chunk(t, a_slots[parity])
        if do_mix:
            mix(t - 1, a_slots[1 - parity], y_slots[1 - parity])
        if do_finish:
            finish(t - 2, y_slots[parity])

    assert n_chunks % 2 == 0 and n_chunks >= 4
    step(0, 0, do_mix=False, do_finish=False)
    step(1, 1, do_finish=False)

    def steady(i, carry):
        step(2 + 2 * i, 0)
        step(3 + 2 * i, 1)
        return carry

    lax.fori_loop(0, (n_chunks - 2) // 2, steady, 0, unroll=True)
    step(n_chunks, 0, do_a=False)
    step(n_chunks + 1, 1, do_a=False, do_mix=False)


def _retention_mixer(proj, col0, batch, seq):
    cos2, sin2, tab = _retention_consts(seq)
    blk = (1, seq, RET_HEAD_DIM)
    h_ = RET_HEADS
    assert col0 % RET_HEAD_DIM == 0
    c0 = col0 // RET_HEAD_DIM
    return pl.pallas_call(
        _retention_kernel,
        grid=(batch, RET_HEADS),
        in_specs=[
            pl.BlockSpec(blk, lambda b, h: (b, 0, c0 + h)),
            pl.BlockSpec(blk, lambda b, h: (b, 0, c0 + h_ + h)),
            pl.BlockSpec(blk, lambda b, h: (b, 0, c0 + 2 * h_ + h)),
            pl.BlockSpec(blk, lambda b, h: (b, 0, c0 + 3 * h_ + h)),
            pl.BlockSpec((seq, RET_HEAD_DIM), lambda b, h: (0, 0)),
            pl.BlockSpec((seq, RET_HEAD_DIM), lambda b, h: (0, 0)),
            pl.BlockSpec((1, 7, RET_CHUNK, RET_CHUNK), lambda b, h: (h, 0, 0, 0)),
        ],
        out_specs=pl.BlockSpec(blk, lambda b, h: (b, 0, h)),
        out_shape=jax.ShapeDtypeStruct((batch, seq, RET_W), BF16),
        scratch_shapes=[pltpu.VMEM((seq, RET_HEAD_DIM), F32), pltpu.VMEM((seq, RET_HEAD_DIM), F32),
                        pltpu.VMEM((seq // RET_CHUNK, RET_HEAD_DIM, RET_HEAD_DIM), F32),
                        pltpu.VMEM((seq // RET_CHUNK, RET_HEAD_DIM, RET_HEAD_DIM), F32),
                        pltpu.VMEM((RET_CHUNK, RET_CHUNK), BF16), pltpu.VMEM((RET_CHUNK, RET_CHUNK), BF16),
                        pltpu.VMEM((RET_CHUNK, RET_HEAD_DIM), F32), pltpu.VMEM((RET_CHUNK, RET_HEAD_DIM), F32)],
        compiler_params=_params("parallel", "arbitrary"),
        name="retention",
    )(proj, proj, proj, proj, jnp.asarray(cos2), jnp.asarray(sin2), jnp.asarray(tab))


def _router_kernel(x_ref, g_ref, wr_ref, hm_ref, aff_ref):
    n_e = aff_ref.shape[1]
    h = _rms(x_ref[...], g_ref[...])
    h_hi = h.astype(BF16)
    hm_ref[...] = h_hi
    h_lo = (h - h_hi.astype(F32)).astype(BF16)
    w = wr_ref[...]
    w_hi = w.astype(BF16)
    w_lo = (w - w_hi.astype(F32)).astype(BF16)
    logits = (jnp.dot(h_hi, w_hi, preferred_element_type=F32) + jnp.dot(h_lo, w_hi, preferred_element_type=F32)
              + jnp.dot(h_hi, w_lo, preferred_element_type=F32)).T[:n_e]
    m = jnp.max(logits, axis=0, keepdims=True)
    e = jnp.exp(logits - m)
    aff_ref[0] = e / jnp.sum(e, axis=0, keepdims=True)


def _router(x2d, gain, w_router, batch, seq, tm):
    m = x2d.shape[0]
    per_b = seq // tm
    w_pad = jnp.zeros((D_MODEL, 128), F32).at[:, :N_EXPERTS].set(w_router)
    return pl.pallas_call(
        _router_kernel,
        grid=(m // tm,),
        in_specs=[
            pl.BlockSpec((tm, D_MODEL), lambda i: (i, 0)),
            pl.BlockSpec((1, D_MODEL), lambda i: (0, 0)),
            pl.BlockSpec((D_MODEL, 128), lambda i: (0, 0)),
        ],
        out_specs=[
            pl.BlockSpec((tm, D_MODEL), lambda i: (i, 0)),
            pl.BlockSpec((1, N_EXPERTS, tm), lambda i: (i // per_b, 0, i % per_b)),
        ],
        out_shape=[
            jax.ShapeDtypeStruct((m, D_MODEL), BF16),
            jax.ShapeDtypeStruct((batch, N_EXPERTS, seq), F32),
        ],
        compiler_params=_params("parallel"),
        name="moe_router",
    )(x2d, gain.reshape(1, D_MODEL), w_pad)


def _sort_descending(x):
    rows, n = x.shape
    lanes = 128
    n_chunks = n // lanes
    chunks = [x[:, c * lanes:(c + 1) * lanes] for c in range(n_chunks)]
    lane = lax.broadcasted_iota(I32, (rows, lanes), 1)
    k = 2
    while k <= n:
        j = k // 2
        while j >= 1:
            nxt = []
            for c, xc in enumerate(chunks):
                if j >= lanes:
                    lo = (c & (j // lanes)) == 0
                    partner = chunks[c ^ (j // lanes)]
                else:
                    lo = (lane & j) == 0
                    partner = jnp.where(lo, pltpu.roll(xc, lanes - j, 1), pltpu.roll(xc, j, 1))
                desc = ((c * lanes) & k) == 0 if k >= lanes else (lane & k) == 0
                hi, lw = jnp.maximum(xc, partner), jnp.minimum(xc, partner)
                if j >= lanes and k >= lanes:
                    nxt.append(hi if lo == desc else lw)
                else:
                    nxt.append(jnp.where(lo == desc, hi, lw))
            chunks = nxt
            j //= 2
        k *= 2
    return jnp.concatenate(chunks, axis=1)


def _topk_kernel(cap, n_e, aff_ref, slot_ref, wsel_ref, slot_t_ref, thr_ref):
    n_rows, n_tok = aff_ref.shape

    def thr_body(i, carry):
        rows8 = pl.ds(pl.multiple_of(i * 8, 8), 8)
        thr_ref[rows8, :] = _sort_descending(aff_ref[rows8, :])[:, cap - 1:cap]
        return carry

    lax.fori_loop(0, n_rows // 8, thr_body, 0)
    a = aff_ref[...]
    thr = thr_ref[...]
    gt = a > thr
    eq = a == thr
    need = cap - jnp.sum(gt.astype(I32), axis=1, keepdims=True)
    upper = (lax.broadcasted_iota(I32, (n_tok, n_tok), 0) < lax.broadcasted_iota(I32, (n_tok, n_tok), 1))
    upper = upper.astype(BF16)
    eq_rank = jnp.dot(eq.astype(BF16), upper, preferred_element_type=F32)
    sel = jnp.logical_or(gt, jnp.logical_and(eq, eq_rank < need.astype(F32)))
    rank = jnp.dot(sel.astype(BF16), upper, preferred_element_type=F32)
    slot = jnp.where(sel, rank, -1.0)
    slot_ref[...] = slot.astype(I32)
    wsel_ref[...] = jnp.where(sel, a, 0.0)
    pad = jnp.full((128 - n_e, n_tok), -1.0, F32)
    for b in range(n_rows // n_e):
        slot_t_ref[b] = jnp.concatenate([slot[b * n_e:(b + 1) * n_e], pad], axis=0).T.astype(I32)


def _topk_select(aff2d, cap, n_e):
    rows, n_tok = aff2d.shape
    spec = pl.BlockSpec((rows, n_tok), lambda i: (0, 0))
    spec_t = pl.BlockSpec((rows // n_e, n_tok, 128), lambda i: (0, 0, 0))
    return pl.pallas_call(
        functools.partial(_topk_kernel, cap, n_e),
        grid=(1,),
        in_specs=[spec],
        out_specs=[spec, spec, spec_t],
        out_shape=[jax.ShapeDtypeStruct((rows, n_tok), I32), jax.ShapeDtypeStruct((rows, n_tok), F32),
                   jax.ShapeDtypeStruct((rows // n_e, n_tok, 128), I32)],
        scratch_shapes=[pltpu.VMEM((rows, 1), F32)],
        compiler_params=_params("arbitrary"),
        name="moe_topk_select",
    )(aff2d)


def _gather_kernel(cap, slot_ref, wsel_ref, hm_ref, xe_ref, gs_ref):
    n_tok = hm_ref.shape[0]
    onehot = slot_ref[0] == lax.broadcasted_iota(I32, (cap, n_tok), 0)
    xe_ref[0] = jnp.dot(onehot.astype(BF16), hm_ref[...], preferred_element_type=F32).astype(BF16)
    gs_ref[0] = jnp.sum(jnp.where(onehot, wsel_ref[0], 0.0), axis=1, keepdims=True)


def _gather_tokens(slot, wsel, hm, batch, seq, cap):
    rows = batch * N_EXPERTS
    row_spec = pl.BlockSpec((1, 1, seq), lambda b, e: (b * N_EXPERTS + e, 0, 0))
    return pl.pallas_call(
        functools.partial(_gather_kernel, cap),
        grid=(batch, N_EXPERTS),
        in_specs=[row_spec, row_spec, pl.BlockSpec((seq, D_MODEL), lambda b, e: (b, 0))],
        out_specs=[
            pl.BlockSpec((1, cap, D_MODEL), lambda b, e: (e, b, 0)),
            pl.BlockSpec((1, cap, 1), lambda b, e: (e, b, 0)),
        ],
        out_shape=[
            jax.ShapeDtypeStruct((N_EXPERTS, batch * cap, D_MODEL), BF16),
            jax.ShapeDtypeStruct((N_EXPERTS, batch * cap, 1), F32),
        ],
        compiler_params=_params("parallel", "arbitrary"),
        name="moe_gather",
    )(slot.reshape(rows, 1, seq), wsel.reshape(rows, 1, seq), hm)


def _expert_kernel(n_e, n_t, xe_ref, wg_ref, wu_ref, wd_ref, gs_ref, ye_ref, mid_ref):
    g = pl.program_id(0)
    s = pl.program_id(1)
    t = s // 2
    tf = wg_ref.shape[-1]

    @pl.when(jnp.logical_and(s % 2 == 0, g < n_e))
    def _():
        x = xe_ref[0]
        a = jnp.dot(x, wg_ref[...].astype(BF16), preferred_element_type=F32)
        u = jnp.dot(x, wu_ref[...].astype(BF16), preferred_element_type=F32)
        mid_ref[g % 2, t] = (a * jax.nn.sigmoid(a) * u).astype(BF16)

    @pl.when(jnp.logical_and(s % 2 == 1, g >= 1))
    def _():
        prev = (g - 1) % 2
        acc = jnp.dot(mid_ref[prev, 0], wd_ref[0:tf, :].astype(BF16), preferred_element_type=F32)
        for f in range(1, n_t):
            acc += jnp.dot(mid_ref[prev, f], wd_ref[f * tf:(f + 1) * tf, :].astype(BF16),
                           preferred_element_type=F32)
        ye_ref[0] = (acc * gs_ref[0]).astype(BF16)


def _expert_ffn(xe, gs, w_gate, w_up, w_down, layer, tf):
    n_e, rows, d = xe.shape
    ff = w_gate.shape[-1]
    n_t = ff // tf
    assert d // tf == n_t
    last = n_e - 1

    n_s = 2 * n_t

    def fill_tile(g, s, lead):
        v = jnp.minimum(g * n_s + s + lead, n_e * n_s - 1)
        return (layer, v // n_s, 0, (v % n_s) // 2)

    def emit_tile(g, s):
        v = jnp.maximum(g * n_s + s - 1 - n_s, 0)
        return (layer, v // n_s, 0, (v % n_s) // 2)

    return pl.pallas_call(
        functools.partial(_expert_kernel, n_e, n_t),
        grid=(n_e + 1, n_s),
        in_specs=[
            pl.BlockSpec((1, rows, d), lambda g, s: (jnp.minimum(g, last), 0, 0)),
            pl.BlockSpec((None, None, d, tf), functools.partial(fill_tile, lead=1)),
            pl.BlockSpec((None, None, d, tf), functools.partial(fill_tile, lead=0)),
            pl.BlockSpec((None, None, ff, tf), emit_tile),
            pl.BlockSpec((1, rows, 1), lambda g, s: (jnp.maximum(g - 1, 0), 0, 0)),
        ],
        out_specs=pl.BlockSpec((1, rows, tf), lambda g, s: emit_tile(g, s)[1:]),
        out_shape=jax.ShapeDtypeStruct((n_e, rows, d), BF16),
        scratch_shapes=[pltpu.VMEM((2, n_t, rows, tf), BF16)],
        compiler_params=_params("arbitrary", "arbitrary"),
        name="moe_expert_ffn",
    )(xe, w_gate, w_up, w_down, gs)


def _scatter_kernel(cap, slot_ref, ye_ref, x_ref, o_ref, onehot_ref):
    n_e = ye_ref.shape[0]
    n_tok = x_ref.shape[0]

    @pl.when(pl.program_id(1) == 0)
    def _():
        col = lax.broadcasted_iota(I32, (n_tok, cap), 1)
        slots = slot_ref[0]
        for e in range(n_e):
            onehot_ref[:, e * cap:(e + 1) * cap] = (slots[:, e:e + 1] == col).astype(BF16)

    ye = ye_ref[...].reshape(n_e * cap, ye_ref.shape[-1])
    o_ref[...] = x_ref[...] + jnp.dot(onehot_ref[...], ye, preferred_element_type=F32)


def _scatter_add(slot_t, ye, x2d, batch, seq, cap, tn):
    return pl.pallas_call(
        functools.partial(_scatter_kernel, cap),
        grid=(batch, D_MODEL // tn),
        in_specs=[
            pl.BlockSpec((1, seq, 128), lambda b, n: (b, 0, 0)),
            pl.BlockSpec((N_EXPERTS, cap, tn), lambda b, n: (0, b, n)),
            pl.BlockSpec((seq, tn), lambda b, n: (b, n)),
        ],
        out_specs=pl.BlockSpec((seq, tn), lambda b, n: (b, n)),
        out_shape=jax.ShapeDtypeStruct(x2d.shape, F32),
        scratch_shapes=[pltpu.VMEM((seq, N_EXPERTS * cap), BF16)],
        compiler_params=_params("parallel", "arbitrary"),
        name="moe_scatter_add",
    )(slot_t, ye, x2d)


def _expert_choice_ffn(x2d, gain, w_router, w_gate, w_up, w_down, layer, batch, seq):
    cap = EC_CAPACITY * seq // N_EXPERTS
    hm, aff = _router(x2d, gain, w_router, batch, seq, 512)
    slot, wsel, slot_t = _topk_select(aff.reshape(batch * N_EXPERTS, seq), cap, N_EXPERTS)
    xe, gs = _gather_tokens(slot, wsel, hm, batch, seq, cap)
    ye = _expert_ffn(xe, gs, w_gate, w_up, w_down, layer, 512)
    return _scatter_add(slot_t, ye, x2d, batch, seq, cap, 512)


def _final_norm_kernel(x_ref, g_ref, o_ref):
    o_ref[...] = _rms(x_ref[...], g_ref[...])


def _final_norm(x2d, gain, tm):
    m, d = x2d.shape
    return pl.pallas_call(
        _final_norm_kernel,
        grid=(m // tm,),
        in_specs=[pl.BlockSpec((tm, d), lambda i: (i, 0)), pl.BlockSpec((1, d), lambda i: (0, 0))],
        out_specs=pl.BlockSpec((tm, d), lambda i: (i, 0)),
        out_shape=jax.ShapeDtypeStruct((m, d), F32),
        compiler_params=_params("parallel"),
        name="final_norm",
    )(x2d, gain.reshape(1, d))


def kernel(x, mem, norm_mix, w_in, na_rpb, hy_conv_w, hy_conv_b, hy_filt_w1, hy_filt_b1, hy_filt_w2, hy_filt_b2, hy_filt_w3, hy_sin_freq, hy_skip_d, branch_norm, w_out, norm_cross, mem_norm, w_cq, w_ckv, w_co, norm_moe, w_router, w_gate, w_up, w_down, final_norm):
    batch, seq, d = x.shape
    n_mem = mem.shape[1]
    depth = w_in.shape[0]
    m = batch * seq
    na_cols = 3 * NA_W
    hy_cols = 3 * HY_W
    rows = seq // GRID_W

    dft_fwd, dft_inv = _dft_tables(seq)
    x2d = x.reshape(m, d)
    mem2d = mem.reshape(batch * n_mem, d)

    w_out_b, w_cq_b, w_co_b = (w.astype(BF16) for w in (w_out, w_cq, w_co))
    p_in = w_in.shape[-1]

    for l in range(depth):
        proj = _norm_matmul(x2d, norm_mix[l], _whole(w_in, l), 1024, 768, BF16, "in_proj")
        proj = proj.reshape(batch, seq, p_in)

        y_na = _neighbourhood_attention(proj, _na_bias_table(na_rpb[l], rows), batch, seq)
        y_hy = _hyena_mixer(proj, na_cols, hy_conv_w[l], hy_conv_b[l], hy_filt_w1[l],
                            hy_filt_b1[l], hy_filt_w2[l], hy_filt_b2[l], hy_filt_w3[l], hy_sin_freq[l],
                            hy_skip_d[l], dft_fwd, dft_inv, batch, seq)
        y_ret = _retention_mixer(proj, na_cols + hy_cols, batch, seq)
        x2d = _branch_out_proj(y_na.reshape(m, NA_W), y_hy.reshape(m, HY_W), y_ret.reshape(m, RET_W),
                               branch_norm[l], _whole(w_out_b, l), x2d, 1024, 1024)

        q = _norm_matmul(x2d, norm_cross[l], _whole(w_cq_b, l), 1024, 1024, BF16, "cross_q_proj")
        kv = _norm_matmul(mem2d, mem_norm, _whole(w_ckv, l), batch * n_mem, 1024, BF16, "cross_kv_proj")
        x2d = _cross_attend_out_proj(q, kv, _whole(w_co_b, l), x2d, seq, n_mem, 1024, 1024)

        x2d = _expert_choice_ffn(x2d, norm_moe[l], w_router[l], w_gate, w_up, w_down, l, batch, seq)

    return _final_norm(x2d, final_norm, 512).reshape(batch, seq, d)
```

```python
import functools
import math

import numpy as np
import jax
import jax.numpy as jnp
from jax import lax
from jax.experimental import pallas as pl
from jax.experimental.pallas import tpu as pltpu

F32 = jnp.float32
BF16 = jnp.bfloat16
I32 = jnp.int32

D_MODEL = 2048
GRID_W = 64
NA_HEAD_DIM = 64
NA_W = 768
NA_HEADS = 12
NA_KR = 8
NA_KC = 16
HY_W = 512
HY_ORDER = 2
HY_BANDS = 8
HY_POS_DIM = 17
HY_FILT_FF = 64
RET_HEAD_DIM = 128
RET_W = 768
RET_HEADS = 6
RET_CHUNK = 128
CROSS_HEADS = 4
CROSS_HEAD_DIM = 512
N_EXPERTS = 16
EXPERT_FF = 2048
EC_CAPACITY = 2
RMS_EPS = 1e-6
GN_EPS = 1e-5

MASK_VALUE = -1e30
LANES = 128
VMEM_LIMIT_BYTES = 56 * 1024 * 1024

NT_DIMS = (((1,), (1,)), ((), ()))


def _params(*sem):
    return pltpu.CompilerParams(dimension_semantics=sem, vmem_limit_bytes=VMEM_LIMIT_BYTES)


def _rms(xf, g):
    return xf * lax.rsqrt(jnp.mean(xf * xf, axis=-1, keepdims=True) + RMS_EPS) * g


def _mm_kernel(prologue, n_pro, has_res, *refs):
    pro_refs = refs[:n_pro]
    w_ref = refs[n_pro]
    res_ref = refs[n_pro + 1] if has_res else None
    o_ref = refs[n_pro + 1 + has_res]
    h_ref = refs[n_pro + 2 + has_res]

    @pl.when(pl.program_id(1) == 0)
    def _():
        prologue(h_ref, *pro_refs)

    w = w_ref[...]
    if w.dtype != BF16:
        w = w.astype(BF16)
    acc = jnp.dot(h_ref[...], w, preferred_element_type=F32)
    if has_res:
        acc = acc + res_ref[...]
    o_ref[...] = acc.astype(o_ref.dtype)


def _fused_matmul(prologue, pro_args, pro_specs, wsel, res, m, tm, tn, out_dtype, name):
    w, layer, col0, n = wsel
    k = w.shape[1]
    cb0 = col0 // tn
    assert col0 % tn == 0 and n % tn == 0 and m % tm == 0
    has_res = res is not None
    in_specs = list(pro_specs) + [pl.BlockSpec((None, k, tn), lambda i, j: (layer, 0, cb0 + j))]
    args = list(pro_args) + [w]
    if has_res:
        in_specs.append(pl.BlockSpec((tm, tn), lambda i, j: (i, j)))
        args.append(res)
    return pl.pallas_call(
        functools.partial(_mm_kernel, prologue, len(pro_args), has_res),
        grid=(m // tm, n // tn),
        in_specs=in_specs,
        out_specs=pl.BlockSpec((tm, tn), lambda i, j: (i, j)),
        out_shape=jax.ShapeDtypeStruct((m, n), out_dtype),
        scratch_shapes=[pltpu.VMEM((tm, k), BF16)],
        compiler_params=_params("parallel", "arbitrary"),
        name=name,
    )(*args)


def _rms_prologue(h_ref, x_ref, g_ref):
    h_ref[...] = _rms(x_ref[...], g_ref[...]).astype(BF16)


def _whole(w, layer):
    return (w, layer, 0, w.shape[-1])


def _norm_matmul(x2d, gain, wsel, tm, tn, out_dtype, name, res=None):
    m, k = x2d.shape
    specs = [pl.BlockSpec((tm, k), lambda i, j: (i, 0)), pl.BlockSpec((1, k), lambda i, j: (0, 0))]
    return _fused_matmul(_rms_prologue, [x2d, gain.reshape(1, k)], specs, wsel, res, m, tm, tn, out_dtype, name)


def _cross_prologue(h_ref, q_ref, k_ref, v_ref):
    dh = CROSS_HEAD_DIM
    for h in range(CROSS_HEADS):
        sl = slice(h * dh, (h + 1) * dh)
        s = lax.dot_general(q_ref[:, sl], k_ref[:, sl], NT_DIMS, preferred_element_type=F32) * (dh ** -0.5)
        m = jnp.max(s, axis=-1, keepdims=True)
        p = jnp.exp(s - m)
        l = jnp.sum(p, axis=-1, keepdims=True)
        o = jnp.dot(p.astype(BF16), v_ref[:, sl], preferred_element_type=F32) * (1.0 / l)
        h_ref[:, sl] = o.astype(BF16)


def _cross_attend_out_proj(q, kv, wsel, res, seq, n_mem, tm, tn):
    m, d = q.shape
    per_b = seq // tm
    assert seq % tm == 0
    specs = [
        pl.BlockSpec((tm, d), lambda i, j: (i, 0)),
        pl.BlockSpec((n_mem, d), lambda i, j: (i // per_b, 0)),
        pl.BlockSpec((n_mem, d), lambda i, j: (i // per_b, 1)),
    ]
    return _fused_matmul(_cross_prologue, [q, kv, kv], specs, wsel, res, m, tm, tn, F32, "cross_attend_out_proj")


def _branch_prologue(h_ref, na_ref, hy_ref, ret_ref, g_ref):
    def nrm(p):
        return p * lax.rsqrt(jnp.mean(p * p, axis=-1, keepdims=True) + RMS_EPS)

    y = jnp.concatenate([nrm(r[...].astype(F32)) for r in (na_ref, hy_ref, ret_ref)], axis=-1)
    h_ref[...] = (y * g_ref[...]).astype(BF16)


def _branch_out_proj(y_na, y_hy, y_ret, gain, wsel, res, tm, tn):
    m = y_na.shape[0]
    specs = [
        pl.BlockSpec((tm, NA_W), lambda i, j: (i, 0)),
        pl.BlockSpec((tm, HY_W), lambda i, j: (i, 0)),
        pl.BlockSpec((tm, RET_W), lambda i, j: (i, 0)),
        pl.BlockSpec((1, D_MODEL), lambda i, j: (0, 0)),
    ]
    return _fused_matmul(_branch_prologue, [y_na, y_hy, y_ret, gain.reshape(1, D_MODEL)], specs, wsel, res,
                         m, tm, tn, F32, "branch_out_proj")


NA_PAIR = 2
NA_BAND = NA_KR + NA_PAIR


def _na_band_base(r, rows):
    return np.clip(r - NA_KR // 2, 0, rows - NA_BAND)


@functools.lru_cache(maxsize=None)
def _na_variants(rows):
    assert rows % NA_PAIR == 0 and rows >= NA_BAND + 2
    n_var = NA_KR // 2 + 1
    dr = np.full((n_var, NA_BAND, NA_PAIR), -2, np.int64)
    for r in range(0, rows, NA_PAIR):
        base = int(_na_band_base(r, rows))
        v = (r - base) // 2
        for j in range(NA_PAIR):
            rs = int(np.clip(r + j - NA_KR // 2, 0, rows - NA_KR))
            for i in range(NA_BAND):
                val = base + i - (r + j) + (NA_KR - 1) if rs <= base + i < rs + NA_KR else -1
                assert dr[v, i, j] in (-2, val)
                dr[v, i, j] = val
    assert (dr > -2).all()
    return dr


def _na_bias_table(rpb, rows):
    c = np.arange(GRID_W)
    col_start = np.clip(c - NA_KC // 2, 0, GRID_W - NA_KC)
    col_in = (c[None, :] >= col_start[:, None]) & (c[None, :] < col_start[:, None] + NA_KC)
    dc = np.clip(c[None, :] - c[:, None] + (NA_KC - 1), 0, 2 * NA_KC - 2)
    onehot = (dc.T.reshape(-1)[None, :] == np.arange(2 * NA_KC - 1)[:, None]).astype(np.float32)
    cols = jnp.einsum("hab,bn->han", rpb.astype(F32), jnp.asarray(onehot), precision=lax.Precision.HIGHEST)
    cols = cols.reshape(NA_HEADS, 2 * NA_KR - 1, GRID_W, GRID_W)
    cols = jnp.where(col_in.T[None, None], cols, MASK_VALUE)
    dr = _na_variants(rows)
    n_var = dr.shape[0]
    return pl.pallas_call(
        functools.partial(_na_table_kernel, dr),
        grid=(NA_HEADS,),
        in_specs=[pl.BlockSpec((1, 2 * NA_KR - 1, GRID_W, GRID_W), lambda h: (h, 0, 0, 0))],
        out_specs=pl.BlockSpec((1, n_var, NA_BAND * GRID_W, NA_PAIR * GRID_W), lambda h: (h, 0, 0, 0)),
        out_shape=jax.ShapeDtypeStruct((NA_HEADS, n_var, NA_BAND * GRID_W, NA_PAIR * GRID_W), F32),
        compiler_params=_params("parallel"),
        name="na_bias_table",
    )(cols)


def _na_table_kernel(dr, cols_ref, o_ref):
    masked = jnp.full((GRID_W, GRID_W), MASK_VALUE, F32)
    for v in range(dr.shape[0]):
        for i in range(NA_BAND):
            blocks = [cols_ref[0, int(dr[v, i, j])] if dr[v, i, j] >= 0 else masked for j in range(NA_PAIR)]
            o_ref[0, v, i * GRID_W:(i + 1) * GRID_W, :] = jnp.concatenate(blocks, axis=-1)


def _na_kernel(rows, q_ref, k_ref, v_ref, bias_ref, o_ref, vt_ref, s0_ref, s1_ref, p0_ref, p1_ref, l0_ref,
               l1_ref):
    dh = NA_HEAD_DIM
    nq = NA_PAIR * GRID_W
    nk = NA_BAND * GRID_W
    n_chunk = nk // LANES
    lane = lax.broadcasted_iota(I32, (nq, 2 * dh), 1)

    for ch in range(vt_ref.shape[0]):
        vt_ref[ch] = v_ref[0, ch * LANES:(ch + 1) * LANES, :].astype(F32).T.astype(BF16)

    n_steps = rows // NA_PAIR

    def band_base(p):
        return jnp.clip(p * NA_PAIR - NA_KR // 2, 0, rows - NA_BAND)


    def scores(p, s_out):
        r = p * NA_PAIR
        base = band_base(p)
        variant = (r - base) // 2
        q = q_ref[0, pl.ds(pl.multiple_of(r * GRID_W, nq), nq), :] * (dh ** -0.5)
        kb = k_ref[0, pl.ds(pl.multiple_of(base * GRID_W, LANES), nk), :]
        zero = jnp.zeros_like(q)
        q2 = jnp.concatenate([jnp.where(lane < dh, q, zero), jnp.where(lane >= dh, q, zero)], axis=0)
        st2 = lax.dot_general(kb, q2, NT_DIMS, preferred_element_type=F32)
        for hh in range(2):
            s_out[:, hh * nq:(hh + 1) * nq] = st2[:, hh * nq:(hh + 1) * nq] + bias_ref[hh, variant]

    def softmax(s_in, p_out, l_out):
        st = s_in[...]
        pt = jnp.exp(st - jnp.max(st, axis=0, keepdims=True))
        p_out[...] = pt.astype(BF16)
        l_out[...] = 1.0 / jnp.sum(pt, axis=0, keepdims=True)

    def values(p, p_in, l_in):
        r = p * NA_PAIR
        c0 = band_base(p) // 2
        vt = jnp.concatenate([vt_ref[c0 + i] for i in range(n_chunk)], axis=1)
        pt = p_in[...]
        linv = l_in[...]
        outs = []
        for hh in range(2):
            cols = slice(hh * nq, (hh + 1) * nq)
            ot = jnp.dot(vt[hh * dh:(hh + 1) * dh, :], pt[:, cols], preferred_element_type=F32)
            outs.append(ot * linv[:, cols])
        o_ref[0, pl.ds(pl.multiple_of(r * GRID_W, nq), nq), :] = jnp.concatenate(outs, axis=0).T.astype(BF16)

    s_slots = (s0_ref, s1_ref)
    p_slots = (p0_ref, p1_ref)
    l_slots = (l0_ref, l1_ref)

    def step(t, parity, do_scores=True, do_softmax=True, do_values=True):
        a, b = parity, 1 - parity
        t = jnp.asarray(t, I32)
        if do_scores:
            scores(t, s_slots[a])
        if do_softmax:
            softmax(s_slots[b], p_slots[b], l_slots[b])
        if do_values:
            values(t - 2, p_slots[a], l_slots[a])

    assert n_steps % 2 == 0 and n_steps >= 4
    step(0, 0, do_softmax=False, do_values=False)
    step(1, 1, do_values=False)

    def steady(i, carry):
        t = 2 + 2 * i
        step(t, 0)
        step(t + 1, 1)
        return carry

    lax.fori_loop(0, (n_steps - 2) // 2, steady, 0, unroll=True)
    step(n_steps, 0, do_scores=False)
    step(n_steps + 1, 1, do_scores=False, do_softmax=False)


def _neighbourhood_attention(proj_na, bias_tbl, batch, seq):
    rows = seq // GRID_W
    n_pairs = NA_HEADS // 2
    blk = (1, seq, 2 * NA_HEAD_DIM)
    n_var, nk, nq = bias_tbl.shape[1:]
    return pl.pallas_call(
        functools.partial(_na_kernel, rows),
        grid=(batch, n_pairs),
        in_specs=[
            pl.BlockSpec(blk, lambda b, h: (b, 0, h)),
            pl.BlockSpec(blk, lambda b, h: (b, 0, n_pairs + h)),
            pl.BlockSpec(blk, lambda b, h: (b, 0, 2 * n_pairs + h)),
            pl.BlockSpec((2, n_var, nk, nq), lambda b, h: (h, 0, 0, 0)),
        ],
        out_specs=pl.BlockSpec(blk, lambda b, h: (b, 0, h)),
        out_shape=jax.ShapeDtypeStruct((batch, seq, NA_W), BF16),
        scratch_shapes=[pltpu.VMEM((seq // LANES, 2 * NA_HEAD_DIM, LANES), BF16),
                        pltpu.VMEM((nk, 2 * nq), F32), pltpu.VMEM((nk, 2 * nq), F32),
                        pltpu.VMEM((nk, 2 * nq), BF16), pltpu.VMEM((nk, 2 * nq), BF16),
                        pltpu.VMEM((1, 2 * nq), F32), pltpu.VMEM((1, 2 * nq), F32)],
        compiler_params=_params("parallel", "arbitrary"),
        name="neighbourhood_attention",
    )(proj_na, proj_na, proj_na, bias_tbl)


@functools.lru_cache(maxsize=None)
def _dft_factors(seq):
    n = 2 * seq
    t = np.arange(seq, dtype=np.int64)
    f1 = np.arange(seq // 64, dtype=np.int64)
    f0 = np.arange(64, dtype=np.int64)
    a = 2.0 * np.pi * ((64 * f1[:, None] * t[None, :]) % n).astype(np.float64) / n
    b = 2.0 * np.pi * ((f0[:, None] * t[None, :]) % n).astype(np.float64) / n
    ny = np.where(t % 2 == 0, 1.0, -1.0)
    return tuple(np.asarray(v, np.float32) for v in (np.cos(a), np.sin(a), np.cos(b), np.sin(b), ny))


def _dft_table_kernel(u_ref, v_ref, cb_ref, sb_ref, fwd_ref, inv_ref):
    sin_half = pl.program_id(0) == 1
    first = pl.program_id(1) == 0
    n_fine, seq = cb_ref.shape
    row = lax.broadcasted_iota(I32, (n_fine, seq), 0)
    col = lax.broadcasted_iota(I32, (n_fine, seq), 1)
    for c in range(u_ref.shape[1]):
        tile = u_ref[0, c:c + 1, :] * cb_ref[...] + v_ref[0, c:c + 1, :] * sb_ref[...]
        rows = slice(c * n_fine, (c + 1) * n_fine)
        fwd = tile
        if c == 0:
            fwd_nyq = jnp.logical_and(jnp.logical_and(sin_half, first), row == 0)
            fwd = jnp.where(fwd_nyq, (1 - 2 * (col & 1)).astype(F32), tile)
        fwd_ref[0, rows, :] = fwd.astype(BF16)
        inv_nyq = jnp.logical_and(sin_half, col == 0)
        inv_ref[rows, :] = jnp.where(inv_nyq, (1 - 2 * (row & 1)).astype(F32), tile).astype(BF16)


def _dft_tables(seq):
    ca, sa, cb, sb, _ = _dft_factors(seq)
    u = np.stack([ca, sa])
    v = np.stack([-sa, ca])
    n_coarse = 8
    rows = 64 * n_coarse
    coarse_spec = pl.BlockSpec((1, n_coarse, seq), lambda h, i: (h, i, 0))
    fine_spec = pl.BlockSpec((64, seq), lambda h, i: (0, 0))
    return pl.pallas_call(
        _dft_table_kernel,
        grid=(2, seq // rows),
        in_specs=[coarse_spec, coarse_spec, fine_spec, fine_spec],
        out_specs=[pl.BlockSpec((1, rows, seq), lambda h, i: (h, i, 0)),
                   pl.BlockSpec((rows, seq), lambda h, i: (i, h))],
        out_shape=[jax.ShapeDtypeStruct((2, seq, seq), BF16), jax.ShapeDtypeStruct((seq, 2 * seq), BF16)],
        compiler_params=_params("parallel", "arbitrary"),
        name="dft_tables",
    )(jnp.asarray(u), jnp.asarray(v), jnp.asarray(cb), jnp.asarray(sb))


@functools.lru_cache(maxsize=None)
def _hyena_consts(seq):
    t = np.arange(seq, dtype=np.float64)
    t01 = t / (seq - 1)
    bands = np.linspace(1e-4, HY_BANDS - 1, HY_BANDS)
    ang = (2.0 * math.pi) * (t[:, None] / seq) * bands[None, :]
    feats = np.concatenate([t01[:, None], np.cos(ang), -np.sin(ang)], axis=-1)
    feats_p = np.zeros((seq, LANES), np.float32)
    feats_p[:, :HY_POS_DIM] = feats
    min_decay = math.log(1e-2) / 1.5
    max_decay = math.log(1e-2) / 0.3
    deltas = np.abs(np.linspace(min_decay, max_decay, HY_W))
    window = np.exp(-t01[:, None] * deltas[None, :]).astype(np.float32)
    return feats_p, window


def _filter_kernel(feats_ref, w1_ref, b1_ref, w2_ref, b2_ref, freq_ref, w3f_ref, w3b_ref, win_ref, sum_ref,
                   diff_ref, hid_ref):
    hp = lax.Precision.HIGHEST

    @pl.when(pl.program_id(0) == 0)
    def _():
        f = freq_ref[...]
        h1 = jnp.sin(f * (jnp.dot(feats_ref[...], w1_ref[...], precision=hp, preferred_element_type=F32)
                          + b1_ref[...]))
        hid_ref[...] = jnp.sin(f * (jnp.dot(h1, w2_ref[...], precision=hp, preferred_element_type=F32)
                                    + b2_ref[...]))

    hid = hid_ref[...]
    win = win_ref[...]
    fwd = jnp.dot(hid, w3f_ref[...], precision=hp, preferred_element_type=F32) * win
    bwd = jnp.dot(hid, w3b_ref[...], precision=hp, preferred_element_type=F32) * win
    bwd = jnp.where(lax.broadcasted_iota(I32, bwd.shape, 0) == 0, 0.0, bwd)
    sum_ref[...] = (fwd + bwd).astype(BF16)
    diff_ref[...] = (fwd - bwd).astype(BF16)


def _hyena_filters_time(w1, b1, w2, b2, w3, freq, seq):
    feats, window = _hyena_consts(seq)
    w1p = jnp.zeros((LANES, HY_FILT_FF), F32).at[:HY_POS_DIM].set(w1)
    full = lambda shape: pl.BlockSpec(shape, lambda o: (0,) * len(shape))
    out_spec = pl.BlockSpec((seq, HY_W), lambda o: (0, o))
    out_shape = jax.ShapeDtypeStruct((seq, HY_ORDER * HY_W), BF16)
    return pl.pallas_call(
        _filter_kernel,
        grid=(HY_ORDER,),
        in_specs=[
            full((seq, LANES)), full((LANES, HY_FILT_FF)), full((1, HY_FILT_FF)),
            full((HY_FILT_FF, HY_FILT_FF)), full((1, HY_FILT_FF)), full((1, HY_FILT_FF)),
            pl.BlockSpec((HY_FILT_FF, HY_W), lambda o: (0, 2 * o)),
            pl.BlockSpec((HY_FILT_FF, HY_W), lambda o: (0, 2 * o + 1)),
            full((seq, HY_W)),
        ],
        out_specs=[out_spec, out_spec],
        out_shape=[out_shape, out_shape],
        scratch_shapes=[pltpu.VMEM((seq, HY_FILT_FF), F32)],
        compiler_params=_params("arbitrary"),
        name="hyena_filter_mlp",
    )(jnp.asarray(feats), w1p, b1.reshape(1, -1), w2, b2.reshape(1, -1), freq.reshape(1, -1), w3, w3,
      jnp.asarray(window))


def _filter_dft_kernel(f_ref, sum_ref, diff_ref, o_ref):
    o_ref[0] = jnp.dot(f_ref[0], sum_ref[...], preferred_element_type=F32)
    o_ref[1] = jnp.dot(f_ref[1], diff_ref[...], preferred_element_type=F32)

    @pl.when(pl.program_id(0) == 0)
    def _():
        top = jnp.dot(f_ref[1, 0:8, :], sum_ref[...], preferred_element_type=F32)
        row = lax.broadcasted_iota(I32, top.shape, 0)
        o_ref[1, 0:8, :] = jnp.where(row == 0, top, o_ref[1, 0:8, :])


def _filter_dft(dft_fwd, filt_sum, filt_diff, seq, fb):
    return pl.pallas_call(
        _filter_dft_kernel,
        grid=(seq // fb, HY_ORDER),
        in_specs=[
            pl.BlockSpec((2, fb, seq), lambda f, o: (0, f, 0)),
            pl.BlockSpec((seq, HY_W), lambda f, o: (0, o)),
            pl.BlockSpec((seq, HY_W), lambda f, o: (0, o)),
        ],
        out_specs=pl.BlockSpec((2, fb, HY_W), lambda f, o: (0, f, o)),
        out_shape=jax.ShapeDtypeStruct((2, seq, HY_ORDER * HY_W), F32),
        compiler_params=_params("parallel", "arbitrary"),
        name="hyena_filter_dft",
    )(dft_fwd, filt_sum, filt_diff)


def _short_conv_kernel(p_ref, w_ref, b_ref, o_ref):
    p = p_ref[0].astype(F32)
    seq = p.shape[0]
    row = lax.broadcasted_iota(I32, p.shape, 0)
    prev = jnp.where(row == 0, 0.0, pltpu.roll(p, 1, 0))
    nxt = jnp.where(row == seq - 1, 0.0, pltpu.roll(p, seq - 1, 0))
    w = w_ref[...]
    o_ref[0] = prev * w[0:1] + p * w[1:2] + nxt * w[2:3] + b_ref[...]


def _short_conv(proj, col0, conv_w, conv_b, batch, seq, tc):
    n_cols = conv_w.shape[-1]
    assert col0 % tc == 0 and n_cols % tc == 0
    cb0 = col0 // tc
    return pl.pallas_call(
        _short_conv_kernel,
        grid=(batch, n_cols // tc),
        in_specs=[
            pl.BlockSpec((1, seq, tc), lambda b, c: (b, 0, cb0 + c)),
            pl.BlockSpec((3, tc), lambda b, c: (0, c)),
            pl.BlockSpec((1, tc), lambda b, c: (0, c)),
        ],
        out_specs=pl.BlockSpec((1, seq, tc), lambda b, c: (b, 0, c)),
        out_shape=jax.ShapeDtypeStruct((batch, seq, n_cols), F32),
        compiler_params=_params("parallel", "arbitrary"),
        name="hyena_short_conv",
    )(proj, conv_w, conv_b.reshape(1, -1))


def _spectrum_kernel(n_fft, f_ref, z_ref, k_ref, o_ref):
    z = z_ref[0].astype(BF16)
    xr = jnp.dot(f_ref[0], z, preferred_element_type=F32)
    xs = jnp.dot(f_ref[1], z, preferred_element_type=F32)
    kr = k_ref[0]
    ks = k_ref[1]
    row = lax.broadcasted_iota(I32, xr.shape, 0)
    edge = jnp.logical_and(row == 0, pl.program_id(0) == 0)
    yr = jnp.where(edge, xr * kr * (1.0 / n_fft), (xr * kr - xs * ks) * (2.0 / n_fft))
    ys = jnp.where(edge, xs * ks * (1.0 / n_fft), (xr * ks + xs * kr) * (2.0 / n_fft))
    o_ref[0, 0] = yr.astype(BF16)
    o_ref[0, 1] = ys.astype(BF16)


def _spectrum_product(dft_fwd, z_arr, z_col, kfreq, order, batch, seq, fb):
    return pl.pallas_call(
        functools.partial(_spectrum_kernel, 2 * seq),
        grid=(seq // fb, batch),
        in_specs=[
            pl.BlockSpec((2, fb, seq), lambda f, b: (0, f, 0)),
            pl.BlockSpec((1, seq, HY_W), lambda f, b: (b, 0, z_col)),
            pl.BlockSpec((2, fb, HY_W), lambda f, b: (0, f, order)),
        ],
        out_specs=pl.BlockSpec((1, 2, fb, HY_W), lambda f, b: (b, 0, f, 0)),
        out_shape=jax.ShapeDtypeStruct((batch, 2, seq, HY_W), BF16),
        compiler_params=_params("parallel", "arbitrary"),
        name="hyena_spectrum",
    )(dft_fwd, z_arr, kfreq)


def _inverse_kernel(ft_ref, y_ref, gate_ref, z_ref, d_ref, o_ref):
    conv = jnp.dot(ft_ref[...], y_ref[0], preferred_element_type=F32)
    o_ref[0] = gate_ref[0] * (conv + d_ref[...] * z_ref[0])


def _inverse_gate(dft_inv, y, gate_arr, gate_col, z_arr, z_col, skip_row, batch, seq, tb):
    y2 = y.reshape(batch, 2 * seq, HY_W)
    return pl.pallas_call(
        _inverse_kernel,
        grid=(seq // tb, batch),
        in_specs=[
            pl.BlockSpec((tb, 2 * seq), lambda t, b: (t, 0)),
            pl.BlockSpec((1, 2 * seq, HY_W), lambda t, b: (b, 0, 0)),
            pl.BlockSpec((1, tb, HY_W), lambda t, b: (b, t, gate_col)),
            pl.BlockSpec((1, tb, HY_W), lambda t, b: (b, t, z_col)),
            pl.BlockSpec((1, HY_W), lambda t, b: (0, 0)),
        ],
        out_specs=pl.BlockSpec((1, tb, HY_W), lambda t, b: (b, t, 0)),
        out_shape=jax.ShapeDtypeStruct((batch, seq, HY_W), F32),
        compiler_params=_params("parallel", "arbitrary"),
        name="hyena_inverse_gate",
    )(dft_inv, y2, gate_arr, z_arr, skip_row.reshape(1, HY_W))


def _hyena_mixer(proj, col0, conv_w, conv_b, w1, b1, w2, b2, w3, freq, skip_d, dft_fwd, dft_inv, batch, seq):
    s = _short_conv(proj, col0, conv_w, conv_b, batch, seq, 768)
    filt_sum, filt_diff = _hyena_filters_time(w1, b1, w2, b2, w3, freq, seq)
    kfreq = _filter_dft(dft_fwd, filt_sum, filt_diff, seq, 512)
    z_arr, z_col = s, 2
    for o in range(HY_ORDER):
        y = _spectrum_product(dft_fwd, z_arr, z_col, kfreq, o, batch, seq, 1024)
        z_arr = _inverse_gate(dft_inv, y, s, o, z_arr, z_col, skip_d[o], batch, seq, 1024)
        z_col = 0
    return z_arr


@functools.lru_cache(maxsize=None)
def _retention_consts(seq):
    c = RET_CHUNK
    half = RET_HEAD_DIM // 2
    inv = 1.0 / (10000.0 ** np.linspace(0.0, 1.0, half))
    ang = np.arange(seq, dtype=np.float64)[:, None] * inv[None, :]
    cos2 = np.concatenate([np.cos(ang), np.cos(ang)], axis=-1).astype(np.float32)
    sin2 = np.concatenate([-np.sin(ang), np.sin(ang)], axis=-1).astype(np.float32)
    hidx = np.arange(RET_HEADS, dtype=np.float64)
    lg_f = np.log1p(-np.exp2(-5.0 - hidx))[:, None, None]
    lg_b = np.log1p(-np.exp2(-5.5 - hidx))[:, None, None]
    i = np.arange(c, dtype=np.float64)
    diff = i[:, None] - i[None, :]
    ones = np.ones((1, c, c))
    dec = np.where(diff >= 0, np.exp(lg_f * np.maximum(diff, 0.0)), np.exp(lg_b * np.maximum(-diff, 0.0)))
    rowv = lambda v: v[:, :, None] * ones
    tab = np.stack([
        dec,
        rowv(np.exp(lg_f[:, :, 0] * (i + 1.0)[None, :])),
        rowv(np.exp(lg_f[:, :, 0] * (c - 1.0 - i)[None, :])),
        rowv(np.exp(lg_b[:, :, 0] * (c - i)[None, :])),
        rowv(np.exp(lg_b[:, :, 0] * i[None, :])),
        np.exp(lg_f * c) * ones,
        np.exp(lg_b * c) * ones,
    ], axis=1).astype(np.float32)
    return cos2, sin2, tab


def _retention_kernel(q_ref, k_ref, v_ref, g_ref, cos_ref, sin_ref, tab_ref, o_ref, qs_ref, ks_ref, kvf_ref,
                      kvb_ref, a0_ref, a1_ref, y0_ref, y1_ref):
    c = RET_CHUNK
    d = RET_HEAD_DIM
    seq = q_ref.shape[1]
    n_chunks = seq // c
    cos = cos_ref[...]
    sin = sin_ref[...]
    q = q_ref[0].astype(F32)
    k = k_ref[0].astype(F32)
    qs_ref[...] = (q * cos + pltpu.roll(q, d // 2, 1) * sin) * (d ** -0.5)
    ks_ref[...] = k * cos + pltpu.roll(k, d // 2, 1) * sin

    def mm(a, b):
        return jnp.dot(a.astype(BF16), b.astype(BF16), preferred_element_type=F32)

    def chunk(n):
        return pl.ds(pl.multiple_of(n * c, c), c)

    def kv_body(n, carry):
        kc = ks_ref[chunk(n), :]
        vc = v_ref[0, chunk(n), :]
        kvf_ref[n] = mm((kc * tab_ref[0, 2]).T, vc)
        kvb_ref[n] = mm((kc * tab_ref[0, 4]).T, vc)
        return carry

    lax.fori_loop(0, n_chunks, kv_body, 0, unroll=True)

    def scan_fwd(n, state):
        kv = kvf_ref[n]
        kvf_ref[n] = state
        return tab_ref[0, 5] * state + kv

    lax.fori_loop(0, n_chunks, scan_fwd, jnp.zeros((d, d), F32))

    def scan_bwd(m, state):
        n = n_chunks - 1 - m
        kv = kvb_ref[n]
        kvb_ref[n] = state
        return tab_ref[0, 6] * state + kv

    lax.fori_loop(0, n_chunks, scan_bwd, jnp.zeros((d, d), F32))

    def in_chunk(n, a_out):
        qc = qs_ref[chunk(n), :].astype(BF16)
        kc = ks_ref[chunk(n), :].astype(BF16)
        a = lax.dot_general(qc, kc, NT_DIMS, preferred_element_type=F32) * tab_ref[0, 0]
        a_out[...] = a.astype(BF16)

    def mix(n, a_in, y_out):
        qc = qs_ref[chunk(n), :]
        y_out[...] = (jnp.dot(a_in[...], v_ref[0, chunk(n), :], preferred_element_type=F32)
                      + mm(qc * tab_ref[0, 1], kvf_ref[n]) + mm(qc * tab_ref[0, 3], kvb_ref[n]))

    def finish(n, y_in):
        y = y_in[...]
        mu = jnp.mean(y, axis=-1, keepdims=True)
        yc = y - mu
        var = jnp.mean(yc * yc, axis=-1, keepdims=True)
        g = g_ref[0, chunk(n), :].astype(F32)
        o_ref[0, chunk(n), :] = (yc * lax.rsqrt(var + GN_EPS) * (g * jax.nn.sigmoid(g))).astype(BF16)

    a_slots = (a0_ref, a1_ref)
    y_slots = (y0_ref, y1_ref)

    def step(t, parity, do_a=True, do_mix=True, do_finish=True):
        t = jnp.asarray(t, I32)
        if do_a:
            in_chunk(t, a_slots[parity])
        if do_mix:
            mix(t - 1, a_slots[1 - parity], y_slots[1 - parity])
        if do_finish:
            finish(t - 2, y_slots[parity])

    assert n_chunks % 2 == 0 and n_chunks >= 4
    step(0, 0, do_mix=False, do_finish=False)
    step(1, 1, do_finish=False)

    def steady(i, carry):
        step(2 + 2 * i, 0)
        step(3 + 2 * i, 1)
        return carry

    lax.fori_loop(0, (n_chunks - 2) // 2, steady, 0, unroll=True)
    step(n_chunks, 0, do_a=False)
    step(n_chunks + 1, 1, do_a=False, do_mix=False)


def _retention_mixer(proj, col0, batch, seq):
    cos2, sin2, tab = _retention_consts(seq)
    blk = (1, seq, RET_HEAD_DIM)
    h_ = RET_HEADS
    assert col0 % RET_HEAD_DIM == 0
    c0 = col0 // RET_HEAD_DIM
    return pl.pallas_call(
        _retention_kernel,
        grid=(batch, RET_HEADS),
        in_specs=[
            pl.BlockSpec(blk, lambda b, h: (b, 0, c0 + h)),
            pl.BlockSpec(blk, lambda b, h: (b, 0, c0 + h_ + h)),
            pl.BlockSpec(blk, lambda b, h: (b, 0, c0 + 2 * h_ + h)),
            pl.BlockSpec(blk, lambda b, h: (b, 0, c0 + 3 * h_ + h)),
            pl.BlockSpec((seq, RET_HEAD_DIM), lambda b, h: (0, 0)),
            pl.BlockSpec((seq, RET_HEAD_DIM), lambda b, h: (0, 0)),
            pl.BlockSpec((1, 7, RET_CHUNK, RET_CHUNK), lambda b, h: (h, 0, 0, 0)),
        ],
        out_specs=pl.BlockSpec(blk, lambda b, h: (b, 0, h)),
        out_shape=jax.ShapeDtypeStruct((batch, seq, RET_W), BF16),
        scratch_shapes=[pltpu.VMEM((seq, RET_HEAD_DIM), F32), pltpu.VMEM((seq, RET_HEAD_DIM), F32),
                        pltpu.VMEM((seq // RET_CHUNK, RET_HEAD_DIM, RET_HEAD_DIM), F32),
                        pltpu.VMEM((seq // RET_CHUNK, RET_HEAD_DIM, RET_HEAD_DIM), F32),
                        pltpu.VMEM((RET_CHUNK, RET_CHUNK), BF16), pltpu.VMEM((RET_CHUNK, RET_CHUNK), BF16),
                        pltpu.VMEM((RET_CHUNK, RET_HEAD_DIM), F32), pltpu.VMEM((RET_CHUNK, RET_HEAD_DIM), F32)],
        compiler_params=_params("parallel", "arbitrary"),
        name="retention",
    )(proj, proj, proj, proj, jnp.asarray(cos2), jnp.asarray(sin2), jnp.asarray(tab))


def _router_kernel(x_ref, g_ref, wr_ref, hm_ref, aff_ref):
    n_e = aff_ref.shape[1]
    h = _rms(x_ref[...], g_ref[...])
    h_hi = h.astype(BF16)
    hm_ref[...] = h_hi
    h_lo = (h - h_hi.astype(F32)).astype(BF16)
    w = wr_ref[...]
    w_hi = w.astype(BF16)
    w_lo = (w - w_hi.astype(F32)).astype(BF16)
    logits = (jnp.dot(h_hi, w_hi, preferred_element_type=F32) + jnp.dot(h_lo, w_hi, preferred_element_type=F32)
              + jnp.dot(h_hi, w_lo, preferred_element_type=F32)).T[:n_e]
    m = jnp.max(logits, axis=0, keepdims=True)
    e = jnp.exp(logits - m)
    aff_ref[0] = e / jnp.sum(e, axis=0, keepdims=True)


def _router(x2d, gain, w_router, batch, seq, tm):
    m = x2d.shape[0]
    per_b = seq // tm
    w_pad = jnp.zeros((D_MODEL, LANES), F32).at[:, :N_EXPERTS].set(w_router)
    return pl.pallas_call(
        _router_kernel,
        grid=(m // tm,),
        in_specs=[
            pl.BlockSpec((tm, D_MODEL), lambda i: (i, 0)),
            pl.BlockSpec((1, D_MODEL), lambda i: (0, 0)),
            pl.BlockSpec((D_MODEL, LANES), lambda i: (0, 0)),
        ],
        out_specs=[
            pl.BlockSpec((tm, D_MODEL), lambda i: (i, 0)),
            pl.BlockSpec((1, N_EXPERTS, tm), lambda i: (i // per_b, 0, i % per_b)),
        ],
        out_shape=[
            jax.ShapeDtypeStruct((m, D_MODEL), BF16),
            jax.ShapeDtypeStruct((batch, N_EXPERTS, seq), F32),
        ],
        compiler_params=_params("parallel"),
        name="moe_router",
    )(x2d, gain.reshape(1, D_MODEL), w_pad)


def _sort_descending(x):
    rows, n = x.shape
    lanes = LANES
    n_chunks = n // lanes
    chunks = [x[:, c * lanes:(c + 1) * lanes] for c in range(n_chunks)]
    lane = lax.broadcasted_iota(I32, (rows, lanes), 1)
    k = 2
    while k <= n:
        j = k // 2
        while j >= 1:
            nxt = []
            for c, xc in enumerate(chunks):
                if j >= lanes:
                    lo = (c & (j // lanes)) == 0
                    partner = chunks[c ^ (j // lanes)]
                else:
                    lo = (lane & j) == 0
                    partner = jnp.where(lo, pltpu.roll(xc, lanes - j, 1), pltpu.roll(xc, j, 1))
                desc = ((c * lanes) & k) == 0 if k >= lanes else (lane & k) == 0
                hi, lw = jnp.maximum(xc, partner), jnp.minimum(xc, partner)
                if j >= lanes and k >= lanes:
                    nxt.append(hi if lo == desc else lw)
                else:
                    nxt.append(jnp.where(lo == desc, hi, lw))
            chunks = nxt
            j //= 2
        k *= 2
    return jnp.concatenate(chunks, axis=1)


def _topk_kernel(cap, n_e, aff_ref, slot_ref, wsel_ref, slot_t_ref, thr_ref):
    n_rows, n_tok = aff_ref.shape

    def thr_body(i, carry):
        rows8 = pl.ds(pl.multiple_of(i * 8, 8), 8)
        thr_ref[rows8, :] = _sort_descending(aff_ref[rows8, :])[:, cap - 1:cap]
        return carry

    lax.fori_loop(0, n_rows // 8, thr_body, 0)
    a = aff_ref[...]
    thr = thr_ref[...]
    gt = a > thr
    eq = a == thr
    need = cap - jnp.sum(gt.astype(I32), axis=1, keepdims=True)
    upper = (lax.broadcasted_iota(I32, (n_tok, n_tok), 0) < lax.broadcasted_iota(I32, (n_tok, n_tok), 1))
    upper = upper.astype(BF16)
    eq_rank = jnp.dot(eq.astype(BF16), upper, preferred_element_type=F32)
    sel = jnp.logical_or(gt, jnp.logical_and(eq, eq_rank < need.astype(F32)))
    rank = jnp.dot(sel.astype(BF16), upper, preferred_element_type=F32)
    slot = jnp.where(sel, rank, -1.0)
    slot_ref[...] = slot.astype(I32)
    wsel_ref[...] = jnp.where(sel, a, 0.0)
    pad = jnp.full((LANES - n_e, n_tok), -1.0, F32)
    for b in range(n_rows // n_e):
        slot_t_ref[b] = jnp.concatenate([slot[b * n_e:(b + 1) * n_e], pad], axis=0).T.astype(I32)


def _topk_select(aff2d, cap, n_e):
    rows, n_tok = aff2d.shape
    spec = pl.BlockSpec((rows, n_tok), lambda i: (0, 0))
    spec_t = pl.BlockSpec((rows // n_e, n_tok, LANES), lambda i: (0, 0, 0))
    return pl.pallas_call(
        functools.partial(_topk_kernel, cap, n_e),
        grid=(1,),
        in_specs=[spec],
        out_specs=[spec, spec, spec_t],
        out_shape=[jax.ShapeDtypeStruct((rows, n_tok), I32), jax.ShapeDtypeStruct((rows, n_tok), F32),
                   jax.ShapeDtypeStruct((rows // n_e, n_tok, LANES), I32)],
        scratch_shapes=[pltpu.VMEM((rows, 1), F32)],
        compiler_params=_params("arbitrary"),
        name="moe_topk_select",
    )(aff2d)


def _gather_kernel(cap, slot_ref, wsel_ref, hm_ref, xe_ref, gs_ref):
    n_tok = hm_ref.shape[0]
    onehot = slot_ref[0] == lax.broadcasted_iota(I32, (cap, n_tok), 0)
    xe_ref[0] = jnp.dot(onehot.astype(BF16), hm_ref[...], preferred_element_type=F32).astype(BF16)
    gs_ref[0] = jnp.sum(jnp.where(onehot, wsel_ref[0], 0.0), axis=1, keepdims=True)


def _gather_tokens(slot, wsel, hm, batch, seq, cap):
    rows = batch * N_EXPERTS
    row_spec = pl.BlockSpec((1, 1, seq), lambda b, e: (b * N_EXPERTS + e, 0, 0))
    return pl.pallas_call(
        functools.partial(_gather_kernel, cap),
        grid=(batch, N_EXPERTS),
        in_specs=[row_spec, row_spec, pl.BlockSpec((seq, D_MODEL), lambda b, e: (b, 0))],
        out_specs=[
            pl.BlockSpec((1, cap, D_MODEL), lambda b, e: (e, b, 0)),
            pl.BlockSpec((1, cap, 1), lambda b, e: (e, b, 0)),
        ],
        out_shape=[
            jax.ShapeDtypeStruct((N_EXPERTS, batch * cap, D_MODEL), BF16),
            jax.ShapeDtypeStruct((N_EXPERTS, batch * cap, 1), F32),
        ],
        compiler_params=_params("parallel", "arbitrary"),
        name="moe_gather",
    )(slot.reshape(rows, 1, seq), wsel.reshape(rows, 1, seq), hm)


def _expert_kernel(n_e, n_t, xe_ref, wg_ref, wu_ref, wd_ref, gs_ref, ye_ref, mid_ref):
    g = pl.program_id(0)
    s = pl.program_id(1)
    t = s // 2
    tf = wg_ref.shape[-1]

    @pl.when(jnp.logical_and(s % 2 == 0, g < n_e))
    def _():
        x = xe_ref[0]
        a = jnp.dot(x, wg_ref[...].astype(BF16), preferred_element_type=F32)
        u = jnp.dot(x, wu_ref[...].astype(BF16), preferred_element_type=F32)
        mid_ref[g % 2, t] = (a * jax.nn.sigmoid(a) * u).astype(BF16)

    @pl.when(jnp.logical_and(s % 2 == 1, g >= 1))
    def _():
        prev = (g - 1) % 2
        acc = jnp.dot(mid_ref[prev, 0], wd_ref[0:tf, :].astype(BF16), preferred_element_type=F32)
        for f in range(1, n_t):
            acc += jnp.dot(mid_ref[prev, f], wd_ref[f * tf:(f + 1) * tf, :].astype(BF16),
                           preferred_element_type=F32)
        ye_ref[0] = (acc * gs_ref[0]).astype(BF16)


def _expert_ffn(xe, gs, w_gate, w_up, w_down, layer, tf):
    n_e, rows, d = xe.shape
    ff = w_gate.shape[-1]
    n_t = ff // tf
    assert d // tf == n_t
    last = n_e - 1

    n_s = 2 * n_t

    def fill_tile(g, s, lead):
        v = jnp.minimum(g * n_s + s + lead, n_e * n_s - 1)
        return (layer, v // n_s, 0, (v % n_s) // 2)

    def emit_tile(g, s):
        v = jnp.maximum(g * n_s + s - 1 - n_s, 0)
        return (layer, v // n_s, 0, (v % n_s) // 2)

    return pl.pallas_call(
        functools.partial(_expert_kernel, n_e, n_t),
        grid=(n_e + 1, n_s),
        in_specs=[
            pl.BlockSpec((1, rows, d), lambda g, s: (jnp.minimum(g, last), 0, 0)),
            pl.BlockSpec((None, None, d, tf), functools.partial(fill_tile, lead=1)),
            pl.BlockSpec((None, None, d, tf), functools.partial(fill_tile, lead=0)),
            pl.BlockSpec((None, None, ff, tf), emit_tile),
            pl.BlockSpec((1, rows, 1), lambda g, s: (jnp.maximum(g - 1, 0), 0, 0)),
        ],
        out_specs=pl.BlockSpec((1, rows, tf), lambda g, s: emit_tile(g, s)[1:]),
        out_shape=jax.ShapeDtypeStruct((n_e, rows, d), BF16),
        scratch_shapes=[pltpu.VMEM((2, n_t, rows, tf), BF16)],
        compiler_params=_params("arbitrary", "arbitrary"),
        name="moe_expert_ffn",
    )(xe, w_gate, w_up, w_down, gs)


def _scatter_kernel(cap, slot_ref, ye_ref, x_ref, o_ref, onehot_ref):
    n_e = ye_ref.shape[0]
    n_tok = x_ref.shape[0]

    @pl.when(pl.program_id(1) == 0)
    def _():
        col = lax.broadcasted_iota(I32, (n_tok, cap), 1)
        slots = slot_ref[0]
        for e in range(n_e):
            onehot_ref[:, e * cap:(e + 1) * cap] = (slots[:, e:e + 1] == col).astype(BF16)

    ye = ye_ref[...].reshape(n_e * cap, ye_ref.shape[-1])
    o_ref[...] = x_ref[...] + jnp.dot(onehot_ref[...], ye, preferred_element_type=F32)


def _scatter_add(slot_t, ye, x2d, batch, seq, cap, tn):
    return pl.pallas_call(
        functools.partial(_scatter_kernel, cap),
        grid=(batch, D_MODEL // tn),
        in_specs=[
            pl.BlockSpec((1, seq, LANES), lambda b, n: (b, 0, 0)),
            pl.BlockSpec((N_EXPERTS, cap, tn), lambda b, n: (0, b, n)),
            pl.BlockSpec((seq, tn), lambda b, n: (b, n)),
        ],
        out_specs=pl.BlockSpec((seq, tn), lambda b, n: (b, n)),
        out_shape=jax.ShapeDtypeStruct(x2d.shape, F32),
        scratch_shapes=[pltpu.VMEM((seq, N_EXPERTS * cap), BF16)],
        compiler_params=_params("parallel", "arbitrary"),
        name="moe_scatter_add",
    )(slot_t, ye, x2d)


def _expert_choice_ffn(x2d, gain, w_router, w_gate, w_up, w_down, layer, batch, seq):
    cap = EC_CAPACITY * seq // N_EXPERTS
    hm, aff = _router(x2d, gain, w_router, batch, seq, 512)
    slot, wsel, slot_t = _topk_select(aff.reshape(batch * N_EXPERTS, seq), cap, N_EXPERTS)
    xe, gs = _gather_tokens(slot, wsel, hm, batch, seq, cap)
    ye = _expert_ffn(xe, gs, w_gate, w_up, w_down, layer, 512)
    return _scatter_add(slot_t, ye, x2d, batch, seq, cap, 512)


def _final_norm_kernel(x_ref, g_ref, o_ref):
    o_ref[...] = _rms(x_ref[...], g_ref[...])


def _final_norm(x2d, gain, tm):
    m, d = x2d.shape
    return pl.pallas_call(
        _final_norm_kernel,
        grid=(m // tm,),
        in_specs=[pl.BlockSpec((tm, d), lambda i: (i, 0)), pl.BlockSpec((1, d), lambda i: (0, 0))],
        out_specs=pl.BlockSpec((tm, d), lambda i: (i, 0)),
        out_shape=jax.ShapeDtypeStruct((m, d), F32),
        compiler_params=_params("parallel"),
        name="final_norm",
    )(x2d, gain.reshape(1, d))


def kernel(x, mem, norm_mix, w_in, na_rpb, hy_conv_w, hy_conv_b, hy_filt_w1, hy_filt_b1, hy_filt_w2, hy_filt_b2, hy_filt_w3, hy_sin_freq, hy_skip_d, branch_norm, w_out, norm_cross, mem_norm, w_cq, w_ckv, w_co, norm_moe, w_router, w_gate, w_up, w_down, final_norm):
    batch, seq, d = x.shape
    n_mem = mem.shape[1]
    depth = w_in.shape[0]
    m = batch * seq
    na_cols = 3 * NA_W
    hy_cols = 3 * HY_W
    rows = seq // GRID_W

    dft_fwd, dft_inv = _dft_tables(seq)
    x2d = x.reshape(m, d)
    mem2d = mem.reshape(batch * n_mem, d)

    w_out_b, w_cq_b, w_co_b = (w.astype(BF16) for w in (w_out, w_cq, w_co))
    p_in = w_in.shape[-1]

    for l in range(depth):
        proj = _norm_matmul(x2d, norm_mix[l], _whole(w_in, l), 1024, 768, BF16, "in_proj")
        proj = proj.reshape(batch, seq, p_in)

        y_na = _neighbourhood_attention(proj, _na_bias_table(na_rpb[l], rows), batch, seq)
        y_hy = _hyena_mixer(proj, na_cols, hy_conv_w[l], hy_conv_b[l], hy_filt_w1[l],
                            hy_filt_b1[l], hy_filt_w2[l], hy_filt_b2[l], hy_filt_w3[l], hy_sin_freq[l],
                            hy_skip_d[l], dft_fwd, dft_inv, batch, seq)
        y_ret = _retention_mixer(proj, na_cols + hy_cols, batch, seq)
        x2d = _branch_out_proj(y_na.reshape(m, NA_W), y_hy.reshape(m, HY_W), y_ret.reshape(m, RET_W),
                               branch_norm[l], _whole(w_out_b, l), x2d, 1024, 1024)

        q = _norm_matmul(x2d, norm_cross[l], _whole(w_cq_b, l), 1024, 1024, BF16, "cross_q_proj")
        kv = _norm_matmul(mem2d, mem_norm, _whole(w_ckv, l), batch * n_mem, 1024, BF16, "cross_kv_proj")
        x2d = _cross_attend_out_proj(q, kv, _whole(w_co_b, l), x2d, seq, n_mem, 1024, 1024)

        x2d = _expert_choice_ffn(x2d, norm_moe[l], w_router[l], w_gate, w_up, w_down, l, batch, seq)

    return _final_norm(x2d, final_norm, 512).reshape(batch, seq, d)
```

```python
import functools
import math

import numpy as np
import jax
import jax.numpy as jnp
from jax import lax
from jax.experimental import pallas as pl
from jax.experimental.pallas import tpu as pltpu

F32 = jnp.float32
BF16 = jnp.bfloat16
I32 = jnp.int32

D_MODEL = 2048
GRID_W = 64
NA_HEAD_DIM = 64
NA_W = 768
NA_HEADS = 12
NA_KR = 8
NA_KC = 16
HY_W = 512
HY_ORDER = 2
HY_BANDS = 8
HY_POS_DIM = 17
HY_FILT_FF = 64
RET_HEAD_DIM = 128
RET_W = 768
RET_HEADS = 6
RET_CHUNK = 128
CROSS_HEADS = 4
CROSS_HEAD_DIM = 512
N_EXPERTS = 16
EXPERT_FF = 2048
EC_CAPACITY = 2
RMS_EPS = 1e-6
GN_EPS = 1e-5

MASK_VALUE = -1e30
LANES = 128
VMEM_LIMIT_BYTES = 56 * 1024 * 1024

NT_DIMS = (((1,), (1,)), ((), ()))


def _params(*sem):
    return pltpu.CompilerParams(dimension_semantics=sem, vmem_limit_bytes=VMEM_LIMIT_BYTES)


def _rms(xf, g):
    return xf * lax.rsqrt(jnp.mean(xf * xf, axis=-1, keepdims=True) + RMS_EPS) * g


def _mm_kernel(prologue, n_pro, has_res, *refs):
    pro_refs = refs[:n_pro]
    w_ref = refs[n_pro]
    res_ref = refs[n_pro + 1] if has_res else None
    o_ref = refs[n_pro + 1 + has_res]
    h_ref = refs[n_pro + 2 + has_res]

    @pl.when(pl.program_id(1) == 0)
    def _():
        prologue(h_ref, *pro_refs)

    w = w_ref[...]
    if w.dtype != BF16:
        w = w.astype(BF16)
    acc = jnp.dot(h_ref[...], w, preferred_element_type=F32)
    if has_res:
        acc = acc + res_ref[...]
    o_ref[...] = acc.astype(o_ref.dtype)


def _fused_matmul(prologue, pro_args, pro_specs, wsel, res, m, tm, tn, out_dtype, name):
    w, layer, col0, n = wsel
    k = w.shape[1]
    cb0 = col0 // tn
    assert col0 % tn == 0 and n % tn == 0 and m % tm == 0
    has_res = res is not None
    if layer is None:
        per_layer = n // tn
        n = n * w.shape[0]
        w_spec = pl.BlockSpec((None, k, tn), lambda i, j: (j // per_layer, 0, cb0 + j % per_layer))
    else:
        w_spec = pl.BlockSpec((None, k, tn), lambda i, j: (layer, 0, cb0 + j))
    in_specs = list(pro_specs) + [w_spec]
    args = list(pro_args) + [w]
    if has_res:
        in_specs.append(pl.BlockSpec((tm, tn), lambda i, j: (i, j)))
        args.append(res)
    return pl.pallas_call(
        functools.partial(_mm_kernel, prologue, len(pro_args), has_res),
        grid=(m // tm, n // tn),
        in_specs=in_specs,
        out_specs=pl.BlockSpec((tm, tn), lambda i, j: (i, j)),
        out_shape=jax.ShapeDtypeStruct((m, n), out_dtype),
        scratch_shapes=[pltpu.VMEM((tm, k), BF16)],
        compiler_params=_params("parallel", "arbitrary"),
        name=name,
    )(*args)


def _rms_prologue(h_ref, x_ref, g_ref):
    h_ref[...] = _rms(x_ref[...], g_ref[...]).astype(BF16)


def _whole(w, layer):
    return (w, layer, 0, w.shape[-1])


def _norm_matmul(x2d, gain, wsel, tm, tn, out_dtype, name, res=None):
    m, k = x2d.shape
    specs = [pl.BlockSpec((tm, k), lambda i, j: (i, 0)), pl.BlockSpec((1, k), lambda i, j: (0, 0))]
    return _fused_matmul(_rms_prologue, [x2d, gain.reshape(1, k)], specs, wsel, res, m, tm, tn, out_dtype, name)


def _cross_prologue(h_ref, q_ref, k_ref, v_ref):
    dh = CROSS_HEAD_DIM
    for h in range(CROSS_HEADS):
        sl = slice(h * dh, (h + 1) * dh)
        s = lax.dot_general(q_ref[:, sl], k_ref[:, sl], NT_DIMS, preferred_element_type=F32) * (dh ** -0.5)
        m = jnp.max(s, axis=-1, keepdims=True)
        p = jnp.exp(s - m)
        l = jnp.sum(p, axis=-1, keepdims=True)
        o = jnp.dot(p.astype(BF16), v_ref[:, sl], preferred_element_type=F32) * (1.0 / l)
        h_ref[:, sl] = o.astype(BF16)


def _branch_prologue(h_ref, na_ref, hy_ref, ret_ref, g_ref):
    def nrm(p):
        return p * lax.rsqrt(jnp.mean(p * p, axis=-1, keepdims=True) + RMS_EPS)

    y = jnp.concatenate([nrm(r[...].astype(F32)) for r in (na_ref, hy_ref, ret_ref)], axis=-1)
    h_ref[...] = (y * g_ref[...]).astype(BF16)


def _branch_out_proj(y_na, y_hy, y_ret, gain, wsel, res, tm, tn):
    m = y_na.shape[0]
    specs = [
        pl.BlockSpec((tm, NA_W), lambda i, j: (i, 0)),
        pl.BlockSpec((tm, HY_W), lambda i, j: (i, 0)),
        pl.BlockSpec((tm, RET_W), lambda i, j: (i, 0)),
        pl.BlockSpec((1, D_MODEL), lambda i, j: (0, 0)),
    ]
    return _fused_matmul(_branch_prologue, [y_na, y_hy, y_ret, gain.reshape(1, D_MODEL)], specs, wsel, res,
                         m, tm, tn, F32, "branch_out_proj")


NA_PAIR = 2
NA_BAND = NA_KR + NA_PAIR


def _na_band_base(r, rows):
    return np.clip(r - NA_KR // 2, 0, rows - NA_BAND)


@functools.lru_cache(maxsize=None)
def _na_variants(rows):
    assert rows % NA_PAIR == 0 and rows >= NA_BAND + 2
    n_var = NA_KR // 2 + 1
    dr = np.full((n_var, NA_BAND, NA_PAIR), -2, np.int64)
    for r in range(0, rows, NA_PAIR):
        base = int(_na_band_base(r, rows))
        v = (r - base) // 2
        for j in range(NA_PAIR):
            rs = int(np.clip(r + j - NA_KR // 2, 0, rows - NA_KR))
            for i in range(NA_BAND):
                val = base + i - (r + j) + (NA_KR - 1) if rs <= base + i < rs + NA_KR else -1
                assert dr[v, i, j] in (-2, val)
                dr[v, i, j] = val
    assert (dr > -2).all()
    return dr


def _na_bias_table(rpb, rows):
    c = np.arange(GRID_W)
    col_start = np.clip(c - NA_KC // 2, 0, GRID_W - NA_KC)
    col_in = (c[None, :] >= col_start[:, None]) & (c[None, :] < col_start[:, None] + NA_KC)
    dc = np.clip(c[None, :] - c[:, None] + (NA_KC - 1), 0, 2 * NA_KC - 2)
    onehot = (dc.T.reshape(-1)[None, :] == np.arange(2 * NA_KC - 1)[:, None]).astype(np.float32)
    cols = jnp.einsum("hab,bn->han", rpb.astype(F32), jnp.asarray(onehot), precision=lax.Precision.HIGHEST)
    cols = cols.reshape(NA_HEADS, 2 * NA_KR - 1, GRID_W, GRID_W)
    cols = jnp.where(col_in.T[None, None], cols, MASK_VALUE)
    dr = _na_variants(rows)
    n_var = dr.shape[0]
    return pl.pallas_call(
        functools.partial(_na_table_kernel, dr),
        grid=(NA_HEADS,),
        in_specs=[pl.BlockSpec((1, 2 * NA_KR - 1, GRID_W, GRID_W), lambda h: (h, 0, 0, 0))],
        out_specs=pl.BlockSpec((1, n_var, NA_BAND * GRID_W, NA_PAIR * GRID_W), lambda h: (h, 0, 0, 0)),
        out_shape=jax.ShapeDtypeStruct((NA_HEADS, n_var, NA_BAND * GRID_W, NA_PAIR * GRID_W), F32),
        compiler_params=_params("parallel"),
        name="na_bias_table",
    )(cols)


def _na_table_kernel(dr, cols_ref, o_ref):
    masked = jnp.full((GRID_W, GRID_W), MASK_VALUE, F32)
    for v in range(dr.shape[0]):
        for i in range(NA_BAND):
            blocks = [cols_ref[0, int(dr[v, i, j])] if dr[v, i, j] >= 0 else masked for j in range(NA_PAIR)]
            o_ref[0, v, i * GRID_W:(i + 1) * GRID_W, :] = jnp.concatenate(blocks, axis=-1)


def _na_kernel(rows, q_ref, k_ref, v_ref, bias_ref, o_ref, vt_ref, s0_ref, s1_ref, p0_ref, p1_ref, l0_ref,
               l1_ref):
    dh = NA_HEAD_DIM
    nq = NA_PAIR * GRID_W
    nk = NA_BAND * GRID_W
    n_chunk = nk // LANES
    lane = lax.broadcasted_iota(I32, (nq, 2 * dh), 1)

    for ch in range(vt_ref.shape[0]):
        vt_ref[ch] = v_ref[0, ch * LANES:(ch + 1) * LANES, :].astype(F32).T.astype(BF16)

    n_steps = rows // NA_PAIR

    def band_base(p):
        return jnp.clip(p * NA_PAIR - NA_KR // 2, 0, rows - NA_BAND)


    def scores(p, s_out):
        r = p * NA_PAIR
        base = band_base(p)
        variant = (r - base) // 2
        q = q_ref[0, pl.ds(pl.multiple_of(r * GRID_W, nq), nq), :] * (dh ** -0.5)
        kb = k_ref[0, pl.ds(pl.multiple_of(base * GRID_W, LANES), nk), :]
        zero = jnp.zeros_like(q)
        q2 = jnp.concatenate([jnp.where(lane < dh, q, zero), jnp.where(lane >= dh, q, zero)], axis=0)
        st2 = lax.dot_general(kb, q2, NT_DIMS, preferred_element_type=F32)
        for hh in range(2):
            s_out[:, hh * nq:(hh + 1) * nq] = st2[:, hh * nq:(hh + 1) * nq] + bias_ref[hh, variant]

    def softmax(s_in, p_out, l_out):
        st = s_in[...]
        pt = jnp.exp(st - jnp.max(st, axis=0, keepdims=True))
        p_out[...] = pt.astype(BF16)
        l_out[...] = 1.0 / jnp.sum(pt, axis=0, keepdims=True)

    def values(p, p_in, l_in):
        r = p * NA_PAIR
        c0 = band_base(p) // 2
        vt = jnp.concatenate([vt_ref[c0 + i] for i in range(n_chunk)], axis=1)
        pt = p_in[...]
        linv = l_in[...]
        outs = []
        for hh in range(2):
            cols = slice(hh * nq, (hh + 1) * nq)
            ot = jnp.dot(vt[hh * dh:(hh + 1) * dh, :], pt[:, cols], preferred_element_type=F32)
            outs.append(ot * linv[:, cols])
        o_ref[0, pl.ds(pl.multiple_of(r * GRID_W, nq), nq), :] = jnp.concatenate(outs, axis=0).T.astype(BF16)

    s_slots = (s0_ref, s1_ref)
    p_slots = (p0_ref, p1_ref)
    l_slots = (l0_ref, l1_ref)

    def step(t, parity, do_scores=True, do_softmax=True, do_values=True):
        a, b = parity, 1 - parity
        t = jnp.asarray(t, I32)
        if do_scores:
            scores(t, s_slots[a])
        if do_softmax:
            softmax(s_slots[b], p_slots[b], l_slots[b])
        if do_values:
            values(t - 2, p_slots[a], l_slots[a])

    assert n_steps % 2 == 0 and n_steps >= 4
    step(0, 0, do_softmax=False, do_values=False)
    step(1, 1, do_values=False)

    def steady(i, carry):
        t = 2 + 2 * i
        step(t, 0)
        step(t + 1, 1)
        return carry

    lax.fori_loop(0, (n_steps - 2) // 2, steady, 0, unroll=True)
    step(n_steps, 0, do_scores=False)
    step(n_steps + 1, 1, do_scores=False, do_softmax=False)


def _neighbourhood_attention(proj_na, bias_tbl, batch, seq):
    rows = seq // GRID_W
    n_pairs = NA_HEADS // 2
    blk = (1, seq, 2 * NA_HEAD_DIM)
    n_var, nk, nq = bias_tbl.shape[1:]
    return pl.pallas_call(
        functools.partial(_na_kernel, rows),
        grid=(batch, n_pairs),
        in_specs=[
            pl.BlockSpec(blk, lambda b, h: (b, 0, h)),
            pl.BlockSpec(blk, lambda b, h: (b, 0, n_pairs + h)),
            pl.BlockSpec(blk, lambda b, h: (b, 0, 2 * n_pairs + h)),
            pl.BlockSpec((2, n_var, nk, nq), lambda b, h: (h, 0, 0, 0)),
        ],
        out_specs=pl.BlockSpec(blk, lambda b, h: (b, 0, h)),
        out_shape=jax.ShapeDtypeStruct((batch, seq, NA_W), BF16),
        scratch_shapes=[pltpu.VMEM((seq // LANES, 2 * NA_HEAD_DIM, LANES), BF16),
                        pltpu.VMEM((nk, 2 * nq), F32), pltpu.VMEM((nk, 2 * nq), F32),
                        pltpu.VMEM((nk, 2 * nq), BF16), pltpu.VMEM((nk, 2 * nq), BF16),
                        pltpu.VMEM((1, 2 * nq), F32), pltpu.VMEM((1, 2 * nq), F32)],
        compiler_params=_params("parallel", "arbitrary"),
        name="neighbourhood_attention",
    )(proj_na, proj_na, proj_na, bias_tbl)


@functools.lru_cache(maxsize=None)
def _dft_factors(seq):
    n = 2 * seq
    t = np.arange(seq, dtype=np.int64)
    f1 = np.arange(seq // 64, dtype=np.int64)
    f0 = np.arange(64, dtype=np.int64)
    a = 2.0 * np.pi * ((64 * f1[:, None] * t[None, :]) % n).astype(np.float64) / n
    b = 2.0 * np.pi * ((f0[:, None] * t[None, :]) % n).astype(np.float64) / n
    ny = np.where(t % 2 == 0, 1.0, -1.0)
    return tuple(np.asarray(v, np.float32) for v in (np.cos(a), np.sin(a), np.cos(b), np.sin(b), ny))


def _dft_table_kernel(u_ref, v_ref, cb_ref, sb_ref, fwd_ref, inv_ref):
    sin_half = pl.program_id(0) == 1
    first = pl.program_id(1) == 0
    n_fine, seq = cb_ref.shape
    row = lax.broadcasted_iota(I32, (n_fine, seq), 0)
    col = lax.broadcasted_iota(I32, (n_fine, seq), 1)
    for c in range(u_ref.shape[1]):
        tile = u_ref[0, c:c + 1, :] * cb_ref[...] + v_ref[0, c:c + 1, :] * sb_ref[...]
        rows = slice(c * n_fine, (c + 1) * n_fine)
        fwd = tile
        if c == 0:
            fwd_nyq = jnp.logical_and(jnp.logical_and(sin_half, first), row == 0)
            fwd = jnp.where(fwd_nyq, (1 - 2 * (col & 1)).astype(F32), tile)
        fwd_ref[0, rows, :] = fwd.astype(BF16)
        inv_nyq = jnp.logical_and(sin_half, col == 0)
        inv_ref[rows, :] = jnp.where(inv_nyq, (1 - 2 * (row & 1)).astype(F32), tile).astype(BF16)


def _dft_tables(seq):
    ca, sa, cb, sb, _ = _dft_factors(seq)
    u = np.stack([ca, sa])
    v = np.stack([-sa, ca])
    n_coarse = 8
    rows = 64 * n_coarse
    coarse_spec = pl.BlockSpec((1, n_coarse, seq), lambda h, i: (h, i, 0))
    fine_spec = pl.BlockSpec((64, seq), lambda h, i: (0, 0))
    return pl.pallas_call(
        _dft_table_kernel,
        grid=(2, seq // rows),
        in_specs=[coarse_spec, coarse_spec, fine_spec, fine_spec],
        out_specs=[pl.BlockSpec((1, rows, seq), lambda h, i: (h, i, 0)),
                   pl.BlockSpec((rows, seq), lambda h, i: (i, h))],
        out_shape=[jax.ShapeDtypeStruct((2, seq, seq), BF16), jax.ShapeDtypeStruct((seq, 2 * seq), BF16)],
        compiler_params=_params("parallel", "arbitrary"),
        name="dft_tables",
    )(jnp.asarray(u), jnp.asarray(v), jnp.asarray(cb), jnp.asarray(sb))


@functools.lru_cache(maxsize=None)
def _hyena_consts(seq):
    t = np.arange(seq, dtype=np.float64)
    t01 = t / (seq - 1)
    bands = np.linspace(1e-4, HY_BANDS - 1, HY_BANDS)
    ang = (2.0 * math.pi) * (t[:, None] / seq) * bands[None, :]
    feats = np.concatenate([t01[:, None], np.cos(ang), -np.sin(ang)], axis=-1)
    feats_p = np.zeros((seq, LANES), np.float32)
    feats_p[:, :HY_POS_DIM] = feats
    min_decay = math.log(1e-2) / 1.5
    max_decay = math.log(1e-2) / 0.3
    deltas = np.abs(np.linspace(min_decay, max_decay, HY_W))
    window = np.exp(-t01[:, None] * deltas[None, :]).astype(np.float32)
    return feats_p, window


def _filter_kernel(feats_ref, w1_ref, b1_ref, w2_ref, b2_ref, freq_ref, w3f_ref, w3b_ref, win_ref, sum_ref,
                   diff_ref, hid_ref):
    hp = lax.Precision.HIGHEST

    @pl.when(pl.program_id(0) == 0)
    def _():
        f = freq_ref[...]
        h1 = jnp.sin(f * (jnp.dot(feats_ref[...], w1_ref[...], precision=hp, preferred_element_type=F32)
                          + b1_ref[...]))
        hid_ref[...] = jnp.sin(f * (jnp.dot(h1, w2_ref[...], precision=hp, preferred_element_type=F32)
                                    + b2_ref[...]))

    hid = hid_ref[...]
    win = win_ref[...]
    fwd = jnp.dot(hid, w3f_ref[...], precision=hp, preferred_element_type=F32) * win
    bwd = jnp.dot(hid, w3b_ref[...], precision=hp, preferred_element_type=F32) * win
    bwd = jnp.where(lax.broadcasted_iota(I32, bwd.shape, 0) == 0, 0.0, bwd)
    sum_ref[...] = (fwd + bwd).astype(BF16)
    diff_ref[...] = (fwd - bwd).astype(BF16)


def _hyena_filters_time(w1, b1, w2, b2, w3, freq, seq):
    feats, window = _hyena_consts(seq)
    w1p = jnp.zeros((LANES, HY_FILT_FF), F32).at[:HY_POS_DIM].set(w1)
    full = lambda shape: pl.BlockSpec(shape, lambda o: (0,) * len(shape))
    out_spec = pl.BlockSpec((seq, HY_W), lambda o: (0, o))
    out_shape = jax.ShapeDtypeStruct((seq, HY_ORDER * HY_W), BF16)
    return pl.pallas_call(
        _filter_kernel,
        grid=(HY_ORDER,),
        in_specs=[
            full((seq, LANES)), full((LANES, HY_FILT_FF)), full((1, HY_FILT_FF)),
            full((HY_FILT_FF, HY_FILT_FF)), full((1, HY_FILT_FF)), full((1, HY_FILT_FF)),
            pl.BlockSpec((HY_FILT_FF, HY_W), lambda o: (0, 2 * o)),
            pl.BlockSpec((HY_FILT_FF, HY_W), lambda o: (0, 2 * o + 1)),
            full((seq, HY_W)),
        ],
        out_specs=[out_spec, out_spec],
        out_shape=[out_shape, out_shape],
        scratch_shapes=[pltpu.VMEM((seq, HY_FILT_FF), F32)],
        compiler_params=_params("arbitrary"),
        name="hyena_filter_mlp",
    )(jnp.asarray(feats), w1p, b1.reshape(1, -1), w2, b2.reshape(1, -1), freq.reshape(1, -1), w3, w3,
      jnp.asarray(window))


def _filter_dft_kernel(f_ref, sum_ref, diff_ref, o_ref):
    o_ref[0] = jnp.dot(f_ref[0], sum_ref[...], preferred_element_type=F32)
    o_ref[1] = jnp.dot(f_ref[1], diff_ref[...], preferred_element_type=F32)

    @pl.when(pl.program_id(0) == 0)
    def _():
        top = jnp.dot(f_ref[1, 0:8, :], sum_ref[...], preferred_element_type=F32)
        row = lax.broadcasted_iota(I32, top.shape, 0)
        o_ref[1, 0:8, :] = jnp.where(row == 0, top, o_ref[1, 0:8, :])


def _filter_dft(dft_fwd, filt_sum, filt_diff, seq, fb):
    return pl.pallas_call(
        _filter_dft_kernel,
        grid=(seq // fb, HY_ORDER),
        in_specs=[
            pl.BlockSpec((2, fb, seq), lambda f, o: (0, f, 0)),
            pl.BlockSpec((seq, HY_W), lambda f, o: (0, o)),
            pl.BlockSpec((seq, HY_W), lambda f, o: (0, o)),
        ],
        out_specs=pl.BlockSpec((2, fb, HY_W), lambda f, o: (0, f, o)),
        out_shape=jax.ShapeDtypeStruct((2, seq, HY_ORDER * HY_W), F32),
        compiler_params=_params("parallel", "arbitrary"),
        name="hyena_filter_dft",
    )(dft_fwd, filt_sum, filt_diff)


def _short_conv_kernel(p_ref, w_ref, b_ref, o_ref):
    p = p_ref[0].astype(F32)
    seq = p.shape[0]
    row = lax.broadcasted_iota(I32, p.shape, 0)
    prev = jnp.where(row == 0, 0.0, pltpu.roll(p, 1, 0))
    nxt = jnp.where(row == seq - 1, 0.0, pltpu.roll(p, seq - 1, 0))
    w = w_ref[...]
    o_ref[0] = prev * w[0:1] + p * w[1:2] + nxt * w[2:3] + b_ref[...]


def _short_conv(proj, col0, conv_w, conv_b, batch, seq, tc):
    n_cols = conv_w.shape[-1]
    assert col0 % tc == 0 and n_cols % tc == 0
    cb0 = col0 // tc
    return pl.pallas_call(
        _short_conv_kernel,
        grid=(batch, n_cols // tc),
        in_specs=[
            pl.BlockSpec((1, seq, tc), lambda b, c: (b, 0, cb0 + c)),
            pl.BlockSpec((3, tc), lambda b, c: (0, c)),
            pl.BlockSpec((1, tc), lambda b, c: (0, c)),
        ],
        out_specs=pl.BlockSpec((1, seq, tc), lambda b, c: (b, 0, c)),
        out_shape=jax.ShapeDtypeStruct((batch, seq, n_cols), F32),
        compiler_params=_params("parallel", "arbitrary"),
        name="hyena_short_conv",
    )(proj, conv_w, conv_b.reshape(1, -1))


def _spectrum_kernel(n_fft, f_ref, z_ref, k_ref, o_ref):
    z = z_ref[0].astype(BF16)
    xr = jnp.dot(f_ref[0], z, preferred_element_type=F32)
    xs = jnp.dot(f_ref[1], z, preferred_element_type=F32)
    kr = k_ref[0]
    ks = k_ref[1]
    row = lax.broadcasted_iota(I32, xr.shape, 0)
    edge = jnp.logical_and(row == 0, pl.program_id(0) == 0)
    yr = jnp.where(edge, xr * kr * (1.0 / n_fft), (xr * kr - xs * ks) * (2.0 / n_fft))
    ys = jnp.where(edge, xs * ks * (1.0 / n_fft), (xr * ks + xs * kr) * (2.0 / n_fft))
    o_ref[0, 0] = yr.astype(BF16)
    o_ref[0, 1] = ys.astype(BF16)


def _spectrum_product(dft_fwd, z_arr, z_col, kfreq, order, batch, seq, fb):
    return pl.pallas_call(
        functools.partial(_spectrum_kernel, 2 * seq),
        grid=(seq // fb, batch),
        in_specs=[
            pl.BlockSpec((2, fb, seq), lambda f, b: (0, f, 0)),
            pl.BlockSpec((1, seq, HY_W), lambda f, b: (b, 0, z_col)),
            pl.BlockSpec((2, fb, HY_W), lambda f, b: (0, f, order)),
        ],
        out_specs=pl.BlockSpec((1, 2, fb, HY_W), lambda f, b: (b, 0, f, 0)),
        out_shape=jax.ShapeDtypeStruct((batch, 2, seq, HY_W), BF16),
        compiler_params=_params("parallel", "arbitrary"),
        name="hyena_spectrum",
    )(dft_fwd, z_arr, kfreq)


def _inverse_kernel(ft_ref, y_ref, gate_ref, z_ref, d_ref, o_ref):
    conv = jnp.dot(ft_ref[...], y_ref[0], preferred_element_type=F32)
    o_ref[0] = gate_ref[0] * (conv + d_ref[...] * z_ref[0])


def _inverse_gate(dft_inv, y, gate_arr, gate_col, z_arr, z_col, skip_row, batch, seq, tb):
    y2 = y.reshape(batch, 2 * seq, HY_W)
    return pl.pallas_call(
        _inverse_kernel,
        grid=(seq // tb, batch),
        in_specs=[
            pl.BlockSpec((tb, 2 * seq), lambda t, b: (t, 0)),
            pl.BlockSpec((1, 2 * seq, HY_W), lambda t, b: (b, 0, 0)),
            pl.BlockSpec((1, tb, HY_W), lambda t, b: (b, t, gate_col)),
            pl.BlockSpec((1, tb, HY_W), lambda t, b: (b, t, z_col)),
            pl.BlockSpec((1, HY_W), lambda t, b: (0, 0)),
        ],
        out_specs=pl.BlockSpec((1, tb, HY_W), lambda t, b: (b, t, 0)),
        out_shape=jax.ShapeDtypeStruct((batch, seq, HY_W), F32),
        compiler_params=_params("parallel", "arbitrary"),
        name="hyena_inverse_gate",
    )(dft_inv, y2, gate_arr, z_arr, skip_row.reshape(1, HY_W))


def _hyena_mixer(proj, col0, conv_w, conv_b, w1, b1, w2, b2, w3, freq, skip_d, dft_fwd, dft_inv, batch, seq):
    s = _short_conv(proj, col0, conv_w, conv_b, batch, seq, 768)
    filt_sum, filt_diff = _hyena_filters_time(w1, b1, w2, b2, w3, freq, seq)
    kfreq = _filter_dft(dft_fwd, filt_sum, filt_diff, seq, 512)
    z_arr, z_col = s, 2
    for o in range(HY_ORDER):
        y = _spectrum_product(dft_fwd, z_arr, z_col, kfreq, o, batch, seq, 1024)
        z_arr = _inverse_gate(dft_inv, y, s, o, z_arr, z_col, skip_d[o], batch, seq, 1024)
        z_col = 0
    return z_arr


@functools.lru_cache(maxsize=None)
def _retention_consts(seq):
    c = RET_CHUNK
    half = RET_HEAD_DIM // 2
    inv = 1.0 / (10000.0 ** np.linspace(0.0, 1.0, half))
    ang = np.arange(seq, dtype=np.float64)[:, None] * inv[None, :]
    cos2 = np.concatenate([np.cos(ang), np.cos(ang)], axis=-1).astype(np.float32)
    sin2 = np.concatenate([-np.sin(ang), np.sin(ang)], axis=-1).astype(np.float32)
    hidx = np.arange(RET_HEADS, dtype=np.float64)
    lg_f = np.log1p(-np.exp2(-5.0 - hidx))[:, None, None]
    lg_b = np.log1p(-np.exp2(-5.5 - hidx))[:, None, None]
    i = np.arange(c, dtype=np.float64)
    diff = i[:, None] - i[None, :]
    ones = np.ones((1, c, c))
    dec = np.where(diff >= 0, np.exp(lg_f * np.maximum(diff, 0.0)), np.exp(lg_b * np.maximum(-diff, 0.0)))
    rowv = lambda v: v[:, :, None] * ones
    tab = np.stack([
        dec,
        rowv(np.exp(lg_f[:, :, 0] * (i + 1.0)[None, :])),
        rowv(np.exp(lg_f[:, :, 0] * (c - 1.0 - i)[None, :])),
        rowv(np.exp(lg_b[:, :, 0] * (c - i)[None, :])),
        rowv(np.exp(lg_b[:, :, 0] * i[None, :])),
        np.exp(lg_f * c) * ones,
        np.exp(lg_b * c) * ones,
    ], axis=1).astype(np.float32)
    return cos2, sin2, tab


def _retention_kernel(q_ref, k_ref, v_ref, g_ref, cos_ref, sin_ref, tab_ref, o_ref, qs_ref, ks_ref, kvf_ref,
                      kvb_ref, a0_ref, a1_ref, y0_ref, y1_ref):
    c = RET_CHUNK
    d = RET_HEAD_DIM
    seq = q_ref.shape[1]
    n_chunks = seq // c
    cos = cos_ref[...]
    sin = sin_ref[...]
    q = q_ref[0].astype(F32)
    k = k_ref[0].astype(F32)
    qs_ref[...] = (q * cos + pltpu.roll(q, d // 2, 1) * sin) * (d ** -0.5)
    ks_ref[...] = k * cos + pltpu.roll(k, d // 2, 1) * sin

    def mm(a, b):
        return jnp.dot(a.astype(BF16), b.astype(BF16), preferred_element_type=F32)

    def chunk(n):
        return pl.ds(pl.multiple_of(n * c, c), c)

    def kv_body(n, carry):
        kc = ks_ref[chunk(n), :]
        vc = v_ref[0, chunk(n), :]
        kvf_ref[n] = mm((kc * tab_ref[0, 2]).T, vc)
        kvb_ref[n] = mm((kc * tab_ref[0, 4]).T, vc)
        return carry

    lax.fori_loop(0, n_chunks, kv_body, 0, unroll=True)

    def scan_fwd(n, state):
        kv = kvf_ref[n]
        kvf_ref[n] = state
        return tab_ref[0, 5] * state + kv

    lax.fori_loop(0, n_chunks, scan_fwd, jnp.zeros((d, d), F32))

    def scan_bwd(m, state):
        n = n_chunks - 1 - m
        kv = kvb_ref[n]
        kvb_ref[n] = state
        return tab_ref[0, 6] * state + kv

    lax.fori_loop(0, n_chunks, scan_bwd, jnp.zeros((d, d), F32))

    def in_chunk(n, a_out):
        qc = qs_ref[chunk(n), :].astype(BF16)
        kc = ks_ref[chunk(n), :].astype(BF16)
        a = lax.dot_general(qc, kc, NT_DIMS, preferred_element_type=F32) * tab_ref[0, 0]
        a_out[...] = a.astype(BF16)

    def mix(n, a_in, y_out):
        qc = qs_ref[chunk(n), :]
        y_out[...] = (jnp.dot(a_in[...], v_ref[0, chunk(n), :], preferred_element_type=F32)
                      + mm(qc * tab_ref[0, 1], kvf_ref[n]) + mm(qc * tab_ref[0, 3], kvb_ref[n]))

    def finish(n, y_in):
        y = y_in[...]
        mu = jnp.mean(y, axis=-1, keepdims=True)
        yc = y - mu
        var = jnp.mean(yc * yc, axis=-1, keepdims=True)
        g = g_ref[0, chunk(n), :].astype(F32)
        o_ref[0, chunk(n), :] = (yc * lax.rsqrt(var + GN_EPS) * (g * jax.nn.sigmoid(g))).astype(BF16)

    a_slots = (a0_ref, a1_ref)
    y_slots = (y0_ref, y1_ref)

    def step(t, parity, do_a=True, do_mix=True, do_finish=True):
        t = jnp.asarray(t, I32)
        if do_a:
            in_chunk(t, a_slots[parity])
        if do_mix:
            mix(t - 1, a_slots[1 - parity], y_slots[1 - parity])
        if do_finish:
            finish(t - 2, y_slots[parity])

    assert n_chunks % 2 == 0 and n_chunks >= 4
    step(0, 0, do_mix=False, do_finish=False)
    step(1, 1, do_finish=False)

    def steady(i, carry):
        step(2 + 2 * i, 0)
        step(3 + 2 * i, 1)
        return carry

    lax.fori_loop(0, (n_chunks - 2) // 2, steady, 0, unroll=True)
    step(n_chunks, 0, do_a=False)
    step(n_chunks + 1, 1, do_a=False, do_mix=False)


def _retention_mixer(proj, col0, batch, seq):
    cos2, sin2, tab = _retention_consts(seq)
    blk = (1, seq, RET_HEAD_DIM)
    h_ = RET_HEADS
    assert col0 % RET_HEAD_DIM == 0
    c0 = col0 // RET_HEAD_DIM
    return pl.pallas_call(
        _retention_kernel,
        grid=(batch, RET_HEADS),
        in_specs=[
            pl.BlockSpec(blk, lambda b, h: (b, 0, c0 + h)),
            pl.BlockSpec(blk, lambda b, h: (b, 0, c0 + h_ + h)),
            pl.BlockSpec(blk, lambda b, h: (b, 0, c0 + 2 * h_ + h)),
            pl.BlockSpec(blk, lambda b, h: (b, 0, c0 + 3 * h_ + h)),
            pl.BlockSpec((seq, RET_HEAD_DIM), lambda b, h: (0, 0)),
            pl.BlockSpec((seq, RET_HEAD_DIM), lambda b, h: (0, 0)),
            pl.BlockSpec((1, 7, RET_CHUNK, RET_CHUNK), lambda b, h: (h, 0, 0, 0)),
        ],
        out_specs=pl.BlockSpec(blk, lambda b, h: (b, 0, h)),
        out_shape=jax.ShapeDtypeStruct((batch, seq, RET_W), BF16),
        scratch_shapes=[pltpu.VMEM((seq, RET_HEAD_DIM), F32), pltpu.VMEM((seq, RET_HEAD_DIM), F32),
                        pltpu.VMEM((seq // RET_CHUNK, RET_HEAD_DIM, RET_HEAD_DIM), F32),
                        pltpu.VMEM((seq // RET_CHUNK, RET_HEAD_DIM, RET_HEAD_DIM), F32),
                        pltpu.VMEM((RET_CHUNK, RET_CHUNK), BF16), pltpu.VMEM((RET_CHUNK, RET_CHUNK), BF16),
                        pltpu.VMEM((RET_CHUNK, RET_HEAD_DIM), F32), pltpu.VMEM((RET_CHUNK, RET_HEAD_DIM), F32)],
        compiler_params=_params("parallel", "arbitrary"),
        name="retention",
    )(proj, proj, proj, proj, jnp.asarray(cos2), jnp.asarray(sin2), jnp.asarray(tab))


def _cross_out_router_kernel(q_ref, k_ref, v_ref, w_ref, res_ref, g_ref, wr_ref, x_ref, hm_ref, aff_ref, h_ref):
    n_e = aff_ref.shape[1]
    _cross_prologue(h_ref, q_ref, k_ref, v_ref)
    x = res_ref[...] + jnp.dot(h_ref[...], w_ref[...], preferred_element_type=F32)
    x_ref[...] = x
    h = _rms(x, g_ref[...])
    h_hi = h.astype(BF16)
    hm_ref[...] = h_hi
    h_lo = (h - h_hi.astype(F32)).astype(BF16)
    wr = wr_ref[...]
    w_hi = wr.astype(BF16)
    w_lo = (wr - w_hi.astype(F32)).astype(BF16)
    both = jnp.dot(h_hi, jnp.concatenate([w_hi, w_lo], axis=1), preferred_element_type=F32)
    logits = (both[:, :LANES] + both[:, LANES:] + jnp.dot(h_lo, w_hi, preferred_element_type=F32)).T[:n_e]
    m = jnp.max(logits, axis=0, keepdims=True)
    e = jnp.exp(logits - m)
    aff_ref[0] = e / jnp.sum(e, axis=0, keepdims=True)


def _cross_out_and_router(q, kv, w_co, layer, res, gain, w_router, batch, seq, n_mem, tm):
    m, d = q.shape
    per_b = seq // tm
    assert seq % tm == 0
    w_pad = jnp.zeros((d, LANES), F32).at[:, :N_EXPERTS].set(w_router)
    row_spec = pl.BlockSpec((tm, d), lambda i: (i, 0))
    return pl.pallas_call(
        _cross_out_router_kernel,
        grid=(m // tm,),
        in_specs=[
            row_spec,
            pl.BlockSpec((n_mem, d), lambda i: (i // per_b, 2 * layer)),
            pl.BlockSpec((n_mem, d), lambda i: (i // per_b, 2 * layer + 1)),
            pl.BlockSpec((None, d, d), lambda i: (layer, 0, 0)),
            row_spec,
            pl.BlockSpec((1, d), lambda i: (0, 0)),
            pl.BlockSpec((d, LANES), lambda i: (0, 0)),
        ],
        out_specs=[
            row_spec,
            row_spec,
            pl.BlockSpec((1, N_EXPERTS, tm), lambda i: (i // per_b, 0, i % per_b)),
        ],
        out_shape=[
            jax.ShapeDtypeStruct((m, d), F32),
            jax.ShapeDtypeStruct((m, d), BF16),
            jax.ShapeDtypeStruct((batch, N_EXPERTS, seq), F32),
        ],
        scratch_shapes=[pltpu.VMEM((tm, d), BF16)],
        compiler_params=_params("parallel"),
        name="cross_out_router",
    )(q, kv, kv, w_co, res, gain.reshape(1, d), w_pad)


def _sort_descending(x):
    rows, n = x.shape
    lanes = LANES
    n_chunks = n // lanes
    chunks = [x[:, c * lanes:(c + 1) * lanes] for c in range(n_chunks)]
    lane = lax.broadcasted_iota(I32, (rows, lanes), 1)
    k = 2
    while k <= n:
        j = k // 2
        while j >= 1:
            nxt = []
            for c, xc in enumerate(chunks):
                if j >= lanes:
                    lo = (c & (j // lanes)) == 0
                    partner = chunks[c ^ (j // lanes)]
                else:
                    lo = (lane & j) == 0
                    partner = jnp.where(lo, pltpu.roll(xc, lanes - j, 1), pltpu.roll(xc, j, 1))
                desc = ((c * lanes) & k) == 0 if k >= lanes else (lane & k) == 0
                hi, lw = jnp.maximum(xc, partner), jnp.minimum(xc, partner)
                if j >= lanes and k >= lanes:
                    nxt.append(hi if lo == desc else lw)
                else:
                    nxt.append(jnp.where(lo == desc, hi, lw))
            chunks = nxt
            j //= 2
        k *= 2
    return jnp.concatenate(chunks, axis=1)


def _topk_kernel(cap, n_e, aff_ref, slot_ref, wsel_ref, slot_t_ref, thr_ref):
    n_rows, n_tok = aff_ref.shape

    def thr_body(i, carry):
        rows8 = pl.ds(pl.multiple_of(i * 8, 8), 8)
        thr_ref[rows8, :] = _sort_descending(aff_ref[rows8, :])[:, cap - 1:cap]
        return carry

    lax.fori_loop(0, n_rows // 8, thr_body, 0)
    a = aff_ref[...]
    thr = thr_ref[...]
    gt = a > thr
    eq = a == thr
    need = cap - jnp.sum(gt.astype(I32), axis=1, keepdims=True)
    upper = (lax.broadcasted_iota(I32, (n_tok, n_tok), 0) < lax.broadcasted_iota(I32, (n_tok, n_tok), 1))
    upper = upper.astype(BF16)
    eq_rank = jnp.dot(eq.astype(BF16), upper, preferred_element_type=F32)
    sel = jnp.logical_or(gt, jnp.logical_and(eq, eq_rank < need.astype(F32)))
    rank = jnp.dot(sel.astype(BF16), upper, preferred_element_type=F32)
    slot = jnp.where(sel, rank, -1.0)
    slot_ref[...] = slot.astype(I32)
    wsel_ref[...] = jnp.where(sel, a, 0.0)
    pad = jnp.full((LANES - n_e, n_tok), -1.0, F32)
    for b in range(n_rows // n_e):
        slot_t_ref[b] = jnp.concatenate([slot[b * n_e:(b + 1) * n_e], pad], axis=0).T.astype(I32)


def _topk_select(aff2d, cap, n_e):
    rows, n_tok = aff2d.shape
    spec = pl.BlockSpec((rows, n_tok), lambda i: (0, 0))
    spec_t = pl.BlockSpec((rows // n_e, n_tok, LANES), lambda i: (0, 0, 0))
    return pl.pallas_call(
        functools.partial(_topk_kernel, cap, n_e),
        grid=(1,),
        in_specs=[spec],
        out_specs=[spec, spec, spec_t],
        out_shape=[jax.ShapeDtypeStruct((rows, n_tok), I32), jax.ShapeDtypeStruct((rows, n_tok), F32),
                   jax.ShapeDtypeStruct((rows // n_e, n_tok, LANES), I32)],
        scratch_shapes=[pltpu.VMEM((rows, 1), F32)],
        compiler_params=_params("arbitrary"),
        name="moe_topk_select",
    )(aff2d)


def _gather_kernel(cap, slot_ref, wsel_ref, hm_ref, xe_ref, gs_ref):
    n_tok = hm_ref.shape[0]
    onehot = slot_ref[0] == lax.broadcasted_iota(I32, (cap, n_tok), 0)
    xe_ref[0] = jnp.dot(onehot.astype(BF16), hm_ref[...], preferred_element_type=F32).astype(BF16)
    gs_ref[0] = jnp.sum(jnp.where(onehot, wsel_ref[0], 0.0), axis=1, keepdims=True)


def _gather_tokens(slot, wsel, hm, batch, seq, cap):
    rows = batch * N_EXPERTS
    row_spec = pl.BlockSpec((1, 1, seq), lambda b, e: (b * N_EXPERTS + e, 0, 0))
    return pl.pallas_call(
        functools.partial(_gather_kernel, cap),
        grid=(batch, N_EXPERTS),
        in_specs=[row_spec, row_spec, pl.BlockSpec((seq, D_MODEL), lambda b, e: (b, 0))],
        out_specs=[
            pl.BlockSpec((1, cap, D_MODEL), lambda b, e: (e, b, 0)),
            pl.BlockSpec((1, cap, 1), lambda b, e: (e, b, 0)),
        ],
        out_shape=[
            jax.ShapeDtypeStruct((N_EXPERTS, batch * cap, D_MODEL), BF16),
            jax.ShapeDtypeStruct((N_EXPERTS, batch * cap, 1), F32),
        ],
        compiler_params=_params("parallel", "arbitrary"),
        name="moe_gather",
    )(slot.reshape(rows, 1, seq), wsel.reshape(rows, 1, seq), hm)


def _expert_kernel(n_e, n_t, xe_ref, wg_ref, wu_ref, wd_ref, gs_ref, ye_ref, mid_ref):
    g = pl.program_id(0)
    s = pl.program_id(1)
    t = s // 2
    tf = wg_ref.shape[-1]

    @pl.when(jnp.logical_and(s % 2 == 0, g < n_e))
    def _():
        x = xe_ref[0]
        a = jnp.dot(x, wg_ref[...].astype(BF16), preferred_element_type=F32)
        u = jnp.dot(x, wu_ref[...].astype(BF16), preferred_element_type=F32)
        mid_ref[g % 2, t] = (a * jax.nn.sigmoid(a) * u).astype(BF16)

    @pl.when(jnp.logical_and(s % 2 == 1, g >= 1))
    def _():
        prev = (g - 1) % 2
        acc = jnp.dot(mid_ref[prev, 0], wd_ref[0:tf, :].astype(BF16), preferred_element_type=F32)
        for f in range(1, n_t):
            acc += jnp.dot(mid_ref[prev, f], wd_ref[f * tf:(f + 1) * tf, :].astype(BF16),
                           preferred_element_type=F32)
        ye_ref[0] = (acc * gs_ref[0]).astype(BF16)


def _expert_ffn(xe, gs, w_gate, w_up, w_down, layer, tf):
    n_e, rows, d = xe.shape
    ff = w_gate.shape[-1]
    n_t = ff // tf
    assert d // tf == n_t
    last = n_e - 1

    n_s = 2 * n_t

    def fill_tile(g, s, lead):
        v = jnp.minimum(g * n_s + s + lead, n_e * n_s - 1)
        return (layer, v // n_s, 0, (v % n_s) // 2)

    def emit_tile(g, s):
        v = jnp.maximum(g * n_s + s - 1 - n_s, 0)
        return (layer, v // n_s, 0, (v % n_s) // 2)

    return pl.pallas_call(
        functools.partial(_expert_kernel, n_e, n_t),
        grid=(n_e + 1, n_s),
        in_specs=[
            pl.BlockSpec((1, rows, d), lambda g, s: (jnp.minimum(g, last), 0, 0)),
            pl.BlockSpec((None, None, d, tf), functools.partial(fill_tile, lead=1)),
            pl.BlockSpec((None, None, d, tf), functools.partial(fill_tile, lead=0)),
            pl.BlockSpec((None, None, ff, tf), emit_tile),
            pl.BlockSpec((1, rows, 1), lambda g, s: (jnp.maximum(g - 1, 0), 0, 0)),
        ],
        out_specs=pl.BlockSpec((1, rows, tf), lambda g, s: emit_tile(g, s)[1:]),
        out_shape=jax.ShapeDtypeStruct((n_e, rows, d), BF16),
        scratch_shapes=[pltpu.VMEM((2, n_t, rows, tf), BF16)],
        compiler_params=_params("arbitrary", "arbitrary"),
        name="moe_expert_ffn",
    )(xe, w_gate, w_up, w_down, gs)


def _scatter_kernel(cap, slot_ref, ye_ref, x_ref, o_ref, onehot_ref):
    n_e = ye_ref.shape[0]
    n_tok = x_ref.shape[0]

    @pl.when(pl.program_id(1) == 0)
    def _():
        col = lax.broadcasted_iota(I32, (n_tok, cap), 1)
        slots = slot_ref[0]
        for e in range(n_e):
            onehot_ref[:, e * cap:(e + 1) * cap] = (slots[:, e:e + 1] == col).astype(BF16)

    ye = ye_ref[...].reshape(n_e * cap, ye_ref.shape[-1])
    o_ref[...] = x_ref[...] + jnp.dot(onehot_ref[...], ye, preferred_element_type=F32)


def _scatter_add(slot_t, ye, x2d, batch, seq, cap, tn):
    return pl.pallas_call(
        functools.partial(_scatter_kernel, cap),
        grid=(batch, D_MODEL // tn),
        in_specs=[
            pl.BlockSpec((1, seq, LANES), lambda b, n: (b, 0, 0)),
            pl.BlockSpec((N_EXPERTS, cap, tn), lambda b, n: (0, b, n)),
            pl.BlockSpec((seq, tn), lambda b, n: (b, n)),
        ],
        out_specs=pl.BlockSpec((seq, tn), lambda b, n: (b, n)),
        out_shape=jax.ShapeDtypeStruct(x2d.shape, F32),
        scratch_shapes=[pltpu.VMEM((seq, N_EXPERTS * cap), BF16)],
        compiler_params=_params("parallel", "arbitrary"),
        name="moe_scatter_add",
    )(slot_t, ye, x2d)


def _expert_choice_ffn(x2d, hm, aff, w_gate, w_up, w_down, layer, batch, seq):
    cap = EC_CAPACITY * seq // N_EXPERTS
    slot, wsel, slot_t = _topk_select(aff.reshape(batch * N_EXPERTS, seq), cap, N_EXPERTS)
    xe, gs = _gather_tokens(slot, wsel, hm, batch, seq, cap)
    ye = _expert_ffn(xe, gs, w_gate, w_up, w_down, layer, 512)
    return _scatter_add(slot_t, ye, x2d, batch, seq, cap, 512)


def _final_norm_kernel(x_ref, g_ref, o_ref):
    o_ref[...] = _rms(x_ref[...], g_ref[...])


def _final_norm(x2d, gain, tm):
    m, d = x2d.shape
    return pl.pallas_call(
        _final_norm_kernel,
        grid=(m // tm,),
        in_specs=[pl.BlockSpec((tm, d), lambda i: (i, 0)), pl.BlockSpec((1, d), lambda i: (0, 0))],
        out_specs=pl.BlockSpec((tm, d), lambda i: (i, 0)),
        out_shape=jax.ShapeDtypeStruct((m, d), F32),
        compiler_params=_params("parallel"),
        name="final_norm",
    )(x2d, gain.reshape(1, d))


def kernel(x, mem, norm_mix, w_in, na_rpb, hy_conv_w, hy_conv_b, hy_filt_w1, hy_filt_b1, hy_filt_w2, hy_filt_b2, hy_filt_w3, hy_sin_freq, hy_skip_d, branch_norm, w_out, norm_cross, mem_norm, w_cq, w_ckv, w_co, norm_moe, w_router, w_gate, w_up, w_down, final_norm):
    batch, seq, d = x.shape
    n_mem = mem.shape[1]
    depth = w_in.shape[0]
    m = batch * seq
    na_cols = 3 * NA_W
    hy_cols = 3 * HY_W
    rows = seq // GRID_W

    dft_fwd, dft_inv = _dft_tables(seq)
    x2d = x.reshape(m, d)
    mem2d = mem.reshape(batch * n_mem, d)

    w_out_b, w_cq_b, w_co_b = (w.astype(BF16) for w in (w_out, w_cq, w_co))
    p_in = w_in.shape[-1]
    kv = _norm_matmul(mem2d, mem_norm, (w_ckv, None, 0, w_ckv.shape[-1]), batch * n_mem, 1024, BF16,
                      "cross_kv_proj")

    for l in range(depth):
        proj = _norm_matmul(x2d, norm_mix[l], _whole(w_in, l), 1024, 768, BF16, "in_proj")
        proj = proj.reshape(batch, seq, p_in)

        y_na = _neighbourhood_attention(proj, _na_bias_table(na_rpb[l], rows), batch, seq)
        y_hy = _hyena_mixer(proj, na_cols, hy_conv_w[l], hy_conv_b[l], hy_filt_w1[l],
                            hy_filt_b1[l], hy_filt_w2[l], hy_filt_b2[l], hy_filt_w3[l], hy_sin_freq[l],
                            hy_skip_d[l], dft_fwd, dft_inv, batch, seq)
        y_ret = _retention_mixer(proj, na_cols + hy_cols, batch, seq)
        x2d = _branch_out_proj(y_na.reshape(m, NA_W), y_hy.reshape(m, HY_W), y_ret.reshape(m, RET_W),
                               branch_norm[l], _whole(w_out_b, l), x2d, 1024, 1024)

        q = _norm_matmul(x2d, norm_cross[l], _whole(w_cq_b, l), 1024, 1024, BF16, "cross_q_proj")
        x2d, hm, aff = _cross_out_and_router(q, kv, w_co_b, l, x2d, norm_moe[l], w_router[l], batch, seq, n_mem,
                                             256)

        x2d = _expert_choice_ffn(x2d, hm, aff, w_gate, w_up, w_down, l, batch, seq)

    return _final_norm(x2d, final_norm, 512).reshape(batch, seq, d)
```

```python
import functools
import math

import numpy as np
import jax
import jax.numpy as jnp
from jax import lax
from jax.experimental import pallas as pl
from jax.experimental.pallas import tpu as pltpu

F32 = jnp.float32
BF16 = jnp.bfloat16
I32 = jnp.int32

D_MODEL = 2048
GRID_W = 64
NA_HEAD_DIM = 64
NA_W = 768
NA_HEADS = 12
NA_KR = 8
NA_KC = 16
HY_W = 512
HY_ORDER = 2
HY_BANDS = 8
HY_POS_DIM = 17
HY_FILT_FF = 64
RET_HEAD_DIM = 128
RET_W = 768
RET_HEADS = 6
RET_CHUNK = 128
CROSS_HEADS = 4
CROSS_HEAD_DIM = 512
N_EXPERTS = 16
EXPERT_FF = 2048
EC_CAPACITY = 2
RMS_EPS = 1e-6
GN_EPS = 1e-5

MASK_VALUE = -1e30
LANES = 128
VMEM_LIMIT_BYTES = 56 * 1024 * 1024

NT_DIMS = (((1,), (1,)), ((), ()))


def _params(*sem):
    return pltpu.CompilerParams(dimension_semantics=sem, vmem_limit_bytes=VMEM_LIMIT_BYTES)


def _rms(xf, g):
    return xf * lax.rsqrt(jnp.mean(xf * xf, axis=-1, keepdims=True) + RMS_EPS) * g


def _mm_kernel(prologue, n_pro, has_res, *refs):
    pro_refs = refs[:n_pro]
    w_ref = refs[n_pro]
    res_ref = refs[n_pro + 1] if has_res else None
    o_ref = refs[n_pro + 1 + has_res]
    h_ref = refs[n_pro + 2 + has_res]

    @pl.when(pl.program_id(1) == 0)
    def _():
        prologue(h_ref, *pro_refs)

    w = w_ref[...]
    if w.dtype != BF16:
        w = w.astype(BF16)
    acc = jnp.dot(h_ref[...], w, preferred_element_type=F32)
    if has_res:
        acc = acc + res_ref[...]
    o_ref[...] = acc.astype(o_ref.dtype)


def _fused_matmul(prologue, pro_args, pro_specs, wsel, res, m, tm, tn, out_dtype, name):
    w, layer, col0, n = wsel
    k = w.shape[1]
    cb0 = col0 // tn
    assert col0 % tn == 0 and n % tn == 0 and m % tm == 0
    has_res = res is not None
    if layer is None:
        per_layer = n // tn
        n = n * w.shape[0]
        w_spec = pl.BlockSpec((None, k, tn), lambda i, j: (j // per_layer, 0, cb0 + j % per_layer))
    else:
        w_spec = pl.BlockSpec((None, k, tn), lambda i, j: (layer, 0, cb0 + j))
    in_specs = list(pro_specs) + [w_spec]
    args = list(pro_args) + [w]
    if has_res:
        in_specs.append(pl.BlockSpec((tm, tn), lambda i, j: (i, j)))
        args.append(res)
    return pl.pallas_call(
        functools.partial(_mm_kernel, prologue, len(pro_args), has_res),
        grid=(m // tm, n // tn),
        in_specs=in_specs,
        out_specs=pl.BlockSpec((tm, tn), lambda i, j: (i, j)),
        out_shape=jax.ShapeDtypeStruct((m, n), out_dtype),
        scratch_shapes=[pltpu.VMEM((tm, k), BF16)],
        compiler_params=_params("parallel", "arbitrary"),
        name=name,
    )(*args)


def _rms_prologue(h_ref, x_ref, g_ref):
    h_ref[...] = _rms(x_ref[...], g_ref[...]).astype(BF16)


def _whole(w, layer):
    return (w, layer, 0, w.shape[-1])


def _norm_matmul(x2d, gain, wsel, tm, tn, out_dtype, name, res=None):
    m, k = x2d.shape
    specs = [pl.BlockSpec((tm, k), lambda i, j: (i, 0)), pl.BlockSpec((1, k), lambda i, j: (0, 0))]
    return _fused_matmul(_rms_prologue, [x2d, gain.reshape(1, k)], specs, wsel, res, m, tm, tn, out_dtype, name)


def _cross_prologue(h_ref, q_ref, k_ref, v_ref):
    dh = CROSS_HEAD_DIM
    for h in range(CROSS_HEADS):
        sl = slice(h * dh, (h + 1) * dh)
        s = lax.dot_general(q_ref[:, sl], k_ref[:, sl], NT_DIMS, preferred_element_type=F32) * (dh ** -0.5)
        m = jnp.max(s, axis=-1, keepdims=True)
        p = jnp.exp(s - m)
        l = jnp.sum(p, axis=-1, keepdims=True)
        o = jnp.dot(p.astype(BF16), v_ref[:, sl], preferred_element_type=F32) * (1.0 / l)
        h_ref[:, sl] = o.astype(BF16)


def _branch_prologue(h_ref, na_ref, hy_ref, ret_ref, g_ref):
    def nrm(p):
        return p * lax.rsqrt(jnp.mean(p * p, axis=-1, keepdims=True) + RMS_EPS)

    y = jnp.concatenate([nrm(r[...].astype(F32)) for r in (na_ref, hy_ref, ret_ref)], axis=-1)
    h_ref[...] = (y * g_ref[...]).astype(BF16)


def _branch_out_cross_q_kernel(na_ref, hy_ref, ret_ref, g_ref, wo_ref, res_ref, gq_ref, wq_ref, x_ref, q_ref,
                               h_ref):
    _branch_prologue(h_ref, na_ref, hy_ref, ret_ref, g_ref)
    x = res_ref[...] + jnp.dot(h_ref[...], wo_ref[...], preferred_element_type=F32)
    x_ref[...] = x
    hq = _rms(x, gq_ref[...]).astype(BF16)
    q_ref[...] = jnp.dot(hq, wq_ref[...], preferred_element_type=F32).astype(BF16)


def _branch_out_cross_q(y_na, y_hy, y_ret, gain, w_out, res, gain_q, w_cq, layer, tm):
    m, d = res.shape
    rows = lambda width: pl.BlockSpec((tm, width), lambda i: (i, 0))
    vec = pl.BlockSpec((1, d), lambda i: (0, 0))
    square = pl.BlockSpec((None, d, d), lambda i: (layer, 0, 0))
    return pl.pallas_call(
        _branch_out_cross_q_kernel,
        grid=(m // tm,),
        in_specs=[rows(NA_W), rows(HY_W), rows(RET_W), vec, square, rows(d), vec, square],
        out_specs=[rows(d), rows(d)],
        out_shape=[jax.ShapeDtypeStruct((m, d), F32), jax.ShapeDtypeStruct((m, d), BF16)],
        scratch_shapes=[pltpu.VMEM((tm, d), BF16)],
        compiler_params=_params("parallel"),
        name="branch_out_cross_q",
    )(y_na, y_hy, y_ret, gain.reshape(1, d), w_out, res, gain_q.reshape(1, d), w_cq)


NA_PAIR = 2
NA_BAND = NA_KR + NA_PAIR


def _na_band_base(r, rows):
    return np.clip(r - NA_KR // 2, 0, rows - NA_BAND)


@functools.lru_cache(maxsize=None)
def _na_variants(rows):
    assert rows % NA_PAIR == 0 and rows >= NA_BAND + 2
    n_var = NA_KR // 2 + 1
    dr = np.full((n_var, NA_BAND, NA_PAIR), -2, np.int64)
    for r in range(0, rows, NA_PAIR):
        base = int(_na_band_base(r, rows))
        v = (r - base) // 2
        for j in range(NA_PAIR):
            rs = int(np.clip(r + j - NA_KR // 2, 0, rows - NA_KR))
            for i in range(NA_BAND):
                val = base + i - (r + j) + (NA_KR - 1) if rs <= base + i < rs + NA_KR else -1
                assert dr[v, i, j] in (-2, val)
                dr[v, i, j] = val
    assert (dr > -2).all()
    return dr


def _na_bias_table(rpb, rows):
    c = np.arange(GRID_W)
    col_start = np.clip(c - NA_KC // 2, 0, GRID_W - NA_KC)
    col_in = (c[None, :] >= col_start[:, None]) & (c[None, :] < col_start[:, None] + NA_KC)
    dc = np.clip(c[None, :] - c[:, None] + (NA_KC - 1), 0, 2 * NA_KC - 2)
    onehot = (dc.T.reshape(-1)[None, :] == np.arange(2 * NA_KC - 1)[:, None]).astype(np.float32)
    cols = jnp.einsum("hab,bn->han", rpb.astype(F32), jnp.asarray(onehot), precision=lax.Precision.HIGHEST)
    cols = cols.reshape(NA_HEADS, 2 * NA_KR - 1, GRID_W, GRID_W)
    cols = jnp.where(col_in.T[None, None], cols, MASK_VALUE)
    dr = _na_variants(rows)
    n_var = dr.shape[0]
    return pl.pallas_call(
        functools.partial(_na_table_kernel, dr),
        grid=(NA_HEADS,),
        in_specs=[pl.BlockSpec((1, 2 * NA_KR - 1, GRID_W, GRID_W), lambda h: (h, 0, 0, 0))],
        out_specs=pl.BlockSpec((1, n_var, NA_BAND * GRID_W, NA_PAIR * GRID_W), lambda h: (h, 0, 0, 0)),
        out_shape=jax.ShapeDtypeStruct((NA_HEADS, n_var, NA_BAND * GRID_W, NA_PAIR * GRID_W), F32),
        compiler_params=_params("parallel"),
        name="na_bias_table",
    )(cols)


def _na_table_kernel(dr, cols_ref, o_ref):
    masked = jnp.full((GRID_W, GRID_W), MASK_VALUE, F32)
    for v in range(dr.shape[0]):
        for i in range(NA_BAND):
            blocks = [cols_ref[0, int(dr[v, i, j])] if dr[v, i, j] >= 0 else masked for j in range(NA_PAIR)]
            o_ref[0, v, i * GRID_W:(i + 1) * GRID_W, :] = jnp.concatenate(blocks, axis=-1)


def _na_kernel(rows, q_ref, k_ref, v_ref, bias_ref, o_ref, vt_ref, s0_ref, s1_ref, p0_ref, p1_ref, l0_ref,
               l1_ref):
    dh = NA_HEAD_DIM
    nq = NA_PAIR * GRID_W
    nk = NA_BAND * GRID_W
    n_chunk = nk // LANES
    lane = lax.broadcasted_iota(I32, (nq, 2 * dh), 1)

    for ch in range(vt_ref.shape[0]):
        vt_ref[ch] = v_ref[0, ch * LANES:(ch + 1) * LANES, :].astype(F32).T.astype(BF16)

    n_steps = rows // NA_PAIR

    def band_base(p):
        return jnp.clip(p * NA_PAIR - NA_KR // 2, 0, rows - NA_BAND)


    def scores(p, s_out):
        r = p * NA_PAIR
        base = band_base(p)
        variant = (r - base) // 2
        q = q_ref[0, pl.ds(pl.multiple_of(r * GRID_W, nq), nq), :] * (dh ** -0.5)
        kb = k_ref[0, pl.ds(pl.multiple_of(base * GRID_W, LANES), nk), :]
        zero = jnp.zeros_like(q)
        q2 = jnp.concatenate([jnp.where(lane < dh, q, zero), jnp.where(lane >= dh, q, zero)], axis=0)
        st2 = lax.dot_general(kb, q2, NT_DIMS, preferred_element_type=F32)
        for hh in range(2):
            s_out[:, hh * nq:(hh + 1) * nq] = st2[:, hh * nq:(hh + 1) * nq] + bias_ref[hh, variant]

    def softmax(s_in, p_out, l_out):
        st = s_in[...]
        pt = jnp.exp(st - jnp.max(st, axis=0, keepdims=True))
        p_out[...] = pt.astype(BF16)
        l_out[...] = 1.0 / jnp.sum(pt, axis=0, keepdims=True)

    def values(p, p_in, l_in):
        r = p * NA_PAIR
        c0 = band_base(p) // 2
        vt = jnp.concatenate([vt_ref[c0 + i] for i in range(n_chunk)], axis=1)
        pt = p_in[...]
        linv = l_in[...]
        outs = []
        for hh in range(2):
            cols = slice(hh * nq, (hh + 1) * nq)
            ot = jnp.dot(vt[hh * dh:(hh + 1) * dh, :], pt[:, cols], preferred_element_type=F32)
            outs.append(ot * linv[:, cols])
        o_ref[0, pl.ds(pl.multiple_of(r * GRID_W, nq), nq), :] = jnp.concatenate(outs, axis=0).T.astype(BF16)

    s_slots = (s0_ref, s1_ref)
    p_slots = (p0_ref, p1_ref)
    l_slots = (l0_ref, l1_ref)

    def step(t, parity, do_scores=True, do_softmax=True, do_values=True):
        a, b = parity, 1 - parity
        t = jnp.asarray(t, I32)
        if do_scores:
            scores(t, s_slots[a])
        if do_softmax:
            softmax(s_slots[b], p_slots[b], l_slots[b])
        if do_values:
            values(t - 2, p_slots[a], l_slots[a])

    assert n_steps % 2 == 0 and n_steps >= 4
    step(0, 0, do_softmax=False, do_values=False)
    step(1, 1, do_values=False)

    def steady(i, carry):
        t = 2 + 2 * i
        step(t, 0)
        step(t + 1, 1)
        return carry

    lax.fori_loop(0, (n_steps - 2) // 2, steady, 0, unroll=True)
    step(n_steps, 0, do_scores=False)
    step(n_steps + 1, 1, do_scores=False, do_softmax=False)


def _neighbourhood_attention(proj_na, bias_tbl, batch, seq):
    rows = seq // GRID_W
    n_pairs = NA_HEADS // 2
    blk = (1, seq, 2 * NA_HEAD_DIM)
    n_var, nk, nq = bias_tbl.shape[1:]
    return pl.pallas_call(
        functools.partial(_na_kernel, rows),
        grid=(batch, n_pairs),
        in_specs=[
            pl.BlockSpec(blk, lambda b, h: (b, 0, h)),
            pl.BlockSpec(blk, lambda b, h: (b, 0, n_pairs + h)),
            pl.BlockSpec(blk, lambda b, h: (b, 0, 2 * n_pairs + h)),
            pl.BlockSpec((2, n_var, nk, nq), lambda b, h: (h, 0, 0, 0)),
        ],
        out_specs=pl.BlockSpec(blk, lambda b, h: (b, 0, h)),
        out_shape=jax.ShapeDtypeStruct((batch, seq, NA_W), BF16),
        scratch_shapes=[pltpu.VMEM((seq // LANES, 2 * NA_HEAD_DIM, LANES), BF16),
                        pltpu.VMEM((nk, 2 * nq), F32), pltpu.VMEM((nk, 2 * nq), F32),
                        pltpu.VMEM((nk, 2 * nq), BF16), pltpu.VMEM((nk, 2 * nq), BF16),
                        pltpu.VMEM((1, 2 * nq), F32), pltpu.VMEM((1, 2 * nq), F32)],
        compiler_params=_params("parallel", "arbitrary"),
        name="neighbourhood_attention",
    )(proj_na, proj_na, proj_na, bias_tbl)


@functools.lru_cache(maxsize=None)
def _dft_factors(seq):
    n = 2 * seq
    t = np.arange(seq, dtype=np.int64)
    f1 = np.arange(seq // 64, dtype=np.int64)
    f0 = np.arange(64, dtype=np.int64)
    a = 2.0 * np.pi * ((64 * f1[:, None] * t[None, :]) % n).astype(np.float64) / n
    b = 2.0 * np.pi * ((f0[:, None] * t[None, :]) % n).astype(np.float64) / n
    ny = np.where(t % 2 == 0, 1.0, -1.0)
    return tuple(np.asarray(v, np.float32) for v in (np.cos(a), np.sin(a), np.cos(b), np.sin(b), ny))


def _dft_table_kernel(u_ref, v_ref, cb_ref, sb_ref, fwd_ref, inv_ref):
    sin_half = pl.program_id(0) == 1
    first = pl.program_id(1) == 0
    n_fine, seq = cb_ref.shape
    row = lax.broadcasted_iota(I32, (n_fine, seq), 0)
    col = lax.broadcasted_iota(I32, (n_fine, seq), 1)
    for c in range(u_ref.shape[1]):
        tile = u_ref[0, c:c + 1, :] * cb_ref[...] + v_ref[0, c:c + 1, :] * sb_ref[...]
        rows = slice(c * n_fine, (c + 1) * n_fine)
        fwd = tile
        if c == 0:
            fwd_nyq = jnp.logical_and(jnp.logical_and(sin_half, first), row == 0)
            fwd = jnp.where(fwd_nyq, (1 - 2 * (col & 1)).astype(F32), tile)
        fwd_ref[0, rows, :] = fwd.astype(BF16)
        inv_nyq = jnp.logical_and(sin_half, col == 0)
        inv_ref[rows, :] = jnp.where(inv_nyq, (1 - 2 * (row & 1)).astype(F32), tile).astype(BF16)


def _dft_tables(seq):
    ca, sa, cb, sb, _ = _dft_factors(seq)
    u = np.stack([ca, sa])
    v = np.stack([-sa, ca])
    n_coarse = 8
    rows = 64 * n_coarse
    coarse_spec = pl.BlockSpec((1, n_coarse, seq), lambda h, i: (h, i, 0))
    fine_spec = pl.BlockSpec((64, seq), lambda h, i: (0, 0))
    return pl.pallas_call(
        _dft_table_kernel,
        grid=(2, seq // rows),
        in_specs=[coarse_spec, coarse_spec, fine_spec, fine_spec],
        out_specs=[pl.BlockSpec((1, rows, seq), lambda h, i: (h, i, 0)),
                   pl.BlockSpec((rows, seq), lambda h, i: (i, h))],
        out_shape=[jax.ShapeDtypeStruct((2, seq, seq), BF16), jax.ShapeDtypeStruct((seq, 2 * seq), BF16)],
        compiler_params=_params("parallel", "arbitrary"),
        name="dft_tables",
    )(jnp.asarray(u), jnp.asarray(v), jnp.asarray(cb), jnp.asarray(sb))


@functools.lru_cache(maxsize=None)
def _hyena_consts(seq):
    t = np.arange(seq, dtype=np.float64)
    t01 = t / (seq - 1)
    bands = np.linspace(1e-4, HY_BANDS - 1, HY_BANDS)
    ang = (2.0 * math.pi) * (t[:, None] / seq) * bands[None, :]
    feats = np.concatenate([t01[:, None], np.cos(ang), -np.sin(ang)], axis=-1)
    feats_p = np.zeros((seq, LANES), np.float32)
    feats_p[:, :HY_POS_DIM] = feats
    min_decay = math.log(1e-2) / 1.5
    max_decay = math.log(1e-2) / 0.3
    deltas = np.abs(np.linspace(min_decay, max_decay, HY_W))
    window = np.exp(-t01[:, None] * deltas[None, :]).astype(np.float32)
    return feats_p, window


def _filter_kernel(feats_ref, w1_ref, b1_ref, w2_ref, b2_ref, freq_ref, w3f_ref, w3b_ref, win_ref, sum_ref,
                   diff_ref, hid_ref):
    hp = lax.Precision.HIGHEST

    @pl.when(pl.program_id(0) == 0)
    def _():
        f = freq_ref[...]
        h1 = jnp.sin(f * (jnp.dot(feats_ref[...], w1_ref[...], precision=hp, preferred_element_type=F32)
                          + b1_ref[...]))
        hid_ref[...] = jnp.sin(f * (jnp.dot(h1, w2_ref[...], precision=hp, preferred_element_type=F32)
                                    + b2_ref[...]))

    hid = hid_ref[...]
    win = win_ref[...]
    fwd = jnp.dot(hid, w3f_ref[...], precision=hp, preferred_element_type=F32) * win
    bwd = jnp.dot(hid, w3b_ref[...], precision=hp, preferred_element_type=F32) * win
    bwd = jnp.where(lax.broadcasted_iota(I32, bwd.shape, 0) == 0, 0.0, bwd)
    sum_ref[...] = (fwd + bwd).astype(BF16)
    diff_ref[...] = (fwd - bwd).astype(BF16)


def _hyena_filters_time(w1, b1, w2, b2, w3, freq, seq):
    feats, window = _hyena_consts(seq)
    w1p = jnp.zeros((LANES, HY_FILT_FF), F32).at[:HY_POS_DIM].set(w1)
    full = lambda shape: pl.BlockSpec(shape, lambda o: (0,) * len(shape))
    out_spec = pl.BlockSpec((seq, HY_W), lambda o: (0, o))
    out_shape = jax.ShapeDtypeStruct((seq, HY_ORDER * HY_W), BF16)
    return pl.pallas_call(
        _filter_kernel,
        grid=(HY_ORDER,),
        in_specs=[
            full((seq, LANES)), full((LANES, HY_FILT_FF)), full((1, HY_FILT_FF)),
            full((HY_FILT_FF, HY_FILT_FF)), full((1, HY_FILT_FF)), full((1, HY_FILT_FF)),
            pl.BlockSpec((HY_FILT_FF, HY_W), lambda o: (0, 2 * o)),
            pl.BlockSpec((HY_FILT_FF, HY_W), lambda o: (0, 2 * o + 1)),
            full((seq, HY_W)),
        ],
        out_specs=[out_spec, out_spec],
        out_shape=[out_shape, out_shape],
        scratch_shapes=[pltpu.VMEM((seq, HY_FILT_FF), F32)],
        compiler_params=_params("arbitrary"),
        name="hyena_filter_mlp",
    )(jnp.asarray(feats), w1p, b1.reshape(1, -1), w2, b2.reshape(1, -1), freq.reshape(1, -1), w3, w3,
      jnp.asarray(window))


def _filter_dft_kernel(f_ref, sum_ref, diff_ref, o_ref):
    o_ref[0] = jnp.dot(f_ref[0], sum_ref[...], preferred_element_type=F32)
    o_ref[1] = jnp.dot(f_ref[1], diff_ref[...], preferred_element_type=F32)

    @pl.when(pl.program_id(0) == 0)
    def _():
        top = jnp.dot(f_ref[1, 0:8, :], sum_ref[...], preferred_element_type=F32)
        row = lax.broadcasted_iota(I32, top.shape, 0)
        o_ref[1, 0:8, :] = jnp.where(row == 0, top, o_ref[1, 0:8, :])


def _filter_dft(dft_fwd, filt_sum, filt_diff, seq, fb):
    return pl.pallas_call(
        _filter_dft_kernel,
        grid=(seq // fb, HY_ORDER),
        in_specs=[
            pl.BlockSpec((2, fb, seq), lambda f, o: (0, f, 0)),
            pl.BlockSpec((seq, HY_W), lambda f, o: (0, o)),
            pl.BlockSpec((seq, HY_W), lambda f, o: (0, o)),
        ],
        out_specs=pl.BlockSpec((2, fb, HY_W), lambda f, o: (0, f, o)),
        out_shape=jax.ShapeDtypeStruct((2, seq, HY_ORDER * HY_W), F32),
        compiler_params=_params("parallel", "arbitrary"),
        name="hyena_filter_dft",
    )(dft_fwd, filt_sum, filt_diff)


def _short_conv_kernel(p_ref, w_ref, b_ref, o_ref):
    p = p_ref[0].astype(F32)
    seq = p.shape[0]
    row = lax.broadcasted_iota(I32, p.shape, 0)
    prev = jnp.where(row == 0, 0.0, pltpu.roll(p, 1, 0))
    nxt = jnp.where(row == seq - 1, 0.0, pltpu.roll(p, seq - 1, 0))
    w = w_ref[...]
    o_ref[0] = prev * w[0:1] + p * w[1:2] + nxt * w[2:3] + b_ref[...]


def _short_conv(proj, col0, conv_w, conv_b, batch, seq, tc):
    n_cols = conv_w.shape[-1]
    assert col0 % tc == 0 and n_cols % tc == 0
    cb0 = col0 // tc
    return pl.pallas_call(
        _short_conv_kernel,
        grid=(batch, n_cols // tc),
        in_specs=[
            pl.BlockSpec((1, seq, tc), lambda b, c: (b, 0, cb0 + c)),
            pl.BlockSpec((3, tc), lambda b, c: (0, c)),
            pl.BlockSpec((1, tc), lambda b, c: (0, c)),
        ],
        out_specs=pl.BlockSpec((1, seq, tc), lambda b, c: (b, 0, c)),
        out_shape=jax.ShapeDtypeStruct((batch, seq, n_cols), F32),
        compiler_params=_params("parallel", "arbitrary"),
        name="hyena_short_conv",
    )(proj, conv_w, conv_b.reshape(1, -1))


def _spectrum_kernel(n_fft, f_ref, z_ref, k_ref, o_ref):
    z = z_ref[0].astype(BF16)
    xr = jnp.dot(f_ref[0], z, preferred_element_type=F32)
    xs = jnp.dot(f_ref[1], z, preferred_element_type=F32)
    kr = k_ref[0]
    ks = k_ref[1]
    row = lax.broadcasted_iota(I32, xr.shape, 0)
    edge = jnp.logical_and(row == 0, pl.program_id(0) == 0)
    yr = jnp.where(edge, xr * kr * (1.0 / n_fft), (xr * kr - xs * ks) * (2.0 / n_fft))
    ys = jnp.where(edge, xs * ks * (1.0 / n_fft), (xr * ks + xs * kr) * (2.0 / n_fft))
    o_ref[0, 0] = yr.astype(BF16)
    o_ref[0, 1] = ys.astype(BF16)


def _spectrum_product(dft_fwd, z_arr, z_col, kfreq, order, batch, seq, fb):
    return pl.pallas_call(
        functools.partial(_spectrum_kernel, 2 * seq),
        grid=(seq // fb, batch),
        in_specs=[
            pl.BlockSpec((2, fb, seq), lambda f, b: (0, f, 0)),
            pl.BlockSpec((1, seq, HY_W), lambda f, b: (b, 0, z_col)),
            pl.BlockSpec((2, fb, HY_W), lambda f, b: (0, f, order)),
        ],
        out_specs=pl.BlockSpec((1, 2, fb, HY_W), lambda f, b: (b, 0, f, 0)),
        out_shape=jax.ShapeDtypeStruct((batch, 2, seq, HY_W), BF16),
        compiler_params=_params("parallel", "arbitrary"),
        name="hyena_spectrum",
    )(dft_fwd, z_arr, kfreq)


def _inverse_kernel(ft_ref, y_ref, gate_ref, z_ref, d_ref, o_ref):
    conv = jnp.dot(ft_ref[...], y_ref[0], preferred_element_type=F32)
    o_ref[0] = gate_ref[0] * (conv + d_ref[...] * z_ref[0])


def _inverse_gate(dft_inv, y, gate_arr, gate_col, z_arr, z_col, skip_row, batch, seq, tb):
    y2 = y.reshape(batch, 2 * seq, HY_W)
    return pl.pallas_call(
        _inverse_kernel,
        grid=(seq // tb, batch),
        in_specs=[
            pl.BlockSpec((tb, 2 * seq), lambda t, b: (t, 0)),
            pl.BlockSpec((1, 2 * seq, HY_W), lambda t, b: (b, 0, 0)),
            pl.BlockSpec((1, tb, HY_W), lambda t, b: (b, t, gate_col)),
            pl.BlockSpec((1, tb, HY_W), lambda t, b: (b, t, z_col)),
            pl.BlockSpec((1, HY_W), lambda t, b: (0, 0)),
        ],
        out_specs=pl.BlockSpec((1, tb, HY_W), lambda t, b: (b, t, 0)),
        out_shape=jax.ShapeDtypeStruct((batch, seq, HY_W), F32),
        compiler_params=_params("parallel", "arbitrary"),
        name="hyena_inverse_gate",
    )(dft_inv, y2, gate_arr, z_arr, skip_row.reshape(1, HY_W))


def _hyena_mixer(proj, col0, conv_w, conv_b, w1, b1, w2, b2, w3, freq, skip_d, dft_fwd, dft_inv, batch, seq):
    s = _short_conv(proj, col0, conv_w, conv_b, batch, seq, 768)
    filt_sum, filt_diff = _hyena_filters_time(w1, b1, w2, b2, w3, freq, seq)
    kfreq = _filter_dft(dft_fwd, filt_sum, filt_diff, seq, 512)
    z_arr, z_col = s, 2
    for o in range(HY_ORDER):
        y = _spectrum_product(dft_fwd, z_arr, z_col, kfreq, o, batch, seq, 1024)
        z_arr = _inverse_gate(dft_inv, y, s, o, z_arr, z_col, skip_d[o], batch, seq, 1024)
        z_col = 0
    return z_arr


@functools.lru_cache(maxsize=None)
def _retention_consts(seq):
    c = RET_CHUNK
    half = RET_HEAD_DIM // 2
    inv = 1.0 / (10000.0 ** np.linspace(0.0, 1.0, half))
    ang = np.arange(seq, dtype=np.float64)[:, None] * inv[None, :]
    cos2 = np.concatenate([np.cos(ang), np.cos(ang)], axis=-1).astype(np.float32)
    sin2 = np.concatenate([-np.sin(ang), np.sin(ang)], axis=-1).astype(np.float32)
    hidx = np.arange(RET_HEADS, dtype=np.float64)
    lg_f = np.log1p(-np.exp2(-5.0 - hidx))[:, None, None]
    lg_b = np.log1p(-np.exp2(-5.5 - hidx))[:, None, None]
    i = np.arange(c, dtype=np.float64)
    diff = i[:, None] - i[None, :]
    ones = np.ones((1, c, c))
    dec = np.where(diff >= 0, np.exp(lg_f * np.maximum(diff, 0.0)), np.exp(lg_b * np.maximum(-diff, 0.0)))
    rowv = lambda v: v[:, :, None] * ones
    tab = np.stack([
        dec,
        rowv(np.exp(lg_f[:, :, 0] * (i + 1.0)[None, :])),
        rowv(np.exp(lg_f[:, :, 0] * (c - 1.0 - i)[None, :])),
        rowv(np.exp(lg_b[:, :, 0] * (c - i)[None, :])),
        rowv(np.exp(lg_b[:, :, 0] * i[None, :])),
        np.exp(lg_f * c) * ones,
        np.exp(lg_b * c) * ones,
    ], axis=1).astype(np.float32)
    return cos2, sin2, tab


def _retention_kernel(q_ref, k_ref, v_ref, g_ref, cos_ref, sin_ref, tab_ref, o_ref, qs_ref, ks_ref, kvf_ref,
                      kvb_ref, a0_ref, a1_ref, y0_ref, y1_ref):
    c = RET_CHUNK
    d = RET_HEAD_DIM
    seq = q_ref.shape[1]
    n_chunks = seq // c
    cos = cos_ref[...]
    sin = sin_ref[...]
    q = q_ref[0].astype(F32)
    k = k_ref[0].astype(F32)
    qs_ref[...] = (q * cos + pltpu.roll(q, d // 2, 1) * sin) * (d ** -0.5)
    ks_ref[...] = k * cos + pltpu.roll(k, d // 2, 1) * sin

    def mm(a, b):
        return jnp.dot(a.astype(BF16), b.astype(BF16), preferred_element_type=F32)

    def chunk(n):
        return pl.ds(pl.multiple_of(n * c, c), c)

    def kv_body(n, carry):
        kc = ks_ref[chunk(n), :]
        vc = v_ref[0, chunk(n), :]
        kvf_ref[n] = mm((kc * tab_ref[0, 2]).T, vc)
        kvb_ref[n] = mm((kc * tab_ref[0, 4]).T, vc)
        return carry

    lax.fori_loop(0, n_chunks, kv_body, 0, unroll=True)

    def scan_fwd(n, state):
        kv = kvf_ref[n]
        kvf_ref[n] = state
        return tab_ref[0, 5] * state + kv

    lax.fori_loop(0, n_chunks, scan_fwd, jnp.zeros((d, d), F32))

    def scan_bwd(m, state):
        n = n_chunks - 1 - m
        kv = kvb_ref[n]
        kvb_ref[n] = state
        return tab_ref[0, 6] * state + kv

    lax.fori_loop(0, n_chunks, scan_bwd, jnp.zeros((d, d), F32))

    def in_chunk(n, a_out):
        qc = qs_ref[chunk(n), :].astype(BF16)
        kc = ks_ref[chunk(n), :].astype(BF16)
        a = lax.dot_general(qc, kc, NT_DIMS, preferred_element_type=F32) * tab_ref[0, 0]
        a_out[...] = a.astype(BF16)

    def mix(n, a_in, y_out):
        qc = qs_ref[chunk(n), :]
        y_out[...] = (jnp.dot(a_in[...], v_ref[0, chunk(n), :], preferred_element_type=F32)
                      + mm(qc * tab_ref[0, 1], kvf_ref[n]) + mm(qc * tab_ref[0, 3], kvb_ref[n]))

    def finish(n, y_in):
        y = y_in[...]
        mu = jnp.mean(y, axis=-1, keepdims=True)
        yc = y - mu
        var = jnp.mean(yc * yc, axis=-1, keepdims=True)
        g = g_ref[0, chunk(n), :].astype(F32)
        o_ref[0, chunk(n), :] = (yc * lax.rsqrt(var + GN_EPS) * (g * jax.nn.sigmoid(g))).astype(BF16)

    a_slots = (a0_ref, a1_ref)
    y_slots = (y0_ref, y1_ref)

    def step(t, parity, do_a=True, do_mix=True, do_finish=True):
        t = jnp.asarray(t, I32)
        if do_a:
            in_chunk(t, a_slots[parity])
        if do_mix:
            mix(t - 1, a_slots[1 - parity], y_slots[1 - parity])
        if do_finish:
            finish(t - 2, y_slots[parity])

    assert n_chunks % 2 == 0 and n_chunks >= 4
    step(0, 0, do_mix=False, do_finish=False)
    step(1, 1, do_finish=False)

    def steady(i, carry):
        step(2 + 2 * i, 0)
        step(3 + 2 * i, 1)
        return carry

    lax.fori_loop(0, (n_chunks - 2) // 2, steady, 0, unroll=True)
    step(n_chunks, 0, do_a=False)
    step(n_chunks + 1, 1, do_a=False, do_mix=False)


def _retention_mixer(proj, col0, batch, seq):
    cos2, sin2, tab = _retention_consts(seq)
    blk = (1, seq, RET_HEAD_DIM)
    h_ = RET_HEADS
    assert col0 % RET_HEAD_DIM == 0
    c0 = col0 // RET_HEAD_DIM
    return pl.pallas_call(
        _retention_kernel,
        grid=(batch, RET_HEADS),
        in_specs=[
            pl.BlockSpec(blk, lambda b, h: (b, 0, c0 + h)),
            pl.BlockSpec(blk, lambda b, h: (b, 0, c0 + h_ + h)),
            pl.BlockSpec(blk, lambda b, h: (b, 0, c0 + 2 * h_ + h)),
            pl.BlockSpec(blk, lambda b, h: (b, 0, c0 + 3 * h_ + h)),
            pl.BlockSpec((seq, RET_HEAD_DIM), lambda b, h: (0, 0)),
            pl.BlockSpec((seq, RET_HEAD_DIM), lambda b, h: (0, 0)),
            pl.BlockSpec((1, 7, RET_CHUNK, RET_CHUNK), lambda b, h: (h, 0, 0, 0)),
        ],
        out_specs=pl.BlockSpec(blk, lambda b, h: (b, 0, h)),
        out_shape=jax.ShapeDtypeStruct((batch, seq, RET_W), BF16),
        scratch_shapes=[pltpu.VMEM((seq, RET_HEAD_DIM), F32), pltpu.VMEM((seq, RET_HEAD_DIM), F32),
                        pltpu.VMEM((seq // RET_CHUNK, RET_HEAD_DIM, RET_HEAD_DIM), F32),
                        pltpu.VMEM((seq // RET_CHUNK, RET_HEAD_DIM, RET_HEAD_DIM), F32),
                        pltpu.VMEM((RET_CHUNK, RET_CHUNK), BF16), pltpu.VMEM((RET_CHUNK, RET_CHUNK), BF16),
                        pltpu.VMEM((RET_CHUNK, RET_HEAD_DIM), F32), pltpu.VMEM((RET_CHUNK, RET_HEAD_DIM), F32)],
        compiler_params=_params("parallel", "arbitrary"),
        name="retention",
    )(proj, proj, proj, proj, jnp.asarray(cos2), jnp.asarray(sin2), jnp.asarray(tab))


def _cross_out_router_kernel(q_ref, k_ref, v_ref, w_ref, res_ref, g_ref, wr_ref, x_ref, hm_ref, aff_ref, h_ref):
    n_e = aff_ref.shape[1]
    _cross_prologue(h_ref, q_ref, k_ref, v_ref)
    x = res_ref[...] + jnp.dot(h_ref[...], w_ref[...], preferred_element_type=F32)
    x_ref[...] = x
    h = _rms(x, g_ref[...])
    h_hi = h.astype(BF16)
    hm_ref[...] = h_hi
    h_lo = (h - h_hi.astype(F32)).astype(BF16)
    wr = wr_ref[...]
    w_hi = wr.astype(BF16)
    w_lo = (wr - w_hi.astype(F32)).astype(BF16)
    both = jnp.dot(h_hi, jnp.concatenate([w_hi, w_lo], axis=1), preferred_element_type=F32)
    logits = (both[:, :LANES] + both[:, LANES:] + jnp.dot(h_lo, w_hi, preferred_element_type=F32)).T[:n_e]
    m = jnp.max(logits, axis=0, keepdims=True)
    e = jnp.exp(logits - m)
    aff_ref[0] = e / jnp.sum(e, axis=0, keepdims=True)


def _cross_out_and_router(q, kv, w_co, layer, res, gain, w_router, batch, seq, n_mem, tm):
    m, d = q.shape
    per_b = seq // tm
    assert seq % tm == 0
    w_pad = jnp.zeros((d, LANES), F32).at[:, :N_EXPERTS].set(w_router)
    row_spec = pl.BlockSpec((tm, d), lambda i: (i, 0))
    return pl.pallas_call(
        _cross_out_router_kernel,
        grid=(m // tm,),
        in_specs=[
            row_spec,
            pl.BlockSpec((n_mem, d), lambda i: (i // per_b, 2 * layer)),
            pl.BlockSpec((n_mem, d), lambda i: (i // per_b, 2 * layer + 1)),
            pl.BlockSpec((None, d, d), lambda i: (layer, 0, 0)),
            row_spec,
            pl.BlockSpec((1, d), lambda i: (0, 0)),
            pl.BlockSpec((d, LANES), lambda i: (0, 0)),
        ],
        out_specs=[
            row_spec,
            row_spec,
            pl.BlockSpec((1, N_EXPERTS, tm), lambda i: (i // per_b, 0, i % per_b)),
        ],
        out_shape=[
            jax.ShapeDtypeStruct((m, d), F32),
            jax.ShapeDtypeStruct((m, d), BF16),
            jax.ShapeDtypeStruct((batch, N_EXPERTS, seq), F32),
        ],
        scratch_shapes=[pltpu.VMEM((tm, d), BF16)],
        compiler_params=_params("parallel"),
        name="cross_out_router",
    )(q, kv, kv, w_co, res, gain.reshape(1, d), w_pad)


def _sort_descending(x):
    rows, n = x.shape
    lanes = LANES
    n_chunks = n // lanes
    chunks = [x[:, c * lanes:(c + 1) * lanes] for c in range(n_chunks)]
    lane = lax.broadcasted_iota(I32, (rows, lanes), 1)
    k = 2
    while k <= n:
        j = k // 2
        while j >= 1:
            nxt = []
            for c, xc in enumerate(chunks):
                if j >= lanes:
                    lo = (c & (j // lanes)) == 0
                    partner = chunks[c ^ (j // lanes)]
                else:
                    lo = (lane & j) == 0
                    partner = jnp.where(lo, pltpu.roll(xc, lanes - j, 1), pltpu.roll(xc, j, 1))
                desc = ((c * lanes) & k) == 0 if k >= lanes else (lane & k) == 0
                hi, lw = jnp.maximum(xc, partner), jnp.minimum(xc, partner)
                if j >= lanes and k >= lanes:
                    nxt.append(hi if lo == desc else lw)
                else:
                    nxt.append(jnp.where(lo == desc, hi, lw))
            chunks = nxt
            j //= 2
        k *= 2
    return jnp.concatenate(chunks, axis=1)


def _topk_kernel(cap, n_e, aff_ref, slot_ref, wsel_ref, slot_t_ref, thr_ref):
    n_rows, n_tok = aff_ref.shape

    def thr_body(i, carry):
        rows8 = pl.ds(pl.multiple_of(i * 8, 8), 8)
        thr_ref[rows8, :] = _sort_descending(aff_ref[rows8, :])[:, cap - 1:cap]
        return carry

    lax.fori_loop(0, n_rows // 8, thr_body, 0)
    a = aff_ref[...]
    thr = thr_ref[...]
    gt = a > thr
    eq = a == thr
    need = cap - jnp.sum(gt.astype(I32), axis=1, keepdims=True)
    upper = (lax.broadcasted_iota(I32, (n_tok, n_tok), 0) < lax.broadcasted_iota(I32, (n_tok, n_tok), 1))
    upper = upper.astype(BF16)
    eq_rank = jnp.dot(eq.astype(BF16), upper, preferred_element_type=F32)
    sel = jnp.logical_or(gt, jnp.logical_and(eq, eq_rank < need.astype(F32)))
    rank = jnp.dot(sel.astype(BF16), upper, preferred_element_type=F32)
    slot = jnp.where(sel, rank, -1.0)
    slot_ref[...] = slot.astype(I32)
    wsel_ref[...] = jnp.where(sel, a, 0.0)
    pad = jnp.full((LANES - n_e, n_tok), -1.0, F32)
    for b in range(n_rows // n_e):
        slot_t_ref[b] = jnp.concatenate([slot[b * n_e:(b + 1) * n_e], pad], axis=0).T.astype(I32)


def _topk_select(aff2d, cap, n_e):
    rows, n_tok = aff2d.shape
    spec = pl.BlockSpec((rows, n_tok), lambda i: (0, 0))
    spec_t = pl.BlockSpec((rows // n_e, n_tok, LANES), lambda i: (0, 0, 0))
    return pl.pallas_call(
        functools.partial(_topk_kernel, cap, n_e),
        grid=(1,),
        in_specs=[spec],
        out_specs=[spec, spec, spec_t],
        out_shape=[jax.ShapeDtypeStruct((rows, n_tok), I32), jax.ShapeDtypeStruct((rows, n_tok), F32),
                   jax.ShapeDtypeStruct((rows // n_e, n_tok, LANES), I32)],
        scratch_shapes=[pltpu.VMEM((rows, 1), F32)],
        compiler_params=_params("arbitrary"),
        name="moe_topk_select",
    )(aff2d)


def _gather_kernel(cap, slot_ref, wsel_ref, hm_ref, xe_ref, gs_ref):
    n_tok = hm_ref.shape[0]
    onehot = slot_ref[0] == lax.broadcasted_iota(I32, (cap, n_tok), 0)
    xe_ref[0] = jnp.dot(onehot.astype(BF16), hm_ref[...], preferred_element_type=F32).astype(BF16)
    gs_ref[0] = jnp.sum(jnp.where(onehot, wsel_ref[0], 0.0), axis=1, keepdims=True)


def _gather_tokens(slot, wsel, hm, batch, seq, cap):
    rows = batch * N_EXPERTS
    row_spec = pl.BlockSpec((1, 1, seq), lambda b, e: (b * N_EXPERTS + e, 0, 0))
    return pl.pallas_call(
        functools.partial(_gather_kernel, cap),
        grid=(batch, N_EXPERTS),
        in_specs=[row_spec, row_spec, pl.BlockSpec((seq, D_MODEL), lambda b, e: (b, 0))],
        out_specs=[
            pl.BlockSpec((1, cap, D_MODEL), lambda b, e: (e, b, 0)),
            pl.BlockSpec((1, cap, 1), lambda b, e: (e, b, 0)),
        ],
        out_shape=[
            jax.ShapeDtypeStruct((N_EXPERTS, batch * cap, D_MODEL), BF16),
            jax.ShapeDtypeStruct((N_EXPERTS, batch * cap, 1), F32),
        ],
        compiler_params=_params("parallel", "arbitrary"),
        name="moe_gather",
    )(slot.reshape(rows, 1, seq), wsel.reshape(rows, 1, seq), hm)


def _expert_kernel(n_e, n_t, xe_ref, wg_ref, wu_ref, wd_ref, gs_ref, ye_ref, mid_ref):
    g = pl.program_id(0)
    s = pl.program_id(1)
    t = s // 2
    tf = wg_ref.shape[-1]

    @pl.when(jnp.logical_and(s % 2 == 0, g < n_e))
    def _():
        x = xe_ref[0]
        a = jnp.dot(x, wg_ref[...].astype(BF16), preferred_element_type=F32)
        u = jnp.dot(x, wu_ref[...].astype(BF16), preferred_element_type=F32)
        mid_ref[g % 2, t] = (a * jax.nn.sigmoid(a) * u).astype(BF16)

    @pl.when(jnp.logical_and(s % 2 == 1, g >= 1))
    def _():
        prev = (g - 1) % 2
        acc = jnp.dot(mid_ref[prev, 0], wd_ref[0:tf, :].astype(BF16), preferred_element_type=F32)
        for f in range(1, n_t):
            acc += jnp.dot(mid_ref[prev, f], wd_ref[f * tf:(f + 1) * tf, :].astype(BF16),
                           preferred_element_type=F32)
        ye_ref[0] = (acc * gs_ref[0]).astype(BF16)


def _expert_ffn(xe, gs, w_gate, w_up, w_down, layer, tf):
    n_e, rows, d = xe.shape
    ff = w_gate.shape[-1]
    n_t = ff // tf
    assert d // tf == n_t
    last = n_e - 1

    n_s = 2 * n_t

    def fill_tile(g, s, lead):
        v = jnp.minimum(g * n_s + s + lead, n_e * n_s - 1)
        return (layer, v // n_s, 0, (v % n_s) // 2)

    def emit_tile(g, s):
        v = jnp.maximum(g * n_s + s - 1 - n_s, 0)
        return (layer, v // n_s, 0, (v % n_s) // 2)

    return pl.pallas_call(
        functools.partial(_expert_kernel, n_e, n_t),
        grid=(n_e + 1, n_s),
        in_specs=[
            pl.BlockSpec((1, rows, d), lambda g, s: (jnp.minimum(g, last), 0, 0)),
            pl.BlockSpec((None, None, d, tf), functools.partial(fill_tile, lead=1)),
            pl.BlockSpec((None, None, d, tf), functools.partial(fill_tile, lead=0)),
            pl.BlockSpec((None, None, ff, tf), emit_tile),
            pl.BlockSpec((1, rows, 1), lambda g, s: (jnp.maximum(g - 1, 0), 0, 0)),
        ],
        out_specs=pl.BlockSpec((1, rows, tf), lambda g, s: emit_tile(g, s)[1:]),
        out_shape=jax.ShapeDtypeStruct((n_e, rows, d), BF16),
        scratch_shapes=[pltpu.VMEM((2, n_t, rows, tf), BF16)],
        compiler_params=_params("arbitrary", "arbitrary"),
        name="moe_expert_ffn",
    )(xe, w_gate, w_up, w_down, gs)


def _scatter_kernel(cap, slot_ref, ye_ref, x_ref, o_ref, onehot_ref):
    n_e = ye_ref.shape[0]
    n_tok = x_ref.shape[0]

    @pl.when(pl.program_id(1) == 0)
    def _():
        col = lax.broadcasted_iota(I32, (n_tok, cap), 1)
        slots = slot_ref[0]
        for e in range(n_e):
            onehot_ref[:, e * cap:(e + 1) * cap] = (slots[:, e:e + 1] == col).astype(BF16)

    ye = ye_ref[...].reshape(n_e * cap, ye_ref.shape[-1])
    o_ref[...] = x_ref[...] + jnp.dot(onehot_ref[...], ye, preferred_element_type=F32)


def _scatter_add(slot_t, ye, x2d, batch, seq, cap, tn):
    return pl.pallas_call(
        functools.partial(_scatter_kernel, cap),
        grid=(batch, D_MODEL // tn),
        in_specs=[
            pl.BlockSpec((1, seq, LANES), lambda b, n: (b, 0, 0)),
            pl.BlockSpec((N_EXPERTS, cap, tn), lambda b, n: (0, b, n)),
            pl.BlockSpec((seq, tn), lambda b, n: (b, n)),
        ],
        out_specs=pl.BlockSpec((seq, tn), lambda b, n: (b, n)),
        out_shape=jax.ShapeDtypeStruct(x2d.shape, F32),
        scratch_shapes=[pltpu.VMEM((seq, N_EXPERTS * cap), BF16)],
        compiler_params=_params("parallel", "arbitrary"),
        name="moe_scatter_add",
    )(slot_t, ye, x2d)


def _expert_choice_ffn(x2d, hm, aff, w_gate, w_up, w_down, layer, batch, seq):
    cap = EC_CAPACITY * seq // N_EXPERTS
    slot, wsel, slot_t = _topk_select(aff.reshape(batch * N_EXPERTS, seq), cap, N_EXPERTS)
    xe, gs = _gather_tokens(slot, wsel, hm, batch, seq, cap)
    ye = _expert_ffn(xe, gs, w_gate, w_up, w_down, layer, 512)
    return _scatter_add(slot_t, ye, x2d, batch, seq, cap, 512)


def _final_norm_kernel(x_ref, g_ref, o_ref):
    o_ref[...] = _rms(x_ref[...], g_ref[...])


def _final_norm(x2d, gain, tm):
    m, d = x2d.shape
    return pl.pallas_call(
        _final_norm_kernel,
        grid=(m // tm,),
        in_specs=[pl.BlockSpec((tm, d), lambda i: (i, 0)), pl.BlockSpec((1, d), lambda i: (0, 0))],
        out_specs=pl.BlockSpec((tm, d), lambda i: (i, 0)),
        out_shape=jax.ShapeDtypeStruct((m, d), F32),
        compiler_params=_params("parallel"),
        name="final_norm",
    )(x2d, gain.reshape(1, d))


def kernel(x, mem, norm_mix, w_in, na_rpb, hy_conv_w, hy_conv_b, hy_filt_w1, hy_filt_b1, hy_filt_w2, hy_filt_b2, hy_filt_w3, hy_sin_freq, hy_skip_d, branch_norm, w_out, norm_cross, mem_norm, w_cq, w_ckv, w_co, norm_moe, w_router, w_gate, w_up, w_down, final_norm):
    batch, seq, d = x.shape
    n_mem = mem.shape[1]
    depth = w_in.shape[0]
    m = batch * seq
    na_cols = 3 * NA_W
    hy_cols = 3 * HY_W
    rows = seq // GRID_W

    dft_fwd, dft_inv = _dft_tables(seq)
    x2d = x.reshape(m, d)
    mem2d = mem.reshape(batch * n_mem, d)

    w_out_b, w_cq_b, w_co_b = (w.astype(BF16) for w in (w_out, w_cq, w_co))
    p_in = w_in.shape[-1]
    kv = _norm_matmul(mem2d, mem_norm, (w_ckv, None, 0, w_ckv.shape[-1]), batch * n_mem, 1024, BF16,
                      "cross_kv_proj")

    for l in range(depth):
        proj = _norm_matmul(x2d, norm_mix[l], _whole(w_in, l), 1024, 768, BF16, "in_proj")
        proj = proj.reshape(batch, seq, p_in)

        y_na = _neighbourhood_attention(proj, _na_bias_table(na_rpb[l], rows), batch, seq)
        y_hy = _hyena_mixer(proj, na_cols, hy_conv_w[l], hy_conv_b[l], hy_filt_w1[l],
                            hy_filt_b1[l], hy_filt_w2[l], hy_filt_b2[l], hy_filt_w3[l], hy_sin_freq[l],
                            hy_skip_d[l], dft_fwd, dft_inv, batch, seq)
        y_ret = _retention_mixer(proj, na_cols + hy_cols, batch, seq)
        x2d, q = _branch_out_cross_q(y_na.reshape(m, NA_W), y_hy.reshape(m, HY_W), y_ret.reshape(m, RET_W),
                                     branch_norm[l], w_out_b, x2d, norm_cross[l], w_cq_b, l, 512)
        x2d, hm, aff = _cross_out_and_router(q, kv, w_co_b, l, x2d, norm_moe[l], w_router[l], batch, seq, n_mem,
                                             256)

        x2d = _expert_choice_ffn(x2d, hm, aff, w_gate, w_up, w_down, l, batch, seq)

    return _final_norm(x2d, final_norm, 512).reshape(batch, seq, d)
```

```python
import functools
import math

import numpy as np
import jax
import jax.numpy as jnp
from jax import lax
from jax.experimental import pallas as pl
from jax.experimental.pallas import tpu as pltpu

F32 = jnp.float32
BF16 = jnp.bfloat16
I32 = jnp.int32

D_MODEL = 2048
GRID_W = 64
NA_HEAD_DIM = 64
NA_W = 768
NA_HEADS = 12
NA_KR = 8
NA_KC = 16
HY_W = 512
HY_ORDER = 2
HY_BANDS = 8
HY_POS_DIM = 17
HY_FILT_FF = 64
RET_HEAD_DIM = 128
RET_W = 768
RET_HEADS = 6
RET_CHUNK = 128
CROSS_HEADS = 4
CROSS_HEAD_DIM = 512
N_EXPERTS = 16
EXPERT_FF = 2048
EC_CAPACITY = 2
RMS_EPS = 1e-6
GN_EPS = 1e-5

MASK_VALUE = -1e30
LANES = 128
VMEM_LIMIT_BYTES = 56 * 1024 * 1024

NT_DIMS = (((1,), (1,)), ((), ()))


def _params(*sem):
    return pltpu.CompilerParams(dimension_semantics=sem, vmem_limit_bytes=VMEM_LIMIT_BYTES)


def _rms(xf, g):
    return xf * lax.rsqrt(jnp.mean(xf * xf, axis=-1, keepdims=True) + RMS_EPS) * g


def _mm_kernel(prologue, n_pro, has_res, *refs):
    pro_refs = refs[:n_pro]
    w_ref = refs[n_pro]
    res_ref = refs[n_pro + 1] if has_res else None
    o_ref = refs[n_pro + 1 + has_res]
    h_ref = refs[n_pro + 2 + has_res]

    @pl.when(pl.program_id(1) == 0)
    def _():
        prologue(h_ref, *pro_refs)

    w = w_ref[...]
    if w.dtype != BF16:
        w = w.astype(BF16)
    acc = jnp.dot(h_ref[...], w, preferred_element_type=F32)
    if has_res:
        acc = acc + res_ref[...]
    o_ref[...] = acc.astype(o_ref.dtype)


def _fused_matmul(prologue, pro_args, pro_specs, wsel, res, m, tm, tn, out_dtype, name):
    w, layer, col0, n = wsel
    k = w.shape[1]
    cb0 = col0 // tn
    assert col0 % tn == 0 and n % tn == 0 and m % tm == 0
    has_res = res is not None
    if layer is None:
        per_layer = n // tn
        n = n * w.shape[0]
        w_spec = pl.BlockSpec((None, k, tn), lambda i, j: (j // per_layer, 0, cb0 + j % per_layer))
    else:
        w_spec = pl.BlockSpec((None, k, tn), lambda i, j: (layer, 0, cb0 + j))
    in_specs = list(pro_specs) + [w_spec]
    args = list(pro_args) + [w]
    if has_res:
        in_specs.append(pl.BlockSpec((tm, tn), lambda i, j: (i, j)))
        args.append(res)
    return pl.pallas_call(
        functools.partial(_mm_kernel, prologue, len(pro_args), has_res),
        grid=(m // tm, n // tn),
        in_specs=in_specs,
        out_specs=pl.BlockSpec((tm, tn), lambda i, j: (i, j)),
        out_shape=jax.ShapeDtypeStruct((m, n), out_dtype),
        scratch_shapes=[pltpu.VMEM((tm, k), BF16)],
        compiler_params=_params("parallel", "arbitrary"),
        name=name,
    )(*args)


def _rms_prologue(h_ref, x_ref, g_ref):
    h_ref[...] = _rms(x_ref[...], g_ref[...]).astype(BF16)


def _whole(w, layer):
    return (w, layer, 0, w.shape[-1])


def _norm_matmul(x2d, gain, wsel, tm, tn, out_dtype, name, res=None):
    m, k = x2d.shape
    specs = [pl.BlockSpec((tm, k), lambda i, j: (i, 0)), pl.BlockSpec((1, k), lambda i, j: (0, 0))]
    return _fused_matmul(_rms_prologue, [x2d, gain.reshape(1, k)], specs, wsel, res, m, tm, tn, out_dtype, name)


def _cross_prologue(h_ref, q_ref, k_ref, v_ref):
    dh = CROSS_HEAD_DIM
    for h in range(CROSS_HEADS):
        sl = slice(h * dh, (h + 1) * dh)
        s = lax.dot_general(q_ref[:, sl], k_ref[:, sl], NT_DIMS, preferred_element_type=F32) * (dh ** -0.5)
        m = jnp.max(s, axis=-1, keepdims=True)
        p = jnp.exp(s - m)
        l = jnp.sum(p, axis=-1, keepdims=True)
        o = jnp.dot(p.astype(BF16), v_ref[:, sl], preferred_element_type=F32) * (1.0 / l)
        h_ref[:, sl] = o.astype(BF16)


def _branch_prologue(h_ref, na_ref, hy_ref, ret_ref, g_ref):
    def nrm(p):
        return p * lax.rsqrt(jnp.mean(p * p, axis=-1, keepdims=True) + RMS_EPS)

    y = jnp.concatenate([nrm(r[...].astype(F32)) for r in (na_ref, hy_ref, ret_ref)], axis=-1)
    h_ref[...] = (y * g_ref[...]).astype(BF16)


def _branch_out_cross_q_kernel(na_ref, hy_ref, ret_ref, g_ref, wo_ref, res_ref, gq_ref, wq_ref, x_ref, q_ref,
                               h_ref):
    _branch_prologue(h_ref, na_ref, hy_ref, ret_ref, g_ref)
    x = res_ref[...] + jnp.dot(h_ref[...], wo_ref[...], preferred_element_type=F32)
    x_ref[...] = x
    hq = _rms(x, gq_ref[...]).astype(BF16)
    q_ref[...] = jnp.dot(hq, wq_ref[...], preferred_element_type=F32).astype(BF16)


def _branch_out_cross_q(y_na, y_hy, y_ret, gain, w_out, res, gain_q, w_cq, layer, tm):
    m, d = res.shape
    rows = lambda width: pl.BlockSpec((tm, width), lambda i: (i, 0))
    vec = pl.BlockSpec((1, d), lambda i: (0, 0))
    square = pl.BlockSpec((None, d, d), lambda i: (layer, 0, 0))
    return pl.pallas_call(
        _branch_out_cross_q_kernel,
        grid=(m // tm,),
        in_specs=[rows(NA_W), rows(HY_W), rows(RET_W), vec, square, rows(d), vec, square],
        out_specs=[rows(d), rows(d)],
        out_shape=[jax.ShapeDtypeStruct((m, d), F32), jax.ShapeDtypeStruct((m, d), BF16)],
        scratch_shapes=[pltpu.VMEM((tm, d), BF16)],
        compiler_params=_params("parallel"),
        name="branch_out_cross_q",
    )(y_na, y_hy, y_ret, gain.reshape(1, d), w_out, res, gain_q.reshape(1, d), w_cq)


NA_PAIR = 2
NA_BAND = NA_KR + NA_PAIR


def _na_band_base(r, rows):
    return np.clip(r - NA_KR // 2, 0, rows - NA_BAND)


@functools.lru_cache(maxsize=None)
def _na_variants(rows):
    assert rows % NA_PAIR == 0 and rows >= NA_BAND + 2
    n_var = NA_KR // 2 + 1
    dr = np.full((n_var, NA_BAND, NA_PAIR), -2, np.int64)
    for r in range(0, rows, NA_PAIR):
        base = int(_na_band_base(r, rows))
        v = (r - base) // 2
        for j in range(NA_PAIR):
            rs = int(np.clip(r + j - NA_KR // 2, 0, rows - NA_KR))
            for i in range(NA_BAND):
                val = base + i - (r + j) + (NA_KR - 1) if rs <= base + i < rs + NA_KR else -1
                assert dr[v, i, j] in (-2, val)
                dr[v, i, j] = val
    assert (dr > -2).all()
    return dr


def _na_bias_table(rpb, rows):
    c = np.arange(GRID_W)
    col_start = np.clip(c - NA_KC // 2, 0, GRID_W - NA_KC)
    col_in = (c[None, :] >= col_start[:, None]) & (c[None, :] < col_start[:, None] + NA_KC)
    dc = np.clip(c[None, :] - c[:, None] + (NA_KC - 1), 0, 2 * NA_KC - 2)
    onehot = (dc.T.reshape(-1)[None, :] == np.arange(2 * NA_KC - 1)[:, None]).astype(np.float32)
    cols = jnp.einsum("hab,bn->han", rpb.astype(F32), jnp.asarray(onehot), precision=lax.Precision.HIGHEST)
    cols = cols.reshape(NA_HEADS, 2 * NA_KR - 1, GRID_W, GRID_W)
    cols = jnp.where(col_in.T[None, None], cols, MASK_VALUE)
    dr = _na_variants(rows)
    n_var = dr.shape[0]
    return pl.pallas_call(
        functools.partial(_na_table_kernel, dr),
        grid=(NA_HEADS,),
        in_specs=[pl.BlockSpec((1, 2 * NA_KR - 1, GRID_W, GRID_W), lambda h: (h, 0, 0, 0))],
        out_specs=pl.BlockSpec((1, n_var, NA_BAND * GRID_W, NA_PAIR * GRID_W), lambda h: (h, 0, 0, 0)),
        out_shape=jax.ShapeDtypeStruct((NA_HEADS, n_var, NA_BAND * GRID_W, NA_PAIR * GRID_W), F32),
        compiler_params=_params("parallel"),
        name="na_bias_table",
    )(cols)


def _na_table_kernel(dr, cols_ref, o_ref):
    masked = jnp.full((GRID_W, GRID_W), MASK_VALUE, F32)
    for v in range(dr.shape[0]):
        for i in range(NA_BAND):
            blocks = [cols_ref[0, int(dr[v, i, j])] if dr[v, i, j] >= 0 else masked for j in range(NA_PAIR)]
            o_ref[0, v, i * GRID_W:(i + 1) * GRID_W, :] = jnp.concatenate(blocks, axis=-1)


def _na_kernel(rows, q_ref, k_ref, v_ref, bias_ref, o_ref, vt_ref, s0_ref, s1_ref, p0_ref, p1_ref, l0_ref,
               l1_ref):
    dh = NA_HEAD_DIM
    nq = NA_PAIR * GRID_W
    nk = NA_BAND * GRID_W
    n_chunk = nk // LANES
    lane = lax.broadcasted_iota(I32, (nq, 2 * dh), 1)

    for ch in range(vt_ref.shape[0]):
        vt_ref[ch] = v_ref[0, ch * LANES:(ch + 1) * LANES, :].astype(F32).T.astype(BF16)

    n_steps = rows // NA_PAIR

    def band_base(p):
        return jnp.clip(p * NA_PAIR - NA_KR // 2, 0, rows - NA_BAND)


    def scores(p, s_out):
        r = p * NA_PAIR
        base = band_base(p)
        variant = (r - base) // 2
        q = q_ref[0, pl.ds(pl.multiple_of(r * GRID_W, nq), nq), :] * (dh ** -0.5)
        kb = k_ref[0, pl.ds(pl.multiple_of(base * GRID_W, LANES), nk), :]
        zero = jnp.zeros_like(q)
        q2 = jnp.concatenate([jnp.where(lane < dh, q, zero), jnp.where(lane >= dh, q, zero)], axis=0)
        st2 = lax.dot_general(kb, q2, NT_DIMS, preferred_element_type=F32)
        for hh in range(2):
            s_out[:, hh * nq:(hh + 1) * nq] = st2[:, hh * nq:(hh + 1) * nq] + bias_ref[hh, variant]

    def softmax(s_in, p_out, l_out):
        st = s_in[...]
        pt = jnp.exp(st - jnp.max(st, axis=0, keepdims=True))
        p_out[...] = pt.astype(BF16)
        l_out[...] = 1.0 / jnp.sum(pt, axis=0, keepdims=True)

    def values(p, p_in, l_in):
        r = p * NA_PAIR
        c0 = band_base(p) // 2
        vt = jnp.concatenate([vt_ref[c0 + i] for i in range(n_chunk)], axis=1)
        pt = p_in[...]
        linv = l_in[...]
        outs = []
        for hh in range(2):
            cols = slice(hh * nq, (hh + 1) * nq)
            ot = jnp.dot(vt[hh * dh:(hh + 1) * dh, :], pt[:, cols], preferred_element_type=F32)
            outs.append(ot * linv[:, cols])
        o_ref[0, pl.ds(pl.multiple_of(r * GRID_W, nq), nq), :] = jnp.concatenate(outs, axis=0).T.astype(BF16)

    s_slots = (s0_ref, s1_ref)
    p_slots = (p0_ref, p1_ref)
    l_slots = (l0_ref, l1_ref)

    def step(t, parity, do_scores=True, do_softmax=True, do_values=True):
        a, b = parity, 1 - parity
        t = jnp.asarray(t, I32)
        if do_scores:
            scores(t, s_slots[a])
        if do_softmax:
            softmax(s_slots[b], p_slots[b], l_slots[b])
        if do_values:
            values(t - 2, p_slots[a], l_slots[a])

    assert n_steps % 2 == 0 and n_steps >= 4
    step(0, 0, do_softmax=False, do_values=False)
    step(1, 1, do_values=False)

    def steady(i, carry):
        t = 2 + 2 * i
        step(t, 0)
        step(t + 1, 1)
        return carry

    lax.fori_loop(0, (n_steps - 2) // 2, steady, 0, unroll=True)
    step(n_steps, 0, do_scores=False)
    step(n_steps + 1, 1, do_scores=False, do_softmax=False)


def _neighbourhood_attention(proj_na, bias_tbl, batch, seq):
    rows = seq // GRID_W
    n_pairs = NA_HEADS // 2
    blk = (1, seq, 2 * NA_HEAD_DIM)
    n_var, nk, nq = bias_tbl.shape[1:]
    return pl.pallas_call(
        functools.partial(_na_kernel, rows),
        grid=(batch, n_pairs),
        in_specs=[
            pl.BlockSpec(blk, lambda b, h: (b, 0, h)),
            pl.BlockSpec(blk, lambda b, h: (b, 0, n_pairs + h)),
            pl.BlockSpec(blk, lambda b, h: (b, 0, 2 * n_pairs + h)),
            pl.BlockSpec((2, n_var, nk, nq), lambda b, h: (h, 0, 0, 0)),
        ],
        out_specs=pl.BlockSpec(blk, lambda b, h: (b, 0, h)),
        out_shape=jax.ShapeDtypeStruct((batch, seq, NA_W), BF16),
        scratch_shapes=[pltpu.VMEM((seq // LANES, 2 * NA_HEAD_DIM, LANES), BF16),
                        pltpu.VMEM((nk, 2 * nq), F32), pltpu.VMEM((nk, 2 * nq), F32),
                        pltpu.VMEM((nk, 2 * nq), BF16), pltpu.VMEM((nk, 2 * nq), BF16),
                        pltpu.VMEM((1, 2 * nq), F32), pltpu.VMEM((1, 2 * nq), F32)],
        compiler_params=_params("parallel", "arbitrary"),
        name="neighbourhood_attention",
    )(proj_na, proj_na, proj_na, bias_tbl)


@functools.lru_cache(maxsize=None)
def _dft_factors(seq):
    n = 2 * seq
    t = np.arange(seq, dtype=np.int64)
    f1 = np.arange(seq // 64, dtype=np.int64)
    f0 = np.arange(64, dtype=np.int64)
    a = 2.0 * np.pi * ((64 * f1[:, None] * t[None, :]) % n).astype(np.float64) / n
    b = 2.0 * np.pi * ((f0[:, None] * t[None, :]) % n).astype(np.float64) / n
    ny = np.where(t % 2 == 0, 1.0, -1.0)
    return tuple(np.asarray(v, np.float32) for v in (np.cos(a), np.sin(a), np.cos(b), np.sin(b), ny))


def _dft_table_kernel(u_ref, v_ref, cb_ref, sb_ref, fwd_ref, inv_ref):
    sin_half = pl.program_id(0) == 1
    first = pl.program_id(1) == 0
    n_fine, seq = cb_ref.shape
    row = lax.broadcasted_iota(I32, (n_fine, seq), 0)
    col = lax.broadcasted_iota(I32, (n_fine, seq), 1)
    for c in range(u_ref.shape[1]):
        tile = u_ref[0, c:c + 1, :] * cb_ref[...] + v_ref[0, c:c + 1, :] * sb_ref[...]
        rows = slice(c * n_fine, (c + 1) * n_fine)
        fwd = tile
        if c == 0:
            fwd_nyq = jnp.logical_and(jnp.logical_and(sin_half, first), row == 0)
            fwd = jnp.where(fwd_nyq, (1 - 2 * (col & 1)).astype(F32), tile)
        fwd_ref[0, rows, :] = fwd.astype(BF16)
        inv_nyq = jnp.logical_and(sin_half, col == 0)
        inv_ref[rows, :] = jnp.where(inv_nyq, (1 - 2 * (row & 1)).astype(F32), tile).astype(BF16)


def _dft_tables(seq):
    ca, sa, cb, sb, _ = _dft_factors(seq)
    u = np.stack([ca, sa])
    v = np.stack([-sa, ca])
    n_coarse = 8
    rows = 64 * n_coarse
    coarse_spec = pl.BlockSpec((1, n_coarse, seq), lambda h, i: (h, i, 0))
    fine_spec = pl.BlockSpec((64, seq), lambda h, i: (0, 0))
    return pl.pallas_call(
        _dft_table_kernel,
        grid=(2, seq // rows),
        in_specs=[coarse_spec, coarse_spec, fine_spec, fine_spec],
        out_specs=[pl.BlockSpec((1, rows, seq), lambda h, i: (h, i, 0)),
                   pl.BlockSpec((rows, seq), lambda h, i: (i, h))],
        out_shape=[jax.ShapeDtypeStruct((2, seq, seq), BF16), jax.ShapeDtypeStruct((seq, 2 * seq), BF16)],
        compiler_params=_params("parallel", "arbitrary"),
        name="dft_tables",
    )(jnp.asarray(u), jnp.asarray(v), jnp.asarray(cb), jnp.asarray(sb))


@functools.lru_cache(maxsize=None)
def _hyena_consts(seq):
    t = np.arange(seq, dtype=np.float64)
    t01 = t / (seq - 1)
    bands = np.linspace(1e-4, HY_BANDS - 1, HY_BANDS)
    ang = (2.0 * math.pi) * (t[:, None] / seq) * bands[None, :]
    feats = np.concatenate([t01[:, None], np.cos(ang), -np.sin(ang)], axis=-1)
    feats_p = np.zeros((seq, LANES), np.float32)
    feats_p[:, :HY_POS_DIM] = feats
    min_decay = math.log(1e-2) / 1.5
    max_decay = math.log(1e-2) / 0.3
    deltas = np.abs(np.linspace(min_decay, max_decay, HY_W))
    window = np.exp(-t01[:, None] * deltas[None, :]).astype(np.float32)
    return feats_p, window


def _filter_kernel(feats_ref, w1_ref, b1_ref, w2_ref, b2_ref, freq_ref, w3f_ref, w3b_ref, win_ref, sum_ref,
                   diff_ref, hid_ref):
    hp = lax.Precision.HIGHEST

    @pl.when(pl.program_id(0) == 0)
    def _():
        f = freq_ref[...]
        h1 = jnp.sin(f * (jnp.dot(feats_ref[...], w1_ref[...], precision=hp, preferred_element_type=F32)
                          + b1_ref[...]))
        hid_ref[...] = jnp.sin(f * (jnp.dot(h1, w2_ref[...], precision=hp, preferred_element_type=F32)
                                    + b2_ref[...]))

    hid = hid_ref[...]
    win = win_ref[...]
    fwd = jnp.dot(hid, w3f_ref[...], precision=hp, preferred_element_type=F32) * win
    bwd = jnp.dot(hid, w3b_ref[...], precision=hp, preferred_element_type=F32) * win
    bwd = jnp.where(lax.broadcasted_iota(I32, bwd.shape, 0) == 0, 0.0, bwd)
    sum_ref[...] = (fwd + bwd).astype(BF16)
    diff_ref[...] = (fwd - bwd).astype(BF16)


def _hyena_filters_time(w1, b1, w2, b2, w3, freq, seq):
    feats, window = _hyena_consts(seq)
    w1p = jnp.zeros((LANES, HY_FILT_FF), F32).at[:HY_POS_DIM].set(w1)
    full = lambda shape: pl.BlockSpec(shape, lambda o: (0,) * len(shape))
    out_spec = pl.BlockSpec((seq, HY_W), lambda o: (0, o))
    out_shape = jax.ShapeDtypeStruct((seq, HY_ORDER * HY_W), BF16)
    return pl.pallas_call(
        _filter_kernel,
        grid=(HY_ORDER,),
        in_specs=[
            full((seq, LANES)), full((LANES, HY_FILT_FF)), full((1, HY_FILT_FF)),
            full((HY_FILT_FF, HY_FILT_FF)), full((1, HY_FILT_FF)), full((1, HY_FILT_FF)),
            pl.BlockSpec((HY_FILT_FF, HY_W), lambda o: (0, 2 * o)),
            pl.BlockSpec((HY_FILT_FF, HY_W), lambda o: (0, 2 * o + 1)),
            full((seq, HY_W)),
        ],
        out_specs=[out_spec, out_spec],
        out_shape=[out_shape, out_shape],
        scratch_shapes=[pltpu.VMEM((seq, HY_FILT_FF), F32)],
        compiler_params=_params("arbitrary"),
        name="hyena_filter_mlp",
    )(jnp.asarray(feats), w1p, b1.reshape(1, -1), w2, b2.reshape(1, -1), freq.reshape(1, -1), w3, w3,
      jnp.asarray(window))


def _filter_dft_kernel(f_ref, sum_ref, diff_ref, o_ref):
    o_ref[0] = jnp.dot(f_ref[0], sum_ref[...], preferred_element_type=F32)
    o_ref[1] = jnp.dot(f_ref[1], diff_ref[...], preferred_element_type=F32)

    @pl.when(pl.program_id(0) == 0)
    def _():
        top = jnp.dot(f_ref[1, 0:8, :], sum_ref[...], preferred_element_type=F32)
        row = lax.broadcasted_iota(I32, top.shape, 0)
        o_ref[1, 0:8, :] = jnp.where(row == 0, top, o_ref[1, 0:8, :])


def _filter_dft(dft_fwd, filt_sum, filt_diff, seq, fb):
    return pl.pallas_call(
        _filter_dft_kernel,
        grid=(seq // fb, HY_ORDER),
        in_specs=[
            pl.BlockSpec((2, fb, seq), lambda f, o: (0, f, 0)),
            pl.BlockSpec((seq, HY_W), lambda f, o: (0, o)),
            pl.BlockSpec((seq, HY_W), lambda f, o: (0, o)),
        ],
        out_specs=pl.BlockSpec((2, fb, HY_W), lambda f, o: (0, f, o)),
        out_shape=jax.ShapeDtypeStruct((2, seq, HY_ORDER * HY_W), F32),
        compiler_params=_params("parallel", "arbitrary"),
        name="hyena_filter_dft",
    )(dft_fwd, filt_sum, filt_diff)


def _short_conv_kernel(p_ref, w_ref, b_ref, o_ref):
    p = p_ref[0].astype(F32)
    seq = p.shape[0]
    row = lax.broadcasted_iota(I32, p.shape, 0)
    prev = jnp.where(row == 0, 0.0, pltpu.roll(p, 1, 0))
    nxt = jnp.where(row == seq - 1, 0.0, pltpu.roll(p, seq - 1, 0))
    w = w_ref[...]
    o_ref[0] = prev * w[0:1] + p * w[1:2] + nxt * w[2:3] + b_ref[...]


def _short_conv(proj, col0, conv_w, conv_b, batch, seq, tc):
    n_cols = conv_w.shape[-1]
    assert col0 % tc == 0 and n_cols % tc == 0
    cb0 = col0 // tc
    return pl.pallas_call(
        _short_conv_kernel,
        grid=(batch, n_cols // tc),
        in_specs=[
            pl.BlockSpec((1, seq, tc), lambda b, c: (b, 0, cb0 + c)),
            pl.BlockSpec((3, tc), lambda b, c: (0, c)),
            pl.BlockSpec((1, tc), lambda b, c: (0, c)),
        ],
        out_specs=pl.BlockSpec((1, seq, tc), lambda b, c: (b, 0, c)),
        out_shape=jax.ShapeDtypeStruct((batch, seq, n_cols), F32),
        compiler_params=_params("parallel", "arbitrary"),
        name="hyena_short_conv",
    )(proj, conv_w, conv_b.reshape(1, -1))


def _spectrum_kernel(n_fft, f_ref, z_ref, k_ref, o_ref):
    z = z_ref[0].astype(BF16)
    xr = jnp.dot(f_ref[0], z, preferred_element_type=F32)
    xs = jnp.dot(f_ref[1], z, preferred_element_type=F32)
    kr = k_ref[0]
    ks = k_ref[1]
    row = lax.broadcasted_iota(I32, xr.shape, 0)
    edge = jnp.logical_and(row == 0, pl.program_id(0) == 0)
    yr = jnp.where(edge, xr * kr * (1.0 / n_fft), (xr * kr - xs * ks) * (2.0 / n_fft))
    ys = jnp.where(edge, xs * ks * (1.0 / n_fft), (xr * ks + xs * kr) * (2.0 / n_fft))
    o_ref[0, 0] = yr.astype(BF16)
    o_ref[0, 1] = ys.astype(BF16)


def _spectrum_product(dft_fwd, z_arr, z_col, kfreq, order, batch, seq, fb):
    return pl.pallas_call(
        functools.partial(_spectrum_kernel, 2 * seq),
        grid=(seq // fb, batch),
        in_specs=[
            pl.BlockSpec((2, fb, seq), lambda f, b: (0, f, 0)),
            pl.BlockSpec((1, seq, HY_W), lambda f, b: (b, 0, z_col)),
            pl.BlockSpec((2, fb, HY_W), lambda f, b: (0, f, order)),
        ],
        out_specs=pl.BlockSpec((1, 2, fb, HY_W), lambda f, b: (b, 0, f, 0)),
        out_shape=jax.ShapeDtypeStruct((batch, 2, seq, HY_W), BF16),
        compiler_params=_params("parallel", "arbitrary"),
        name="hyena_spectrum",
    )(dft_fwd, z_arr, kfreq)


def _inverse_kernel(ft_ref, y_ref, gate_ref, z_ref, d_ref, o_ref):
    conv = jnp.dot(ft_ref[...], y_ref[0], preferred_element_type=F32)
    o_ref[0] = gate_ref[0] * (conv + d_ref[...] * z_ref[0])


def _inverse_gate(dft_inv, y, gate_arr, gate_col, z_arr, z_col, skip_row, batch, seq, tb):
    y2 = y.reshape(batch, 2 * seq, HY_W)
    return pl.pallas_call(
        _inverse_kernel,
        grid=(seq // tb, batch),
        in_specs=[
            pl.BlockSpec((tb, 2 * seq), lambda t, b: (t, 0)),
            pl.BlockSpec((1, 2 * seq, HY_W), lambda t, b: (b, 0, 0)),
            pl.BlockSpec((1, tb, HY_W), lambda t, b: (b, t, gate_col)),
            pl.BlockSpec((1, tb, HY_W), lambda t, b: (b, t, z_col)),
            pl.BlockSpec((1, HY_W), lambda t, b: (0, 0)),
        ],
        out_specs=pl.BlockSpec((1, tb, HY_W), lambda t, b: (b, t, 0)),
        out_shape=jax.ShapeDtypeStruct((batch, seq, HY_W), F32),
        compiler_params=_params("parallel", "arbitrary"),
        name="hyena_inverse_gate",
    )(dft_inv, y2, gate_arr, z_arr, skip_row.reshape(1, HY_W))


def _hyena_mixer(proj, col0, conv_w, conv_b, w1, b1, w2, b2, w3, freq, skip_d, dft_fwd, dft_inv, batch, seq):
    s = _short_conv(proj, col0, conv_w, conv_b, batch, seq, 768)
    filt_sum, filt_diff = _hyena_filters_time(w1, b1, w2, b2, w3, freq, seq)
    kfreq = _filter_dft(dft_fwd, filt_sum, filt_diff, seq, 512)
    z_arr, z_col = s, 2
    for o in range(HY_ORDER):
        y = _spectrum_product(dft_fwd, z_arr, z_col, kfreq, o, batch, seq, 1024)
        z_arr = _inverse_gate(dft_inv, y, s, o, z_arr, z_col, skip_d[o], batch, seq, 1024)
        z_col = 0
    return z_arr


@functools.lru_cache(maxsize=None)
def _retention_consts(seq):
    c = RET_CHUNK
    half = RET_HEAD_DIM // 2
    inv = 1.0 / (10000.0 ** np.linspace(0.0, 1.0, half))
    ang = np.arange(seq, dtype=np.float64)[:, None] * inv[None, :]
    cos2 = np.concatenate([np.cos(ang), np.cos(ang)], axis=-1).astype(np.float32)
    sin2 = np.concatenate([-np.sin(ang), np.sin(ang)], axis=-1).astype(np.float32)
    hidx = np.arange(RET_HEADS, dtype=np.float64)
    lg_f = np.log1p(-np.exp2(-5.0 - hidx))[:, None, None]
    lg_b = np.log1p(-np.exp2(-5.5 - hidx))[:, None, None]
    i = np.arange(c, dtype=np.float64)
    diff = i[:, None] - i[None, :]
    ones = np.ones((1, c, c))
    dec = np.where(diff >= 0, np.exp(lg_f * np.maximum(diff, 0.0)), np.exp(lg_b * np.maximum(-diff, 0.0)))
    rowv = lambda v: v[:, :, None] * ones
    tab = np.stack([
        dec,
        rowv(np.exp(lg_f[:, :, 0] * (i + 1.0)[None, :])),
        rowv(np.exp(lg_f[:, :, 0] * (c - 1.0 - i)[None, :])),
        rowv(np.exp(lg_b[:, :, 0] * (c - i)[None, :])),
        rowv(np.exp(lg_b[:, :, 0] * i[None, :])),
        np.exp(lg_f * c) * ones,
        np.exp(lg_b * c) * ones,
    ], axis=1).astype(np.float32)
    return cos2, sin2, tab


def _retention_kernel(q_ref, k_ref, v_ref, g_ref, cos_ref, sin_ref, tab_ref, o_ref, qs_ref, ks_ref, kvf_ref,
                      kvb_ref, a0_ref, a1_ref, y0_ref, y1_ref):
    c = RET_CHUNK
    d = RET_HEAD_DIM
    seq = q_ref.shape[1]
    n_chunks = seq // c
    cos = cos_ref[...]
    sin = sin_ref[...]
    q = q_ref[0].astype(F32)
    k = k_ref[0].astype(F32)
    qs_ref[...] = (q * cos + pltpu.roll(q, d // 2, 1) * sin) * (d ** -0.5)
    ks_ref[...] = k * cos + pltpu.roll(k, d // 2, 1) * sin

    def mm(a, b):
        return jnp.dot(a.astype(BF16), b.astype(BF16), preferred_element_type=F32)

    def chunk(n):
        return pl.ds(pl.multiple_of(n * c, c), c)

    def kv_body(n, carry):
        kc = ks_ref[chunk(n), :]
        vc = v_ref[0, chunk(n), :]
        kvf_ref[n] = mm((kc * tab_ref[0, 2]).T, vc)
        kvb_ref[n] = mm((kc * tab_ref[0, 4]).T, vc)
        return carry

    lax.fori_loop(0, n_chunks, kv_body, 0, unroll=True)

    def scan_fwd(n, state):
        kv = kvf_ref[n]
        kvf_ref[n] = state
        return tab_ref[0, 5] * state + kv

    lax.fori_loop(0, n_chunks, scan_fwd, jnp.zeros((d, d), F32))

    def scan_bwd(m, state):
        n = n_chunks - 1 - m
        kv = kvb_ref[n]
        kvb_ref[n] = state
        return tab_ref[0, 6] * state + kv

    lax.fori_loop(0, n_chunks, scan_bwd, jnp.zeros((d, d), F32))

    def in_chunk(n, a_out):
        qc = qs_ref[chunk(n), :].astype(BF16)
        kc = ks_ref[chunk(n), :].astype(BF16)
        a = lax.dot_general(qc, kc, NT_DIMS, preferred_element_type=F32) * tab_ref[0, 0]
        a_out[...] = a.astype(BF16)

    def mix(n, a_in, y_out):
        qc = qs_ref[chunk(n), :]
        y_out[...] = (jnp.dot(a_in[...], v_ref[0, chunk(n), :], preferred_element_type=F32)
                      + mm(qc * tab_ref[0, 1], kvf_ref[n]) + mm(qc * tab_ref[0, 3], kvb_ref[n]))

    def finish(n, y_in):
        y = y_in[...]
        mu = jnp.mean(y, axis=-1, keepdims=True)
        yc = y - mu
        var = jnp.mean(yc * yc, axis=-1, keepdims=True)
        g = g_ref[0, chunk(n), :].astype(F32)
        o_ref[0, chunk(n), :] = (yc * lax.rsqrt(var + GN_EPS) * (g * jax.nn.sigmoid(g))).astype(BF16)

    a_slots = (a0_ref, a1_ref)
    y_slots = (y0_ref, y1_ref)

    def step(t, parity, do_a=True, do_mix=True, do_finish=True):
        t = jnp.asarray(t, I32)
        if do_a:
            in_chunk(t, a_slots[parity])
        if do_mix:
            mix(t - 1, a_slots[1 - parity], y_slots[1 - parity])
        if do_finish:
            finish(t - 2, y_slots[parity])

    assert n_chunks % 2 == 0 and n_chunks >= 4
    step(0, 0, do_mix=False, do_finish=False)
    step(1, 1, do_finish=False)

    def steady(i, carry):
        step(2 + 2 * i, 0)
        step(3 + 2 * i, 1)
        return carry

    lax.fori_loop(0, (n_chunks - 2) // 2, steady, 0, unroll=True)
    step(n_chunks, 0, do_a=False)
    step(n_chunks + 1, 1, do_a=False, do_mix=False)


def _retention_mixer(proj, col0, batch, seq):
    cos2, sin2, tab = _retention_consts(seq)
    blk = (1, seq, RET_HEAD_DIM)
    h_ = RET_HEADS
    assert col0 % RET_HEAD_DIM == 0
    c0 = col0 // RET_HEAD_DIM
    return pl.pallas_call(
        _retention_kernel,
        grid=(batch, RET_HEADS),
        in_specs=[
            pl.BlockSpec(blk, lambda b, h: (b, 0, c0 + h)),
            pl.BlockSpec(blk, lambda b, h: (b, 0, c0 + h_ + h)),
            pl.BlockSpec(blk, lambda b, h: (b, 0, c0 + 2 * h_ + h)),
            pl.BlockSpec(blk, lambda b, h: (b, 0, c0 + 3 * h_ + h)),
            pl.BlockSpec((seq, RET_HEAD_DIM), lambda b, h: (0, 0)),
            pl.BlockSpec((seq, RET_HEAD_DIM), lambda b, h: (0, 0)),
            pl.BlockSpec((1, 7, RET_CHUNK, RET_CHUNK), lambda b, h: (h, 0, 0, 0)),
        ],
        out_specs=pl.BlockSpec(blk, lambda b, h: (b, 0, h)),
        out_shape=jax.ShapeDtypeStruct((batch, seq, RET_W), BF16),
        scratch_shapes=[pltpu.VMEM((seq, RET_HEAD_DIM), F32), pltpu.VMEM((seq, RET_HEAD_DIM), F32),
                        pltpu.VMEM((seq // RET_CHUNK, RET_HEAD_DIM, RET_HEAD_DIM), F32),
                        pltpu.VMEM((seq // RET_CHUNK, RET_HEAD_DIM, RET_HEAD_DIM), F32),
                        pltpu.VMEM((RET_CHUNK, RET_CHUNK), BF16), pltpu.VMEM((RET_CHUNK, RET_CHUNK), BF16),
                        pltpu.VMEM((RET_CHUNK, RET_HEAD_DIM), F32), pltpu.VMEM((RET_CHUNK, RET_HEAD_DIM), F32)],
        compiler_params=_params("parallel", "arbitrary"),
        name="retention",
    )(proj, proj, proj, proj, jnp.asarray(cos2), jnp.asarray(sin2), jnp.asarray(tab))


def _cross_out_router_kernel(q_ref, k_ref, v_ref, w_ref, res_ref, g_ref, wr_ref, x_ref, hm_ref, aff_ref, h_ref):
    n_e = aff_ref.shape[1]
    _cross_prologue(h_ref, q_ref, k_ref, v_ref)
    x = res_ref[...] + jnp.dot(h_ref[...], w_ref[...], preferred_element_type=F32)
    x_ref[...] = x
    h = _rms(x, g_ref[...])
    h_hi = h.astype(BF16)
    hm_ref[...] = h_hi
    h_lo = (h - h_hi.astype(F32)).astype(BF16)
    wr = wr_ref[...]
    w_hi = wr.astype(BF16)
    w_lo = (wr - w_hi.astype(F32)).astype(BF16)
    both = jnp.dot(h_hi, jnp.concatenate([w_hi, w_lo], axis=1), preferred_element_type=F32)
    logits = (both[:, :LANES] + both[:, LANES:] + jnp.dot(h_lo, w_hi, preferred_element_type=F32)).T[:n_e]
    m = jnp.max(logits, axis=0, keepdims=True)
    e = jnp.exp(logits - m)
    aff_ref[0] = e / jnp.sum(e, axis=0, keepdims=True)


def _cross_out_and_router(q, kv, w_co, layer, res, gain, w_router, batch, seq, n_mem, tm):
    m, d = q.shape
    per_b = seq // tm
    assert seq % tm == 0
    w_pad = jnp.zeros((d, LANES), F32).at[:, :N_EXPERTS].set(w_router)
    row_spec = pl.BlockSpec((tm, d), lambda i: (i, 0))
    return pl.pallas_call(
        _cross_out_router_kernel,
        grid=(m // tm,),
        in_specs=[
            row_spec,
            pl.BlockSpec((n_mem, d), lambda i: (i // per_b, 2 * layer)),
            pl.BlockSpec((n_mem, d), lambda i: (i // per_b, 2 * layer + 1)),
            pl.BlockSpec((None, d, d), lambda i: (layer, 0, 0)),
            row_spec,
            pl.BlockSpec((1, d), lambda i: (0, 0)),
            pl.BlockSpec((d, LANES), lambda i: (0, 0)),
        ],
        out_specs=[
            row_spec,
            row_spec,
            pl.BlockSpec((1, N_EXPERTS, tm), lambda i: (i // per_b, 0, i % per_b)),
        ],
        out_shape=[
            jax.ShapeDtypeStruct((m, d), F32),
            jax.ShapeDtypeStruct((m, d), BF16),
            jax.ShapeDtypeStruct((batch, N_EXPERTS, seq), F32),
        ],
        scratch_shapes=[pltpu.VMEM((tm, d), BF16)],
        compiler_params=_params("parallel"),
        name="cross_out_router",
    )(q, kv, kv, w_co, res, gain.reshape(1, d), w_pad)


def _sort_descending(x):
    rows, n = x.shape
    lanes = LANES
    n_chunks = n // lanes
    chunks = [x[:, c * lanes:(c + 1) * lanes] for c in range(n_chunks)]
    lane = lax.broadcasted_iota(I32, (rows, lanes), 1)
    k = 2
    while k <= n:
        j = k // 2
        while j >= 1:
            nxt = []
            for c, xc in enumerate(chunks):
                if j >= lanes:
                    lo = (c & (j // lanes)) == 0
                    partner = chunks[c ^ (j // lanes)]
                else:
                    lo = (lane & j) == 0
                    partner = jnp.where(lo, pltpu.roll(xc, lanes - j, 1), pltpu.roll(xc, j, 1))
                desc = ((c * lanes) & k) == 0 if k >= lanes else (lane & k) == 0
                hi, lw = jnp.maximum(xc, partner), jnp.minimum(xc, partner)
                if j >= lanes and k >= lanes:
                    nxt.append(hi if lo == desc else lw)
                else:
                    nxt.append(jnp.where(lo == desc, hi, lw))
            chunks = nxt
            j //= 2
        k *= 2
    return jnp.concatenate(chunks, axis=1)


def _topk_kernel(cap, n_e, aff_ref, slot_ref, wsel_ref, slot_t_ref, thr_ref):
    n_rows, n_tok = aff_ref.shape

    def thr_body(i, carry):
        rows8 = pl.ds(pl.multiple_of(i * 8, 8), 8)
        thr_ref[rows8, :] = _sort_descending(aff_ref[rows8, :])[:, cap - 1:cap]
        return carry

    lax.fori_loop(0, n_rows // 8, thr_body, 0)
    a = aff_ref[...]
    thr = thr_ref[...]
    gt = a > thr
    eq = a == thr
    need = cap - jnp.sum(gt.astype(I32), axis=1, keepdims=True)
    upper = (lax.broadcasted_iota(I32, (n_tok, n_tok), 0) < lax.broadcasted_iota(I32, (n_tok, n_tok), 1))
    upper = upper.astype(BF16)
    eq_rank = jnp.dot(eq.astype(BF16), upper, preferred_element_type=F32)
    sel = jnp.logical_or(gt, jnp.logical_and(eq, eq_rank < need.astype(F32)))
    rank = jnp.dot(sel.astype(BF16), upper, preferred_element_type=F32)
    slot = jnp.where(sel, rank, -1.0)
    slot_ref[...] = slot.astype(I32)
    wsel_ref[...] = jnp.where(sel, a, 0.0)
    pad = jnp.full((LANES - n_e, n_tok), -1.0, F32)
    for b in range(n_rows // n_e):
        slot_t_ref[b] = jnp.concatenate([slot[b * n_e:(b + 1) * n_e], pad], axis=0).T.astype(I32)


def _topk_select(aff2d, cap, n_e):
    rows, n_tok = aff2d.shape
    spec = pl.BlockSpec((rows, n_tok), lambda i: (0, 0))
    spec_t = pl.BlockSpec((rows // n_e, n_tok, LANES), lambda i: (0, 0, 0))
    return pl.pallas_call(
        functools.partial(_topk_kernel, cap, n_e),
        grid=(1,),
        in_specs=[spec],
        out_specs=[spec, spec, spec_t],
        out_shape=[jax.ShapeDtypeStruct((rows, n_tok), I32), jax.ShapeDtypeStruct((rows, n_tok), F32),
                   jax.ShapeDtypeStruct((rows // n_e, n_tok, LANES), I32)],
        scratch_shapes=[pltpu.VMEM((rows, 1), F32)],
        compiler_params=_params("arbitrary"),
        name="moe_topk_select",
    )(aff2d)


def _gather_kernel(cap, slot_ref, wsel_ref, hm_ref, xe_ref, gs_ref):
    n_tok = hm_ref.shape[0]
    onehot = slot_ref[0] == lax.broadcasted_iota(I32, (cap, n_tok), 0)
    xe_ref[0] = jnp.dot(onehot.astype(BF16), hm_ref[...], preferred_element_type=F32).astype(BF16)
    gs_ref[0] = jnp.sum(jnp.where(onehot, wsel_ref[0], 0.0), axis=1, keepdims=True)


def _gather_tokens(slot, wsel, hm, batch, seq, cap):
    rows = batch * N_EXPERTS
    row_spec = pl.BlockSpec((1, 1, seq), lambda b, e: (b * N_EXPERTS + e, 0, 0))
    return pl.pallas_call(
        functools.partial(_gather_kernel, cap),
        grid=(batch, N_EXPERTS),
        in_specs=[row_spec, row_spec, pl.BlockSpec((seq, D_MODEL), lambda b, e: (b, 0))],
        out_specs=[
            pl.BlockSpec((1, cap, D_MODEL), lambda b, e: (e, b, 0)),
            pl.BlockSpec((1, cap, 1), lambda b, e: (e, b, 0)),
        ],
        out_shape=[
            jax.ShapeDtypeStruct((N_EXPERTS, batch * cap, D_MODEL), BF16),
            jax.ShapeDtypeStruct((N_EXPERTS, batch * cap, 1), F32),
        ],
        compiler_params=_params("parallel", "arbitrary"),
        name="moe_gather",
    )(slot.reshape(rows, 1, seq), wsel.reshape(rows, 1, seq), hm)


def _expert_kernel(n_e, n_t, xe_ref, wg_ref, wu_ref, wd_ref, gs_ref, ye_ref, mid_ref):
    g = pl.program_id(0)
    s = pl.program_id(1)
    t = s // 2
    tf = wg_ref.shape[-1]

    @pl.when(jnp.logical_and(s % 2 == 0, g < n_e))
    def _():
        x = xe_ref[0]
        a = jnp.dot(x, wg_ref[...].astype(BF16), preferred_element_type=F32)
        u = jnp.dot(x, wu_ref[...].astype(BF16), preferred_element_type=F32)
        mid_ref[g % 2, t] = (a * jax.nn.sigmoid(a) * u).astype(BF16)

    @pl.when(jnp.logical_and(s % 2 == 1, g >= 1))
    def _():
        prev = (g - 1) % 2
        acc = jnp.dot(mid_ref[prev, 0], wd_ref[0:tf, :].astype(BF16), preferred_element_type=F32)
        for f in range(1, n_t):
            acc += jnp.dot(mid_ref[prev, f], wd_ref[f * tf:(f + 1) * tf, :].astype(BF16),
                           preferred_element_type=F32)
        ye_ref[0] = (acc * gs_ref[0]).astype(BF16)


def _expert_ffn(xe, gs, w_gate, w_up, w_down, layer, tf):
    n_e, rows, d = xe.shape
    ff = w_gate.shape[-1]
    n_t = ff // tf
    assert d // tf == n_t
    last = n_e - 1

    n_s = 2 * n_t

    def fill_tile(g, s, lead):
        v = jnp.minimum(g * n_s + s + lead, n_e * n_s - 1)
        return (layer, v // n_s, 0, (v % n_s) // 2)

    def emit_tile(g, s):
        v = jnp.maximum(g * n_s + s - 1 - n_s, 0)
        return (layer, v // n_s, 0, (v % n_s) // 2)

    return pl.pallas_call(
        functools.partial(_expert_kernel, n_e, n_t),
        grid=(n_e + 1, n_s),
        in_specs=[
            pl.BlockSpec((1, rows, d), lambda g, s: (jnp.minimum(g, last), 0, 0)),
            pl.BlockSpec((None, None, d, tf), functools.partial(fill_tile, lead=1)),
            pl.BlockSpec((None, None, d, tf), functools.partial(fill_tile, lead=0)),
            pl.BlockSpec((None, None, ff, tf), emit_tile),
            pl.BlockSpec((1, rows, 1), lambda g, s: (jnp.maximum(g - 1, 0), 0, 0)),
        ],
        out_specs=pl.BlockSpec((1, rows, tf), lambda g, s: emit_tile(g, s)[1:]),
        out_shape=jax.ShapeDtypeStruct((n_e, rows, d), BF16),
        scratch_shapes=[pltpu.VMEM((2, n_t, rows, tf), BF16)],
        compiler_params=_params("arbitrary", "arbitrary"),
        name="moe_expert_ffn",
    )(xe, w_gate, w_up, w_down, gs)


def _scatter_kernel(cap, slot_ref, ye_ref, x_ref, *rest):
    o_ref = rest[-1]
    n_e = ye_ref.shape[0]
    col = lax.broadcasted_iota(I32, (x_ref.shape[0], cap), 1)
    slots = slot_ref[0]
    onehot = jnp.concatenate([(slots[:, e:e + 1] == col).astype(BF16) for e in range(n_e)], axis=1)
    ye = ye_ref[...].reshape(n_e * cap, ye_ref.shape[-1])
    y = x_ref[...] + jnp.dot(onehot, ye, preferred_element_type=F32)
    o_ref[...] = _rms(y, rest[0][...]) if len(rest) == 2 else y


def _scatter_add(slot_t, ye, x2d, batch, seq, cap, tr, final_gain=None):
    d = x2d.shape[-1]
    per_b = seq // tr
    in_specs = [
        pl.BlockSpec((1, tr, LANES), lambda b, i: (b, i, 0)),
        pl.BlockSpec((N_EXPERTS, cap, d), lambda b, i: (0, b, 0)),
        pl.BlockSpec((tr, d), lambda b, i: (b * per_b + i, 0)),
    ]
    args = [slot_t, ye, x2d]
    if final_gain is not None:
        in_specs.append(pl.BlockSpec((1, d), lambda b, i: (0, 0)))
        args.append(final_gain.reshape(1, d))
    return pl.pallas_call(
        functools.partial(_scatter_kernel, cap),
        grid=(batch, per_b),
        in_specs=in_specs,
        out_specs=pl.BlockSpec((tr, d), lambda b, i: (b * per_b + i, 0)),
        out_shape=jax.ShapeDtypeStruct(x2d.shape, F32),
        compiler_params=_params("parallel", "arbitrary"),
        name="moe_scatter_add",
    )(*args)


def _expert_choice_ffn(x2d, hm, aff, w_gate, w_up, w_down, layer, batch, seq, final_gain=None):
    cap = EC_CAPACITY * seq // N_EXPERTS
    slot, wsel, slot_t = _topk_select(aff.reshape(batch * N_EXPERTS, seq), cap, N_EXPERTS)
    xe, gs = _gather_tokens(slot, wsel, hm, batch, seq, cap)
    ye = _expert_ffn(xe, gs, w_gate, w_up, w_down, layer, 512)
    return _scatter_add(slot_t, ye, x2d, batch, seq, cap, 256, final_gain)


def kernel(x, mem, norm_mix, w_in, na_rpb, hy_conv_w, hy_conv_b, hy_filt_w1, hy_filt_b1, hy_filt_w2, hy_filt_b2, hy_filt_w3, hy_sin_freq, hy_skip_d, branch_norm, w_out, norm_cross, mem_norm, w_cq, w_ckv, w_co, norm_moe, w_router, w_gate, w_up, w_down, final_norm):
    batch, seq, d = x.shape
    n_mem = mem.shape[1]
    depth = w_in.shape[0]
    m = batch * seq
    na_cols = 3 * NA_W
    hy_cols = 3 * HY_W
    rows = seq // GRID_W

    dft_fwd, dft_inv = _dft_tables(seq)
    x2d = x.reshape(m, d)
    mem2d = mem.reshape(batch * n_mem, d)

    w_out_b, w_cq_b, w_co_b = (w.astype(BF16) for w in (w_out, w_cq, w_co))
    p_in = w_in.shape[-1]
    kv = _norm_matmul(mem2d, mem_norm, (w_ckv, None, 0, w_ckv.shape[-1]), batch * n_mem, 1024, BF16,
                      "cross_kv_proj")

    for l in range(depth):
        proj = _norm_matmul(x2d, norm_mix[l], _whole(w_in, l), 1024, 768, BF16, "in_proj")
        proj = proj.reshape(batch, seq, p_in)

        y_na = _neighbourhood_attention(proj, _na_bias_table(na_rpb[l], rows), batch, seq)
        y_hy = _hyena_mixer(proj, na_cols, hy_conv_w[l], hy_conv_b[l], hy_filt_w1[l],
                            hy_filt_b1[l], hy_filt_w2[l], hy_filt_b2[l], hy_filt_w3[l], hy_sin_freq[l],
                            hy_skip_d[l], dft_fwd, dft_inv, batch, seq)
        y_ret = _retention_mixer(proj, na_cols + hy_cols, batch, seq)
        x2d, q = _branch_out_cross_q(y_na.reshape(m, NA_W), y_hy.reshape(m, HY_W), y_ret.reshape(m, RET_W),
                                     branch_norm[l], w_out_b, x2d, norm_cross[l], w_cq_b, l, 512)
        x2d, hm, aff = _cross_out_and_router(q, kv, w_co_b, l, x2d, norm_moe[l], w_router[l], batch, seq, n_mem,
                                             256)

        x2d = _expert_choice_ffn(x2d, hm, aff, w_gate, w_up, w_down, l, batch, seq,
                                 final_gain=final_norm if l == depth - 1 else None)

    return x2d.reshape(batch, seq, d)
```

```python
import functools
import math

import numpy as np
import jax
import jax.numpy as jnp
from jax import lax
from jax.experimental import pallas as pl
from jax.experimental.pallas import tpu as pltpu

F32 = jnp.float32
BF16 = jnp.bfloat16
I32 = jnp.int32

D_MODEL = 2048
GRID_W = 64
NA_HEAD_DIM = 64
NA_W = 768
NA_HEADS = 12
NA_KR = 8
NA_KC = 16
HY_W = 512
HY_ORDER = 2
HY_BANDS = 8
HY_POS_DIM = 17
HY_FILT_FF = 64
RET_HEAD_DIM = 128
RET_W = 768
RET_HEADS = 6
RET_CHUNK = 128
CROSS_HEADS = 4
CROSS_HEAD_DIM = 512
N_EXPERTS = 16
EXPERT_FF = 2048
EC_CAPACITY = 2
RMS_EPS = 1e-6
GN_EPS = 1e-5

MASK_VALUE = -1e30
LANES = 128
VMEM_LIMIT_BYTES = 56 * 1024 * 1024

NT_DIMS = (((1,), (1,)), ((), ()))


def _params(*sem):
    return pltpu.CompilerParams(dimension_semantics=sem, vmem_limit_bytes=VMEM_LIMIT_BYTES)


def _rms(xf, g):
    return xf * lax.rsqrt(jnp.mean(xf * xf, axis=-1, keepdims=True) + RMS_EPS) * g


def _mm_kernel(prologue, n_pro, has_res, *refs):
    pro_refs = refs[:n_pro]
    w_ref = refs[n_pro]
    res_ref = refs[n_pro + 1] if has_res else None
    o_ref = refs[n_pro + 1 + has_res]
    h_ref = refs[n_pro + 2 + has_res]

    @pl.when(pl.program_id(1) == 0)
    def _():
        prologue(h_ref, *pro_refs)

    w = w_ref[...]
    if w.dtype != BF16:
        w = w.astype(BF16)
    acc = jnp.dot(h_ref[...], w, preferred_element_type=F32)
    if has_res:
        acc = acc + res_ref[...]
    o_ref[...] = acc.astype(o_ref.dtype)


def _fused_matmul(prologue, pro_args, pro_specs, wsel, res, m, tm, tn, out_dtype, name):
    w, layer, col0, n = wsel
    k = w.shape[1]
    cb0 = col0 // tn
    assert col0 % tn == 0 and n % tn == 0 and m % tm == 0
    has_res = res is not None
    if layer is None:
        per_layer = n // tn
        n = n * w.shape[0]
        w_spec = pl.BlockSpec((None, k, tn), lambda i, j: (j // per_layer, 0, cb0 + j % per_layer))
    else:
        w_spec = pl.BlockSpec((None, k, tn), lambda i, j: (layer, 0, cb0 + j))
    in_specs = list(pro_specs) + [w_spec]
    args = list(pro_args) + [w]
    if has_res:
        in_specs.append(pl.BlockSpec((tm, tn), lambda i, j: (i, j)))
        args.append(res)
    return pl.pallas_call(
        functools.partial(_mm_kernel, prologue, len(pro_args), has_res),
        grid=(m // tm, n // tn),
        in_specs=in_specs,
        out_specs=pl.BlockSpec((tm, tn), lambda i, j: (i, j)),
        out_shape=jax.ShapeDtypeStruct((m, n), out_dtype),
        scratch_shapes=[pltpu.VMEM((tm, k), BF16)],
        compiler_params=_params("parallel", "arbitrary"),
        name=name,
    )(*args)


def _rms_prologue(h_ref, x_ref, g_ref):
    h_ref[...] = _rms(x_ref[...], g_ref[...]).astype(BF16)


def _whole(w, layer):
    return (w, layer, 0, w.shape[-1])


def _norm_matmul(x2d, gain, wsel, tm, tn, out_dtype, name, res=None):
    m, k = x2d.shape
    specs = [pl.BlockSpec((tm, k), lambda i, j: (i, 0)), pl.BlockSpec((1, k), lambda i, j: (0, 0))]
    return _fused_matmul(_rms_prologue, [x2d, gain.reshape(1, k)], specs, wsel, res, m, tm, tn, out_dtype, name)


def _cross_prologue(h_ref, q_ref, k_ref, v_ref):
    dh = CROSS_HEAD_DIM
    for h in range(CROSS_HEADS):
        sl = slice(h * dh, (h + 1) * dh)
        s = lax.dot_general(q_ref[:, sl], k_ref[:, sl], NT_DIMS, preferred_element_type=F32) * (dh ** -0.5)
        m = jnp.max(s, axis=-1, keepdims=True)
        p = jnp.exp(s - m)
        l = jnp.sum(p, axis=-1, keepdims=True)
        o = jnp.dot(p.astype(BF16), v_ref[:, sl], preferred_element_type=F32) * (1.0 / l)
        h_ref[:, sl] = o.astype(BF16)


def _branch_prologue(h_ref, na_ref, hy_ref, ret_ref, g_ref):
    def nrm(p):
        return p * lax.rsqrt(jnp.mean(p * p, axis=-1, keepdims=True) + RMS_EPS)

    y = jnp.concatenate([nrm(r[...].astype(F32)) for r in (na_ref, hy_ref, ret_ref)], axis=-1)
    h_ref[...] = (y * g_ref[...]).astype(BF16)


def _branch_out_cross_q_kernel(na_ref, hy_ref, ret_ref, g_ref, wo_ref, res_ref, gq_ref, wq_ref, x_ref, q_ref,
                               h_ref):
    _branch_prologue(h_ref, na_ref, hy_ref, ret_ref, g_ref)
    x = res_ref[...] + jnp.dot(h_ref[...], wo_ref[...], preferred_element_type=F32)
    x_ref[...] = x
    hq = _rms(x, gq_ref[...]).astype(BF16)
    q_ref[...] = jnp.dot(hq, wq_ref[...], preferred_element_type=F32).astype(BF16)


def _branch_out_cross_q(y_na, y_hy, y_ret, gain, w_out, res, gain_q, w_cq, layer, tm):
    m, d = res.shape
    rows = lambda width: pl.BlockSpec((tm, width), lambda i: (i, 0))
    vec = pl.BlockSpec((1, d), lambda i: (0, 0))
    square = pl.BlockSpec((None, d, d), lambda i: (layer, 0, 0))
    return pl.pallas_call(
        _branch_out_cross_q_kernel,
        grid=(m // tm,),
        in_specs=[rows(NA_W), rows(HY_W), rows(RET_W), vec, square, rows(d), vec, square],
        out_specs=[rows(d), rows(d)],
        out_shape=[jax.ShapeDtypeStruct((m, d), F32), jax.ShapeDtypeStruct((m, d), BF16)],
        scratch_shapes=[pltpu.VMEM((tm, d), BF16)],
        compiler_params=_params("parallel"),
        name="branch_out_cross_q",
    )(y_na, y_hy, y_ret, gain.reshape(1, d), w_out, res, gain_q.reshape(1, d), w_cq)


NA_PAIR = 2
NA_BAND = NA_KR + NA_PAIR


def _na_band_base(r, rows):
    return np.clip(r - NA_KR // 2, 0, rows - NA_BAND)


@functools.lru_cache(maxsize=None)
def _na_variants(rows):
    assert rows % NA_PAIR == 0 and rows >= NA_BAND + 2
    n_var = NA_KR // 2 + 1
    dr = np.full((n_var, NA_BAND, NA_PAIR), -2, np.int64)
    for r in range(0, rows, NA_PAIR):
        base = int(_na_band_base(r, rows))
        v = (r - base) // 2
        for j in range(NA_PAIR):
            rs = int(np.clip(r + j - NA_KR // 2, 0, rows - NA_KR))
            for i in range(NA_BAND):
                val = base + i - (r + j) + (NA_KR - 1) if rs <= base + i < rs + NA_KR else -1
                assert dr[v, i, j] in (-2, val)
                dr[v, i, j] = val
    assert (dr > -2).all()
    return dr


def _na_bias_table(rpb, rows):
    c = np.arange(GRID_W)
    col_start = np.clip(c - NA_KC // 2, 0, GRID_W - NA_KC)
    col_in = (c[None, :] >= col_start[:, None]) & (c[None, :] < col_start[:, None] + NA_KC)
    dc = np.clip(c[None, :] - c[:, None] + (NA_KC - 1), 0, 2 * NA_KC - 2)
    onehot = (dc.T.reshape(-1)[None, :] == np.arange(2 * NA_KC - 1)[:, None]).astype(np.float32)
    cols = jnp.einsum("hab,bn->han", rpb.astype(F32), jnp.asarray(onehot), precision=lax.Precision.HIGHEST)
    cols = cols.reshape(NA_HEADS, 2 * NA_KR - 1, GRID_W, GRID_W)
    cols = jnp.where(col_in.T[None, None], cols, MASK_VALUE)
    dr = _na_variants(rows)
    n_var = dr.shape[0]
    return pl.pallas_call(
        functools.partial(_na_table_kernel, dr),
        grid=(NA_HEADS,),
        in_specs=[pl.BlockSpec((1, 2 * NA_KR - 1, GRID_W, GRID_W), lambda h: (h, 0, 0, 0))],
        out_specs=pl.BlockSpec((1, n_var, NA_BAND * GRID_W, NA_PAIR * GRID_W), lambda h: (h, 0, 0, 0)),
        out_shape=jax.ShapeDtypeStruct((NA_HEADS, n_var, NA_BAND * GRID_W, NA_PAIR * GRID_W), F32),
        compiler_params=_params("parallel"),
        name="na_bias_table",
    )(cols)


def _na_table_kernel(dr, cols_ref, o_ref):
    masked = jnp.full((GRID_W, GRID_W), MASK_VALUE, F32)
    for v in range(dr.shape[0]):
        for i in range(NA_BAND):
            blocks = [cols_ref[0, int(dr[v, i, j])] if dr[v, i, j] >= 0 else masked for j in range(NA_PAIR)]
            o_ref[0, v, i * GRID_W:(i + 1) * GRID_W, :] = jnp.concatenate(blocks, axis=-1)


def _na_kernel(rows, q_ref, k_ref, v_ref, bias_ref, o_ref, vt_ref, s0_ref, s1_ref, p0_ref, p1_ref, l0_ref,
               l1_ref):
    dh = NA_HEAD_DIM
    nq = NA_PAIR * GRID_W
    nk = NA_BAND * GRID_W
    n_chunk = nk // LANES
    lane = lax.broadcasted_iota(I32, (nq, 2 * dh), 1)

    for ch in range(vt_ref.shape[0]):
        vt_ref[ch] = v_ref[0, ch * LANES:(ch + 1) * LANES, :].astype(F32).T.astype(BF16)

    n_steps = rows // NA_PAIR

    def band_base(p):
        return jnp.clip(p * NA_PAIR - NA_KR // 2, 0, rows - NA_BAND)


    def scores(p, s_out):
        r = p * NA_PAIR
        base = band_base(p)
        variant = (r - base) // 2
        q = q_ref[0, pl.ds(pl.multiple_of(r * GRID_W, nq), nq), :] * (dh ** -0.5)
        kb = k_ref[0, pl.ds(pl.multiple_of(base * GRID_W, LANES), nk), :]
        zero = jnp.zeros_like(q)
        q2 = jnp.concatenate([jnp.where(lane < dh, q, zero), jnp.where(lane >= dh, q, zero)], axis=0)
        st2 = lax.dot_general(kb, q2, NT_DIMS, preferred_element_type=F32)
        for hh in range(2):
            s_out[:, hh * nq:(hh + 1) * nq] = st2[:, hh * nq:(hh + 1) * nq] + bias_ref[hh, variant]

    def softmax(s_in, p_out, l_out):
        st = s_in[...]
        pt = jnp.exp(st - jnp.max(st, axis=0, keepdims=True))
        p_out[...] = pt.astype(BF16)
        l_out[...] = 1.0 / jnp.sum(pt, axis=0, keepdims=True)

    def values(p, p_in, l_in):
        r = p * NA_PAIR
        c0 = band_base(p) // 2
        vt = jnp.concatenate([vt_ref[c0 + i] for i in range(n_chunk)], axis=1)
        pt = p_in[...]
        linv = l_in[...]
        outs = []
        for hh in range(2):
            cols = slice(hh * nq, (hh + 1) * nq)
            ot = jnp.dot(vt[hh * dh:(hh + 1) * dh, :], pt[:, cols], preferred_element_type=F32)
            outs.append(ot * linv[:, cols])
        o_ref[0, pl.ds(pl.multiple_of(r * GRID_W, nq), nq), :] = jnp.concatenate(outs, axis=0).T.astype(BF16)

    s_slots = (s0_ref, s1_ref)
    p_slots = (p0_ref, p1_ref)
    l_slots = (l0_ref, l1_ref)

    def step(t, parity, do_scores=True, do_softmax=True, do_values=True):
        a, b = parity, 1 - parity
        t = jnp.asarray(t, I32)
        if do_scores:
            scores(t, s_slots[a])
        if do_softmax:
            softmax(s_slots[b], p_slots[b], l_slots[b])
        if do_values:
            values(t - 2, p_slots[a], l_slots[a])

    assert n_steps % 2 == 0 and n_steps >= 4
    step(0, 0, do_softmax=False, do_values=False)
    step(1, 1, do_values=False)

    def steady(i, carry):
        t = 2 + 2 * i
        step(t, 0)
        step(t + 1, 1)
        return carry

    lax.fori_loop(0, (n_steps - 2) // 2, steady, 0, unroll=True)
    step(n_steps, 0, do_scores=False)
    step(n_steps + 1, 1, do_scores=False, do_softmax=False)


def _neighbourhood_attention(proj_na, bias_tbl, batch, seq):
    rows = seq // GRID_W
    n_pairs = NA_HEADS // 2
    blk = (1, seq, 2 * NA_HEAD_DIM)
    n_var, nk, nq = bias_tbl.shape[1:]
    return pl.pallas_call(
        functools.partial(_na_kernel, rows),
        grid=(batch, n_pairs),
        in_specs=[
            pl.BlockSpec(blk, lambda b, h: (b, 0, h)),
            pl.BlockSpec(blk, lambda b, h: (b, 0, n_pairs + h)),
            pl.BlockSpec(blk, lambda b, h: (b, 0, 2 * n_pairs + h)),
            pl.BlockSpec((2, n_var, nk, nq), lambda b, h: (h, 0, 0, 0)),
        ],
        out_specs=pl.BlockSpec(blk, lambda b, h: (b, 0, h)),
        out_shape=jax.ShapeDtypeStruct((batch, seq, NA_W), BF16),
        scratch_shapes=[pltpu.VMEM((seq // LANES, 2 * NA_HEAD_DIM, LANES), BF16),
                        pltpu.VMEM((nk, 2 * nq), F32), pltpu.VMEM((nk, 2 * nq), F32),
                        pltpu.VMEM((nk, 2 * nq), BF16), pltpu.VMEM((nk, 2 * nq), BF16),
                        pltpu.VMEM((1, 2 * nq), F32), pltpu.VMEM((1, 2 * nq), F32)],
        compiler_params=_params("parallel", "arbitrary"),
        name="neighbourhood_attention",
    )(proj_na, proj_na, proj_na, bias_tbl)


@functools.lru_cache(maxsize=None)
def _dft_factors(seq):
    n = 2 * seq
    t = np.arange(seq, dtype=np.int64)
    f1 = np.arange(seq // 64, dtype=np.int64)
    f0 = np.arange(64, dtype=np.int64)
    a = 2.0 * np.pi * ((64 * f1[:, None] * t[None, :]) % n).astype(np.float64) / n
    b = 2.0 * np.pi * ((f0[:, None] * t[None, :]) % n).astype(np.float64) / n
    ny = np.where(t % 2 == 0, 1.0, -1.0)
    return tuple(np.asarray(v, np.float32) for v in (np.cos(a), np.sin(a), np.cos(b), np.sin(b), ny))


def _dft_table_kernel(u_ref, v_ref, cb_ref, sb_ref, fwd_ref, inv_ref):
    sin_half = pl.program_id(0) == 1
    first = pl.program_id(1) == 0
    n_fine, seq = cb_ref.shape
    row = lax.broadcasted_iota(I32, (n_fine, seq), 0)
    col = lax.broadcasted_iota(I32, (n_fine, seq), 1)
    for c in range(u_ref.shape[1]):
        tile = u_ref[0, c:c + 1, :] * cb_ref[...] + v_ref[0, c:c + 1, :] * sb_ref[...]
        rows = slice(c * n_fine, (c + 1) * n_fine)
        fwd = tile
        if c == 0:
            fwd_nyq = jnp.logical_and(jnp.logical_and(sin_half, first), row == 0)
            fwd = jnp.where(fwd_nyq, (1 - 2 * (col & 1)).astype(F32), tile)
        fwd_ref[0, rows, :] = fwd.astype(BF16)
        inv_nyq = jnp.logical_and(sin_half, col == 0)
        inv_ref[rows, :] = jnp.where(inv_nyq, (1 - 2 * (row & 1)).astype(F32), tile).astype(BF16)


def _dft_tables(seq):
    ca, sa, cb, sb, _ = _dft_factors(seq)
    u = np.stack([ca, sa])
    v = np.stack([-sa, ca])
    n_coarse = 8
    rows = 64 * n_coarse
    coarse_spec = pl.BlockSpec((1, n_coarse, seq), lambda h, i: (h, i, 0))
    fine_spec = pl.BlockSpec((64, seq), lambda h, i: (0, 0))
    return pl.pallas_call(
        _dft_table_kernel,
        grid=(2, seq // rows),
        in_specs=[coarse_spec, coarse_spec, fine_spec, fine_spec],
        out_specs=[pl.BlockSpec((1, rows, seq), lambda h, i: (h, i, 0)),
                   pl.BlockSpec((rows, seq), lambda h, i: (i, h))],
        out_shape=[jax.ShapeDtypeStruct((2, seq, seq), BF16), jax.ShapeDtypeStruct((seq, 2 * seq), BF16)],
        compiler_params=_params("parallel", "arbitrary"),
        name="dft_tables",
    )(jnp.asarray(u), jnp.asarray(v), jnp.asarray(cb), jnp.asarray(sb))


@functools.lru_cache(maxsize=None)
def _hyena_consts(seq):
    t = np.arange(seq, dtype=np.float64)
    t01 = t / (seq - 1)
    bands = np.linspace(1e-4, HY_BANDS - 1, HY_BANDS)
    ang = (2.0 * math.pi) * (t[:, None] / seq) * bands[None, :]
    feats = np.concatenate([t01[:, None], np.cos(ang), -np.sin(ang)], axis=-1)
    feats_p = np.zeros((seq, LANES), np.float32)
    feats_p[:, :HY_POS_DIM] = feats
    min_decay = math.log(1e-2) / 1.5
    max_decay = math.log(1e-2) / 0.3
    deltas = np.abs(np.linspace(min_decay, max_decay, HY_W))
    window = np.exp(-t01[:, None] * deltas[None, :]).astype(np.float32)
    return feats_p, window


def _filter_kernel(feats_ref, w1_ref, b1_ref, w2_ref, b2_ref, freq_ref, w3f_ref, w3b_ref, win_ref, sum_ref,
                   diff_ref, hid_ref):
    hp = lax.Precision.HIGHEST

    @pl.when(pl.program_id(0) == 0)
    def _():
        f = freq_ref[...]
        h1 = jnp.sin(f * (jnp.dot(feats_ref[...], w1_ref[...], precision=hp, preferred_element_type=F32)
                          + b1_ref[...]))
        hid_ref[...] = jnp.sin(f * (jnp.dot(h1, w2_ref[...], precision=hp, preferred_element_type=F32)
                                    + b2_ref[...]))

    hid = hid_ref[...]
    win = win_ref[...]
    fwd = jnp.dot(hid, w3f_ref[...], precision=hp, preferred_element_type=F32) * win
    bwd = jnp.dot(hid, w3b_ref[...], precision=hp, preferred_element_type=F32) * win
    bwd = jnp.where(lax.broadcasted_iota(I32, bwd.shape, 0) == 0, 0.0, bwd)
    sum_ref[...] = (fwd + bwd).astype(BF16)
    diff_ref[...] = (fwd - bwd).astype(BF16)


def _hyena_filters_time(w1, b1, w2, b2, w3, freq, seq):
    feats, window = _hyena_consts(seq)
    w1p = jnp.zeros((LANES, HY_FILT_FF), F32).at[:HY_POS_DIM].set(w1)
    full = lambda shape: pl.BlockSpec(shape, lambda o: (0,) * len(shape))
    out_spec = pl.BlockSpec((seq, HY_W), lambda o: (0, o))
    out_shape = jax.ShapeDtypeStruct((seq, HY_ORDER * HY_W), BF16)
    return pl.pallas_call(
        _filter_kernel,
        grid=(HY_ORDER,),
        in_specs=[
            full((seq, LANES)), full((LANES, HY_FILT_FF)), full((1, HY_FILT_FF)),
            full((HY_FILT_FF, HY_FILT_FF)), full((1, HY_FILT_FF)), full((1, HY_FILT_FF)),
            pl.BlockSpec((HY_FILT_FF, HY_W), lambda o: (0, 2 * o)),
            pl.BlockSpec((HY_FILT_FF, HY_W), lambda o: (0, 2 * o + 1)),
            full((seq, HY_W)),
        ],
        out_specs=[out_spec, out_spec],
        out_shape=[out_shape, out_shape],
        scratch_shapes=[pltpu.VMEM((seq, HY_FILT_FF), F32)],
        compiler_params=_params("arbitrary"),
        name="hyena_filter_mlp",
    )(jnp.asarray(feats), w1p, b1.reshape(1, -1), w2, b2.reshape(1, -1), freq.reshape(1, -1), w3, w3,
      jnp.asarray(window))


def _filter_dft_kernel(f_ref, sum_ref, diff_ref, o_ref):
    o_ref[0] = jnp.dot(f_ref[0], sum_ref[...], preferred_element_type=F32)
    o_ref[1] = jnp.dot(f_ref[1], diff_ref[...], preferred_element_type=F32)

    @pl.when(pl.program_id(0) == 0)
    def _():
        top = jnp.dot(f_ref[1, 0:8, :], sum_ref[...], preferred_element_type=F32)
        row = lax.broadcasted_iota(I32, top.shape, 0)
        o_ref[1, 0:8, :] = jnp.where(row == 0, top, o_ref[1, 0:8, :])


def _filter_dft(dft_fwd, filt_sum, filt_diff, seq, fb):
    return pl.pallas_call(
        _filter_dft_kernel,
        grid=(seq // fb, HY_ORDER),
        in_specs=[
            pl.BlockSpec((2, fb, seq), lambda f, o: (0, f, 0)),
            pl.BlockSpec((seq, HY_W), lambda f, o: (0, o)),
            pl.BlockSpec((seq, HY_W), lambda f, o: (0, o)),
        ],
        out_specs=pl.BlockSpec((2, fb, HY_W), lambda f, o: (0, f, o)),
        out_shape=jax.ShapeDtypeStruct((2, seq, HY_ORDER * HY_W), F32),
        compiler_params=_params("parallel", "arbitrary"),
        name="hyena_filter_dft",
    )(dft_fwd, filt_sum, filt_diff)


def _short_conv_kernel(p_ref, w_ref, b_ref, o_ref):
    p = p_ref[0].astype(F32)
    seq = p.shape[0]
    row = lax.broadcasted_iota(I32, p.shape, 0)
    prev = jnp.where(row == 0, 0.0, pltpu.roll(p, 1, 0))
    nxt = jnp.where(row == seq - 1, 0.0, pltpu.roll(p, seq - 1, 0))
    w = w_ref[...]
    o_ref[0] = prev * w[0:1] + p * w[1:2] + nxt * w[2:3] + b_ref[...]


def _short_conv(proj, col0, conv_w, conv_b, batch, seq, tc):
    n_cols = conv_w.shape[-1]
    assert col0 % tc == 0 and n_cols % tc == 0
    cb0 = col0 // tc
    return pl.pallas_call(
        _short_conv_kernel,
        grid=(batch, n_cols // tc),
        in_specs=[
            pl.BlockSpec((1, seq, tc), lambda b, c: (b, 0, cb0 + c)),
            pl.BlockSpec((3, tc), lambda b, c: (0, c)),
            pl.BlockSpec((1, tc), lambda b, c: (0, c)),
        ],
        out_specs=pl.BlockSpec((1, seq, tc), lambda b, c: (b, 0, c)),
        out_shape=jax.ShapeDtypeStruct((batch, seq, n_cols), F32),
        compiler_params=_params("parallel", "arbitrary"),
        name="hyena_short_conv",
    )(proj, conv_w, conv_b.reshape(1, -1))


def _spectrum_kernel(n_fft, f_ref, z_ref, k_ref, o_ref):
    z = z_ref[0].astype(BF16)
    xr = jnp.dot(f_ref[0], z, preferred_element_type=F32)
    xs = jnp.dot(f_ref[1], z, preferred_element_type=F32)
    kr = k_ref[0]
    ks = k_ref[1]
    row = lax.broadcasted_iota(I32, xr.shape, 0)
    edge = jnp.logical_and(row == 0, pl.program_id(0) == 0)
    yr = jnp.where(edge, xr * kr * (1.0 / n_fft), (xr * kr - xs * ks) * (2.0 / n_fft))
    ys = jnp.where(edge, xs * ks * (1.0 / n_fft), (xr * ks + xs * kr) * (2.0 / n_fft))
    o_ref[0, 0] = yr.astype(BF16)
    o_ref[0, 1] = ys.astype(BF16)


def _spectrum_product(dft_fwd, z_arr, z_col, kfreq, order, batch, seq, fb):
    return pl.pallas_call(
        functools.partial(_spectrum_kernel, 2 * seq),
        grid=(seq // fb, batch),
        in_specs=[
            pl.BlockSpec((2, fb, seq), lambda f, b: (0, f, 0)),
            pl.BlockSpec((1, seq, HY_W), lambda f, b: (b, 0, z_col)),
            pl.BlockSpec((2, fb, HY_W), lambda f, b: (0, f, order)),
        ],
        out_specs=pl.BlockSpec((1, 2, fb, HY_W), lambda f, b: (b, 0, f, 0)),
        out_shape=jax.ShapeDtypeStruct((batch, 2, seq, HY_W), BF16),
        compiler_params=_params("parallel", "arbitrary"),
        name="hyena_spectrum",
    )(dft_fwd, z_arr, kfreq)


def _inverse_kernel(ft_ref, y_ref, gate_ref, z_ref, d_ref, o_ref):
    conv = jnp.dot(ft_ref[...], y_ref[0], preferred_element_type=F32)
    o_ref[0] = gate_ref[0] * (conv + d_ref[...] * z_ref[0])


def _inverse_gate(dft_inv, y, gate_arr, gate_col, z_arr, z_col, skip_row, batch, seq, tb):
    y2 = y.reshape(batch, 2 * seq, HY_W)
    return pl.pallas_call(
        _inverse_kernel,
        grid=(seq // tb, batch),
        in_specs=[
            pl.BlockSpec((tb, 2 * seq), lambda t, b: (t, 0)),
            pl.BlockSpec((1, 2 * seq, HY_W), lambda t, b: (b, 0, 0)),
            pl.BlockSpec((1, tb, HY_W), lambda t, b: (b, t, gate_col)),
            pl.BlockSpec((1, tb, HY_W), lambda t, b: (b, t, z_col)),
            pl.BlockSpec((1, HY_W), lambda t, b: (0, 0)),
        ],
        out_specs=pl.BlockSpec((1, tb, HY_W), lambda t, b: (b, t, 0)),
        out_shape=jax.ShapeDtypeStruct((batch, seq, HY_W), F32),
        compiler_params=_params("parallel", "arbitrary"),
        name="hyena_inverse_gate",
    )(dft_inv, y2, gate_arr, z_arr, skip_row.reshape(1, HY_W))


def _hyena_mixer(proj, col0, conv_w, conv_b, w1, b1, w2, b2, w3, freq, skip_d, dft_fwd, dft_inv, batch, seq):
    s = _short_conv(proj, col0, conv_w, conv_b, batch, seq, 768)
    filt_sum, filt_diff = _hyena_filters_time(w1, b1, w2, b2, w3, freq, seq)
    kfreq = _filter_dft(dft_fwd, filt_sum, filt_diff, seq, 512)
    z_arr, z_col = s, 2
    for o in range(HY_ORDER):
        y = _spectrum_product(dft_fwd, z_arr, z_col, kfreq, o, batch, seq, 1024)
        z_arr = _inverse_gate(dft_inv, y, s, o, z_arr, z_col, skip_d[o], batch, seq, 1024)
        z_col = 0
    return z_arr


@functools.lru_cache(maxsize=None)
def _retention_consts(seq):
    c = RET_CHUNK
    half = RET_HEAD_DIM // 2
    inv = 1.0 / (10000.0 ** np.linspace(0.0, 1.0, half))
    ang = np.arange(seq, dtype=np.float64)[:, None] * inv[None, :]
    cos2 = np.concatenate([np.cos(ang), np.cos(ang)], axis=-1).astype(np.float32)
    sin2 = np.concatenate([-np.sin(ang), np.sin(ang)], axis=-1).astype(np.float32)
    hidx = np.arange(RET_HEADS, dtype=np.float64)
    lg_f = np.log1p(-np.exp2(-5.0 - hidx))[:, None, None]
    lg_b = np.log1p(-np.exp2(-5.5 - hidx))[:, None, None]
    i = np.arange(c, dtype=np.float64)
    diff = i[:, None] - i[None, :]
    ones = np.ones((1, c, c))
    dec = np.where(diff >= 0, np.exp(lg_f * np.maximum(diff, 0.0)), np.exp(lg_b * np.maximum(-diff, 0.0)))
    rowv = lambda v: v[:, :, None] * ones
    tab = np.stack([
        dec,
        rowv(np.exp(lg_f[:, :, 0] * (i + 1.0)[None, :])),
        rowv(np.exp(lg_f[:, :, 0] * (c - 1.0 - i)[None, :])),
        rowv(np.exp(lg_b[:, :, 0] * (c - i)[None, :])),
        rowv(np.exp(lg_b[:, :, 0] * i[None, :])),
        np.exp(lg_f * c) * ones,
        np.exp(lg_b * c) * ones,
    ], axis=1).astype(np.float32)
    return cos2, sin2, tab


def _retention_kernel(q_ref, k_ref, v_ref, g_ref, cos_ref, sin_ref, tab_ref, o_ref, qs_ref, ks_ref, kvf_ref,
                      kvb_ref, a0_ref, a1_ref, y0_ref, y1_ref):
    c = RET_CHUNK
    d = RET_HEAD_DIM
    seq = q_ref.shape[1]
    n_chunks = seq // c
    cos = cos_ref[...]
    sin = sin_ref[...]
    q = q_ref[0].astype(F32)
    k = k_ref[0].astype(F32)
    qs_ref[...] = (q * cos + pltpu.roll(q, d // 2, 1) * sin) * (d ** -0.5)
    ks_ref[...] = k * cos + pltpu.roll(k, d // 2, 1) * sin

    def mm(a, b):
        return jnp.dot(a.astype(BF16), b.astype(BF16), preferred_element_type=F32)

    def chunk(n):
        return pl.ds(pl.multiple_of(n * c, c), c)

    def kv_body(n, carry):
        kc = ks_ref[chunk(n), :]
        vc = v_ref[0, chunk(n), :]
        kvf_ref[n] = mm((kc * tab_ref[0, 2]).T, vc)
        kvb_ref[n] = mm((kc * tab_ref[0, 4]).T, vc)
        return carry

    lax.fori_loop(0, n_chunks, kv_body, 0, unroll=True)

    def scan_fwd(n, state):
        kv = kvf_ref[n]
        kvf_ref[n] = state
        return tab_ref[0, 5] * state + kv

    lax.fori_loop(0, n_chunks, scan_fwd, jnp.zeros((d, d), F32))

    def scan_bwd(m, state):
        n = n_chunks - 1 - m
        kv = kvb_ref[n]
        kvb_ref[n] = state
        return tab_ref[0, 6] * state + kv

    lax.fori_loop(0, n_chunks, scan_bwd, jnp.zeros((d, d), F32))

    def in_chunk(n, a_out):
        qc = qs_ref[chunk(n), :].astype(BF16)
        kc = ks_ref[chunk(n), :].astype(BF16)
        a = lax.dot_general(qc, kc, NT_DIMS, preferred_element_type=F32) * tab_ref[0, 0]
        a_out[...] = a.astype(BF16)

    def mix(n, a_in, y_out):
        qc = qs_ref[chunk(n), :]
        y_out[...] = (jnp.dot(a_in[...], v_ref[0, chunk(n), :], preferred_element_type=F32)
                      + mm(qc * tab_ref[0, 1], kvf_ref[n]) + mm(qc * tab_ref[0, 3], kvb_ref[n]))

    def finish(n, y_in):
        y = y_in[...]
        mu = jnp.mean(y, axis=-1, keepdims=True)
        yc = y - mu
        var = jnp.mean(yc * yc, axis=-1, keepdims=True)
        g = g_ref[0, chunk(n), :].astype(F32)
        o_ref[0, chunk(n), :] = (yc * lax.rsqrt(var + GN_EPS) * (g * jax.nn.sigmoid(g))).astype(BF16)

    a_slots = (a0_ref, a1_ref)
    y_slots = (y0_ref, y1_ref)

    def step(t, parity, do_a=True, do_mix=True, do_finish=True):
        t = jnp.asarray(t, I32)
        if do_a:
            in_chunk(t, a_slots[parity])
        if do_mix:
            mix(t - 1, a_slots[1 - parity], y_slots[1 - parity])
        if do_finish:
            finish(t - 2, y_slots[parity])

    assert n_chunks % 2 == 0 and n_chunks >= 4
    step(0, 0, do_mix=False, do_finish=False)
    step(1, 1, do_finish=False)

    def steady(i, carry):
        step(2 + 2 * i, 0)
        step(3 + 2 * i, 1)
        return carry

    lax.fori_loop(0, (n_chunks - 2) // 2, steady, 0, unroll=True)
    step(n_chunks, 0, do_a=False)
    step(n_chunks + 1, 1, do_a=False, do_mix=False)


def _retention_mixer(proj, col0, batch, seq):
    cos2, sin2, tab = _retention_consts(seq)
    blk = (1, seq, RET_HEAD_DIM)
    h_ = RET_HEADS
    assert col0 % RET_HEAD_DIM == 0
    c0 = col0 // RET_HEAD_DIM
    return pl.pallas_call(
        _retention_kernel,
        grid=(batch, RET_HEADS),
        in_specs=[
            pl.BlockSpec(blk, lambda b, h: (b, 0, c0 + h)),
            pl.BlockSpec(blk, lambda b, h: (b, 0, c0 + h_ + h)),
            pl.BlockSpec(blk, lambda b, h: (b, 0, c0 + 2 * h_ + h)),
            pl.BlockSpec(blk, lambda b, h: (b, 0, c0 + 3 * h_ + h)),
            pl.BlockSpec((seq, RET_HEAD_DIM), lambda b, h: (0, 0)),
            pl.BlockSpec((seq, RET_HEAD_DIM), lambda b, h: (0, 0)),
            pl.BlockSpec((1, 7, RET_CHUNK, RET_CHUNK), lambda b, h: (h, 0, 0, 0)),
        ],
        out_specs=pl.BlockSpec(blk, lambda b, h: (b, 0, h)),
        out_shape=jax.ShapeDtypeStruct((batch, seq, RET_W), BF16),
        scratch_shapes=[pltpu.VMEM((seq, RET_HEAD_DIM), F32), pltpu.VMEM((seq, RET_HEAD_DIM), F32),
                        pltpu.VMEM((seq // RET_CHUNK, RET_HEAD_DIM, RET_HEAD_DIM), F32),
                        pltpu.VMEM((seq // RET_CHUNK, RET_HEAD_DIM, RET_HEAD_DIM), F32),
                        pltpu.VMEM((RET_CHUNK, RET_CHUNK), BF16), pltpu.VMEM((RET_CHUNK, RET_CHUNK), BF16),
                        pltpu.VMEM((RET_CHUNK, RET_HEAD_DIM), F32), pltpu.VMEM((RET_CHUNK, RET_HEAD_DIM), F32)],
        compiler_params=_params("parallel", "arbitrary"),
        name="retention",
    )(proj, proj, proj, proj, jnp.asarray(cos2), jnp.asarray(sin2), jnp.asarray(tab))


def _cross_out_router_kernel(q_ref, k_ref, v_ref, w_ref, res_ref, g_ref, wr_ref, x_ref, hm_ref, aff_ref, h_ref):
    n_e = aff_ref.shape[1]
    _cross_prologue(h_ref, q_ref, k_ref, v_ref)
    x = res_ref[...] + jnp.dot(h_ref[...], w_ref[...], preferred_element_type=F32)
    x_ref[...] = x
    h = _rms(x, g_ref[...])
    h_hi = h.astype(BF16)
    hm_ref[...] = h_hi
    h_lo = (h - h_hi.astype(F32)).astype(BF16)
    wr = wr_ref[...]
    w_hi = wr.astype(BF16)
    w_lo = (wr - w_hi.astype(F32)).astype(BF16)
    both = jnp.dot(h_hi, jnp.concatenate([w_hi, w_lo], axis=1), preferred_element_type=F32)
    logits = (both[:, :LANES] + both[:, LANES:] + jnp.dot(h_lo, w_hi, preferred_element_type=F32)).T[:n_e]
    m = jnp.max(logits, axis=0, keepdims=True)
    e = jnp.exp(logits - m)
    aff_ref[0] = e / jnp.sum(e, axis=0, keepdims=True)


def _cross_out_and_router(q, kv, w_co, layer, res, gain, w_router, batch, seq, n_mem, tm):
    m, d = q.shape
    per_b = seq // tm
    assert seq % tm == 0
    w_pad = jnp.zeros((d, LANES), F32).at[:, :N_EXPERTS].set(w_router)
    row_spec = pl.BlockSpec((tm, d), lambda i: (i, 0))
    return pl.pallas_call(
        _cross_out_router_kernel,
        grid=(m // tm,),
        in_specs=[
            row_spec,
            pl.BlockSpec((n_mem, d), lambda i: (i // per_b, 2 * layer)),
            pl.BlockSpec((n_mem, d), lambda i: (i // per_b, 2 * layer + 1)),
            pl.BlockSpec((None, d, d), lambda i: (layer, 0, 0)),
            row_spec,
            pl.BlockSpec((1, d), lambda i: (0, 0)),
            pl.BlockSpec((d, LANES), lambda i: (0, 0)),
        ],
        out_specs=[
            row_spec,
            row_spec,
            pl.BlockSpec((1, N_EXPERTS, tm), lambda i: (i // per_b, 0, i % per_b)),
        ],
        out_shape=[
            jax.ShapeDtypeStruct((m, d), F32),
            jax.ShapeDtypeStruct((m, d), BF16),
            jax.ShapeDtypeStruct((batch, N_EXPERTS, seq), F32),
        ],
        scratch_shapes=[pltpu.VMEM((tm, d), BF16)],
        compiler_params=_params("parallel"),
        name="cross_out_router",
    )(q, kv, kv, w_co, res, gain.reshape(1, d), w_pad)


SUBLANES = 8


def _sort_descending(x):
    n, lanes = x.shape
    k = 2
    while k <= n:
        j = k // 2
        while j >= 1:
            if j >= SUBLANES and k < n:
                v = x.reshape(n // (2 * k), 2, k // (2 * j), 2, j, lanes)
                a, b = v[:, :, :, 0], v[:, :, :, 1]
                mx, mn = jnp.maximum(a, b), jnp.minimum(a, b)
                lo = jnp.concatenate([mx[:, 0:1], mn[:, 1:2]], axis=1)
                hi = jnp.concatenate([mn[:, 0:1], mx[:, 1:2]], axis=1)
                x = jnp.stack([lo, hi], axis=3).reshape(n, lanes)
            elif j >= SUBLANES:
                v = x.reshape(n // (2 * j), 2, j, lanes)
                x = jnp.stack([jnp.maximum(v[:, 0], v[:, 1]), jnp.minimum(v[:, 0], v[:, 1])], axis=1)
                x = x.reshape(n, lanes)
            else:
                v = x.reshape(n // SUBLANES, SUBLANES, lanes)
                s = lax.broadcasted_iota(I32, v.shape, 0) * SUBLANES + lax.broadcasted_iota(I32, v.shape, 1)
                is_lo = (s & j) == 0
                partner = jnp.where(is_lo, pltpu.roll(v, SUBLANES - j, 1), pltpu.roll(v, j, 1))
                take_max = is_lo == ((s & k) == 0)
                x = jnp.where(take_max, jnp.maximum(v, partner), jnp.minimum(v, partner)).reshape(n, lanes)
            j //= 2
        k *= 2
    return x


def _topk_kernel(cap, n_e, aff_ref, slot_ref, wsel_ref, slot_t_ref):
    n_rows, n_tok = aff_ref.shape
    a = aff_ref[...]
    token_major = jnp.concatenate([a, jnp.zeros((LANES - n_rows, n_tok), F32)], axis=0).T
    kth = _sort_descending(token_major)[cap - 1:cap, :]
    thr = jnp.broadcast_to(kth, (SUBLANES, LANES)).T[:n_rows, 0:1]
    gt = a > thr
    eq = a == thr
    need = cap - jnp.sum(gt.astype(I32), axis=1, keepdims=True)
    upper = (lax.broadcasted_iota(I32, (n_tok, n_tok), 0) < lax.broadcasted_iota(I32, (n_tok, n_tok), 1))
    upper = upper.astype(BF16)
    eq_rank = jnp.dot(eq.astype(BF16), upper, preferred_element_type=F32)
    sel = jnp.logical_or(gt, jnp.logical_and(eq, eq_rank < need.astype(F32)))
    rank = jnp.dot(sel.astype(BF16), upper, preferred_element_type=F32)
    slot = jnp.where(sel, rank, -1.0)
    slot_ref[...] = slot.astype(I32)
    wsel_ref[...] = jnp.where(sel, a, 0.0)
    pad = jnp.full((LANES - n_e, n_tok), -1.0, F32)
    for b in range(n_rows // n_e):
        slot_t_ref[b] = jnp.concatenate([slot[b * n_e:(b + 1) * n_e], pad], axis=0).T.astype(I32)


def _topk_select(aff2d, cap, n_e):
    rows, n_tok = aff2d.shape
    spec = pl.BlockSpec((rows, n_tok), lambda i: (0, 0))
    spec_t = pl.BlockSpec((rows // n_e, n_tok, LANES), lambda i: (0, 0, 0))
    return pl.pallas_call(
        functools.partial(_topk_kernel, cap, n_e),
        grid=(1,),
        in_specs=[spec],
        out_specs=[spec, spec, spec_t],
        out_shape=[jax.ShapeDtypeStruct((rows, n_tok), I32), jax.ShapeDtypeStruct((rows, n_tok), F32),
                   jax.ShapeDtypeStruct((rows // n_e, n_tok, LANES), I32)],
        compiler_params=_params("arbitrary"),
        name="moe_topk_select",
    )(aff2d)


def _gather_kernel(cap, slot_ref, wsel_ref, hm_ref, xe_ref, gs_ref):
    n_tok = hm_ref.shape[0]
    onehot = slot_ref[0] == lax.broadcasted_iota(I32, (cap, n_tok), 0)
    xe_ref[0] = jnp.dot(onehot.astype(BF16), hm_ref[...], preferred_element_type=F32).astype(BF16)
    gs_ref[0] = jnp.sum(jnp.where(onehot, wsel_ref[0], 0.0), axis=1, keepdims=True)


def _gather_tokens(slot, wsel, hm, batch, seq, cap):
    rows = batch * N_EXPERTS
    row_spec = pl.BlockSpec((1, 1, seq), lambda b, e: (b * N_EXPERTS + e, 0, 0))
    return pl.pallas_call(
        functools.partial(_gather_kernel, cap),
        grid=(batch, N_EXPERTS),
        in_specs=[row_spec, row_spec, pl.BlockSpec((seq, D_MODEL), lambda b, e: (b, 0))],
        out_specs=[
            pl.BlockSpec((1, cap, D_MODEL), lambda b, e: (e, b, 0)),
            pl.BlockSpec((1, cap, 1), lambda b, e: (e, b, 0)),
        ],
        out_shape=[
            jax.ShapeDtypeStruct((N_EXPERTS, batch * cap, D_MODEL), BF16),
            jax.ShapeDtypeStruct((N_EXPERTS, batch * cap, 1), F32),
        ],
        compiler_params=_params("parallel", "arbitrary"),
        name="moe_gather",
    )(slot.reshape(rows, 1, seq), wsel.reshape(rows, 1, seq), hm)


def _expert_kernel(n_e, n_t, xe_ref, wg_ref, wu_ref, wd_ref, gs_ref, ye_ref, mid_ref):
    g = pl.program_id(0)
    s = pl.program_id(1)
    t = s // 2
    tf = wg_ref.shape[-1]

    @pl.when(jnp.logical_and(s % 2 == 0, g < n_e))
    def _():
        x = xe_ref[0]
        a = jnp.dot(x, wg_ref[...].astype(BF16), preferred_element_type=F32)
        u = jnp.dot(x, wu_ref[...].astype(BF16), preferred_element_type=F32)
        mid_ref[g % 2, t] = (a * jax.nn.sigmoid(a) * u).astype(BF16)

    @pl.when(jnp.logical_and(s % 2 == 1, g >= 1))
    def _():
        prev = (g - 1) % 2
        acc = jnp.dot(mid_ref[prev, 0], wd_ref[0:tf, :].astype(BF16), preferred_element_type=F32)
        for f in range(1, n_t):
            acc += jnp.dot(mid_ref[prev, f], wd_ref[f * tf:(f + 1) * tf, :].astype(BF16),
                           preferred_element_type=F32)
        ye_ref[0] = (acc * gs_ref[0]).astype(BF16)


def _expert_ffn(xe, gs, w_gate, w_up, w_down, layer, tf):
    n_e, rows, d = xe.shape
    ff = w_gate.shape[-1]
    n_t = ff // tf
    assert d // tf == n_t
    last = n_e - 1

    n_s = 2 * n_t

    def fill_tile(g, s, lead):
        v = jnp.minimum(g * n_s + s + lead, n_e * n_s - 1)
        return (layer, v // n_s, 0, (v % n_s) // 2)

    def emit_tile(g, s):
        v = jnp.maximum(g * n_s + s - 1 - n_s, 0)
        return (layer, v // n_s, 0, (v % n_s) // 2)

    return pl.pallas_call(
        functools.partial(_expert_kernel, n_e, n_t),
        grid=(n_e + 1, n_s),
        in_specs=[
            pl.BlockSpec((1, rows, d), lambda g, s: (jnp.minimum(g, last), 0, 0)),
            pl.BlockSpec((None, None, d, tf), functools.partial(fill_tile, lead=1)),
            pl.BlockSpec((None, None, d, tf), functools.partial(fill_tile, lead=0)),
            pl.BlockSpec((None, None, ff, tf), emit_tile),
            pl.BlockSpec((1, rows, 1), lambda g, s: (jnp.maximum(g - 1, 0), 0, 0)),
        ],
        out_specs=pl.BlockSpec((1, rows, tf), lambda g, s: emit_tile(g, s)[1:]),
        out_shape=jax.ShapeDtypeStruct((n_e, rows, d), BF16),
        scratch_shapes=[pltpu.VMEM((2, n_t, rows, tf), BF16)],
        compiler_params=_params("arbitrary", "arbitrary"),
        name="moe_expert_ffn",
    )(xe, w_gate, w_up, w_down, gs)


def _scatter_kernel(cap, slot_ref, ye_ref, x_ref, *rest):
    o_ref = rest[-1]
    n_e = ye_ref.shape[0]
    col = lax.broadcasted_iota(I32, (x_ref.shape[0], cap), 1)
    slots = slot_ref[0]
    onehot = jnp.concatenate([(slots[:, e:e + 1] == col).astype(BF16) for e in range(n_e)], axis=1)
    ye = ye_ref[...].reshape(n_e * cap, ye_ref.shape[-1])
    y = x_ref[...] + jnp.dot(onehot, ye, preferred_element_type=F32)
    o_ref[...] = _rms(y, rest[0][...]) if len(rest) == 2 else y


def _scatter_add(slot_t, ye, x2d, batch, seq, cap, tr, final_gain=None):
    d = x2d.shape[-1]
    per_b = seq // tr
    in_specs = [
        pl.BlockSpec((1, tr, LANES), lambda b, i: (b, i, 0)),
        pl.BlockSpec((N_EXPERTS, cap, d), lambda b, i: (0, b, 0)),
        pl.BlockSpec((tr, d), lambda b, i: (b * per_b + i, 0)),
    ]
    args = [slot_t, ye, x2d]
    if final_gain is not None:
        in_specs.append(pl.BlockSpec((1, d), lambda b, i: (0, 0)))
        args.append(final_gain.reshape(1, d))
    return pl.pallas_call(
        functools.partial(_scatter_kernel, cap),
        grid=(batch, per_b),
        in_specs=in_specs,
        out_specs=pl.BlockSpec((tr, d), lambda b, i: (b * per_b + i, 0)),
        out_shape=jax.ShapeDtypeStruct(x2d.shape, F32),
        compiler_params=_params("parallel", "arbitrary"),
        name="moe_scatter_add",
    )(*args)


def _expert_choice_ffn(x2d, hm, aff, w_gate, w_up, w_down, layer, batch, seq, final_gain=None):
    cap = EC_CAPACITY * seq // N_EXPERTS
    slot, wsel, slot_t = _topk_select(aff.reshape(batch * N_EXPERTS, seq), cap, N_EXPERTS)
    xe, gs = _gather_tokens(slot, wsel, hm, batch, seq, cap)
    ye = _expert_ffn(xe, gs, w_gate, w_up, w_down, layer, 512)
    return _scatter_add(slot_t, ye, x2d, batch, seq, cap, 256, final_gain)


def kernel(x, mem, norm_mix, w_in, na_rpb, hy_conv_w, hy_conv_b, hy_filt_w1, hy_filt_b1, hy_filt_w2, hy_filt_b2, hy_filt_w3, hy_sin_freq, hy_skip_d, branch_norm, w_out, norm_cross, mem_norm, w_cq, w_ckv, w_co, norm_moe, w_router, w_gate, w_up, w_down, final_norm):
    batch, seq, d = x.shape
    n_mem = mem.shape[1]
    depth = w_in.shape[0]
    m = batch * seq
    na_cols = 3 * NA_W
    hy_cols = 3 * HY_W
    rows = seq // GRID_W

    dft_fwd, dft_inv = _dft_tables(seq)
    x2d = x.reshape(m, d)
    mem2d = mem.reshape(batch * n_mem, d)

    w_out_b, w_cq_b, w_co_b = (w.astype(BF16) for w in (w_out, w_cq, w_co))
    p_in = w_in.shape[-1]
    kv = _norm_matmul(mem2d, mem_norm, (w_ckv, None, 0, w_ckv.shape[-1]), batch * n_mem, 1024, BF16,
                      "cross_kv_proj")

    for l in range(depth):
        proj = _norm_matmul(x2d, norm_mix[l], _whole(w_in, l), 1024, 768, BF16, "in_proj")
        proj = proj.reshape(batch, seq, p_in)

        y_na = _neighbourhood_attention(proj, _na_bias_table(na_rpb[l], rows), batch, seq)
        y_hy = _hyena_mixer(proj, na_cols, hy_conv_w[l], hy_conv_b[l], hy_filt_w1[l],
                            hy_filt_b1[l], hy_filt_w2[l], hy_filt_b2[l], hy_filt_w3[l], hy_sin_freq[l],
                            hy_skip_d[l], dft_fwd, dft_inv, batch, seq)
        y_ret = _retention_mixer(proj, na_cols + hy_cols, batch, seq)
        x2d, q = _branch_out_cross_q(y_na.reshape(m, NA_W), y_hy.reshape(m, HY_W), y_ret.reshape(m, RET_W),
                                     branch_norm[l], w_out_b, x2d, norm_cross[l], w_cq_b, l, 512)
        x2d, hm, aff = _cross_out_and_router(q, kv, w_co_b, l, x2d, norm_moe[l], w_router[l], batch, seq, n_mem,
                                             256)

        x2d = _expert_choice_ffn(x2d, hm, aff, w_gate, w_up, w_down, l, batch, seq,
                                 final_gain=final_norm if l == depth - 1 else None)

    return x2d.reshape(batch, seq, d)
```

```python
import functools
import math

import numpy as np
import jax
import jax.numpy as jnp
from jax import lax
from jax.experimental import pallas as pl
from jax.experimental.pallas import tpu as pltpu

F32 = jnp.float32
BF16 = jnp.bfloat16
I32 = jnp.int32

D_MODEL = 2048
GRID_W = 64
NA_HEAD_DIM = 64
NA_W = 768
NA_HEADS = 12
NA_KR = 8
NA_KC = 16
HY_W = 512
HY_ORDER = 2
HY_BANDS = 8
HY_POS_DIM = 17
HY_FILT_FF = 64
RET_HEAD_DIM = 128
RET_W = 768
RET_HEADS = 6
RET_CHUNK = 128
CROSS_HEADS = 4
CROSS_HEAD_DIM = 512
N_EXPERTS = 16
EXPERT_FF = 2048
EC_CAPACITY = 2
RMS_EPS = 1e-6
GN_EPS = 1e-5

MASK_VALUE = -1e30
LANES = 128
VMEM_LIMIT_BYTES = 56 * 1024 * 1024

NT_DIMS = (((1,), (1,)), ((), ()))


def _params(*sem):
    return pltpu.CompilerParams(dimension_semantics=sem, vmem_limit_bytes=VMEM_LIMIT_BYTES)


def _rms(xf, g):
    return xf * lax.rsqrt(jnp.mean(xf * xf, axis=-1, keepdims=True) + RMS_EPS) * g


def _mm_kernel(prologue, n_pro, has_res, *refs):
    pro_refs = refs[:n_pro]
    w_ref = refs[n_pro]
    res_ref = refs[n_pro + 1] if has_res else None
    o_ref = refs[n_pro + 1 + has_res]
    h_ref = refs[n_pro + 2 + has_res]

    @pl.when(pl.program_id(1) == 0)
    def _():
        prologue(h_ref, *pro_refs)

    w = w_ref[...]
    if w.dtype != BF16:
        w = w.astype(BF16)
    acc = jnp.dot(h_ref[...], w, preferred_element_type=F32)
    if has_res:
        acc = acc + res_ref[...]
    o_ref[...] = acc.astype(o_ref.dtype)


def _fused_matmul(prologue, pro_args, pro_specs, wsel, res, m, tm, tn, out_dtype, name):
    w, layer, col0, n = wsel
    k = w.shape[1]
    cb0 = col0 // tn
    assert col0 % tn == 0 and n % tn == 0 and m % tm == 0
    has_res = res is not None
    if layer is None:
        per_layer = n // tn
        n = n * w.shape[0]
        w_spec = pl.BlockSpec((None, k, tn), lambda i, j: (j // per_layer, 0, cb0 + j % per_layer))
    else:
        w_spec = pl.BlockSpec((None, k, tn), lambda i, j: (layer, 0, cb0 + j))
    in_specs = list(pro_specs) + [w_spec]
    args = list(pro_args) + [w]
    if has_res:
        in_specs.append(pl.BlockSpec((tm, tn), lambda i, j: (i, j)))
        args.append(res)
    return pl.pallas_call(
        functools.partial(_mm_kernel, prologue, len(pro_args), has_res),
        grid=(m // tm, n // tn),
        in_specs=in_specs,
        out_specs=pl.BlockSpec((tm, tn), lambda i, j: (i, j)),
        out_shape=jax.ShapeDtypeStruct((m, n), out_dtype),
        scratch_shapes=[pltpu.VMEM((tm, k), BF16)],
        compiler_params=_params("parallel", "arbitrary"),
        name=name,
    )(*args)


def _rms_prologue(h_ref, x_ref, g_ref):
    h_ref[...] = _rms(x_ref[...], g_ref[...]).astype(BF16)


def _whole(w, layer):
    return (w, layer, 0, w.shape[-1])


def _norm_matmul(x2d, gain, wsel, tm, tn, out_dtype, name, res=None):
    m, k = x2d.shape
    specs = [pl.BlockSpec((tm, k), lambda i, j: (i, 0)), pl.BlockSpec((1, k), lambda i, j: (0, 0))]
    return _fused_matmul(_rms_prologue, [x2d, gain.reshape(1, k)], specs, wsel, res, m, tm, tn, out_dtype, name)


def _norm_proj_resident_kernel(x_ref, g_ref, w_ref, o_ref, wb_ref):
    @pl.when(pl.program_id(0) == 0)
    def _():
        wb_ref[...] = w_ref[...].astype(BF16)

    h = _rms(x_ref[...], g_ref[...]).astype(BF16)
    o_ref[...] = jnp.dot(h, wb_ref[...], preferred_element_type=F32).astype(o_ref.dtype)


def _norm_proj_resident(x2d, gain, w, layer, slab, n_slab, tm, name):
    m, k = x2d.shape
    width = w.shape[-1] // n_slab
    assert w.shape[-1] % n_slab == 0 and m % tm == 0
    return pl.pallas_call(
        _norm_proj_resident_kernel,
        grid=(m // tm,),
        in_specs=[
            pl.BlockSpec((tm, k), lambda i: (i, 0)),
            pl.BlockSpec((1, k), lambda i: (0, 0)),
            pl.BlockSpec((None, k, width), lambda i: (layer, 0, slab)),
        ],
        out_specs=pl.BlockSpec((tm, width), lambda i: (i, 0)),
        out_shape=jax.ShapeDtypeStruct((m, width), BF16),
        scratch_shapes=[pltpu.VMEM((k, width), BF16)],
        compiler_params=_params("arbitrary"),
        name=name,
    )(x2d, gain.reshape(1, k), w)


def _cross_prologue(h_ref, q_ref, k_ref, v_ref):
    dh = CROSS_HEAD_DIM
    for h in range(CROSS_HEADS):
        sl = slice(h * dh, (h + 1) * dh)
        s = lax.dot_general(q_ref[:, sl], k_ref[:, sl], NT_DIMS, preferred_element_type=F32) * (dh ** -0.5)
        m = jnp.max(s, axis=-1, keepdims=True)
        p = jnp.exp(s - m)
        l = jnp.sum(p, axis=-1, keepdims=True)
        o = jnp.dot(p.astype(BF16), v_ref[:, sl], preferred_element_type=F32) * (1.0 / l)
        h_ref[:, sl] = o.astype(BF16)


def _branch_prologue(h_ref, na_ref, hy_ref, ret_ref, g_ref):
    def nrm(p):
        return p * lax.rsqrt(jnp.mean(p * p, axis=-1, keepdims=True) + RMS_EPS)

    y = jnp.concatenate([nrm(r[...].astype(F32)) for r in (na_ref, hy_ref, ret_ref)], axis=-1)
    h_ref[...] = (y * g_ref[...]).astype(BF16)


def _branch_out_cross_q_kernel(na_ref, hy_ref, ret_ref, g_ref, wo_ref, res_ref, gq_ref, wq_ref, x_ref, q_ref,
                               h_ref):
    _branch_prologue(h_ref, na_ref, hy_ref, ret_ref, g_ref)
    x = res_ref[...] + jnp.dot(h_ref[...], wo_ref[...], preferred_element_type=F32)
    x_ref[...] = x
    hq = _rms(x, gq_ref[...]).astype(BF16)
    q_ref[...] = jnp.dot(hq, wq_ref[...], preferred_element_type=F32).astype(BF16)


def _branch_out_cross_q(y_na, y_hy, y_ret, gain, w_out, res, gain_q, w_cq, layer, tm):
    m, d = res.shape
    rows = lambda width: pl.BlockSpec((tm, width), lambda i: (i, 0))
    vec = pl.BlockSpec((1, d), lambda i: (0, 0))
    square = pl.BlockSpec((None, d, d), lambda i: (layer, 0, 0))
    return pl.pallas_call(
        _branch_out_cross_q_kernel,
        grid=(m // tm,),
        in_specs=[rows(NA_W), rows(HY_W), rows(RET_W), vec, square, rows(d), vec, square],
        out_specs=[rows(d), rows(d)],
        out_shape=[jax.ShapeDtypeStruct((m, d), F32), jax.ShapeDtypeStruct((m, d), BF16)],
        scratch_shapes=[pltpu.VMEM((tm, d), BF16)],
        compiler_params=_params("parallel"),
        name="branch_out_cross_q",
    )(y_na, y_hy, y_ret, gain.reshape(1, d), w_out, res, gain_q.reshape(1, d), w_cq)


NA_PAIR = 2
NA_BAND = NA_KR + NA_PAIR


def _na_band_base(r, rows):
    return np.clip(r - NA_KR // 2, 0, rows - NA_BAND)


@functools.lru_cache(maxsize=None)
def _na_variants(rows):
    assert rows % NA_PAIR == 0 and rows >= NA_BAND + 2
    n_var = NA_KR // 2 + 1
    dr = np.full((n_var, NA_BAND, NA_PAIR), -2, np.int64)
    for r in range(0, rows, NA_PAIR):
        base = int(_na_band_base(r, rows))
        v = (r - base) // 2
        for j in range(NA_PAIR):
            rs = int(np.clip(r + j - NA_KR // 2, 0, rows - NA_KR))
            for i in range(NA_BAND):
                val = base + i - (r + j) + (NA_KR - 1) if rs <= base + i < rs + NA_KR else -1
                assert dr[v, i, j] in (-2, val)
                dr[v, i, j] = val
    assert (dr > -2).all()
    return dr


def _na_bias_table(rpb, rows):
    c = np.arange(GRID_W)
    col_start = np.clip(c - NA_KC // 2, 0, GRID_W - NA_KC)
    col_in = (c[None, :] >= col_start[:, None]) & (c[None, :] < col_start[:, None] + NA_KC)
    dc = np.clip(c[None, :] - c[:, None] + (NA_KC - 1), 0, 2 * NA_KC - 2)
    onehot = (dc.T.reshape(-1)[None, :] == np.arange(2 * NA_KC - 1)[:, None]).astype(np.float32)
    cols = jnp.einsum("hab,bn->han", rpb.astype(F32), jnp.asarray(onehot), precision=lax.Precision.HIGHEST)
    cols = cols.reshape(NA_HEADS, 2 * NA_KR - 1, GRID_W, GRID_W)
    cols = jnp.where(col_in.T[None, None], cols, MASK_VALUE)
    dr = _na_variants(rows)
    n_var = dr.shape[0]
    return pl.pallas_call(
        functools.partial(_na_table_kernel, dr),
        grid=(NA_HEADS,),
        in_specs=[pl.BlockSpec((1, 2 * NA_KR - 1, GRID_W, GRID_W), lambda h: (h, 0, 0, 0))],
        out_specs=pl.BlockSpec((1, n_var, NA_BAND * GRID_W, NA_PAIR * GRID_W), lambda h: (h, 0, 0, 0)),
        out_shape=jax.ShapeDtypeStruct((NA_HEADS, n_var, NA_BAND * GRID_W, NA_PAIR * GRID_W), F32),
        compiler_params=_params("parallel"),
        name="na_bias_table",
    )(cols)


def _na_table_kernel(dr, cols_ref, o_ref):
    masked = jnp.full((GRID_W, GRID_W), MASK_VALUE, F32)
    for v in range(dr.shape[0]):
        for i in range(NA_BAND):
            blocks = [cols_ref[0, int(dr[v, i, j])] if dr[v, i, j] >= 0 else masked for j in range(NA_PAIR)]
            o_ref[0, v, i * GRID_W:(i + 1) * GRID_W, :] = jnp.concatenate(blocks, axis=-1)


def _na_kernel(rows, q_ref, k_ref, v_ref, bias_ref, o_ref, vt_ref, s0_ref, s1_ref, p0_ref, p1_ref, l0_ref,
               l1_ref):
    dh = NA_HEAD_DIM
    nq = NA_PAIR * GRID_W
    nk = NA_BAND * GRID_W
    n_chunk = nk // LANES
    lane = lax.broadcasted_iota(I32, (nq, 2 * dh), 1)

    for ch in range(vt_ref.shape[0]):
        vt_ref[ch] = v_ref[0, ch * LANES:(ch + 1) * LANES, :].astype(F32).T.astype(BF16)

    n_steps = rows // NA_PAIR

    def band_base(p):
        return jnp.clip(p * NA_PAIR - NA_KR // 2, 0, rows - NA_BAND)


    def scores(p, s_out):
        r = p * NA_PAIR
        base = band_base(p)
        variant = (r - base) // 2
        q = q_ref[0, pl.ds(pl.multiple_of(r * GRID_W, nq), nq), :] * (dh ** -0.5)
        kb = k_ref[0, pl.ds(pl.multiple_of(base * GRID_W, LANES), nk), :]
        zero = jnp.zeros_like(q)
        q2 = jnp.concatenate([jnp.where(lane < dh, q, zero), jnp.where(lane >= dh, q, zero)], axis=0)
        st2 = lax.dot_general(kb, q2, NT_DIMS, preferred_element_type=F32)
        for hh in range(2):
            s_out[:, hh * nq:(hh + 1) * nq] = st2[:, hh * nq:(hh + 1) * nq] + bias_ref[hh, variant]

    def softmax(s_in, p_out, l_out):
        st = s_in[...]
        pt = jnp.exp(st - jnp.max(st, axis=0, keepdims=True))
        p_out[...] = pt.astype(BF16)
        l_out[...] = 1.0 / jnp.sum(pt, axis=0, keepdims=True)

    def values(p, p_in, l_in):
        r = p * NA_PAIR
        c0 = band_base(p) // 2
        vt = jnp.concatenate([vt_ref[c0 + i] for i in range(n_chunk)], axis=1)
        pt = p_in[...]
        linv = l_in[...]
        outs = []
        for hh in range(2):
            cols = slice(hh * nq, (hh + 1) * nq)
            ot = jnp.dot(vt[hh * dh:(hh + 1) * dh, :], pt[:, cols], preferred_element_type=F32)
            outs.append(ot * linv[:, cols])
        o_ref[0, pl.ds(pl.multiple_of(r * GRID_W, nq), nq), :] = jnp.concatenate(outs, axis=0).T.astype(BF16)

    s_slots = (s0_ref, s1_ref)
    p_slots = (p0_ref, p1_ref)
    l_slots = (l0_ref, l1_ref)

    def step(t, parity, do_scores=True, do_softmax=True, do_values=True):
        a, b = parity, 1 - parity
        t = jnp.asarray(t, I32)
        if do_scores:
            scores(t, s_slots[a])
        if do_softmax:
            softmax(s_slots[b], p_slots[b], l_slots[b])
        if do_values:
            values(t - 2, p_slots[a], l_slots[a])

    assert n_steps % 2 == 0 and n_steps >= 4
    step(0, 0, do_softmax=False, do_values=False)
    step(1, 1, do_values=False)

    def steady(i, carry):
        t = 2 + 2 * i
        step(t, 0)
        step(t + 1, 1)
        return carry

    lax.fori_loop(0, (n_steps - 2) // 2, steady, 0, unroll=True)
    step(n_steps, 0, do_scores=False)
    step(n_steps + 1, 1, do_scores=False, do_softmax=False)


def _neighbourhood_attention(proj_na, bias_tbl, batch, seq):
    rows = seq // GRID_W
    n_pairs = NA_HEADS // 2
    blk = (1, seq, 2 * NA_HEAD_DIM)
    n_var, nk, nq = bias_tbl.shape[1:]
    return pl.pallas_call(
        functools.partial(_na_kernel, rows),
        grid=(batch, n_pairs),
        in_specs=[
            pl.BlockSpec(blk, lambda b, h: (b, 0, h)),
            pl.BlockSpec(blk, lambda b, h: (b, 0, n_pairs + h)),
            pl.BlockSpec(blk, lambda b, h: (b, 0, 2 * n_pairs + h)),
            pl.BlockSpec((2, n_var, nk, nq), lambda b, h: (h, 0, 0, 0)),
        ],
        out_specs=pl.BlockSpec(blk, lambda b, h: (b, 0, h)),
        out_shape=jax.ShapeDtypeStruct((batch, seq, NA_W), BF16),
        scratch_shapes=[pltpu.VMEM((seq // LANES, 2 * NA_HEAD_DIM, LANES), BF16),
                        pltpu.VMEM((nk, 2 * nq), F32), pltpu.VMEM((nk, 2 * nq), F32),
                        pltpu.VMEM((nk, 2 * nq), BF16), pltpu.VMEM((nk, 2 * nq), BF16),
                        pltpu.VMEM((1, 2 * nq), F32), pltpu.VMEM((1, 2 * nq), F32)],
        compiler_params=_params("parallel", "arbitrary"),
        name="neighbourhood_attention",
    )(proj_na, proj_na, proj_na, bias_tbl)


@functools.lru_cache(maxsize=None)
def _dft_factors(seq):
    n = 2 * seq
    t = np.arange(seq, dtype=np.int64)
    f1 = np.arange(seq // 64, dtype=np.int64)
    f0 = np.arange(64, dtype=np.int64)
    a = 2.0 * np.pi * ((64 * f1[:, None] * t[None, :]) % n).astype(np.float64) / n
    b = 2.0 * np.pi * ((f0[:, None] * t[None, :]) % n).astype(np.float64) / n
    ny = np.where(t % 2 == 0, 1.0, -1.0)
    return tuple(np.asarray(v, np.float32) for v in (np.cos(a), np.sin(a), np.cos(b), np.sin(b), ny))


def _dft_table_kernel(u_ref, v_ref, cb_ref, sb_ref, fwd_ref, inv_ref):
    sin_half = pl.program_id(0) == 1
    first = pl.program_id(1) == 0
    n_fine, seq = cb_ref.shape
    row = lax.broadcasted_iota(I32, (n_fine, seq), 0)
    col = lax.broadcasted_iota(I32, (n_fine, seq), 1)
    for c in range(u_ref.shape[1]):
        tile = u_ref[0, c:c + 1, :] * cb_ref[...] + v_ref[0, c:c + 1, :] * sb_ref[...]
        rows = slice(c * n_fine, (c + 1) * n_fine)
        fwd = tile
        if c == 0:
            fwd_nyq = jnp.logical_and(jnp.logical_and(sin_half, first), row == 0)
            fwd = jnp.where(fwd_nyq, (1 - 2 * (col & 1)).astype(F32), tile)
        fwd_ref[0, rows, :] = fwd.astype(BF16)
        inv_nyq = jnp.logical_and(sin_half, col == 0)
        inv_ref[rows, :] = jnp.where(inv_nyq, (1 - 2 * (row & 1)).astype(F32), tile).astype(BF16)


def _dft_tables(seq):
    ca, sa, cb, sb, _ = _dft_factors(seq)
    u = np.stack([ca, sa])
    v = np.stack([-sa, ca])
    n_coarse = 8
    rows = 64 * n_coarse
    coarse_spec = pl.BlockSpec((1, n_coarse, seq), lambda h, i: (h, i, 0))
    fine_spec = pl.BlockSpec((64, seq), lambda h, i: (0, 0))
    return pl.pallas_call(
        _dft_table_kernel,
        grid=(2, seq // rows),
        in_specs=[coarse_spec, coarse_spec, fine_spec, fine_spec],
        out_specs=[pl.BlockSpec((1, rows, seq), lambda h, i: (h, i, 0)),
                   pl.BlockSpec((rows, seq), lambda h, i: (i, h))],
        out_shape=[jax.ShapeDtypeStruct((2, seq, seq), BF16), jax.ShapeDtypeStruct((seq, 2 * seq), BF16)],
        compiler_params=_params("parallel", "arbitrary"),
        name="dft_tables",
    )(jnp.asarray(u), jnp.asarray(v), jnp.asarray(cb), jnp.asarray(sb))


@functools.lru_cache(maxsize=None)
def _hyena_consts(seq):
    t = np.arange(seq, dtype=np.float64)
    t01 = t / (seq - 1)
    bands = np.linspace(1e-4, HY_BANDS - 1, HY_BANDS)
    ang = (2.0 * math.pi) * (t[:, None] / seq) * bands[None, :]
    feats = np.concatenate([t01[:, None], np.cos(ang), -np.sin(ang)], axis=-1)
    feats_p = np.zeros((seq, LANES), np.float32)
    feats_p[:, :HY_POS_DIM] = feats
    min_decay = math.log(1e-2) / 1.5
    max_decay = math.log(1e-2) / 0.3
    deltas = np.abs(np.linspace(min_decay, max_decay, HY_W))
    window = np.exp(-t01[:, None] * deltas[None, :]).astype(np.float32)
    return feats_p, window


def _filter_kernel(feats_ref, w1_ref, b1_ref, w2_ref, b2_ref, freq_ref, w3f_ref, w3b_ref, win_ref, sum_ref,
                   diff_ref, hid_ref):
    hp = lax.Precision.HIGHEST

    @pl.when(pl.program_id(0) == 0)
    def _():
        f = freq_ref[...]
        h1 = jnp.sin(f * (jnp.dot(feats_ref[...], w1_ref[...], precision=hp, preferred_element_type=F32)
                          + b1_ref[...]))
        hid_ref[...] = jnp.sin(f * (jnp.dot(h1, w2_ref[...], precision=hp, preferred_element_type=F32)
                                    + b2_ref[...]))

    hid = hid_ref[...]
    win = win_ref[...]
    fwd = jnp.dot(hid, w3f_ref[...], precision=hp, preferred_element_type=F32) * win
    bwd = jnp.dot(hid, w3b_ref[...], precision=hp, preferred_element_type=F32) * win
    bwd = jnp.where(lax.broadcasted_iota(I32, bwd.shape, 0) == 0, 0.0, bwd)
    sum_ref[...] = (fwd + bwd).astype(BF16)
    diff_ref[...] = (fwd - bwd).astype(BF16)


def _hyena_filters_time(w1, b1, w2, b2, w3, freq, seq):
    feats, window = _hyena_consts(seq)
    w1p = jnp.zeros((LANES, HY_FILT_FF), F32).at[:HY_POS_DIM].set(w1)
    full = lambda shape: pl.BlockSpec(shape, lambda o: (0,) * len(shape))
    out_spec = pl.BlockSpec((seq, HY_W), lambda o: (0, o))
    out_shape = jax.ShapeDtypeStruct((seq, HY_ORDER * HY_W), BF16)
    return pl.pallas_call(
        _filter_kernel,
        grid=(HY_ORDER,),
        in_specs=[
            full((seq, LANES)), full((LANES, HY_FILT_FF)), full((1, HY_FILT_FF)),
            full((HY_FILT_FF, HY_FILT_FF)), full((1, HY_FILT_FF)), full((1, HY_FILT_FF)),
            pl.BlockSpec((HY_FILT_FF, HY_W), lambda o: (0, 2 * o)),
            pl.BlockSpec((HY_FILT_FF, HY_W), lambda o: (0, 2 * o + 1)),
            full((seq, HY_W)),
        ],
        out_specs=[out_spec, out_spec],
        out_shape=[out_shape, out_shape],
        scratch_shapes=[pltpu.VMEM((seq, HY_FILT_FF), F32)],
        compiler_params=_params("arbitrary"),
        name="hyena_filter_mlp",
    )(jnp.asarray(feats), w1p, b1.reshape(1, -1), w2, b2.reshape(1, -1), freq.reshape(1, -1), w3, w3,
      jnp.asarray(window))


def _filter_dft_kernel(f_ref, sum_ref, diff_ref, o_ref):
    o_ref[0] = jnp.dot(f_ref[0], sum_ref[...], preferred_element_type=F32)
    o_ref[1] = jnp.dot(f_ref[1], diff_ref[...], preferred_element_type=F32)

    @pl.when(pl.program_id(0) == 0)
    def _():
        top = jnp.dot(f_ref[1, 0:8, :], sum_ref[...], preferred_element_type=F32)
        row = lax.broadcasted_iota(I32, top.shape, 0)
        o_ref[1, 0:8, :] = jnp.where(row == 0, top, o_ref[1, 0:8, :])


def _filter_dft(dft_fwd, filt_sum, filt_diff, seq, fb):
    return pl.pallas_call(
        _filter_dft_kernel,
        grid=(seq // fb, HY_ORDER),
        in_specs=[
            pl.BlockSpec((2, fb, seq), lambda f, o: (0, f, 0)),
            pl.BlockSpec((seq, HY_W), lambda f, o: (0, o)),
            pl.BlockSpec((seq, HY_W), lambda f, o: (0, o)),
        ],
        out_specs=pl.BlockSpec((2, fb, HY_W), lambda f, o: (0, f, o)),
        out_shape=jax.ShapeDtypeStruct((2, seq, HY_ORDER * HY_W), F32),
        compiler_params=_params("parallel", "arbitrary"),
        name="hyena_filter_dft",
    )(dft_fwd, filt_sum, filt_diff)


def _short_conv_kernel(p_ref, w_ref, b_ref, o_ref):
    p = p_ref[0].astype(F32)
    seq = p.shape[0]
    row = lax.broadcasted_iota(I32, p.shape, 0)
    prev = jnp.where(row == 0, 0.0, pltpu.roll(p, 1, 0))
    nxt = jnp.where(row == seq - 1, 0.0, pltpu.roll(p, seq - 1, 0))
    w = w_ref[...]
    o_ref[0] = prev * w[0:1] + p * w[1:2] + nxt * w[2:3] + b_ref[...]


def _short_conv(proj, col0, conv_w, conv_b, batch, seq, tc):
    n_cols = conv_w.shape[-1]
    assert col0 % tc == 0 and n_cols % tc == 0
    cb0 = col0 // tc
    return pl.pallas_call(
        _short_conv_kernel,
        grid=(batch, n_cols // tc),
        in_specs=[
            pl.BlockSpec((1, seq, tc), lambda b, c: (b, 0, cb0 + c)),
            pl.BlockSpec((3, tc), lambda b, c: (0, c)),
            pl.BlockSpec((1, tc), lambda b, c: (0, c)),
        ],
        out_specs=pl.BlockSpec((1, seq, tc), lambda b, c: (b, 0, c)),
        out_shape=jax.ShapeDtypeStruct((batch, seq, n_cols), F32),
        compiler_params=_params("parallel", "arbitrary"),
        name="hyena_short_conv",
    )(proj, conv_w, conv_b.reshape(1, -1))


def _spectrum_kernel(n_fft, f_ref, z_ref, k_ref, o_ref):
    z = z_ref[0].astype(BF16)
    xr = jnp.dot(f_ref[0], z, preferred_element_type=F32)
    xs = jnp.dot(f_ref[1], z, preferred_element_type=F32)
    kr = k_ref[0]
    ks = k_ref[1]
    row = lax.broadcasted_iota(I32, xr.shape, 0)
    edge = jnp.logical_and(row == 0, pl.program_id(0) == 0)
    yr = jnp.where(edge, xr * kr * (1.0 / n_fft), (xr * kr - xs * ks) * (2.0 / n_fft))
    ys = jnp.where(edge, xs * ks * (1.0 / n_fft), (xr * ks + xs * kr) * (2.0 / n_fft))
    o_ref[0, 0] = yr.astype(BF16)
    o_ref[0, 1] = ys.astype(BF16)


def _spectrum_product(dft_fwd, z_arr, z_col, kfreq, order, batch, seq, fb):
    return pl.pallas_call(
        functools.partial(_spectrum_kernel, 2 * seq),
        grid=(seq // fb, batch),
        in_specs=[
            pl.BlockSpec((2, fb, seq), lambda f, b: (0, f, 0)),
            pl.BlockSpec((1, seq, HY_W), lambda f, b: (b, 0, z_col)),
            pl.BlockSpec((2, fb, HY_W), lambda f, b: (0, f, order)),
        ],
        out_specs=pl.BlockSpec((1, 2, fb, HY_W), lambda f, b: (b, 0, f, 0)),
        out_shape=jax.ShapeDtypeStruct((batch, 2, seq, HY_W), BF16),
        compiler_params=_params("parallel", "arbitrary"),
        name="hyena_spectrum",
    )(dft_fwd, z_arr, kfreq)


def _inverse_kernel(ft_ref, y_ref, gate_ref, z_ref, d_ref, o_ref):
    conv = jnp.dot(ft_ref[...], y_ref[0], preferred_element_type=F32)
    o_ref[0] = gate_ref[0] * (conv + d_ref[...] * z_ref[0])


def _inverse_gate(dft_inv, y, gate_arr, gate_col, z_arr, z_col, skip_row, batch, seq, tb):
    y2 = y.reshape(batch, 2 * seq, HY_W)
    return pl.pallas_call(
        _inverse_kernel,
        grid=(seq // tb, batch),
        in_specs=[
            pl.BlockSpec((tb, 2 * seq), lambda t, b: (t, 0)),
            pl.BlockSpec((1, 2 * seq, HY_W), lambda t, b: (b, 0, 0)),
            pl.BlockSpec((1, tb, HY_W), lambda t, b: (b, t, gate_col)),
            pl.BlockSpec((1, tb, HY_W), lambda t, b: (b, t, z_col)),
            pl.BlockSpec((1, HY_W), lambda t, b: (0, 0)),
        ],
        out_specs=pl.BlockSpec((1, tb, HY_W), lambda t, b: (b, t, 0)),
        out_shape=jax.ShapeDtypeStruct((batch, seq, HY_W), F32),
        compiler_params=_params("parallel", "arbitrary"),
        name="hyena_inverse_gate",
    )(dft_inv, y2, gate_arr, z_arr, skip_row.reshape(1, HY_W))


def _hyena_mixer(proj, col0, conv_w, conv_b, w1, b1, w2, b2, w3, freq, skip_d, dft_fwd, dft_inv, batch, seq):
    s = _short_conv(proj, col0, conv_w, conv_b, batch, seq, 768)
    filt_sum, filt_diff = _hyena_filters_time(w1, b1, w2, b2, w3, freq, seq)
    kfreq = _filter_dft(dft_fwd, filt_sum, filt_diff, seq, 512)
    z_arr, z_col = s, 2
    for o in range(HY_ORDER):
        y = _spectrum_product(dft_fwd, z_arr, z_col, kfreq, o, batch, seq, 1024)
        z_arr = _inverse_gate(dft_inv, y, s, o, z_arr, z_col, skip_d[o], batch, seq, 1024)
        z_col = 0
    return z_arr


@functools.lru_cache(maxsize=None)
def _retention_consts(seq):
    c = RET_CHUNK
    half = RET_HEAD_DIM // 2
    inv = 1.0 / (10000.0 ** np.linspace(0.0, 1.0, half))
    ang = np.arange(seq, dtype=np.float64)[:, None] * inv[None, :]
    cos2 = np.concatenate([np.cos(ang), np.cos(ang)], axis=-1).astype(np.float32)
    sin2 = np.concatenate([-np.sin(ang), np.sin(ang)], axis=-1).astype(np.float32)
    hidx = np.arange(RET_HEADS, dtype=np.float64)
    lg_f = np.log1p(-np.exp2(-5.0 - hidx))[:, None, None]
    lg_b = np.log1p(-np.exp2(-5.5 - hidx))[:, None, None]
    i = np.arange(c, dtype=np.float64)
    diff = i[:, None] - i[None, :]
    ones = np.ones((1, c, c))
    dec = np.where(diff >= 0, np.exp(lg_f * np.maximum(diff, 0.0)), np.exp(lg_b * np.maximum(-diff, 0.0)))
    rowv = lambda v: v[:, :, None] * ones
    tab = np.stack([
        dec,
        rowv(np.exp(lg_f[:, :, 0] * (i + 1.0)[None, :])),
        rowv(np.exp(lg_f[:, :, 0] * (c - 1.0 - i)[None, :])),
        rowv(np.exp(lg_b[:, :, 0] * (c - i)[None, :])),
        rowv(np.exp(lg_b[:, :, 0] * i[None, :])),
        np.exp(lg_f * c) * ones,
        np.exp(lg_b * c) * ones,
    ], axis=1).astype(np.float32)
    return cos2, sin2, tab


def _retention_kernel(q_ref, k_ref, v_ref, g_ref, cos_ref, sin_ref, tab_ref, o_ref, qs_ref, ks_ref, kvf_ref,
                      kvb_ref, a0_ref, a1_ref, y0_ref, y1_ref):
    c = RET_CHUNK
    d = RET_HEAD_DIM
    seq = q_ref.shape[1]
    n_chunks = seq // c
    cos = cos_ref[...]
    sin = sin_ref[...]
    q = q_ref[0].astype(F32)
    k = k_ref[0].astype(F32)
    qs_ref[...] = (q * cos + pltpu.roll(q, d // 2, 1) * sin) * (d ** -0.5)
    ks_ref[...] = k * cos + pltpu.roll(k, d // 2, 1) * sin

    def mm(a, b):
        return jnp.dot(a.astype(BF16), b.astype(BF16), preferred_element_type=F32)

    def chunk(n):
        return pl.ds(pl.multiple_of(n * c, c), c)

    def kv_body(n, carry):
        kc = ks_ref[chunk(n), :]
        vc = v_ref[0, chunk(n), :]
        kvf_ref[n] = mm((kc * tab_ref[0, 2]).T, vc)
        kvb_ref[n] = mm((kc * tab_ref[0, 4]).T, vc)
        return carry

    lax.fori_loop(0, n_chunks, kv_body, 0, unroll=True)

    def scan_fwd(n, state):
        kv = kvf_ref[n]
        kvf_ref[n] = state
        return tab_ref[0, 5] * state + kv

    lax.fori_loop(0, n_chunks, scan_fwd, jnp.zeros((d, d), F32))

    def scan_bwd(m, state):
        n = n_chunks - 1 - m
        kv = kvb_ref[n]
        kvb_ref[n] = state
        return tab_ref[0, 6] * state + kv

    lax.fori_loop(0, n_chunks, scan_bwd, jnp.zeros((d, d), F32))

    def in_chunk(n, a_out):
        qc = qs_ref[chunk(n), :].astype(BF16)
        kc = ks_ref[chunk(n), :].astype(BF16)
        a = lax.dot_general(qc, kc, NT_DIMS, preferred_element_type=F32) * tab_ref[0, 0]
        a_out[...] = a.astype(BF16)

    def mix(n, a_in, y_out):
        qc = qs_ref[chunk(n), :]
        y_out[...] = (jnp.dot(a_in[...], v_ref[0, chunk(n), :], preferred_element_type=F32)
                      + mm(qc * tab_ref[0, 1], kvf_ref[n]) + mm(qc * tab_ref[0, 3], kvb_ref[n]))

    def finish(n, y_in):
        y = y_in[...]
        mu = jnp.mean(y, axis=-1, keepdims=True)
        yc = y - mu
        var = jnp.mean(yc * yc, axis=-1, keepdims=True)
        g = g_ref[0, chunk(n), :].astype(F32)
        o_ref[0, chunk(n), :] = (yc * lax.rsqrt(var + GN_EPS) * (g * jax.nn.sigmoid(g))).astype(BF16)

    a_slots = (a0_ref, a1_ref)
    y_slots = (y0_ref, y1_ref)

    def step(t, parity, do_a=True, do_mix=True, do_finish=True):
        t = jnp.asarray(t, I32)
        if do_a:
            in_chunk(t, a_slots[parity])
        if do_mix:
            mix(t - 1, a_slots[1 - parity], y_slots[1 - parity])
        if do_finish:
            finish(t - 2, y_slots[parity])

    assert n_chunks % 2 == 0 and n_chunks >= 4
    step(0, 0, do_mix=False, do_finish=False)
    step(1, 1, do_finish=False)

    def steady(i, carry):
        step(2 + 2 * i, 0)
        step(3 + 2 * i, 1)
        return carry

    lax.fori_loop(0, (n_chunks - 2) // 2, steady, 0, unroll=True)
    step(n_chunks, 0, do_a=False)
    step(n_chunks + 1, 1, do_a=False, do_mix=False)


def _retention_mixer(proj_q, col_q, proj, col0, batch, seq):
    cos2, sin2, tab = _retention_consts(seq)
    blk = (1, seq, RET_HEAD_DIM)
    h_ = RET_HEADS
    assert col0 % RET_HEAD_DIM == 0 and col_q % RET_HEAD_DIM == 0
    cq = col_q // RET_HEAD_DIM
    c0 = col0 // RET_HEAD_DIM - h_
    return pl.pallas_call(
        _retention_kernel,
        grid=(batch, RET_HEADS),
        in_specs=[
            pl.BlockSpec(blk, lambda b, h: (b, 0, cq + h)),
            pl.BlockSpec(blk, lambda b, h: (b, 0, c0 + h_ + h)),
            pl.BlockSpec(blk, lambda b, h: (b, 0, c0 + 2 * h_ + h)),
            pl.BlockSpec(blk, lambda b, h: (b, 0, c0 + 3 * h_ + h)),
            pl.BlockSpec((seq, RET_HEAD_DIM), lambda b, h: (0, 0)),
            pl.BlockSpec((seq, RET_HEAD_DIM), lambda b, h: (0, 0)),
            pl.BlockSpec((1, 7, RET_CHUNK, RET_CHUNK), lambda b, h: (h, 0, 0, 0)),
        ],
        out_specs=pl.BlockSpec(blk, lambda b, h: (b, 0, h)),
        out_shape=jax.ShapeDtypeStruct((batch, seq, RET_W), BF16),
        scratch_shapes=[pltpu.VMEM((seq, RET_HEAD_DIM), F32), pltpu.VMEM((seq, RET_HEAD_DIM), F32),
                        pltpu.VMEM((seq // RET_CHUNK, RET_HEAD_DIM, RET_HEAD_DIM), F32),
                        pltpu.VMEM((seq // RET_CHUNK, RET_HEAD_DIM, RET_HEAD_DIM), F32),
                        pltpu.VMEM((RET_CHUNK, RET_CHUNK), BF16), pltpu.VMEM((RET_CHUNK, RET_CHUNK), BF16),
                        pltpu.VMEM((RET_CHUNK, RET_HEAD_DIM), F32), pltpu.VMEM((RET_CHUNK, RET_HEAD_DIM), F32)],
        compiler_params=_params("parallel", "arbitrary"),
        name="retention",
    )(proj_q, proj, proj, proj, jnp.asarray(cos2), jnp.asarray(sin2), jnp.asarray(tab))


def _cross_out_router_kernel(q_ref, k_ref, v_ref, w_ref, res_ref, g_ref, wr_ref, x_ref, hm_ref, aff_ref, h_ref):
    n_e = aff_ref.shape[1]
    _cross_prologue(h_ref, q_ref, k_ref, v_ref)
    x = res_ref[...] + jnp.dot(h_ref[...], w_ref[...], preferred_element_type=F32)
    x_ref[...] = x
    h = _rms(x, g_ref[...])
    h_hi = h.astype(BF16)
    hm_ref[...] = h_hi
    h_lo = (h - h_hi.astype(F32)).astype(BF16)
    wr = wr_ref[...]
    w_hi = wr.astype(BF16)
    w_lo = (wr - w_hi.astype(F32)).astype(BF16)
    both = jnp.dot(h_hi, jnp.concatenate([w_hi, w_lo], axis=1), preferred_element_type=F32)
    logits = (both[:, :LANES] + both[:, LANES:] + jnp.dot(h_lo, w_hi, preferred_element_type=F32)).T[:n_e]
    m = jnp.max(logits, axis=0, keepdims=True)
    e = jnp.exp(logits - m)
    aff_ref[0] = e / jnp.sum(e, axis=0, keepdims=True)


def _cross_out_and_router(q, kv, w_co, layer, res, gain, w_router, batch, seq, n_mem, tm):
    m, d = q.shape
    per_b = seq // tm
    assert seq % tm == 0
    w_pad = jnp.zeros((d, LANES), F32).at[:, :N_EXPERTS].set(w_router)
    row_spec = pl.BlockSpec((tm, d), lambda i: (i, 0))
    return pl.pallas_call(
        _cross_out_router_kernel,
        grid=(m // tm,),
        in_specs=[
            row_spec,
            pl.BlockSpec((n_mem, d), lambda i: (i // per_b, 2 * layer)),
            pl.BlockSpec((n_mem, d), lambda i: (i // per_b, 2 * layer + 1)),
            pl.BlockSpec((None, d, d), lambda i: (layer, 0, 0)),
            row_spec,
            pl.BlockSpec((1, d), lambda i: (0, 0)),
            pl.BlockSpec((d, LANES), lambda i: (0, 0)),
        ],
        out_specs=[
            row_spec,
            row_spec,
            pl.BlockSpec((1, N_EXPERTS, tm), lambda i: (i // per_b, 0, i % per_b)),
        ],
        out_shape=[
            jax.ShapeDtypeStruct((m, d), F32),
            jax.ShapeDtypeStruct((m, d), BF16),
            jax.ShapeDtypeStruct((batch, N_EXPERTS, seq), F32),
        ],
        scratch_shapes=[pltpu.VMEM((tm, d), BF16)],
        compiler_params=_params("parallel"),
        name="cross_out_router",
    )(q, kv, kv, w_co, res, gain.reshape(1, d), w_pad)


SUBLANES = 8


def _sort_descending(x):
    n, lanes = x.shape
    k = 2
    while k <= n:
        j = k // 2
        while j >= 1:
            if j >= SUBLANES and k < n:
                v = x.reshape(n // (2 * k), 2, k // (2 * j), 2, j, lanes)
                a, b = v[:, :, :, 0], v[:, :, :, 1]
                mx, mn = jnp.maximum(a, b), jnp.minimum(a, b)
                lo = jnp.concatenate([mx[:, 0:1], mn[:, 1:2]], axis=1)
                hi = jnp.concatenate([mn[:, 0:1], mx[:, 1:2]], axis=1)
                x = jnp.stack([lo, hi], axis=3).reshape(n, lanes)
            elif j >= SUBLANES:
                v = x.reshape(n // (2 * j), 2, j, lanes)
                x = jnp.stack([jnp.maximum(v[:, 0], v[:, 1]), jnp.minimum(v[:, 0], v[:, 1])], axis=1)
                x = x.reshape(n, lanes)
            else:
                v = x.reshape(n // SUBLANES, SUBLANES, lanes)
                s = lax.broadcasted_iota(I32, v.shape, 0) * SUBLANES + lax.broadcasted_iota(I32, v.shape, 1)
                is_lo = (s & j) == 0
                partner = jnp.where(is_lo, pltpu.roll(v, SUBLANES - j, 1), pltpu.roll(v, j, 1))
                take_max = is_lo == ((s & k) == 0)
                x = jnp.where(take_max, jnp.maximum(v, partner), jnp.minimum(v, partner)).reshape(n, lanes)
            j //= 2
        k *= 2
    return x


def _topk_kernel(cap, n_e, aff_ref, slot_ref, wsel_ref, slot_t_ref):
    n_rows, n_tok = aff_ref.shape
    a = aff_ref[...]
    token_major = jnp.concatenate([a, jnp.zeros((LANES - n_rows, n_tok), F32)], axis=0).T
    kth = _sort_descending(token_major)[cap - 1:cap, :]
    thr = jnp.broadcast_to(kth, (SUBLANES, LANES)).T[:n_rows, 0:1]
    gt = a > thr
    eq = a == thr
    need = cap - jnp.sum(gt.astype(I32), axis=1, keepdims=True)
    upper = (lax.broadcasted_iota(I32, (n_tok, n_tok), 0) < lax.broadcasted_iota(I32, (n_tok, n_tok), 1))
    upper = upper.astype(BF16)
    eq_rank = jnp.dot(eq.astype(BF16), upper, preferred_element_type=F32)
    sel = jnp.logical_or(gt, jnp.logical_and(eq, eq_rank < need.astype(F32)))
    rank = jnp.dot(sel.astype(BF16), upper, preferred_element_type=F32)
    slot = jnp.where(sel, rank, -1.0)
    slot_ref[...] = slot.astype(I32)
    wsel_ref[...] = jnp.where(sel, a, 0.0)
    pad = jnp.full((LANES - n_e, n_tok), -1.0, F32)
    for b in range(n_rows // n_e):
        slot_t_ref[b] = jnp.concatenate([slot[b * n_e:(b + 1) * n_e], pad], axis=0).T.astype(I32)


def _topk_select(aff2d, cap, n_e):
    rows, n_tok = aff2d.shape
    spec = pl.BlockSpec((rows, n_tok), lambda i: (0, 0))
    spec_t = pl.BlockSpec((rows // n_e, n_tok, LANES), lambda i: (0, 0, 0))
    return pl.pallas_call(
        functools.partial(_topk_kernel, cap, n_e),
        grid=(1,),
        in_specs=[spec],
        out_specs=[spec, spec, spec_t],
        out_shape=[jax.ShapeDtypeStruct((rows, n_tok), I32), jax.ShapeDtypeStruct((rows, n_tok), F32),
                   jax.ShapeDtypeStruct((rows // n_e, n_tok, LANES), I32)],
        compiler_params=_params("arbitrary"),
        name="moe_topk_select",
    )(aff2d)


def _gather_kernel(cap, slot_ref, wsel_ref, hm_ref, xe_ref, gs_ref):
    n_tok = hm_ref.shape[0]
    onehot = slot_ref[0] == lax.broadcasted_iota(I32, (cap, n_tok), 0)
    xe_ref[0] = jnp.dot(onehot.astype(BF16), hm_ref[...], preferred_element_type=F32).astype(BF16)
    gs_ref[0] = jnp.sum(jnp.where(onehot, wsel_ref[0], 0.0), axis=1, keepdims=True)


def _gather_tokens(slot, wsel, hm, batch, seq, cap):
    rows = batch * N_EXPERTS
    row_spec = pl.BlockSpec((1, 1, seq), lambda b, e: (b * N_EXPERTS + e, 0, 0))
    return pl.pallas_call(
        functools.partial(_gather_kernel, cap),
        grid=(batch, N_EXPERTS),
        in_specs=[row_spec, row_spec, pl.BlockSpec((seq, D_MODEL), lambda b, e: (b, 0))],
        out_specs=[
            pl.BlockSpec((1, cap, D_MODEL), lambda b, e: (e, b, 0)),
            pl.BlockSpec((1, cap, 1), lambda b, e: (e, b, 0)),
        ],
        out_shape=[
            jax.ShapeDtypeStruct((N_EXPERTS, batch * cap, D_MODEL), BF16),
            jax.ShapeDtypeStruct((N_EXPERTS, batch * cap, 1), F32),
        ],
        compiler_params=_params("parallel", "arbitrary"),
        name="moe_gather",
    )(slot.reshape(rows, 1, seq), wsel.reshape(rows, 1, seq), hm)


def _expert_kernel(n_e, n_t, xe_ref, wg_ref, wu_ref, wd_ref, gs_ref, ye_ref, mid_ref):
    g = pl.program_id(0)
    s = pl.program_id(1)
    t = s // 2
    tf = wg_ref.shape[-1]

    @pl.when(jnp.logical_and(s % 2 == 0, g < n_e))
    def _():
        x = xe_ref[0]
        a = jnp.dot(x, wg_ref[...].astype(BF16), preferred_element_type=F32)
        u = jnp.dot(x, wu_ref[...].astype(BF16), preferred_element_type=F32)
        mid_ref[g % 2, t] = (a * jax.nn.sigmoid(a) * u).astype(BF16)

    @pl.when(jnp.logical_and(s % 2 == 1, g >= 1))
    def _():
        prev = (g - 1) % 2
        acc = jnp.dot(mid_ref[prev, 0], wd_ref[0:tf, :].astype(BF16), preferred_element_type=F32)
        for f in range(1, n_t):
            acc += jnp.dot(mid_ref[prev, f], wd_ref[f * tf:(f + 1) * tf, :].astype(BF16),
                           preferred_element_type=F32)
        ye_ref[0] = (acc * gs_ref[0]).astype(BF16)


def _expert_ffn(xe, gs, w_gate, w_up, w_down, layer, tf):
    n_e, rows, d = xe.shape
    ff = w_gate.shape[-1]
    n_t = ff // tf
    assert d // tf == n_t
    last = n_e - 1

    n_s = 2 * n_t

    def fill_tile(g, s, lead):
        v = jnp.minimum(g * n_s + s + lead, n_e * n_s - 1)
        return (layer, v // n_s, 0, (v % n_s) // 2)

    def emit_tile(g, s):
        v = jnp.maximum(g * n_s + s - 1 - n_s, 0)
        return (layer, v // n_s, 0, (v % n_s) // 2)

    return pl.pallas_call(
        functools.partial(_expert_kernel, n_e, n_t),
        grid=(n_e + 1, n_s),
        in_specs=[
            pl.BlockSpec((1, rows, d), lambda g, s: (jnp.minimum(g, last), 0, 0)),
            pl.BlockSpec((None, None, d, tf), functools.partial(fill_tile, lead=1)),
            pl.BlockSpec((None, None, d, tf), functools.partial(fill_tile, lead=0)),
            pl.BlockSpec((None, None, ff, tf), emit_tile),
            pl.BlockSpec((1, rows, 1), lambda g, s: (jnp.maximum(g - 1, 0), 0, 0)),
        ],
        out_specs=pl.BlockSpec((1, rows, tf), lambda g, s: emit_tile(g, s)[1:]),
        out_shape=jax.ShapeDtypeStruct((n_e, rows, d), BF16),
        scratch_shapes=[pltpu.VMEM((2, n_t, rows, tf), BF16)],
        compiler_params=_params("arbitrary", "arbitrary"),
        name="moe_expert_ffn",
    )(xe, w_gate, w_up, w_down, gs)


def _scatter_kernel(cap, slot_ref, ye_ref, x_ref, *rest):
    o_ref = rest[-1]
    n_e = ye_ref.shape[0]
    col = lax.broadcasted_iota(I32, (x_ref.shape[0], cap), 1)
    slots = slot_ref[0]
    onehot = jnp.concatenate([(slots[:, e:e + 1] == col).astype(BF16) for e in range(n_e)], axis=1)
    ye = ye_ref[...].reshape(n_e * cap, ye_ref.shape[-1])
    y = x_ref[...] + jnp.dot(onehot, ye, preferred_element_type=F32)
    o_ref[...] = _rms(y, rest[0][...]) if len(rest) == 2 else y


def _scatter_add(slot_t, ye, x2d, batch, seq, cap, tr, final_gain=None):
    d = x2d.shape[-1]
    per_b = seq // tr
    in_specs = [
        pl.BlockSpec((1, tr, LANES), lambda b, i: (b, i, 0)),
        pl.BlockSpec((N_EXPERTS, cap, d), lambda b, i: (0, b, 0)),
        pl.BlockSpec((tr, d), lambda b, i: (b * per_b + i, 0)),
    ]
    args = [slot_t, ye, x2d]
    if final_gain is not None:
        in_specs.append(pl.BlockSpec((1, d), lambda b, i: (0, 0)))
        args.append(final_gain.reshape(1, d))
    return pl.pallas_call(
        functools.partial(_scatter_kernel, cap),
        grid=(batch, per_b),
        in_specs=in_specs,
        out_specs=pl.BlockSpec((tr, d), lambda b, i: (b * per_b + i, 0)),
        out_shape=jax.ShapeDtypeStruct(x2d.shape, F32),
        compiler_params=_params("parallel", "arbitrary"),
        name="moe_scatter_add",
    )(*args)


def _expert_choice_ffn(x2d, hm, aff, w_gate, w_up, w_down, layer, batch, seq, final_gain=None):
    cap = EC_CAPACITY * seq // N_EXPERTS
    slot, wsel, slot_t = _topk_select(aff.reshape(batch * N_EXPERTS, seq), cap, N_EXPERTS)
    xe, gs = _gather_tokens(slot, wsel, hm, batch, seq, cap)
    ye = _expert_ffn(xe, gs, w_gate, w_up, w_down, layer, 512)
    return _scatter_add(slot_t, ye, x2d, batch, seq, cap, 256, final_gain)


def kernel(x, mem, norm_mix, w_in, na_rpb, hy_conv_w, hy_conv_b, hy_filt_w1, hy_filt_b1, hy_filt_w2, hy_filt_b2, hy_filt_w3, hy_sin_freq, hy_skip_d, branch_norm, w_out, norm_cross, mem_norm, w_cq, w_ckv, w_co, norm_moe, w_router, w_gate, w_up, w_down, final_norm):
    batch, seq, d = x.shape
    n_mem = mem.shape[1]
    depth = w_in.shape[0]
    m = batch * seq
    na_cols = 3 * NA_W
    hy_cols = 3 * HY_W
    rows = seq // GRID_W

    dft_fwd, dft_inv = _dft_tables(seq)
    x2d = x.reshape(m, d)
    mem2d = mem.reshape(batch * n_mem, d)

    w_out_b, w_cq_b, w_co_b = (w.astype(BF16) for w in (w_out, w_cq, w_co))
    p_in = w_in.shape[-1]
    kv = _norm_matmul(mem2d, mem_norm, (w_ckv, None, 0, w_ckv.shape[-1]), batch * n_mem, 1024, BF16,
                      "cross_kv_proj")

    for l in range(depth):
        assert p_in == 3 * na_cols
        slabs = [_norm_proj_resident(x2d, norm_mix[l], w_in, l, s, 3, 512, "in_proj").reshape(batch, seq, na_cols)
                 for s in range(3)]

        y_na = _neighbourhood_attention(slabs[0], _na_bias_table(na_rpb[l], rows), batch, seq)
        y_hy = _hyena_mixer(slabs[1], 0, hy_conv_w[l], hy_conv_b[l], hy_filt_w1[l],
                            hy_filt_b1[l], hy_filt_w2[l], hy_filt_b2[l], hy_filt_w3[l], hy_sin_freq[l],
                            hy_skip_d[l], dft_fwd, dft_inv, batch, seq)
        y_ret = _retention_mixer(slabs[1], hy_cols, slabs[2], 0, batch, seq)
        x2d, q = _branch_out_cross_q(y_na.reshape(m, NA_W), y_hy.reshape(m, HY_W), y_ret.reshape(m, RET_W),
                                     branch_norm[l], w_out_b, x2d, norm_cross[l], w_cq_b, l, 512)
        x2d, hm, aff = _cross_out_and_router(q, kv, w_co_b, l, x2d, norm_moe[l], w_router[l], batch, seq, n_mem,
                                             256)

        x2d = _expert_choice_ffn(x2d, hm, aff, w_gate, w_up, w_down, l, batch, seq,
                                 final_gain=final_norm if l == depth - 1 else None)

    return x2d.reshape(batch, seq, d)
```

```python
import functools
import math

import numpy as np
import jax
import jax.numpy as jnp
from jax import lax
from jax.experimental import pallas as pl
from jax.experimental.pallas import tpu as pltpu

F32 = jnp.float32
BF16 = jnp.bfloat16
I32 = jnp.int32

D_MODEL = 2048
GRID_W = 64
NA_HEAD_DIM = 64
NA_W = 768
NA_HEADS = 12
NA_KR = 8
NA_KC = 16
HY_W = 512
HY_ORDER = 2
HY_BANDS = 8
HY_POS_DIM = 17
HY_FILT_FF = 64
RET_HEAD_DIM = 128
RET_W = 768
RET_HEADS = 6
RET_CHUNK = 128
CROSS_HEADS = 4
CROSS_HEAD_DIM = 512
N_EXPERTS = 16
EXPERT_FF = 2048
EC_CAPACITY = 2
RMS_EPS = 1e-6
GN_EPS = 1e-5

MASK_VALUE = -1e30
LANES = 128
VMEM_LIMIT_BYTES = 56 * 1024 * 1024

NT_DIMS = (((1,), (1,)), ((), ()))


def _params(*sem):
    return pltpu.CompilerParams(dimension_semantics=sem, vmem_limit_bytes=VMEM_LIMIT_BYTES)


def _rms(xf, g):
    return xf * lax.rsqrt(jnp.mean(xf * xf, axis=-1, keepdims=True) + RMS_EPS) * g


def _norm_proj_layers_kernel(x_ref, g_ref, w_ref, o_ref, h_ref):
    @pl.when(pl.program_id(0) == 0)
    def _():
        h_ref[...] = _rms(x_ref[...], g_ref[...]).astype(BF16)

    o_ref[...] = jnp.dot(h_ref[...], w_ref[...].astype(BF16), preferred_element_type=F32).astype(o_ref.dtype)


def _norm_proj_layers(x2d, gain, w, tn):
    m, k = x2d.shape
    n_layers, _, n = w.shape
    per_layer = n // tn
    assert n % tn == 0
    return pl.pallas_call(
        _norm_proj_layers_kernel,
        grid=(n_layers * per_layer,),
        in_specs=[
            pl.BlockSpec((m, k), lambda j: (0, 0)),
            pl.BlockSpec((1, k), lambda j: (0, 0)),
            pl.BlockSpec((None, k, tn), lambda j: (j // per_layer, 0, j % per_layer)),
        ],
        out_specs=pl.BlockSpec((m, tn), lambda j: (0, j)),
        out_shape=jax.ShapeDtypeStruct((m, n_layers * n), BF16),
        scratch_shapes=[pltpu.VMEM((m, k), BF16)],
        compiler_params=_params("arbitrary"),
        name="cross_kv_proj",
    )(x2d, gain.reshape(1, k), w)


def _norm_proj_resident_kernel(x_ref, g_ref, w_ref, o_ref, wb_ref):
    @pl.when(pl.program_id(0) == 0)
    def _():
        wb_ref[...] = w_ref[...].astype(BF16)

    h = _rms(x_ref[...], g_ref[...]).astype(BF16)
    o_ref[...] = jnp.dot(h, wb_ref[...], preferred_element_type=F32).astype(o_ref.dtype)


def _norm_proj_resident(x2d, gain, w, layer, slab, n_slab, tm, name):
    m, k = x2d.shape
    width = w.shape[-1] // n_slab
    assert w.shape[-1] % n_slab == 0 and m % tm == 0
    return pl.pallas_call(
        _norm_proj_resident_kernel,
        grid=(m // tm,),
        in_specs=[
            pl.BlockSpec((tm, k), lambda i: (i, 0)),
            pl.BlockSpec((1, k), lambda i: (0, 0)),
            pl.BlockSpec((None, k, width), lambda i: (layer, 0, slab)),
        ],
        out_specs=pl.BlockSpec((tm, width), lambda i: (i, 0)),
        out_shape=jax.ShapeDtypeStruct((m, width), BF16),
        scratch_shapes=[pltpu.VMEM((k, width), BF16)],
        compiler_params=_params("arbitrary"),
        name=name,
    )(x2d, gain.reshape(1, k), w)


def _cross_prologue(h_ref, q_ref, k_ref, v_ref):
    dh = CROSS_HEAD_DIM
    for h in range(CROSS_HEADS):
        sl = slice(h * dh, (h + 1) * dh)
        s = lax.dot_general(q_ref[:, sl], k_ref[:, sl], NT_DIMS, preferred_element_type=F32) * (dh ** -0.5)
        m = jnp.max(s, axis=-1, keepdims=True)
        p = jnp.exp(s - m)
        l = jnp.sum(p, axis=-1, keepdims=True)
        o = jnp.dot(p.astype(BF16), v_ref[:, sl], preferred_element_type=F32) * (1.0 / l)
        h_ref[:, sl] = o.astype(BF16)


def _branch_prologue(h_ref, na_ref, hy_ref, ret_ref, g_ref):
    def nrm(p):
        return p * lax.rsqrt(jnp.mean(p * p, axis=-1, keepdims=True) + RMS_EPS)

    y = jnp.concatenate([nrm(r[...].astype(F32)) for r in (na_ref, hy_ref, ret_ref)], axis=-1)
    h_ref[...] = (y * g_ref[...]).astype(BF16)


def _branch_out_cross_q_kernel(na_ref, hy_ref, ret_ref, g_ref, wo_ref, res_ref, gq_ref, wq_ref, x_ref, q_ref,
                               h_ref):
    _branch_prologue(h_ref, na_ref, hy_ref, ret_ref, g_ref)
    x = res_ref[...] + jnp.dot(h_ref[...], wo_ref[...], preferred_element_type=F32)
    x_ref[...] = x
    hq = _rms(x, gq_ref[...]).astype(BF16)
    q_ref[...] = jnp.dot(hq, wq_ref[...], preferred_element_type=F32).astype(BF16)


def _branch_out_cross_q(y_na, y_hy, y_ret, gain, w_out, res, gain_q, w_cq, layer, tm):
    m, d = res.shape
    rows = lambda width: pl.BlockSpec((tm, width), lambda i: (i, 0))
    vec = pl.BlockSpec((1, d), lambda i: (0, 0))
    square = pl.BlockSpec((None, d, d), lambda i: (layer, 0, 0))
    return pl.pallas_call(
        _branch_out_cross_q_kernel,
        grid=(m // tm,),
        in_specs=[rows(NA_W), rows(HY_W), rows(RET_W), vec, square, rows(d), vec, square],
        out_specs=[rows(d), rows(d)],
        out_shape=[jax.ShapeDtypeStruct((m, d), F32), jax.ShapeDtypeStruct((m, d), BF16)],
        scratch_shapes=[pltpu.VMEM((tm, d), BF16)],
        compiler_params=_params("parallel"),
        name="branch_out_cross_q",
    )(y_na, y_hy, y_ret, gain.reshape(1, d), w_out, res, gain_q.reshape(1, d), w_cq)


NA_PAIR = 2
NA_BAND = NA_KR + NA_PAIR


def _na_band_base(r, rows):
    return np.clip(r - NA_KR // 2, 0, rows - NA_BAND)


@functools.lru_cache(maxsize=None)
def _na_variants(rows):
    assert rows % NA_PAIR == 0 and rows >= NA_BAND + 2
    n_var = NA_KR // 2 + 1
    dr = np.full((n_var, NA_BAND, NA_PAIR), -2, np.int64)
    for r in range(0, rows, NA_PAIR):
        base = int(_na_band_base(r, rows))
        v = (r - base) // 2
        for j in range(NA_PAIR):
            rs = int(np.clip(r + j - NA_KR // 2, 0, rows - NA_KR))
            for i in range(NA_BAND):
                val = base + i - (r + j) + (NA_KR - 1) if rs <= base + i < rs + NA_KR else -1
                assert dr[v, i, j] in (-2, val)
                dr[v, i, j] = val
    assert (dr > -2).all()
    return dr


def _na_bias_table(rpb, rows):
    c = np.arange(GRID_W)
    col_start = np.clip(c - NA_KC // 2, 0, GRID_W - NA_KC)
    col_in = (c[None, :] >= col_start[:, None]) & (c[None, :] < col_start[:, None] + NA_KC)
    dc = np.clip(c[None, :] - c[:, None] + (NA_KC - 1), 0, 2 * NA_KC - 2)
    onehot = (dc.T.reshape(-1)[None, :] == np.arange(2 * NA_KC - 1)[:, None]).astype(np.float32)
    cols = jnp.einsum("hab,bn->han", rpb.astype(F32), jnp.asarray(onehot), precision=lax.Precision.HIGHEST)
    cols = cols.reshape(NA_HEADS, 2 * NA_KR - 1, GRID_W, GRID_W)
    cols = jnp.where(col_in.T[None, None], cols, MASK_VALUE)
    dr = _na_variants(rows)
    n_var = dr.shape[0]
    return pl.pallas_call(
        functools.partial(_na_table_kernel, dr),
        grid=(NA_HEADS,),
        in_specs=[pl.BlockSpec((1, 2 * NA_KR - 1, GRID_W, GRID_W), lambda h: (h, 0, 0, 0))],
        out_specs=pl.BlockSpec((1, n_var, NA_BAND * GRID_W, NA_PAIR * GRID_W), lambda h: (h, 0, 0, 0)),
        out_shape=jax.ShapeDtypeStruct((NA_HEADS, n_var, NA_BAND * GRID_W, NA_PAIR * GRID_W), F32),
        compiler_params=_params("parallel"),
        name="na_bias_table",
    )(cols)


def _na_table_kernel(dr, cols_ref, o_ref):
    masked = jnp.full((GRID_W, GRID_W), MASK_VALUE, F32)
    for v in range(dr.shape[0]):
        for i in range(NA_BAND):
            blocks = [cols_ref[0, int(dr[v, i, j])] if dr[v, i, j] >= 0 else masked for j in range(NA_PAIR)]
            o_ref[0, v, i * GRID_W:(i + 1) * GRID_W, :] = jnp.concatenate(blocks, axis=-1)


def _na_kernel(rows, q_ref, k_ref, v_ref, bias_ref, o_ref, vt_ref, s0_ref, s1_ref, p0_ref, p1_ref, l0_ref,
               l1_ref):
    dh = NA_HEAD_DIM
    nq = NA_PAIR * GRID_W
    nk = NA_BAND * GRID_W
    n_chunk = nk // LANES
    lane = lax.broadcasted_iota(I32, (nq, 2 * dh), 1)

    for ch in range(vt_ref.shape[0]):
        vt_ref[ch] = v_ref[0, ch * LANES:(ch + 1) * LANES, :].astype(F32).T.astype(BF16)

    n_steps = rows // NA_PAIR

    def band_base(p):
        return jnp.clip(p * NA_PAIR - NA_KR // 2, 0, rows - NA_BAND)


    def scores(p, s_out):
        r = p * NA_PAIR
        base = band_base(p)
        variant = (r - base) // 2
        q = q_ref[0, pl.ds(pl.multiple_of(r * GRID_W, nq), nq), :] * (dh ** -0.5)
        kb = k_ref[0, pl.ds(pl.multiple_of(base * GRID_W, LANES), nk), :]
        zero = jnp.zeros_like(q)
        q2 = jnp.concatenate([jnp.where(lane < dh, q, zero), jnp.where(lane >= dh, q, zero)], axis=0)
        st2 = lax.dot_general(kb, q2, NT_DIMS, preferred_element_type=F32)
        for hh in range(2):
            s_out[:, hh * nq:(hh + 1) * nq] = st2[:, hh * nq:(hh + 1) * nq] + bias_ref[hh, variant]

    def softmax(s_in, p_out, l_out):
        st = s_in[...]
        pt = jnp.exp(st - jnp.max(st, axis=0, keepdims=True))
        p_out[...] = pt.astype(BF16)
        l_out[...] = 1.0 / jnp.sum(pt, axis=0, keepdims=True)

    def values(p, p_in, l_in):
        r = p * NA_PAIR
        c0 = band_base(p) // 2
        vt = jnp.concatenate([vt_ref[c0 + i] for i in range(n_chunk)], axis=1)
        pt = p_in[...]
        linv = l_in[...]
        outs = []
        for hh in range(2):
            cols = slice(hh * nq, (hh + 1) * nq)
            ot = jnp.dot(vt[hh * dh:(hh + 1) * dh, :], pt[:, cols], preferred_element_type=F32)
            outs.append(ot * linv[:, cols])
        o_ref[0, pl.ds(pl.multiple_of(r * GRID_W, nq), nq), :] = jnp.concatenate(outs, axis=0).T.astype(BF16)

    s_slots = (s0_ref, s1_ref)
    p_slots = (p0_ref, p1_ref)
    l_slots = (l0_ref, l1_ref)

    def step(t, parity, do_scores=True, do_softmax=True, do_values=True):
        a, b = parity, 1 - parity
        t = jnp.asarray(t, I32)
        if do_scores:
            scores(t, s_slots[a])
        if do_softmax:
            softmax(s_slots[b], p_slots[b], l_slots[b])
        if do_values:
            values(t - 2, p_slots[a], l_slots[a])

    assert n_steps % 2 == 0 and n_steps >= 4
    step(0, 0, do_softmax=False, do_values=False)
    step(1, 1, do_values=False)

    def steady(i, carry):
        t = 2 + 2 * i
        step(t, 0)
        step(t + 1, 1)
        return carry

    lax.fori_loop(0, (n_steps - 2) // 2, steady, 0, unroll=True)
    step(n_steps, 0, do_scores=False)
    step(n_steps + 1, 1, do_scores=False, do_softmax=False)


def _neighbourhood_attention(proj_na, bias_tbl, batch, seq):
    rows = seq // GRID_W
    n_pairs = NA_HEADS // 2
    blk = (1, seq, 2 * NA_HEAD_DIM)
    n_var, nk, nq = bias_tbl.shape[1:]
    return pl.pallas_call(
        functools.partial(_na_kernel, rows),
        grid=(batch, n_pairs),
        in_specs=[
            pl.BlockSpec(blk, lambda b, h: (b, 0, h)),
            pl.BlockSpec(blk, lambda b, h: (b, 0, n_pairs + h)),
            pl.BlockSpec(blk, lambda b, h: (b, 0, 2 * n_pairs + h)),
            pl.BlockSpec((2, n_var, nk, nq), lambda b, h: (h, 0, 0, 0)),
        ],
        out_specs=pl.BlockSpec(blk, lambda b, h: (b, 0, h)),
        out_shape=jax.ShapeDtypeStruct((batch, seq, NA_W), BF16),
        scratch_shapes=[pltpu.VMEM((seq // LANES, 2 * NA_HEAD_DIM, LANES), BF16),
                        pltpu.VMEM((nk, 2 * nq), F32), pltpu.VMEM((nk, 2 * nq), F32),
                        pltpu.VMEM((nk, 2 * nq), BF16), pltpu.VMEM((nk, 2 * nq), BF16),
                        pltpu.VMEM((1, 2 * nq), F32), pltpu.VMEM((1, 2 * nq), F32)],
        compiler_params=_params("parallel", "arbitrary"),
        name="neighbourhood_attention",
    )(proj_na, proj_na, proj_na, bias_tbl)


@functools.lru_cache(maxsize=None)
def _dft_factors(seq):
    n = 2 * seq
    t = np.arange(seq, dtype=np.int64)
    f1 = np.arange(seq // 64, dtype=np.int64)
    f0 = np.arange(64, dtype=np.int64)
    a = 2.0 * np.pi * ((64 * f1[:, None] * t[None, :]) % n).astype(np.float64) / n
    b = 2.0 * np.pi * ((f0[:, None] * t[None, :]) % n).astype(np.float64) / n
    ny = np.where(t % 2 == 0, 1.0, -1.0)
    return tuple(np.asarray(v, np.float32) for v in (np.cos(a), np.sin(a), np.cos(b), np.sin(b), ny))


def _dft_table_kernel(u_ref, v_ref, cb_ref, sb_ref, fwd_ref, inv_ref):
    sin_half = pl.program_id(0) == 1
    first = pl.program_id(1) == 0
    n_fine, seq = cb_ref.shape
    row = lax.broadcasted_iota(I32, (n_fine, seq), 0)
    col = lax.broadcasted_iota(I32, (n_fine, seq), 1)
    for c in range(u_ref.shape[1]):
        tile = u_ref[0, c:c + 1, :] * cb_ref[...] + v_ref[0, c:c + 1, :] * sb_ref[...]
        rows = slice(c * n_fine, (c + 1) * n_fine)
        fwd = tile
        if c == 0:
            fwd_nyq = jnp.logical_and(jnp.logical_and(sin_half, first), row == 0)
            fwd = jnp.where(fwd_nyq, (1 - 2 * (col & 1)).astype(F32), tile)
        fwd_ref[0, rows, :] = fwd.astype(BF16)
        inv_nyq = jnp.logical_and(sin_half, col == 0)
        inv_ref[rows, :] = jnp.where(inv_nyq, (1 - 2 * (row & 1)).astype(F32), tile).astype(BF16)


def _dft_tables(seq):
    ca, sa, cb, sb, _ = _dft_factors(seq)
    u = np.stack([ca, sa])
    v = np.stack([-sa, ca])
    n_coarse = 8
    rows = 64 * n_coarse
    coarse_spec = pl.BlockSpec((1, n_coarse, seq), lambda h, i: (h, i, 0))
    fine_spec = pl.BlockSpec((64, seq), lambda h, i: (0, 0))
    return pl.pallas_call(
        _dft_table_kernel,
        grid=(2, seq // rows),
        in_specs=[coarse_spec, coarse_spec, fine_spec, fine_spec],
        out_specs=[pl.BlockSpec((1, rows, seq), lambda h, i: (h, i, 0)),
                   pl.BlockSpec((rows, seq), lambda h, i: (i, h))],
        out_shape=[jax.ShapeDtypeStruct((2, seq, seq), BF16), jax.ShapeDtypeStruct((seq, 2 * seq), BF16)],
        compiler_params=_params("parallel", "arbitrary"),
        name="dft_tables",
    )(jnp.asarray(u), jnp.asarray(v), jnp.asarray(cb), jnp.asarray(sb))


@functools.lru_cache(maxsize=None)
def _hyena_consts(seq):
    t = np.arange(seq, dtype=np.float64)
    t01 = t / (seq - 1)
    bands = np.linspace(1e-4, HY_BANDS - 1, HY_BANDS)
    ang = (2.0 * math.pi) * (t[:, None] / seq) * bands[None, :]
    feats = np.concatenate([t01[:, None], np.cos(ang), -np.sin(ang)], axis=-1)
    feats_p = np.zeros((seq, LANES), np.float32)
    feats_p[:, :HY_POS_DIM] = feats
    min_decay = math.log(1e-2) / 1.5
    max_decay = math.log(1e-2) / 0.3
    deltas = np.abs(np.linspace(min_decay, max_decay, HY_W))
    window = np.exp(-t01[:, None] * deltas[None, :]).astype(np.float32)
    return feats_p, window


def _filter_kernel(feats_ref, w1_ref, b1_ref, w2_ref, b2_ref, freq_ref, w3f_ref, w3b_ref, win_ref, sum_ref,
                   diff_ref, hid_ref):
    hp = lax.Precision.HIGHEST

    @pl.when(pl.program_id(0) == 0)
    def _():
        f = freq_ref[...]
        h1 = jnp.sin(f * (jnp.dot(feats_ref[...], w1_ref[...], precision=hp, preferred_element_type=F32)
                          + b1_ref[...]))
        hid_ref[...] = jnp.sin(f * (jnp.dot(h1, w2_ref[...], precision=hp, preferred_element_type=F32)
                                    + b2_ref[...]))

    hid = hid_ref[...]
    win = win_ref[...]
    fwd = jnp.dot(hid, w3f_ref[...], precision=hp, preferred_element_type=F32) * win
    bwd = jnp.dot(hid, w3b_ref[...], precision=hp, preferred_element_type=F32) * win
    bwd = jnp.where(lax.broadcasted_iota(I32, bwd.shape, 0) == 0, 0.0, bwd)
    sum_ref[...] = (fwd + bwd).astype(BF16)
    diff_ref[...] = (fwd - bwd).astype(BF16)


def _hyena_filters_time(w1, b1, w2, b2, w3, freq, seq):
    feats, window = _hyena_consts(seq)
    w1p = jnp.zeros((LANES, HY_FILT_FF), F32).at[:HY_POS_DIM].set(w1)
    full = lambda shape: pl.BlockSpec(shape, lambda o: (0,) * len(shape))
    out_spec = pl.BlockSpec((seq, HY_W), lambda o: (0, o))
    out_shape = jax.ShapeDtypeStruct((seq, HY_ORDER * HY_W), BF16)
    return pl.pallas_call(
        _filter_kernel,
        grid=(HY_ORDER,),
        in_specs=[
            full((seq, LANES)), full((LANES, HY_FILT_FF)), full((1, HY_FILT_FF)),
            full((HY_FILT_FF, HY_FILT_FF)), full((1, HY_FILT_FF)), full((1, HY_FILT_FF)),
            pl.BlockSpec((HY_FILT_FF, HY_W), lambda o: (0, 2 * o)),
            pl.BlockSpec((HY_FILT_FF, HY_W), lambda o: (0, 2 * o + 1)),
            full((seq, HY_W)),
        ],
        out_specs=[out_spec, out_spec],
        out_shape=[out_shape, out_shape],
        scratch_shapes=[pltpu.VMEM((seq, HY_FILT_FF), F32)],
        compiler_params=_params("arbitrary"),
        name="hyena_filter_mlp",
    )(jnp.asarray(feats), w1p, b1.reshape(1, -1), w2, b2.reshape(1, -1), freq.reshape(1, -1), w3, w3,
      jnp.asarray(window))


def _filter_dft_kernel(f_ref, sum_ref, diff_ref, o_ref):
    o_ref[0] = jnp.dot(f_ref[0], sum_ref[...], preferred_element_type=F32)
    o_ref[1] = jnp.dot(f_ref[1], diff_ref[...], preferred_element_type=F32)

    @pl.when(pl.program_id(0) == 0)
    def _():
        top = jnp.dot(f_ref[1, 0:8, :], sum_ref[...], preferred_element_type=F32)
        row = lax.broadcasted_iota(I32, top.shape, 0)
        o_ref[1, 0:8, :] = jnp.where(row == 0, top, o_ref[1, 0:8, :])


def _filter_dft(dft_fwd, filt_sum, filt_diff, seq, fb):
    return pl.pallas_call(
        _filter_dft_kernel,
        grid=(seq // fb, HY_ORDER),
        in_specs=[
            pl.BlockSpec((2, fb, seq), lambda f, o: (0, f, 0)),
            pl.BlockSpec((seq, HY_W), lambda f, o: (0, o)),
            pl.BlockSpec((seq, HY_W), lambda f, o: (0, o)),
        ],
        out_specs=pl.BlockSpec((2, fb, HY_W), lambda f, o: (0, f, o)),
        out_shape=jax.ShapeDtypeStruct((2, seq, HY_ORDER * HY_W), F32),
        compiler_params=_params("parallel", "arbitrary"),
        name="hyena_filter_dft",
    )(dft_fwd, filt_sum, filt_diff)


def _short_conv_kernel(p_ref, w_ref, b_ref, o_ref):
    p = p_ref[0].astype(F32)
    seq = p.shape[0]
    row = lax.broadcasted_iota(I32, p.shape, 0)
    prev = jnp.where(row == 0, 0.0, pltpu.roll(p, 1, 0))
    nxt = jnp.where(row == seq - 1, 0.0, pltpu.roll(p, seq - 1, 0))
    w = w_ref[...]
    o_ref[0] = prev * w[0:1] + p * w[1:2] + nxt * w[2:3] + b_ref[...]


def _short_conv(proj, col0, conv_w, conv_b, batch, seq, tc):
    n_cols = conv_w.shape[-1]
    assert col0 % tc == 0 and n_cols % tc == 0
    cb0 = col0 // tc
    return pl.pallas_call(
        _short_conv_kernel,
        grid=(batch, n_cols // tc),
        in_specs=[
            pl.BlockSpec((1, seq, tc), lambda b, c: (b, 0, cb0 + c)),
            pl.BlockSpec((3, tc), lambda b, c: (0, c)),
            pl.BlockSpec((1, tc), lambda b, c: (0, c)),
        ],
        out_specs=pl.BlockSpec((1, seq, tc), lambda b, c: (b, 0, c)),
        out_shape=jax.ShapeDtypeStruct((batch, seq, n_cols), F32),
        compiler_params=_params("parallel", "arbitrary"),
        name="hyena_short_conv",
    )(proj, conv_w, conv_b.reshape(1, -1))


def _spectrum_kernel(n_fft, f_ref, z_ref, k_ref, o_ref):
    z = z_ref[0].astype(BF16)
    xr = jnp.dot(f_ref[0], z, preferred_element_type=F32)
    xs = jnp.dot(f_ref[1], z, preferred_element_type=F32)
    kr = k_ref[0]
    ks = k_ref[1]
    row = lax.broadcasted_iota(I32, xr.shape, 0)
    edge = jnp.logical_and(row == 0, pl.program_id(0) == 0)
    yr = jnp.where(edge, xr * kr * (1.0 / n_fft), (xr * kr - xs * ks) * (2.0 / n_fft))
    ys = jnp.where(edge, xs * ks * (1.0 / n_fft), (xr * ks + xs * kr) * (2.0 / n_fft))
    o_ref[0, 0] = yr.astype(BF16)
    o_ref[0, 1] = ys.astype(BF16)


def _spectrum_product(dft_fwd, z_arr, z_col, kfreq, order, batch, seq, fb):
    return pl.pallas_call(
        functools.partial(_spectrum_kernel, 2 * seq),
        grid=(seq // fb, batch),
        in_specs=[
            pl.BlockSpec((2, fb, seq), lambda f, b: (0, f, 0)),
            pl.BlockSpec((1, seq, HY_W), lambda f, b: (b, 0, z_col)),
            pl.BlockSpec((2, fb, HY_W), lambda f, b: (0, f, order)),
        ],
        out_specs=pl.BlockSpec((1, 2, fb, HY_W), lambda f, b: (b, 0, f, 0)),
        out_shape=jax.ShapeDtypeStruct((batch, 2, seq, HY_W), BF16),
        compiler_params=_params("parallel", "arbitrary"),
        name="hyena_spectrum",
    )(dft_fwd, z_arr, kfreq)


def _inverse_kernel(ft_ref, y_ref, gate_ref, z_ref, d_ref, o_ref):
    conv = jnp.dot(ft_ref[...], y_ref[0], preferred_element_type=F32)
    o_ref[0] = gate_ref[0] * (conv + d_ref[...] * z_ref[0])


def _inverse_gate(dft_inv, y, gate_arr, gate_col, z_arr, z_col, skip_row, batch, seq, tb):
    y2 = y.reshape(batch, 2 * seq, HY_W)
    return pl.pallas_call(
        _inverse_kernel,
        grid=(seq // tb, batch),
        in_specs=[
            pl.BlockSpec((tb, 2 * seq), lambda t, b: (t, 0)),
            pl.BlockSpec((1, 2 * seq, HY_W), lambda t, b: (b, 0, 0)),
            pl.BlockSpec((1, tb, HY_W), lambda t, b: (b, t, gate_col)),
            pl.BlockSpec((1, tb, HY_W), lambda t, b: (b, t, z_col)),
            pl.BlockSpec((1, HY_W), lambda t, b: (0, 0)),
        ],
        out_specs=pl.BlockSpec((1, tb, HY_W), lambda t, b: (b, t, 0)),
        out_shape=jax.ShapeDtypeStruct((batch, seq, HY_W), F32),
        compiler_params=_params("parallel", "arbitrary"),
        name="hyena_inverse_gate",
    )(dft_inv, y2, gate_arr, z_arr, skip_row.reshape(1, HY_W))


def _hyena_mixer(proj, col0, conv_w, conv_b, w1, b1, w2, b2, w3, freq, skip_d, dft_fwd, dft_inv, batch, seq):
    s = _short_conv(proj, col0, conv_w, conv_b, batch, seq, 768)
    filt_sum, filt_diff = _hyena_filters_time(w1, b1, w2, b2, w3, freq, seq)
    kfreq = _filter_dft(dft_fwd, filt_sum, filt_diff, seq, 512)
    z_arr, z_col = s, 2
    for o in range(HY_ORDER):
        y = _spectrum_product(dft_fwd, z_arr, z_col, kfreq, o, batch, seq, 1024)
        z_arr = _inverse_gate(dft_inv, y, s, o, z_arr, z_col, skip_d[o], batch, seq, 1024)
        z_col = 0
    return z_arr


@functools.lru_cache(maxsize=None)
def _retention_consts(seq):
    c = RET_CHUNK
    half = RET_HEAD_DIM // 2
    inv = 1.0 / (10000.0 ** np.linspace(0.0, 1.0, half))
    ang = np.arange(seq, dtype=np.float64)[:, None] * inv[None, :]
    cos2 = np.concatenate([np.cos(ang), np.cos(ang)], axis=-1).astype(np.float32)
    sin2 = np.concatenate([-np.sin(ang), np.sin(ang)], axis=-1).astype(np.float32)
    hidx = np.arange(RET_HEADS, dtype=np.float64)
    lg_f = np.log1p(-np.exp2(-5.0 - hidx))[:, None, None]
    lg_b = np.log1p(-np.exp2(-5.5 - hidx))[:, None, None]
    i = np.arange(c, dtype=np.float64)
    diff = i[:, None] - i[None, :]
    ones = np.ones((1, c, c))
    dec = np.where(diff >= 0, np.exp(lg_f * np.maximum(diff, 0.0)), np.exp(lg_b * np.maximum(-diff, 0.0)))
    rowv = lambda v: v[:, :, None] * ones
    tab = np.stack([
        dec,
        rowv(np.exp(lg_f[:, :, 0] * (i + 1.0)[None, :])),
        rowv(np.exp(lg_f[:, :, 0] * (c - 1.0 - i)[None, :])),
        rowv(np.exp(lg_b[:, :, 0] * (c - i)[None, :])),
        rowv(np.exp(lg_b[:, :, 0] * i[None, :])),
        np.exp(lg_f * c) * ones,
        np.exp(lg_b * c) * ones,
    ], axis=1).astype(np.float32)
    return cos2, sin2, tab


def _retention_kernel(q_ref, k_ref, v_ref, g_ref, cos_ref, sin_ref, tab_ref, o_ref, qs_ref, ks_ref, kvf_ref,
                      kvb_ref, a0_ref, a1_ref, y0_ref, y1_ref):
    c = RET_CHUNK
    d = RET_HEAD_DIM
    seq = q_ref.shape[1]
    n_chunks = seq // c
    cos = cos_ref[...]
    sin = sin_ref[...]
    q = q_ref[0].astype(F32)
    k = k_ref[0].astype(F32)
    qs_ref[...] = (q * cos + pltpu.roll(q, d // 2, 1) * sin) * (d ** -0.5)
    ks_ref[...] = k * cos + pltpu.roll(k, d // 2, 1) * sin

    def mm(a, b):
        return jnp.dot(a.astype(BF16), b.astype(BF16), preferred_element_type=F32)

    def chunk(n):
        return pl.ds(pl.multiple_of(n * c, c), c)

    def kv_body(n, carry):
        kc = ks_ref[chunk(n), :]
        vc = v_ref[0, chunk(n), :]
        kvf_ref[n] = mm((kc * tab_ref[0, 2]).T, vc)
        kvb_ref[n] = mm((kc * tab_ref[0, 4]).T, vc)
        return carry

    lax.fori_loop(0, n_chunks, kv_body, 0, unroll=True)

    def scan_fwd(n, state):
        kv = kvf_ref[n]
        kvf_ref[n] = state
        return tab_ref[0, 5] * state + kv

    lax.fori_loop(0, n_chunks, scan_fwd, jnp.zeros((d, d), F32))

    def scan_bwd(m, state):
        n = n_chunks - 1 - m
        kv = kvb_ref[n]
        kvb_ref[n] = state
        return tab_ref[0, 6] * state + kv

    lax.fori_loop(0, n_chunks, scan_bwd, jnp.zeros((d, d), F32))

    def in_chunk(n, a_out):
        qc = qs_ref[chunk(n), :].astype(BF16)
        kc = ks_ref[chunk(n), :].astype(BF16)
        a = lax.dot_general(qc, kc, NT_DIMS, preferred_element_type=F32) * tab_ref[0, 0]
        a_out[...] = a.astype(BF16)

    def mix(n, a_in, y_out):
        qc = qs_ref[chunk(n), :]
        y_out[...] = (jnp.dot(a_in[...], v_ref[0, chunk(n), :], preferred_element_type=F32)
                      + mm(qc * tab_ref[0, 1], kvf_ref[n]) + mm(qc * tab_ref[0, 3], kvb_ref[n]))

    def finish(n, y_in):
        y = y_in[...]
        mu = jnp.mean(y, axis=-1, keepdims=True)
        yc = y - mu
        var = jnp.mean(yc * yc, axis=-1, keepdims=True)
        g = g_ref[0, chunk(n), :].astype(F32)
        o_ref[0, chunk(n), :] = (yc * lax.rsqrt(var + GN_EPS) * (g * jax.nn.sigmoid(g))).astype(BF16)

    a_slots = (a0_ref, a1_ref)
    y_slots = (y0_ref, y1_ref)

    def step(t, parity, do_a=True, do_mix=True, do_finish=True):
        t = jnp.asarray(t, I32)
        if do_a:
            in_chunk(t, a_slots[parity])
        if do_mix:
            mix(t - 1, a_slots[1 - parity], y_slots[1 - parity])
        if do_finish:
            finish(t - 2, y_slots[parity])

    assert n_chunks % 2 == 0 and n_chunks >= 4
    step(0, 0, do_mix=False, do_finish=False)
    step(1, 1, do_finish=False)

    def steady(i, carry):
        step(2 + 2 * i, 0)
        step(3 + 2 * i, 1)
        return carry

    lax.fori_loop(0, (n_chunks - 2) // 2, steady, 0, unroll=True)
    step(n_chunks, 0, do_a=False)
    step(n_chunks + 1, 1, do_a=False, do_mix=False)


def _retention_mixer(proj_q, col_q, proj, col0, batch, seq):
    cos2, sin2, tab = _retention_consts(seq)
    blk = (1, seq, RET_HEAD_DIM)
    h_ = RET_HEADS
    assert col0 % RET_HEAD_DIM == 0 and col_q % RET_HEAD_DIM == 0
    cq = col_q // RET_HEAD_DIM
    c0 = col0 // RET_HEAD_DIM - h_
    return pl.pallas_call(
        _retention_kernel,
        grid=(batch, RET_HEADS),
        in_specs=[
            pl.BlockSpec(blk, lambda b, h: (b, 0, cq + h)),
            pl.BlockSpec(blk, lambda b, h: (b, 0, c0 + h_ + h)),
            pl.BlockSpec(blk, lambda b, h: (b, 0, c0 + 2 * h_ + h)),
            pl.BlockSpec(blk, lambda b, h: (b, 0, c0 + 3 * h_ + h)),
            pl.BlockSpec((seq, RET_HEAD_DIM), lambda b, h: (0, 0)),
            pl.BlockSpec((seq, RET_HEAD_DIM), lambda b, h: (0, 0)),
            pl.BlockSpec((1, 7, RET_CHUNK, RET_CHUNK), lambda b, h: (h, 0, 0, 0)),
        ],
        out_specs=pl.BlockSpec(blk, lambda b, h: (b, 0, h)),
        out_shape=jax.ShapeDtypeStruct((batch, seq, RET_W), BF16),
        scratch_shapes=[pltpu.VMEM((seq, RET_HEAD_DIM), F32), pltpu.VMEM((seq, RET_HEAD_DIM), F32),
                        pltpu.VMEM((seq // RET_CHUNK, RET_HEAD_DIM, RET_HEAD_DIM), F32),
                        pltpu.VMEM((seq // RET_CHUNK, RET_HEAD_DIM, RET_HEAD_DIM), F32),
                        pltpu.VMEM((RET_CHUNK, RET_CHUNK), BF16), pltpu.VMEM((RET_CHUNK, RET_CHUNK), BF16),
                        pltpu.VMEM((RET_CHUNK, RET_HEAD_DIM), F32), pltpu.VMEM((RET_CHUNK, RET_HEAD_DIM), F32)],
        compiler_params=_params("parallel", "arbitrary"),
        name="retention",
    )(proj_q, proj, proj, proj, jnp.asarray(cos2), jnp.asarray(sin2), jnp.asarray(tab))


def _cross_out_router_kernel(q_ref, k_ref, v_ref, w_ref, res_ref, g_ref, wr_ref, x_ref, hm_ref, aff_ref, h_ref):
    n_e = aff_ref.shape[1]
    _cross_prologue(h_ref, q_ref, k_ref, v_ref)
    x = res_ref[...] + jnp.dot(h_ref[...], w_ref[...], preferred_element_type=F32)
    x_ref[...] = x
    h = _rms(x, g_ref[...])
    h_hi = h.astype(BF16)
    hm_ref[...] = h_hi
    h_lo = (h - h_hi.astype(F32)).astype(BF16)
    wr = wr_ref[...]
    w_hi = wr.astype(BF16)
    w_lo = (wr - w_hi.astype(F32)).astype(BF16)
    both = jnp.dot(h_hi, jnp.concatenate([w_hi, w_lo], axis=1), preferred_element_type=F32)
    logits = (both[:, :LANES] + both[:, LANES:] + jnp.dot(h_lo, w_hi, preferred_element_type=F32)).T[:n_e]
    m = jnp.max(logits, axis=0, keepdims=True)
    e = jnp.exp(logits - m)
    aff_ref[0] = e / jnp.sum(e, axis=0, keepdims=True)


def _cross_out_and_router(q, kv, w_co, layer, res, gain, w_router, batch, seq, n_mem, tm):
    m, d = q.shape
    per_b = seq // tm
    assert seq % tm == 0
    w_pad = jnp.zeros((d, LANES), F32).at[:, :N_EXPERTS].set(w_router)
    row_spec = pl.BlockSpec((tm, d), lambda i: (i, 0))
    return pl.pallas_call(
        _cross_out_router_kernel,
        grid=(m // tm,),
        in_specs=[
            row_spec,
            pl.BlockSpec((n_mem, d), lambda i: (i // per_b, 2 * layer)),
            pl.BlockSpec((n_mem, d), lambda i: (i // per_b, 2 * layer + 1)),
            pl.BlockSpec((None, d, d), lambda i: (layer, 0, 0)),
            row_spec,
            pl.BlockSpec((1, d), lambda i: (0, 0)),
            pl.BlockSpec((d, LANES), lambda i: (0, 0)),
        ],
        out_specs=[
            row_spec,
            row_spec,
            pl.BlockSpec((1, N_EXPERTS, tm), lambda i: (i // per_b, 0, i % per_b)),
        ],
        out_shape=[
            jax.ShapeDtypeStruct((m, d), F32),
            jax.ShapeDtypeStruct((m, d), BF16),
            jax.ShapeDtypeStruct((batch, N_EXPERTS, seq), F32),
        ],
        scratch_shapes=[pltpu.VMEM((tm, d), BF16)],
        compiler_params=_params("parallel"),
        name="cross_out_router",
    )(q, kv, kv, w_co, res, gain.reshape(1, d), w_pad)


SUBLANES = 8


def _sort_descending(x):
    n, lanes = x.shape
    k = 2
    while k <= n:
        j = k // 2
        while j >= 1:
            if j >= SUBLANES and k < n:
                v = x.reshape(n // (2 * k), 2, k // (2 * j), 2, j, lanes)
                a, b = v[:, :, :, 0], v[:, :, :, 1]
                mx, mn = jnp.maximum(a, b), jnp.minimum(a, b)
                lo = jnp.concatenate([mx[:, 0:1], mn[:, 1:2]], axis=1)
                hi = jnp.concatenate([mn[:, 0:1], mx[:, 1:2]], axis=1)
                x = jnp.stack([lo, hi], axis=3).reshape(n, lanes)
            elif j >= SUBLANES:
                v = x.reshape(n // (2 * j), 2, j, lanes)
                x = jnp.stack([jnp.maximum(v[:, 0], v[:, 1]), jnp.minimum(v[:, 0], v[:, 1])], axis=1)
                x = x.reshape(n, lanes)
            else:
                v = x.reshape(n // SUBLANES, SUBLANES, lanes)
                s = lax.broadcasted_iota(I32, v.shape, 0) * SUBLANES + lax.broadcasted_iota(I32, v.shape, 1)
                is_lo = (s & j) == 0
                partner = jnp.where(is_lo, pltpu.roll(v, SUBLANES - j, 1), pltpu.roll(v, j, 1))
                take_max = is_lo == ((s & k) == 0)
                x = jnp.where(take_max, jnp.maximum(v, partner), jnp.minimum(v, partner)).reshape(n, lanes)
            j //= 2
        k *= 2
    return x


def _topk_kernel(cap, n_e, aff_ref, slot_ref, wsel_ref, slot_t_ref):
    n_rows, n_tok = aff_ref.shape
    a = aff_ref[...]
    token_major = jnp.concatenate([a, jnp.zeros((LANES - n_rows, n_tok), F32)], axis=0).T
    kth = _sort_descending(token_major)[cap - 1:cap, :]
    thr = jnp.broadcast_to(kth, (SUBLANES, LANES)).T[:n_rows, 0:1]
    gt = a > thr
    eq = a == thr
    need = cap - jnp.sum(gt.astype(I32), axis=1, keepdims=True)
    upper = (lax.broadcasted_iota(I32, (n_tok, n_tok), 0) < lax.broadcasted_iota(I32, (n_tok, n_tok), 1))
    upper = upper.astype(BF16)
    eq_rank = jnp.dot(eq.astype(BF16), upper, preferred_element_type=F32)
    sel = jnp.logical_or(gt, jnp.logical_and(eq, eq_rank < need.astype(F32)))
    rank = jnp.dot(sel.astype(BF16), upper, preferred_element_type=F32)
    slot = jnp.where(sel, rank, -1.0)
    slot_ref[...] = slot.astype(I32)
    wsel_ref[...] = jnp.where(sel, a, 0.0)
    pad = jnp.full((LANES - n_e, n_tok), -1.0, F32)
    for b in range(n_rows // n_e):
        slot_t_ref[b] = jnp.concatenate([slot[b * n_e:(b + 1) * n_e], pad], axis=0).T.astype(I32)


def _topk_select(aff2d, cap, n_e):
    rows, n_tok = aff2d.shape
    spec = pl.BlockSpec((rows, n_tok), lambda i: (0, 0))
    spec_t = pl.BlockSpec((rows // n_e, n_tok, LANES), lambda i: (0, 0, 0))
    return pl.pallas_call(
        functools.partial(_topk_kernel, cap, n_e),
        grid=(1,),
        in_specs=[spec],
        out_specs=[spec, spec, spec_t],
        out_shape=[jax.ShapeDtypeStruct((rows, n_tok), I32), jax.ShapeDtypeStruct((rows, n_tok), F32),
                   jax.ShapeDtypeStruct((rows // n_e, n_tok, LANES), I32)],
        compiler_params=_params("arbitrary"),
        name="moe_topk_select",
    )(aff2d)


def _gather_kernel(cap, slot_ref, wsel_ref, hm_ref, xe_ref, gs_ref):
    n_tok = hm_ref.shape[0]
    onehot = slot_ref[0] == lax.broadcasted_iota(I32, (cap, n_tok), 0)
    xe_ref[0] = jnp.dot(onehot.astype(BF16), hm_ref[...], preferred_element_type=F32).astype(BF16)
    gs_ref[0] = jnp.sum(jnp.where(onehot, wsel_ref[0], 0.0), axis=1, keepdims=True)


def _gather_tokens(slot, wsel, hm, batch, seq, cap):
    rows = batch * N_EXPERTS
    row_spec = pl.BlockSpec((1, 1, seq), lambda b, e: (b * N_EXPERTS + e, 0, 0))
    return pl.pallas_call(
        functools.partial(_gather_kernel, cap),
        grid=(batch, N_EXPERTS),
        in_specs=[row_spec, row_spec, pl.BlockSpec((seq, D_MODEL), lambda b, e: (b, 0))],
        out_specs=[
            pl.BlockSpec((1, cap, D_MODEL), lambda b, e: (e, b, 0)),
            pl.BlockSpec((1, cap, 1), lambda b, e: (e, b, 0)),
        ],
        out_shape=[
            jax.ShapeDtypeStruct((N_EXPERTS, batch * cap, D_MODEL), BF16),
            jax.ShapeDtypeStruct((N_EXPERTS, batch * cap, 1), F32),
        ],
        compiler_params=_params("parallel", "arbitrary"),
        name="moe_gather",
    )(slot.reshape(rows, 1, seq), wsel.reshape(rows, 1, seq), hm)


def _expert_kernel(n_e, n_t, xe_ref, wg_ref, wu_ref, wd_ref, gs_ref, ye_ref, mid_ref):
    g = pl.program_id(0)
    s = pl.program_id(1)
    t = s // 2
    tf = wg_ref.shape[-1]

    @pl.when(jnp.logical_and(s % 2 == 0, g < n_e))
    def _():
        x = xe_ref[0]
        a = jnp.dot(x, wg_ref[...].astype(BF16), preferred_element_type=F32)
        u = jnp.dot(x, wu_ref[...].astype(BF16), preferred_element_type=F32)
        mid_ref[g % 2, t] = (a * jax.nn.sigmoid(a) * u).astype(BF16)

    @pl.when(jnp.logical_and(s % 2 == 1, g >= 1))
    def _():
        prev = (g - 1) % 2
        acc = jnp.dot(mid_ref[prev, 0], wd_ref[0:tf, :].astype(BF16), preferred_element_type=F32)
        for f in range(1, n_t):
            acc += jnp.dot(mid_ref[prev, f], wd_ref[f * tf:(f + 1) * tf, :].astype(BF16),
                           preferred_element_type=F32)
        ye_ref[0] = (acc * gs_ref[0]).astype(BF16)


def _expert_ffn(xe, gs, w_gate, w_up, w_down, layer, tf):
    n_e, rows, d = xe.shape
    ff = w_gate.shape[-1]
    n_t = ff // tf
    assert d // tf == n_t
    last = n_e - 1

    n_s = 2 * n_t

    def fill_tile(g, s, lead):
        v = jnp.minimum(g * n_s + s + lead, n_e * n_s - 1)
        return (layer, v // n_s, 0, (v % n_s) // 2)

    def emit_tile(g, s):
        v = jnp.maximum(g * n_s + s - 1 - n_s, 0)
        return (layer, v // n_s, 0, (v % n_s) // 2)

    return pl.pallas_call(
        functools.partial(_expert_kernel, n_e, n_t),
        grid=(n_e + 1, n_s),
        in_specs=[
            pl.BlockSpec((1, rows, d), lambda g, s: (jnp.minimum(g, last), 0, 0)),
            pl.BlockSpec((None, None, d, tf), functools.partial(fill_tile, lead=1)),
            pl.BlockSpec((None, None, d, tf), functools.partial(fill_tile, lead=0)),
            pl.BlockSpec((None, None, ff, tf), emit_tile),
            pl.BlockSpec((1, rows, 1), lambda g, s: (jnp.maximum(g - 1, 0), 0, 0)),
        ],
        out_specs=pl.BlockSpec((1, rows, tf), lambda g, s: emit_tile(g, s)[1:]),
        out_shape=jax.ShapeDtypeStruct((n_e, rows, d), BF16),
        scratch_shapes=[pltpu.VMEM((2, n_t, rows, tf), BF16)],
        compiler_params=_params("arbitrary", "arbitrary"),
        name="moe_expert_ffn",
    )(xe, w_gate, w_up, w_down, gs)


def _scatter_kernel(cap, slot_ref, ye_ref, x_ref, *rest):
    o_ref = rest[-1]
    n_e = ye_ref.shape[0]
    col = lax.broadcasted_iota(I32, (x_ref.shape[0], cap), 1)
    slots = slot_ref[0]
    onehot = jnp.concatenate([(slots[:, e:e + 1] == col).astype(BF16) for e in range(n_e)], axis=1)
    ye = ye_ref[...].reshape(n_e * cap, ye_ref.shape[-1])
    y = x_ref[...] + jnp.dot(onehot, ye, preferred_element_type=F32)
    o_ref[...] = _rms(y, rest[0][...]) if len(rest) == 2 else y


def _scatter_add(slot_t, ye, x2d, batch, seq, cap, tr, final_gain=None):
    d = x2d.shape[-1]
    per_b = seq // tr
    in_specs = [
        pl.BlockSpec((1, tr, LANES), lambda b, i: (b, i, 0)),
        pl.BlockSpec((N_EXPERTS, cap, d), lambda b, i: (0, b, 0)),
        pl.BlockSpec((tr, d), lambda b, i: (b * per_b + i, 0)),
    ]
    args = [slot_t, ye, x2d]
    if final_gain is not None:
        in_specs.append(pl.BlockSpec((1, d), lambda b, i: (0, 0)))
        args.append(final_gain.reshape(1, d))
    return pl.pallas_call(
        functools.partial(_scatter_kernel, cap),
        grid=(batch, per_b),
        in_specs=in_specs,
        out_specs=pl.BlockSpec((tr, d), lambda b, i: (b * per_b + i, 0)),
        out_shape=jax.ShapeDtypeStruct(x2d.shape, F32),
        compiler_params=_params("parallel", "arbitrary"),
        name="moe_scatter_add",
    )(*args)


def _expert_choice_ffn(x2d, hm, aff, w_gate, w_up, w_down, layer, batch, seq, final_gain=None):
    cap = EC_CAPACITY * seq // N_EXPERTS
    slot, wsel, slot_t = _topk_select(aff.reshape(batch * N_EXPERTS, seq), cap, N_EXPERTS)
    xe, gs = _gather_tokens(slot, wsel, hm, batch, seq, cap)
    ye = _expert_ffn(xe, gs, w_gate, w_up, w_down, layer, 512)
    return _scatter_add(slot_t, ye, x2d, batch, seq, cap, 256, final_gain)


def kernel(x, mem, norm_mix, w_in, na_rpb, hy_conv_w, hy_conv_b, hy_filt_w1, hy_filt_b1, hy_filt_w2, hy_filt_b2, hy_filt_w3, hy_sin_freq, hy_skip_d, branch_norm, w_out, norm_cross, mem_norm, w_cq, w_ckv, w_co, norm_moe, w_router, w_gate, w_up, w_down, final_norm):
    batch, seq, d = x.shape
    n_mem = mem.shape[1]
    depth = w_in.shape[0]
    m = batch * seq
    na_cols = 3 * NA_W
    hy_cols = 3 * HY_W
    rows = seq // GRID_W

    dft_fwd, dft_inv = _dft_tables(seq)
    x2d = x.reshape(m, d)
    mem2d = mem.reshape(batch * n_mem, d)

    w_out_b, w_cq_b, w_co_b = (w.astype(BF16) for w in (w_out, w_cq, w_co))
    p_in = w_in.shape[-1]
    kv = _norm_proj_layers(mem2d, mem_norm, w_ckv, 1024)

    for l in range(depth):
        assert p_in == 3 * na_cols
        slabs = [_norm_proj_resident(x2d, norm_mix[l], w_in, l, s, 3, 512, "in_proj").reshape(batch, seq, na_cols)
                 for s in range(3)]

        y_na = _neighbourhood_attention(slabs[0], _na_bias_table(na_rpb[l], rows), batch, seq)
        y_hy = _hyena_mixer(slabs[1], 0, hy_conv_w[l], hy_conv_b[l], hy_filt_w1[l],
                            hy_filt_b1[l], hy_filt_w2[l], hy_filt_b2[l], hy_filt_w3[l], hy_sin_freq[l],
                            hy_skip_d[l], dft_fwd, dft_inv, batch, seq)
        y_ret = _retention_mixer(slabs[1], hy_cols, slabs[2], 0, batch, seq)
        x2d, q = _branch_out_cross_q(y_na.reshape(m, NA_W), y_hy.reshape(m, HY_W), y_ret.reshape(m, RET_W),
                                     branch_norm[l], w_out_b, x2d, norm_cross[l], w_cq_b, l, 512)
        x2d, hm, aff = _cross_out_and_router(q, kv, w_co_b, l, x2d, norm_moe[l], w_router[l], batch, seq, n_mem,
                                             256)

        x2d = _expert_choice_ffn(x2d, hm, aff, w_gate, w_up, w_down, l, batch, seq,
                                 final_gain=final_norm if l == depth - 1 else None)

    return x2d.reshape(batch, seq, d)
```

```python
import functools
import math

import numpy as np
import jax
import jax.numpy as jnp
from jax import lax
from jax.experimental import pallas as pl
from jax.experimental.pallas import tpu as pltpu

F32 = jnp.float32
BF16 = jnp.bfloat16
I32 = jnp.int32

D_MODEL = 2048
GRID_W = 64
NA_HEAD_DIM = 64
NA_W = 768
NA_HEADS = 12
NA_KR = 8
NA_KC = 16
HY_W = 512
HY_ORDER = 2
HY_BANDS = 8
HY_POS_DIM = 17
HY_FILT_FF = 64
RET_HEAD_DIM = 128
RET_W = 768
RET_HEADS = 6
RET_CHUNK = 128
CROSS_HEADS = 4
CROSS_HEAD_DIM = 512
N_EXPERTS = 16
EXPERT_FF = 2048
EC_CAPACITY = 2
RMS_EPS = 1e-6
GN_EPS = 1e-5

MASK_VALUE = -1e30
LANES = 128
VMEM_LIMIT_BYTES = 56 * 1024 * 1024

NT_DIMS = (((1,), (1,)), ((), ()))


def _params(*sem):
    return pltpu.CompilerParams(dimension_semantics=sem, vmem_limit_bytes=VMEM_LIMIT_BYTES)


def _rms(xf, g):
    return xf * lax.rsqrt(jnp.mean(xf * xf, axis=-1, keepdims=True) + RMS_EPS) * g


def _norm_proj_layers_kernel(x_ref, g_ref, w_ref, o_ref, h_ref):
    @pl.when(pl.program_id(0) == 0)
    def _():
        h_ref[...] = _rms(x_ref[...], g_ref[...]).astype(BF16)

    o_ref[...] = jnp.dot(h_ref[...], w_ref[...].astype(BF16), preferred_element_type=F32).astype(o_ref.dtype)


def _norm_proj_layers(x2d, gain, w, tn):
    m, k = x2d.shape
    n_layers, _, n = w.shape
    per_layer = n // tn
    assert n % tn == 0
    return pl.pallas_call(
        _norm_proj_layers_kernel,
        grid=(n_layers * per_layer,),
        in_specs=[
            pl.BlockSpec((m, k), lambda j: (0, 0)),
            pl.BlockSpec((1, k), lambda j: (0, 0)),
            pl.BlockSpec((None, k, tn), lambda j: (j // per_layer, 0, j % per_layer)),
        ],
        out_specs=pl.BlockSpec((m, tn), lambda j: (0, j)),
        out_shape=jax.ShapeDtypeStruct((m, n_layers * n), BF16),
        scratch_shapes=[pltpu.VMEM((m, k), BF16)],
        compiler_params=_params("arbitrary"),
        name="cross_kv_proj",
    )(x2d, gain.reshape(1, k), w)


def _norm_proj_resident_kernel(x_ref, g_ref, w_ref, o_ref, wb_ref):
    @pl.when(pl.program_id(0) == 0)
    def _():
        wb_ref[...] = w_ref[...].astype(BF16)

    h = _rms(x_ref[...], g_ref[...]).astype(BF16)
    o_ref[...] = jnp.dot(h, wb_ref[...], preferred_element_type=F32).astype(o_ref.dtype)


def _norm_proj_resident(x2d, gain, w, layer, slab, n_slab, tm, name):
    m, k = x2d.shape
    width = w.shape[-1] // n_slab
    assert w.shape[-1] % n_slab == 0 and m % tm == 0
    return pl.pallas_call(
        _norm_proj_resident_kernel,
        grid=(m // tm,),
        in_specs=[
            pl.BlockSpec((tm, k), lambda i: (i, 0)),
            pl.BlockSpec((1, k), lambda i: (0, 0)),
            pl.BlockSpec((None, k, width), lambda i: (layer, 0, slab)),
        ],
        out_specs=pl.BlockSpec((tm, width), lambda i: (i, 0)),
        out_shape=jax.ShapeDtypeStruct((m, width), BF16),
        scratch_shapes=[pltpu.VMEM((k, width), BF16)],
        compiler_params=_params("arbitrary"),
        name=name,
    )(x2d, gain.reshape(1, k), w)


def _cross_prologue(h_ref, q_ref, k_ref, v_ref):
    dh = CROSS_HEAD_DIM
    for h in range(CROSS_HEADS):
        sl = slice(h * dh, (h + 1) * dh)
        s = lax.dot_general(q_ref[:, sl], k_ref[:, sl], NT_DIMS, preferred_element_type=F32) * (dh ** -0.5)
        m = jnp.max(s, axis=-1, keepdims=True)
        p = jnp.exp(s - m)
        l = jnp.sum(p, axis=-1, keepdims=True)
        o = jnp.dot(p.astype(BF16), v_ref[:, sl], preferred_element_type=F32) * (1.0 / l)
        h_ref[:, sl] = o.astype(BF16)


def _branch_prologue(h_ref, na_ref, hy_ref, ret_ref, g_ref):
    def nrm(p):
        return p * lax.rsqrt(jnp.mean(p * p, axis=-1, keepdims=True) + RMS_EPS)

    y = jnp.concatenate([nrm(r[...].astype(F32)) for r in (na_ref, hy_ref, ret_ref)], axis=-1)
    h_ref[...] = (y * g_ref[...]).astype(BF16)


def _branch_out_cross_q_kernel(na_ref, hy_ref, ret_ref, g_ref, wo_ref, res_ref, gq_ref, wq_ref, x_ref, q_ref,
                               h_ref):
    _branch_prologue(h_ref, na_ref, hy_ref, ret_ref, g_ref)
    x = res_ref[...] + jnp.dot(h_ref[...], wo_ref[...], preferred_element_type=F32)
    x_ref[...] = x
    hq = _rms(x, gq_ref[...]).astype(BF16)
    q_ref[...] = jnp.dot(hq, wq_ref[...], preferred_element_type=F32).astype(BF16)


def _branch_out_cross_q(y_na, y_hy, y_ret, gain, w_out, res, gain_q, w_cq, layer, tm):
    m, d = res.shape
    rows = lambda width: pl.BlockSpec((tm, width), lambda i: (i, 0))
    vec = pl.BlockSpec((1, d), lambda i: (0, 0))
    square = pl.BlockSpec((None, d, d), lambda i: (layer, 0, 0))
    return pl.pallas_call(
        _branch_out_cross_q_kernel,
        grid=(m // tm,),
        in_specs=[rows(NA_W), rows(HY_W), rows(RET_W), vec, square, rows(d), vec, square],
        out_specs=[rows(d), rows(d)],
        out_shape=[jax.ShapeDtypeStruct((m, d), F32), jax.ShapeDtypeStruct((m, d), BF16)],
        scratch_shapes=[pltpu.VMEM((tm, d), BF16)],
        compiler_params=_params("parallel"),
        name="branch_out_cross_q",
    )(y_na, y_hy, y_ret, gain.reshape(1, d), w_out, res, gain_q.reshape(1, d), w_cq)


NA_PAIR = 2
NA_BAND = NA_KR + NA_PAIR


def _na_band_base(r, rows):
    return np.clip(r - NA_KR // 2, 0, rows - NA_BAND)


@functools.lru_cache(maxsize=None)
def _na_variants(rows):
    assert rows % NA_PAIR == 0 and rows >= NA_BAND + 2
    n_var = NA_KR // 2 + 1
    dr = np.full((n_var, NA_BAND, NA_PAIR), -2, np.int64)
    for r in range(0, rows, NA_PAIR):
        base = int(_na_band_base(r, rows))
        v = (r - base) // 2
        for j in range(NA_PAIR):
            rs = int(np.clip(r + j - NA_KR // 2, 0, rows - NA_KR))
            for i in range(NA_BAND):
                val = base + i - (r + j) + (NA_KR - 1) if rs <= base + i < rs + NA_KR else -1
                assert dr[v, i, j] in (-2, val)
                dr[v, i, j] = val
    assert (dr > -2).all()
    return dr


def _na_bias_table(rpb, rows):
    c = np.arange(GRID_W)
    col_start = np.clip(c - NA_KC // 2, 0, GRID_W - NA_KC)
    col_in = (c[None, :] >= col_start[:, None]) & (c[None, :] < col_start[:, None] + NA_KC)
    dc = np.clip(c[None, :] - c[:, None] + (NA_KC - 1), 0, 2 * NA_KC - 2)
    onehot = (dc.T.reshape(-1)[None, :] == np.arange(2 * NA_KC - 1)[:, None]).astype(np.float32)
    cols = jnp.einsum("hab,bn->han", rpb.astype(F32), jnp.asarray(onehot), precision=lax.Precision.HIGHEST)
    cols = cols.reshape(NA_HEADS, 2 * NA_KR - 1, GRID_W, GRID_W)
    cols = jnp.where(col_in.T[None, None], cols, MASK_VALUE)
    dr = _na_variants(rows)
    n_var = dr.shape[0]
    return pl.pallas_call(
        functools.partial(_na_table_kernel, dr),
        grid=(NA_HEADS,),
        in_specs=[pl.BlockSpec((1, 2 * NA_KR - 1, GRID_W, GRID_W), lambda h: (h, 0, 0, 0))],
        out_specs=pl.BlockSpec((1, n_var, NA_BAND * GRID_W, NA_PAIR * GRID_W), lambda h: (h, 0, 0, 0)),
        out_shape=jax.ShapeDtypeStruct((NA_HEADS, n_var, NA_BAND * GRID_W, NA_PAIR * GRID_W), F32),
        compiler_params=_params("parallel"),
        name="na_bias_table",
    )(cols)


def _na_table_kernel(dr, cols_ref, o_ref):
    masked = jnp.full((GRID_W, GRID_W), MASK_VALUE, F32)
    for v in range(dr.shape[0]):
        for i in range(NA_BAND):
            blocks = [cols_ref[0, int(dr[v, i, j])] if dr[v, i, j] >= 0 else masked for j in range(NA_PAIR)]
            o_ref[0, v, i * GRID_W:(i + 1) * GRID_W, :] = jnp.concatenate(blocks, axis=-1)


def _na_kernel(rows, q_ref, k_ref, v_ref, bias_ref, o_ref, vt_ref, s0_ref, s1_ref, p0_ref, p1_ref, l0_ref,
               l1_ref):
    dh = NA_HEAD_DIM
    nq = NA_PAIR * GRID_W
    nk = NA_BAND * GRID_W
    n_chunk = nk // LANES
    lane = lax.broadcasted_iota(I32, (nq, 2 * dh), 1)

    for ch in range(vt_ref.shape[0]):
        vt_ref[ch] = v_ref[0, ch * LANES:(ch + 1) * LANES, :].astype(F32).T.astype(BF16)

    n_steps = rows // NA_PAIR

    def band_base(p):
        return jnp.clip(p * NA_PAIR - NA_KR // 2, 0, rows - NA_BAND)


    def scores(p, s_out):
        r = p * NA_PAIR
        base = band_base(p)
        variant = (r - base) // 2
        q = q_ref[0, pl.ds(pl.multiple_of(r * GRID_W, nq), nq), :] * (dh ** -0.5)
        kb = k_ref[0, pl.ds(pl.multiple_of(base * GRID_W, LANES), nk), :]
        zero = jnp.zeros_like(q)
        q2 = jnp.concatenate([jnp.where(lane < dh, q, zero), jnp.where(lane >= dh, q, zero)], axis=0)
        st2 = lax.dot_general(kb, q2, NT_DIMS, preferred_element_type=F32)
        for hh in range(2):
            s_out[:, hh * nq:(hh + 1) * nq] = st2[:, hh * nq:(hh + 1) * nq] + bias_ref[hh, variant]

    def softmax(s_in, p_out, l_out):
        st = s_in[...]
        pt = jnp.exp(st - jnp.max(st, axis=0, keepdims=True))
        p_out[...] = pt.astype(BF16)
        l_out[...] = 1.0 / jnp.sum(pt, axis=0, keepdims=True)

    def values(p, p_in, l_in):
        r = p * NA_PAIR
        c0 = band_base(p) // 2
        vt = jnp.concatenate([vt_ref[c0 + i] for i in range(n_chunk)], axis=1)
        pt = p_in[...]
        linv = l_in[...]
        outs = []
        for hh in range(2):
            cols = slice(hh * nq, (hh + 1) * nq)
            ot = jnp.dot(vt[hh * dh:(hh + 1) * dh, :], pt[:, cols], preferred_element_type=F32)
            outs.append(ot * linv[:, cols])
        o_ref[0, pl.ds(pl.multiple_of(r * GRID_W, nq), nq), :] = jnp.concatenate(outs, axis=0).T.astype(BF16)

    s_slots = (s0_ref, s1_ref)
    p_slots = (p0_ref, p1_ref)
    l_slots = (l0_ref, l1_ref)

    def step(t, parity, do_scores=True, do_softmax=True, do_values=True):
        a, b = parity, 1 - parity
        t = jnp.asarray(t, I32)
        if do_scores:
            scores(t, s_slots[a])
        if do_softmax:
            softmax(s_slots[b], p_slots[b], l_slots[b])
        if do_values:
            values(t - 2, p_slots[a], l_slots[a])

    assert n_steps % 2 == 0 and n_steps >= 4
    step(0, 0, do_softmax=False, do_values=False)
    step(1, 1, do_values=False)

    def steady(i, carry):
        t = 2 + 2 * i
        step(t, 0)
        step(t + 1, 1)
        return carry

    lax.fori_loop(0, (n_steps - 2) // 2, steady, 0, unroll=True)
    step(n_steps, 0, do_scores=False)
    step(n_steps + 1, 1, do_scores=False, do_softmax=False)


def _neighbourhood_attention(proj_na, bias_tbl, batch, seq):
    rows = seq // GRID_W
    n_pairs = NA_HEADS // 2
    blk = (1, seq, 2 * NA_HEAD_DIM)
    n_var, nk, nq = bias_tbl.shape[1:]
    return pl.pallas_call(
        functools.partial(_na_kernel, rows),
        grid=(batch, n_pairs),
        in_specs=[
            pl.BlockSpec(blk, lambda b, h: (b, 0, h)),
            pl.BlockSpec(blk, lambda b, h: (b, 0, n_pairs + h)),
            pl.BlockSpec(blk, lambda b, h: (b, 0, 2 * n_pairs + h)),
            pl.BlockSpec((2, n_var, nk, nq), lambda b, h: (h, 0, 0, 0)),
        ],
        out_specs=pl.BlockSpec(blk, lambda b, h: (b, 0, h)),
        out_shape=jax.ShapeDtypeStruct((batch, seq, NA_W), BF16),
        scratch_shapes=[pltpu.VMEM((seq // LANES, 2 * NA_HEAD_DIM, LANES), BF16),
                        pltpu.VMEM((nk, 2 * nq), F32), pltpu.VMEM((nk, 2 * nq), F32),
                        pltpu.VMEM((nk, 2 * nq), BF16), pltpu.VMEM((nk, 2 * nq), BF16),
                        pltpu.VMEM((1, 2 * nq), F32), pltpu.VMEM((1, 2 * nq), F32)],
        compiler_params=_params("parallel", "arbitrary"),
        name="neighbourhood_attention",
    )(proj_na, proj_na, proj_na, bias_tbl)


@functools.lru_cache(maxsize=None)
def _dft_factors(seq):
    n = 2 * seq
    t = np.arange(seq, dtype=np.int64)
    f1 = np.arange(seq // 64, dtype=np.int64)
    f0 = np.arange(64, dtype=np.int64)
    a = 2.0 * np.pi * ((64 * f1[:, None] * t[None, :]) % n).astype(np.float64) / n
    b = 2.0 * np.pi * ((f0[:, None] * t[None, :]) % n).astype(np.float64) / n
    ny = np.where(t % 2 == 0, 1.0, -1.0)
    return tuple(np.asarray(v, np.float32) for v in (np.cos(a), np.sin(a), np.cos(b), np.sin(b), ny))


def _dft_table_kernel(u_ref, v_ref, cb_ref, sb_ref, fwd_ref, inv_ref):
    sin_half = pl.program_id(0) == 1
    first = pl.program_id(1) == 0
    n_fine, seq = cb_ref.shape
    row = lax.broadcasted_iota(I32, (n_fine, seq), 0)
    col = lax.broadcasted_iota(I32, (n_fine, seq), 1)
    for c in range(u_ref.shape[1]):
        tile = u_ref[0, c:c + 1, :] * cb_ref[...] + v_ref[0, c:c + 1, :] * sb_ref[...]
        rows = slice(c * n_fine, (c + 1) * n_fine)
        fwd = tile
        if c == 0:
            fwd_nyq = jnp.logical_and(jnp.logical_and(sin_half, first), row == 0)
            fwd = jnp.where(fwd_nyq, (1 - 2 * (col & 1)).astype(F32), tile)
        fwd_ref[0, rows, :] = fwd.astype(BF16)
        inv_nyq = jnp.logical_and(sin_half, col == 0)
        inv_ref[rows, :] = jnp.where(inv_nyq, (1 - 2 * (row & 1)).astype(F32), tile).astype(BF16)


def _dft_tables(seq):
    ca, sa, cb, sb, _ = _dft_factors(seq)
    u = np.stack([ca, sa])
    v = np.stack([-sa, ca])
    n_coarse = 8
    rows = 64 * n_coarse
    coarse_spec = pl.BlockSpec((1, n_coarse, seq), lambda h, i: (h, i, 0))
    fine_spec = pl.BlockSpec((64, seq), lambda h, i: (0, 0))
    return pl.pallas_call(
        _dft_table_kernel,
        grid=(2, seq // rows),
        in_specs=[coarse_spec, coarse_spec, fine_spec, fine_spec],
        out_specs=[pl.BlockSpec((1, rows, seq), lambda h, i: (h, i, 0)),
                   pl.BlockSpec((rows, seq), lambda h, i: (i, h))],
        out_shape=[jax.ShapeDtypeStruct((2, seq, seq), BF16), jax.ShapeDtypeStruct((seq, 2 * seq), BF16)],
        compiler_params=_params("parallel", "arbitrary"),
        name="dft_tables",
    )(jnp.asarray(u), jnp.asarray(v), jnp.asarray(cb), jnp.asarray(sb))


@functools.lru_cache(maxsize=None)
def _hyena_consts(seq):
    t = np.arange(seq, dtype=np.float64)
    t01 = t / (seq - 1)
    bands = np.linspace(1e-4, HY_BANDS - 1, HY_BANDS)
    ang = (2.0 * math.pi) * (t[:, None] / seq) * bands[None, :]
    feats = np.concatenate([t01[:, None], np.cos(ang), -np.sin(ang)], axis=-1)
    feats_p = np.zeros((seq, LANES), np.float32)
    feats_p[:, :HY_POS_DIM] = feats
    min_decay = math.log(1e-2) / 1.5
    max_decay = math.log(1e-2) / 0.3
    deltas = np.abs(np.linspace(min_decay, max_decay, HY_W))
    window = np.exp(-t01[:, None] * deltas[None, :]).astype(np.float32)
    return feats_p, window


def _filter_kernel(feats_ref, w1_ref, b1_ref, w2_ref, b2_ref, freq_ref, w3f_ref, w3b_ref, win_ref, sum_ref,
                   diff_ref, hid_ref):
    hp = lax.Precision.HIGHEST

    @pl.when(pl.program_id(0) == 0)
    def _():
        f = freq_ref[...]
        h1 = jnp.sin(f * (jnp.dot(feats_ref[...], w1_ref[...], precision=hp, preferred_element_type=F32)
                          + b1_ref[...]))
        hid_ref[...] = jnp.sin(f * (jnp.dot(h1, w2_ref[...], precision=hp, preferred_element_type=F32)
                                    + b2_ref[...]))

    hid = hid_ref[...]
    win = win_ref[...]
    fwd = jnp.dot(hid, w3f_ref[...], precision=hp, preferred_element_type=F32) * win
    bwd = jnp.dot(hid, w3b_ref[...], precision=hp, preferred_element_type=F32) * win
    bwd = jnp.where(lax.broadcasted_iota(I32, bwd.shape, 0) == 0, 0.0, bwd)
    sum_ref[...] = (fwd + bwd).astype(BF16)
    diff_ref[...] = (fwd - bwd).astype(BF16)


def _hyena_filters_time(w1, b1, w2, b2, w3, freq, seq):
    feats, window = _hyena_consts(seq)
    w1p = jnp.zeros((LANES, HY_FILT_FF), F32).at[:HY_POS_DIM].set(w1)
    full = lambda shape: pl.BlockSpec(shape, lambda o: (0,) * len(shape))
    out_spec = pl.BlockSpec((seq, HY_W), lambda o: (0, o))
    out_shape = jax.ShapeDtypeStruct((seq, HY_ORDER * HY_W), BF16)
    return pl.pallas_call(
        _filter_kernel,
        grid=(HY_ORDER,),
        in_specs=[
            full((seq, LANES)), full((LANES, HY_FILT_FF)), full((1, HY_FILT_FF)),
            full((HY_FILT_FF, HY_FILT_FF)), full((1, HY_FILT_FF)), full((1, HY_FILT_FF)),
            pl.BlockSpec((HY_FILT_FF, HY_W), lambda o: (0, 2 * o)),
            pl.BlockSpec((HY_FILT_FF, HY_W), lambda o: (0, 2 * o + 1)),
            full((seq, HY_W)),
        ],
        out_specs=[out_spec, out_spec],
        out_shape=[out_shape, out_shape],
        scratch_shapes=[pltpu.VMEM((seq, HY_FILT_FF), F32)],
        compiler_params=_params("arbitrary"),
        name="hyena_filter_mlp",
    )(jnp.asarray(feats), w1p, b1.reshape(1, -1), w2, b2.reshape(1, -1), freq.reshape(1, -1), w3, w3,
      jnp.asarray(window))


def _filter_dft_kernel(f_ref, sum_ref, diff_ref, o_ref):
    o_ref[0] = jnp.dot(f_ref[0], sum_ref[...], preferred_element_type=F32)
    o_ref[1] = jnp.dot(f_ref[1], diff_ref[...], preferred_element_type=F32)

    @pl.when(pl.program_id(0) == 0)
    def _():
        top = jnp.dot(f_ref[1, 0:8, :], sum_ref[...], preferred_element_type=F32)
        row = lax.broadcasted_iota(I32, top.shape, 0)
        o_ref[1, 0:8, :] = jnp.where(row == 0, top, o_ref[1, 0:8, :])


def _filter_dft(dft_fwd, filt_sum, filt_diff, seq, fb):
    return pl.pallas_call(
        _filter_dft_kernel,
        grid=(seq // fb, HY_ORDER),
        in_specs=[
            pl.BlockSpec((2, fb, seq), lambda f, o: (0, f, 0)),
            pl.BlockSpec((seq, HY_W), lambda f, o: (0, o)),
            pl.BlockSpec((seq, HY_W), lambda f, o: (0, o)),
        ],
        out_specs=pl.BlockSpec((2, fb, HY_W), lambda f, o: (0, f, o)),
        out_shape=jax.ShapeDtypeStruct((2, seq, HY_ORDER * HY_W), F32),
        compiler_params=_params("parallel", "arbitrary"),
        name="hyena_filter_dft",
    )(dft_fwd, filt_sum, filt_diff)


def _short_conv_kernel(p_ref, w_ref, b_ref, o_ref):
    p = p_ref[0].astype(F32)
    seq = p.shape[0]
    row = lax.broadcasted_iota(I32, p.shape, 0)
    prev = jnp.where(row == 0, 0.0, pltpu.roll(p, 1, 0))
    nxt = jnp.where(row == seq - 1, 0.0, pltpu.roll(p, seq - 1, 0))
    w = w_ref[...]
    o_ref[0] = (prev * w[0:1] + p * w[1:2] + nxt * w[2:3] + b_ref[...]).astype(o_ref.dtype)


def _short_conv(proj, col0, conv_w, conv_b, batch, seq, tc):
    n_cols = conv_w.shape[-1]
    assert col0 % tc == 0 and n_cols % tc == 0
    cb0 = col0 // tc
    return pl.pallas_call(
        _short_conv_kernel,
        grid=(batch, n_cols // tc),
        in_specs=[
            pl.BlockSpec((1, seq, tc), lambda b, c: (b, 0, cb0 + c)),
            pl.BlockSpec((3, tc), lambda b, c: (0, c)),
            pl.BlockSpec((1, tc), lambda b, c: (0, c)),
        ],
        out_specs=pl.BlockSpec((1, seq, tc), lambda b, c: (b, 0, c)),
        out_shape=jax.ShapeDtypeStruct((batch, seq, n_cols), BF16),
        compiler_params=_params("parallel", "arbitrary"),
        name="hyena_short_conv",
    )(proj, conv_w, conv_b.reshape(1, -1))


def _spectrum_kernel(n_fft, f_ref, z_ref, k_ref, o_ref):
    z = z_ref[0].astype(BF16)
    xr = jnp.dot(f_ref[0], z, preferred_element_type=F32)
    xs = jnp.dot(f_ref[1], z, preferred_element_type=F32)
    kr = k_ref[0]
    ks = k_ref[1]
    row = lax.broadcasted_iota(I32, xr.shape, 0)
    edge = jnp.logical_and(row == 0, pl.program_id(0) == 0)
    yr = jnp.where(edge, xr * kr * (1.0 / n_fft), (xr * kr - xs * ks) * (2.0 / n_fft))
    ys = jnp.where(edge, xs * ks * (1.0 / n_fft), (xr * ks + xs * kr) * (2.0 / n_fft))
    o_ref[0, 0] = yr.astype(BF16)
    o_ref[0, 1] = ys.astype(BF16)


def _spectrum_product(dft_fwd, z_arr, z_col, kfreq, order, batch, seq, fb):
    return pl.pallas_call(
        functools.partial(_spectrum_kernel, 2 * seq),
        grid=(seq // fb, batch),
        in_specs=[
            pl.BlockSpec((2, fb, seq), lambda f, b: (0, f, 0)),
            pl.BlockSpec((1, seq, HY_W), lambda f, b: (b, 0, z_col)),
            pl.BlockSpec((2, fb, HY_W), lambda f, b: (0, f, order)),
        ],
        out_specs=pl.BlockSpec((1, 2, fb, HY_W), lambda f, b: (b, 0, f, 0)),
        out_shape=jax.ShapeDtypeStruct((batch, 2, seq, HY_W), BF16),
        compiler_params=_params("parallel", "arbitrary"),
        name="hyena_spectrum",
    )(dft_fwd, z_arr, kfreq)


def _inverse_kernel(ft_ref, y_ref, gate_ref, z_ref, d_ref, o_ref):
    conv = jnp.dot(ft_ref[...], y_ref[0], preferred_element_type=F32)
    o_ref[0] = (gate_ref[0].astype(F32) * (conv + d_ref[...] * z_ref[0].astype(F32))).astype(o_ref.dtype)


def _inverse_gate(dft_inv, y, gate_arr, gate_col, z_arr, z_col, skip_row, batch, seq, tb):
    y2 = y.reshape(batch, 2 * seq, HY_W)
    return pl.pallas_call(
        _inverse_kernel,
        grid=(seq // tb, batch),
        in_specs=[
            pl.BlockSpec((tb, 2 * seq), lambda t, b: (t, 0)),
            pl.BlockSpec((1, 2 * seq, HY_W), lambda t, b: (b, 0, 0)),
            pl.BlockSpec((1, tb, HY_W), lambda t, b: (b, t, gate_col)),
            pl.BlockSpec((1, tb, HY_W), lambda t, b: (b, t, z_col)),
            pl.BlockSpec((1, HY_W), lambda t, b: (0, 0)),
        ],
        out_specs=pl.BlockSpec((1, tb, HY_W), lambda t, b: (b, t, 0)),
        out_shape=jax.ShapeDtypeStruct((batch, seq, HY_W), BF16),
        compiler_params=_params("parallel", "arbitrary"),
        name="hyena_inverse_gate",
    )(dft_inv, y2, gate_arr, z_arr, skip_row.reshape(1, HY_W))


def _hyena_mixer(proj, col0, conv_w, conv_b, w1, b1, w2, b2, w3, freq, skip_d, dft_fwd, dft_inv, batch, seq):
    s = _short_conv(proj, col0, conv_w, conv_b, batch, seq, 768)
    filt_sum, filt_diff = _hyena_filters_time(w1, b1, w2, b2, w3, freq, seq)
    kfreq = _filter_dft(dft_fwd, filt_sum, filt_diff, seq, 512)
    z_arr, z_col = s, 2
    for o in range(HY_ORDER):
        y = _spectrum_product(dft_fwd, z_arr, z_col, kfreq, o, batch, seq, 1024)
        z_arr = _inverse_gate(dft_inv, y, s, o, z_arr, z_col, skip_d[o], batch, seq, 1024)
        z_col = 0
    return z_arr


@functools.lru_cache(maxsize=None)
def _retention_consts(seq):
    c = RET_CHUNK
    half = RET_HEAD_DIM // 2
    inv = 1.0 / (10000.0 ** np.linspace(0.0, 1.0, half))
    ang = np.arange(seq, dtype=np.float64)[:, None] * inv[None, :]
    cos2 = np.concatenate([np.cos(ang), np.cos(ang)], axis=-1).astype(np.float32)
    sin2 = np.concatenate([-np.sin(ang), np.sin(ang)], axis=-1).astype(np.float32)
    hidx = np.arange(RET_HEADS, dtype=np.float64)
    lg_f = np.log1p(-np.exp2(-5.0 - hidx))[:, None, None]
    lg_b = np.log1p(-np.exp2(-5.5 - hidx))[:, None, None]
    i = np.arange(c, dtype=np.float64)
    diff = i[:, None] - i[None, :]
    ones = np.ones((1, c, c))
    dec = np.where(diff >= 0, np.exp(lg_f * np.maximum(diff, 0.0)), np.exp(lg_b * np.maximum(-diff, 0.0)))
    rowv = lambda v: v[:, :, None] * ones
    tab = np.stack([
        dec,
        rowv(np.exp(lg_f[:, :, 0] * (i + 1.0)[None, :])),
        rowv(np.exp(lg_f[:, :, 0] * (c - 1.0 - i)[None, :])),
        rowv(np.exp(lg_b[:, :, 0] * (c - i)[None, :])),
        rowv(np.exp(lg_b[:, :, 0] * i[None, :])),
        np.exp(lg_f * c) * ones,
        np.exp(lg_b * c) * ones,
    ], axis=1).astype(np.float32)
    return cos2, sin2, tab


def _retention_kernel(q_ref, k_ref, v_ref, g_ref, cos_ref, sin_ref, tab_ref, o_ref, qs_ref, ks_ref, kvf_ref,
                      kvb_ref, a0_ref, a1_ref, y0_ref, y1_ref):
    c = RET_CHUNK
    d = RET_HEAD_DIM
    seq = q_ref.shape[1]
    n_chunks = seq // c
    cos = cos_ref[...]
    sin = sin_ref[...]
    q = q_ref[0].astype(F32)
    k = k_ref[0].astype(F32)
    qs_ref[...] = (q * cos + pltpu.roll(q, d // 2, 1) * sin) * (d ** -0.5)
    ks_ref[...] = k * cos + pltpu.roll(k, d // 2, 1) * sin

    def mm(a, b):
        return jnp.dot(a.astype(BF16), b.astype(BF16), preferred_element_type=F32)

    def chunk(n):
        return pl.ds(pl.multiple_of(n * c, c), c)

    def kv_body(n, carry):
        kc = ks_ref[chunk(n), :]
        vc = v_ref[0, chunk(n), :]
        kvf_ref[n] = mm((kc * tab_ref[0, 2]).T, vc)
        kvb_ref[n] = mm((kc * tab_ref[0, 4]).T, vc)
        return carry

    lax.fori_loop(0, n_chunks, kv_body, 0, unroll=True)

    def scan_fwd(n, state):
        kv = kvf_ref[n]
        kvf_ref[n] = state
        return tab_ref[0, 5] * state + kv

    lax.fori_loop(0, n_chunks, scan_fwd, jnp.zeros((d, d), F32))

    def scan_bwd(m, state):
        n = n_chunks - 1 - m
        kv = kvb_ref[n]
        kvb_ref[n] = state
        return tab_ref[0, 6] * state + kv

    lax.fori_loop(0, n_chunks, scan_bwd, jnp.zeros((d, d), F32))

    def in_chunk(n, a_out):
        qc = qs_ref[chunk(n), :].astype(BF16)
        kc = ks_ref[chunk(n), :].astype(BF16)
        a = lax.dot_general(qc, kc, NT_DIMS, preferred_element_type=F32) * tab_ref[0, 0]
        a_out[...] = a.astype(BF16)

    def mix(n, a_in, y_out):
        qc = qs_ref[chunk(n), :]
        y_out[...] = (jnp.dot(a_in[...], v_ref[0, chunk(n), :], preferred_element_type=F32)
                      + mm(qc * tab_ref[0, 1], kvf_ref[n]) + mm(qc * tab_ref[0, 3], kvb_ref[n]))

    def finish(n, y_in):
        y = y_in[...]
        mu = jnp.mean(y, axis=-1, keepdims=True)
        yc = y - mu
        var = jnp.mean(yc * yc, axis=-1, keepdims=True)
        g = g_ref[0, chunk(n), :].astype(F32)
        o_ref[0, chunk(n), :] = (yc * lax.rsqrt(var + GN_EPS) * (g * jax.nn.sigmoid(g))).astype(BF16)

    a_slots = (a0_ref, a1_ref)
    y_slots = (y0_ref, y1_ref)

    def step(t, parity, do_a=True, do_mix=True, do_finish=True):
        t = jnp.asarray(t, I32)
        if do_a:
            in_chunk(t, a_slots[parity])
        if do_mix:
            mix(t - 1, a_slots[1 - parity], y_slots[1 - parity])
        if do_finish:
            finish(t - 2, y_slots[parity])

    assert n_chunks % 2 == 0 and n_chunks >= 4
    step(0, 0, do_mix=False, do_finish=False)
    step(1, 1, do_finish=False)

    def steady(i, carry):
        step(2 + 2 * i, 0)
        step(3 + 2 * i, 1)
        return carry

    lax.fori_loop(0, (n_chunks - 2) // 2, steady, 0, unroll=True)
    step(n_chunks, 0, do_a=False)
    step(n_chunks + 1, 1, do_a=False, do_mix=False)


def _retention_mixer(proj_q, col_q, proj, col0, batch, seq):
    cos2, sin2, tab = _retention_consts(seq)
    blk = (1, seq, RET_HEAD_DIM)
    h_ = RET_HEADS
    assert col0 % RET_HEAD_DIM == 0 and col_q % RET_HEAD_DIM == 0
    cq = col_q // RET_HEAD_DIM
    c0 = col0 // RET_HEAD_DIM - h_
    return pl.pallas_call(
        _retention_kernel,
        grid=(batch, RET_HEADS),
        in_specs=[
            pl.BlockSpec(blk, lambda b, h: (b, 0, cq + h)),
            pl.BlockSpec(blk, lambda b, h: (b, 0, c0 + h_ + h)),
            pl.BlockSpec(blk, lambda b, h: (b, 0, c0 + 2 * h_ + h)),
            pl.BlockSpec(blk, lambda b, h: (b, 0, c0 + 3 * h_ + h)),
            pl.BlockSpec((seq, RET_HEAD_DIM), lambda b, h: (0, 0)),
            pl.BlockSpec((seq, RET_HEAD_DIM), lambda b, h: (0, 0)),
            pl.BlockSpec((1, 7, RET_CHUNK, RET_CHUNK), lambda b, h: (h, 0, 0, 0)),
        ],
        out_specs=pl.BlockSpec(blk, lambda b, h: (b, 0, h)),
        out_shape=jax.ShapeDtypeStruct((batch, seq, RET_W), BF16),
        scratch_shapes=[pltpu.VMEM((seq, RET_HEAD_DIM), F32), pltpu.VMEM((seq, RET_HEAD_DIM), F32),
                        pltpu.VMEM((seq // RET_CHUNK, RET_HEAD_DIM, RET_HEAD_DIM), F32),
                        pltpu.VMEM((seq // RET_CHUNK, RET_HEAD_DIM, RET_HEAD_DIM), F32),
                        pltpu.VMEM((RET_CHUNK, RET_CHUNK), BF16), pltpu.VMEM((RET_CHUNK, RET_CHUNK), BF16),
                        pltpu.VMEM((RET_CHUNK, RET_HEAD_DIM), F32), pltpu.VMEM((RET_CHUNK, RET_HEAD_DIM), F32)],
        compiler_params=_params("parallel", "arbitrary"),
        name="retention",
    )(proj_q, proj, proj, proj, jnp.asarray(cos2), jnp.asarray(sin2), jnp.asarray(tab))


def _cross_out_router_kernel(q_ref, k_ref, v_ref, w_ref, res_ref, g_ref, wr_ref, x_ref, hm_ref, aff_ref, h_ref):
    n_e = aff_ref.shape[1]
    _cross_prologue(h_ref, q_ref, k_ref, v_ref)
    x = res_ref[...] + jnp.dot(h_ref[...], w_ref[...], preferred_element_type=F32)
    x_ref[...] = x
    h = _rms(x, g_ref[...])
    h_hi = h.astype(BF16)
    hm_ref[...] = h_hi
    h_lo = (h - h_hi.astype(F32)).astype(BF16)
    wr = wr_ref[...]
    w_hi = wr.astype(BF16)
    w_lo = (wr - w_hi.astype(F32)).astype(BF16)
    both = jnp.dot(h_hi, jnp.concatenate([w_hi, w_lo], axis=1), preferred_element_type=F32)
    logits = (both[:, :LANES] + both[:, LANES:] + jnp.dot(h_lo, w_hi, preferred_element_type=F32)).T[:n_e]
    m = jnp.max(logits, axis=0, keepdims=True)
    e = jnp.exp(logits - m)
    aff_ref[0] = e / jnp.sum(e, axis=0, keepdims=True)


def _cross_out_and_router(q, kv, w_co, layer, res, gain, w_router, batch, seq, n_mem, tm):
    m, d = q.shape
    per_b = seq // tm
    assert seq % tm == 0
    w_pad = jnp.zeros((d, LANES), F32).at[:, :N_EXPERTS].set(w_router)
    row_spec = pl.BlockSpec((tm, d), lambda i: (i, 0))
    return pl.pallas_call(
        _cross_out_router_kernel,
        grid=(m // tm,),
        in_specs=[
            row_spec,
            pl.BlockSpec((n_mem, d), lambda i: (i // per_b, 2 * layer)),
            pl.BlockSpec((n_mem, d), lambda i: (i // per_b, 2 * layer + 1)),
            pl.BlockSpec((None, d, d), lambda i: (layer, 0, 0)),
            row_spec,
            pl.BlockSpec((1, d), lambda i: (0, 0)),
            pl.BlockSpec((d, LANES), lambda i: (0, 0)),
        ],
        out_specs=[
            row_spec,
            row_spec,
            pl.BlockSpec((1, N_EXPERTS, tm), lambda i: (i // per_b, 0, i % per_b)),
        ],
        out_shape=[
            jax.ShapeDtypeStruct((m, d), F32),
            jax.ShapeDtypeStruct((m, d), BF16),
            jax.ShapeDtypeStruct((batch, N_EXPERTS, seq), F32),
        ],
        scratch_shapes=[pltpu.VMEM((tm, d), BF16)],
        compiler_params=_params("parallel"),
        name="cross_out_router",
    )(q, kv, kv, w_co, res, gain.reshape(1, d), w_pad)


SUBLANES = 8


def _sort_descending(x):
    n, lanes = x.shape
    k = 2
    while k <= n:
        j = k // 2
        while j >= 1:
            if j >= SUBLANES and k < n:
                v = x.reshape(n // (2 * k), 2, k // (2 * j), 2, j, lanes)
                a, b = v[:, :, :, 0], v[:, :, :, 1]
                mx, mn = jnp.maximum(a, b), jnp.minimum(a, b)
                lo = jnp.concatenate([mx[:, 0:1], mn[:, 1:2]], axis=1)
                hi = jnp.concatenate([mn[:, 0:1], mx[:, 1:2]], axis=1)
                x = jnp.stack([lo, hi], axis=3).reshape(n, lanes)
            elif j >= SUBLANES:
                v = x.reshape(n // (2 * j), 2, j, lanes)
                x = jnp.stack([jnp.maximum(v[:, 0], v[:, 1]), jnp.minimum(v[:, 0], v[:, 1])], axis=1)
                x = x.reshape(n, lanes)
            else:
                v = x.reshape(n // SUBLANES, SUBLANES, lanes)
                s = lax.broadcasted_iota(I32, v.shape, 0) * SUBLANES + lax.broadcasted_iota(I32, v.shape, 1)
                is_lo = (s & j) == 0
                partner = jnp.where(is_lo, pltpu.roll(v, SUBLANES - j, 1), pltpu.roll(v, j, 1))
                take_max = is_lo == ((s & k) == 0)
                x = jnp.where(take_max, jnp.maximum(v, partner), jnp.minimum(v, partner)).reshape(n, lanes)
            j //= 2
        k *= 2
    return x


def _topk_kernel(cap, n_e, aff_ref, slot_ref, wsel_ref, slot_t_ref):
    n_rows, n_tok = aff_ref.shape
    a = aff_ref[...]
    token_major = jnp.concatenate([a, jnp.zeros((LANES - n_rows, n_tok), F32)], axis=0).T
    kth = _sort_descending(token_major)[cap - 1:cap, :]
    thr = jnp.broadcast_to(kth, (SUBLANES, LANES)).T[:n_rows, 0:1]
    gt = a > thr
    eq = a == thr
    need = cap - jnp.sum(gt.astype(I32), axis=1, keepdims=True)
    upper = (lax.broadcasted_iota(I32, (n_tok, n_tok), 0) < lax.broadcasted_iota(I32, (n_tok, n_tok), 1))
    upper = upper.astype(BF16)
    eq_rank = jnp.dot(eq.astype(BF16), upper, preferred_element_type=F32)
    sel = jnp.logical_or(gt, jnp.logical_and(eq, eq_rank < need.astype(F32)))
    rank = jnp.dot(sel.astype(BF16), upper, preferred_element_type=F32)
    slot = jnp.where(sel, rank, -1.0)
    slot_ref[...] = slot.astype(I32)
    wsel_ref[...] = jnp.where(sel, a, 0.0)
    pad = jnp.full((LANES - n_e, n_tok), -1.0, F32)
    for b in range(n_rows // n_e):
        slot_t_ref[b] = jnp.concatenate([slot[b * n_e:(b + 1) * n_e], pad], axis=0).T.astype(I32)


def _topk_select(aff2d, cap, n_e):
    rows, n_tok = aff2d.shape
    spec = pl.BlockSpec((rows, n_tok), lambda i: (0, 0))
    spec_t = pl.BlockSpec((rows // n_e, n_tok, LANES), lambda i: (0, 0, 0))
    return pl.pallas_call(
        functools.partial(_topk_kernel, cap, n_e),
        grid=(1,),
        in_specs=[spec],
        out_specs=[spec, spec, spec_t],
        out_shape=[jax.ShapeDtypeStruct((rows, n_tok), I32), jax.ShapeDtypeStruct((rows, n_tok), F32),
                   jax.ShapeDtypeStruct((rows // n_e, n_tok, LANES), I32)],
        compiler_params=_params("arbitrary"),
        name="moe_topk_select",
    )(aff2d)


def _gather_kernel(cap, slot_ref, wsel_ref, hm_ref, xe_ref, gs_ref):
    n_tok = hm_ref.shape[0]
    onehot = slot_ref[0] == lax.broadcasted_iota(I32, (cap, n_tok), 0)
    xe_ref[0] = jnp.dot(onehot.astype(BF16), hm_ref[...], preferred_element_type=F32).astype(BF16)
    gs_ref[0] = jnp.sum(jnp.where(onehot, wsel_ref[0], 0.0), axis=1, keepdims=True)


def _gather_tokens(slot, wsel, hm, batch, seq, cap):
    rows = batch * N_EXPERTS
    row_spec = pl.BlockSpec((1, 1, seq), lambda b, e: (b * N_EXPERTS + e, 0, 0))
    return pl.pallas_call(
        functools.partial(_gather_kernel, cap),
        grid=(batch, N_EXPERTS),
        in_specs=[row_spec, row_spec, pl.BlockSpec((seq, D_MODEL), lambda b, e: (b, 0))],
        out_specs=[
            pl.BlockSpec((1, cap, D_MODEL), lambda b, e: (e, b, 0)),
            pl.BlockSpec((1, cap, 1), lambda b, e: (e, b, 0)),
        ],
        out_shape=[
            jax.ShapeDtypeStruct((N_EXPERTS, batch * cap, D_MODEL), BF16),
            jax.ShapeDtypeStruct((N_EXPERTS, batch * cap, 1), F32),
        ],
        compiler_params=_params("parallel", "arbitrary"),
        name="moe_gather",
    )(slot.reshape(rows, 1, seq), wsel.reshape(rows, 1, seq), hm)


def _expert_kernel(n_e, n_t, xe_ref, wg_ref, wu_ref, wd_ref, gs_ref, ye_ref, mid_ref):
    g = pl.program_id(0)
    s = pl.program_id(1)
    t = s // 2
    tf = wg_ref.shape[-1]

    @pl.when(jnp.logical_and(s % 2 == 0, g < n_e))
    def _():
        x = xe_ref[0]
        a = jnp.dot(x, wg_ref[...].astype(BF16), preferred_element_type=F32)
        u = jnp.dot(x, wu_ref[...].astype(BF16), preferred_element_type=F32)
        mid_ref[g % 2, t] = (a * jax.nn.sigmoid(a) * u).astype(BF16)

    @pl.when(jnp.logical_and(s % 2 == 1, g >= 1))
    def _():
        prev = (g - 1) % 2
        acc = jnp.dot(mid_ref[prev, 0], wd_ref[0:tf, :].astype(BF16), preferred_element_type=F32)
        for f in range(1, n_t):
            acc += jnp.dot(mid_ref[prev, f], wd_ref[f * tf:(f + 1) * tf, :].astype(BF16),
                           preferred_element_type=F32)
        ye_ref[0] = (acc * gs_ref[0]).astype(BF16)


def _expert_ffn(xe, gs, w_gate, w_up, w_down, layer, tf):
    n_e, rows, d = xe.shape
    ff = w_gate.shape[-1]
    n_t = ff // tf
    assert d // tf == n_t
    last = n_e - 1

    n_s = 2 * n_t

    def fill_tile(g, s, lead):
        v = jnp.minimum(g * n_s + s + lead, n_e * n_s - 1)
        return (layer, v // n_s, 0, (v % n_s) // 2)

    def emit_tile(g, s):
        v = jnp.maximum(g * n_s + s - 1 - n_s, 0)
        return (layer, v // n_s, 0, (v % n_s) // 2)

    return pl.pallas_call(
        functools.partial(_expert_kernel, n_e, n_t),
        grid=(n_e + 1, n_s),
        in_specs=[
            pl.BlockSpec((1, rows, d), lambda g, s: (jnp.minimum(g, last), 0, 0)),
            pl.BlockSpec((None, None, d, tf), functools.partial(fill_tile, lead=1)),
            pl.BlockSpec((None, None, d, tf), functools.partial(fill_tile, lead=0)),
            pl.BlockSpec((None, None, ff, tf), emit_tile),
            pl.BlockSpec((1, rows, 1), lambda g, s: (jnp.maximum(g - 1, 0), 0, 0)),
        ],
        out_specs=pl.BlockSpec((1, rows, tf), lambda g, s: emit_tile(g, s)[1:]),
        out_shape=jax.ShapeDtypeStruct((n_e, rows, d), BF16),
        scratch_shapes=[pltpu.VMEM((2, n_t, rows, tf), BF16)],
        compiler_params=_params("arbitrary", "arbitrary"),
        name="moe_expert_ffn",
    )(xe, w_gate, w_up, w_down, gs)


def _scatter_kernel(cap, slot_ref, ye_ref, x_ref, *rest):
    o_ref = rest[-1]
    n_e = ye_ref.shape[0]
    col = lax.broadcasted_iota(I32, (x_ref.shape[0], cap), 1)
    slots = slot_ref[0]
    onehot = jnp.concatenate([(slots[:, e:e + 1] == col).astype(BF16) for e in range(n_e)], axis=1)
    ye = ye_ref[...].reshape(n_e * cap, ye_ref.shape[-1])
    y = x_ref[...] + jnp.dot(onehot, ye, preferred_element_type=F32)
    o_ref[...] = _rms(y, rest[0][...]) if len(rest) == 2 else y


def _scatter_add(slot_t, ye, x2d, batch, seq, cap, tr, final_gain=None):
    d = x2d.shape[-1]
    per_b = seq // tr
    in_specs = [
        pl.BlockSpec((1, tr, LANES), lambda b, i: (b, i, 0)),
        pl.BlockSpec((N_EXPERTS, cap, d), lambda b, i: (0, b, 0)),
        pl.BlockSpec((tr, d), lambda b, i: (b * per_b + i, 0)),
    ]
    args = [slot_t, ye, x2d]
    if final_gain is not None:
        in_specs.append(pl.BlockSpec((1, d), lambda b, i: (0, 0)))
        args.append(final_gain.reshape(1, d))
    return pl.pallas_call(
        functools.partial(_scatter_kernel, cap),
        grid=(batch, per_b),
        in_specs=in_specs,
        out_specs=pl.BlockSpec((tr, d), lambda b, i: (b * per_b + i, 0)),
        out_shape=jax.ShapeDtypeStruct(x2d.shape, F32),
        compiler_params=_params("parallel", "arbitrary"),
        name="moe_scatter_add",
    )(*args)


def _expert_choice_ffn(x2d, hm, aff, w_gate, w_up, w_down, layer, batch, seq, final_gain=None):
    cap = EC_CAPACITY * seq // N_EXPERTS
    slot, wsel, slot_t = _topk_select(aff.reshape(batch * N_EXPERTS, seq), cap, N_EXPERTS)
    xe, gs = _gather_tokens(slot, wsel, hm, batch, seq, cap)
    ye = _expert_ffn(xe, gs, w_gate, w_up, w_down, layer, 512)
    return _scatter_add(slot_t, ye, x2d, batch, seq, cap, 256, final_gain)


def kernel(x, mem, norm_mix, w_in, na_rpb, hy_conv_w, hy_conv_b, hy_filt_w1, hy_filt_b1, hy_filt_w2, hy_filt_b2, hy_filt_w3, hy_sin_freq, hy_skip_d, branch_norm, w_out, norm_cross, mem_norm, w_cq, w_ckv, w_co, norm_moe, w_router, w_gate, w_up, w_down, final_norm):
    batch, seq, d = x.shape
    n_mem = mem.shape[1]
    depth = w_in.shape[0]
    m = batch * seq
    na_cols = 3 * NA_W
    hy_cols = 3 * HY_W
    rows = seq // GRID_W

    dft_fwd, dft_inv = _dft_tables(seq)
    x2d = x.reshape(m, d)
    mem2d = mem.reshape(batch * n_mem, d)

    w_out_b, w_cq_b, w_co_b = (w.astype(BF16) for w in (w_out, w_cq, w_co))
    p_in = w_in.shape[-1]
    kv = _norm_proj_layers(mem2d, mem_norm, w_ckv, 1024)

    for l in range(depth):
        assert p_in == 3 * na_cols
        slabs = [_norm_proj_resident(x2d, norm_mix[l], w_in, l, s, 3, 512, "in_proj").reshape(batch, seq, na_cols)
                 for s in range(3)]

        y_na = _neighbourhood_attention(slabs[0], _na_bias_table(na_rpb[l], rows), batch, seq)
        y_hy = _hyena_mixer(slabs[1], 0, hy_conv_w[l], hy_conv_b[l], hy_filt_w1[l],
                            hy_filt_b1[l], hy_filt_w2[l], hy_filt_b2[l], hy_filt_w3[l], hy_sin_freq[l],
                            hy_skip_d[l], dft_fwd, dft_inv, batch, seq)
        y_ret = _retention_mixer(slabs[1], hy_cols, slabs[2], 0, batch, seq)
        x2d, q = _branch_out_cross_q(y_na.reshape(m, NA_W), y_hy.reshape(m, HY_W), y_ret.reshape(m, RET_W),
                                     branch_norm[l], w_out_b, x2d, norm_cross[l], w_cq_b, l, 512)
        x2d, hm, aff = _cross_out_and_router(q, kv, w_co_b, l, x2d, norm_moe[l], w_router[l], batch, seq, n_mem,
                                             256)

        x2d = _expert_choice_ffn(x2d, hm, aff, w_gate, w_up, w_down, l, batch, seq,
                                 final_gain=final_norm if l == depth - 1 else None)

    return x2d.reshape(batch, seq, d)
```

```python
import functools
import math

import numpy as np
import jax
import jax.numpy as jnp
from jax import lax
from jax.experimental import pallas as pl
from jax.experimental.pallas import tpu as pltpu

F32 = jnp.float32
BF16 = jnp.bfloat16
I32 = jnp.int32

D_MODEL = 2048
GRID_W = 64
NA_HEAD_DIM = 64
NA_W = 768
NA_HEADS = 12
NA_KR = 8
NA_KC = 16
HY_W = 512
HY_ORDER = 2
HY_BANDS = 8
HY_POS_DIM = 17
HY_FILT_FF = 64
RET_HEAD_DIM = 128
RET_W = 768
RET_HEADS = 6
RET_CHUNK = 128
CROSS_HEADS = 4
CROSS_HEAD_DIM = 512
N_EXPERTS = 16
EXPERT_FF = 2048
EC_CAPACITY = 2
RMS_EPS = 1e-6
GN_EPS = 1e-5

MASK_VALUE = -1e30
LANES = 128
VMEM_LIMIT_BYTES = 56 * 1024 * 1024

NT_DIMS = (((1,), (1,)), ((), ()))


def _params(*sem):
    return pltpu.CompilerParams(dimension_semantics=sem, vmem_limit_bytes=VMEM_LIMIT_BYTES)


def _rms(xf, g):
    return xf * lax.rsqrt(jnp.mean(xf * xf, axis=-1, keepdims=True) + RMS_EPS) * g


def _norm_proj_layers_kernel(x_ref, g_ref, w_ref, o_ref, h_ref):
    @pl.when(pl.program_id(0) == 0)
    def _():
        h_ref[...] = _rms(x_ref[...], g_ref[...]).astype(BF16)

    o_ref[...] = jnp.dot(h_ref[...], w_ref[...].astype(BF16), preferred_element_type=F32).astype(o_ref.dtype)


def _norm_proj_layers(x2d, gain, w, tn):
    m, k = x2d.shape
    n_layers, _, n = w.shape
    per_layer = n // tn
    assert n % tn == 0
    return pl.pallas_call(
        _norm_proj_layers_kernel,
        grid=(n_layers * per_layer,),
        in_specs=[
            pl.BlockSpec((m, k), lambda j: (0, 0)),
            pl.BlockSpec((1, k), lambda j: (0, 0)),
            pl.BlockSpec((None, k, tn), lambda j: (j // per_layer, 0, j % per_layer)),
        ],
        out_specs=pl.BlockSpec((m, tn), lambda j: (0, j)),
        out_shape=jax.ShapeDtypeStruct((m, n_layers * n), BF16),
        scratch_shapes=[pltpu.VMEM((m, k), BF16)],
        compiler_params=_params("arbitrary"),
        name="cross_kv_proj",
    )(x2d, gain.reshape(1, k), w)


def _norm_proj_resident_kernel(x_ref, g_ref, w_ref, o_ref, wb_ref):
    @pl.when(pl.program_id(0) == 0)
    def _():
        wb_ref[...] = w_ref[...].astype(BF16)

    h = _rms(x_ref[...], g_ref[...]).astype(BF16)
    o_ref[...] = jnp.dot(h, wb_ref[...], preferred_element_type=F32).astype(o_ref.dtype)


def _norm_proj_resident(x2d, gain, w, layer, slab, n_slab, tm, name):
    m, k = x2d.shape
    width = w.shape[-1] // n_slab
    assert w.shape[-1] % n_slab == 0 and m % tm == 0
    return pl.pallas_call(
        _norm_proj_resident_kernel,
        grid=(m // tm,),
        in_specs=[
            pl.BlockSpec((tm, k), lambda i: (i, 0)),
            pl.BlockSpec((1, k), lambda i: (0, 0)),
            pl.BlockSpec((None, k, width), lambda i: (layer, 0, slab)),
        ],
        out_specs=pl.BlockSpec((tm, width), lambda i: (i, 0)),
        out_shape=jax.ShapeDtypeStruct((m, width), BF16),
        scratch_shapes=[pltpu.VMEM((k, width), BF16)],
        compiler_params=_params("arbitrary"),
        name=name,
    )(x2d, gain.reshape(1, k), w)


def _cross_prologue(h_ref, q_ref, k_ref, v_ref):
    dh = CROSS_HEAD_DIM
    for h in range(CROSS_HEADS):
        sl = slice(h * dh, (h + 1) * dh)
        s = lax.dot_general(q_ref[:, sl], k_ref[:, sl], NT_DIMS, preferred_element_type=F32) * (dh ** -0.5)
        m = jnp.max(s, axis=-1, keepdims=True)
        p = jnp.exp(s - m)
        l = jnp.sum(p, axis=-1, keepdims=True)
        o = jnp.dot(p.astype(BF16), v_ref[:, sl], preferred_element_type=F32) * (1.0 / l)
        h_ref[:, sl] = o.astype(BF16)


def _branch_prologue(h_ref, na_ref, hy_ref, ret_ref, g_ref):
    def nrm(p):
        return p * lax.rsqrt(jnp.mean(p * p, axis=-1, keepdims=True) + RMS_EPS)

    y = jnp.concatenate([nrm(r[...].astype(F32)) for r in (na_ref, hy_ref, ret_ref)], axis=-1)
    h_ref[...] = (y * g_ref[...]).astype(BF16)


def _branch_out_cross_q_kernel(na_ref, hy_ref, ret_ref, g_ref, wo_ref, res_ref, gq_ref, wq_ref, x_ref, q_ref,
                               h_ref):
    _branch_prologue(h_ref, na_ref, hy_ref, ret_ref, g_ref)
    x = res_ref[...] + jnp.dot(h_ref[...], wo_ref[...], preferred_element_type=F32)
    x_ref[...] = x
    hq = _rms(x, gq_ref[...]).astype(BF16)
    q_ref[...] = jnp.dot(hq, wq_ref[...], preferred_element_type=F32).astype(BF16)


def _branch_out_cross_q(y_na, y_hy, y_ret, gain, w_out, res, gain_q, w_cq, layer, tm):
    m, d = res.shape
    rows = lambda width: pl.BlockSpec((tm, width), lambda i: (i, 0))
    vec = pl.BlockSpec((1, d), lambda i: (0, 0))
    square = pl.BlockSpec((None, d, d), lambda i: (layer, 0, 0))
    return pl.pallas_call(
        _branch_out_cross_q_kernel,
        grid=(m // tm,),
        in_specs=[rows(NA_W), rows(HY_W), rows(RET_W), vec, square, rows(d), vec, square],
        out_specs=[rows(d), rows(d)],
        out_shape=[jax.ShapeDtypeStruct((m, d), F32), jax.ShapeDtypeStruct((m, d), BF16)],
        scratch_shapes=[pltpu.VMEM((tm, d), BF16)],
        compiler_params=_params("parallel"),
        name="branch_out_cross_q",
    )(y_na, y_hy, y_ret, gain.reshape(1, d), w_out, res, gain_q.reshape(1, d), w_cq)


NA_PAIR = 2
NA_BAND = NA_KR + NA_PAIR


def _na_band_base(r, rows):
    return np.clip(r - NA_KR // 2, 0, rows - NA_BAND)


@functools.lru_cache(maxsize=None)
def _na_variants(rows):
    assert rows % NA_PAIR == 0 and rows >= NA_BAND + 2
    n_var = NA_KR // 2 + 1
    dr = np.full((n_var, NA_BAND, NA_PAIR), -2, np.int64)
    for r in range(0, rows, NA_PAIR):
        base = int(_na_band_base(r, rows))
        v = (r - base) // 2
        for j in range(NA_PAIR):
            rs = int(np.clip(r + j - NA_KR // 2, 0, rows - NA_KR))
            for i in range(NA_BAND):
                val = base + i - (r + j) + (NA_KR - 1) if rs <= base + i < rs + NA_KR else -1
                assert dr[v, i, j] in (-2, val)
                dr[v, i, j] = val
    assert (dr > -2).all()
    return dr


def _na_bias_table(rpb, rows):
    c = np.arange(GRID_W)
    col_start = np.clip(c - NA_KC // 2, 0, GRID_W - NA_KC)
    col_in = (c[None, :] >= col_start[:, None]) & (c[None, :] < col_start[:, None] + NA_KC)
    dc = np.clip(c[None, :] - c[:, None] + (NA_KC - 1), 0, 2 * NA_KC - 2)
    onehot = (dc.T.reshape(-1)[None, :] == np.arange(2 * NA_KC - 1)[:, None]).astype(np.float32)
    cols = jnp.einsum("hab,bn->han", rpb.astype(F32), jnp.asarray(onehot), precision=lax.Precision.HIGHEST)
    cols = cols.reshape(NA_HEADS, 2 * NA_KR - 1, GRID_W, GRID_W)
    cols = jnp.where(col_in.T[None, None], cols, MASK_VALUE)
    dr = _na_variants(rows)
    n_var = dr.shape[0]
    return pl.pallas_call(
        functools.partial(_na_table_kernel, dr),
        grid=(NA_HEADS,),
        in_specs=[pl.BlockSpec((1, 2 * NA_KR - 1, GRID_W, GRID_W), lambda h: (h, 0, 0, 0))],
        out_specs=pl.BlockSpec((1, n_var, NA_BAND * GRID_W, NA_PAIR * GRID_W), lambda h: (h, 0, 0, 0)),
        out_shape=jax.ShapeDtypeStruct((NA_HEADS, n_var, NA_BAND * GRID_W, NA_PAIR * GRID_W), F32),
        compiler_params=_params("parallel"),
        name="na_bias_table",
    )(cols)


def _na_table_kernel(dr, cols_ref, o_ref):
    masked = jnp.full((GRID_W, GRID_W), MASK_VALUE, F32)
    for v in range(dr.shape[0]):
        for i in range(NA_BAND):
            blocks = [cols_ref[0, int(dr[v, i, j])] if dr[v, i, j] >= 0 else masked for j in range(NA_PAIR)]
            o_ref[0, v, i * GRID_W:(i + 1) * GRID_W, :] = jnp.concatenate(blocks, axis=-1)


def _na_kernel(rows, q_ref, k_ref, v_ref, bias_ref, o_ref, vt_ref, s0_ref, s1_ref, p0_ref, p1_ref, l0_ref,
               l1_ref):
    dh = NA_HEAD_DIM
    nq = NA_PAIR * GRID_W
    nk = NA_BAND * GRID_W
    n_chunk = nk // LANES
    lane = lax.broadcasted_iota(I32, (nq, 2 * dh), 1)

    for ch in range(vt_ref.shape[0]):
        vt_ref[ch] = v_ref[0, ch * LANES:(ch + 1) * LANES, :].astype(F32).T.astype(BF16)

    n_steps = rows // NA_PAIR

    def band_base(p):
        return jnp.clip(p * NA_PAIR - NA_KR // 2, 0, rows - NA_BAND)


    def scores(p, s_out):
        r = p * NA_PAIR
        base = band_base(p)
        variant = (r - base) // 2
        q = q_ref[0, pl.ds(pl.multiple_of(r * GRID_W, nq), nq), :] * (dh ** -0.5)
        kb = k_ref[0, pl.ds(pl.multiple_of(base * GRID_W, LANES), nk), :]
        zero = jnp.zeros_like(q)
        q2 = jnp.concatenate([jnp.where(lane < dh, q, zero), jnp.where(lane >= dh, q, zero)], axis=0)
        st2 = lax.dot_general(kb, q2, NT_DIMS, preferred_element_type=F32)
        for hh in range(2):
            s_out[:, hh * nq:(hh + 1) * nq] = st2[:, hh * nq:(hh + 1) * nq] + bias_ref[hh, variant]

    def softmax(s_in, p_out, l_out):
        st = s_in[...]
        pt = jnp.exp(st - jnp.max(st, axis=0, keepdims=True))
        p_out[...] = pt.astype(BF16)
        l_out[...] = 1.0 / jnp.sum(pt, axis=0, keepdims=True)

    def values(p, p_in, l_in):
        r = p * NA_PAIR
        c0 = band_base(p) // 2
        vt = jnp.concatenate([vt_ref[c0 + i] for i in range(n_chunk)], axis=1)
        pt = p_in[...]
        linv = l_in[...]
        outs = []
        for hh in range(2):
            cols = slice(hh * nq, (hh + 1) * nq)
            ot = jnp.dot(vt[hh * dh:(hh + 1) * dh, :], pt[:, cols], preferred_element_type=F32)
            outs.append(ot * linv[:, cols])
        o_ref[0, pl.ds(pl.multiple_of(r * GRID_W, nq), nq), :] = jnp.concatenate(outs, axis=0).T.astype(BF16)

    s_slots = (s0_ref, s1_ref)
    p_slots = (p0_ref, p1_ref)
    l_slots = (l0_ref, l1_ref)

    def step(t, parity, do_scores=True, do_softmax=True, do_values=True):
        a, b = parity, 1 - parity
        t = jnp.asarray(t, I32)
        if do_scores:
            scores(t, s_slots[a])
        if do_softmax:
            softmax(s_slots[b], p_slots[b], l_slots[b])
        if do_values:
            values(t - 2, p_slots[a], l_slots[a])

    assert n_steps % 2 == 0 and n_steps >= 4
    step(0, 0, do_softmax=False, do_values=False)
    step(1, 1, do_values=False)

    def steady(i, carry):
        t = 2 + 2 * i
        step(t, 0)
        step(t + 1, 1)
        return carry

    lax.fori_loop(0, (n_steps - 2) // 2, steady, 0, unroll=True)
    step(n_steps, 0, do_scores=False)
    step(n_steps + 1, 1, do_scores=False, do_softmax=False)


def _neighbourhood_attention(proj_na, bias_tbl, batch, seq):
    rows = seq // GRID_W
    n_pairs = NA_HEADS // 2
    blk = (1, seq, 2 * NA_HEAD_DIM)
    n_var, nk, nq = bias_tbl.shape[1:]
    return pl.pallas_call(
        functools.partial(_na_kernel, rows),
        grid=(batch, n_pairs),
        in_specs=[
            pl.BlockSpec(blk, lambda b, h: (b, 0, h)),
            pl.BlockSpec(blk, lambda b, h: (b, 0, n_pairs + h)),
            pl.BlockSpec(blk, lambda b, h: (b, 0, 2 * n_pairs + h)),
            pl.BlockSpec((2, n_var, nk, nq), lambda b, h: (h, 0, 0, 0)),
        ],
        out_specs=pl.BlockSpec(blk, lambda b, h: (b, 0, h)),
        out_shape=jax.ShapeDtypeStruct((batch, seq, NA_W), BF16),
        scratch_shapes=[pltpu.VMEM((seq // LANES, 2 * NA_HEAD_DIM, LANES), BF16),
                        pltpu.VMEM((nk, 2 * nq), F32), pltpu.VMEM((nk, 2 * nq), F32),
                        pltpu.VMEM((nk, 2 * nq), BF16), pltpu.VMEM((nk, 2 * nq), BF16),
                        pltpu.VMEM((1, 2 * nq), F32), pltpu.VMEM((1, 2 * nq), F32)],
        compiler_params=_params("parallel", "arbitrary"),
        name="neighbourhood_attention",
    )(proj_na, proj_na, proj_na, bias_tbl)


@functools.lru_cache(maxsize=None)
def _dft_factors(seq):
    n = 2 * seq
    t = np.arange(seq, dtype=np.int64)
    f1 = np.arange(seq // 64, dtype=np.int64)
    f0 = np.arange(64, dtype=np.int64)
    a = 2.0 * np.pi * ((64 * f1[:, None] * t[None, :]) % n).astype(np.float64) / n
    b = 2.0 * np.pi * ((f0[:, None] * t[None, :]) % n).astype(np.float64) / n
    ny = np.where(t % 2 == 0, 1.0, -1.0)
    return tuple(np.asarray(v, np.float32) for v in (np.cos(a), np.sin(a), np.cos(b), np.sin(b), ny))


def _dft_table_kernel(u_ref, v_ref, cb_ref, sb_ref, fwd_ref, inv_ref):
    sin_half = pl.program_id(0) == 1
    first = pl.program_id(1) == 0
    n_fine, seq = cb_ref.shape
    row = lax.broadcasted_iota(I32, (n_fine, seq), 0)
    col = lax.broadcasted_iota(I32, (n_fine, seq), 1)
    for c in range(u_ref.shape[1]):
        tile = u_ref[0, c:c + 1, :] * cb_ref[...] + v_ref[0, c:c + 1, :] * sb_ref[...]
        rows = slice(c * n_fine, (c + 1) * n_fine)
        fwd = tile
        if c == 0:
            fwd_nyq = jnp.logical_and(jnp.logical_and(sin_half, first), row == 0)
            fwd = jnp.where(fwd_nyq, (1 - 2 * (col & 1)).astype(F32), tile)
        fwd_ref[0, rows, :] = fwd.astype(BF16)
        inv_nyq = jnp.logical_and(sin_half, col == 0)
        inv_ref[rows, :] = jnp.where(inv_nyq, (1 - 2 * (row & 1)).astype(F32), tile).astype(BF16)


def _dft_tables(seq):
    ca, sa, cb, sb, _ = _dft_factors(seq)
    u = np.stack([ca, sa])
    v = np.stack([-sa, ca])
    n_coarse = 8
    rows = 64 * n_coarse
    coarse_spec = pl.BlockSpec((1, n_coarse, seq), lambda h, i: (h, i, 0))
    fine_spec = pl.BlockSpec((64, seq), lambda h, i: (0, 0))
    return pl.pallas_call(
        _dft_table_kernel,
        grid=(2, seq // rows),
        in_specs=[coarse_spec, coarse_spec, fine_spec, fine_spec],
        out_specs=[pl.BlockSpec((1, rows, seq), lambda h, i: (h, i, 0)),
                   pl.BlockSpec((rows, seq), lambda h, i: (i, h))],
        out_shape=[jax.ShapeDtypeStruct((2, seq, seq), BF16), jax.ShapeDtypeStruct((seq, 2 * seq), BF16)],
        compiler_params=_params("parallel", "arbitrary"),
        name="dft_tables",
    )(jnp.asarray(u), jnp.asarray(v), jnp.asarray(cb), jnp.asarray(sb))


@functools.lru_cache(maxsize=None)
def _hyena_consts(seq):
    t = np.arange(seq, dtype=np.float64)
    t01 = t / (seq - 1)
    bands = np.linspace(1e-4, HY_BANDS - 1, HY_BANDS)
    ang = (2.0 * math.pi) * (t[:, None] / seq) * bands[None, :]
    feats = np.concatenate([t01[:, None], np.cos(ang), -np.sin(ang)], axis=-1)
    feats_p = np.zeros((seq, LANES), np.float32)
    feats_p[:, :HY_POS_DIM] = feats
    min_decay = math.log(1e-2) / 1.5
    max_decay = math.log(1e-2) / 0.3
    deltas = np.abs(np.linspace(min_decay, max_decay, HY_W))
    window = np.exp(-t01[:, None] * deltas[None, :]).astype(np.float32)
    return feats_p, window


def _filter_kernel(feats_ref, w1_ref, b1_ref, w2_ref, b2_ref, freq_ref, w3f_ref, w3b_ref, win_ref, sum_ref,
                   diff_ref, hid_ref):
    hp = lax.Precision.HIGHEST

    @pl.when(pl.program_id(0) == 0)
    def _():
        f = freq_ref[...]
        h1 = jnp.sin(f * (jnp.dot(feats_ref[...], w1_ref[...], precision=hp, preferred_element_type=F32)
                          + b1_ref[...]))
        hid_ref[...] = jnp.sin(f * (jnp.dot(h1, w2_ref[...], precision=hp, preferred_element_type=F32)
                                    + b2_ref[...]))

    hid = hid_ref[...]
    win = win_ref[...]
    fwd = jnp.dot(hid, w3f_ref[...], precision=hp, preferred_element_type=F32) * win
    bwd = jnp.dot(hid, w3b_ref[...], precision=hp, preferred_element_type=F32) * win
    bwd = jnp.where(lax.broadcasted_iota(I32, bwd.shape, 0) == 0, 0.0, bwd)
    sum_ref[...] = (fwd + bwd).astype(BF16)
    diff_ref[...] = (fwd - bwd).astype(BF16)


def _hyena_filters_time(w1, b1, w2, b2, w3, freq, seq):
    feats, window = _hyena_consts(seq)
    w1p = jnp.zeros((LANES, HY_FILT_FF), F32).at[:HY_POS_DIM].set(w1)
    full = lambda shape: pl.BlockSpec(shape, lambda o: (0,) * len(shape))
    out_spec = pl.BlockSpec((seq, HY_W), lambda o: (0, o))
    out_shape = jax.ShapeDtypeStruct((seq, HY_ORDER * HY_W), BF16)
    return pl.pallas_call(
        _filter_kernel,
        grid=(HY_ORDER,),
        in_specs=[
            full((seq, LANES)), full((LANES, HY_FILT_FF)), full((1, HY_FILT_FF)),
            full((HY_FILT_FF, HY_FILT_FF)), full((1, HY_FILT_FF)), full((1, HY_FILT_FF)),
            pl.BlockSpec((HY_FILT_FF, HY_W), lambda o: (0, 2 * o)),
            pl.BlockSpec((HY_FILT_FF, HY_W), lambda o: (0, 2 * o + 1)),
            full((seq, HY_W)),
        ],
        out_specs=[out_spec, out_spec],
        out_shape=[out_shape, out_shape],
        scratch_shapes=[pltpu.VMEM((seq, HY_FILT_FF), F32)],
        compiler_params=_params("arbitrary"),
        name="hyena_filter_mlp",
    )(jnp.asarray(feats), w1p, b1.reshape(1, -1), w2, b2.reshape(1, -1), freq.reshape(1, -1), w3, w3,
      jnp.asarray(window))


def _filter_dft_kernel(f_ref, sum_ref, diff_ref, o_ref):
    o_ref[0] = jnp.dot(f_ref[0], sum_ref[...], preferred_element_type=F32)
    o_ref[1] = jnp.dot(f_ref[1], diff_ref[...], preferred_element_type=F32)

    @pl.when(pl.program_id(0) == 0)
    def _():
        top = jnp.dot(f_ref[1, 0:8, :], sum_ref[...], preferred_element_type=F32)
        row = lax.broadcasted_iota(I32, top.shape, 0)
        o_ref[1, 0:8, :] = jnp.where(row == 0, top, o_ref[1, 0:8, :])


def _filter_dft(dft_fwd, filt_sum, filt_diff, seq, fb):
    return pl.pallas_call(
        _filter_dft_kernel,
        grid=(seq // fb, HY_ORDER),
        in_specs=[
            pl.BlockSpec((2, fb, seq), lambda f, o: (0, f, 0)),
            pl.BlockSpec((seq, HY_W), lambda f, o: (0, o)),
            pl.BlockSpec((seq, HY_W), lambda f, o: (0, o)),
        ],
        out_specs=pl.BlockSpec((2, fb, HY_W), lambda f, o: (0, f, o)),
        out_shape=jax.ShapeDtypeStruct((2, seq, HY_ORDER * HY_W), F32),
        compiler_params=_params("parallel", "arbitrary"),
        name="hyena_filter_dft",
    )(dft_fwd, filt_sum, filt_diff)


def _short_conv_kernel(p_ref, w_ref, b_ref, o_ref):
    p = p_ref[0].astype(F32)
    seq = p.shape[0]
    row = lax.broadcasted_iota(I32, p.shape, 0)
    prev = jnp.where(row == 0, 0.0, pltpu.roll(p, 1, 0))
    nxt = jnp.where(row == seq - 1, 0.0, pltpu.roll(p, seq - 1, 0))
    w = w_ref[...]
    o_ref[0] = (prev * w[0:1] + p * w[1:2] + nxt * w[2:3] + b_ref[...]).astype(o_ref.dtype)


def _short_conv(proj, col0, conv_w, conv_b, batch, seq, tc):
    n_cols = conv_w.shape[-1]
    assert col0 % tc == 0 and n_cols % tc == 0
    cb0 = col0 // tc
    return pl.pallas_call(
        _short_conv_kernel,
        grid=(batch, n_cols // tc),
        in_specs=[
            pl.BlockSpec((1, seq, tc), lambda b, c: (b, 0, cb0 + c)),
            pl.BlockSpec((3, tc), lambda b, c: (0, c)),
            pl.BlockSpec((1, tc), lambda b, c: (0, c)),
        ],
        out_specs=pl.BlockSpec((1, seq, tc), lambda b, c: (b, 0, c)),
        out_shape=jax.ShapeDtypeStruct((batch, seq, n_cols), BF16),
        compiler_params=_params("parallel", "arbitrary"),
        name="hyena_short_conv",
    )(proj, conv_w, conv_b.reshape(1, -1))


def _spectrum_kernel(n_fft, f_ref, z_ref, k_ref, o_ref):
    z = z_ref[0].astype(BF16)
    xr = jnp.dot(f_ref[0], z, preferred_element_type=F32)
    xs = jnp.dot(f_ref[1], z, preferred_element_type=F32)
    kr = k_ref[0]
    ks = k_ref[1]
    row = lax.broadcasted_iota(I32, xr.shape, 0)
    edge = jnp.logical_and(row == 0, pl.program_id(0) == 0)
    yr = jnp.where(edge, xr * kr * (1.0 / n_fft), (xr * kr - xs * ks) * (2.0 / n_fft))
    ys = jnp.where(edge, xs * ks * (1.0 / n_fft), (xr * ks + xs * kr) * (2.0 / n_fft))
    o_ref[0, 0] = yr.astype(BF16)
    o_ref[0, 1] = ys.astype(BF16)


def _spectrum_product(dft_fwd, z_arr, z_col, kfreq, order, batch, seq, fb):
    return pl.pallas_call(
        functools.partial(_spectrum_kernel, 2 * seq),
        grid=(seq // fb, batch),
        in_specs=[
            pl.BlockSpec((2, fb, seq), lambda f, b: (0, f, 0)),
            pl.BlockSpec((1, seq, HY_W), lambda f, b: (b, 0, z_col)),
            pl.BlockSpec((2, fb, HY_W), lambda f, b: (0, f, order)),
        ],
        out_specs=pl.BlockSpec((1, 2, fb, HY_W), lambda f, b: (b, 0, f, 0)),
        out_shape=jax.ShapeDtypeStruct((batch, 2, seq, HY_W), BF16),
        compiler_params=_params("parallel", "arbitrary"),
        name="hyena_spectrum",
    )(dft_fwd, z_arr, kfreq)


def _inverse_kernel(ft_ref, y_ref, gate_ref, z_ref, d_ref, o_ref):
    conv = jnp.dot(ft_ref[...], y_ref[0], preferred_element_type=F32)
    o_ref[0] = (gate_ref[0].astype(F32) * (conv + d_ref[...] * z_ref[0].astype(F32))).astype(o_ref.dtype)


def _inverse_gate(dft_inv, y, gate_arr, gate_col, z_arr, z_col, skip_row, batch, seq, tb):
    y2 = y.reshape(batch, 2 * seq, HY_W)
    return pl.pallas_call(
        _inverse_kernel,
        grid=(seq // tb, batch),
        in_specs=[
            pl.BlockSpec((tb, 2 * seq), lambda t, b: (t, 0)),
            pl.BlockSpec((1, 2 * seq, HY_W), lambda t, b: (b, 0, 0)),
            pl.BlockSpec((1, tb, HY_W), lambda t, b: (b, t, gate_col)),
            pl.BlockSpec((1, tb, HY_W), lambda t, b: (b, t, z_col)),
            pl.BlockSpec((1, HY_W), lambda t, b: (0, 0)),
        ],
        out_specs=pl.BlockSpec((1, tb, HY_W), lambda t, b: (b, t, 0)),
        out_shape=jax.ShapeDtypeStruct((batch, seq, HY_W), BF16),
        compiler_params=_params("parallel", "arbitrary"),
        name="hyena_inverse_gate",
    )(dft_inv, y2, gate_arr, z_arr, skip_row.reshape(1, HY_W))


def _hyena_mixer(proj, col0, conv_w, conv_b, w1, b1, w2, b2, w3, freq, skip_d, dft_fwd, dft_inv, batch, seq):
    s = _short_conv(proj, col0, conv_w, conv_b, batch, seq, 768)
    filt_sum, filt_diff = _hyena_filters_time(w1, b1, w2, b2, w3, freq, seq)
    kfreq = _filter_dft(dft_fwd, filt_sum, filt_diff, seq, 512)
    z_arr, z_col = s, 2
    for o in range(HY_ORDER):
        y = _spectrum_product(dft_fwd, z_arr, z_col, kfreq, o, batch, seq, 1024)
        z_arr = _inverse_gate(dft_inv, y, s, o, z_arr, z_col, skip_d[o], batch, seq, 1024)
        z_col = 0
    return z_arr


@functools.lru_cache(maxsize=None)
def _retention_consts(seq):
    c = RET_CHUNK
    half = RET_HEAD_DIM // 2
    inv = 1.0 / (10000.0 ** np.linspace(0.0, 1.0, half))
    ang = np.arange(seq, dtype=np.float64)[:, None] * inv[None, :]
    cos2 = np.concatenate([np.cos(ang), np.cos(ang)], axis=-1).astype(np.float32)
    sin2 = np.concatenate([-np.sin(ang), np.sin(ang)], axis=-1).astype(np.float32)
    hidx = np.arange(RET_HEADS, dtype=np.float64)
    lg_f = np.log1p(-np.exp2(-5.0 - hidx))[:, None, None]
    lg_b = np.log1p(-np.exp2(-5.5 - hidx))[:, None, None]
    i = np.arange(c, dtype=np.float64)
    diff = i[:, None] - i[None, :]
    ones = np.ones((1, c, c))
    dec = np.where(diff >= 0, np.exp(lg_f * np.maximum(diff, 0.0)), np.exp(lg_b * np.maximum(-diff, 0.0)))
    rowv = lambda v: v[:, :, None] * ones
    tab = np.stack([
        dec,
        rowv(np.exp(lg_f[:, :, 0] * (i + 1.0)[None, :])),
        rowv(np.exp(lg_f[:, :, 0] * (c - 1.0 - i)[None, :])),
        rowv(np.exp(lg_b[:, :, 0] * (c - i)[None, :])),
        rowv(np.exp(lg_b[:, :, 0] * i[None, :])),
        np.exp(lg_f * c) * ones,
        np.exp(lg_b * c) * ones,
    ], axis=1).astype(np.float32)
    return cos2, sin2, tab


def _retention_kernel(q_ref, k_ref, v_ref, g_ref, cos_ref, sin_ref, tab_ref, o_ref, qs_ref, ks_ref, kvf_ref,
                      kvb_ref, a0_ref, a1_ref, y0_ref, y1_ref):
    c = RET_CHUNK
    d = RET_HEAD_DIM
    seq = q_ref.shape[1]
    n_chunks = seq // c
    cos = cos_ref[...]
    sin = sin_ref[...]
    q = q_ref[0].astype(F32)
    k = k_ref[0].astype(F32)
    qs_ref[...] = (q * cos + pltpu.roll(q, d // 2, 1) * sin) * (d ** -0.5)
    ks_ref[...] = k * cos + pltpu.roll(k, d // 2, 1) * sin

    def mm(a, b):
        return jnp.dot(a.astype(BF16), b.astype(BF16), preferred_element_type=F32)

    def chunk(n):
        return pl.ds(pl.multiple_of(n * c, c), c)

    def kv_body(n, carry):
        kc = ks_ref[chunk(n), :]
        vc = v_ref[0, chunk(n), :]
        kvf_ref[n] = mm((kc * tab_ref[0, 2]).T, vc)
        kvb_ref[n] = mm((kc * tab_ref[0, 4]).T, vc)
        return carry

    lax.fori_loop(0, n_chunks, kv_body, 0, unroll=True)

    def scan_fwd(n, state):
        kv = kvf_ref[n]
        kvf_ref[n] = state
        return tab_ref[0, 5] * state + kv

    lax.fori_loop(0, n_chunks, scan_fwd, jnp.zeros((d, d), F32))

    def scan_bwd(m, state):
        n = n_chunks - 1 - m
        kv = kvb_ref[n]
        kvb_ref[n] = state
        return tab_ref[0, 6] * state + kv

    lax.fori_loop(0, n_chunks, scan_bwd, jnp.zeros((d, d), F32))

    def in_chunk(n, a_out):
        qc = qs_ref[chunk(n), :].astype(BF16)
        kc = ks_ref[chunk(n), :].astype(BF16)
        a = lax.dot_general(qc, kc, NT_DIMS, preferred_element_type=F32) * tab_ref[0, 0]
        a_out[...] = a.astype(BF16)

    def mix(n, a_in, y_out):
        qc = qs_ref[chunk(n), :]
        y_out[...] = (jnp.dot(a_in[...], v_ref[0, chunk(n), :], preferred_element_type=F32)
                      + mm(qc * tab_ref[0, 1], kvf_ref[n]) + mm(qc * tab_ref[0, 3], kvb_ref[n]))

    def finish(n, y_in):
        y = y_in[...]
        mu = jnp.mean(y, axis=-1, keepdims=True)
        yc = y - mu
        var = jnp.mean(yc * yc, axis=-1, keepdims=True)
        g = g_ref[0, chunk(n), :].astype(F32)
        o_ref[0, chunk(n), :] = (yc * lax.rsqrt(var + GN_EPS) * (g * jax.nn.sigmoid(g))).astype(BF16)

    a_slots = (a0_ref, a1_ref)
    y_slots = (y0_ref, y1_ref)

    def step(t, parity, do_a=True, do_mix=True, do_finish=True):
        t = jnp.asarray(t, I32)
        if do_a:
            in_chunk(t, a_slots[parity])
        if do_mix:
            mix(t - 1, a_slots[1 - parity], y_slots[1 - parity])
        if do_finish:
            finish(t - 2, y_slots[parity])

    assert n_chunks % 2 == 0 and n_chunks >= 4
    step(0, 0, do_mix=False, do_finish=False)
    step(1, 1, do_finish=False)

    def steady(i, carry):
        step(2 + 2 * i, 0)
        step(3 + 2 * i, 1)
        return carry

    lax.fori_loop(0, (n_chunks - 2) // 2, steady, 0, unroll=True)
    step(n_chunks, 0, do_a=False)
    step(n_chunks + 1, 1, do_a=False, do_mix=False)


def _retention_mixer(proj_q, col_q, proj, col0, batch, seq):
    cos2, sin2, tab = _retention_consts(seq)
    blk = (1, seq, RET_HEAD_DIM)
    h_ = RET_HEADS
    assert col0 % RET_HEAD_DIM == 0 and col_q % RET_HEAD_DIM == 0
    cq = col_q // RET_HEAD_DIM
    c0 = col0 // RET_HEAD_DIM - h_
    return pl.pallas_call(
        _retention_kernel,
        grid=(batch, RET_HEADS),
        in_specs=[
            pl.BlockSpec(blk, lambda b, h: (b, 0, cq + h)),
            pl.BlockSpec(blk, lambda b, h: (b, 0, c0 + h_ + h)),
            pl.BlockSpec(blk, lambda b, h: (b, 0, c0 + 2 * h_ + h)),
            pl.BlockSpec(blk, lambda b, h: (b, 0, c0 + 3 * h_ + h)),
            pl.BlockSpec((seq, RET_HEAD_DIM), lambda b, h: (0, 0)),
            pl.BlockSpec((seq, RET_HEAD_DIM), lambda b, h: (0, 0)),
            pl.BlockSpec((1, 7, RET_CHUNK, RET_CHUNK), lambda b, h: (h, 0, 0, 0)),
        ],
        out_specs=pl.BlockSpec(blk, lambda b, h: (b, 0, h)),
        out_shape=jax.ShapeDtypeStruct((batch, seq, RET_W), BF16),
        scratch_shapes=[pltpu.VMEM((seq, RET_HEAD_DIM), F32), pltpu.VMEM((seq, RET_HEAD_DIM), F32),
                        pltpu.VMEM((seq // RET_CHUNK, RET_HEAD_DIM, RET_HEAD_DIM), F32),
                        pltpu.VMEM((seq // RET_CHUNK, RET_HEAD_DIM, RET_HEAD_DIM), F32),
                        pltpu.VMEM((RET_CHUNK, RET_CHUNK), BF16), pltpu.VMEM((RET_CHUNK, RET_CHUNK), BF16),
                        pltpu.VMEM((RET_CHUNK, RET_HEAD_DIM), F32), pltpu.VMEM((RET_CHUNK, RET_HEAD_DIM), F32)],
        compiler_params=_params("parallel", "arbitrary"),
        name="retention",
    )(proj_q, proj, proj, proj, jnp.asarray(cos2), jnp.asarray(sin2), jnp.asarray(tab))


def _cross_out_router_kernel(q_ref, k_ref, v_ref, w_ref, res_ref, g_ref, wr_ref, x_ref, hm_ref, aff_ref, h_ref):
    n_e = aff_ref.shape[1]
    _cross_prologue(h_ref, q_ref, k_ref, v_ref)
    x = res_ref[...] + jnp.dot(h_ref[...], w_ref[...], preferred_element_type=F32)
    x_ref[...] = x
    h = _rms(x, g_ref[...])
    h_hi = h.astype(BF16)
    hm_ref[...] = h_hi
    h_lo = (h - h_hi.astype(F32)).astype(BF16)
    wr = wr_ref[...]
    w_hi = wr.astype(BF16)
    w_lo = (wr - w_hi.astype(F32)).astype(BF16)
    both = jnp.dot(h_hi, jnp.concatenate([w_hi, w_lo], axis=1), preferred_element_type=F32)
    logits = (both[:, :LANES] + both[:, LANES:] + jnp.dot(h_lo, w_hi, preferred_element_type=F32)).T[:n_e]
    m = jnp.max(logits, axis=0, keepdims=True)
    e = jnp.exp(logits - m)
    aff_ref[0] = e / jnp.sum(e, axis=0, keepdims=True)


def _cross_out_and_router(q, kv, w_co, layer, res, gain, w_router, batch, seq, n_mem, tm):
    m, d = q.shape
    per_b = seq // tm
    assert seq % tm == 0
    w_pad = jnp.zeros((d, LANES), F32).at[:, :N_EXPERTS].set(w_router)
    row_spec = pl.BlockSpec((tm, d), lambda i: (i, 0))
    return pl.pallas_call(
        _cross_out_router_kernel,
        grid=(m // tm,),
        in_specs=[
            row_spec,
            pl.BlockSpec((n_mem, d), lambda i: (i // per_b, 2 * layer)),
            pl.BlockSpec((n_mem, d), lambda i: (i // per_b, 2 * layer + 1)),
            pl.BlockSpec((None, d, d), lambda i: (layer, 0, 0)),
            row_spec,
            pl.BlockSpec((1, d), lambda i: (0, 0)),
            pl.BlockSpec((d, LANES), lambda i: (0, 0)),
        ],
        out_specs=[
            row_spec,
            row_spec,
            pl.BlockSpec((1, N_EXPERTS, tm), lambda i: (i // per_b, 0, i % per_b)),
        ],
        out_shape=[
            jax.ShapeDtypeStruct((m, d), F32),
            jax.ShapeDtypeStruct((m, d), BF16),
            jax.ShapeDtypeStruct((batch, N_EXPERTS, seq), F32),
        ],
        scratch_shapes=[pltpu.VMEM((tm, d), BF16)],
        compiler_params=_params("parallel"),
        name="cross_out_router",
    )(q, kv, kv, w_co, res, gain.reshape(1, d), w_pad)


SUBLANES = 8


def _sort_descending(x):
    n, lanes = x.shape
    k = 2
    while k <= n:
        j = k // 2
        while j >= 1:
            if j >= SUBLANES and k < n:
                v = x.reshape(n // (2 * k), 2, k // (2 * j), 2, j, lanes)
                a, b = v[:, :, :, 0], v[:, :, :, 1]
                mx, mn = jnp.maximum(a, b), jnp.minimum(a, b)
                lo = jnp.concatenate([mx[:, 0:1], mn[:, 1:2]], axis=1)
                hi = jnp.concatenate([mn[:, 0:1], mx[:, 1:2]], axis=1)
                x = jnp.stack([lo, hi], axis=3).reshape(n, lanes)
            elif j >= SUBLANES:
                v = x.reshape(n // (2 * j), 2, j, lanes)
                x = jnp.stack([jnp.maximum(v[:, 0], v[:, 1]), jnp.minimum(v[:, 0], v[:, 1])], axis=1)
                x = x.reshape(n, lanes)
            else:
                v = x.reshape(n // SUBLANES, SUBLANES, lanes)
                s = lax.broadcasted_iota(I32, v.shape, 0) * SUBLANES + lax.broadcasted_iota(I32, v.shape, 1)
                is_lo = (s & j) == 0
                partner = jnp.where(is_lo, pltpu.roll(v, SUBLANES - j, 1), pltpu.roll(v, j, 1))
                take_max = is_lo == ((s & k) == 0)
                x = jnp.where(take_max, jnp.maximum(v, partner), jnp.minimum(v, partner)).reshape(n, lanes)
            j //= 2
        k *= 2
    return x


def _topk_kernel(cap, n_e, aff_ref, slot_ref, wsel_ref, slot_t_ref):
    n_rows, n_tok = aff_ref.shape
    a = aff_ref[...]
    token_major = jnp.concatenate([a, jnp.zeros((LANES - n_rows, n_tok), F32)], axis=0).T
    kth = _sort_descending(token_major)[cap - 1:cap, :]
    thr = jnp.broadcast_to(kth, (SUBLANES, LANES)).T[:n_rows, 0:1]
    gt = a > thr
    eq = a == thr
    need = cap - jnp.sum(gt.astype(I32), axis=1, keepdims=True)
    upper = (lax.broadcasted_iota(I32, (n_tok, n_tok), 0) < lax.broadcasted_iota(I32, (n_tok, n_tok), 1))
    upper = upper.astype(BF16)
    eq_rank = jnp.dot(eq.astype(BF16), upper, preferred_element_type=F32)
    sel = jnp.logical_or(gt, jnp.logical_and(eq, eq_rank < need.astype(F32)))
    rank = jnp.dot(sel.astype(BF16), upper, preferred_element_type=F32)
    slot = jnp.where(sel, rank, -1.0)
    slot_ref[...] = slot.astype(I32)
    wsel_ref[...] = jnp.where(sel, a, 0.0)
    pad = jnp.full((LANES - n_e, n_tok), -1.0, F32)
    for b in range(n_rows // n_e):
        slot_t_ref[b] = jnp.concatenate([slot[b * n_e:(b + 1) * n_e], pad], axis=0).T.astype(I32)


def _topk_select(aff2d, cap, n_e):
    rows, n_tok = aff2d.shape
    spec = pl.BlockSpec((rows, n_tok), lambda i: (0, 0))
    spec_t = pl.BlockSpec((rows // n_e, n_tok, LANES), lambda i: (0, 0, 0))
    return pl.pallas_call(
        functools.partial(_topk_kernel, cap, n_e),
        grid=(1,),
        in_specs=[spec],
        out_specs=[spec, spec, spec_t],
        out_shape=[jax.ShapeDtypeStruct((rows, n_tok), I32), jax.ShapeDtypeStruct((rows, n_tok), F32),
                   jax.ShapeDtypeStruct((rows // n_e, n_tok, LANES), I32)],
        compiler_params=_params("arbitrary"),
        name="moe_topk_select",
    )(aff2d)


def _gather_kernel(cap, slot_ref, wsel_ref, hm_ref, xe_ref, gs_ref):
    n_tok = hm_ref.shape[0]
    group = slot_ref.shape[0]
    slot_of_row = lax.broadcasted_iota(I32, (cap, n_tok), 0)
    hots = [slot_ref[k] == slot_of_row for k in range(group)]
    onehot = jnp.concatenate([h.astype(BF16) for h in hots], axis=0)
    xe = jnp.dot(onehot, hm_ref[...], preferred_element_type=F32).astype(BF16)
    for k in range(group):
        xe_ref[k] = xe[k * cap:(k + 1) * cap]
        gs_ref[k] = jnp.sum(jnp.where(hots[k], wsel_ref[k], 0.0), axis=1, keepdims=True)


def _gather_tokens(slot, wsel, hm, batch, seq, cap, group=4):
    rows = batch * N_EXPERTS
    per_b = N_EXPERTS // group
    row_spec = pl.BlockSpec((group, 1, seq), lambda b, e: (b * per_b + e, 0, 0))
    return pl.pallas_call(
        functools.partial(_gather_kernel, cap),
        grid=(batch, per_b),
        in_specs=[row_spec, row_spec, pl.BlockSpec((seq, D_MODEL), lambda b, e: (b, 0))],
        out_specs=[
            pl.BlockSpec((group, cap, D_MODEL), lambda b, e: (e, b, 0)),
            pl.BlockSpec((group, cap, 1), lambda b, e: (e, b, 0)),
        ],
        out_shape=[
            jax.ShapeDtypeStruct((N_EXPERTS, batch * cap, D_MODEL), BF16),
            jax.ShapeDtypeStruct((N_EXPERTS, batch * cap, 1), F32),
        ],
        compiler_params=_params("parallel", "arbitrary"),
        name="moe_gather",
    )(slot.reshape(rows, 1, seq), wsel.reshape(rows, 1, seq), hm)


def _expert_kernel(n_e, n_t, xe_ref, wg_ref, wu_ref, wd_ref, gs_ref, ye_ref, mid_ref):
    g = pl.program_id(0)
    s = pl.program_id(1)
    t = s // 2
    tf = wg_ref.shape[-1]

    @pl.when(jnp.logical_and(s % 2 == 0, g < n_e))
    def _():
        x = xe_ref[0]
        a = jnp.dot(x, wg_ref[...].astype(BF16), preferred_element_type=F32)
        u = jnp.dot(x, wu_ref[...].astype(BF16), preferred_element_type=F32)
        mid_ref[g % 2, t] = (a * jax.nn.sigmoid(a) * u).astype(BF16)

    @pl.when(jnp.logical_and(s % 2 == 1, g >= 1))
    def _():
        prev = (g - 1) % 2
        acc = jnp.dot(mid_ref[prev, 0], wd_ref[0:tf, :].astype(BF16), preferred_element_type=F32)
        for f in range(1, n_t):
            acc += jnp.dot(mid_ref[prev, f], wd_ref[f * tf:(f + 1) * tf, :].astype(BF16),
                           preferred_element_type=F32)
        ye_ref[0] = (acc * gs_ref[0]).astype(BF16)


def _expert_ffn(xe, gs, w_gate, w_up, w_down, layer, tf):
    n_e, rows, d = xe.shape
    ff = w_gate.shape[-1]
    n_t = ff // tf
    assert d // tf == n_t
    last = n_e - 1

    n_s = 2 * n_t

    def fill_tile(g, s, lead):
        v = jnp.minimum(g * n_s + s + lead, n_e * n_s - 1)
        return (layer, v // n_s, 0, (v % n_s) // 2)

    def emit_tile(g, s):
        v = jnp.maximum(g * n_s + s - 1 - n_s, 0)
        return (layer, v // n_s, 0, (v % n_s) // 2)

    return pl.pallas_call(
        functools.partial(_expert_kernel, n_e, n_t),
        grid=(n_e + 1, n_s),
        in_specs=[
            pl.BlockSpec((1, rows, d), lambda g, s: (jnp.minimum(g, last), 0, 0)),
            pl.BlockSpec((None, None, d, tf), functools.partial(fill_tile, lead=1)),
            pl.BlockSpec((None, None, d, tf), functools.partial(fill_tile, lead=0)),
            pl.BlockSpec((None, None, ff, tf), emit_tile),
            pl.BlockSpec((1, rows, 1), lambda g, s: (jnp.maximum(g - 1, 0), 0, 0)),
        ],
        out_specs=pl.BlockSpec((1, rows, tf), lambda g, s: emit_tile(g, s)[1:]),
        out_shape=jax.ShapeDtypeStruct((n_e, rows, d), BF16),
        scratch_shapes=[pltpu.VMEM((2, n_t, rows, tf), BF16)],
        compiler_params=_params("arbitrary", "arbitrary"),
        name="moe_expert_ffn",
    )(xe, w_gate, w_up, w_down, gs)


def _scatter_kernel(cap, slot_ref, ye_ref, x_ref, *rest):
    o_ref = rest[-1]
    n_e = ye_ref.shape[0]
    col = lax.broadcasted_iota(I32, (x_ref.shape[0], cap), 1)
    slots = slot_ref[0]
    onehot = jnp.concatenate([(slots[:, e:e + 1] == col).astype(BF16) for e in range(n_e)], axis=1)
    ye = ye_ref[...].reshape(n_e * cap, ye_ref.shape[-1])
    y = x_ref[...] + jnp.dot(onehot, ye, preferred_element_type=F32)
    o_ref[...] = _rms(y, rest[0][...]) if len(rest) == 2 else y


def _scatter_add(slot_t, ye, x2d, batch, seq, cap, tr, final_gain=None):
    d = x2d.shape[-1]
    per_b = seq // tr
    in_specs = [
        pl.BlockSpec((1, tr, LANES), lambda b, i: (b, i, 0)),
        pl.BlockSpec((N_EXPERTS, cap, d), lambda b, i: (0, b, 0)),
        pl.BlockSpec((tr, d), lambda b, i: (b * per_b + i, 0)),
    ]
    args = [slot_t, ye, x2d]
    if final_gain is not None:
        in_specs.append(pl.BlockSpec((1, d), lambda b, i: (0, 0)))
        args.append(final_gain.reshape(1, d))
    return pl.pallas_call(
        functools.partial(_scatter_kernel, cap),
        grid=(batch, per_b),
        in_specs=in_specs,
        out_specs=pl.BlockSpec((tr, d), lambda b, i: (b * per_b + i, 0)),
        out_shape=jax.ShapeDtypeStruct(x2d.shape, F32),
        compiler_params=_params("parallel", "arbitrary"),
        name="moe_scatter_add",
    )(*args)


def _expert_choice_ffn(x2d, hm, aff, w_gate, w_up, w_down, layer, batch, seq, final_gain=None):
    cap = EC_CAPACITY * seq // N_EXPERTS
    slot, wsel, slot_t = _topk_select(aff.reshape(batch * N_EXPERTS, seq), cap, N_EXPERTS)
    xe, gs = _gather_tokens(slot, wsel, hm, batch, seq, cap)
    ye = _expert_ffn(xe, gs, w_gate, w_up, w_down, layer, 512)
    return _scatter_add(slot_t, ye, x2d, batch, seq, cap, 256, final_gain)


def kernel(x, mem, norm_mix, w_in, na_rpb, hy_conv_w, hy_conv_b, hy_filt_w1, hy_filt_b1, hy_filt_w2, hy_filt_b2, hy_filt_w3, hy_sin_freq, hy_skip_d, branch_norm, w_out, norm_cross, mem_norm, w_cq, w_ckv, w_co, norm_moe, w_router, w_gate, w_up, w_down, final_norm):
    batch, seq, d = x.shape
    n_mem = mem.shape[1]
    depth = w_in.shape[0]
    m = batch * seq
    na_cols = 3 * NA_W
    hy_cols = 3 * HY_W
    rows = seq // GRID_W

    dft_fwd, dft_inv = _dft_tables(seq)
    x2d = x.reshape(m, d)
    mem2d = mem.reshape(batch * n_mem, d)

    w_out_b, w_cq_b, w_co_b = (w.astype(BF16) for w in (w_out, w_cq, w_co))
    p_in = w_in.shape[-1]
    kv = _norm_proj_layers(mem2d, mem_norm, w_ckv, 1024)

    for l in range(depth):
        assert p_in == 3 * na_cols
        slabs = [_norm_proj_resident(x2d, norm_mix[l], w_in, l, s, 3, 512, "in_proj").reshape(batch, seq, na_cols)
                 for s in range(3)]

        y_na = _neighbourhood_attention(slabs[0], _na_bias_table(na_rpb[l], rows), batch, seq)
        y_hy = _hyena_mixer(slabs[1], 0, hy_conv_w[l], hy_conv_b[l], hy_filt_w1[l],
                            hy_filt_b1[l], hy_filt_w2[l], hy_filt_b2[l], hy_filt_w3[l], hy_sin_freq[l],
                            hy_skip_d[l], dft_fwd, dft_inv, batch, seq)
        y_ret = _retention_mixer(slabs[1], hy_cols, slabs[2], 0, batch, seq)
        x2d, q = _branch_out_cross_q(y_na.reshape(m, NA_W), y_hy.reshape(m, HY_W), y_ret.reshape(m, RET_W),
                                     branch_norm[l], w_out_b, x2d, norm_cross[l], w_cq_b, l, 512)
        x2d, hm, aff = _cross_out_and_router(q, kv, w_co_b, l, x2d, norm_moe[l], w_router[l], batch, seq, n_mem,
                                             256)

        x2d = _expert_choice_ffn(x2d, hm, aff, w_gate, w_up, w_down, l, batch, seq,
                                 final_gain=final_norm if l == depth - 1 else None)

    return x2d.reshape(batch, seq, d)
```

```python
import functools
import math

import numpy as np
import jax
import jax.numpy as jnp
from jax import lax
from jax.experimental import pallas as pl
from jax.experimental.pallas import tpu as pltpu

F32 = jnp.float32
BF16 = jnp.bfloat16
I32 = jnp.int32

D_MODEL = 2048
GRID_W = 64
NA_HEAD_DIM = 64
NA_W = 768
NA_HEADS = 12
NA_KR = 8
NA_KC = 16
HY_W = 512
HY_ORDER = 2
HY_BANDS = 8
HY_POS_DIM = 17
HY_FILT_FF = 64
RET_HEAD_DIM = 128
RET_W = 768
RET_HEADS = 6
RET_CHUNK = 128
CROSS_HEADS = 4
CROSS_HEAD_DIM = 512
N_EXPERTS = 16
EXPERT_FF = 2048
EC_CAPACITY = 2
RMS_EPS = 1e-6
GN_EPS = 1e-5

MASK_VALUE = -1e30
LANES = 128
VMEM_LIMIT_BYTES = 56 * 1024 * 1024

NT_DIMS = (((1,), (1,)), ((), ()))


def _params(*sem):
    return pltpu.CompilerParams(dimension_semantics=sem, vmem_limit_bytes=VMEM_LIMIT_BYTES)


def _rms(xf, g):
    return xf * lax.rsqrt(jnp.mean(xf * xf, axis=-1, keepdims=True) + RMS_EPS) * g


def _norm_proj_layers_kernel(x_ref, g_ref, w_ref, o_ref, h_ref):
    @pl.when(pl.program_id(0) == 0)
    def _():
        h_ref[...] = _rms(x_ref[...], g_ref[...]).astype(BF16)

    o_ref[...] = jnp.dot(h_ref[...], w_ref[...].astype(BF16), preferred_element_type=F32).astype(o_ref.dtype)


def _norm_proj_layers(x2d, gain, w, tn):
    m, k = x2d.shape
    n_layers, _, n = w.shape
    per_layer = n // tn
    assert n % tn == 0
    return pl.pallas_call(
        _norm_proj_layers_kernel,
        grid=(n_layers * per_layer,),
        in_specs=[
            pl.BlockSpec((m, k), lambda j: (0, 0)),
            pl.BlockSpec((1, k), lambda j: (0, 0)),
            pl.BlockSpec((None, k, tn), lambda j: (j // per_layer, 0, j % per_layer)),
        ],
        out_specs=pl.BlockSpec((m, tn), lambda j: (0, j)),
        out_shape=jax.ShapeDtypeStruct((m, n_layers * n), BF16),
        scratch_shapes=[pltpu.VMEM((m, k), BF16)],
        compiler_params=_params("arbitrary"),
        name="cross_kv_proj",
    )(x2d, gain.reshape(1, k), w)


def _norm_proj_resident_kernel(x_ref, g_ref, w_ref, o_ref, wb_ref):
    @pl.when(pl.program_id(0) == 0)
    def _():
        wb_ref[...] = w_ref[...].astype(BF16)

    h = _rms(x_ref[...], g_ref[...]).astype(BF16)
    o_ref[...] = jnp.dot(h, wb_ref[...], preferred_element_type=F32).astype(o_ref.dtype)


def _norm_proj_resident(x2d, gain, w, layer, slab, n_slab, tm, name):
    m, k = x2d.shape
    width = w.shape[-1] // n_slab
    assert w.shape[-1] % n_slab == 0 and m % tm == 0
    return pl.pallas_call(
        _norm_proj_resident_kernel,
        grid=(m // tm,),
        in_specs=[
            pl.BlockSpec((tm, k), lambda i: (i, 0)),
            pl.BlockSpec((1, k), lambda i: (0, 0)),
            pl.BlockSpec((None, k, width), lambda i: (layer, 0, slab)),
        ],
        out_specs=pl.BlockSpec((tm, width), lambda i: (i, 0)),
        out_shape=jax.ShapeDtypeStruct((m, width), BF16),
        scratch_shapes=[pltpu.VMEM((k, width), BF16)],
        compiler_params=_params("arbitrary"),
        name=name,
    )(x2d, gain.reshape(1, k), w)


def _cross_prologue(h_ref, q_ref, k_ref, v_ref):
    dh = CROSS_HEAD_DIM
    for h in range(CROSS_HEADS):
        sl = slice(h * dh, (h + 1) * dh)
        s = lax.dot_general(q_ref[:, sl], k_ref[:, sl], NT_DIMS, preferred_element_type=F32) * (dh ** -0.5)
        m = jnp.max(s, axis=-1, keepdims=True)
        p = jnp.exp(s - m)
        l = jnp.sum(p, axis=-1, keepdims=True)
        o = jnp.dot(p.astype(BF16), v_ref[:, sl], preferred_element_type=F32) * (1.0 / l)
        h_ref[:, sl] = o.astype(BF16)


def _branch_prologue(h_ref, na_ref, hy_ref, ret_ref, g_ref):
    def nrm(p):
        return p * lax.rsqrt(jnp.mean(p * p, axis=-1, keepdims=True) + RMS_EPS)

    y = jnp.concatenate([nrm(r[...].astype(F32)) for r in (na_ref, hy_ref, ret_ref)], axis=-1)
    h_ref[...] = (y * g_ref[...]).astype(BF16)


def _branch_out_cross_q_kernel(na_ref, hy_ref, ret_ref, g_ref, wo_ref, res_ref, gq_ref, wq_ref, x_ref, q_ref,
                               h_ref):
    _branch_prologue(h_ref, na_ref, hy_ref, ret_ref, g_ref)
    x = res_ref[...] + jnp.dot(h_ref[...], wo_ref[...], preferred_element_type=F32)
    x_ref[...] = x
    hq = _rms(x, gq_ref[...]).astype(BF16)
    q_ref[...] = jnp.dot(hq, wq_ref[...], preferred_element_type=F32).astype(BF16)


def _branch_out_cross_q(y_na, y_hy, y_ret, gain, w_out, res, gain_q, w_cq, layer, tm):
    m, d = res.shape
    rows = lambda width: pl.BlockSpec((tm, width), lambda i: (i, 0))
    vec = pl.BlockSpec((1, d), lambda i: (0, 0))
    square = pl.BlockSpec((None, d, d), lambda i: (layer, 0, 0))
    return pl.pallas_call(
        _branch_out_cross_q_kernel,
        grid=(m // tm,),
        in_specs=[rows(NA_W), rows(HY_W), rows(RET_W), vec, square, rows(d), vec, square],
        out_specs=[rows(d), rows(d)],
        out_shape=[jax.ShapeDtypeStruct((m, d), F32), jax.ShapeDtypeStruct((m, d), BF16)],
        scratch_shapes=[pltpu.VMEM((tm, d), BF16)],
        compiler_params=_params("parallel"),
        name="branch_out_cross_q",
    )(y_na, y_hy, y_ret, gain.reshape(1, d), w_out, res, gain_q.reshape(1, d), w_cq)


NA_PAIR = 2
NA_BAND = NA_KR + NA_PAIR


def _na_band_base(r, rows):
    return np.clip(r - NA_KR // 2, 0, rows - NA_BAND)


@functools.lru_cache(maxsize=None)
def _na_variants(rows):
    assert rows % NA_PAIR == 0 and rows >= NA_BAND + 2
    n_var = NA_KR // 2 + 1
    dr = np.full((n_var, NA_BAND, NA_PAIR), -2, np.int64)
    for r in range(0, rows, NA_PAIR):
        base = int(_na_band_base(r, rows))
        v = (r - base) // 2
        for j in range(NA_PAIR):
            rs = int(np.clip(r + j - NA_KR // 2, 0, rows - NA_KR))
            for i in range(NA_BAND):
                val = base + i - (r + j) + (NA_KR - 1) if rs <= base + i < rs + NA_KR else -1
                assert dr[v, i, j] in (-2, val)
                dr[v, i, j] = val
    assert (dr > -2).all()
    return dr


def _na_bias_table(rpb, rows):
    c = np.arange(GRID_W)
    col_start = np.clip(c - NA_KC // 2, 0, GRID_W - NA_KC)
    col_in = (c[None, :] >= col_start[:, None]) & (c[None, :] < col_start[:, None] + NA_KC)
    dc = np.clip(c[None, :] - c[:, None] + (NA_KC - 1), 0, 2 * NA_KC - 2)
    onehot = (dc.T.reshape(-1)[None, :] == np.arange(2 * NA_KC - 1)[:, None]).astype(np.float32)
    cols = jnp.einsum("hab,bn->han", rpb.astype(F32), jnp.asarray(onehot), precision=lax.Precision.HIGHEST)
    cols = cols.reshape(NA_HEADS, 2 * NA_KR - 1, GRID_W, GRID_W)
    cols = jnp.where(col_in.T[None, None], cols, MASK_VALUE)
    dr = _na_variants(rows)
    n_var = dr.shape[0]
    return pl.pallas_call(
        functools.partial(_na_table_kernel, dr),
        grid=(NA_HEADS,),
        in_specs=[pl.BlockSpec((1, 2 * NA_KR - 1, GRID_W, GRID_W), lambda h: (h, 0, 0, 0))],
        out_specs=pl.BlockSpec((1, n_var, NA_BAND * GRID_W, NA_PAIR * GRID_W), lambda h: (h, 0, 0, 0)),
        out_shape=jax.ShapeDtypeStruct((NA_HEADS, n_var, NA_BAND * GRID_W, NA_PAIR * GRID_W), F32),
        compiler_params=_params("parallel"),
        name="na_bias_table",
    )(cols)


def _na_table_kernel(dr, cols_ref, o_ref):
    masked = jnp.full((GRID_W, GRID_W), MASK_VALUE, F32)
    for v in range(dr.shape[0]):
        for i in range(NA_BAND):
            blocks = [cols_ref[0, int(dr[v, i, j])] if dr[v, i, j] >= 0 else masked for j in range(NA_PAIR)]
            o_ref[0, v, i * GRID_W:(i + 1) * GRID_W, :] = jnp.concatenate(blocks, axis=-1)


def _na_kernel(rows, q_ref, k_ref, v_ref, bias_ref, o_ref, vt_ref, s0_ref, s1_ref, p0_ref, p1_ref, l0_ref,
               l1_ref):
    dh = NA_HEAD_DIM
    nq = NA_PAIR * GRID_W
    nk = NA_BAND * GRID_W
    n_chunk = nk // LANES
    lane = lax.broadcasted_iota(I32, (nq, 2 * dh), 1)

    for ch in range(vt_ref.shape[0]):
        vt_ref[ch] = v_ref[0, ch * LANES:(ch + 1) * LANES, :].astype(F32).T.astype(BF16)

    n_steps = rows // NA_PAIR

    def band_base(p):
        return jnp.clip(p * NA_PAIR - NA_KR // 2, 0, rows - NA_BAND)


    def scores(p, s_out):
        r = p * NA_PAIR
        base = band_base(p)
        variant = (r - base) // 2
        q = q_ref[0, pl.ds(pl.multiple_of(r * GRID_W, nq), nq), :] * (dh ** -0.5)
        kb = k_ref[0, pl.ds(pl.multiple_of(base * GRID_W, LANES), nk), :]
        zero = jnp.zeros_like(q)
        q2 = jnp.concatenate([jnp.where(lane < dh, q, zero), jnp.where(lane >= dh, q, zero)], axis=0)
        st2 = lax.dot_general(kb, q2, NT_DIMS, preferred_element_type=F32)
        for hh in range(2):
            s_out[:, hh * nq:(hh + 1) * nq] = st2[:, hh * nq:(hh + 1) * nq] + bias_ref[hh, variant]

    def softmax(s_in, p_out, l_out):
        st = s_in[...]
        pt = jnp.exp(st - jnp.max(st, axis=0, keepdims=True))
        p_out[...] = pt.astype(BF16)
        l_out[...] = 1.0 / jnp.sum(pt, axis=0, keepdims=True)

    def values(p, p_in, l_in):
        r = p * NA_PAIR
        c0 = band_base(p) // 2
        vt = jnp.concatenate([vt_ref[c0 + i] for i in range(n_chunk)], axis=1)
        pt = p_in[...]
        linv = l_in[...]
        outs = []
        for hh in range(2):
            cols = slice(hh * nq, (hh + 1) * nq)
            ot = jnp.dot(vt[hh * dh:(hh + 1) * dh, :], pt[:, cols], preferred_element_type=F32)
            outs.append(ot * linv[:, cols])
        o_ref[0, pl.ds(pl.multiple_of(r * GRID_W, nq), nq), :] = jnp.concatenate(outs, axis=0).T.astype(BF16)

    s_slots = (s0_ref, s1_ref)
    p_slots = (p0_ref, p1_ref)
    l_slots = (l0_ref, l1_ref)

    def step(t, parity, do_scores=True, do_softmax=True, do_values=True):
        a, b = parity, 1 - parity
        t = jnp.asarray(t, I32)
        if do_scores:
            scores(t, s_slots[a])
        if do_softmax:
            softmax(s_slots[b], p_slots[b], l_slots[b])
        if do_values:
            values(t - 2, p_slots[a], l_slots[a])

    assert n_steps % 2 == 0 and n_steps >= 4
    step(0, 0, do_softmax=False, do_values=False)
    step(1, 1, do_values=False)

    def steady(i, carry):
        t = 2 + 2 * i
        step(t, 0)
        step(t + 1, 1)
        return carry

    lax.fori_loop(0, (n_steps - 2) // 2, steady, 0, unroll=True)
    step(n_steps, 0, do_scores=False)
    step(n_steps + 1, 1, do_scores=False, do_softmax=False)


def _neighbourhood_attention(proj_na, bias_tbl, batch, seq):
    rows = seq // GRID_W
    n_pairs = NA_HEADS // 2
    blk = (1, seq, 2 * NA_HEAD_DIM)
    n_var, nk, nq = bias_tbl.shape[1:]
    return pl.pallas_call(
        functools.partial(_na_kernel, rows),
        grid=(batch, n_pairs),
        in_specs=[
            pl.BlockSpec(blk, lambda b, h: (b, 0, h)),
            pl.BlockSpec(blk, lambda b, h: (b, 0, n_pairs + h)),
            pl.BlockSpec(blk, lambda b, h: (b, 0, 2 * n_pairs + h)),
            pl.BlockSpec((2, n_var, nk, nq), lambda b, h: (h, 0, 0, 0)),
        ],
        out_specs=pl.BlockSpec(blk, lambda b, h: (b, 0, h)),
        out_shape=jax.ShapeDtypeStruct((batch, seq, NA_W), BF16),
        scratch_shapes=[pltpu.VMEM((seq // LANES, 2 * NA_HEAD_DIM, LANES), BF16),
                        pltpu.VMEM((nk, 2 * nq), F32), pltpu.VMEM((nk, 2 * nq), F32),
                        pltpu.VMEM((nk, 2 * nq), BF16), pltpu.VMEM((nk, 2 * nq), BF16),
                        pltpu.VMEM((1, 2 * nq), F32), pltpu.VMEM((1, 2 * nq), F32)],
        compiler_params=_params("parallel", "arbitrary"),
        name="neighbourhood_attention",
    )(proj_na, proj_na, proj_na, bias_tbl)


@functools.lru_cache(maxsize=None)
def _dft_factors(seq):
    n = 2 * seq
    t = np.arange(seq, dtype=np.int64)
    f1 = np.arange(seq // 64, dtype=np.int64)
    f0 = np.arange(64, dtype=np.int64)
    a = 2.0 * np.pi * ((64 * f1[:, None] * t[None, :]) % n).astype(np.float64) / n
    b = 2.0 * np.pi * ((f0[:, None] * t[None, :]) % n).astype(np.float64) / n
    ny = np.where(t % 2 == 0, 1.0, -1.0)
    return tuple(np.asarray(v, np.float32) for v in (np.cos(a), np.sin(a), np.cos(b), np.sin(b), ny))


def _dft_table_kernel(u_ref, v_ref, cb_ref, sb_ref, fwd_ref, inv_ref):
    sin_half = pl.program_id(0) == 1
    first = pl.program_id(1) == 0
    n_fine, seq = cb_ref.shape
    row = lax.broadcasted_iota(I32, (n_fine, seq), 0)
    col = lax.broadcasted_iota(I32, (n_fine, seq), 1)
    for c in range(u_ref.shape[1]):
        tile = u_ref[0, c:c + 1, :] * cb_ref[...] + v_ref[0, c:c + 1, :] * sb_ref[...]
        rows = slice(c * n_fine, (c + 1) * n_fine)
        fwd = tile
        if c == 0:
            fwd_nyq = jnp.logical_and(jnp.logical_and(sin_half, first), row == 0)
            fwd = jnp.where(fwd_nyq, (1 - 2 * (col & 1)).astype(F32), tile)
        fwd_ref[0, rows, :] = fwd.astype(BF16)
        inv_nyq = jnp.logical_and(sin_half, col == 0)
        inv_ref[rows, :] = jnp.where(inv_nyq, (1 - 2 * (row & 1)).astype(F32), tile).astype(BF16)


def _dft_tables(seq):
    ca, sa, cb, sb, _ = _dft_factors(seq)
    u = np.stack([ca, sa])
    v = np.stack([-sa, ca])
    n_coarse = 8
    rows = 64 * n_coarse
    coarse_spec = pl.BlockSpec((1, n_coarse, seq), lambda h, i: (h, i, 0))
    fine_spec = pl.BlockSpec((64, seq), lambda h, i: (0, 0))
    return pl.pallas_call(
        _dft_table_kernel,
        grid=(2, seq // rows),
        in_specs=[coarse_spec, coarse_spec, fine_spec, fine_spec],
        out_specs=[pl.BlockSpec((1, rows, seq), lambda h, i: (h, i, 0)),
                   pl.BlockSpec((rows, seq), lambda h, i: (i, h))],
        out_shape=[jax.ShapeDtypeStruct((2, seq, seq), BF16), jax.ShapeDtypeStruct((seq, 2 * seq), BF16)],
        compiler_params=_params("parallel", "arbitrary"),
        name="dft_tables",
    )(jnp.asarray(u), jnp.asarray(v), jnp.asarray(cb), jnp.asarray(sb))


@functools.lru_cache(maxsize=None)
def _hyena_consts(seq):
    t = np.arange(seq, dtype=np.float64)
    t01 = t / (seq - 1)
    bands = np.linspace(1e-4, HY_BANDS - 1, HY_BANDS)
    ang = (2.0 * math.pi) * (t[:, None] / seq) * bands[None, :]
    feats = np.concatenate([t01[:, None], np.cos(ang), -np.sin(ang)], axis=-1)
    feats_p = np.zeros((seq, LANES), np.float32)
    feats_p[:, :HY_POS_DIM] = feats
    min_decay = math.log(1e-2) / 1.5
    max_decay = math.log(1e-2) / 0.3
    deltas = np.abs(np.linspace(min_decay, max_decay, HY_W))
    window = np.exp(-t01[:, None] * deltas[None, :]).astype(np.float32)
    return feats_p, window


def _filter_kernel(feats_ref, w1_ref, b1_ref, w2_ref, b2_ref, freq_ref, w3f_ref, w3b_ref, win_ref, sum_ref,
                   diff_ref, hid_ref):
    hp = lax.Precision.HIGHEST

    @pl.when(pl.program_id(0) == 0)
    def _():
        f = freq_ref[...]
        h1 = jnp.sin(f * (jnp.dot(feats_ref[...], w1_ref[...], precision=hp, preferred_element_type=F32)
                          + b1_ref[...]))
        hid_ref[...] = jnp.sin(f * (jnp.dot(h1, w2_ref[...], precision=hp, preferred_element_type=F32)
                                    + b2_ref[...]))

    hid = hid_ref[...]
    win = win_ref[...]
    fwd = jnp.dot(hid, w3f_ref[...], precision=hp, preferred_element_type=F32) * win
    bwd = jnp.dot(hid, w3b_ref[...], precision=hp, preferred_element_type=F32) * win
    bwd = jnp.where(lax.broadcasted_iota(I32, bwd.shape, 0) == 0, 0.0, bwd)
    sum_ref[...] = (fwd + bwd).astype(BF16)
    diff_ref[...] = (fwd - bwd).astype(BF16)


def _hyena_filters_time(w1, b1, w2, b2, w3, freq, seq):
    feats, window = _hyena_consts(seq)
    w1p = jnp.zeros((LANES, HY_FILT_FF), F32).at[:HY_POS_DIM].set(w1)
    full = lambda shape: pl.BlockSpec(shape, lambda o: (0,) * len(shape))
    out_spec = pl.BlockSpec((seq, HY_W), lambda o: (0, o))
    out_shape = jax.ShapeDtypeStruct((seq, HY_ORDER * HY_W), BF16)
    return pl.pallas_call(
        _filter_kernel,
        grid=(HY_ORDER,),
        in_specs=[
            full((seq, LANES)), full((LANES, HY_FILT_FF)), full((1, HY_FILT_FF)),
            full((HY_FILT_FF, HY_FILT_FF)), full((1, HY_FILT_FF)), full((1, HY_FILT_FF)),
            pl.BlockSpec((HY_FILT_FF, HY_W), lambda o: (0, 2 * o)),
            pl.BlockSpec((HY_FILT_FF, HY_W), lambda o: (0, 2 * o + 1)),
            full((seq, HY_W)),
        ],
        out_specs=[out_spec, out_spec],
        out_shape=[out_shape, out_shape],
        scratch_shapes=[pltpu.VMEM((seq, HY_FILT_FF), F32)],
        compiler_params=_params("arbitrary"),
        name="hyena_filter_mlp",
    )(jnp.asarray(feats), w1p, b1.reshape(1, -1), w2, b2.reshape(1, -1), freq.reshape(1, -1), w3, w3,
      jnp.asarray(window))


def _filter_dft_kernel(f_ref, sum_ref, diff_ref, o_ref):
    o_ref[0] = jnp.dot(f_ref[0], sum_ref[...], preferred_element_type=F32)
    o_ref[1] = jnp.dot(f_ref[1], diff_ref[...], preferred_element_type=F32)

    @pl.when(pl.program_id(0) == 0)
    def _():
        top = jnp.dot(f_ref[1, 0:8, :], sum_ref[...], preferred_element_type=F32)
        row = lax.broadcasted_iota(I32, top.shape, 0)
        o_ref[1, 0:8, :] = jnp.where(row == 0, top, o_ref[1, 0:8, :])


def _filter_dft(dft_fwd, filt_sum, filt_diff, seq, fb):
    return pl.pallas_call(
        _filter_dft_kernel,
        grid=(seq // fb, HY_ORDER),
        in_specs=[
            pl.BlockSpec((2, fb, seq), lambda f, o: (0, f, 0)),
            pl.BlockSpec((seq, HY_W), lambda f, o: (0, o)),
            pl.BlockSpec((seq, HY_W), lambda f, o: (0, o)),
        ],
        out_specs=pl.BlockSpec((2, fb, HY_W), lambda f, o: (0, f, o)),
        out_shape=jax.ShapeDtypeStruct((2, seq, HY_ORDER * HY_W), F32),
        compiler_params=_params("parallel", "arbitrary"),
        name="hyena_filter_dft",
    )(dft_fwd, filt_sum, filt_diff)


def _short_conv_kernel(p_ref, w_ref, b_ref, o_ref):
    p = p_ref[0].astype(F32)
    seq = p.shape[0]
    row = lax.broadcasted_iota(I32, p.shape, 0)
    prev = jnp.where(row == 0, 0.0, pltpu.roll(p, 1, 0))
    nxt = jnp.where(row == seq - 1, 0.0, pltpu.roll(p, seq - 1, 0))
    w = w_ref[...]
    o_ref[0] = (prev * w[0:1] + p * w[1:2] + nxt * w[2:3] + b_ref[...]).astype(o_ref.dtype)


def _short_conv(proj, col0, conv_w, conv_b, batch, seq, tc):
    n_cols = conv_w.shape[-1]
    assert col0 % tc == 0 and n_cols % tc == 0
    cb0 = col0 // tc
    return pl.pallas_call(
        _short_conv_kernel,
        grid=(batch, n_cols // tc),
        in_specs=[
            pl.BlockSpec((1, seq, tc), lambda b, c: (b, 0, cb0 + c)),
            pl.BlockSpec((3, tc), lambda b, c: (0, c)),
            pl.BlockSpec((1, tc), lambda b, c: (0, c)),
        ],
        out_specs=pl.BlockSpec((1, seq, tc), lambda b, c: (b, 0, c)),
        out_shape=jax.ShapeDtypeStruct((batch, seq, n_cols), BF16),
        compiler_params=_params("parallel", "arbitrary"),
        name="hyena_short_conv",
    )(proj, conv_w, conv_b.reshape(1, -1))


def _spectrum_kernel(n_fft, f_ref, z_ref, k_ref, o_ref):
    z = z_ref[0].astype(BF16)
    xr = jnp.dot(f_ref[0], z, preferred_element_type=F32)
    xs = jnp.dot(f_ref[1], z, preferred_element_type=F32)
    kr = k_ref[0]
    ks = k_ref[1]
    row = lax.broadcasted_iota(I32, xr.shape, 0)
    edge = jnp.logical_and(row == 0, pl.program_id(0) == 0)
    yr = jnp.where(edge, xr * kr * (1.0 / n_fft), (xr * kr - xs * ks) * (2.0 / n_fft))
    ys = jnp.where(edge, xs * ks * (1.0 / n_fft), (xr * ks + xs * kr) * (2.0 / n_fft))
    o_ref[0, 0] = yr.astype(BF16)
    o_ref[0, 1] = ys.astype(BF16)


def _spectrum_product(dft_fwd, z_arr, z_col, kfreq, order, batch, seq, fb):
    return pl.pallas_call(
        functools.partial(_spectrum_kernel, 2 * seq),
        grid=(seq // fb, batch),
        in_specs=[
            pl.BlockSpec((2, fb, seq), lambda f, b: (0, f, 0)),
            pl.BlockSpec((1, seq, HY_W), lambda f, b: (b, 0, z_col)),
            pl.BlockSpec((2, fb, HY_W), lambda f, b: (0, f, order)),
        ],
        out_specs=pl.BlockSpec((1, 2, fb, HY_W), lambda f, b: (b, 0, f, 0)),
        out_shape=jax.ShapeDtypeStruct((batch, 2, seq, HY_W), BF16),
        compiler_params=_params("parallel", "arbitrary"),
        name="hyena_spectrum",
    )(dft_fwd, z_arr, kfreq)


def _inverse_kernel(ft_ref, y_ref, gate_ref, z_ref, d_ref, o_ref):
    conv = jnp.dot(ft_ref[...], y_ref[0], preferred_element_type=F32)
    o_ref[0] = (gate_ref[0].astype(F32) * (conv + d_ref[...] * z_ref[0].astype(F32))).astype(o_ref.dtype)


def _inverse_gate(dft_inv, y, gate_arr, gate_col, z_arr, z_col, skip_row, batch, seq, tb):
    y2 = y.reshape(batch, 2 * seq, HY_W)
    return pl.pallas_call(
        _inverse_kernel,
        grid=(seq // tb, batch),
        in_specs=[
            pl.BlockSpec((tb, 2 * seq), lambda t, b: (t, 0)),
            pl.BlockSpec((1, 2 * seq, HY_W), lambda t, b: (b, 0, 0)),
            pl.BlockSpec((1, tb, HY_W), lambda t, b: (b, t, gate_col)),
            pl.BlockSpec((1, tb, HY_W), lambda t, b: (b, t, z_col)),
            pl.BlockSpec((1, HY_W), lambda t, b: (0, 0)),
        ],
        out_specs=pl.BlockSpec((1, tb, HY_W), lambda t, b: (b, t, 0)),
        out_shape=jax.ShapeDtypeStruct((batch, seq, HY_W), BF16),
        compiler_params=_params("parallel", "arbitrary"),
        name="hyena_inverse_gate",
    )(dft_inv, y2, gate_arr, z_arr, skip_row.reshape(1, HY_W))


def _hyena_mixer(proj, col0, conv_w, conv_b, w1, b1, w2, b2, w3, freq, skip_d, dft_fwd, dft_inv, batch, seq):
    s = _short_conv(proj, col0, conv_w, conv_b, batch, seq, 768)
    filt_sum, filt_diff = _hyena_filters_time(w1, b1, w2, b2, w3, freq, seq)
    kfreq = _filter_dft(dft_fwd, filt_sum, filt_diff, seq, 512)
    z_arr, z_col = s, 2
    for o in range(HY_ORDER):
        y = _spectrum_product(dft_fwd, z_arr, z_col, kfreq, o, batch, seq, 2048)
        z_arr = _inverse_gate(dft_inv, y, s, o, z_arr, z_col, skip_d[o], batch, seq, 2048)
        z_col = 0
    return z_arr


@functools.lru_cache(maxsize=None)
def _retention_consts(seq):
    c = RET_CHUNK
    half = RET_HEAD_DIM // 2
    inv = 1.0 / (10000.0 ** np.linspace(0.0, 1.0, half))
    ang = np.arange(seq, dtype=np.float64)[:, None] * inv[None, :]
    cos2 = np.concatenate([np.cos(ang), np.cos(ang)], axis=-1).astype(np.float32)
    sin2 = np.concatenate([-np.sin(ang), np.sin(ang)], axis=-1).astype(np.float32)
    hidx = np.arange(RET_HEADS, dtype=np.float64)
    lg_f = np.log1p(-np.exp2(-5.0 - hidx))[:, None, None]
    lg_b = np.log1p(-np.exp2(-5.5 - hidx))[:, None, None]
    i = np.arange(c, dtype=np.float64)
    diff = i[:, None] - i[None, :]
    ones = np.ones((1, c, c))
    dec = np.where(diff >= 0, np.exp(lg_f * np.maximum(diff, 0.0)), np.exp(lg_b * np.maximum(-diff, 0.0)))
    rowv = lambda v: v[:, :, None] * ones
    tab = np.stack([
        dec,
        rowv(np.exp(lg_f[:, :, 0] * (i + 1.0)[None, :])),
        rowv(np.exp(lg_f[:, :, 0] * (c - 1.0 - i)[None, :])),
        rowv(np.exp(lg_b[:, :, 0] * (c - i)[None, :])),
        rowv(np.exp(lg_b[:, :, 0] * i[None, :])),
        np.exp(lg_f * c) * ones,
        np.exp(lg_b * c) * ones,
    ], axis=1).astype(np.float32)
    return cos2, sin2, tab


def _retention_kernel(q_ref, k_ref, v_ref, g_ref, cos_ref, sin_ref, tab_ref, o_ref, qs_ref, ks_ref, kvf_ref,
                      kvb_ref, a0_ref, a1_ref, y0_ref, y1_ref):
    c = RET_CHUNK
    d = RET_HEAD_DIM
    seq = q_ref.shape[1]
    n_chunks = seq // c
    cos = cos_ref[...]
    sin = sin_ref[...]
    q = q_ref[0].astype(F32)
    k = k_ref[0].astype(F32)
    qs_ref[...] = (q * cos + pltpu.roll(q, d // 2, 1) * sin) * (d ** -0.5)
    ks_ref[...] = k * cos + pltpu.roll(k, d // 2, 1) * sin

    def mm(a, b):
        return jnp.dot(a.astype(BF16), b.astype(BF16), preferred_element_type=F32)

    def chunk(n):
        return pl.ds(pl.multiple_of(n * c, c), c)

    def kv_body(n, carry):
        kc = ks_ref[chunk(n), :]
        vc = v_ref[0, chunk(n), :]
        kvf_ref[n] = mm((kc * tab_ref[0, 2]).T, vc)
        kvb_ref[n] = mm((kc * tab_ref[0, 4]).T, vc)
        return carry

    lax.fori_loop(0, n_chunks, kv_body, 0, unroll=True)

    def scan_fwd(n, state):
        kv = kvf_ref[n]
        kvf_ref[n] = state
        return tab_ref[0, 5] * state + kv

    lax.fori_loop(0, n_chunks, scan_fwd, jnp.zeros((d, d), F32))

    def scan_bwd(m, state):
        n = n_chunks - 1 - m
        kv = kvb_ref[n]
        kvb_ref[n] = state
        return tab_ref[0, 6] * state + kv

    lax.fori_loop(0, n_chunks, scan_bwd, jnp.zeros((d, d), F32))

    def in_chunk(n, a_out):
        qc = qs_ref[chunk(n), :].astype(BF16)
        kc = ks_ref[chunk(n), :].astype(BF16)
        a = lax.dot_general(qc, kc, NT_DIMS, preferred_element_type=F32) * tab_ref[0, 0]
        a_out[...] = a.astype(BF16)

    def mix(n, a_in, y_out):
        qc = qs_ref[chunk(n), :]
        y_out[...] = (jnp.dot(a_in[...], v_ref[0, chunk(n), :], preferred_element_type=F32)
                      + mm(qc * tab_ref[0, 1], kvf_ref[n]) + mm(qc * tab_ref[0, 3], kvb_ref[n]))

    def finish(n, y_in):
        y = y_in[...]
        mu = jnp.mean(y, axis=-1, keepdims=True)
        yc = y - mu
        var = jnp.mean(yc * yc, axis=-1, keepdims=True)
        g = g_ref[0, chunk(n), :].astype(F32)
        o_ref[0, chunk(n), :] = (yc * lax.rsqrt(var + GN_EPS) * (g * jax.nn.sigmoid(g))).astype(BF16)

    a_slots = (a0_ref, a1_ref)
    y_slots = (y0_ref, y1_ref)

    def step(t, parity, do_a=True, do_mix=True, do_finish=True):
        t = jnp.asarray(t, I32)
        if do_a:
            in_chunk(t, a_slots[parity])
        if do_mix:
            mix(t - 1, a_slots[1 - parity], y_slots[1 - parity])
        if do_finish:
            finish(t - 2, y_slots[parity])

    assert n_chunks % 2 == 0 and n_chunks >= 4
    step(0, 0, do_mix=False, do_finish=False)
    step(1, 1, do_finish=False)

    def steady(i, carry):
        step(2 + 2 * i, 0)
        step(3 + 2 * i, 1)
        return carry

    lax.fori_loop(0, (n_chunks - 2) // 2, steady, 0, unroll=True)
    step(n_chunks, 0, do_a=False)
    step(n_chunks + 1, 1, do_a=False, do_mix=False)


def _retention_mixer(proj_q, col_q, proj, col0, batch, seq):
    cos2, sin2, tab = _retention_consts(seq)
    blk = (1, seq, RET_HEAD_DIM)
    h_ = RET_HEADS
    assert col0 % RET_HEAD_DIM == 0 and col_q % RET_HEAD_DIM == 0
    cq = col_q // RET_HEAD_DIM
    c0 = col0 // RET_HEAD_DIM - h_
    return pl.pallas_call(
        _retention_kernel,
        grid=(batch, RET_HEADS),
        in_specs=[
            pl.BlockSpec(blk, lambda b, h: (b, 0, cq + h)),
            pl.BlockSpec(blk, lambda b, h: (b, 0, c0 + h_ + h)),
            pl.BlockSpec(blk, lambda b, h: (b, 0, c0 + 2 * h_ + h)),
            pl.BlockSpec(blk, lambda b, h: (b, 0, c0 + 3 * h_ + h)),
            pl.BlockSpec((seq, RET_HEAD_DIM), lambda b, h: (0, 0)),
            pl.BlockSpec((seq, RET_HEAD_DIM), lambda b, h: (0, 0)),
            pl.BlockSpec((1, 7, RET_CHUNK, RET_CHUNK), lambda b, h: (h, 0, 0, 0)),
        ],
        out_specs=pl.BlockSpec(blk, lambda b, h: (b, 0, h)),
        out_shape=jax.ShapeDtypeStruct((batch, seq, RET_W), BF16),
        scratch_shapes=[pltpu.VMEM((seq, RET_HEAD_DIM), F32), pltpu.VMEM((seq, RET_HEAD_DIM), F32),
                        pltpu.VMEM((seq // RET_CHUNK, RET_HEAD_DIM, RET_HEAD_DIM), F32),
                        pltpu.VMEM((seq // RET_CHUNK, RET_HEAD_DIM, RET_HEAD_DIM), F32),
                        pltpu.VMEM((RET_CHUNK, RET_CHUNK), BF16), pltpu.VMEM((RET_CHUNK, RET_CHUNK), BF16),
                        pltpu.VMEM((RET_CHUNK, RET_HEAD_DIM), F32), pltpu.VMEM((RET_CHUNK, RET_HEAD_DIM), F32)],
        compiler_params=_params("parallel", "arbitrary"),
        name="retention",
    )(proj_q, proj, proj, proj, jnp.asarray(cos2), jnp.asarray(sin2), jnp.asarray(tab))


def _cross_out_router_kernel(q_ref, k_ref, v_ref, w_ref, res_ref, g_ref, wr_ref, x_ref, hm_ref, aff_ref, h_ref):
    n_e = aff_ref.shape[1]
    _cross_prologue(h_ref, q_ref, k_ref, v_ref)
    x = res_ref[...] + jnp.dot(h_ref[...], w_ref[...], preferred_element_type=F32)
    x_ref[...] = x
    h = _rms(x, g_ref[...])
    h_hi = h.astype(BF16)
    hm_ref[...] = h_hi
    h_lo = (h - h_hi.astype(F32)).astype(BF16)
    wr = wr_ref[...]
    w_hi = wr.astype(BF16)
    w_lo = (wr - w_hi.astype(F32)).astype(BF16)
    both = jnp.dot(h_hi, jnp.concatenate([w_hi, w_lo], axis=1), preferred_element_type=F32)
    logits = (both[:, :LANES] + both[:, LANES:] + jnp.dot(h_lo, w_hi, preferred_element_type=F32)).T[:n_e]
    m = jnp.max(logits, axis=0, keepdims=True)
    e = jnp.exp(logits - m)
    aff_ref[0] = e / jnp.sum(e, axis=0, keepdims=True)


def _cross_out_and_router(q, kv, w_co, layer, res, gain, w_router, batch, seq, n_mem, tm):
    m, d = q.shape
    per_b = seq // tm
    assert seq % tm == 0
    w_pad = jnp.zeros((d, LANES), F32).at[:, :N_EXPERTS].set(w_router)
    row_spec = pl.BlockSpec((tm, d), lambda i: (i, 0))
    return pl.pallas_call(
        _cross_out_router_kernel,
        grid=(m // tm,),
        in_specs=[
            row_spec,
            pl.BlockSpec((n_mem, d), lambda i: (i // per_b, 2 * layer)),
            pl.BlockSpec((n_mem, d), lambda i: (i // per_b, 2 * layer + 1)),
            pl.BlockSpec((None, d, d), lambda i: (layer, 0, 0)),
            row_spec,
            pl.BlockSpec((1, d), lambda i: (0, 0)),
            pl.BlockSpec((d, LANES), lambda i: (0, 0)),
        ],
        out_specs=[
            row_spec,
            row_spec,
            pl.BlockSpec((1, N_EXPERTS, tm), lambda i: (i // per_b, 0, i % per_b)),
        ],
        out_shape=[
            jax.ShapeDtypeStruct((m, d), F32),
            jax.ShapeDtypeStruct((m, d), BF16),
            jax.ShapeDtypeStruct((batch, N_EXPERTS, seq), F32),
        ],
        scratch_shapes=[pltpu.VMEM((tm, d), BF16)],
        compiler_params=_params("parallel"),
        name="cross_out_router",
    )(q, kv, kv, w_co, res, gain.reshape(1, d), w_pad)


SUBLANES = 8


def _sort_descending(x):
    n, lanes = x.shape
    k = 2
    while k <= n:
        j = k // 2
        while j >= 1:
            if j >= SUBLANES and k < n:
                v = x.reshape(n // (2 * k), 2, k // (2 * j), 2, j, lanes)
                a, b = v[:, :, :, 0], v[:, :, :, 1]
                mx, mn = jnp.maximum(a, b), jnp.minimum(a, b)
                lo = jnp.concatenate([mx[:, 0:1], mn[:, 1:2]], axis=1)
                hi = jnp.concatenate([mn[:, 0:1], mx[:, 1:2]], axis=1)
                x = jnp.stack([lo, hi], axis=3).reshape(n, lanes)
            elif j >= SUBLANES:
                v = x.reshape(n // (2 * j), 2, j, lanes)
                x = jnp.stack([jnp.maximum(v[:, 0], v[:, 1]), jnp.minimum(v[:, 0], v[:, 1])], axis=1)
                x = x.reshape(n, lanes)
            else:
                v = x.reshape(n // SUBLANES, SUBLANES, lanes)
                s = lax.broadcasted_iota(I32, v.shape, 0) * SUBLANES + lax.broadcasted_iota(I32, v.shape, 1)
                is_lo = (s & j) == 0
                partner = jnp.where(is_lo, pltpu.roll(v, SUBLANES - j, 1), pltpu.roll(v, j, 1))
                take_max = is_lo == ((s & k) == 0)
                x = jnp.where(take_max, jnp.maximum(v, partner), jnp.minimum(v, partner)).reshape(n, lanes)
            j //= 2
        k *= 2
    return x


def _topk_kernel(cap, n_e, aff_ref, slot_ref, wsel_ref, slot_t_ref):
    n_rows, n_tok = aff_ref.shape
    a = aff_ref[...]
    token_major = jnp.concatenate([a, jnp.zeros((LANES - n_rows, n_tok), F32)], axis=0).T
    kth = _sort_descending(token_major)[cap - 1:cap, :]
    thr = jnp.broadcast_to(kth, (SUBLANES, LANES)).T[:n_rows, 0:1]
    gt = a > thr
    eq = a == thr
    need = cap - jnp.sum(gt.astype(I32), axis=1, keepdims=True)
    upper = (lax.broadcasted_iota(I32, (n_tok, n_tok), 0) < lax.broadcasted_iota(I32, (n_tok, n_tok), 1))
    upper = upper.astype(BF16)
    eq_rank = jnp.dot(eq.astype(BF16), upper, preferred_element_type=F32)
    sel = jnp.logical_or(gt, jnp.logical_and(eq, eq_rank < need.astype(F32)))
    rank = jnp.dot(sel.astype(BF16), upper, preferred_element_type=F32)
    slot = jnp.where(sel, rank, -1.0)
    slot_ref[...] = slot.astype(I32)
    wsel_ref[...] = jnp.where(sel, a, 0.0)
    pad = jnp.full((LANES - n_e, n_tok), -1.0, F32)
    for b in range(n_rows // n_e):
        slot_t_ref[b] = jnp.concatenate([slot[b * n_e:(b + 1) * n_e], pad], axis=0).T.astype(I32)


def _topk_select(aff2d, cap, n_e):
    rows, n_tok = aff2d.shape
    spec = pl.BlockSpec((rows, n_tok), lambda i: (0, 0))
    spec_t = pl.BlockSpec((rows // n_e, n_tok, LANES), lambda i: (0, 0, 0))
    return pl.pallas_call(
        functools.partial(_topk_kernel, cap, n_e),
        grid=(1,),
        in_specs=[spec],
        out_specs=[spec, spec, spec_t],
        out_shape=[jax.ShapeDtypeStruct((rows, n_tok), I32), jax.ShapeDtypeStruct((rows, n_tok), F32),
                   jax.ShapeDtypeStruct((rows // n_e, n_tok, LANES), I32)],
        compiler_params=_params("arbitrary"),
        name="moe_topk_select",
    )(aff2d)


def _gather_kernel(cap, slot_ref, wsel_ref, hm_ref, xe_ref, gs_ref):
    n_tok = hm_ref.shape[0]
    group = slot_ref.shape[0]
    slot_of_row = lax.broadcasted_iota(I32, (cap, n_tok), 0)
    hots = [slot_ref[k] == slot_of_row for k in range(group)]
    onehot = jnp.concatenate([h.astype(BF16) for h in hots], axis=0)
    xe = jnp.dot(onehot, hm_ref[...], preferred_element_type=F32).astype(BF16)
    for k in range(group):
        xe_ref[k] = xe[k * cap:(k + 1) * cap]
        gs_ref[k] = jnp.sum(jnp.where(hots[k], wsel_ref[k], 0.0), axis=1, keepdims=True)


def _gather_tokens(slot, wsel, hm, batch, seq, cap, group=4):
    rows = batch * N_EXPERTS
    per_b = N_EXPERTS // group
    row_spec = pl.BlockSpec((group, 1, seq), lambda b, e: (b * per_b + e, 0, 0))
    return pl.pallas_call(
        functools.partial(_gather_kernel, cap),
        grid=(batch, per_b),
        in_specs=[row_spec, row_spec, pl.BlockSpec((seq, D_MODEL), lambda b, e: (b, 0))],
        out_specs=[
            pl.BlockSpec((group, cap, D_MODEL), lambda b, e: (e, b, 0)),
            pl.BlockSpec((group, cap, 1), lambda b, e: (e, b, 0)),
        ],
        out_shape=[
            jax.ShapeDtypeStruct((N_EXPERTS, batch * cap, D_MODEL), BF16),
            jax.ShapeDtypeStruct((N_EXPERTS, batch * cap, 1), F32),
        ],
        compiler_params=_params("parallel", "arbitrary"),
        name="moe_gather",
    )(slot.reshape(rows, 1, seq), wsel.reshape(rows, 1, seq), hm)


def _expert_kernel(n_e, n_t, xe_ref, wg_ref, wu_ref, wd_ref, gs_ref, ye_ref, mid_ref):
    g = pl.program_id(0)
    s = pl.program_id(1)
    t = s // 2
    tf = wg_ref.shape[-1]

    @pl.when(jnp.logical_and(s % 2 == 0, g < n_e))
    def _():
        x = xe_ref[0]
        a = jnp.dot(x, wg_ref[...].astype(BF16), preferred_element_type=F32)
        u = jnp.dot(x, wu_ref[...].astype(BF16), preferred_element_type=F32)
        mid_ref[g % 2, t] = (a * jax.nn.sigmoid(a) * u).astype(BF16)

    @pl.when(jnp.logical_and(s % 2 == 1, g >= 1))
    def _():
        prev = (g - 1) % 2
        acc = jnp.dot(mid_ref[prev, 0], wd_ref[0:tf, :].astype(BF16), preferred_element_type=F32)
        for f in range(1, n_t):
            acc += jnp.dot(mid_ref[prev, f], wd_ref[f * tf:(f + 1) * tf, :].astype(BF16),
                           preferred_element_type=F32)
        ye_ref[0] = (acc * gs_ref[0]).astype(BF16)


def _expert_ffn(xe, gs, w_gate, w_up, w_down, layer, tf):
    n_e, rows, d = xe.shape
    ff = w_gate.shape[-1]
    n_t = ff // tf
    assert d // tf == n_t
    last = n_e - 1

    n_s = 2 * n_t

    def fill_tile(g, s, lead):
        v = jnp.minimum(g * n_s + s + lead, n_e * n_s - 1)
        return (layer, v // n_s, 0, (v % n_s) // 2)

    def emit_tile(g, s):
        v = jnp.maximum(g * n_s + s - 1 - n_s, 0)
        return (layer, v // n_s, 0, (v % n_s) // 2)

    return pl.pallas_call(
        functools.partial(_expert_kernel, n_e, n_t),
        grid=(n_e + 1, n_s),
        in_specs=[
            pl.BlockSpec((1, rows, d), lambda g, s: (jnp.minimum(g, last), 0, 0)),
            pl.BlockSpec((None, None, d, tf), functools.partial(fill_tile, lead=1)),
            pl.BlockSpec((None, None, d, tf), functools.partial(fill_tile, lead=0)),
            pl.BlockSpec((None, None, ff, tf), emit_tile),
            pl.BlockSpec((1, rows, 1), lambda g, s: (jnp.maximum(g - 1, 0), 0, 0)),
        ],
        out_specs=pl.BlockSpec((1, rows, tf), lambda g, s: emit_tile(g, s)[1:]),
        out_shape=jax.ShapeDtypeStruct((n_e, rows, d), BF16),
        scratch_shapes=[pltpu.VMEM((2, n_t, rows, tf), BF16)],
        compiler_params=_params("arbitrary", "arbitrary"),
        name="moe_expert_ffn",
    )(xe, w_gate, w_up, w_down, gs)


def _scatter_kernel(cap, slot_ref, ye_ref, x_ref, *rest):
    o_ref = rest[-1]
    n_e = ye_ref.shape[0]
    col = lax.broadcasted_iota(I32, (x_ref.shape[0], cap), 1)
    slots = slot_ref[0]
    onehot = jnp.concatenate([(slots[:, e:e + 1] == col).astype(BF16) for e in range(n_e)], axis=1)
    ye = ye_ref[...].reshape(n_e * cap, ye_ref.shape[-1])
    y = x_ref[...] + jnp.dot(onehot, ye, preferred_element_type=F32)
    o_ref[...] = _rms(y, rest[0][...]) if len(rest) == 2 else y


def _scatter_add(slot_t, ye, x2d, batch, seq, cap, tr, final_gain=None):
    d = x2d.shape[-1]
    per_b = seq // tr
    in_specs = [
        pl.BlockSpec((1, tr, LANES), lambda b, i: (b, i, 0)),
        pl.BlockSpec((N_EXPERTS, cap, d), lambda b, i: (0, b, 0)),
        pl.BlockSpec((tr, d), lambda b, i: (b * per_b + i, 0)),
    ]
    args = [slot_t, ye, x2d]
    if final_gain is not None:
        in_specs.append(pl.BlockSpec((1, d), lambda b, i: (0, 0)))
        args.append(final_gain.reshape(1, d))
    return pl.pallas_call(
        functools.partial(_scatter_kernel, cap),
        grid=(batch, per_b),
        in_specs=in_specs,
        out_specs=pl.BlockSpec((tr, d), lambda b, i: (b * per_b + i, 0)),
        out_shape=jax.ShapeDtypeStruct(x2d.shape, F32),
        compiler_params=_params("parallel", "arbitrary"),
        name="moe_scatter_add",
    )(*args)


def _expert_choice_ffn(x2d, hm, aff, w_gate, w_up, w_down, layer, batch, seq, final_gain=None):
    cap = EC_CAPACITY * seq // N_EXPERTS
    slot, wsel, slot_t = _topk_select(aff.reshape(batch * N_EXPERTS, seq), cap, N_EXPERTS)
    xe, gs = _gather_tokens(slot, wsel, hm, batch, seq, cap)
    ye = _expert_ffn(xe, gs, w_gate, w_up, w_down, layer, 512)
    return _scatter_add(slot_t, ye, x2d, batch, seq, cap, 256, final_gain)


def kernel(x, mem, norm_mix, w_in, na_rpb, hy_conv_w, hy_conv_b, hy_filt_w1, hy_filt_b1, hy_filt_w2, hy_filt_b2, hy_filt_w3, hy_sin_freq, hy_skip_d, branch_norm, w_out, norm_cross, mem_norm, w_cq, w_ckv, w_co, norm_moe, w_router, w_gate, w_up, w_down, final_norm):
    batch, seq, d = x.shape
    n_mem = mem.shape[1]
    depth = w_in.shape[0]
    m = batch * seq
    na_cols = 3 * NA_W
    hy_cols = 3 * HY_W
    rows = seq // GRID_W

    dft_fwd, dft_inv = _dft_tables(seq)
    x2d = x.reshape(m, d)
    mem2d = mem.reshape(batch * n_mem, d)

    w_out_b, w_cq_b, w_co_b = (w.astype(BF16) for w in (w_out, w_cq, w_co))
    p_in = w_in.shape[-1]
    kv = _norm_proj_layers(mem2d, mem_norm, w_ckv, 1024)

    for l in range(depth):
        assert p_in == 3 * na_cols
        slabs = [_norm_proj_resident(x2d, norm_mix[l], w_in, l, s, 3, 512, "in_proj").reshape(batch, seq, na_cols)
                 for s in range(3)]

        y_na = _neighbourhood_attention(slabs[0], _na_bias_table(na_rpb[l], rows), batch, seq)
        y_hy = _hyena_mixer(slabs[1], 0, hy_conv_w[l], hy_conv_b[l], hy_filt_w1[l],
                            hy_filt_b1[l], hy_filt_w2[l], hy_filt_b2[l], hy_filt_w3[l], hy_sin_freq[l],
                            hy_skip_d[l], dft_fwd, dft_inv, batch, seq)
        y_ret = _retention_mixer(slabs[1], hy_cols, slabs[2], 0, batch, seq)
        x2d, q = _branch_out_cross_q(y_na.reshape(m, NA_W), y_hy.reshape(m, HY_W), y_ret.reshape(m, RET_W),
                                     branch_norm[l], w_out_b, x2d, norm_cross[l], w_cq_b, l, 512)
        x2d, hm, aff = _cross_out_and_router(q, kv, w_co_b, l, x2d, norm_moe[l], w_router[l], batch, seq, n_mem,
                                             256)

        x2d = _expert_choice_ffn(x2d, hm, aff, w_gate, w_up, w_down, l, batch, seq,
                                 final_gain=final_norm if l == depth - 1 else None)

    return x2d.reshape(batch, seq, d)
```
